```python
import math
import jax, jax.numpy as jnp
from jax import lax
import numpy as np

D_MODEL = 1024
BATCH = 8
SEQ = 8192
DEPTH = 1

D_LRU = D_MODEL
LRU_BLOCKS = 16
LRU_BW = D_LRU // LRU_BLOCKS
CONV_W = 4
LRU_C = 8.0
N_HEADS = 8
HEAD_DIM = 128
D_ATTN = N_HEADS * HEAD_DIM
BLOCK_Q = 128
D_FF = 4 * D_MODEL
N_BRANCH = 2
RMS_EPS = 1e-6
IN_SPLITS = (D_LRU, D_LRU, D_ATTN, D_ATTN, D_ATTN, N_BRANCH * D_MODEL, N_HEADS)
D_IN = sum(IN_SPLITS)

kernel_name = "hybrid_rglru_fox_gated_block"


def rms_norm(x, g):
    xf = x.astype(jnp.float32)
    y = xf * lax.rsqrt(jnp.mean(xf * xf, axis=-1, keepdims=True) + RMS_EPS)
    return (y * g.astype(jnp.float32)).astype(x.dtype)


def causal_depthwise_conv(x, w, b):
    S = x.shape[1]
    xp = jnp.pad(x, ((0, 0), (CONV_W - 1, 0), (0, 0)))
    out = b
    for k in range(CONV_W):
        out = out + xp[:, k:k + S, :] * w[k]
    return out


def block_diag_linear(x, w, b):
    B, S, _ = x.shape
    xb = x.reshape(B, S, LRU_BLOCKS, LRU_BW)
    y = jnp.einsum('bsnc,ncd->bsnd', xb, w).reshape(B, S, D_LRU)
    return y + b


def rg_lru(x, wa, ba, wx, bx, lam):
    r = jax.nn.sigmoid(block_diag_linear(x, wa, ba).astype(jnp.float32))
    i = jax.nn.sigmoid(block_diag_linear(x, wx, bx).astype(jnp.float32))
    log_a = -LRU_C * r * jax.nn.softplus(-lam.astype(jnp.float32))
    a = jnp.exp(log_a)
    mult = jnp.sqrt(-jnp.expm1(2.0 * log_a))
    bterm = mult * (i * x.astype(jnp.float32))

    def combine(left, right):
        a1, b1 = left
        a2, b2 = right
        return a1 * a2, a2 * b1 + b2

    _, h = lax.associative_scan(combine, (a, bterm), axis=1)
    return h.astype(x.dtype)


def fox_attention(q, k, v, log_f):
    S = q.shape[1]
    scale = 1.0 / math.sqrt(HEAD_DIM)
    F = jnp.cumsum(log_f.astype(jnp.float32), axis=1)
    F = jnp.transpose(F, (0, 2, 1))
    outs = []
    for blk in range(S // BLOCK_Q):
        q0, q1 = blk * BLOCK_Q, (blk + 1) * BLOCK_Q
        qb = q[:, q0:q1]
        kb = k[:, :q1]
        vb = v[:, :q1]
        s = jnp.einsum('bqhd,bkhd->bhqk', qb, kb).astype(jnp.float32) * scale
        s = s + F[:, :, q0:q1, None] - F[:, :, None, :q1]
        q_pos = jnp.arange(q0, q1)
        k_pos = jnp.arange(q1)
        mask = q_pos[:, None] >= k_pos[None, :]
        s = jnp.where(mask[None, None], s, -jnp.inf)
        p = jax.nn.softmax(s, axis=-1).astype(v.dtype)
        outs.append(jnp.einsum('bhqk,bkhd->bqhd', p, vb))
    return jnp.concatenate(outs, axis=1)


def _fwd_setup_inputs(seed: int = 0) -> dict:
    key = jax.random.key(seed)
    ks = jax.random.split(key, 20)
    f32 = jnp.float32
    n = lambda k, shape, s: jax.random.normal(k, shape, f32) * s
    x = jax.random.normal(ks[0], (BATCH, SEQ, D_MODEL), f32)
    norm_mix_g = 1.0 + n(ks[1], (D_MODEL,), 0.02)
    w_in = n(ks[2], (D_MODEL, D_IN), D_MODEL ** -0.5)
    conv_w = n(ks[3], (CONV_W, D_LRU), CONV_W ** -0.5)
    conv_b = n(ks[4], (D_LRU,), 0.02)
    lru_wa = n(ks[5], (LRU_BLOCKS, LRU_BW, LRU_BW), LRU_BW ** -0.5)
    lru_ba = n(ks[6], (D_LRU,), 0.02)
    lru_wx = n(ks[7], (LRU_BLOCKS, LRU_BW, LRU_BW), LRU_BW ** -0.5)
    lru_bx = n(ks[8], (D_LRU,), 0.02)
    a0 = jax.random.uniform(ks[9], (D_LRU,), f32, 0.9, 0.999)
    s0 = a0 ** (1.0 / LRU_C)
    lru_lambda = jnp.log(s0) - jnp.log1p(-s0)
    forget_b = 2.0 + n(ks[10], (N_HEADS,), 0.5)
    w_branch_a = n(ks[11], (D_LRU, D_MODEL), D_LRU ** -0.5)
    w_branch_b = n(ks[12], (D_ATTN, D_MODEL), D_ATTN ** -0.5)
    w_out = n(ks[13], (D_MODEL, D_MODEL), D_MODEL ** -0.5)
    norm_mlp_g = 1.0 + n(ks[14], (D_MODEL,), 0.02)
    w_up = n(ks[15], (D_MODEL, D_FF), D_MODEL ** -0.5)
    w_down = n(ks[16], (D_FF, D_MODEL), D_FF ** -0.5)
    norm_final_g = 1.0 + n(ks[17], (D_MODEL,), 0.02)
    return {"x": x, "norm_mix_g": norm_mix_g, "w_in": w_in, "conv_w": conv_w,
            "conv_b": conv_b, "lru_wa": lru_wa, "lru_ba": lru_ba, "lru_wx": lru_wx,
            "lru_bx": lru_bx, "lru_lambda": lru_lambda, "forget_b": forget_b,
            "w_branch_a": w_branch_a, "w_branch_b": w_branch_b, "w_out": w_out,
            "norm_mlp_g": norm_mlp_g, "w_up": w_up, "w_down": w_down,
            "norm_final_g": norm_final_g}


def _fwd_reference(x, norm_mix_g, w_in, conv_w, conv_b, lru_wa, lru_ba, lru_wx, lru_bx,
              lru_lambda, forget_b, w_branch_a, w_branch_b, w_out, norm_mlp_g,
              w_up, w_down, norm_final_g):
    B, S, _ = x.shape
    for _layer in range(DEPTH):
        u = rms_norm(x, norm_mix_g)
        proj = u @ w_in
        cuts = list(np.cumsum(IN_SPLITS)[:-1])
        x_lru, g_lru, q, k, v, gates, f_logit = jnp.split(proj, cuts, axis=-1)

        xa = causal_depthwise_conv(x_lru, conv_w, conv_b)
        ha = rg_lru(xa, lru_wa, lru_ba, lru_wx, lru_bx, lru_lambda)
        ya = (jax.nn.gelu(g_lru) * ha) @ w_branch_a

        log_f = jax.nn.log_sigmoid((f_logit + forget_b).astype(jnp.float32))
        qh = q.reshape(B, S, N_HEADS, HEAD_DIM)
        kh = k.reshape(B, S, N_HEADS, HEAD_DIM)
        vh = v.reshape(B, S, N_HEADS, HEAD_DIM)
        ob = fox_attention(qh, kh, vh, log_f).reshape(B, S, D_ATTN)
        yb = ob @ w_branch_b

        g_a, g_b = jnp.split(jax.nn.sigmoid(gates), N_BRANCH, axis=-1)
        x = x + (g_a * ya + g_b * yb) @ w_out

        m = rms_norm(x, norm_mlp_g)
        h = jnp.square(jax.nn.relu(m @ w_up))
        x = x + h @ w_down
    return rms_norm(x, norm_final_g)


import jax as _jax
import jax.numpy as _jnp

TWIN_FORMAT = 'train_step'
FWD_PARAMS = ['x', 'norm_mix_g', 'w_in', 'conv_w', 'conv_b', 'lru_wa', 'lru_ba', 'lru_wx', 'lru_bx', 'lru_lambda', 'forget_b', 'w_branch_a', 'w_branch_b', 'w_out', 'norm_mlp_g', 'w_up', 'w_down', 'norm_final_g']
TWIN_WEIGHTS = ['norm_mix_g', 'w_in', 'conv_w', 'conv_b', 'lru_wa', 'lru_ba', 'lru_wx', 'lru_bx', 'lru_lambda', 'forget_b', 'w_branch_a', 'w_branch_b', 'w_out', 'norm_mlp_g', 'w_up', 'w_down', 'norm_final_g']
TWIN_DIFF_INPUT = 'x'
TWIN_INPUTS = ['x', 'norm_mix_g', 'w_in', 'conv_w', 'conv_b', 'lru_wa', 'lru_ba', 'lru_wx', 'lru_bx', 'lru_lambda', 'forget_b', 'w_branch_a', 'w_branch_b', 'w_out', 'norm_mlp_g', 'w_up', 'w_down', 'norm_final_g', 'loss_target', 'm_norm_mix_g', 'm_w_in', 'm_conv_w', 'm_conv_b', 'm_lru_wa', 'm_lru_ba', 'm_lru_wx', 'm_lru_bx', 'm_lru_lambda', 'm_forget_b', 'm_w_branch_a', 'm_w_branch_b', 'm_w_out', 'm_norm_mlp_g', 'm_w_up', 'm_w_down', 'm_norm_final_g', 'v_norm_mix_g', 'v_w_in', 'v_conv_w', 'v_conv_b', 'v_lru_wa', 'v_lru_ba', 'v_lru_wx', 'v_lru_bx', 'v_lru_lambda', 'v_forget_b', 'v_w_branch_a', 'v_w_branch_b', 'v_w_out', 'v_norm_mlp_g', 'v_w_up', 'v_w_down', 'v_norm_final_g']
TWIN_OUTPUTS = ['loss', 'grad_x', 'grad_norm_mix_g', 'grad_w_in', 'grad_conv_w', 'grad_conv_b', 'grad_lru_wa', 'grad_lru_ba', 'grad_lru_wx', 'grad_lru_bx', 'grad_lru_lambda', 'grad_forget_b', 'grad_w_branch_a', 'grad_w_branch_b', 'grad_w_out', 'grad_norm_mlp_g', 'grad_w_up', 'grad_w_down', 'grad_norm_final_g', 'delta_norm_mix_g', 'delta_w_in', 'delta_conv_w', 'delta_conv_b', 'delta_lru_wa', 'delta_lru_ba', 'delta_lru_wx', 'delta_lru_bx', 'delta_lru_lambda', 'delta_forget_b', 'delta_w_branch_a', 'delta_w_branch_b', 'delta_w_out', 'delta_norm_mlp_g', 'delta_w_up', 'delta_w_down', 'delta_norm_final_g', 'new_m_norm_mix_g', 'new_m_w_in', 'new_m_conv_w', 'new_m_conv_b', 'new_m_lru_wa', 'new_m_lru_ba', 'new_m_lru_wx', 'new_m_lru_bx', 'new_m_lru_lambda', 'new_m_forget_b', 'new_m_w_branch_a', 'new_m_w_branch_b', 'new_m_w_out', 'new_m_norm_mlp_g', 'new_m_w_up', 'new_m_w_down', 'new_m_norm_final_g', 'new_v_norm_mix_g', 'new_v_w_in', 'new_v_conv_w', 'new_v_conv_b', 'new_v_lru_wa', 'new_v_lru_ba', 'new_v_lru_wx', 'new_v_lru_bx', 'new_v_lru_lambda', 'new_v_forget_b', 'new_v_w_branch_a', 'new_v_w_branch_b', 'new_v_w_out', 'new_v_norm_mlp_g', 'new_v_w_up', 'new_v_w_down', 'new_v_norm_final_g']
TWIN_LEAF_KINDS = {'loss': 'loss', 'grad_x': 'grad_x', 'grad_norm_mix_g': 'grad_w', 'grad_w_in': 'grad_w', 'grad_conv_w': 'grad_w', 'grad_conv_b': 'grad_w', 'grad_lru_wa': 'grad_w', 'grad_lru_ba': 'grad_w', 'grad_lru_wx': 'grad_w', 'grad_lru_bx': 'grad_w', 'grad_lru_lambda': 'grad_w', 'grad_forget_b': 'grad_w', 'grad_w_branch_a': 'grad_w', 'grad_w_branch_b': 'grad_w', 'grad_w_out': 'grad_w', 'grad_norm_mlp_g': 'grad_w', 'grad_w_up': 'grad_w', 'grad_w_down': 'grad_w', 'grad_norm_final_g': 'grad_w', 'delta_norm_mix_g': 'delta_w', 'delta_w_in': 'delta_w', 'delta_conv_w': 'delta_w', 'delta_conv_b': 'delta_w', 'delta_lru_wa': 'delta_w', 'delta_lru_ba': 'delta_w', 'delta_lru_wx': 'delta_w', 'delta_lru_bx': 'delta_w', 'delta_lru_lambda': 'delta_w', 'delta_forget_b': 'delta_w', 'delta_w_branch_a': 'delta_w', 'delta_w_branch_b': 'delta_w', 'delta_w_out': 'delta_w', 'delta_norm_mlp_g': 'delta_w', 'delta_w_up': 'delta_w', 'delta_w_down': 'delta_w', 'delta_norm_final_g': 'delta_w', 'new_m_norm_mix_g': 'new_m', 'new_m_w_in': 'new_m', 'new_m_conv_w': 'new_m', 'new_m_conv_b': 'new_m', 'new_m_lru_wa': 'new_m', 'new_m_lru_ba': 'new_m', 'new_m_lru_wx': 'new_m', 'new_m_lru_bx': 'new_m', 'new_m_lru_lambda': 'new_m', 'new_m_forget_b': 'new_m', 'new_m_w_branch_a': 'new_m', 'new_m_w_branch_b': 'new_m', 'new_m_w_out': 'new_m', 'new_m_norm_mlp_g': 'new_m', 'new_m_w_up': 'new_m', 'new_m_w_down': 'new_m', 'new_m_norm_final_g': 'new_m', 'new_v_norm_mix_g': 'new_v', 'new_v_w_in': 'new_v', 'new_v_conv_w': 'new_v', 'new_v_conv_b': 'new_v', 'new_v_lru_wa': 'new_v', 'new_v_lru_ba': 'new_v', 'new_v_lru_wx': 'new_v', 'new_v_lru_bx': 'new_v', 'new_v_lru_lambda': 'new_v', 'new_v_forget_b': 'new_v', 'new_v_w_branch_a': 'new_v', 'new_v_w_branch_b': 'new_v', 'new_v_w_out': 'new_v', 'new_v_norm_mlp_g': 'new_v', 'new_v_w_up': 'new_v', 'new_v_w_down': 'new_v', 'new_v_norm_final_g': 'new_v'}


def _forward(args):
    return _fwd_reference(*[args[k] for k in FWD_PARAMS])


def _output_shape():
    def fwd():
        inp = _fwd_setup_inputs(0)
        return _fwd_reference(*[inp[k] for k in FWD_PARAMS])
    out = _jax.eval_shape(fwd)
    return out.shape, out.dtype

N_MICROBATCH = 1
ADAM_LR = 0.001
ADAM_B1 = 0.9
ADAM_B2 = 0.999
ADAM_EPS = 1e-08
ADAM_WD = 0.01
ADAM_STEP = 10
PER_EXAMPLE_BATCH_AXIS = {'x': 0, 'loss_target': 0}
SHARED_INPUTS = []
_WEIGHT_DTYPES = {'norm_mix_g': _jnp.float32, 'w_in': _jnp.float32, 'conv_w': _jnp.float32, 'conv_b': _jnp.float32, 'lru_wa': _jnp.float32, 'lru_ba': _jnp.float32, 'lru_wx': _jnp.float32, 'lru_bx': _jnp.float32, 'lru_lambda': _jnp.float32, 'forget_b': _jnp.float32, 'w_branch_a': _jnp.float32, 'w_branch_b': _jnp.float32, 'w_out': _jnp.float32, 'norm_mlp_g': _jnp.float32, 'w_up': _jnp.float32, 'w_down': _jnp.float32, 'norm_final_g': _jnp.float32}
MOMENT_SCALE = {'norm_mix_g': 1.196066e-01, 'w_in': 4.798710e-02, 'conv_w': 6.935191e-02, 'conv_b': 8.101299e-01, 'lru_wa': 2.668354e-02, 'lru_ba': 2.205230e-02, 'lru_wx': 4.846606e-02, 'lru_bx': 2.609215e-02, 'lru_lambda': 3.938881e-02, 'forget_b': 3.109954e-01, 'w_branch_a': 7.357557e-02, 'w_branch_b': 5.891342e-02, 'w_out': 8.903578e-02, 'norm_mlp_g': 2.203246e-01, 'w_up': 1.103598e-01, 'w_down': 2.358218e-01, 'norm_final_g': 6.451904e+01}


def _to_microbatches(a, axis):
    t = _jnp.moveaxis(a, axis, 0)
    t = t.reshape((N_MICROBATCH, t.shape[0] // N_MICROBATCH) + t.shape[1:])
    return _jnp.moveaxis(t, 1, axis + 1)


def setup_inputs(seed: int = 0) -> dict:
    inp = _fwd_setup_inputs(seed)
    key = _jax.random.fold_in(_jax.random.key(seed), 7919)
    shape, _ = _output_shape()
    out = dict(inp)
    out["loss_target"] = _jax.random.normal(_jax.random.fold_in(key, 0), shape, _jnp.float32)
    for i, name in enumerate(TWIN_WEIGHTS):
        w = inp[name].astype(_jnp.float32)
        if MOMENT_SCALE is None:
            s = _jnp.sqrt(_jnp.mean(_jnp.square(w)) + 1e-30)
        else:
            s = MOMENT_SCALE[name]
        km, kv = _jax.random.split(_jax.random.fold_in(key, i + 1))
        out[name] = w
        out["m_" + name] = s * _jax.random.normal(km, w.shape, _jnp.float32)
        out["v_" + name] = (s * s) * _jax.random.uniform(kv, w.shape, _jnp.float32, 0.5, 1.5)
    if N_MICROBATCH > 1:
        for name, axis in PER_EXAMPLE_BATCH_AXIS.items():
            out[name] = _to_microbatches(out[name], axis)
    return {'x': out['x'], 'norm_mix_g': out['norm_mix_g'], 'w_in': out['w_in'], 'conv_w': out['conv_w'], 'conv_b': out['conv_b'], 'lru_wa': out['lru_wa'], 'lru_ba': out['lru_ba'], 'lru_wx': out['lru_wx'], 'lru_bx': out['lru_bx'], 'lru_lambda': out['lru_lambda'], 'forget_b': out['forget_b'], 'w_branch_a': out['w_branch_a'], 'w_branch_b': out['w_branch_b'], 'w_out': out['w_out'], 'norm_mlp_g': out['norm_mlp_g'], 'w_up': out['w_up'], 'w_down': out['w_down'], 'norm_final_g': out['norm_final_g'], 'loss_target': out['loss_target'], 'm_norm_mix_g': out['m_norm_mix_g'], 'm_w_in': out['m_w_in'], 'm_conv_w': out['m_conv_w'], 'm_conv_b': out['m_conv_b'], 'm_lru_wa': out['m_lru_wa'], 'm_lru_ba': out['m_lru_ba'], 'm_lru_wx': out['m_lru_wx'], 'm_lru_bx': out['m_lru_bx'], 'm_lru_lambda': out['m_lru_lambda'], 'm_forget_b': out['m_forget_b'], 'm_w_branch_a': out['m_w_branch_a'], 'm_w_branch_b': out['m_w_branch_b'], 'm_w_out': out['m_w_out'], 'm_norm_mlp_g': out['m_norm_mlp_g'], 'm_w_up': out['m_w_up'], 'm_w_down': out['m_w_down'], 'm_norm_final_g': out['m_norm_final_g'], 'v_norm_mix_g': out['v_norm_mix_g'], 'v_w_in': out['v_w_in'], 'v_conv_w': out['v_conv_w'], 'v_conv_b': out['v_conv_b'], 'v_lru_wa': out['v_lru_wa'], 'v_lru_ba': out['v_lru_ba'], 'v_lru_wx': out['v_lru_wx'], 'v_lru_bx': out['v_lru_bx'], 'v_lru_lambda': out['v_lru_lambda'], 'v_forget_b': out['v_forget_b'], 'v_w_branch_a': out['v_w_branch_a'], 'v_w_branch_b': out['v_w_branch_b'], 'v_w_out': out['v_w_out'], 'v_norm_mlp_g': out['v_norm_mlp_g'], 'v_w_up': out['v_w_up'], 'v_w_down': out['v_w_down'], 'v_norm_final_g': out['v_norm_final_g']}


def _loss(weights, diff, rest, loss_target):
    with _jax.named_scope("forward"):
        args = {**rest, TWIN_DIFF_INPUT: diff, **{k: w.astype(_WEIGHT_DTYPES[k]) for k, w in weights.items()}}
        y = _forward(args)
    with _jax.named_scope("loss_head"):
        err = _jnp.square(y.astype(_jnp.float32) - loss_target)
        return 0.5 * _jnp.sum(_jnp.mean(err, axis=-1)) if err.ndim else 0.5 * err


def _adamw(w, g, m, v):
    m = ADAM_B1 * m + (1.0 - ADAM_B1) * g
    v = ADAM_B2 * v + (1.0 - ADAM_B2) * _jnp.square(g)
    m_hat = m / (1.0 - ADAM_B1 ** ADAM_STEP)
    v_hat = v / (1.0 - ADAM_B2 ** ADAM_STEP)
    delta = -ADAM_LR * (m_hat / (_jnp.sqrt(v_hat) + ADAM_EPS) + ADAM_WD * w)
    return delta, m, v


def reference(x, norm_mix_g, w_in, conv_w, conv_b, lru_wa, lru_ba, lru_wx, lru_bx, lru_lambda, forget_b, w_branch_a, w_branch_b, w_out, norm_mlp_g, w_up, w_down, norm_final_g, loss_target, m_norm_mix_g, m_w_in, m_conv_w, m_conv_b, m_lru_wa, m_lru_ba, m_lru_wx, m_lru_bx, m_lru_lambda, m_forget_b, m_w_branch_a, m_w_branch_b, m_w_out, m_norm_mlp_g, m_w_up, m_w_down, m_norm_final_g, v_norm_mix_g, v_w_in, v_conv_w, v_conv_b, v_lru_wa, v_lru_ba, v_lru_wx, v_lru_bx, v_lru_lambda, v_forget_b, v_w_branch_a, v_w_branch_b, v_w_out, v_norm_mlp_g, v_w_up, v_w_down, v_norm_final_g):
    given = dict(x=x, norm_mix_g=norm_mix_g, w_in=w_in, conv_w=conv_w, conv_b=conv_b, lru_wa=lru_wa, lru_ba=lru_ba, lru_wx=lru_wx, lru_bx=lru_bx, lru_lambda=lru_lambda, forget_b=forget_b, w_branch_a=w_branch_a, w_branch_b=w_branch_b, w_out=w_out, norm_mlp_g=norm_mlp_g, w_up=w_up, w_down=w_down, norm_final_g=norm_final_g, loss_target=loss_target, m_norm_mix_g=m_norm_mix_g, m_w_in=m_w_in, m_conv_w=m_conv_w, m_conv_b=m_conv_b, m_lru_wa=m_lru_wa, m_lru_ba=m_lru_ba, m_lru_wx=m_lru_wx, m_lru_bx=m_lru_bx, m_lru_lambda=m_lru_lambda, m_forget_b=m_forget_b, m_w_branch_a=m_w_branch_a, m_w_branch_b=m_w_branch_b, m_w_out=m_w_out, m_norm_mlp_g=m_norm_mlp_g, m_w_up=m_w_up, m_w_down=m_w_down, m_norm_final_g=m_norm_final_g, v_norm_mix_g=v_norm_mix_g, v_w_in=v_w_in, v_conv_w=v_conv_w, v_conv_b=v_conv_b, v_lru_wa=v_lru_wa, v_lru_ba=v_lru_ba, v_lru_wx=v_lru_wx, v_lru_bx=v_lru_bx, v_lru_lambda=v_lru_lambda, v_forget_b=v_forget_b, v_w_branch_a=v_w_branch_a, v_w_branch_b=v_w_branch_b, v_w_out=v_w_out, v_norm_mlp_g=v_norm_mlp_g, v_w_up=v_w_up, v_w_down=v_w_down, v_norm_final_g=v_norm_final_g)
    weights = {n: given[n] for n in TWIN_WEIGHTS}
    shared = {n: given[n] for n in SHARED_INPUTS}
    per_example = {n: given[n] for n in ['x']}
    grad_fn = _jax.value_and_grad(_loss, argnums=(0, 1))

    def one_microbatch(ex, loss_target):
        ex = dict(ex)
        diff = ex.pop(TWIN_DIFF_INPUT)
        return grad_fn(weights, diff, {**shared, **ex}, loss_target)

    if N_MICROBATCH == 1:
        loss, (grad_w, grad_x) = one_microbatch(per_example, given["loss_target"])
    else:
        def body(carry, xs):
            loss_sum, grad_sum = carry
            l_k, (gw_k, gx_k) = one_microbatch(xs[0], xs[1])
            with _jax.named_scope("update"):
                return (loss_sum + l_k, _jax.tree.map(_jnp.add, grad_sum, gw_k)), gx_k

        init = (_jnp.zeros((), _jnp.float32), _jax.tree.map(_jnp.zeros_like, weights))
        (loss, grad_w), grad_x = _jax.lax.scan(body, init, (per_example, given["loss_target"]))
    with _jax.named_scope("update"):
        delta_w, new_m, new_v = {}, {}, {}
        for n in TWIN_WEIGHTS:
            delta_w[n], new_m[n], new_v[n] = _adamw(weights[n], grad_w[n], given["m_" + n], given["v_" + n])
    return (loss, grad_x, *[grad_w[n] for n in TWIN_WEIGHTS], *[delta_w[n] for n in TWIN_WEIGHTS],
            *[new_m[n] for n in TWIN_WEIGHTS], *[new_v[n] for n in TWIN_WEIGHTS])
```

```python
import functools
import math

import jax
import jax.numpy as jnp
from jax import lax
from jax.experimental import pallas as pl
from jax.experimental.pallas import tpu as pltpu

F32 = jnp.float32
BF16 = jnp.bfloat16

D_MODEL = 1024
N_HEADS = 8
HEAD_DIM = 128
D_FF = 4096
LRU_BLOCKS = 16
LRU_BW = 64
LRU_C = 8.0
CONV_W = 4
RMS_EPS = 1e-6
N_DEV = 8
LANES = 128
SUBLANES = 8
N_GROUPS = D_MODEL // LANES
VMEM_LIMIT_BYTES = 52 * 1024 * 1024
ATTN_SCALE = 1.0 / math.sqrt(HEAD_DIM)
NEG_BIG = -1e30
ADAM_LR = 0.001
ADAM_B1 = 0.9
ADAM_B2 = 0.999
ADAM_EPS = 1e-08
ADAM_WD = 0.01
ADAM_STEP = 10
ATTN_BLOCK = 512
LRU_CHUNK = 256
ROW_TILE = 512
MESH_AXES = ("x", "y", "c")
MESH_ID = pl.DeviceIdType.MESH
ANY = pl.BlockSpec(memory_space=pl.ANY)

NT_DIMS = (((1,), (1,)), ((), ()))
TN_DIMS = (((0,), (0,)), ((), ()))
NN_DIMS = (((1,), (0,)), ((), ()))


def _cparams(*sem):
    return pltpu.CompilerParams(dimension_semantics=sem if sem else None, vmem_limit_bytes=VMEM_LIMIT_BYTES)


def _sigmoid(x):
    return 1.0 / (1.0 + jnp.exp(-x))


def _log1p_pos(e):
    u = 1.0 + e
    return jnp.where(u == 1.0, e, jnp.log(u) * (e / (u - 1.0)))


def _softplus(z):
    return jnp.maximum(z, 0.0) + _log1p_pos(jnp.exp(-jnp.abs(z)))


def _expm1_neg(x):
    series = x * (1.0 + x * 0.5 * (1.0 + x * (1.0 / 3.0) * (1.0 + x * 0.25)))
    return jnp.where(x > -0.03, series, jnp.exp(x) - 1.0)


GELU_C = math.sqrt(2.0 / math.pi)
GELU_K = 0.044715


def _gelu(x):
    return 0.5 * x * (1.0 + jnp.tanh(GELU_C * (x + GELU_K * (x * x * x))))


def _gelu_and_grad(x):
    t = jnp.tanh(GELU_C * (x + GELU_K * (x * x * x)))
    g = 0.5 * x * (1.0 + t)
    dg = 0.5 * (1.0 + t) + 0.5 * x * (1.0 - t * t) * (GELU_C * (1.0 + 3.0 * GELU_K * (x * x)))
    return g, dg


def _mm(pairs, *, ta=False, tb=False, tm, tn, tks, outs, name, epi=None, extra=()):
    a0, b0 = pairs[0]
    M = a0.shape[1] if ta else a0.shape[0]
    N = b0.shape[0] if tb else b0.shape[1]
    tm, tn = min(tm, M), min(tn, N)
    nks, offs = [], []
    for (a, b), tk in zip(pairs, tks):
        K = a.shape[0] if ta else a.shape[1]
        assert K % tk == 0 and M % tm == 0 and N % tn == 0
        offs.append(sum(nks))
        nks.append(K // tk)
    nk_total = sum(nks)
    n_pairs, n_extra, n_out = len(pairs), len(extra), len(outs)
    dims = (((0 if ta else 1,), (1 if tb else 0,)), ((), ()))

    def kmap(off, nk):
        return lambda k: jnp.clip(k - off, 0, nk - 1)

    in_specs, operands = [], []
    for (a, b), tk, off, nk in zip(pairs, tks, offs, nks):
        km = kmap(off, nk)
        if ta:
            in_specs.append(pl.BlockSpec((tk, tm), lambda i, j, k, km=km: (km(k), i)))
        else:
            in_specs.append(pl.BlockSpec((tm, tk), lambda i, j, k, km=km: (i, km(k))))
        if tb:
            in_specs.append(pl.BlockSpec((tn, tk), lambda i, j, k, km=km: (j, km(k))))
        else:
            in_specs.append(pl.BlockSpec((tk, tn), lambda i, j, k, km=km: (km(k), j)))
        operands += [a, b]
    for e in extra:
        in_specs.append(pl.BlockSpec((tm, tn), lambda i, j, k: (i, j)))
        operands.append(e)

    def body(*refs):
        ab = refs[:2 * n_pairs]
        ex = refs[2 * n_pairs:2 * n_pairs + n_extra]
        o = refs[2 * n_pairs + n_extra:2 * n_pairs + n_extra + n_out]
        k = pl.program_id(2)

        def finish(acc):
            res = epi(acc, *[e[...] for e in ex]) if epi is not None else (acc,)
            for r, oref in zip(res, o):
                oref[...] = r.astype(oref.dtype)

        if nk_total == 1:
            finish(lax.dot_general(ab[0][...], ab[1][...], dims, preferred_element_type=F32))
            return
        acc = refs[-1]
        for p in range(n_pairs):
            a_ref, b_ref = ab[2 * p], ab[2 * p + 1]

            @pl.when((k >= offs[p]) & (k < offs[p] + nks[p]))
            def _(a_ref=a_ref, b_ref=b_ref):
                prod = lax.dot_general(a_ref[...], b_ref[...], dims, preferred_element_type=F32)

                @pl.when(k == 0)
                def _():
                    acc[...] = prod

                @pl.when(k > 0)
                def _():
                    acc[...] += prod

        @pl.when(k == nk_total - 1)
        def _():
            finish(acc[...])

    return pl.pallas_call(
        body,
        name=name,
        grid=(M // tm, N // tn, nk_total),
        in_specs=in_specs,
        out_specs=[pl.BlockSpec((tm, tn), lambda i, j, k: (i, j)) for _ in outs],
        out_shape=[jax.ShapeDtypeStruct((M, N), dt) for dt in outs],
        scratch_shapes=[] if nk_total == 1 else [pltpu.VMEM((tm, tn), F32)],
        compiler_params=_cparams("parallel", "parallel", "arbitrary"),
    )(*operands)


def _norm_fwd(x, g, name):
    S, D = x.shape
    tr = min(ROW_TILE, S)

    def body(x_ref, g_ref, o_ref):
        xv = x_ref[...]
        r = lax.rsqrt(jnp.mean(xv * xv, axis=-1, keepdims=True) + RMS_EPS)
        o_ref[...] = ((xv * r) * g_ref[...]).astype(o_ref.dtype)

    return pl.pallas_call(
        body, name=name, grid=(S // tr,),
        in_specs=[pl.BlockSpec((tr, D), lambda i: (i, 0)), pl.BlockSpec((1, D), lambda i: (0, 0))],
        out_specs=pl.BlockSpec((tr, D), lambda i: (i, 0)),
        out_shape=jax.ShapeDtypeStruct((S, D), BF16),
        compiler_params=_cparams("parallel"),
    )(x, g.reshape(1, D))


def _rms_bwd_rows(dy, xv, g):
    r = lax.rsqrt(jnp.mean(xv * xv, axis=-1, keepdims=True) + RMS_EPS)
    xn = xv * r
    dxn = dy * g
    dx = r * (dxn - xn * jnp.mean(dxn * xn, axis=-1, keepdims=True))
    dg = jnp.sum(dy * xn, axis=0, keepdims=True)
    return dx, dg


def _norm_bwd(dy, x, g, dres, name):
    S, D = x.shape
    tr = min(ROW_TILE, S)

    def body(dy_ref, x_ref, g_ref, dres_ref, dx_ref, dxb_ref, dg_ref):
        dx, dg = _rms_bwd_rows(dy_ref[...], x_ref[...], g_ref[...])
        dx = dres_ref[...] + dx
        dx_ref[...] = dx
        dxb_ref[...] = dx.astype(BF16)

        @pl.when(pl.program_id(0) == 0)
        def _():
            dg_ref[...] = jnp.zeros_like(dg_ref)

        dg_ref[...] += dg

    row = pl.BlockSpec((tr, D), lambda i: (i, 0))
    vec = pl.BlockSpec((1, D), lambda i: (0, 0))
    return pl.pallas_call(
        body, name=name, grid=(S // tr,),
        in_specs=[row, row, vec, row],
        out_specs=[row, row, vec],
        out_shape=[jax.ShapeDtypeStruct((S, D), F32), jax.ShapeDtypeStruct((S, D), BF16),
                   jax.ShapeDtypeStruct((1, D), F32)],
        compiler_params=_cparams("arbitrary"),
    )(dy, x, g.reshape(1, D), dres)


def _final_norm_loss(x2, target, g):
    S, D = x2.shape
    tr = min(ROW_TILE, S)

    def body(x_ref, t_ref, g_ref, loss_ref, dg_ref, dx_ref, dxb_ref):
        xv = x_ref[...]
        gv = g_ref[...]
        r = lax.rsqrt(jnp.mean(xv * xv, axis=-1, keepdims=True) + RMS_EPS)
        y = (xv * r) * gv
        err = y - t_ref[...]
        part = 0.5 * jnp.sum(jnp.mean(err * err, axis=-1, keepdims=True), axis=0, keepdims=True)
        dy = err * (1.0 / D)
        dx, dg = _rms_bwd_rows(dy, xv, gv)
        dx_ref[...] = dx
        dxb_ref[...] = dx.astype(BF16)

        @pl.when(pl.program_id(0) == 0)
        def _():
            dg_ref[...] = jnp.zeros_like(dg_ref)
            loss_ref[...] = jnp.zeros_like(loss_ref)

        dg_ref[...] += dg
        loss_ref[...] += jnp.broadcast_to(part, loss_ref.shape)

    row = pl.BlockSpec((tr, D), lambda i: (i, 0))
    vec = pl.BlockSpec((1, D), lambda i: (0, 0))
    return pl.pallas_call(
        body, name="final_norm_loss", grid=(S // tr,),
        in_specs=[row, row, vec],
        out_specs=[pl.BlockSpec((SUBLANES, LANES), lambda i: (0, 0)), vec, row, row],
        out_shape=[jax.ShapeDtypeStruct((SUBLANES, LANES), F32), jax.ShapeDtypeStruct((1, D), F32),
                   jax.ShapeDtypeStruct((S, D), F32), jax.ShapeDtypeStruct((S, D), BF16)],
        compiler_params=_cparams("arbitrary"),
    )(x2, target, g.reshape(1, D))


def _lru_gates(xa, bd_j, ba_j, bx_j, sp_j):
    z = jnp.dot(xa.astype(BF16), bd_j, preferred_element_type=F32)
    r = _sigmoid(z[:, :LANES] + ba_j)
    ig = _sigmoid(z[:, LANES:] + bx_j)
    log_a = (-LRU_C) * r * sp_j
    a = jnp.exp(log_a)
    mult = jnp.sqrt(-_expm1_neg(2.0 * log_a))
    return r, ig, a, mult


def _conv_rows(xpad, cw_ref, cb_ref, sl, tc):
    out = jnp.broadcast_to(cb_ref[:, sl], (tc, LANES))
    for k in range(CONV_W):
        out = out + xpad[pl.ds(SUBLANES - (CONV_W - 1) + k, tc), sl] * cw_ref[k:k + 1, sl]
    return out


def _lru_fwd(xg, cw, cb, bd, ba, bx, lam):
    S = xg.shape[0]
    D = D_MODEL
    tc = min(LRU_CHUNK, S)
    hb = tc // SUBLANES

    def body(xl_ref, halo_ref, g_ref, cw_ref, cb_ref, bd_ref, ba_ref, bx_ref, lam_ref,
             h_ref, y_ref, xpad, a_s, b_s, carry):
        i = pl.program_id(0)

        @pl.when(i == 0)
        def _():
            carry[...] = jnp.zeros_like(carry)

        xpad[0:SUBLANES, :] = jnp.where(i > 0, halo_ref[...], 0.0)
        xpad[SUBLANES:, :] = xl_ref[...]
        for j in range(N_GROUPS):
            sl = slice(LANES * j, LANES * (j + 1))
            xa = _conv_rows(xpad, cw_ref, cb_ref, sl, tc)
            sp = _softplus(-lam_ref[:, sl])
            _, ig, a, mult = _lru_gates(xa, bd_ref[j], ba_ref[:, sl], bx_ref[:, sl], sp)
            a_s[:, sl] = a
            b_s[:, sl] = mult * (ig * xa)

        row = lax.broadcasted_iota(jnp.int32, (SUBLANES, D), 0)

        def step(t, c):
            o = pl.multiple_of(t * SUBLANES, SUBLANES)
            A = a_s[pl.ds(o, SUBLANES), :]
            B = b_s[pl.ds(o, SUBLANES), :]
            for d in (1, 2, 4):
                keep = row >= d
                a_sh = jnp.where(keep, pltpu.roll(A, d, 0), 1.0)
                b_sh = jnp.where(keep, pltpu.roll(B, d, 0), 0.0)
                B = A * b_sh + B
                A = A * a_sh
            hh = A * c + B
            h_ref[pl.ds(o, SUBLANES), :] = hh
            return jnp.broadcast_to(hh[SUBLANES - 1:SUBLANES, :], (SUBLANES, D))

        carry[...] = lax.fori_loop(0, hb, step, carry[...])
        y_ref[...] = (_gelu(g_ref[...]) * h_ref[...]).astype(BF16)

    row_spec = lambda col: pl.BlockSpec((tc, D), lambda i, col=col: (i, col))
    halo = pl.BlockSpec((SUBLANES, D), lambda i: (jnp.maximum(i * hb - 1, 0), 0))
    full = lambda shape: pl.BlockSpec(shape, lambda i: tuple(0 for _ in shape))
    return pl.pallas_call(
        body, name="lru_fwd", grid=(S // tc,),
        in_specs=[row_spec(0), halo, row_spec(1), full((CONV_W, D)), full((1, D)),
                  full((N_GROUPS, LANES, 2 * LANES)), full((1, D)), full((1, D)), full((1, D))],
        out_specs=[pl.BlockSpec((tc, D), lambda i: (i, 0)), pl.BlockSpec((tc, D), lambda i: (i, 0))],
        out_shape=[jax.ShapeDtypeStruct((S, D), F32), jax.ShapeDtypeStruct((S, D), BF16)],
        scratch_shapes=[pltpu.VMEM((tc + SUBLANES, D), F32), pltpu.VMEM((tc, D), F32),
                        pltpu.VMEM((tc, D), F32), pltpu.VMEM((SUBLANES, D), F32)],
        compiler_params=_cparams("arbitrary"),
    )(xg, xg, xg, cw, cb, bd, ba, bx, lam)


def _lru_bwd(xg, h, dyain, cw, cb, bd, ba, bx, lam):
    S = xg.shape[0]
    D = D_MODEL
    tc = min(LRU_CHUNK, S)
    hb = tc // SUBLANES
    nc = S // tc

    def body(xl_ref, xhalo_ref, g_ref, h_ref, hhalo_ref, dy_ref, cw_ref, cb_ref, bd_ref, ba_ref, bx_ref,
             lam_ref, dxg_ref, dcw_ref, dcb_ref, dba_ref, dbx_ref, dlam_ref, dbd_ref,
             xpad, hpad, a_s, b_s, dh_s, g_s, xa_s, r_s, ig_s, m_s, dxa_pad, carry_e, dxa_head):
        i = pl.program_id(0)
        c = nc - 1 - i

        @pl.when(i == 0)
        def _():
            carry_e[...] = jnp.zeros_like(carry_e)
            dxa_head[...] = jnp.zeros_like(dxa_head)
            for ref in (dcw_ref, dcb_ref, dba_ref, dbx_ref, dlam_ref, dbd_ref):
                ref[...] = jnp.zeros_like(ref)

        xpad[0:SUBLANES, :] = jnp.where(c > 0, xhalo_ref[...], 0.0)
        xpad[SUBLANES:, :] = xl_ref[...]
        hpad[0:SUBLANES, :] = jnp.where(c > 0, hhalo_ref[...], 0.0)
        hpad[SUBLANES:, :] = h_ref[...]

        for j in range(N_GROUPS):
            sl = slice(LANES * j, LANES * (j + 1))
            xa = _conv_rows(xpad, cw_ref, cb_ref, sl, tc)
            sp = _softplus(-lam_ref[:, sl])
            r, ig, a, mult = _lru_gates(xa, bd_ref[j], ba_ref[:, sl], bx_ref[:, sl], sp)
            gl, dgl = _gelu_and_grad(g_ref[:, sl])
            dy = dy_ref[:, sl]
            dh = dy * gl
            dxg_ref[:, D + LANES * j:D + LANES * (j + 1)] = (dy * h_ref[:, sl] * dgl).astype(BF16)
            a_s[:, sl] = a
            b_s[:, sl] = a * dh
            dh_s[:, sl] = dh
            xa_s[:, sl] = xa
            r_s[:, sl] = r
            ig_s[:, sl] = ig
            m_s[:, sl] = mult

        row = lax.broadcasted_iota(jnp.int32, (SUBLANES, D), 0)

        def step(tt, ce):
            o = pl.multiple_of((hb - 1 - tt) * SUBLANES, SUBLANES)
            A = a_s[pl.ds(o, SUBLANES), :]
            B = b_s[pl.ds(o, SUBLANES), :]
            for d in (1, 2, 4):
                keep = row < SUBLANES - d
                a_sh = jnp.where(keep, pltpu.roll(A, SUBLANES - d, 0), 1.0)
                b_sh = jnp.where(keep, pltpu.roll(B, SUBLANES - d, 0), 0.0)
                B = A * b_sh + B
                A = A * a_sh
            e = A * ce + B
            e_next = jnp.where(row < SUBLANES - 1, pltpu.roll(e, SUBLANES - 1, 0), ce)
            g_s[pl.ds(o, SUBLANES), :] = dh_s[pl.ds(o, SUBLANES), :] + e_next
            return jnp.broadcast_to(e[0:1, :], (SUBLANES, D))

        carry_e[...] = lax.fori_loop(0, hb, step, carry_e[...])

        for j in range(N_GROUPS):
            sl = slice(LANES * j, LANES * (j + 1))
            gg = g_s[:, sl]
            xa, r, ig, mult, a = xa_s[:, sl], r_s[:, sl], ig_s[:, sl], m_s[:, sl], a_s[:, sl]
            hprev = hpad[pl.ds(SUBLANES - 1, tc), sl]
            sp = _softplus(-lam_ref[:, sl])
            da = gg * hprev
            dmult = gg * (ig * xa)
            dig = gg * (mult * xa)
            dxa = gg * (mult * ig)
            dla = da * a - dmult * ((a * a) / mult)
            dr = dla * ((-LRU_C) * sp)
            dlam_ref[:, sl] += jnp.sum(dla * r, axis=0, keepdims=True)
            dza = dr * r * (1.0 - r)
            dzx = dig * ig * (1.0 - ig)
            dba_ref[:, sl] += jnp.sum(dza, axis=0, keepdims=True)
            dbx_ref[:, sl] += jnp.sum(dzx, axis=0, keepdims=True)
            dz = jnp.concatenate([dza, dzx], axis=1).astype(BF16)
            dbd_ref[j] += lax.dot_general(xa.astype(BF16), dz, TN_DIMS, preferred_element_type=F32)
            dxa = dxa + lax.dot_general(dz, bd_ref[j], NT_DIMS, preferred_element_type=F32)
            dxa_pad[0:tc, sl] = dxa

        dxa_pad[tc:, :] = dxa_head[...]
        dxa_head[...] = dxa_pad[0:SUBLANES, :]

        for j in range(N_GROUPS):
            sl = slice(LANES * j, LANES * (j + 1))
            dxa = dxa_pad[0:tc, sl]
            dxl = jnp.zeros((tc, LANES), F32)
            for k in range(CONV_W):
                dxl = dxl + dxa_pad[pl.ds(CONV_W - 1 - k, tc), sl] * cw_ref[k:k + 1, sl]
                dcw_ref[k:k + 1, sl] += jnp.sum(
                    dxa * xpad[pl.ds(SUBLANES - (CONV_W - 1) + k, tc), sl], axis=0, keepdims=True)
            dxg_ref[:, sl] = dxl.astype(BF16)
            dcb_ref[:, sl] += jnp.sum(dxa, axis=0, keepdims=True)

        @pl.when(i == nc - 1)
        def _():
            dlam_ref[...] = dlam_ref[...] * (LRU_C * _sigmoid(-lam_ref[...]))

    rev = lambda col: pl.BlockSpec((tc, D), lambda i, col=col: (nc - 1 - i, col))
    halo = pl.BlockSpec((SUBLANES, D), lambda i: (jnp.maximum((nc - 1 - i) * hb - 1, 0), 0))
    full = lambda shape: pl.BlockSpec(shape, lambda i: tuple(0 for _ in shape))
    big = lambda: pltpu.VMEM((tc, D), F32)
    return pl.pallas_call(
        body, name="lru_bwd", grid=(nc,),
        in_specs=[rev(0), halo, rev(1), rev(0), halo, rev(0), full((CONV_W, D)), full((1, D)),
                  full((N_GROUPS, LANES, 2 * LANES)), full((1, D)), full((1, D)), full((1, D))],
        out_specs=[pl.BlockSpec((tc, 2 * D), lambda i: (nc - 1 - i, 0)), full((CONV_W, D)), full((1, D)),
                   full((1, D)), full((1, D)), full((1, D)), full((N_GROUPS, LANES, 2 * LANES))],
        out_shape=[jax.ShapeDtypeStruct((S, 2 * D), BF16), jax.ShapeDtypeStruct((CONV_W, D), F32),
                   jax.ShapeDtypeStruct((1, D), F32), jax.ShapeDtypeStruct((1, D), F32),
                   jax.ShapeDtypeStruct((1, D), F32), jax.ShapeDtypeStruct((1, D), F32),
                   jax.ShapeDtypeStruct((N_GROUPS, LANES, 2 * LANES), F32)],
        scratch_shapes=[pltpu.VMEM((tc + SUBLANES, D), F32), pltpu.VMEM((tc + SUBLANES, D), F32),
                        big(), big(), big(), big(), big(), big(), big(), big(),
                        pltpu.VMEM((tc + SUBLANES, D), F32), pltpu.VMEM((SUBLANES, D), F32),
                        pltpu.VMEM((SUBLANES, D), F32)],
        compiler_params=_cparams("arbitrary"),
    )(xg, xg, xg, h, h, dyain, cw, cb, bd, ba, bx, lam)


def _forget_cumsum(fl, fb):
    S = fl.shape[0]
    tr = min(ROW_TILE, S)
    hb = tr // SUBLANES

    def body(fl_ref, fb_ref, o_ref, lf_s, carry):
        @pl.when(pl.program_id(0) == 0)
        def _():
            carry[...] = jnp.zeros_like(carry)

        lf_s[...] = -_softplus(-(fl_ref[...] + fb_ref[...]))
        row = lax.broadcasted_iota(jnp.int32, (SUBLANES, LANES), 0)

        def step(t, c):
            o = pl.multiple_of(t * SUBLANES, SUBLANES)
            B = lf_s[pl.ds(o, SUBLANES), :]
            for d in (1, 2, 4):
                B = B + jnp.where(row >= d, pltpu.roll(B, d, 0), 0.0)
            B = B + c
            o_ref[pl.ds(o, SUBLANES), :] = B
            return jnp.broadcast_to(B[SUBLANES - 1:SUBLANES, :], (SUBLANES, LANES))

        carry[...] = lax.fori_loop(0, hb, step, carry[...])

    return pl.pallas_call(
        body, name="forget_cumsum", grid=(S // tr,),
        in_specs=[pl.BlockSpec((tr, LANES), lambda i: (i, 0)), pl.BlockSpec((1, LANES), lambda i: (0, 0))],
        out_specs=pl.BlockSpec((tr, LANES), lambda i: (i, 0)),
        out_shape=jax.ShapeDtypeStruct((S, LANES), F32),
        scratch_shapes=[pltpu.VMEM((tr, LANES), F32), pltpu.VMEM((SUBLANES, LANES), F32)],
        compiler_params=_cparams("arbitrary"),
    )(fl, fb)


def _forget_bwd(dF, fl, fb):
    S = fl.shape[0]
    tr = min(ROW_TILE, S)
    hb = tr // SUBLANES
    nc = S // tr

    def body(df_ref, fl_ref, fb_ref, o_ref, dfb_ref, carry):
        @pl.when(pl.program_id(0) == 0)
        def _():
            carry[...] = jnp.zeros_like(carry)
            dfb_ref[...] = jnp.zeros_like(dfb_ref)

        row = lax.broadcasted_iota(jnp.int32, (SUBLANES, LANES), 0)

        def step(tt, carried):
            c, acc = carried
            o = pl.multiple_of((hb - 1 - tt) * SUBLANES, SUBLANES)
            B = df_ref[pl.ds(o, SUBLANES), :]
            for d in (1, 2, 4):
                B = B + jnp.where(row < SUBLANES - d, pltpu.roll(B, SUBLANES - d, 0), 0.0)
            B = B + c
            z = fl_ref[pl.ds(o, SUBLANES), :] + fb_ref[...]
            dz = B * _sigmoid(-z)
            o_ref[pl.ds(o, SUBLANES), :] = dz.astype(BF16)
            return jnp.broadcast_to(B[0:1, :], (SUBLANES, LANES)), acc + dz

        c, acc = lax.fori_loop(0, hb, step, (carry[...], jnp.zeros((SUBLANES, LANES), F32)))
        carry[...] = c
        dfb_ref[...] += jnp.sum(acc, axis=0, keepdims=True)

    rev = pl.BlockSpec((tr, LANES), lambda i: (nc - 1 - i, 0))
    vec = pl.BlockSpec((1, LANES), lambda i: (0, 0))
    return pl.pallas_call(
        body, name="forget_bwd", grid=(nc,),
        in_specs=[rev, rev, vec],
        out_specs=[rev, vec],
        out_shape=[jax.ShapeDtypeStruct((S, LANES), BF16), jax.ShapeDtypeStruct((1, LANES), F32)],
        scratch_shapes=[pltpu.VMEM((SUBLANES, LANES), F32)],
        compiler_params=_cparams("arbitrary"),
    )(dF, fl, fb)


def _attn_scores(q, k, fq, fk, qi, ki, bq, bk):
    s = lax.dot_general(q, k, NT_DIMS, preferred_element_type=F32) * ATTN_SCALE
    s = s + fq - fk
    rows = lax.broadcasted_iota(jnp.int32, (bq, bk), 0) + qi * bq
    cols = lax.broadcasted_iota(jnp.int32, (bq, bk), 1) + ki * bk
    return s, rows >= cols


def _attn_fwd(qkv, fq, fk):
    S = qkv.shape[0]
    bq = bk = min(ATTN_BLOCK, S)
    nq, nk = S // bq, S // bk

    def body(q_ref, k_ref, v_ref, fq_ref, fk_ref, o_ref, lse_ref, m_s, l_s, acc_s):
        qi, ki = pl.program_id(1), pl.program_id(2)

        @pl.when(ki == 0)
        def _():
            m_s[...] = jnp.full_like(m_s, NEG_BIG)
            l_s[...] = jnp.zeros_like(l_s)
            acc_s[...] = jnp.zeros_like(acc_s)

        @pl.when(ki <= qi)
        def _():
            s, mask = _attn_scores(q_ref[...], k_ref[...], fq_ref[...], fk_ref[...], qi, ki, bq, bk)
            s = jnp.where(mask, s, NEG_BIG)
            m_old = m_s[...]
            m_new = jnp.maximum(m_old, jnp.max(s, axis=-1, keepdims=True))
            alpha = jnp.exp(m_old - m_new)
            p = jnp.exp(s - m_new)
            l_s[...] = alpha * l_s[...] + jnp.sum(p, axis=-1, keepdims=True)
            acc_s[...] = alpha * acc_s[...] + jnp.dot(p.astype(BF16), v_ref[...], preferred_element_type=F32)
            m_s[...] = m_new

        @pl.when(ki == nk - 1)
        def _():
            o_ref[...] = (acc_s[...] / l_s[...]).astype(BF16)
            lse_ref[...] = m_s[...] + jnp.log(l_s[...])

    kv = lambda off: pl.BlockSpec((bk, HEAD_DIM), lambda h, qi, ki, off=off: (jnp.minimum(ki, qi), off + h))
    return pl.pallas_call(
        body, name="attn_fwd", grid=(N_HEADS, nq, nk),
        in_specs=[pl.BlockSpec((bq, HEAD_DIM), lambda h, qi, ki: (qi, h)), kv(N_HEADS), kv(2 * N_HEADS),
                  pl.BlockSpec((None, bq, 1), lambda h, qi, ki: (h, qi, 0)),
                  pl.BlockSpec((None, 1, bk), lambda h, qi, ki: (h, 0, jnp.minimum(ki, qi)))],
        out_specs=[pl.BlockSpec((bq, HEAD_DIM), lambda h, qi, ki: (qi, h)),
                   pl.BlockSpec((None, bq, 1), lambda h, qi, ki: (h, qi, 0))],
        out_shape=[jax.ShapeDtypeStruct((S, N_HEADS * HEAD_DIM), BF16), jax.ShapeDtypeStruct((N_HEADS, S, 1), F32)],
        scratch_shapes=[pltpu.VMEM((bq, 1), F32), pltpu.VMEM((bq, 1), F32), pltpu.VMEM((bq, HEAD_DIM), F32)],
        compiler_params=_cparams("parallel", "parallel", "arbitrary"),
    )(qkv, qkv, qkv, fq, fk)


def _attn_delta(dob, ob):
    S = ob.shape[0]
    tr = min(ROW_TILE, S)

    def body(do_ref, o_ref, d_ref):
        prod = do_ref[...].astype(F32) * o_ref[...].astype(F32)
        for h in range(N_HEADS):
            d_ref[h] = jnp.sum(prod[:, HEAD_DIM * h:HEAD_DIM * (h + 1)], axis=-1, keepdims=True)

    row = pl.BlockSpec((tr, N_HEADS * HEAD_DIM), lambda i: (i, 0))
    return pl.pallas_call(
        body, name="attn_delta", grid=(S // tr,),
        in_specs=[row, row],
        out_specs=pl.BlockSpec((N_HEADS, tr, 1), lambda i: (0, i, 0)),
        out_shape=jax.ShapeDtypeStruct((N_HEADS, S, 1), F32),
        compiler_params=_cparams("parallel"),
    )(dob, ob)


def _attn_bwd(qkv, dob, lse, delta, fq, fk):
    S = qkv.shape[0]
    bq = bk = min(ATTN_BLOCK, S)
    nq, nk = S // bq, S // bk

    def body(q_ref, k_ref, v_ref, do_ref, lse_ref, dl_ref, fq_ref, fk_ref,
             dq_ref, dk_ref, dv_ref, dfk_ref, dfq_ref, dq_s, dk_s, dv_s, dfk_s, dfq_s):
        ki, qi = pl.program_id(1), pl.program_id(2)

        @pl.when((ki == 0) & (qi == 0))
        def _():
            dq_s[...] = jnp.zeros_like(dq_s)
            dfq_s[...] = jnp.zeros_like(dfq_s)

        @pl.when(qi == 0)
        def _():
            dk_s[...] = jnp.zeros_like(dk_s)
            dv_s[...] = jnp.zeros_like(dv_s)
            dfk_s[...] = jnp.zeros_like(dfk_s)

        @pl.when(qi >= ki)
        def _():
            q, k, v, do = q_ref[...], k_ref[...], v_ref[...], do_ref[...]
            s, mask = _attn_scores(q, k, fq_ref[...], fk_ref[...], qi, ki, bq, bk)
            p = jnp.where(mask, jnp.exp(s - lse_ref[...]), 0.0)
            dv_s[...] += lax.dot_general(p.astype(BF16), do, TN_DIMS, preferred_element_type=F32)
            dp = lax.dot_general(do, v, NT_DIMS, preferred_element_type=F32)
            ds = p * (dp - dl_ref[...])
            dfk_s[...] -= jnp.sum(ds, axis=0, keepdims=True)
            dsb = ds.astype(BF16)
            dk_s[...] += lax.dot_general(dsb, q, TN_DIMS, preferred_element_type=F32)
            rows = pl.ds(pl.multiple_of(qi * bq, bq), bq)
            dq_s[rows, :] += jnp.dot(dsb, k, preferred_element_type=F32)
            part = ds[:, 0:LANES]
            for blk in range(1, bk // LANES):
                part = part + ds[:, LANES * blk:LANES * (blk + 1)]
            dfq_s[rows, :] += part

        @pl.when(qi == nq - 1)
        def _():
            dk_ref[...] = (dk_s[...] * ATTN_SCALE).astype(BF16)
            dv_ref[...] = dv_s[...].astype(BF16)
            dfk_ref[...] = dfk_s[...]

        @pl.when(ki == nk - 1)
        def _():
            rows = pl.ds(pl.multiple_of(qi * bq, bq), bq)
            dq_ref[rows, :] = (dq_s[rows, :] * ATTN_SCALE).astype(BF16)
            dfq_ref[rows, :] = jnp.sum(dfq_s[rows, :], axis=-1, keepdims=True)

    qrow = lambda off: pl.BlockSpec((bq, HEAD_DIM), lambda h, ki, qi, off=off: (jnp.maximum(qi, ki), off + h))
    krow = lambda off: pl.BlockSpec((bk, HEAD_DIM), lambda h, ki, qi, off=off: (ki, off + h))
    qcol = pl.BlockSpec((None, bq, 1), lambda h, ki, qi: (h, jnp.maximum(qi, ki), 0))
    return pl.pallas_call(
        body, name="attn_bwd", grid=(N_HEADS, nk, nq),
        in_specs=[qrow(0), krow(N_HEADS), krow(2 * N_HEADS), qrow(0), qcol, qcol, qcol,
                  pl.BlockSpec((None, 1, bk), lambda h, ki, qi: (h, 0, ki))],
        out_specs=[pl.BlockSpec((S, HEAD_DIM), lambda h, ki, qi: (0, h)),
                   pl.BlockSpec((bk, HEAD_DIM), lambda h, ki, qi: (ki, h)),
                   pl.BlockSpec((bk, HEAD_DIM), lambda h, ki, qi: (ki, h)),
                   pl.BlockSpec((None, 1, bk), lambda h, ki, qi: (h, 0, ki)),
                   pl.BlockSpec((None, S, 1), lambda h, ki, qi: (h, 0, 0))],
        out_shape=[jax.ShapeDtypeStruct((S, N_HEADS * HEAD_DIM), BF16)] * 3
        + [jax.ShapeDtypeStruct((N_HEADS, 1, S), F32), jax.ShapeDtypeStruct((N_HEADS, S, 1), F32)],
        scratch_shapes=[pltpu.VMEM((S, HEAD_DIM), F32), pltpu.VMEM((bk, HEAD_DIM), F32),
                        pltpu.VMEM((bk, HEAD_DIM), F32), pltpu.VMEM((1, bk), F32), pltpu.VMEM((S, LANES), F32)],
        compiler_params=_cparams("parallel", "arbitrary", "arbitrary"),
    )(qkv, qkv, qkv, dob, lse, delta, fq, fk)


def _gate_mix(gates, ya, yb):
    S, D = ya.shape
    tr = min(ROW_TILE, S)

    def body(ga_ref, gb_ref, ya_ref, yb_ref, o_ref):
        o_ref[...] = (_sigmoid(ga_ref[...]) * ya_ref[...] + _sigmoid(gb_ref[...]) * yb_ref[...]).astype(BF16)

    col = lambda j: pl.BlockSpec((tr, D), lambda i, j=j: (i, j))
    return pl.pallas_call(
        body, name="gate_mix", grid=(S // tr,),
        in_specs=[col(0), col(1), col(0), col(0)],
        out_specs=col(0),
        out_shape=jax.ShapeDtypeStruct((S, D), BF16),
        compiler_params=_cparams("parallel"),
    )(gates, gates, ya, yb)


def _gate_bwd(dmix, gates, ya, yb):
    S, D = ya.shape
    tr = min(ROW_TILE, S)

    def body(dm_ref, ga_ref, gb_ref, ya_ref, yb_ref, dya_ref, dyb_ref, dg_ref):
        dm = dm_ref[...]
        sa, sb = _sigmoid(ga_ref[...]), _sigmoid(gb_ref[...])
        dya_ref[...] = (dm * sa).astype(BF16)
        dyb_ref[...] = (dm * sb).astype(BF16)
        dg_ref[:, 0:D] = ((dm * ya_ref[...]) * (sa * (1.0 - sa))).astype(BF16)
        dg_ref[:, D:] = ((dm * yb_ref[...]) * (sb * (1.0 - sb))).astype(BF16)

    col = lambda j: pl.BlockSpec((tr, D), lambda i, j=j: (i, j))
    return pl.pallas_call(
        body, name="gate_bwd", grid=(S // tr,),
        in_specs=[col(0), col(0), col(1), col(0), col(0)],
        out_specs=[col(0), col(0), pl.BlockSpec((tr, 2 * D), lambda i: (i, 0))],
        out_shape=[jax.ShapeDtypeStruct((S, D), BF16), jax.ShapeDtypeStruct((S, D), BF16),
                   jax.ShapeDtypeStruct((S, 2 * D), BF16)],
        compiler_params=_cparams("parallel"),
    )(dmix, gates, gates, ya, yb)


def _mesh_place():
    x, y, c = lax.axis_index("x"), lax.axis_index("y"), lax.axis_index("c")
    chips = [(1 - x, y), (x, 1 - y), (1 - x, 1 - y)]
    return x, y, c, chips


def _all_gather(shards):
    n = len(shards)

    def body(*refs):
        ins, outs = refs[:n], refs[n:2 * n]
        send_sems, recv_sems, local_sems = refs[2 * n:]
        x, y, c, chips = _mesh_place()
        me, sib = (x, y, c), (x, y, 1 - c)

        def copy(a, k, block, to, src=None):
            px, py, pc = block
            dst = outs[a].at[4 * px + 2 * py + pc]
            return pltpu.make_async_remote_copy(
                src_ref=dst if src is None else src, dst_ref=dst,
                send_sem=send_sems.at[a, k], recv_sem=recv_sems.at[a, k],
                device_id=to, device_id_type=MESH_ID)

        mine = [pltpu.make_async_copy(ins[a], outs[a].at[4 * x + 2 * y + c], local_sems.at[a]) for a in range(n)]
        for cp in mine:
            cp.start()
        first = []
        for a in range(n):
            first.append(copy(a, 0, me, sib, src=ins[a]))
            for j, chip in enumerate(chips):
                first.append(copy(a, 1 + j, me, (*chip, c), src=ins[a]))
        for cp in first:
            cp.start()
        passed = []
        for j, chip in enumerate(chips):
            for a in range(n):
                copy(a, 1 + j, (*chip, c), me).wait_recv()
                fwd = copy(a, 4 + j, (*chip, c), sib)
                fwd.start()
                passed.append(fwd)
        for a in range(n):
            copy(a, 0, sib, me).wait_recv()
            for j, chip in enumerate(chips):
                copy(a, 4 + j, (*chip, 1 - c), me).wait_recv()
        for cp in first + passed:
            cp.wait_send()
        for cp in mine:
            cp.wait()

    return pl.pallas_call(
        body, name="all_gather_weights",
        in_specs=[ANY] * n, out_specs=[ANY] * n,
        out_shape=[jax.ShapeDtypeStruct((N_DEV,) + s.shape, s.dtype) for s in shards],
        scratch_shapes=[pltpu.SemaphoreType.DMA((n, 7)), pltpu.SemaphoreType.DMA((n, 7)),
                        pltpu.SemaphoreType.DMA((n,))],
    )(*shards)


def _reduce_scatter_cores(grads):
    n = len(grads)

    def body(*refs):
        ins, owns, gots = refs[:n], refs[n:2 * n], refs[2 * n:3 * n]
        send_sems, recv_sems, local_sems = refs[3 * n:]
        x, y, c, _ = _mesh_place()
        sib = (x, y, 1 - c)
        local, remote = [], []
        for a in range(n):
            for k in range(4):
                local.append(pltpu.make_async_copy(ins[a].at[2 * k + c], owns[a].at[k], local_sems.at[a, k]))
                remote.append(pltpu.make_async_remote_copy(
                    src_ref=ins[a].at[2 * k + (1 - c)], dst_ref=gots[a].at[k],
                    send_sem=send_sems.at[a, k], recv_sem=recv_sems.at[a, k],
                    device_id=sib, device_id_type=MESH_ID))
        for cp in remote + local:
            cp.start()
        for cp in remote:
            cp.wait_recv()
        for cp in remote:
            cp.wait_send()
        for cp in local:
            cp.wait()

    quarter = [jax.ShapeDtypeStruct((4,) + g.shape[1:], g.dtype) for g in grads]
    res = pl.pallas_call(
        body, name="reduce_scatter_cores",
        in_specs=[ANY] * n, out_specs=[ANY] * (2 * n),
        out_shape=quarter + quarter,
        scratch_shapes=[pltpu.SemaphoreType.DMA((n, 4)), pltpu.SemaphoreType.DMA((n, 4)),
                        pltpu.SemaphoreType.DMA((n, 4))],
    )(*grads)
    return res[:n], res[n:]


def _chip_partial_sum(own, got):
    R, C = own.shape[1:]
    tr = min(256, R)
    assert R % tr == 0

    def body(a_ref, b_ref, s_ref, sb_ref):
        s = a_ref[...] + b_ref[...]
        s_ref[...] = s
        sb_ref[...] = s.astype(BF16)

    blk = pl.BlockSpec((None, tr, C), lambda k, i: (k, i, 0))
    return pl.pallas_call(
        body, name="chip_partial_sum", grid=(4, R // tr),
        in_specs=[blk, blk], out_specs=[blk, blk],
        out_shape=[jax.ShapeDtypeStruct(own.shape, F32), jax.ShapeDtypeStruct(own.shape, BF16)],
        compiler_params=_cparams("parallel", "parallel"),
    )(own, got)


def _reduce_scatter_chips(sums_f32, sums_bf16):
    n = len(sums_f32)

    def body(*refs):
        f32s, bf16s, mines, gots = refs[:n], refs[n:2 * n], refs[2 * n:3 * n], refs[3 * n:4 * n]
        send_sems, recv_sems, local_sems = refs[4 * n:]
        x, y, c, chips = _mesh_place()
        local, remote = [], []
        for a in range(n):
            local.append(pltpu.make_async_copy(f32s[a].at[2 * x + y], mines[a], local_sems.at[a]))
            for j, (px, py) in enumerate(chips):
                remote.append(pltpu.make_async_remote_copy(
                    src_ref=bf16s[a].at[2 * px + py], dst_ref=gots[a].at[j],
                    send_sem=send_sems.at[a, j], recv_sem=recv_sems.at[a, j],
                    device_id=(px, py, c), device_id_type=MESH_ID))
        for cp in remote + local:
            cp.start()
        for cp in remote:
            cp.wait_recv()
        for cp in remote:
            cp.wait_send()
        for cp in local:
            cp.wait()

    res = pl.pallas_call(
        body, name="reduce_scatter_chips",
        in_specs=[ANY] * (2 * n), out_specs=[ANY] * (2 * n),
        out_shape=[jax.ShapeDtypeStruct(s.shape[1:], F32) for s in sums_f32]
        + [jax.ShapeDtypeStruct((3,) + s.shape[1:], BF16) for s in sums_bf16],
        scratch_shapes=[pltpu.SemaphoreType.DMA((n, 3)), pltpu.SemaphoreType.DMA((n, 3)),
                        pltpu.SemaphoreType.DMA((n,))],
    )(*sums_f32, *sums_bf16)
    return res[:n], res[n:]


def _all_reduce_small(vec):
    R = vec.shape[0]

    def body(v_ref, o_ref, sib_buf, chip_buf, send_sems, recv_sems):
        x, y, c, chips = _mesh_place()
        swap = pltpu.make_async_remote_copy(
            src_ref=v_ref, dst_ref=sib_buf, send_sem=send_sems.at[0], recv_sem=recv_sems.at[0],
            device_id=(x, y, 1 - c), device_id_type=MESH_ID)
        swap.start()
        swap.wait()
        my_chip = 2 * x + y
        chip_buf[my_chip] = v_ref[...] + sib_buf[...]
        sends = []
        for j, (px, py) in enumerate(chips):
            cp = pltpu.make_async_remote_copy(
                src_ref=chip_buf.at[my_chip], dst_ref=chip_buf.at[my_chip],
                send_sem=send_sems.at[1 + j], recv_sem=recv_sems.at[1 + j],
                device_id=(px, py, c), device_id_type=MESH_ID)
            cp.start()
            sends.append(cp)
        for j, (px, py) in enumerate(chips):
            pltpu.make_async_remote_copy(
                src_ref=chip_buf.at[2 * px + py], dst_ref=chip_buf.at[2 * px + py],
                send_sem=send_sems.at[1 + j], recv_sem=recv_sems.at[1 + j],
                device_id=(px, py, c), device_id_type=MESH_ID).wait_recv()
        for cp in sends:
            cp.wait_send()
        o_ref[...] = ((chip_buf[0] + chip_buf[1]) + chip_buf[2]) + chip_buf[3]

    vm = pl.BlockSpec(memory_space=pltpu.VMEM)
    return pl.pallas_call(
        body, name="all_reduce_small",
        in_specs=[vm], out_specs=vm,
        out_shape=jax.ShapeDtypeStruct(vec.shape, F32),
        scratch_shapes=[pltpu.VMEM((R, LANES), F32), pltpu.VMEM((4, R, LANES), F32),
                        pltpu.SemaphoreType.DMA((4,)), pltpu.SemaphoreType.DMA((4,))],
    )(vec)


def _adamw_math(w, g, m, v):
    m = ADAM_B1 * m + (1.0 - ADAM_B1) * g
    v = ADAM_B2 * v + (1.0 - ADAM_B2) * (g * g)
    m_hat = m / (1.0 - ADAM_B1 ** ADAM_STEP)
    v_hat = v / (1.0 - ADAM_B2 ** ADAM_STEP)
    delta = -ADAM_LR * (m_hat / (jnp.sqrt(v_hat) + ADAM_EPS) + ADAM_WD * w)
    return delta, m, v


def _adamw(w, m, v, g_own, g_got, name):
    R, C = w.shape
    tr = R if R * C <= 256 * D_MODEL else 256
    assert R % tr == 0
    n_got = 0 if g_got is None else 3

    def body(*refs):
        w_ref, m_ref, v_ref, go_ref = refs[:4]
        got = refs[4:4 + n_got]
        g_ref, d_ref, nm_ref, nv_ref = refs[4 + n_got:]
        g = go_ref[...]
        for r in got:
            g = g + r[...].astype(F32)
        delta, m_new, v_new = _adamw_math(w_ref[...], g, m_ref[...], v_ref[...])
        g_ref[...] = g
        d_ref[...] = delta
        nm_ref[...] = m_new
        nv_ref[...] = v_new

    blk = pl.BlockSpec((tr, C), lambda i: (i, 0))
    got_specs = [pl.BlockSpec((None, tr, C), lambda i, j=j: (j, i, 0)) for j in range(n_got)]
    return pl.pallas_call(
        body, name=name, grid=(R // tr,),
        in_specs=[blk] * 4 + got_specs, out_specs=[blk] * 4,
        out_shape=[jax.ShapeDtypeStruct((R, C), F32)] * 4,
        compiler_params=_cparams("parallel"),
    )(w, m, v, g_own, *([g_got] * n_got))


def _block_diag_pairs(wa, wx):
    def pairs(w):
        w = w.reshape(N_GROUPS, 2, LRU_BW, LRU_BW)
        z = jnp.zeros((N_GROUPS, LRU_BW, LRU_BW), w.dtype)
        top = jnp.concatenate([w[:, 0], z], axis=2)
        bot = jnp.concatenate([z, w[:, 1]], axis=2)
        return jnp.concatenate([top, bot], axis=1)
    return jnp.concatenate([pairs(wa), pairs(wx)], axis=2).astype(BF16)


def _block_diag_unpair(dbd):
    def unpair(g):
        blocks = jnp.stack([g[:, :LRU_BW, :LRU_BW], g[:, LRU_BW:, LRU_BW:]], axis=1)
        return blocks.reshape(LRU_BLOCKS, LRU_BW, LRU_BW)
    return unpair(dbd[:, :, :LANES]), unpair(dbd[:, :, LANES:])


def _local_step(x, target, W, small):
    S, D = x.shape
    g1, g2, g3 = small["norm_mix_g"], small["norm_mlp_g"], small["norm_final_g"]
    cw, cb = small["conv_w"], small["conv_b"].reshape(1, D)
    ba, bx, lam = (small[k].reshape(1, D) for k in ("lru_ba", "lru_bx", "lru_lambda"))
    fb = jnp.pad(small["forget_b"], (0, LANES - N_HEADS)).reshape(1, LANES)
    bd = _block_diag_pairs(small["lru_wa"], small["lru_wx"])
    big = dict(tm=1024, tn=1024)

    u = _norm_fwd(x, g1, "norm_mix")
    (xg,) = _mm([(u, W["in_xg"])], tks=[D], outs=[F32], name="proj_xg", **big)
    (qkv,) = _mm([(u, W["in_qkv"])], tks=[D], outs=[BF16], name="proj_qkv", **big)
    (gates,) = _mm([(u, W["in_gates"])], tks=[D], outs=[F32], name="proj_gates", **big)
    (fl,) = _mm([(u, W["in_f"])], tks=[D], outs=[F32], name="proj_forget", **big)
    h, yain = _lru_fwd(xg, cw, cb, bd, ba, bx, lam)
    fcum = _forget_cumsum(fl, fb)
    f_heads = fcum[:, :N_HEADS].T
    fq, fk = f_heads.reshape(N_HEADS, S, 1), f_heads.reshape(N_HEADS, 1, S)
    ob, lse = _attn_fwd(qkv, fq, fk)
    (ya,) = _mm([(yain, W["branch_a"])], tks=[D], outs=[F32], name="branch_a", **big)
    (yb,) = _mm([(ob, W["branch_b"])], tks=[D], outs=[F32], name="branch_b", **big)
    mix = _gate_mix(gates, ya, yb)
    (x1,) = _mm([(mix, W["out"])], tks=[D], outs=[F32], name="out_proj", extra=(x,),
                epi=lambda acc, res: (res + acc,), **big)
    m = _norm_fwd(x1, g2, "norm_mlp")
    relu, hh = _mm([(m, W["up"])], tks=[D], outs=[BF16, BF16], name="mlp_up",
                   epi=lambda acc: (jnp.maximum(acc, 0.0), jnp.square(jnp.maximum(acc, 0.0))), **big)
    (x2,) = _mm([(hh, W["down"])], tks=[1024], outs=[F32], name="mlp_down", extra=(x1,),
                epi=lambda acc, res: (res + acc,), **big)
    loss_acc, dg3, dx2, dx2b = _final_norm_loss(x2, target, g3)

    (dhpre,) = _mm([(dx2b, W["down"])], tb=True, tks=[D], outs=[BF16], name="d_mlp_act", extra=(relu,),
                   epi=lambda acc, r: (acc * (2.0 * r.astype(F32)),), **big)
    (dw_down,) = _mm([(hh, dx2b)], ta=True, tks=[min(1024, S)], outs=[F32], name="dw_down", **big)
    (dm,) = _mm([(dhpre, W["up"])], tb=True, tks=[1024], outs=[F32], name="d_mlp_in", **big)
    (dw_up,) = _mm([(m, dhpre)], ta=True, tks=[min(1024, S)], outs=[F32], name="dw_up", **big)
    dx1, dx1b, dg2 = _norm_bwd(dm, x1, g2, dx2, "norm_mlp_bwd")
    (dmix,) = _mm([(dx1b, W["out"])], tb=True, tks=[D], outs=[F32], name="d_mix", **big)
    (dw_out,) = _mm([(mix, dx1b)], ta=True, tks=[min(1024, S)], outs=[F32], name="dw_out", **big)
    dya, dyb, dgates = _gate_bwd(dmix, gates, ya, yb)
    (dob,) = _mm([(dyb, W["branch_b"])], tb=True, tks=[D], outs=[BF16], name="d_attn_out", **big)
    (dw_b,) = _mm([(ob, dyb)], ta=True, tks=[min(1024, S)], outs=[F32], name="dw_branch_b", **big)
    (dyain,) = _mm([(dya, W["branch_a"])], tb=True, tks=[D], outs=[F32], name="d_lru_out", **big)
    (dw_a,) = _mm([(yain, dya)], ta=True, tks=[min(1024, S)], outs=[F32], name="dw_branch_a", **big)
    delta = _attn_delta(dob, ob)
    dq, dk, dv, dfk, dfq = _attn_bwd(qkv, dob, lse, delta, fq, fk)
    dF = jnp.pad((dfk.reshape(N_HEADS, S) + dfq.reshape(N_HEADS, S)).T, ((0, 0), (0, LANES - N_HEADS)))
    dfl, dfb = _forget_bwd(dF, fl, fb)
    dxg, dcw, dcb, dba, dbx, dlam, dbd = _lru_bwd(xg, h, dyain, cw, cb, bd, ba, bx, lam)
    wq, wk, wv = (W["in_qkv"][:, D * i:D * (i + 1)] for i in range(3))
    (du,) = _mm([(dxg, W["in_xg"]), (dq, wq), (dk, wk), (dv, wv), (dgates, W["in_gates"]), (dfl, W["in_f"])],
                tb=True, tks=[1024, D, D, D, 1024, LANES], outs=[F32], name="d_norm_mix_out", tm=512, tn=512)
    tks = [min(1024, S)]
    dw_in_parts = [
        _mm([(u, dxg)], ta=True, tks=tks, outs=[F32], name="dw_in_xg", **big)[0],
        _mm([(u, dq)], ta=True, tks=tks, outs=[F32], name="dw_in_q", **big)[0],
        _mm([(u, dk)], ta=True, tks=tks, outs=[F32], name="dw_in_k", **big)[0],
        _mm([(u, dv)], ta=True, tks=tks, outs=[F32], name="dw_in_v", **big)[0],
        _mm([(u, dgates)], ta=True, tks=tks, outs=[F32], name="dw_in_gates", **big)[0],
        _mm([(u, dfl)], ta=True, tks=tks, outs=[F32], name="dw_in_forget", **big)[0][:, :N_HEADS],
    ]
    grad_x, _, dg1 = _norm_bwd(du, x, g1, dx1, "norm_mix_bwd")

    dwa, dwx = _block_diag_unpair(dbd)
    big_grads = dict(w_in=jnp.concatenate(dw_in_parts, axis=1), w_branch_a=dw_a, w_branch_b=dw_b, w_out=dw_out,
                     w_up=dw_up, w_down=dw_down)
    small_grads = dict(norm_mix_g=dg1.reshape(D), conv_w=dcw, conv_b=dcb.reshape(D), lru_wa=dwa, lru_ba=dba.reshape(D),
                       lru_wx=dwx, lru_bx=dbx.reshape(D), lru_lambda=dlam.reshape(D), forget_b=dfb[0, :N_HEADS],
                       norm_mlp_g=dg2.reshape(D), norm_final_g=dg3.reshape(D))
    return loss_acc[0, 0], grad_x, big_grads, small_grads


SMALL_NAMES = ("norm_mix_g", "conv_b", "lru_wa", "lru_ba", "lru_wx", "lru_bx", "lru_lambda", "forget_b",
               "norm_mlp_g", "norm_final_g")
TILE_ELEMS = SUBLANES * LANES


def _pack_small(parts):
    rows = []
    for p in parts:
        flat = p.reshape(-1)
        flat = jnp.pad(flat, (0, (-flat.shape[0]) % TILE_ELEMS))
        rows.append(flat.reshape(-1, LANES))
    return jnp.concatenate(rows, axis=0)


def _unpack_small(packed, shapes):
    out, r = [], 0
    for shp in shapes:
        size = math.prod(shp)
        nrows = -(-size // TILE_ELEMS) * SUBLANES
        out.append(packed[r:r + nrows].reshape(-1)[:size].reshape(shp))
        r += nrows
    return out


BIG_NAMES = ("w_in", "w_branch_a", "w_branch_b", "w_out", "w_up", "w_down")
WEIGHT_ORDER = ("norm_mix_g", "w_in", "conv_w", "conv_b", "lru_wa", "lru_ba", "lru_wx", "lru_bx", "lru_lambda",
                "forget_b", "w_branch_a", "w_branch_b", "w_out", "norm_mlp_g", "w_up", "w_down", "norm_final_g")


def _to_dest_blocks(name, g):
    if name in ("w_in", "w_up"):
        return g.reshape(g.shape[0], N_DEV, g.shape[1] // N_DEV).transpose(1, 0, 2)
    return g.reshape(N_DEV, g.shape[0] // N_DEV, g.shape[1])


def kernel(x, norm_mix_g, w_in, conv_w, conv_b, lru_wa, lru_ba, lru_wx, lru_bx, lru_lambda, forget_b, w_branch_a, w_branch_b, w_out, norm_mlp_g, w_up, w_down, norm_final_g, loss_target, m_norm_mix_g, m_w_in, m_conv_w, m_conv_b, m_lru_wa, m_lru_ba, m_lru_wx, m_lru_bx, m_lru_lambda, m_forget_b, m_w_branch_a, m_w_branch_b, m_w_out, m_norm_mlp_g, m_w_up, m_w_down, m_norm_final_g, v_norm_mix_g, v_w_in, v_conv_w, v_conv_b, v_lru_wa, v_lru_ba, v_lru_wx, v_lru_bx, v_lru_lambda, v_forget_b, v_w_branch_a, v_w_branch_b, v_w_out, v_norm_mlp_g, v_w_up, v_w_down, v_norm_final_g):
    weights = dict(norm_mix_g=norm_mix_g, w_in=w_in, conv_w=conv_w, conv_b=conv_b, lru_wa=lru_wa, lru_ba=lru_ba,
                   lru_wx=lru_wx, lru_bx=lru_bx, lru_lambda=lru_lambda, forget_b=forget_b, w_branch_a=w_branch_a,
                   w_branch_b=w_branch_b, w_out=w_out, norm_mlp_g=norm_mlp_g, w_up=w_up, w_down=w_down,
                   norm_final_g=norm_final_g)
    moms = dict(norm_mix_g=m_norm_mix_g, w_in=m_w_in, conv_w=m_conv_w, conv_b=m_conv_b, lru_wa=m_lru_wa,
                lru_ba=m_lru_ba, lru_wx=m_lru_wx, lru_bx=m_lru_bx, lru_lambda=m_lru_lambda, forget_b=m_forget_b,
                w_branch_a=m_w_branch_a, w_branch_b=m_w_branch_b, w_out=m_w_out, norm_mlp_g=m_norm_mlp_g,
                w_up=m_w_up, w_down=m_w_down, norm_final_g=m_norm_final_g)
    vels = dict(norm_mix_g=v_norm_mix_g, w_in=v_w_in, conv_w=v_conv_w, conv_b=v_conv_b, lru_wa=v_lru_wa,
                lru_ba=v_lru_ba, lru_wx=v_lru_wx, lru_bx=v_lru_bx, lru_lambda=v_lru_lambda, forget_b=v_forget_b,
                w_branch_a=v_w_branch_a, w_branch_b=v_w_branch_b, w_out=v_w_out, norm_mlp_g=v_norm_mlp_g,
                w_up=v_w_up, w_down=v_w_down, norm_final_g=v_norm_final_g)
    S, D = x.shape[1], x.shape[2]
    me = 4 * lax.axis_index("x") + 2 * lax.axis_index("y") + lax.axis_index("c")

    gathered = _all_gather([weights[k].astype(BF16) for k in BIG_NAMES] + [conv_w])
    win_g, wa_g, wb_g, wo_g, wup_g, wdn_g, cw_g = gathered
    w_in_full = win_g.transpose(1, 0, 2).reshape(D, -1)
    cuts = (0, 2 * D, 5 * D, 7 * D)
    W = dict(in_xg=w_in_full[:, cuts[0]:cuts[1]], in_qkv=w_in_full[:, cuts[1]:cuts[2]],
             in_gates=w_in_full[:, cuts[2]:cuts[3]],
             in_f=jnp.pad(w_in_full[:, cuts[3]:], ((0, 0), (0, LANES - N_HEADS))),
             branch_a=wa_g.reshape(D, D), branch_b=wb_g.reshape(D, D), out=wo_g.reshape(D, D),
             up=wup_g.transpose(1, 0, 2).reshape(D, D_FF), down=wdn_g.reshape(D_FF, D))
    small = {k: weights[k] for k in SMALL_NAMES}
    small["conv_w"] = cw_g.transpose(1, 0, 2).reshape(CONV_W, D)

    loss_part, grad_x, big_grads, small_grads = _local_step(x.reshape(S, D), loss_target.reshape(S, D), W, small)
    loss = lax.psum(loss_part, MESH_AXES)

    blocks = [_to_dest_blocks(k, big_grads[k]) for k in BIG_NAMES]
    own, got = _reduce_scatter_cores(blocks)
    sums = [_chip_partial_sum(o, g) for o, g in zip(own, got)]
    mine, others = _reduce_scatter_chips([s[0] for s in sums], [s[1] for s in sums])

    small_list = [small_grads[k] for k in SMALL_NAMES] + [small_grads["conv_w"]]
    small_shapes = [a.shape for a in small_list]
    reduced = _unpack_small(_all_reduce_small(_pack_small(small_list)), small_shapes)
    small_reduced = dict(zip(SMALL_NAMES + ("conv_w",), reduced))
    cw_cols = lax.dynamic_slice_in_dim(small_reduced["conv_w"], me * (D // N_DEV), D // N_DEV, axis=1)

    grads, deltas, new_m, new_v = {}, {}, {}, {}
    for k, g_own, g_got in zip(BIG_NAMES, mine, others):
        grads[k], deltas[k], new_m[k], new_v[k] = _adamw(weights[k], moms[k], vels[k], g_own, g_got, "adamw_" + k)
    names = SMALL_NAMES + ("conv_w",)
    packed = [_pack_small([src[k] for k in names]) for src in (weights, moms, vels)]
    g_small = [small_reduced[k] for k in SMALL_NAMES] + [cw_cols]
    upd = _adamw(packed[0], packed[1], packed[2], _pack_small(g_small), None, "adamw_small")
    shapes = [weights[k].shape for k in names]
    for dst, arr in zip((grads, deltas, new_m, new_v), upd):
        dst.update(zip(names, _unpack_small(arr, shapes)))

    return (loss, grad_x.reshape(1, S, D), *[grads[k] for k in WEIGHT_ORDER], *[deltas[k] for k in WEIGHT_ORDER],
            *[new_m[k] for k in WEIGHT_ORDER], *[new_v[k] for k in WEIGHT_ORDER])
```

```python
import functools
import math

import jax
import jax.numpy as jnp
from jax import lax
from jax.experimental import pallas as pl
from jax.experimental.pallas import tpu as pltpu

F32 = jnp.float32
BF16 = jnp.bfloat16

D_MODEL = 1024
N_HEADS = 8
HEAD_DIM = 128
D_FF = 4096
LRU_BLOCKS = 16
LRU_BW = 64
LRU_C = 8.0
CONV_W = 4
RMS_EPS = 1e-6
N_DEV = 8
LANES = 128
SUBLANES = 8
N_GROUPS = D_MODEL // LANES
VMEM_LIMIT_BYTES = 52 * 1024 * 1024
ATTN_SCALE = 1.0 / math.sqrt(HEAD_DIM)
NEG_BIG = -1e30
ADAM_LR = 0.001
ADAM_B1 = 0.9
ADAM_B2 = 0.999
ADAM_EPS = 1e-08
ADAM_WD = 0.01
ADAM_STEP = 10
ATTN_BLOCK = 512
LRU_CHUNK = 256
ROW_TILE = 512
MESH_AXES = ("x", "y", "c")
MESH_ID = pl.DeviceIdType.MESH
ANY = pl.BlockSpec(memory_space=pl.ANY)

NT_DIMS = (((1,), (1,)), ((), ()))
TN_DIMS = (((0,), (0,)), ((), ()))
NN_DIMS = (((1,), (0,)), ((), ()))


def _cparams(*sem):
    return pltpu.CompilerParams(dimension_semantics=sem if sem else None, vmem_limit_bytes=VMEM_LIMIT_BYTES)


def _sigmoid(x):
    return 1.0 / (1.0 + jnp.exp(-x))


def _log1p_pos(e):
    u = 1.0 + e
    return jnp.where(u == 1.0, e, jnp.log(u) * (e / (u - 1.0)))


def _softplus(z):
    return jnp.maximum(z, 0.0) + _log1p_pos(jnp.exp(-jnp.abs(z)))


def _expm1_neg(x):
    series = x * (1.0 + x * 0.5 * (1.0 + x * (1.0 / 3.0) * (1.0 + x * 0.25)))
    return jnp.where(x > -0.03, series, jnp.exp(x) - 1.0)


GELU_C = math.sqrt(2.0 / math.pi)
GELU_K = 0.044715


def _gelu(x):
    return 0.5 * x * (1.0 + jnp.tanh(GELU_C * (x + GELU_K * (x * x * x))))


def _gelu_and_grad(x):
    t = jnp.tanh(GELU_C * (x + GELU_K * (x * x * x)))
    g = 0.5 * x * (1.0 + t)
    dg = 0.5 * (1.0 + t) + 0.5 * x * (1.0 - t * t) * (GELU_C * (1.0 + 3.0 * GELU_K * (x * x)))
    return g, dg


def _mm(pairs, *, ta=False, tb=False, tm, tn, tks, outs, name, epi=None, extra=()):
    n_pairs, n_extra, n_out = len(pairs), len(extra), len(outs)
    tas = list(ta) if isinstance(ta, (list, tuple)) else [ta] * n_pairs
    tbs = list(tb) if isinstance(tb, (list, tuple)) else [tb] * n_pairs
    a0, b0 = pairs[0]
    M = a0.shape[1] if tas[0] else a0.shape[0]
    N = b0.shape[0] if tbs[0] else b0.shape[1]
    tm, tn = min(tm, M), min(tn, N)
    nks, offs = [], []
    for (a, b), tk, pta in zip(pairs, tks, tas):
        K = a.shape[0] if pta else a.shape[1]
        assert K % tk == 0 and M % tm == 0 and N % tn == 0
        offs.append(sum(nks))
        nks.append(K // tk)
    nk_total = sum(nks)
    dims = [(((0 if pta else 1,), (1 if ptb else 0,)), ((), ())) for pta, ptb in zip(tas, tbs)]

    def kmap(off, nk):
        return lambda k: jnp.clip(k - off, 0, nk - 1)

    in_specs, operands = [], []
    for (a, b), tk, off, nk, pta, ptb in zip(pairs, tks, offs, nks, tas, tbs):
        km = kmap(off, nk)
        if pta:
            in_specs.append(pl.BlockSpec((tk, tm), lambda i, j, k, km=km: (km(k), i)))
        else:
            in_specs.append(pl.BlockSpec((tm, tk), lambda i, j, k, km=km: (i, km(k))))
        if ptb:
            in_specs.append(pl.BlockSpec((tn, tk), lambda i, j, k, km=km: (j, km(k))))
        else:
            in_specs.append(pl.BlockSpec((tk, tn), lambda i, j, k, km=km: (km(k), j)))
        operands += [a, b]
    for e in extra:
        in_specs.append(pl.BlockSpec((tm, tn), lambda i, j, k: (i, j)))
        operands.append(e)

    def body(*refs):
        ab = refs[:2 * n_pairs]
        ex = refs[2 * n_pairs:2 * n_pairs + n_extra]
        o = refs[2 * n_pairs + n_extra:2 * n_pairs + n_extra + n_out]
        k = pl.program_id(2)

        def finish(acc):
            res = epi(acc, *[e[...] for e in ex]) if epi is not None else (acc,)
            for r, oref in zip(res, o):
                oref[...] = r.astype(oref.dtype)

        if nk_total == 1:
            finish(lax.dot_general(ab[0][...], ab[1][...], dims[0], preferred_element_type=F32))
            return
        acc = refs[-1]
        for p in range(n_pairs):
            a_ref, b_ref = ab[2 * p], ab[2 * p + 1]

            @pl.when((k >= offs[p]) & (k < offs[p] + nks[p]))
            def _(a_ref=a_ref, b_ref=b_ref, pdims=dims[p]):
                prod = lax.dot_general(a_ref[...], b_ref[...], pdims, preferred_element_type=F32)

                @pl.when(k == 0)
                def _():
                    acc[...] = prod

                @pl.when(k > 0)
                def _():
                    acc[...] += prod

        @pl.when(k == nk_total - 1)
        def _():
            finish(acc[...])

    return pl.pallas_call(
        body,
        name=name,
        grid=(M // tm, N // tn, nk_total),
        in_specs=in_specs,
        out_specs=[pl.BlockSpec((tm, tn), lambda i, j, k: (i, j)) for _ in outs],
        out_shape=[jax.ShapeDtypeStruct((M, N), dt) for dt in outs],
        scratch_shapes=[] if nk_total == 1 else [pltpu.VMEM((tm, tn), F32)],
        compiler_params=_cparams("parallel", "parallel", "arbitrary"),
    )(*operands)


def _norm_fwd(x, g, name):
    S, D = x.shape
    tr = min(ROW_TILE, S)

    def body(x_ref, g_ref, o_ref):
        xv = x_ref[...]
        r = lax.rsqrt(jnp.mean(xv * xv, axis=-1, keepdims=True) + RMS_EPS)
        o_ref[...] = ((xv * r) * g_ref[...]).astype(o_ref.dtype)

    return pl.pallas_call(
        body, name=name, grid=(S // tr,),
        in_specs=[pl.BlockSpec((tr, D), lambda i: (i, 0)), pl.BlockSpec((1, D), lambda i: (0, 0))],
        out_specs=pl.BlockSpec((tr, D), lambda i: (i, 0)),
        out_shape=jax.ShapeDtypeStruct((S, D), BF16),
        compiler_params=_cparams("parallel"),
    )(x, g.reshape(1, D))


def _rms_bwd_rows(dy, xv, g):
    r = lax.rsqrt(jnp.mean(xv * xv, axis=-1, keepdims=True) + RMS_EPS)
    xn = xv * r
    dxn = dy * g
    dx = r * (dxn - xn * jnp.mean(dxn * xn, axis=-1, keepdims=True))
    dg = jnp.sum(dy * xn, axis=0, keepdims=True)
    return dx, dg


def _norm_bwd(dy, x, g, dres, name):
    S, D = x.shape
    tr = min(ROW_TILE, S)

    def body(dy_ref, x_ref, g_ref, dres_ref, dx_ref, dxb_ref, dg_ref):
        dx, dg = _rms_bwd_rows(dy_ref[...], x_ref[...], g_ref[...])
        dx = dres_ref[...] + dx
        dx_ref[...] = dx
        dxb_ref[...] = dx.astype(BF16)

        @pl.when(pl.program_id(0) == 0)
        def _():
            dg_ref[...] = jnp.zeros_like(dg_ref)

        dg_ref[...] += dg

    row = pl.BlockSpec((tr, D), lambda i: (i, 0))
    vec = pl.BlockSpec((1, D), lambda i: (0, 0))
    return pl.pallas_call(
        body, name=name, grid=(S // tr,),
        in_specs=[row, row, vec, row],
        out_specs=[row, row, vec],
        out_shape=[jax.ShapeDtypeStruct((S, D), F32), jax.ShapeDtypeStruct((S, D), BF16),
                   jax.ShapeDtypeStruct((1, D), F32)],
        compiler_params=_cparams("arbitrary"),
    )(dy, x, g.reshape(1, D), dres)


def _final_norm_loss(x2, target, g):
    S, D = x2.shape
    tr = min(ROW_TILE, S)

    def body(x_ref, t_ref, g_ref, loss_ref, dg_ref, dx_ref, dxb_ref):
        xv = x_ref[...]
        gv = g_ref[...]
        r = lax.rsqrt(jnp.mean(xv * xv, axis=-1, keepdims=True) + RMS_EPS)
        y = (xv * r) * gv
        err = y - t_ref[...]
        part = 0.5 * jnp.sum(jnp.mean(err * err, axis=-1, keepdims=True), axis=0, keepdims=True)
        dy = err * (1.0 / D)
        dx, dg = _rms_bwd_rows(dy, xv, gv)
        dx_ref[...] = dx
        dxb_ref[...] = dx.astype(BF16)

        @pl.when(pl.program_id(0) == 0)
        def _():
            dg_ref[...] = jnp.zeros_like(dg_ref)
            loss_ref[...] = jnp.zeros_like(loss_ref)

        dg_ref[...] += dg
        loss_ref[...] += jnp.broadcast_to(part, loss_ref.shape)

    row = pl.BlockSpec((tr, D), lambda i: (i, 0))
    vec = pl.BlockSpec((1, D), lambda i: (0, 0))
    return pl.pallas_call(
        body, name="final_norm_loss", grid=(S // tr,),
        in_specs=[row, row, vec],
        out_specs=[pl.BlockSpec((SUBLANES, LANES), lambda i: (0, 0)), vec, row, row],
        out_shape=[jax.ShapeDtypeStruct((SUBLANES, LANES), F32), jax.ShapeDtypeStruct((1, D), F32),
                   jax.ShapeDtypeStruct((S, D), F32), jax.ShapeDtypeStruct((S, D), BF16)],
        compiler_params=_cparams("arbitrary"),
    )(x2, target, g.reshape(1, D))


def _lru_gates(xa, bd_j, ba_j, bx_j, sp_j):
    z = jnp.dot(xa.astype(BF16), bd_j, preferred_element_type=F32)
    r = _sigmoid(z[:, :LANES] + ba_j)
    ig = _sigmoid(z[:, LANES:] + bx_j)
    log_a = (-LRU_C) * r * sp_j
    a = jnp.exp(log_a)
    mult = jnp.sqrt(-_expm1_neg(2.0 * log_a))
    return r, ig, a, mult


def _conv_rows(xpad, cw_ref, cb_ref, sl, tc):
    out = jnp.broadcast_to(cb_ref[:, sl], (tc, LANES))
    for k in range(CONV_W):
        out = out + xpad[pl.ds(SUBLANES - (CONV_W - 1) + k, tc), sl] * cw_ref[k:k + 1, sl]
    return out


def _lru_fwd(xg, cw, cb, bd, ba, bx, lam):
    S = xg.shape[0]
    D = D_MODEL
    tc = min(LRU_CHUNK, S)
    hb = tc // SUBLANES

    def body(xl_ref, halo_ref, g_ref, cw_ref, cb_ref, bd_ref, ba_ref, bx_ref, lam_ref,
             h_ref, y_ref, xpad, a_s, b_s, carry):
        i = pl.program_id(0)

        @pl.when(i == 0)
        def _():
            carry[...] = jnp.zeros_like(carry)

        xpad[0:SUBLANES, :] = jnp.where(i > 0, halo_ref[...], 0.0)
        xpad[SUBLANES:, :] = xl_ref[...]
        for j in range(N_GROUPS):
            sl = slice(LANES * j, LANES * (j + 1))
            xa = _conv_rows(xpad, cw_ref, cb_ref, sl, tc)
            sp = _softplus(-lam_ref[:, sl])
            _, ig, a, mult = _lru_gates(xa, bd_ref[j], ba_ref[:, sl], bx_ref[:, sl], sp)
            a_s[:, sl] = a
            b_s[:, sl] = mult * (ig * xa)

        row = lax.broadcasted_iota(jnp.int32, (SUBLANES, D), 0)

        def step(t, c):
            o = pl.multiple_of(t * SUBLANES, SUBLANES)
            A = a_s[pl.ds(o, SUBLANES), :]
            B = b_s[pl.ds(o, SUBLANES), :]
            for d in (1, 2, 4):
                keep = row >= d
                a_sh = jnp.where(keep, pltpu.roll(A, d, 0), 1.0)
                b_sh = jnp.where(keep, pltpu.roll(B, d, 0), 0.0)
                B = A * b_sh + B
                A = A * a_sh
            hh = A * c + B
            h_ref[pl.ds(o, SUBLANES), :] = hh
            return jnp.broadcast_to(hh[SUBLANES - 1:SUBLANES, :], (SUBLANES, D))

        carry[...] = lax.fori_loop(0, hb, step, carry[...])
        y_ref[...] = (_gelu(g_ref[...]) * h_ref[...]).astype(BF16)

    row_spec = lambda col: pl.BlockSpec((tc, D), lambda i, col=col: (i, col))
    halo = pl.BlockSpec((SUBLANES, D), lambda i: (jnp.maximum(i * hb - 1, 0), 0))
    full = lambda shape: pl.BlockSpec(shape, lambda i: tuple(0 for _ in shape))
    return pl.pallas_call(
        body, name="lru_fwd", grid=(S // tc,),
        in_specs=[row_spec(0), halo, row_spec(1), full((CONV_W, D)), full((1, D)),
                  full((N_GROUPS, LANES, 2 * LANES)), full((1, D)), full((1, D)), full((1, D))],
        out_specs=[pl.BlockSpec((tc, D), lambda i: (i, 0)), pl.BlockSpec((tc, D), lambda i: (i, 0))],
        out_shape=[jax.ShapeDtypeStruct((S, D), F32), jax.ShapeDtypeStruct((S, D), BF16)],
        scratch_shapes=[pltpu.VMEM((tc + SUBLANES, D), F32), pltpu.VMEM((tc, D), F32),
                        pltpu.VMEM((tc, D), F32), pltpu.VMEM((SUBLANES, D), F32)],
        compiler_params=_cparams("arbitrary"),
    )(xg, xg, xg, cw, cb, bd, ba, bx, lam)


def _lru_bwd(xg, h, dyain, cw, cb, bd, ba, bx, lam):
    S = xg.shape[0]
    D = D_MODEL
    tc = min(LRU_CHUNK, S)
    hb = tc // SUBLANES
    nc = S // tc

    def body(xl_ref, xhalo_ref, g_ref, h_ref, hhalo_ref, dy_ref, cw_ref, cb_ref, bd_ref, ba_ref, bx_ref,
             lam_ref, dxg_ref, dcw_ref, dcb_ref, dba_ref, dbx_ref, dlam_ref, dbd_ref,
             xpad, hpad, a_s, b_s, dh_s, g_s, xa_s, r_s, ig_s, m_s, dxa_pad, carry_e, dxa_head):
        i = pl.program_id(0)
        c = nc - 1 - i

        @pl.when(i == 0)
        def _():
            carry_e[...] = jnp.zeros_like(carry_e)
            dxa_head[...] = jnp.zeros_like(dxa_head)
            for ref in (dcw_ref, dcb_ref, dba_ref, dbx_ref, dlam_ref, dbd_ref):
                ref[...] = jnp.zeros_like(ref)

        xpad[0:SUBLANES, :] = jnp.where(c > 0, xhalo_ref[...], 0.0)
        xpad[SUBLANES:, :] = xl_ref[...]
        hpad[0:SUBLANES, :] = jnp.where(c > 0, hhalo_ref[...], 0.0)
        hpad[SUBLANES:, :] = h_ref[...]

        for j in range(N_GROUPS):
            sl = slice(LANES * j, LANES * (j + 1))
            xa = _conv_rows(xpad, cw_ref, cb_ref, sl, tc)
            sp = _softplus(-lam_ref[:, sl])
            r, ig, a, mult = _lru_gates(xa, bd_ref[j], ba_ref[:, sl], bx_ref[:, sl], sp)
            gl, dgl = _gelu_and_grad(g_ref[:, sl])
            dy = dy_ref[:, sl]
            dh = dy * gl
            dxg_ref[:, D + LANES * j:D + LANES * (j + 1)] = (dy * h_ref[:, sl] * dgl).astype(BF16)
            a_s[:, sl] = a
            b_s[:, sl] = a * dh
            dh_s[:, sl] = dh
            xa_s[:, sl] = xa
            r_s[:, sl] = r
            ig_s[:, sl] = ig
            m_s[:, sl] = mult

        row = lax.broadcasted_iota(jnp.int32, (SUBLANES, D), 0)

        def step(tt, ce):
            o = pl.multiple_of((hb - 1 - tt) * SUBLANES, SUBLANES)
            A = a_s[pl.ds(o, SUBLANES), :]
            B = b_s[pl.ds(o, SUBLANES), :]
            for d in (1, 2, 4):
                keep = row < SUBLANES - d
                a_sh = jnp.where(keep, pltpu.roll(A, SUBLANES - d, 0), 1.0)
                b_sh = jnp.where(keep, pltpu.roll(B, SUBLANES - d, 0), 0.0)
                B = A * b_sh + B
                A = A * a_sh
            e = A * ce + B
            e_next = jnp.where(row < SUBLANES - 1, pltpu.roll(e, SUBLANES - 1, 0), ce)
            g_s[pl.ds(o, SUBLANES), :] = dh_s[pl.ds(o, SUBLANES), :] + e_next
            return jnp.broadcast_to(e[0:1, :], (SUBLANES, D))

        carry_e[...] = lax.fori_loop(0, hb, step, carry_e[...])

        for j in range(N_GROUPS):
            sl = slice(LANES * j, LANES * (j + 1))
            gg = g_s[:, sl]
            xa, r, ig, mult, a = xa_s[:, sl], r_s[:, sl], ig_s[:, sl], m_s[:, sl], a_s[:, sl]
            hprev = hpad[pl.ds(SUBLANES - 1, tc), sl]
            sp = _softplus(-lam_ref[:, sl])
            da = gg * hprev
            dmult = gg * (ig * xa)
            dig = gg * (mult * xa)
            dxa = gg * (mult * ig)
            dla = da * a - dmult * ((a * a) / mult)
            dr = dla * ((-LRU_C) * sp)
            dlam_ref[:, sl] += jnp.sum(dla * r, axis=0, keepdims=True)
            dza = dr * r * (1.0 - r)
            dzx = dig * ig * (1.0 - ig)
            dba_ref[:, sl] += jnp.sum(dza, axis=0, keepdims=True)
            dbx_ref[:, sl] += jnp.sum(dzx, axis=0, keepdims=True)
            dz = jnp.concatenate([dza, dzx], axis=1).astype(BF16)
            dbd_ref[j] += lax.dot_general(xa.astype(BF16), dz, TN_DIMS, preferred_element_type=F32)
            dxa = dxa + lax.dot_general(dz, bd_ref[j], NT_DIMS, preferred_element_type=F32)
            dxa_pad[0:tc, sl] = dxa

        dxa_pad[tc:, :] = dxa_head[...]
        dxa_head[...] = dxa_pad[0:SUBLANES, :]

        for j in range(N_GROUPS):
            sl = slice(LANES * j, LANES * (j + 1))
            dxa = dxa_pad[0:tc, sl]
            dxl = jnp.zeros((tc, LANES), F32)
            for k in range(CONV_W):
                dxl = dxl + dxa_pad[pl.ds(CONV_W - 1 - k, tc), sl] * cw_ref[k:k + 1, sl]
                dcw_ref[k:k + 1, sl] += jnp.sum(
                    dxa * xpad[pl.ds(SUBLANES - (CONV_W - 1) + k, tc), sl], axis=0, keepdims=True)
            dxg_ref[:, sl] = dxl.astype(BF16)
            dcb_ref[:, sl] += jnp.sum(dxa, axis=0, keepdims=True)

        @pl.when(i == nc - 1)
        def _():
            dlam_ref[...] = dlam_ref[...] * (LRU_C * _sigmoid(-lam_ref[...]))

    rev = lambda col: pl.BlockSpec((tc, D), lambda i, col=col: (nc - 1 - i, col))
    halo = pl.BlockSpec((SUBLANES, D), lambda i: (jnp.maximum((nc - 1 - i) * hb - 1, 0), 0))
    full = lambda shape: pl.BlockSpec(shape, lambda i: tuple(0 for _ in shape))
    big = lambda: pltpu.VMEM((tc, D), F32)
    return pl.pallas_call(
        body, name="lru_bwd", grid=(nc,),
        in_specs=[rev(0), halo, rev(1), rev(0), halo, rev(0), full((CONV_W, D)), full((1, D)),
                  full((N_GROUPS, LANES, 2 * LANES)), full((1, D)), full((1, D)), full((1, D))],
        out_specs=[pl.BlockSpec((tc, 2 * D), lambda i: (nc - 1 - i, 0)), full((CONV_W, D)), full((1, D)),
                   full((1, D)), full((1, D)), full((1, D)), full((N_GROUPS, LANES, 2 * LANES))],
        out_shape=[jax.ShapeDtypeStruct((S, 2 * D), BF16), jax.ShapeDtypeStruct((CONV_W, D), F32),
                   jax.ShapeDtypeStruct((1, D), F32), jax.ShapeDtypeStruct((1, D), F32),
                   jax.ShapeDtypeStruct((1, D), F32), jax.ShapeDtypeStruct((1, D), F32),
                   jax.ShapeDtypeStruct((N_GROUPS, LANES, 2 * LANES), F32)],
        scratch_shapes=[pltpu.VMEM((tc + SUBLANES, D), F32), pltpu.VMEM((tc + SUBLANES, D), F32),
                        big(), big(), big(), big(), big(), big(), big(), big(),
                        pltpu.VMEM((tc + SUBLANES, D), F32), pltpu.VMEM((SUBLANES, D), F32),
                        pltpu.VMEM((SUBLANES, D), F32)],
        compiler_params=_cparams("arbitrary"),
    )(xg, xg, xg, h, h, dyain, cw, cb, bd, ba, bx, lam)


def _forget_cumsum(fl, fb):
    S = fl.shape[0]
    tr = min(ROW_TILE, S)
    hb = tr // SUBLANES

    def body(fl_ref, fb_ref, o_ref, lf_s, carry):
        @pl.when(pl.program_id(0) == 0)
        def _():
            carry[...] = jnp.zeros_like(carry)

        lf_s[...] = -_softplus(-(fl_ref[...] + fb_ref[...]))
        row = lax.broadcasted_iota(jnp.int32, (SUBLANES, LANES), 0)

        def step(t, c):
            o = pl.multiple_of(t * SUBLANES, SUBLANES)
            B = lf_s[pl.ds(o, SUBLANES), :]
            for d in (1, 2, 4):
                B = B + jnp.where(row >= d, pltpu.roll(B, d, 0), 0.0)
            B = B + c
            o_ref[pl.ds(o, SUBLANES), :] = B
            return jnp.broadcast_to(B[SUBLANES - 1:SUBLANES, :], (SUBLANES, LANES))

        carry[...] = lax.fori_loop(0, hb, step, carry[...])

    return pl.pallas_call(
        body, name="forget_cumsum", grid=(S // tr,),
        in_specs=[pl.BlockSpec((tr, LANES), lambda i: (i, 0)), pl.BlockSpec((1, LANES), lambda i: (0, 0))],
        out_specs=pl.BlockSpec((tr, LANES), lambda i: (i, 0)),
        out_shape=jax.ShapeDtypeStruct((S, LANES), F32),
        scratch_shapes=[pltpu.VMEM((tr, LANES), F32), pltpu.VMEM((SUBLANES, LANES), F32)],
        compiler_params=_cparams("arbitrary"),
    )(fl, fb)


def _forget_bwd(dF, fl, fb):
    S = fl.shape[0]
    tr = min(ROW_TILE, S)
    hb = tr // SUBLANES
    nc = S // tr

    def body(df_ref, fl_ref, fb_ref, o_ref, dfb_ref, carry):
        @pl.when(pl.program_id(0) == 0)
        def _():
            carry[...] = jnp.zeros_like(carry)
            dfb_ref[...] = jnp.zeros_like(dfb_ref)

        row = lax.broadcasted_iota(jnp.int32, (SUBLANES, LANES), 0)

        def step(tt, carried):
            c, acc = carried
            o = pl.multiple_of((hb - 1 - tt) * SUBLANES, SUBLANES)
            B = df_ref[pl.ds(o, SUBLANES), :]
            for d in (1, 2, 4):
                B = B + jnp.where(row < SUBLANES - d, pltpu.roll(B, SUBLANES - d, 0), 0.0)
            B = B + c
            z = fl_ref[pl.ds(o, SUBLANES), :] + fb_ref[...]
            dz = B * _sigmoid(-z)
            o_ref[pl.ds(o, SUBLANES), :] = dz.astype(BF16)
            return jnp.broadcast_to(B[0:1, :], (SUBLANES, LANES)), acc + dz

        c, acc = lax.fori_loop(0, hb, step, (carry[...], jnp.zeros((SUBLANES, LANES), F32)))
        carry[...] = c
        dfb_ref[...] += jnp.sum(acc, axis=0, keepdims=True)

    rev = pl.BlockSpec((tr, LANES), lambda i: (nc - 1 - i, 0))
    vec = pl.BlockSpec((1, LANES), lambda i: (0, 0))
    return pl.pallas_call(
        body, name="forget_bwd", grid=(nc,),
        in_specs=[rev, rev, vec],
        out_specs=[rev, vec],
        out_shape=[jax.ShapeDtypeStruct((S, LANES), BF16), jax.ShapeDtypeStruct((1, LANES), F32)],
        scratch_shapes=[pltpu.VMEM((SUBLANES, LANES), F32)],
        compiler_params=_cparams("arbitrary"),
    )(dF, fl, fb)


def _causal_keep(bk, bq):
    return lax.broadcasted_iota(jnp.int32, (bk, bq), 0) <= lax.broadcasted_iota(jnp.int32, (bk, bq), 1)


def _attn_fwd(kv, qkv_t, f_row, f_col):
    S = kv.shape[0]
    bq = bk = min(ATTN_BLOCK, S)
    nq, nk = S // bq, S // bk

    def body(k_ref, qt_ref, vt_ref, fq_ref, fk_ref, ot_ref, lse_ref, m_s, l_s, acc_s):
        qi, ki = pl.program_id(1), pl.program_id(2)

        @pl.when(ki == 0)
        def _():
            m_s[...] = jnp.full_like(m_s, NEG_BIG)
            l_s[...] = jnp.zeros_like(l_s)
            acc_s[...] = jnp.zeros_like(acc_s)

        def update(diagonal):
            s = jnp.dot(k_ref[...], qt_ref[...], preferred_element_type=F32) * ATTN_SCALE
            s = s + fq_ref[...] - fk_ref[...]
            if diagonal:
                s = jnp.where(_causal_keep(bk, bq), s, NEG_BIG)
            m_old = m_s[...]
            m_new = jnp.maximum(m_old, jnp.max(s, axis=0, keepdims=True))
            alpha = jnp.exp(m_old - m_new)
            p = jnp.exp(s - m_new)
            l_s[...] = alpha * l_s[...] + jnp.sum(p, axis=0, keepdims=True)
            acc_s[...] = alpha * acc_s[...] + jnp.dot(vt_ref[...], p.astype(BF16), preferred_element_type=F32)
            m_s[...] = m_new

        @pl.when(ki < qi)
        def _():
            update(False)

        @pl.when(ki == qi)
        def _():
            update(True)

        @pl.when(ki == nk - 1)
        def _():
            ot_ref[...] = (acc_s[...] / l_s[...]).astype(BF16)
            lse_ref[...] = m_s[...] + jnp.log(l_s[...])

    kc = lambda qi, ki: jnp.minimum(ki, qi)
    return pl.pallas_call(
        body, name="attn_fwd", grid=(N_HEADS, nq, nk),
        in_specs=[pl.BlockSpec((bk, HEAD_DIM), lambda h, qi, ki: (kc(qi, ki), h)),
                  pl.BlockSpec((HEAD_DIM, bq), lambda h, qi, ki: (h, qi)),
                  pl.BlockSpec((HEAD_DIM, bk), lambda h, qi, ki: (2 * N_HEADS + h, kc(qi, ki))),
                  pl.BlockSpec((None, 1, bq), lambda h, qi, ki: (h, 0, qi)),
                  pl.BlockSpec((None, bk, 1), lambda h, qi, ki: (h, kc(qi, ki), 0))],
        out_specs=[pl.BlockSpec((HEAD_DIM, bq), lambda h, qi, ki: (h, qi)),
                   pl.BlockSpec((None, 1, bq), lambda h, qi, ki: (h, 0, qi))],
        out_shape=[jax.ShapeDtypeStruct((N_HEADS * HEAD_DIM, S), BF16), jax.ShapeDtypeStruct((N_HEADS, 1, S), F32)],
        scratch_shapes=[pltpu.VMEM((1, bq), F32), pltpu.VMEM((1, bq), F32), pltpu.VMEM((HEAD_DIM, bq), F32)],
        compiler_params=_cparams("parallel", "parallel", "arbitrary"),
    )(kv, qkv_t, qkv_t, f_row, f_col)


def _attn_bwd(kv, qkv_t, do_t, o_t, lse, f_row, f_col):
    S = kv.shape[0]
    bq = bk = min(ATTN_BLOCK, S)
    nq, nk = S // bq, S // bk

    def body(k_ref, v_ref, qt_ref, kt_ref, dot_ref, ot_ref, lse_ref, fq_ref, fk_ref,
             dqt_ref, dkt_ref, dvt_ref, dfk_ref, dfq_ref, dq_s, dk_s, dv_s, dfk_s, dfq_s):
        ki, qi = pl.program_id(1), pl.program_id(2)

        @pl.when((ki == 0) & (qi == 0))
        def _():
            dq_s[...] = jnp.zeros_like(dq_s)
            dfq_s[...] = jnp.zeros_like(dfq_s)

        @pl.when(qi == 0)
        def _():
            dk_s[...] = jnp.zeros_like(dk_s)
            dv_s[...] = jnp.zeros_like(dv_s)
            dfk_s[...] = jnp.zeros_like(dfk_s)

        def update(diagonal):
            qt, dot = qt_ref[...], dot_ref[...]
            s = jnp.dot(k_ref[...], qt, preferred_element_type=F32) * ATTN_SCALE
            p = jnp.exp(s + (fq_ref[...] - lse_ref[...]) - fk_ref[...])
            if diagonal:
                p = jnp.where(_causal_keep(bk, bq), p, 0.0)
            dp = jnp.dot(v_ref[...], dot, preferred_element_type=F32)
            delta = jnp.sum(dot.astype(F32) * ot_ref[...].astype(F32), axis=0, keepdims=True)
            ds = p * (dp - delta)
            dsb = ds.astype(BF16)
            dv_s[...] += lax.dot_general(dot, p.astype(BF16), NT_DIMS, preferred_element_type=F32)
            dk_s[...] += lax.dot_general(qt, dsb, NT_DIMS, preferred_element_type=F32)
            dq_s[qi] += jnp.dot(kt_ref[...], dsb, preferred_element_type=F32)
            lane_part = ds[:, 0:LANES]
            for blk in range(1, bq // LANES):
                lane_part = lane_part + ds[:, LANES * blk:LANES * (blk + 1)]
            dfk_s[...] += lane_part
            sub_part = ds[0:SUBLANES, :]
            for blk in range(1, bk // SUBLANES):
                sub_part = sub_part + ds[SUBLANES * blk:SUBLANES * (blk + 1), :]
            dfq_s[qi] += sub_part

        @pl.when(qi > ki)
        def _():
            update(False)

        @pl.when(qi == ki)
        def _():
            update(True)

        @pl.when(qi == nq - 1)
        def _():
            dkt_ref[...] = (dk_s[...] * ATTN_SCALE).astype(BF16)
            dvt_ref[...] = dv_s[...].astype(BF16)
            dfk_ref[...] = -jnp.sum(dfk_s[...], axis=-1, keepdims=True)

        for j in range(nq):
            @pl.when((ki == nk - 1) & (qi == j))
            def _(j=j):
                dqt_ref[:, bq * j:bq * (j + 1)] = (dq_s[j] * ATTN_SCALE).astype(BF16)
                dfq_ref[:, bq * j:bq * (j + 1)] = jnp.sum(dfq_s[j], axis=0, keepdims=True)

    qc = lambda ki, qi: jnp.maximum(qi, ki)
    q_feat = pl.BlockSpec((HEAD_DIM, bq), lambda h, ki, qi: (h, qc(ki, qi)))
    q_row = pl.BlockSpec((None, 1, bq), lambda h, ki, qi: (h, 0, qc(ki, qi)))
    k_feat = pl.BlockSpec((HEAD_DIM, bk), lambda h, ki, qi: (h, ki))
    return pl.pallas_call(
        body, name="attn_bwd", grid=(N_HEADS, nk, nq),
        in_specs=[pl.BlockSpec((bk, HEAD_DIM), lambda h, ki, qi: (ki, h)),
                  pl.BlockSpec((bk, HEAD_DIM), lambda h, ki, qi: (ki, N_HEADS + h)),
                  q_feat,
                  pl.BlockSpec((HEAD_DIM, bk), lambda h, ki, qi: (N_HEADS + h, ki)),
                  q_feat, q_feat, q_row, q_row,
                  pl.BlockSpec((None, bk, 1), lambda h, ki, qi: (h, ki, 0))],
        out_specs=[pl.BlockSpec((HEAD_DIM, S), lambda h, ki, qi: (h, 0)), k_feat, k_feat,
                   pl.BlockSpec((None, bk, 1), lambda h, ki, qi: (h, ki, 0)),
                   pl.BlockSpec((None, 1, S), lambda h, ki, qi: (h, 0, 0))],
        out_shape=[jax.ShapeDtypeStruct((N_HEADS * HEAD_DIM, S), BF16)] * 3
        + [jax.ShapeDtypeStruct((N_HEADS, S, 1), F32), jax.ShapeDtypeStruct((N_HEADS, 1, S), F32)],
        scratch_shapes=[pltpu.VMEM((nq, HEAD_DIM, bq), F32), pltpu.VMEM((HEAD_DIM, bk), F32),
                        pltpu.VMEM((HEAD_DIM, bk), F32), pltpu.VMEM((bk, LANES), F32),
                        pltpu.VMEM((nq, SUBLANES, bq), F32)],
        compiler_params=_cparams("parallel", "arbitrary", "arbitrary"),
    )(kv, kv, qkv_t, qkv_t, do_t, o_t, lse, f_row, f_col)


def _gate_mix(gates, ya, yb):
    S, D = ya.shape
    tr = min(ROW_TILE, S)

    def body(ga_ref, gb_ref, ya_ref, yb_ref, o_ref):
        o_ref[...] = (_sigmoid(ga_ref[...]) * ya_ref[...] + _sigmoid(gb_ref[...]) * yb_ref[...]).astype(BF16)

    col = lambda j: pl.BlockSpec((tr, D), lambda i, j=j: (i, j))
    return pl.pallas_call(
        body, name="gate_mix", grid=(S // tr,),
        in_specs=[col(0), col(1), col(0), col(0)],
        out_specs=col(0),
        out_shape=jax.ShapeDtypeStruct((S, D), BF16),
        compiler_params=_cparams("parallel"),
    )(gates, gates, ya, yb)


def _gate_bwd(dmix, gates, ya, yb):
    S, D = ya.shape
    tr = min(ROW_TILE, S)

    def body(dm_ref, ga_ref, gb_ref, ya_ref, yb_ref, dya_ref, dyb_ref, dg_ref):
        dm = dm_ref[...]
        sa, sb = _sigmoid(ga_ref[...]), _sigmoid(gb_ref[...])
        dya_ref[...] = (dm * sa).astype(BF16)
        dyb_ref[...] = (dm * sb).astype(BF16)
        dg_ref[:, 0:D] = ((dm * ya_ref[...]) * (sa * (1.0 - sa))).astype(BF16)
        dg_ref[:, D:] = ((dm * yb_ref[...]) * (sb * (1.0 - sb))).astype(BF16)

    col = lambda j: pl.BlockSpec((tr, D), lambda i, j=j: (i, j))
    return pl.pallas_call(
        body, name="gate_bwd", grid=(S // tr,),
        in_specs=[col(0), col(0), col(1), col(0), col(0)],
        out_specs=[col(0), col(0), pl.BlockSpec((tr, 2 * D), lambda i: (i, 0))],
        out_shape=[jax.ShapeDtypeStruct((S, D), BF16), jax.ShapeDtypeStruct((S, D), BF16),
                   jax.ShapeDtypeStruct((S, 2 * D), BF16)],
        compiler_params=_cparams("parallel"),
    )(dmix, gates, gates, ya, yb)


def _mesh_place():
    x, y, c = lax.axis_index("x"), lax.axis_index("y"), lax.axis_index("c")
    chips = [(1 - x, y), (x, 1 - y), (1 - x, 1 - y)]
    return x, y, c, chips


def _all_gather(shards):
    n = len(shards)

    def body(*refs):
        ins, outs = refs[:n], refs[n:2 * n]
        send_sems, recv_sems, local_sems = refs[2 * n:]
        x, y, c, chips = _mesh_place()
        me, sib = (x, y, c), (x, y, 1 - c)

        def copy(a, k, block, to, src=None):
            px, py, pc = block
            dst = outs[a].at[4 * px + 2 * py + pc]
            return pltpu.make_async_remote_copy(
                src_ref=dst if src is None else src, dst_ref=dst,
                send_sem=send_sems.at[a, k], recv_sem=recv_sems.at[a, k],
                device_id=to, device_id_type=MESH_ID)

        mine = [pltpu.make_async_copy(ins[a], outs[a].at[4 * x + 2 * y + c], local_sems.at[a]) for a in range(n)]
        for cp in mine:
            cp.start()
        first = []
        for a in range(n):
            first.append(copy(a, 0, me, sib, src=ins[a]))
            for j, chip in enumerate(chips):
                first.append(copy(a, 1 + j, me, (*chip, c), src=ins[a]))
        for cp in first:
            cp.start()
        passed = []
        for j, chip in enumerate(chips):
            for a in range(n):
                copy(a, 1 + j, (*chip, c), me).wait_recv()
                fwd = copy(a, 4 + j, (*chip, c), sib)
                fwd.start()
                passed.append(fwd)
        for a in range(n):
            copy(a, 0, sib, me).wait_recv()
            for j, chip in enumerate(chips):
                copy(a, 4 + j, (*chip, 1 - c), me).wait_recv()
        for cp in first + passed:
            cp.wait_send()
        for cp in mine:
            cp.wait()

    return pl.pallas_call(
        body, name="all_gather_weights",
        in_specs=[ANY] * n, out_specs=[ANY] * n,
        out_shape=[jax.ShapeDtypeStruct((N_DEV,) + s.shape, s.dtype) for s in shards],
        scratch_shapes=[pltpu.SemaphoreType.DMA((n, 7)), pltpu.SemaphoreType.DMA((n, 7)),
                        pltpu.SemaphoreType.DMA((n,))],
    )(*shards)


def _reduce_scatter_cores(grads):
    n = len(grads)

    def body(*refs):
        ins, gots = refs[:n], refs[n:2 * n]
        send_sems, recv_sems = refs[2 * n:]
        x, y, c, _ = _mesh_place()
        sib = (x, y, 1 - c)
        remote = []
        for a in range(n):
            for k in range(4):
                remote.append(pltpu.make_async_remote_copy(
                    src_ref=ins[a].at[2 * k + (1 - c)], dst_ref=gots[a].at[k],
                    send_sem=send_sems.at[a, k], recv_sem=recv_sems.at[a, k],
                    device_id=sib, device_id_type=MESH_ID))
        for cp in remote:
            cp.start()
        for cp in remote:
            cp.wait_recv()
        for cp in remote:
            cp.wait_send()

    return pl.pallas_call(
        body, name="reduce_scatter_cores",
        in_specs=[ANY] * n, out_specs=[ANY] * n,
        out_shape=[jax.ShapeDtypeStruct((4,) + g.shape[1:], g.dtype) for g in grads],
        scratch_shapes=[pltpu.SemaphoreType.DMA((n, 4)), pltpu.SemaphoreType.DMA((n, 4))],
    )(*grads)


def _chip_partial_sum(blocks, got, core):
    R, C = got.shape[1:]
    tr = min(256, R)
    assert R % tr == 0

    def body(core_ref, a_ref, b_ref, s_ref, sb_ref):
        s = a_ref[...] + b_ref[...]
        s_ref[...] = s
        sb_ref[...] = s.astype(BF16)

    blk = pl.BlockSpec((None, tr, C), lambda k, i, core_ref: (k, i, 0))
    return pl.pallas_call(
        body, name="chip_partial_sum",
        grid_spec=pltpu.PrefetchScalarGridSpec(
            num_scalar_prefetch=1, grid=(4, R // tr),
            in_specs=[pl.BlockSpec((None, tr, C), lambda k, i, core_ref: (2 * k + core_ref[0], i, 0)), blk],
            out_specs=[blk, blk]),
        out_shape=[jax.ShapeDtypeStruct(got.shape, F32), jax.ShapeDtypeStruct(got.shape, BF16)],
        compiler_params=_cparams("parallel", "parallel"),
    )(core, blocks, got)


def _reduce_scatter_chips(sums_bf16):
    n = len(sums_bf16)

    def body(*refs):
        bf16s, gots = refs[:n], refs[n:2 * n]
        send_sems, recv_sems = refs[2 * n:]
        x, y, c, chips = _mesh_place()
        remote = []
        for a in range(n):
            for j, (px, py) in enumerate(chips):
                remote.append(pltpu.make_async_remote_copy(
                    src_ref=bf16s[a].at[2 * px + py], dst_ref=gots[a].at[j],
                    send_sem=send_sems.at[a, j], recv_sem=recv_sems.at[a, j],
                    device_id=(px, py, c), device_id_type=MESH_ID))
        for cp in remote:
            cp.start()
        for cp in remote:
            cp.wait_recv()
        for cp in remote:
            cp.wait_send()

    return pl.pallas_call(
        body, name="reduce_scatter_chips",
        in_specs=[ANY] * n, out_specs=[ANY] * n,
        out_shape=[jax.ShapeDtypeStruct((3,) + s.shape[1:], BF16) for s in sums_bf16],
        scratch_shapes=[pltpu.SemaphoreType.DMA((n, 3)), pltpu.SemaphoreType.DMA((n, 3))],
    )(*sums_bf16)


def _all_reduce_small(vec):
    R = vec.shape[0]

    def body(v_ref, o_ref, sib_buf, chip_buf, send_sems, recv_sems):
        x, y, c, chips = _mesh_place()
        swap = pltpu.make_async_remote_copy(
            src_ref=v_ref, dst_ref=sib_buf, send_sem=send_sems.at[0], recv_sem=recv_sems.at[0],
            device_id=(x, y, 1 - c), device_id_type=MESH_ID)
        swap.start()
        swap.wait()
        my_chip = 2 * x + y
        chip_buf[my_chip] = v_ref[...] + sib_buf[...]
        sends = []
        for j, (px, py) in enumerate(chips):
            cp = pltpu.make_async_remote_copy(
                src_ref=chip_buf.at[my_chip], dst_ref=chip_buf.at[my_chip],
                send_sem=send_sems.at[1 + j], recv_sem=recv_sems.at[1 + j],
                device_id=(px, py, c), device_id_type=MESH_ID)
            cp.start()
            sends.append(cp)
        for j, (px, py) in enumerate(chips):
            pltpu.make_async_remote_copy(
                src_ref=chip_buf.at[2 * px + py], dst_ref=chip_buf.at[2 * px + py],
                send_sem=send_sems.at[1 + j], recv_sem=recv_sems.at[1 + j],
                device_id=(px, py, c), device_id_type=MESH_ID).wait_recv()
        for cp in sends:
            cp.wait_send()
        o_ref[...] = ((chip_buf[0] + chip_buf[1]) + chip_buf[2]) + chip_buf[3]

    vm = pl.BlockSpec(memory_space=pltpu.VMEM)
    return pl.pallas_call(
        body, name="all_reduce_small",
        in_specs=[vm], out_specs=vm,
        out_shape=jax.ShapeDtypeStruct(vec.shape, F32),
        scratch_shapes=[pltpu.VMEM((R, LANES), F32), pltpu.VMEM((4, R, LANES), F32),
                        pltpu.SemaphoreType.DMA((4,)), pltpu.SemaphoreType.DMA((4,))],
    )(vec)


def _adamw_math(w, g, m, v):
    m = ADAM_B1 * m + (1.0 - ADAM_B1) * g
    v = ADAM_B2 * v + (1.0 - ADAM_B2) * (g * g)
    m_hat = m / (1.0 - ADAM_B1 ** ADAM_STEP)
    v_hat = v / (1.0 - ADAM_B2 ** ADAM_STEP)
    delta = -ADAM_LR * (m_hat / (jnp.sqrt(v_hat) + ADAM_EPS) + ADAM_WD * w)
    return delta, m, v


def _adamw(w, m, v, g_own, g_got, chip, name):
    R, C = w.shape
    tr = R if R * C <= 256 * D_MODEL else 256
    assert R % tr == 0
    n_got = 0 if g_got is None else 3

    def body(*refs):
        w_ref, m_ref, v_ref, go_ref = refs[1:5]
        got = refs[5:5 + n_got]
        g_ref, d_ref, nm_ref, nv_ref = refs[5 + n_got:]
        g = go_ref[...]
        for r in got:
            g = g + r[...].astype(F32)
        delta, m_new, v_new = _adamw_math(w_ref[...], g, m_ref[...], v_ref[...])
        g_ref[...] = g
        d_ref[...] = delta
        nm_ref[...] = m_new
        nv_ref[...] = v_new

    blk = pl.BlockSpec((tr, C), lambda i, chip_ref: (i, 0))
    own_spec = pl.BlockSpec((None, tr, C), lambda i, chip_ref: (chip_ref[0], i, 0))
    got_specs = [pl.BlockSpec((None, tr, C), lambda i, chip_ref, j=j: (j, i, 0)) for j in range(n_got)]
    return pl.pallas_call(
        body, name=name,
        grid_spec=pltpu.PrefetchScalarGridSpec(
            num_scalar_prefetch=1, grid=(R // tr,),
            in_specs=[blk] * 3 + [own_spec] + got_specs, out_specs=[blk] * 4),
        out_shape=[jax.ShapeDtypeStruct((R, C), F32)] * 4,
        compiler_params=_cparams("parallel"),
    )(chip, w, m, v, g_own, *([g_got] * n_got))


def _block_diag_pairs(wa, wx):
    def pairs(w):
        w = w.reshape(N_GROUPS, 2, LRU_BW, LRU_BW)
        z = jnp.zeros((N_GROUPS, LRU_BW, LRU_BW), w.dtype)
        top = jnp.concatenate([w[:, 0], z], axis=2)
        bot = jnp.concatenate([z, w[:, 1]], axis=2)
        return jnp.concatenate([top, bot], axis=1)
    return jnp.concatenate([pairs(wa), pairs(wx)], axis=2).astype(BF16)


def _block_diag_unpair(dbd):
    def unpair(g):
        blocks = jnp.stack([g[:, :LRU_BW, :LRU_BW], g[:, LRU_BW:, LRU_BW:]], axis=1)
        return blocks.reshape(LRU_BLOCKS, LRU_BW, LRU_BW)
    return unpair(dbd[:, :, :LANES]), unpair(dbd[:, :, LANES:])


def _local_step(x, target, W, small):
    S, D = x.shape
    g1, g2, g3 = small["norm_mix_g"], small["norm_mlp_g"], small["norm_final_g"]
    cw, cb = small["conv_w"], small["conv_b"].reshape(1, D)
    ba, bx, lam = (small[k].reshape(1, D) for k in ("lru_ba", "lru_bx", "lru_lambda"))
    fb = jnp.pad(small["forget_b"], (0, LANES - N_HEADS)).reshape(1, LANES)
    bd = _block_diag_pairs(small["lru_wa"], small["lru_wx"])
    big = dict(tm=1024, tn=1024)

    u = _norm_fwd(x, g1, "norm_mix")
    (xg,) = _mm([(u, W["in_xg"])], tks=[D], outs=[F32], name="proj_xg", **big)
    (qkv_t,) = _mm([(W["in_qkv_t"], u)], tb=True, tks=[D], outs=[BF16], name="proj_qkv_t", **big)
    (kv,) = _mm([(u, W["in_kv"])], tks=[D], outs=[BF16], name="proj_kv", **big)
    (gates,) = _mm([(u, W["in_gates"])], tks=[D], outs=[F32], name="proj_gates", **big)
    (fl,) = _mm([(u, W["in_f"])], tks=[D], outs=[F32], name="proj_forget", **big)
    h, yain = _lru_fwd(xg, cw, cb, bd, ba, bx, lam)
    fcum = _forget_cumsum(fl, fb)
    f_heads = fcum[:, :N_HEADS].T
    f_row, f_col = f_heads.reshape(N_HEADS, 1, S), f_heads.reshape(N_HEADS, S, 1)
    ob_t, lse = _attn_fwd(kv, qkv_t, f_row, f_col)
    (ya,) = _mm([(yain, W["branch_a"])], tks=[D], outs=[F32], name="branch_a", **big)
    (yb,) = _mm([(ob_t, W["branch_b"])], ta=True, tks=[D], outs=[F32], name="branch_b", **big)
    mix = _gate_mix(gates, ya, yb)
    (x1,) = _mm([(mix, W["out"])], tks=[D], outs=[F32], name="out_proj", extra=(x,),
                epi=lambda acc, res: (res + acc,), **big)
    m = _norm_fwd(x1, g2, "norm_mlp")
    relu, hh = _mm([(m, W["up"])], tks=[D], outs=[BF16, BF16], name="mlp_up",
                   epi=lambda acc: (jnp.maximum(acc, 0.0), jnp.square(jnp.maximum(acc, 0.0))), **big)
    (x2,) = _mm([(hh, W["down"])], tks=[1024], outs=[F32], name="mlp_down", extra=(x1,),
                epi=lambda acc, res: (res + acc,), **big)
    loss_acc, dg3, dx2, dx2b = _final_norm_loss(x2, target, g3)

    (dhpre,) = _mm([(dx2b, W["down"])], tb=True, tks=[D], outs=[BF16], name="d_mlp_act", extra=(relu,),
                   epi=lambda acc, r: (acc * (2.0 * r.astype(F32)),), **big)
    (dw_down,) = _mm([(hh, dx2b)], ta=True, tks=[min(1024, S)], outs=[F32], name="dw_down", **big)
    (dm,) = _mm([(dhpre, W["up"])], tb=True, tks=[1024], outs=[F32], name="d_mlp_in", **big)
    (dw_up,) = _mm([(m, dhpre)], ta=True, tks=[min(1024, S)], outs=[F32], name="dw_up", **big)
    dx1, dx1b, dg2 = _norm_bwd(dm, x1, g2, dx2, "norm_mlp_bwd")
    (dmix,) = _mm([(dx1b, W["out"])], tb=True, tks=[D], outs=[F32], name="d_mix", **big)
    (dw_out,) = _mm([(mix, dx1b)], ta=True, tks=[min(1024, S)], outs=[F32], name="dw_out", **big)
    dya, dyb, dgates = _gate_bwd(dmix, gates, ya, yb)
    (dob_t,) = _mm([(W["branch_b"], dyb)], tb=True, tks=[D], outs=[BF16], name="d_attn_out_t", **big)
    (dw_b,) = _mm([(ob_t, dyb)], tks=[min(1024, S)], outs=[F32], name="dw_branch_b", **big)
    (dyain,) = _mm([(dya, W["branch_a"])], tb=True, tks=[D], outs=[F32], name="d_lru_out", **big)
    (dw_a,) = _mm([(yain, dya)], ta=True, tks=[min(1024, S)], outs=[F32], name="dw_branch_a", **big)
    dq_t, dk_t, dv_t, dfk, dfq = _attn_bwd(kv, qkv_t, dob_t, ob_t, lse, f_row, f_col)
    dF = jnp.pad((dfk.reshape(N_HEADS, S) + dfq.reshape(N_HEADS, S)).T, ((0, 0), (0, LANES - N_HEADS)))
    dfl, dfb = _forget_bwd(dF, fl, fb)
    dxg, dcw, dcb, dba, dbx, dlam, dbd = _lru_bwd(xg, h, dyain, cw, cb, bd, ba, bx, lam)
    wq_t, wk_t, wv_t = (W["in_qkv_t"][D * i:D * (i + 1)] for i in range(3))
    (du,) = _mm([(dxg, W["in_xg"]), (dq_t, wq_t), (dk_t, wk_t), (dv_t, wv_t), (dgates, W["in_gates"]),
                 (dfl, W["in_f"])],
                ta=[False, True, True, True, False, False], tb=[True, False, False, False, True, True],
                tks=[1024, D, D, D, 1024, LANES], outs=[F32], name="d_norm_mix_out", tm=1024, tn=512)
    tks = [min(1024, S)]
    dw_in_parts = [
        _mm([(u, dxg)], ta=True, tks=tks, outs=[F32], name="dw_in_xg", **big)[0],
        _mm([(dq_t, u)], tks=tks, outs=[F32], name="dw_in_q_t", **big)[0].T,
        _mm([(dk_t, u)], tks=tks, outs=[F32], name="dw_in_k_t", **big)[0].T,
        _mm([(dv_t, u)], tks=tks, outs=[F32], name="dw_in_v_t", **big)[0].T,
        _mm([(u, dgates)], ta=True, tks=tks, outs=[F32], name="dw_in_gates", **big)[0],
        _mm([(u, dfl)], ta=True, tks=tks, outs=[F32], name="dw_in_forget", **big)[0][:, :N_HEADS],
    ]
    grad_x, _, dg1 = _norm_bwd(du, x, g1, dx1, "norm_mix_bwd")

    dwa, dwx = _block_diag_unpair(dbd)
    big_grads = dict(w_in=jnp.concatenate(dw_in_parts, axis=1), w_branch_a=dw_a, w_branch_b=dw_b, w_out=dw_out,
                     w_up=dw_up, w_down=dw_down)
    small_grads = dict(norm_mix_g=dg1.reshape(D), conv_w=dcw, conv_b=dcb.reshape(D), lru_wa=dwa, lru_ba=dba.reshape(D),
                       lru_wx=dwx, lru_bx=dbx.reshape(D), lru_lambda=dlam.reshape(D), forget_b=dfb[0, :N_HEADS],
                       norm_mlp_g=dg2.reshape(D), norm_final_g=dg3.reshape(D))
    return loss_acc[0, 0], grad_x, big_grads, small_grads


SMALL_NAMES = ("norm_mix_g", "conv_b", "lru_wa", "lru_ba", "lru_wx", "lru_bx", "lru_lambda", "forget_b",
               "norm_mlp_g", "norm_final_g")
TILE_ELEMS = SUBLANES * LANES


def _pack_small(parts):
    rows = []
    for p in parts:
        flat = p.reshape(-1)
        flat = jnp.pad(flat, (0, (-flat.shape[0]) % TILE_ELEMS))
        rows.append(flat.reshape(-1, LANES))
    return jnp.concatenate(rows, axis=0)


def _unpack_small(packed, shapes):
    out, r = [], 0
    for shp in shapes:
        size = math.prod(shp)
        nrows = -(-size // TILE_ELEMS) * SUBLANES
        out.append(packed[r:r + nrows].reshape(-1)[:size].reshape(shp))
        r += nrows
    return out


BIG_NAMES = ("w_in", "w_branch_a", "w_branch_b", "w_out", "w_up", "w_down")
WEIGHT_ORDER = ("norm_mix_g", "w_in", "conv_w", "conv_b", "lru_wa", "lru_ba", "lru_wx", "lru_bx", "lru_lambda",
                "forget_b", "w_branch_a", "w_branch_b", "w_out", "norm_mlp_g", "w_up", "w_down", "norm_final_g")


def _to_dest_blocks(name, g):
    if name in ("w_in", "w_up"):
        return g.reshape(g.shape[0], N_DEV, g.shape[1] // N_DEV).transpose(1, 0, 2)
    return g.reshape(N_DEV, g.shape[0] // N_DEV, g.shape[1])


def kernel(x, norm_mix_g, w_in, conv_w, conv_b, lru_wa, lru_ba, lru_wx, lru_bx, lru_lambda, forget_b, w_branch_a, w_branch_b, w_out, norm_mlp_g, w_up, w_down, norm_final_g, loss_target, m_norm_mix_g, m_w_in, m_conv_w, m_conv_b, m_lru_wa, m_lru_ba, m_lru_wx, m_lru_bx, m_lru_lambda, m_forget_b, m_w_branch_a, m_w_branch_b, m_w_out, m_norm_mlp_g, m_w_up, m_w_down, m_norm_final_g, v_norm_mix_g, v_w_in, v_conv_w, v_conv_b, v_lru_wa, v_lru_ba, v_lru_wx, v_lru_bx, v_lru_lambda, v_forget_b, v_w_branch_a, v_w_branch_b, v_w_out, v_norm_mlp_g, v_w_up, v_w_down, v_norm_final_g):
    weights = dict(norm_mix_g=norm_mix_g, w_in=w_in, conv_w=conv_w, conv_b=conv_b, lru_wa=lru_wa, lru_ba=lru_ba,
                   lru_wx=lru_wx, lru_bx=lru_bx, lru_lambda=lru_lambda, forget_b=forget_b, w_branch_a=w_branch_a,
                   w_branch_b=w_branch_b, w_out=w_out, norm_mlp_g=norm_mlp_g, w_up=w_up, w_down=w_down,
                   norm_final_g=norm_final_g)
    moms = dict(norm_mix_g=m_norm_mix_g, w_in=m_w_in, conv_w=m_conv_w, conv_b=m_conv_b, lru_wa=m_lru_wa,
                lru_ba=m_lru_ba, lru_wx=m_lru_wx, lru_bx=m_lru_bx, lru_lambda=m_lru_lambda, forget_b=m_forget_b,
                w_branch_a=m_w_branch_a, w_branch_b=m_w_branch_b, w_out=m_w_out, norm_mlp_g=m_norm_mlp_g,
                w_up=m_w_up, w_down=m_w_down, norm_final_g=m_norm_final_g)
    vels = dict(norm_mix_g=v_norm_mix_g, w_in=v_w_in, conv_w=v_conv_w, conv_b=v_conv_b, lru_wa=v_lru_wa,
                lru_ba=v_lru_ba, lru_wx=v_lru_wx, lru_bx=v_lru_bx, lru_lambda=v_lru_lambda, forget_b=v_forget_b,
                w_branch_a=v_w_branch_a, w_branch_b=v_w_branch_b, w_out=v_w_out, norm_mlp_g=v_norm_mlp_g,
                w_up=v_w_up, w_down=v_w_down, norm_final_g=v_norm_final_g)
    S, D = x.shape[1], x.shape[2]
    me = 4 * lax.axis_index("x") + 2 * lax.axis_index("y") + lax.axis_index("c")

    gathered = _all_gather([weights[k].astype(BF16) for k in BIG_NAMES] + [conv_w])
    win_g, wa_g, wb_g, wo_g, wup_g, wdn_g, cw_g = gathered
    w_in_full = win_g.transpose(1, 0, 2).reshape(D, -1)
    cuts = (0, 2 * D, 5 * D, 7 * D)
    W = dict(in_xg=w_in_full[:, cuts[0]:cuts[1]], in_qkv_t=w_in_full[:, cuts[1]:cuts[2]].T,
             in_kv=w_in_full[:, cuts[1] + D:cuts[2]], in_gates=w_in_full[:, cuts[2]:cuts[3]],
             in_f=jnp.pad(w_in_full[:, cuts[3]:], ((0, 0), (0, LANES - N_HEADS))),
             branch_a=wa_g.reshape(D, D), branch_b=wb_g.reshape(D, D), out=wo_g.reshape(D, D),
             up=wup_g.transpose(1, 0, 2).reshape(D, D_FF), down=wdn_g.reshape(D_FF, D))
    small = {k: weights[k] for k in SMALL_NAMES}
    small["conv_w"] = cw_g.transpose(1, 0, 2).reshape(CONV_W, D)

    loss_part, grad_x, big_grads, small_grads = _local_step(x.reshape(S, D), loss_target.reshape(S, D), W, small)
    loss = lax.psum(loss_part, MESH_AXES)

    core = lax.axis_index("c").astype(jnp.int32).reshape(1)
    chip = (2 * lax.axis_index("x") + lax.axis_index("y")).astype(jnp.int32).reshape(1)
    blocks = [_to_dest_blocks(k, big_grads[k]) for k in BIG_NAMES]
    got = _reduce_scatter_cores(blocks)
    sums = [_chip_partial_sum(b, g, core) for b, g in zip(blocks, got)]
    others = _reduce_scatter_chips([s[1] for s in sums])

    small_list = [small_grads[k] for k in SMALL_NAMES] + [small_grads["conv_w"]]
    small_shapes = [a.shape for a in small_list]
    reduced = _unpack_small(_all_reduce_small(_pack_small(small_list)), small_shapes)
    small_reduced = dict(zip(SMALL_NAMES + ("conv_w",), reduced))
    cw_cols = lax.dynamic_slice_in_dim(small_reduced["conv_w"], me * (D // N_DEV), D // N_DEV, axis=1)

    grads, deltas, new_m, new_v = {}, {}, {}, {}
    for k, s, g_got in zip(BIG_NAMES, sums, others):
        grads[k], deltas[k], new_m[k], new_v[k] = _adamw(weights[k], moms[k], vels[k], s[0], g_got, chip, "adamw_" + k)
    names = SMALL_NAMES + ("conv_w",)
    packed = [_pack_small([src[k] for k in names]) for src in (weights, moms, vels)]
    g_small = _pack_small([small_reduced[k] for k in SMALL_NAMES] + [cw_cols])
    upd = _adamw(packed[0], packed[1], packed[2], g_small[None], None, jnp.zeros((1,), jnp.int32), "adamw_small")
    shapes = [weights[k].shape for k in names]
    for dst, arr in zip((grads, deltas, new_m, new_v), upd):
        dst.update(zip(names, _unpack_small(arr, shapes)))

    return (loss, grad_x.reshape(1, S, D), *[grads[k] for k in WEIGHT_ORDER], *[deltas[k] for k in WEIGHT_ORDER],
            *[new_m[k] for k in WEIGHT_ORDER], *[new_v[k] for k in WEIGHT_ORDER])
```

```python
import functools
import math

import jax
import jax.numpy as jnp
from jax import lax
from jax.experimental import pallas as pl
from jax.experimental.pallas import tpu as pltpu

F32 = jnp.float32
BF16 = jnp.bfloat16

D_MODEL = 1024
N_HEADS = 8
HEAD_DIM = 128
D_FF = 4096
LRU_BLOCKS = 16
LRU_BW = 64
LRU_C = 8.0
CONV_W = 4
RMS_EPS = 1e-6
N_DEV = 8
LANES = 128
SUBLANES = 8
N_GROUPS = D_MODEL // LANES
VMEM_LIMIT_BYTES = 52 * 1024 * 1024
ATTN_SCALE = 1.0 / math.sqrt(HEAD_DIM)
NEG_BIG = -1e30
ADAM_LR = 0.001
ADAM_B1 = 0.9
ADAM_B2 = 0.999
ADAM_EPS = 1e-08
ADAM_WD = 0.01
ADAM_STEP = 10
ATTN_BLOCK = 1024
ATTN_STRIP = 256
LRU_CHUNK = 256
ROW_TILE = 512
MESH_AXES = ("x", "y", "c")
MESH_ID = pl.DeviceIdType.MESH
ANY = pl.BlockSpec(memory_space=pl.ANY)

NT_DIMS = (((1,), (1,)), ((), ()))
TN_DIMS = (((0,), (0,)), ((), ()))
NN_DIMS = (((1,), (0,)), ((), ()))


def _cparams(*sem):
    return pltpu.CompilerParams(dimension_semantics=sem if sem else None, vmem_limit_bytes=VMEM_LIMIT_BYTES)


def _sigmoid(x):
    return 1.0 / (1.0 + jnp.exp(-x))


def _log1p_pos(e):
    u = 1.0 + e
    return jnp.where(u == 1.0, e, jnp.log(u) * (e / (u - 1.0)))


def _softplus(z):
    return jnp.maximum(z, 0.0) + _log1p_pos(jnp.exp(-jnp.abs(z)))


def _expm1_neg(x):
    series = x * (1.0 + x * 0.5 * (1.0 + x * (1.0 / 3.0) * (1.0 + x * 0.25)))
    return jnp.where(x > -0.03, series, jnp.exp(x) - 1.0)


GELU_C = math.sqrt(2.0 / math.pi)
GELU_K = 0.044715


def _gelu(x):
    return 0.5 * x * (1.0 + jnp.tanh(GELU_C * (x + GELU_K * (x * x * x))))


def _gelu_and_grad(x):
    t = jnp.tanh(GELU_C * (x + GELU_K * (x * x * x)))
    g = 0.5 * x * (1.0 + t)
    dg = 0.5 * (1.0 + t) + 0.5 * x * (1.0 - t * t) * (GELU_C * (1.0 + 3.0 * GELU_K * (x * x)))
    return g, dg


def _mm(pairs, *, ta=False, tb=False, tm, tn, tks, outs, name, epi=None, extra=()):
    n_pairs, n_extra, n_out = len(pairs), len(extra), len(outs)
    tas = list(ta) if isinstance(ta, (list, tuple)) else [ta] * n_pairs
    tbs = list(tb) if isinstance(tb, (list, tuple)) else [tb] * n_pairs
    a0, b0 = pairs[0]
    M = a0.shape[1] if tas[0] else a0.shape[0]
    N = b0.shape[0] if tbs[0] else b0.shape[1]
    tm, tn = min(tm, M), min(tn, N)
    nks, offs = [], []
    for (a, b), tk, pta in zip(pairs, tks, tas):
        K = a.shape[0] if pta else a.shape[1]
        assert K % tk == 0 and M % tm == 0 and N % tn == 0
        offs.append(sum(nks))
        nks.append(K // tk)
    nk_total = sum(nks)
    dims = [(((0 if pta else 1,), (1 if ptb else 0,)), ((), ())) for pta, ptb in zip(tas, tbs)]

    def kmap(off, nk):
        return lambda k: jnp.clip(k - off, 0, nk - 1)

    in_specs, operands = [], []
    for (a, b), tk, off, nk, pta, ptb in zip(pairs, tks, offs, nks, tas, tbs):
        km = kmap(off, nk)
        if pta:
            in_specs.append(pl.BlockSpec((tk, tm), lambda i, j, k, km=km: (km(k), i)))
        else:
            in_specs.append(pl.BlockSpec((tm, tk), lambda i, j, k, km=km: (i, km(k))))
        if ptb:
            in_specs.append(pl.BlockSpec((tn, tk), lambda i, j, k, km=km: (j, km(k))))
        else:
            in_specs.append(pl.BlockSpec((tk, tn), lambda i, j, k, km=km: (km(k), j)))
        operands += [a, b]
    for e in extra:
        in_specs.append(pl.BlockSpec((tm, tn), lambda i, j, k: (i, j)))
        operands.append(e)

    def body(*refs):
        ab = refs[:2 * n_pairs]
        ex = refs[2 * n_pairs:2 * n_pairs + n_extra]
        o = refs[2 * n_pairs + n_extra:2 * n_pairs + n_extra + n_out]
        k = pl.program_id(2)

        def finish(acc):
            res = epi(acc, *[e[...] for e in ex]) if epi is not None else (acc,)
            for r, oref in zip(res, o):
                oref[...] = r.astype(oref.dtype)

        if nk_total == 1:
            finish(lax.dot_general(ab[0][...], ab[1][...], dims[0], preferred_element_type=F32))
            return
        acc = refs[-1]
        for p in range(n_pairs):
            a_ref, b_ref = ab[2 * p], ab[2 * p + 1]

            @pl.when((k >= offs[p]) & (k < offs[p] + nks[p]))
            def _(a_ref=a_ref, b_ref=b_ref, pdims=dims[p]):
                prod = lax.dot_general(a_ref[...], b_ref[...], pdims, preferred_element_type=F32)

                @pl.when(k == 0)
                def _():
                    acc[...] = prod

                @pl.when(k > 0)
                def _():
                    acc[...] += prod

        @pl.when(k == nk_total - 1)
        def _():
            finish(acc[...])

    return pl.pallas_call(
        body,
        name=name,
        grid=(M // tm, N // tn, nk_total),
        in_specs=in_specs,
        out_specs=[pl.BlockSpec((tm, tn), lambda i, j, k: (i, j)) for _ in outs],
        out_shape=[jax.ShapeDtypeStruct((M, N), dt) for dt in outs],
        scratch_shapes=[] if nk_total == 1 else [pltpu.VMEM((tm, tn), F32)],
        compiler_params=_cparams("parallel", "parallel", "arbitrary"),
    )(*operands)


def _norm_fwd(x, g, name):
    S, D = x.shape
    tr = min(ROW_TILE, S)

    def body(x_ref, g_ref, o_ref):
        xv = x_ref[...]
        r = lax.rsqrt(jnp.mean(xv * xv, axis=-1, keepdims=True) + RMS_EPS)
        o_ref[...] = ((xv * r) * g_ref[...]).astype(o_ref.dtype)

    return pl.pallas_call(
        body, name=name, grid=(S // tr,),
        in_specs=[pl.BlockSpec((tr, D), lambda i: (i, 0)), pl.BlockSpec((1, D), lambda i: (0, 0))],
        out_specs=pl.BlockSpec((tr, D), lambda i: (i, 0)),
        out_shape=jax.ShapeDtypeStruct((S, D), BF16),
        compiler_params=_cparams("parallel"),
    )(x, g.reshape(1, D))


def _rms_bwd_rows(dy, xv, g):
    r = lax.rsqrt(jnp.mean(xv * xv, axis=-1, keepdims=True) + RMS_EPS)
    xn = xv * r
    dxn = dy * g
    dx = r * (dxn - xn * jnp.mean(dxn * xn, axis=-1, keepdims=True))
    dg = jnp.sum(dy * xn, axis=0, keepdims=True)
    return dx, dg


def _norm_bwd(dy, x, g, dres, name):
    S, D = x.shape
    tr = min(ROW_TILE, S)

    def body(dy_ref, x_ref, g_ref, dres_ref, dx_ref, dxb_ref, dg_ref):
        dx, dg = _rms_bwd_rows(dy_ref[...], x_ref[...], g_ref[...])
        dx = dres_ref[...] + dx
        dx_ref[...] = dx
        dxb_ref[...] = dx.astype(BF16)

        @pl.when(pl.program_id(0) == 0)
        def _():
            dg_ref[...] = jnp.zeros_like(dg_ref)

        dg_ref[...] += dg

    row = pl.BlockSpec((tr, D), lambda i: (i, 0))
    vec = pl.BlockSpec((1, D), lambda i: (0, 0))
    return pl.pallas_call(
        body, name=name, grid=(S // tr,),
        in_specs=[row, row, vec, row],
        out_specs=[row, row, vec],
        out_shape=[jax.ShapeDtypeStruct((S, D), F32), jax.ShapeDtypeStruct((S, D), BF16),
                   jax.ShapeDtypeStruct((1, D), F32)],
        compiler_params=_cparams("arbitrary"),
    )(dy, x, g.reshape(1, D), dres)


def _final_norm_loss(x2, target, g):
    S, D = x2.shape
    tr = min(ROW_TILE, S)

    def body(x_ref, t_ref, g_ref, loss_ref, dg_ref, dx_ref, dxb_ref):
        xv = x_ref[...]
        gv = g_ref[...]
        r = lax.rsqrt(jnp.mean(xv * xv, axis=-1, keepdims=True) + RMS_EPS)
        y = (xv * r) * gv
        err = y - t_ref[...]
        part = 0.5 * jnp.sum(jnp.mean(err * err, axis=-1, keepdims=True), axis=0, keepdims=True)
        dy = err * (1.0 / D)
        dx, dg = _rms_bwd_rows(dy, xv, gv)
        dx_ref[...] = dx
        dxb_ref[...] = dx.astype(BF16)

        @pl.when(pl.program_id(0) == 0)
        def _():
            dg_ref[...] = jnp.zeros_like(dg_ref)
            loss_ref[...] = jnp.zeros_like(loss_ref)

        dg_ref[...] += dg
        loss_ref[...] += jnp.broadcast_to(part, loss_ref.shape)

    row = pl.BlockSpec((tr, D), lambda i: (i, 0))
    vec = pl.BlockSpec((1, D), lambda i: (0, 0))
    return pl.pallas_call(
        body, name="final_norm_loss", grid=(S // tr,),
        in_specs=[row, row, vec],
        out_specs=[pl.BlockSpec((SUBLANES, LANES), lambda i: (0, 0)), vec, row, row],
        out_shape=[jax.ShapeDtypeStruct((SUBLANES, LANES), F32), jax.ShapeDtypeStruct((1, D), F32),
                   jax.ShapeDtypeStruct((S, D), F32), jax.ShapeDtypeStruct((S, D), BF16)],
        compiler_params=_cparams("arbitrary"),
    )(x2, target, g.reshape(1, D))


def _lru_gates(xa, bd_j, ba_j, bx_j, sp_j):
    z = jnp.dot(xa.astype(BF16), bd_j, preferred_element_type=F32)
    r = _sigmoid(z[:, :LANES] + ba_j)
    ig = _sigmoid(z[:, LANES:] + bx_j)
    log_a = (-LRU_C) * r * sp_j
    a = jnp.exp(log_a)
    mult = jnp.sqrt(-_expm1_neg(2.0 * log_a))
    return r, ig, a, mult


def _conv_rows(xpad, cw_ref, cb_ref, sl, tc):
    out = jnp.broadcast_to(cb_ref[:, sl], (tc, LANES))
    for k in range(CONV_W):
        out = out + xpad[pl.ds(SUBLANES - (CONV_W - 1) + k, tc), sl] * cw_ref[k:k + 1, sl]
    return out


def _lru_fwd(xg, cw, cb, bd, ba, bx, lam):
    S = xg.shape[0]
    D = D_MODEL
    tc = min(LRU_CHUNK, S)
    hb = tc // SUBLANES

    def body(xl_ref, halo_ref, g_ref, cw_ref, cb_ref, bd_ref, ba_ref, bx_ref, lam_ref,
             h_ref, y_ref, xpad, a_s, b_s, carry):
        i = pl.program_id(0)

        @pl.when(i == 0)
        def _():
            carry[...] = jnp.zeros_like(carry)

        xpad[0:SUBLANES, :] = jnp.where(i > 0, halo_ref[...], 0.0)
        xpad[SUBLANES:, :] = xl_ref[...]
        for j in range(N_GROUPS):
            sl = slice(LANES * j, LANES * (j + 1))
            xa = _conv_rows(xpad, cw_ref, cb_ref, sl, tc)
            sp = _softplus(-lam_ref[:, sl])
            _, ig, a, mult = _lru_gates(xa, bd_ref[j], ba_ref[:, sl], bx_ref[:, sl], sp)
            a_s[:, sl] = a
            b_s[:, sl] = mult * (ig * xa)

        row = lax.broadcasted_iota(jnp.int32, (SUBLANES, D), 0)

        def step(t, c):
            o = pl.multiple_of(t * SUBLANES, SUBLANES)
            A = a_s[pl.ds(o, SUBLANES), :]
            B = b_s[pl.ds(o, SUBLANES), :]
            for d in (1, 2, 4):
                keep = row >= d
                a_sh = jnp.where(keep, pltpu.roll(A, d, 0), 1.0)
                b_sh = jnp.where(keep, pltpu.roll(B, d, 0), 0.0)
                B = A * b_sh + B
                A = A * a_sh
            hh = A * c + B
            h_ref[pl.ds(o, SUBLANES), :] = hh
            return jnp.broadcast_to(hh[SUBLANES - 1:SUBLANES, :], (SUBLANES, D))

        carry[...] = lax.fori_loop(0, hb, step, carry[...])
        y_ref[...] = (_gelu(g_ref[...]) * h_ref[...]).astype(BF16)

    row_spec = lambda col: pl.BlockSpec((tc, D), lambda i, col=col: (i, col))
    halo = pl.BlockSpec((SUBLANES, D), lambda i: (jnp.maximum(i * hb - 1, 0), 0))
    full = lambda shape: pl.BlockSpec(shape, lambda i: tuple(0 for _ in shape))
    return pl.pallas_call(
        body, name="lru_fwd", grid=(S // tc,),
        in_specs=[row_spec(0), halo, row_spec(1), full((CONV_W, D)), full((1, D)),
                  full((N_GROUPS, LANES, 2 * LANES)), full((1, D)), full((1, D)), full((1, D))],
        out_specs=[pl.BlockSpec((tc, D), lambda i: (i, 0)), pl.BlockSpec((tc, D), lambda i: (i, 0))],
        out_shape=[jax.ShapeDtypeStruct((S, D), F32), jax.ShapeDtypeStruct((S, D), BF16)],
        scratch_shapes=[pltpu.VMEM((tc + SUBLANES, D), F32), pltpu.VMEM((tc, D), F32),
                        pltpu.VMEM((tc, D), F32), pltpu.VMEM((SUBLANES, D), F32)],
        compiler_params=_cparams("arbitrary"),
    )(xg, xg, xg, cw, cb, bd, ba, bx, lam)


def _lru_bwd(xg, h, dyain, cw, cb, bd, ba, bx, lam):
    S = xg.shape[0]
    D = D_MODEL
    tc = min(LRU_CHUNK, S)
    hb = tc // SUBLANES
    nc = S // tc

    def body(xl_ref, xhalo_ref, g_ref, h_ref, hhalo_ref, dy_ref, cw_ref, cb_ref, bd_ref, ba_ref, bx_ref,
             lam_ref, dxg_ref, dcw_ref, dcb_ref, dba_ref, dbx_ref, dlam_ref, dbd_ref,
             xpad, hpad, a_s, b_s, dh_s, g_s, xa_s, r_s, ig_s, m_s, dxa_pad, carry_e, dxa_head):
        i = pl.program_id(0)
        c = nc - 1 - i

        @pl.when(i == 0)
        def _():
            carry_e[...] = jnp.zeros_like(carry_e)
            dxa_head[...] = jnp.zeros_like(dxa_head)
            for ref in (dcw_ref, dcb_ref, dba_ref, dbx_ref, dlam_ref, dbd_ref):
                ref[...] = jnp.zeros_like(ref)

        xpad[0:SUBLANES, :] = jnp.where(c > 0, xhalo_ref[...], 0.0)
        xpad[SUBLANES:, :] = xl_ref[...]
        hpad[0:SUBLANES, :] = jnp.where(c > 0, hhalo_ref[...], 0.0)
        hpad[SUBLANES:, :] = h_ref[...]

        for j in range(N_GROUPS):
            sl = slice(LANES * j, LANES * (j + 1))
            xa = _conv_rows(xpad, cw_ref, cb_ref, sl, tc)
            sp = _softplus(-lam_ref[:, sl])
            r, ig, a, mult = _lru_gates(xa, bd_ref[j], ba_ref[:, sl], bx_ref[:, sl], sp)
            gl, dgl = _gelu_and_grad(g_ref[:, sl])
            dy = dy_ref[:, sl]
            dh = dy * gl
            dxg_ref[:, D + LANES * j:D + LANES * (j + 1)] = (dy * h_ref[:, sl] * dgl).astype(BF16)
            a_s[:, sl] = a
            b_s[:, sl] = a * dh
            dh_s[:, sl] = dh
            xa_s[:, sl] = xa
            r_s[:, sl] = r
            ig_s[:, sl] = ig
            m_s[:, sl] = mult

        row = lax.broadcasted_iota(jnp.int32, (SUBLANES, D), 0)

        def step(tt, ce):
            o = pl.multiple_of((hb - 1 - tt) * SUBLANES, SUBLANES)
            A = a_s[pl.ds(o, SUBLANES), :]
            B = b_s[pl.ds(o, SUBLANES), :]
            for d in (1, 2, 4):
                keep = row < SUBLANES - d
                a_sh = jnp.where(keep, pltpu.roll(A, SUBLANES - d, 0), 1.0)
                b_sh = jnp.where(keep, pltpu.roll(B, SUBLANES - d, 0), 0.0)
                B = A * b_sh + B
                A = A * a_sh
            e = A * ce + B
            e_next = jnp.where(row < SUBLANES - 1, pltpu.roll(e, SUBLANES - 1, 0), ce)
            g_s[pl.ds(o, SUBLANES), :] = dh_s[pl.ds(o, SUBLANES), :] + e_next
            return jnp.broadcast_to(e[0:1, :], (SUBLANES, D))

        carry_e[...] = lax.fori_loop(0, hb, step, carry_e[...])

        for j in range(N_GROUPS):
            sl = slice(LANES * j, LANES * (j + 1))
            gg = g_s[:, sl]
            xa, r, ig, mult, a = xa_s[:, sl], r_s[:, sl], ig_s[:, sl], m_s[:, sl], a_s[:, sl]
            hprev = hpad[pl.ds(SUBLANES - 1, tc), sl]
            sp = _softplus(-lam_ref[:, sl])
            da = gg * hprev
            dmult = gg * (ig * xa)
            dig = gg * (mult * xa)
            dxa = gg * (mult * ig)
            dla = da * a - dmult * ((a * a) / mult)
            dr = dla * ((-LRU_C) * sp)
            dlam_ref[:, sl] += jnp.sum(dla * r, axis=0, keepdims=True)
            dza = dr * r * (1.0 - r)
            dzx = dig * ig * (1.0 - ig)
            dba_ref[:, sl] += jnp.sum(dza, axis=0, keepdims=True)
            dbx_ref[:, sl] += jnp.sum(dzx, axis=0, keepdims=True)
            dz = jnp.concatenate([dza, dzx], axis=1).astype(BF16)
            dbd_ref[j] += lax.dot_general(xa.astype(BF16), dz, TN_DIMS, preferred_element_type=F32)
            dxa = dxa + lax.dot_general(dz, bd_ref[j], NT_DIMS, preferred_element_type=F32)
            dxa_pad[0:tc, sl] = dxa

        dxa_pad[tc:, :] = dxa_head[...]
        dxa_head[...] = dxa_pad[0:SUBLANES, :]

        for j in range(N_GROUPS):
            sl = slice(LANES * j, LANES * (j + 1))
            dxa = dxa_pad[0:tc, sl]
            dxl = jnp.zeros((tc, LANES), F32)
            for k in range(CONV_W):
                dxl = dxl + dxa_pad[pl.ds(CONV_W - 1 - k, tc), sl] * cw_ref[k:k + 1, sl]
                dcw_ref[k:k + 1, sl] += jnp.sum(
                    dxa * xpad[pl.ds(SUBLANES - (CONV_W - 1) + k, tc), sl], axis=0, keepdims=True)
            dxg_ref[:, sl] = dxl.astype(BF16)
            dcb_ref[:, sl] += jnp.sum(dxa, axis=0, keepdims=True)

        @pl.when(i == nc - 1)
        def _():
            dlam_ref[...] = dlam_ref[...] * (LRU_C * _sigmoid(-lam_ref[...]))

    rev = lambda col: pl.BlockSpec((tc, D), lambda i, col=col: (nc - 1 - i, col))
    halo = pl.BlockSpec((SUBLANES, D), lambda i: (jnp.maximum((nc - 1 - i) * hb - 1, 0), 0))
    full = lambda shape: pl.BlockSpec(shape, lambda i: tuple(0 for _ in shape))
    big = lambda: pltpu.VMEM((tc, D), F32)
    return pl.pallas_call(
        body, name="lru_bwd", grid=(nc,),
        in_specs=[rev(0), halo, rev(1), rev(0), halo, rev(0), full((CONV_W, D)), full((1, D)),
                  full((N_GROUPS, LANES, 2 * LANES)), full((1, D)), full((1, D)), full((1, D))],
        out_specs=[pl.BlockSpec((tc, 2 * D), lambda i: (nc - 1 - i, 0)), full((CONV_W, D)), full((1, D)),
                   full((1, D)), full((1, D)), full((1, D)), full((N_GROUPS, LANES, 2 * LANES))],
        out_shape=[jax.ShapeDtypeStruct((S, 2 * D), BF16), jax.ShapeDtypeStruct((CONV_W, D), F32),
                   jax.ShapeDtypeStruct((1, D), F32), jax.ShapeDtypeStruct((1, D), F32),
                   jax.ShapeDtypeStruct((1, D), F32), jax.ShapeDtypeStruct((1, D), F32),
                   jax.ShapeDtypeStruct((N_GROUPS, LANES, 2 * LANES), F32)],
        scratch_shapes=[pltpu.VMEM((tc + SUBLANES, D), F32), pltpu.VMEM((tc + SUBLANES, D), F32),
                        big(), big(), big(), big(), big(), big(), big(), big(),
                        pltpu.VMEM((tc + SUBLANES, D), F32), pltpu.VMEM((SUBLANES, D), F32),
                        pltpu.VMEM((SUBLANES, D), F32)],
        compiler_params=_cparams("arbitrary"),
    )(xg, xg, xg, h, h, dyain, cw, cb, bd, ba, bx, lam)


def _forget_cumsum(fl, fb):
    S = fl.shape[0]
    tr = min(ROW_TILE, S)
    hb = tr // SUBLANES

    def body(fl_ref, fb_ref, o_ref, rep_ref, lf_s, carry):
        @pl.when(pl.program_id(0) == 0)
        def _():
            carry[...] = jnp.zeros_like(carry)

        lf_s[...] = -_softplus(-(fl_ref[...] + fb_ref[...]))
        row = lax.broadcasted_iota(jnp.int32, (SUBLANES, LANES), 0)

        def step(t, c):
            o = pl.multiple_of(t * SUBLANES, SUBLANES)
            B = lf_s[pl.ds(o, SUBLANES), :]
            for d in (1, 2, 4):
                B = B + jnp.where(row >= d, pltpu.roll(B, d, 0), 0.0)
            B = B + c
            o_ref[pl.ds(o, SUBLANES), :] = B
            return jnp.broadcast_to(B[SUBLANES - 1:SUBLANES, :], (SUBLANES, LANES))

        carry[...] = lax.fori_loop(0, hb, step, carry[...])
        for h in range(N_HEADS):
            rep_ref[h] = jnp.broadcast_to(o_ref[:, h:h + 1], (tr, LANES))

    return pl.pallas_call(
        body, name="forget_cumsum", grid=(S // tr,),
        in_specs=[pl.BlockSpec((tr, LANES), lambda i: (i, 0)), pl.BlockSpec((1, LANES), lambda i: (0, 0))],
        out_specs=[pl.BlockSpec((tr, LANES), lambda i: (i, 0)),
                   pl.BlockSpec((N_HEADS, tr, LANES), lambda i: (0, i, 0))],
        out_shape=[jax.ShapeDtypeStruct((S, LANES), F32), jax.ShapeDtypeStruct((N_HEADS, S, LANES), F32)],
        scratch_shapes=[pltpu.VMEM((tr, LANES), F32), pltpu.VMEM((SUBLANES, LANES), F32)],
        compiler_params=_cparams("arbitrary"),
    )(fl, fb)


def _forget_bwd(dF, fl, fb):
    S = fl.shape[0]
    tr = min(ROW_TILE, S)
    hb = tr // SUBLANES
    nc = S // tr

    def body(df_ref, fl_ref, fb_ref, o_ref, dfb_ref, carry):
        @pl.when(pl.program_id(0) == 0)
        def _():
            carry[...] = jnp.zeros_like(carry)
            dfb_ref[...] = jnp.zeros_like(dfb_ref)

        row = lax.broadcasted_iota(jnp.int32, (SUBLANES, LANES), 0)

        def step(tt, carried):
            c, acc = carried
            o = pl.multiple_of((hb - 1 - tt) * SUBLANES, SUBLANES)
            B = df_ref[pl.ds(o, SUBLANES), :]
            for d in (1, 2, 4):
                B = B + jnp.where(row < SUBLANES - d, pltpu.roll(B, SUBLANES - d, 0), 0.0)
            B = B + c
            z = fl_ref[pl.ds(o, SUBLANES), :] + fb_ref[...]
            dz = B * _sigmoid(-z)
            o_ref[pl.ds(o, SUBLANES), :] = dz.astype(BF16)
            return jnp.broadcast_to(B[0:1, :], (SUBLANES, LANES)), acc + dz

        c, acc = lax.fori_loop(0, hb, step, (carry[...], jnp.zeros((SUBLANES, LANES), F32)))
        carry[...] = c
        dfb_ref[...] += jnp.sum(acc, axis=0, keepdims=True)

    rev = pl.BlockSpec((tr, LANES), lambda i: (nc - 1 - i, 0))
    vec = pl.BlockSpec((1, LANES), lambda i: (0, 0))
    return pl.pallas_call(
        body, name="forget_bwd", grid=(nc,),
        in_specs=[rev, rev, vec],
        out_specs=[rev, vec],
        out_shape=[jax.ShapeDtypeStruct((S, LANES), BF16), jax.ShapeDtypeStruct((1, LANES), F32)],
        scratch_shapes=[pltpu.VMEM((SUBLANES, LANES), F32)],
        compiler_params=_cparams("arbitrary"),
    )(dF, fl, fb)


def _triangle(n, key_major):
    pairs = [(q, k) for q in range(n) for k in range(q + 1)]
    if key_major:
        pairs.sort(key=lambda qk: (qk[1], qk[0]))
    return (jnp.asarray([q for q, _ in pairs], jnp.int32), jnp.asarray([k for _, k in pairs], jnp.int32))


def _strip_scores(k_ref, qt_ref, fq_ref, fk_ref, j, strip, nkeys, diagonal):
    cols = slice(strip * j, strip * (j + 1))
    s = jnp.dot(k_ref[0:nkeys, :], qt_ref[:, cols], preferred_element_type=F32) * ATTN_SCALE
    fk = fk_ref[0:nkeys, :]
    s = s + fq_ref[:, cols] - jnp.concatenate([fk] * (strip // LANES), axis=1)
    keep = None
    if diagonal:
        keys = lax.broadcasted_iota(jnp.int32, (nkeys, strip), 0)
        queries = lax.broadcasted_iota(jnp.int32, (nkeys, strip), 1) + strip * j
        keep = keys <= queries
    return s, keep


def _attn_fwd(kv, qkv_t, f_row, f_rep):
    S = kv.shape[0]
    blk = min(ATTN_BLOCK, S)
    strip = min(ATTN_STRIP, blk)
    n = S // blk
    tri_q, tri_k = _triangle(n, key_major=False)

    def body(tq_ref, tk_ref, k_ref, qt_ref, vt_ref, fq_ref, fk_ref, ot_ref, lse_ref, m_s, l_s, acc_s):
        t = pl.program_id(1)
        qi, ki = tq_ref[t], tk_ref[t]

        @pl.when(ki == 0)
        def _():
            m_s[...] = jnp.full_like(m_s, NEG_BIG)
            l_s[...] = jnp.zeros_like(l_s)
            acc_s[...] = jnp.zeros_like(acc_s)

        def update(diagonal):
            n_strips = blk // strip
            keys_of = lambda j: strip * (j + 1) if diagonal else blk
            scores = lambda j: _strip_scores(k_ref, qt_ref, fq_ref, fk_ref, j, strip, keys_of(j), diagonal)
            ahead = scores(0)
            for j in range(n_strips):
                cols = slice(strip * j, strip * (j + 1))
                nkeys = keys_of(j)
                (s, keep), ahead = ahead, (scores(j + 1) if j + 1 < n_strips else None)
                if diagonal:
                    s = jnp.where(keep, s, NEG_BIG)
                m_old = m_s[:, cols]
                m_new = jnp.maximum(m_old, jnp.max(s, axis=0, keepdims=True))
                alpha = jnp.exp(m_old - m_new)
                p = jnp.exp(s - m_new)
                l_s[:, cols] = alpha * l_s[:, cols] + jnp.sum(p, axis=0, keepdims=True)
                acc_s[:, cols] = alpha * acc_s[:, cols] + jnp.dot(
                    vt_ref[:, 0:nkeys], p.astype(BF16), preferred_element_type=F32)
                m_s[:, cols] = m_new

        @pl.when(ki < qi)
        def _():
            update(False)

        @pl.when(ki == qi)
        def _():
            update(True)
            ot_ref[...] = (acc_s[...] / l_s[...]).astype(BF16)
            lse_ref[...] = m_s[...] + jnp.log(l_s[...])

    return pl.pallas_call(
        body, name="attn_fwd",
        grid_spec=pltpu.PrefetchScalarGridSpec(
            num_scalar_prefetch=2, grid=(N_HEADS, tri_q.shape[0]),
            in_specs=[pl.BlockSpec((blk, HEAD_DIM), lambda h, t, tq, tk: (tk[t], h)),
                      pl.BlockSpec((HEAD_DIM, blk), lambda h, t, tq, tk: (h, tq[t])),
                      pl.BlockSpec((HEAD_DIM, blk), lambda h, t, tq, tk: (2 * N_HEADS + h, tk[t])),
                      pl.BlockSpec((None, 1, blk), lambda h, t, tq, tk: (h, 0, tq[t])),
                      pl.BlockSpec((None, blk, LANES), lambda h, t, tq, tk: (h, tk[t], 0))],
            out_specs=[pl.BlockSpec((HEAD_DIM, blk), lambda h, t, tq, tk: (h, tq[t])),
                       pl.BlockSpec((None, 1, blk), lambda h, t, tq, tk: (h, 0, tq[t]))],
            scratch_shapes=[pltpu.VMEM((1, blk), F32), pltpu.VMEM((1, blk), F32), pltpu.VMEM((HEAD_DIM, blk), F32)]),
        out_shape=[jax.ShapeDtypeStruct((N_HEADS * HEAD_DIM, S), BF16), jax.ShapeDtypeStruct((N_HEADS, 1, S), F32)],
        compiler_params=_cparams("parallel", "arbitrary"),
    )(tri_q, tri_k, kv, qkv_t, qkv_t, f_row, f_rep)


def _attn_bwd(kv, qkv_t, do_t, o_t, lse, f_row, f_rep):
    S = kv.shape[0]
    blk = min(ATTN_BLOCK, S)
    strip = min(ATTN_STRIP, blk)
    n = S // blk
    tri_q, tri_k = _triangle(n, key_major=True)
    n_tiles = tri_q.shape[0]

    def body(tq_ref, tk_ref, k_ref, v_ref, qt_ref, kt_ref, dot_ref, ot_ref, lse_ref, fq_ref, fk_ref,
             dqt_ref, dkt_ref, dvt_ref, dfk_ref, dfq_ref, dq_s, dk_s, dv_s, dfk_s, dfq_s, row_s):
        t = pl.program_id(1)
        qi, ki = tq_ref[t], tk_ref[t]

        @pl.when(t == 0)
        def _():
            dq_s[...] = jnp.zeros_like(dq_s)
            dfq_s[...] = jnp.zeros_like(dfq_s)

        @pl.when(qi == ki)
        def _():
            dk_s[...] = jnp.zeros_like(dk_s)
            dv_s[...] = jnp.zeros_like(dv_s)
            dfk_s[...] = jnp.zeros_like(dfk_s)

        def update(diagonal):
            row_s[...] = fq_ref[...] - lse_ref[...]
            n_strips = blk // strip
            keys_of = lambda j: strip * (j + 1) if diagonal else blk

            def matmuls_in(j):
                s, keep = _strip_scores(k_ref, qt_ref, row_s, fk_ref, j, strip, keys_of(j), diagonal)
                dp = jnp.dot(v_ref[0:keys_of(j), :], dot_ref[:, strip * j:strip * (j + 1)], preferred_element_type=F32)
                return s, keep, dp

            ahead = matmuls_in(0)
            for j in range(n_strips):
                cols = slice(strip * j, strip * (j + 1))
                nkeys = keys_of(j)
                (s, keep, dp), ahead = ahead, (matmuls_in(j + 1) if j + 1 < n_strips else None)
                p = jnp.exp(s)
                if diagonal:
                    p = jnp.where(keep, p, 0.0)
                dot = dot_ref[:, cols]
                delta = jnp.sum(dot.astype(F32) * ot_ref[:, cols].astype(F32), axis=0, keepdims=True)
                ds = p * (dp - delta)
                dsb = ds.astype(BF16)
                dv_s[:, 0:nkeys] += lax.dot_general(dot, p.astype(BF16), NT_DIMS, preferred_element_type=F32)
                dk_s[:, 0:nkeys] += lax.dot_general(qt_ref[:, cols], dsb, NT_DIMS, preferred_element_type=F32)
                dq_s[qi, :, cols] += jnp.dot(kt_ref[:, 0:nkeys], dsb, preferred_element_type=F32)
                lane_part = ds[:, 0:LANES]
                for g in range(1, strip // LANES):
                    lane_part = lane_part + ds[:, LANES * g:LANES * (g + 1)]
                dfk_s[0:nkeys, :] += lane_part
                sub_part = ds[0:SUBLANES, :]
                for g in range(1, nkeys // SUBLANES):
                    sub_part = sub_part + ds[SUBLANES * g:SUBLANES * (g + 1), :]
                dfq_s[qi, :, cols] += sub_part

        @pl.when(qi == ki)
        def _():
            update(True)

        @pl.when(qi > ki)
        def _():
            update(False)

        @pl.when(qi == n - 1)
        def _():
            dkt_ref[...] = (dk_s[...] * ATTN_SCALE).astype(BF16)
            dvt_ref[...] = dv_s[...].astype(BF16)
            dfk_ref[...] = -jnp.sum(dfk_s[...], axis=-1, keepdims=True)

        @pl.when(t == n_tiles - 1)
        def _():
            for j in range(n):
                dqt_ref[:, blk * j:blk * (j + 1)] = (dq_s[j] * ATTN_SCALE).astype(BF16)
                dfq_ref[:, blk * j:blk * (j + 1)] = jnp.sum(dfq_s[j], axis=0, keepdims=True)

    q_feat = pl.BlockSpec((HEAD_DIM, blk), lambda h, t, tq, tk: (h, tq[t]))
    q_row = pl.BlockSpec((None, 1, blk), lambda h, t, tq, tk: (h, 0, tq[t]))
    k_feat = pl.BlockSpec((HEAD_DIM, blk), lambda h, t, tq, tk: (h, tk[t]))
    return pl.pallas_call(
        body, name="attn_bwd",
        grid_spec=pltpu.PrefetchScalarGridSpec(
            num_scalar_prefetch=2, grid=(N_HEADS, n_tiles),
            in_specs=[pl.BlockSpec((blk, HEAD_DIM), lambda h, t, tq, tk: (tk[t], h)),
                      pl.BlockSpec((blk, HEAD_DIM), lambda h, t, tq, tk: (tk[t], N_HEADS + h)),
                      q_feat,
                      pl.BlockSpec((HEAD_DIM, blk), lambda h, t, tq, tk: (N_HEADS + h, tk[t])),
                      q_feat, q_feat, q_row, q_row,
                      pl.BlockSpec((None, blk, LANES), lambda h, t, tq, tk: (h, tk[t], 0))],
            out_specs=[pl.BlockSpec((HEAD_DIM, S), lambda h, t, tq, tk: (h, 0)), k_feat, k_feat,
                       pl.BlockSpec((None, blk, 1), lambda h, t, tq, tk: (h, tk[t], 0)),
                       pl.BlockSpec((None, 1, S), lambda h, t, tq, tk: (h, 0, 0))],
            scratch_shapes=[pltpu.VMEM((n, HEAD_DIM, blk), F32), pltpu.VMEM((HEAD_DIM, blk), F32),
                            pltpu.VMEM((HEAD_DIM, blk), F32), pltpu.VMEM((blk, LANES), F32),
                            pltpu.VMEM((n, SUBLANES, blk), F32), pltpu.VMEM((1, blk), F32)]),
        out_shape=[jax.ShapeDtypeStruct((N_HEADS * HEAD_DIM, S), BF16)] * 3
        + [jax.ShapeDtypeStruct((N_HEADS, S, 1), F32), jax.ShapeDtypeStruct((N_HEADS, 1, S), F32)],
        compiler_params=_cparams("parallel", "arbitrary"),
    )(tri_q, tri_k, kv, kv, qkv_t, qkv_t, do_t, o_t, lse, f_row, f_rep)


def _gate_mix(gates, ya, yb):
    S, D = ya.shape
    tr = min(ROW_TILE, S)

    def body(ga_ref, gb_ref, ya_ref, yb_ref, o_ref):
        o_ref[...] = (_sigmoid(ga_ref[...]) * ya_ref[...] + _sigmoid(gb_ref[...]) * yb_ref[...]).astype(BF16)

    col = lambda j: pl.BlockSpec((tr, D), lambda i, j=j: (i, j))
    return pl.pallas_call(
        body, name="gate_mix", grid=(S // tr,),
        in_specs=[col(0), col(1), col(0), col(0)],
        out_specs=col(0),
        out_shape=jax.ShapeDtypeStruct((S, D), BF16),
        compiler_params=_cparams("parallel"),
    )(gates, gates, ya, yb)


def _gate_bwd(dmix, gates, ya, yb):
    S, D = ya.shape
    tr = min(ROW_TILE, S)

    def body(dm_ref, ga_ref, gb_ref, ya_ref, yb_ref, dya_ref, dyb_ref, dg_ref):
        dm = dm_ref[...]
        sa, sb = _sigmoid(ga_ref[...]), _sigmoid(gb_ref[...])
        dya_ref[...] = (dm * sa).astype(BF16)
        dyb_ref[...] = (dm * sb).astype(BF16)
        dg_ref[:, 0:D] = ((dm * ya_ref[...]) * (sa * (1.0 - sa))).astype(BF16)
        dg_ref[:, D:] = ((dm * yb_ref[...]) * (sb * (1.0 - sb))).astype(BF16)

    col = lambda j: pl.BlockSpec((tr, D), lambda i, j=j: (i, j))
    return pl.pallas_call(
        body, name="gate_bwd", grid=(S // tr,),
        in_specs=[col(0), col(0), col(1), col(0), col(0)],
        out_specs=[col(0), col(0), pl.BlockSpec((tr, 2 * D), lambda i: (i, 0))],
        out_shape=[jax.ShapeDtypeStruct((S, D), BF16), jax.ShapeDtypeStruct((S, D), BF16),
                   jax.ShapeDtypeStruct((S, 2 * D), BF16)],
        compiler_params=_cparams("parallel"),
    )(dmix, gates, gates, ya, yb)


def _mesh_place():
    x, y, c = lax.axis_index("x"), lax.axis_index("y"), lax.axis_index("c")
    chips = [(1 - x, y), (x, 1 - y), (1 - x, 1 - y)]
    return x, y, c, chips


def _all_gather(shards):
    n = len(shards)

    def body(*refs):
        ins, outs = refs[:n], refs[n:2 * n]
        send_sems, recv_sems, local_sems = refs[2 * n:]
        x, y, c, chips = _mesh_place()
        me, sib = (x, y, c), (x, y, 1 - c)

        def copy(a, k, block, to, src=None):
            px, py, pc = block
            dst = outs[a].at[4 * px + 2 * py + pc]
            return pltpu.make_async_remote_copy(
                src_ref=dst if src is None else src, dst_ref=dst,
                send_sem=send_sems.at[a, k], recv_sem=recv_sems.at[a, k],
                device_id=to, device_id_type=MESH_ID)

        mine = [pltpu.make_async_copy(ins[a], outs[a].at[4 * x + 2 * y + c], local_sems.at[a]) for a in range(n)]
        for cp in mine:
            cp.start()
        first = []
        for a in range(n):
            first.append(copy(a, 0, me, sib, src=ins[a]))
            for j, chip in enumerate(chips):
                first.append(copy(a, 1 + j, me, (*chip, c), src=ins[a]))
        for cp in first:
            cp.start()
        passed = []
        for j, chip in enumerate(chips):
            for a in range(n):
                copy(a, 1 + j, (*chip, c), me).wait_recv()
                fwd = copy(a, 4 + j, (*chip, c), sib)
                fwd.start()
                passed.append(fwd)
        for a in range(n):
            copy(a, 0, sib, me).wait_recv()
            for j, chip in enumerate(chips):
                copy(a, 4 + j, (*chip, 1 - c), me).wait_recv()
        for cp in first + passed:
            cp.wait_send()
        for cp in mine:
            cp.wait()

    return pl.pallas_call(
        body, name="all_gather_weights",
        in_specs=[ANY] * n, out_specs=[ANY] * n,
        out_shape=[jax.ShapeDtypeStruct((N_DEV,) + s.shape, s.dtype) for s in shards],
        scratch_shapes=[pltpu.SemaphoreType.DMA((n, 7)), pltpu.SemaphoreType.DMA((n, 7)),
                        pltpu.SemaphoreType.DMA((n,))],
    )(*shards)


def _reduce_scatter_cores(grads):
    n = len(grads)

    def body(*refs):
        ins, gots = refs[:n], refs[n:2 * n]
        send_sems, recv_sems = refs[2 * n:]
        x, y, c, _ = _mesh_place()
        sib = (x, y, 1 - c)
        remote = []
        for a in range(n):
            for k in range(4):
                remote.append(pltpu.make_async_remote_copy(
                    src_ref=ins[a].at[2 * k + (1 - c)], dst_ref=gots[a].at[k],
                    send_sem=send_sems.at[a, k], recv_sem=recv_sems.at[a, k],
                    device_id=sib, device_id_type=MESH_ID))
        for cp in remote:
            cp.start()
        for cp in remote:
            cp.wait_recv()
        for cp in remote:
            cp.wait_send()

    return pl.pallas_call(
        body, name="reduce_scatter_cores",
        in_specs=[ANY] * n, out_specs=[ANY] * n,
        out_shape=[jax.ShapeDtypeStruct((4,) + g.shape[1:], g.dtype) for g in grads],
        scratch_shapes=[pltpu.SemaphoreType.DMA((n, 4)), pltpu.SemaphoreType.DMA((n, 4))],
    )(*grads)


def _chip_partial_sum(blocks, got, core):
    R, C = got.shape[1:]
    tr = min(256, R)
    assert R % tr == 0

    def body(core_ref, a_ref, b_ref, s_ref, sb_ref):
        s = a_ref[...] + b_ref[...]
        s_ref[...] = s
        sb_ref[...] = s.astype(BF16)

    blk = pl.BlockSpec((None, tr, C), lambda k, i, core_ref: (k, i, 0))
    return pl.pallas_call(
        body, name="chip_partial_sum",
        grid_spec=pltpu.PrefetchScalarGridSpec(
            num_scalar_prefetch=1, grid=(4, R // tr),
            in_specs=[pl.BlockSpec((None, tr, C), lambda k, i, core_ref: (2 * k + core_ref[0], i, 0)), blk],
            out_specs=[blk, blk]),
        out_shape=[jax.ShapeDtypeStruct(got.shape, F32), jax.ShapeDtypeStruct(got.shape, BF16)],
        compiler_params=_cparams("parallel", "parallel"),
    )(core, blocks, got)


def _reduce_scatter_chips(sums_bf16):
    n = len(sums_bf16)

    def body(*refs):
        bf16s, gots = refs[:n], refs[n:2 * n]
        send_sems, recv_sems = refs[2 * n:]
        x, y, c, chips = _mesh_place()
        remote = []
        for a in range(n):
            for j, (px, py) in enumerate(chips):
                remote.append(pltpu.make_async_remote_copy(
                    src_ref=bf16s[a].at[2 * px + py], dst_ref=gots[a].at[j],
                    send_sem=send_sems.at[a, j], recv_sem=recv_sems.at[a, j],
                    device_id=(px, py, c), device_id_type=MESH_ID))
        for cp in remote:
            cp.start()
        for cp in remote:
            cp.wait_recv()
        for cp in remote:
            cp.wait_send()

    return pl.pallas_call(
        body, name="reduce_scatter_chips",
        in_specs=[ANY] * n, out_specs=[ANY] * n,
        out_shape=[jax.ShapeDtypeStruct((3,) + s.shape[1:], BF16) for s in sums_bf16],
        scratch_shapes=[pltpu.SemaphoreType.DMA((n, 3)), pltpu.SemaphoreType.DMA((n, 3))],
    )(*sums_bf16)


def _all_reduce_small(vec):
    R = vec.shape[0]

    def body(v_ref, o_ref, sib_buf, chip_buf, send_sems, recv_sems):
        x, y, c, chips = _mesh_place()
        swap = pltpu.make_async_remote_copy(
            src_ref=v_ref, dst_ref=sib_buf, send_sem=send_sems.at[0], recv_sem=recv_sems.at[0],
            device_id=(x, y, 1 - c), device_id_type=MESH_ID)
        swap.start()
        swap.wait()
        my_chip = 2 * x + y
        chip_buf[my_chip] = v_ref[...] + sib_buf[...]
        sends = []
        for j, (px, py) in enumerate(chips):
            cp = pltpu.make_async_remote_copy(
                src_ref=chip_buf.at[my_chip], dst_ref=chip_buf.at[my_chip],
                send_sem=send_sems.at[1 + j], recv_sem=recv_sems.at[1 + j],
                device_id=(px, py, c), device_id_type=MESH_ID)
            cp.start()
            sends.append(cp)
        for j, (px, py) in enumerate(chips):
            pltpu.make_async_remote_copy(
                src_ref=chip_buf.at[2 * px + py], dst_ref=chip_buf.at[2 * px + py],
                send_sem=send_sems.at[1 + j], recv_sem=recv_sems.at[1 + j],
                device_id=(px, py, c), device_id_type=MESH_ID).wait_recv()
        for cp in sends:
            cp.wait_send()
        o_ref[...] = ((chip_buf[0] + chip_buf[1]) + chip_buf[2]) + chip_buf[3]

    vm = pl.BlockSpec(memory_space=pltpu.VMEM)
    return pl.pallas_call(
        body, name="all_reduce_small",
        in_specs=[vm], out_specs=vm,
        out_shape=jax.ShapeDtypeStruct(vec.shape, F32),
        scratch_shapes=[pltpu.VMEM((R, LANES), F32), pltpu.VMEM((4, R, LANES), F32),
                        pltpu.SemaphoreType.DMA((4,)), pltpu.SemaphoreType.DMA((4,))],
    )(vec)


def _adamw_math(w, g, m, v):
    m = ADAM_B1 * m + (1.0 - ADAM_B1) * g
    v = ADAM_B2 * v + (1.0 - ADAM_B2) * (g * g)
    m_hat = m / (1.0 - ADAM_B1 ** ADAM_STEP)
    v_hat = v / (1.0 - ADAM_B2 ** ADAM_STEP)
    delta = -ADAM_LR * (m_hat / (jnp.sqrt(v_hat) + ADAM_EPS) + ADAM_WD * w)
    return delta, m, v


def _adamw(w, m, v, g_own, g_got, chip, name):
    R, C = w.shape
    tr = R if R * C <= 256 * D_MODEL else 256
    assert R % tr == 0
    n_got = 0 if g_got is None else 3

    def body(*refs):
        w_ref, m_ref, v_ref, go_ref = refs[1:5]
        got = refs[5:5 + n_got]
        g_ref, d_ref, nm_ref, nv_ref = refs[5 + n_got:]
        g = go_ref[...]
        for r in got:
            g = g + r[...].astype(F32)
        delta, m_new, v_new = _adamw_math(w_ref[...], g, m_ref[...], v_ref[...])
        g_ref[...] = g
        d_ref[...] = delta
        nm_ref[...] = m_new
        nv_ref[...] = v_new

    blk = pl.BlockSpec((tr, C), lambda i, chip_ref: (i, 0))
    own_spec = pl.BlockSpec((None, tr, C), lambda i, chip_ref: (chip_ref[0], i, 0))
    got_specs = [pl.BlockSpec((None, tr, C), lambda i, chip_ref, j=j: (j, i, 0)) for j in range(n_got)]
    return pl.pallas_call(
        body, name=name,
        grid_spec=pltpu.PrefetchScalarGridSpec(
            num_scalar_prefetch=1, grid=(R // tr,),
            in_specs=[blk] * 3 + [own_spec] + got_specs, out_specs=[blk] * 4),
        out_shape=[jax.ShapeDtypeStruct((R, C), F32)] * 4,
        compiler_params=_cparams("parallel"),
    )(chip, w, m, v, g_own, *([g_got] * n_got))


def _block_diag_pairs(wa, wx):
    def pairs(w):
        w = w.reshape(N_GROUPS, 2, LRU_BW, LRU_BW)
        z = jnp.zeros((N_GROUPS, LRU_BW, LRU_BW), w.dtype)
        top = jnp.concatenate([w[:, 0], z], axis=2)
        bot = jnp.concatenate([z, w[:, 1]], axis=2)
        return jnp.concatenate([top, bot], axis=1)
    return jnp.concatenate([pairs(wa), pairs(wx)], axis=2).astype(BF16)


def _block_diag_unpair(dbd):
    def unpair(g):
        blocks = jnp.stack([g[:, :LRU_BW, :LRU_BW], g[:, LRU_BW:, LRU_BW:]], axis=1)
        return blocks.reshape(LRU_BLOCKS, LRU_BW, LRU_BW)
    return unpair(dbd[:, :, :LANES]), unpair(dbd[:, :, LANES:])


def _local_step(x, target, W, small):
    S, D = x.shape
    g1, g2, g3 = small["norm_mix_g"], small["norm_mlp_g"], small["norm_final_g"]
    cw, cb = small["conv_w"], small["conv_b"].reshape(1, D)
    ba, bx, lam = (small[k].reshape(1, D) for k in ("lru_ba", "lru_bx", "lru_lambda"))
    fb = jnp.pad(small["forget_b"], (0, LANES - N_HEADS)).reshape(1, LANES)
    bd = _block_diag_pairs(small["lru_wa"], small["lru_wx"])
    big = dict(tm=1024, tn=1024)

    u = _norm_fwd(x, g1, "norm_mix")
    (xg,) = _mm([(u, W["in_xg"])], tks=[D], outs=[F32], name="proj_xg", **big)
    (qkv_t,) = _mm([(W["in_qkv_t"], u)], tb=True, tks=[D], outs=[BF16], name="proj_qkv_t", **big)
    (kv,) = _mm([(u, W["in_kv"])], tks=[D], outs=[BF16], name="proj_kv", **big)
    (gates,) = _mm([(u, W["in_gates"])], tks=[D], outs=[F32], name="proj_gates", **big)
    (fl,) = _mm([(u, W["in_f"])], tks=[D], outs=[F32], name="proj_forget", **big)
    h, yain = _lru_fwd(xg, cw, cb, bd, ba, bx, lam)
    fcum, f_rep = _forget_cumsum(fl, fb)
    f_row = fcum[:, :N_HEADS].T.reshape(N_HEADS, 1, S)
    ob_t, lse = _attn_fwd(kv, qkv_t, f_row, f_rep)
    (ya,) = _mm([(yain, W["branch_a"])], tks=[D], outs=[F32], name="branch_a", **big)
    (yb,) = _mm([(ob_t, W["branch_b"])], ta=True, tks=[D], outs=[F32], name="branch_b", **big)
    mix = _gate_mix(gates, ya, yb)
    (x1,) = _mm([(mix, W["out"])], tks=[D], outs=[F32], name="out_proj", extra=(x,),
                epi=lambda acc, res: (res + acc,), **big)
    m = _norm_fwd(x1, g2, "norm_mlp")
    relu, hh = _mm([(m, W["up"])], tks=[D], outs=[BF16, BF16], name="mlp_up",
                   epi=lambda acc: (jnp.maximum(acc, 0.0), jnp.square(jnp.maximum(acc, 0.0))), **big)
    (x2,) = _mm([(hh, W["down"])], tks=[1024], outs=[F32], name="mlp_down", extra=(x1,),
                epi=lambda acc, res: (res + acc,), **big)
    loss_acc, dg3, dx2, dx2b = _final_norm_loss(x2, target, g3)

    (dhpre,) = _mm([(dx2b, W["down"])], tb=True, tks=[D], outs=[BF16], name="d_mlp_act", extra=(relu,),
                   epi=lambda acc, r: (acc * (2.0 * r.astype(F32)),), **big)
    (dw_down,) = _mm([(hh, dx2b)], ta=True, tks=[min(1024, S)], outs=[F32], name="dw_down", **big)
    (dm,) = _mm([(dhpre, W["up"])], tb=True, tks=[1024], outs=[F32], name="d_mlp_in", **big)
    (dw_up,) = _mm([(m, dhpre)], ta=True, tks=[min(1024, S)], outs=[F32], name="dw_up", **big)
    dx1, dx1b, dg2 = _norm_bwd(dm, x1, g2, dx2, "norm_mlp_bwd")
    (dmix,) = _mm([(dx1b, W["out"])], tb=True, tks=[D], outs=[F32], name="d_mix", **big)
    (dw_out,) = _mm([(mix, dx1b)], ta=True, tks=[min(1024, S)], outs=[F32], name="dw_out", **big)
    dya, dyb, dgates = _gate_bwd(dmix, gates, ya, yb)
    (dob_t,) = _mm([(W["branch_b"], dyb)], tb=True, tks=[D], outs=[BF16], name="d_attn_out_t", **big)
    (dw_b,) = _mm([(ob_t, dyb)], tks=[min(1024, S)], outs=[F32], name="dw_branch_b", **big)
    (dyain,) = _mm([(dya, W["branch_a"])], tb=True, tks=[D], outs=[F32], name="d_lru_out", **big)
    (dw_a,) = _mm([(yain, dya)], ta=True, tks=[min(1024, S)], outs=[F32], name="dw_branch_a", **big)
    dq_t, dk_t, dv_t, dfk, dfq = _attn_bwd(kv, qkv_t, dob_t, ob_t, lse, f_row, f_rep)
    dF = jnp.pad((dfk.reshape(N_HEADS, S) + dfq.reshape(N_HEADS, S)).T, ((0, 0), (0, LANES - N_HEADS)))
    dfl, dfb = _forget_bwd(dF, fl, fb)
    dxg, dcw, dcb, dba, dbx, dlam, dbd = _lru_bwd(xg, h, dyain, cw, cb, bd, ba, bx, lam)
    wq_t, wk_t, wv_t = (W["in_qkv_t"][D * i:D * (i + 1)] for i in range(3))
    (du,) = _mm([(dxg, W["in_xg"]), (dq_t, wq_t), (dk_t, wk_t), (dv_t, wv_t), (dgates, W["in_gates"]),
                 (dfl, W["in_f"])],
                ta=[False, True, True, True, False, False], tb=[True, False, False, False, True, True],
                tks=[1024, D, D, D, 1024, LANES], outs=[F32], name="d_norm_mix_out", tm=1024, tn=512)
    tks = [min(1024, S)]
    dw_in_parts = [
        _mm([(u, dxg)], ta=True, tks=tks, outs=[F32], name="dw_in_xg", **big)[0],
        _mm([(dq_t, u)], tks=tks, outs=[F32], name="dw_in_q_t", **big)[0].T,
        _mm([(dk_t, u)], tks=tks, outs=[F32], name="dw_in_k_t", **big)[0].T,
        _mm([(dv_t, u)], tks=tks, outs=[F32], name="dw_in_v_t", **big)[0].T,
        _mm([(u, dgates)], ta=True, tks=tks, outs=[F32], name="dw_in_gates", **big)[0],
        _mm([(u, dfl)], ta=True, tks=tks, outs=[F32], name="dw_in_forget", **big)[0][:, :N_HEADS],
    ]
    grad_x, _, dg1 = _norm_bwd(du, x, g1, dx1, "norm_mix_bwd")

    dwa, dwx = _block_diag_unpair(dbd)
    big_grads = dict(w_in=jnp.concatenate(dw_in_parts, axis=1), w_branch_a=dw_a, w_branch_b=dw_b, w_out=dw_out,
                     w_up=dw_up, w_down=dw_down)
    small_grads = dict(norm_mix_g=dg1.reshape(D), conv_w=dcw, conv_b=dcb.reshape(D), lru_wa=dwa, lru_ba=dba.reshape(D),
                       lru_wx=dwx, lru_bx=dbx.reshape(D), lru_lambda=dlam.reshape(D), forget_b=dfb[0, :N_HEADS],
                       norm_mlp_g=dg2.reshape(D), norm_final_g=dg3.reshape(D))
    return loss_acc[0, 0], grad_x, big_grads, small_grads


SMALL_NAMES = ("norm_mix_g", "conv_b", "lru_wa", "lru_ba", "lru_wx", "lru_bx", "lru_lambda", "forget_b",
               "norm_mlp_g", "norm_final_g")
TILE_ELEMS = SUBLANES * LANES


def _pack_small(parts):
    rows = []
    for p in parts:
        flat = p.reshape(-1)
        flat = jnp.pad(flat, (0, (-flat.shape[0]) % TILE_ELEMS))
        rows.append(flat.reshape(-1, LANES))
    return jnp.concatenate(rows, axis=0)


def _unpack_small(packed, shapes):
    out, r = [], 0
    for shp in shapes:
        size = math.prod(shp)
        nrows = -(-size // TILE_ELEMS) * SUBLANES
        out.append(packed[r:r + nrows].reshape(-1)[:size].reshape(shp))
        r += nrows
    return out


BIG_NAMES = ("w_in", "w_branch_a", "w_branch_b", "w_out", "w_up", "w_down")
WEIGHT_ORDER = ("norm_mix_g", "w_in", "conv_w", "conv_b", "lru_wa", "lru_ba", "lru_wx", "lru_bx", "lru_lambda",
                "forget_b", "w_branch_a", "w_branch_b", "w_out", "norm_mlp_g", "w_up", "w_down", "norm_final_g")


def _to_dest_blocks(name, g):
    if name in ("w_in", "w_up"):
        return g.reshape(g.shape[0], N_DEV, g.shape[1] // N_DEV).transpose(1, 0, 2)
    return g.reshape(N_DEV, g.shape[0] // N_DEV, g.shape[1])


def kernel(x, norm_mix_g, w_in, conv_w, conv_b, lru_wa, lru_ba, lru_wx, lru_bx, lru_lambda, forget_b, w_branch_a, w_branch_b, w_out, norm_mlp_g, w_up, w_down, norm_final_g, loss_target, m_norm_mix_g, m_w_in, m_conv_w, m_conv_b, m_lru_wa, m_lru_ba, m_lru_wx, m_lru_bx, m_lru_lambda, m_forget_b, m_w_branch_a, m_w_branch_b, m_w_out, m_norm_mlp_g, m_w_up, m_w_down, m_norm_final_g, v_norm_mix_g, v_w_in, v_conv_w, v_conv_b, v_lru_wa, v_lru_ba, v_lru_wx, v_lru_bx, v_lru_lambda, v_forget_b, v_w_branch_a, v_w_branch_b, v_w_out, v_norm_mlp_g, v_w_up, v_w_down, v_norm_final_g):
    weights = dict(norm_mix_g=norm_mix_g, w_in=w_in, conv_w=conv_w, conv_b=conv_b, lru_wa=lru_wa, lru_ba=lru_ba,
                   lru_wx=lru_wx, lru_bx=lru_bx, lru_lambda=lru_lambda, forget_b=forget_b, w_branch_a=w_branch_a,
                   w_branch_b=w_branch_b, w_out=w_out, norm_mlp_g=norm_mlp_g, w_up=w_up, w_down=w_down,
                   norm_final_g=norm_final_g)
    moms = dict(norm_mix_g=m_norm_mix_g, w_in=m_w_in, conv_w=m_conv_w, conv_b=m_conv_b, lru_wa=m_lru_wa,
                lru_ba=m_lru_ba, lru_wx=m_lru_wx, lru_bx=m_lru_bx, lru_lambda=m_lru_lambda, forget_b=m_forget_b,
                w_branch_a=m_w_branch_a, w_branch_b=m_w_branch_b, w_out=m_w_out, norm_mlp_g=m_norm_mlp_g,
                w_up=m_w_up, w_down=m_w_down, norm_final_g=m_norm_final_g)
    vels = dict(norm_mix_g=v_norm_mix_g, w_in=v_w_in, conv_w=v_conv_w, conv_b=v_conv_b, lru_wa=v_lru_wa,
                lru_ba=v_lru_ba, lru_wx=v_lru_wx, lru_bx=v_lru_bx, lru_lambda=v_lru_lambda, forget_b=v_forget_b,
                w_branch_a=v_w_branch_a, w_branch_b=v_w_branch_b, w_out=v_w_out, norm_mlp_g=v_norm_mlp_g,
                w_up=v_w_up, w_down=v_w_down, norm_final_g=v_norm_final_g)
    S, D = x.shape[1], x.shape[2]
    me = 4 * lax.axis_index("x") + 2 * lax.axis_index("y") + lax.axis_index("c")

    gathered = _all_gather([weights[k].astype(BF16) for k in BIG_NAMES] + [conv_w])
    win_g, wa_g, wb_g, wo_g, wup_g, wdn_g, cw_g = gathered
    w_in_full = win_g.transpose(1, 0, 2).reshape(D, -1)
    cuts = (0, 2 * D, 5 * D, 7 * D)
    W = dict(in_xg=w_in_full[:, cuts[0]:cuts[1]], in_qkv_t=w_in_full[:, cuts[1]:cuts[2]].T,
             in_kv=w_in_full[:, cuts[1] + D:cuts[2]], in_gates=w_in_full[:, cuts[2]:cuts[3]],
             in_f=jnp.pad(w_in_full[:, cuts[3]:], ((0, 0), (0, LANES - N_HEADS))),
             branch_a=wa_g.reshape(D, D), branch_b=wb_g.reshape(D, D), out=wo_g.reshape(D, D),
             up=wup_g.transpose(1, 0, 2).reshape(D, D_FF), down=wdn_g.reshape(D_FF, D))
    small = {k: weights[k] for k in SMALL_NAMES}
    small["conv_w"] = cw_g.transpose(1, 0, 2).reshape(CONV_W, D)

    loss_part, grad_x, big_grads, small_grads = _local_step(x.reshape(S, D), loss_target.reshape(S, D), W, small)
    loss = lax.psum(loss_part, MESH_AXES)

    core = lax.axis_index("c").astype(jnp.int32).reshape(1)
    chip = (2 * lax.axis_index("x") + lax.axis_index("y")).astype(jnp.int32).reshape(1)
    blocks = [_to_dest_blocks(k, big_grads[k]) for k in BIG_NAMES]
    got = _reduce_scatter_cores(blocks)
    sums = [_chip_partial_sum(b, g, core) for b, g in zip(blocks, got)]
    others = _reduce_scatter_chips([s[1] for s in sums])

    small_list = [small_grads[k] for k in SMALL_NAMES] + [small_grads["conv_w"]]
    small_shapes = [a.shape for a in small_list]
    reduced = _unpack_small(_all_reduce_small(_pack_small(small_list)), small_shapes)
    small_reduced = dict(zip(SMALL_NAMES + ("conv_w",), reduced))
    cw_cols = lax.dynamic_slice_in_dim(small_reduced["conv_w"], me * (D // N_DEV), D // N_DEV, axis=1)

    grads, deltas, new_m, new_v = {}, {}, {}, {}
    for k, s, g_got in zip(BIG_NAMES, sums, others):
        grads[k], deltas[k], new_m[k], new_v[k] = _adamw(weights[k], moms[k], vels[k], s[0], g_got, chip, "adamw_" + k)
    names = SMALL_NAMES + ("conv_w",)
    packed = [_pack_small([src[k] for k in names]) for src in (weights, moms, vels)]
    g_small = _pack_small([small_reduced[k] for k in SMALL_NAMES] + [cw_cols])
    upd = _adamw(packed[0], packed[1], packed[2], g_small[None], None, jnp.zeros((1,), jnp.int32), "adamw_small")
    shapes = [weights[k].shape for k in names]
    for dst, arr in zip((grads, deltas, new_m, new_v), upd):
        dst.update(zip(names, _unpack_small(arr, shapes)))

    return (loss, grad_x.reshape(1, S, D), *[grads[k] for k in WEIGHT_ORDER], *[deltas[k] for k in WEIGHT_ORDER],
            *[new_m[k] for k in WEIGHT_ORDER], *[new_v[k] for k in WEIGHT_ORDER])
```

```python
import functools
import math

import jax
import jax.numpy as jnp
from jax import lax
from jax.experimental import pallas as pl
from jax.experimental.pallas import tpu as pltpu

F32 = jnp.float32
BF16 = jnp.bfloat16

D_MODEL = 1024
N_HEADS = 8
HEAD_DIM = 128
D_FF = 4096
LRU_BLOCKS = 16
LRU_BW = 64
LRU_C = 8.0
CONV_W = 4
RMS_EPS = 1e-6
N_DEV = 8
LANES = 128
SUBLANES = 8
N_GROUPS = D_MODEL // LANES
VMEM_LIMIT_BYTES = 52 * 1024 * 1024
ATTN_SCALE = 1.0 / math.sqrt(HEAD_DIM)
LOG2E = math.log2(math.e)
NEG_BIG = -1e30
ADAM_LR = 0.001
ADAM_B1 = 0.9
ADAM_B2 = 0.999
ADAM_EPS = 1e-08
ADAM_WD = 0.01
ADAM_STEP = 10
ATTN_BLOCK = 1024
ATTN_STRIP = 256
LRU_CHUNK = 256
ROW_TILE = 512
MESH_AXES = ("x", "y", "c")
MESH_ID = pl.DeviceIdType.MESH
ANY = pl.BlockSpec(memory_space=pl.ANY)

NT_DIMS = (((1,), (1,)), ((), ()))
TN_DIMS = (((0,), (0,)), ((), ()))
NN_DIMS = (((1,), (0,)), ((), ()))


def _cparams(*sem):
    return pltpu.CompilerParams(dimension_semantics=sem if sem else None, vmem_limit_bytes=VMEM_LIMIT_BYTES)


def _sigmoid(x):
    return 0.5 * (jnp.tanh(0.5 * x) + 1.0)


def _log1p_pos(e):
    u = 1.0 + e
    return jnp.where(u == 1.0, e, jnp.log(u) * (e / (u - 1.0)))


def _softplus(z):
    return jnp.maximum(z, 0.0) + _log1p_pos(jnp.exp(-jnp.abs(z)))


def _expm1_neg(x):
    series = x * (1.0 + x * 0.5 * (1.0 + x * (1.0 / 3.0) * (1.0 + x * 0.25)))
    return jnp.where(x > -0.03, series, jnp.exp(x) - 1.0)


GELU_C = math.sqrt(2.0 / math.pi)
GELU_K = 0.044715


def _gelu(x):
    return 0.5 * x * (1.0 + jnp.tanh(GELU_C * (x + GELU_K * (x * x * x))))


def _gelu_and_grad(x):
    t = jnp.tanh(GELU_C * (x + GELU_K * (x * x * x)))
    g = 0.5 * x * (1.0 + t)
    dg = 0.5 * (1.0 + t) + 0.5 * x * (1.0 - t * t) * (GELU_C * (1.0 + 3.0 * GELU_K * (x * x)))
    return g, dg


def _mm(pairs, *, ta=False, tb=False, tm, tn, tks, outs, name, epi=None, extra=()):
    n_pairs, n_extra, n_out = len(pairs), len(extra), len(outs)
    tas = list(ta) if isinstance(ta, (list, tuple)) else [ta] * n_pairs
    tbs = list(tb) if isinstance(tb, (list, tuple)) else [tb] * n_pairs
    a0, b0 = pairs[0]
    M = a0.shape[1] if tas[0] else a0.shape[0]
    N = b0.shape[0] if tbs[0] else b0.shape[1]
    tm, tn = min(tm, M), min(tn, N)
    nks, offs = [], []
    for (a, b), tk, pta in zip(pairs, tks, tas):
        K = a.shape[0] if pta else a.shape[1]
        assert K % tk == 0 and M % tm == 0 and N % tn == 0
        offs.append(sum(nks))
        nks.append(K // tk)
    nk_total = sum(nks)
    dims = [(((0 if pta else 1,), (1 if ptb else 0,)), ((), ())) for pta, ptb in zip(tas, tbs)]

    def kmap(off, nk):
        return lambda k: jnp.clip(k - off, 0, nk - 1)

    in_specs, operands = [], []
    for (a, b), tk, off, nk, pta, ptb in zip(pairs, tks, offs, nks, tas, tbs):
        km = kmap(off, nk)
        if pta:
            in_specs.append(pl.BlockSpec((tk, tm), lambda i, j, k, km=km: (km(k), i)))
        else:
            in_specs.append(pl.BlockSpec((tm, tk), lambda i, j, k, km=km: (i, km(k))))
        if ptb:
            in_specs.append(pl.BlockSpec((tn, tk), lambda i, j, k, km=km: (j, km(k))))
        else:
            in_specs.append(pl.BlockSpec((tk, tn), lambda i, j, k, km=km: (km(k), j)))
        operands += [a, b]
    for e in extra:
        in_specs.append(pl.BlockSpec((tm, tn), lambda i, j, k: (i, j)))
        operands.append(e)

    def body(*refs):
        ab = refs[:2 * n_pairs]
        ex = refs[2 * n_pairs:2 * n_pairs + n_extra]
        o = refs[2 * n_pairs + n_extra:2 * n_pairs + n_extra + n_out]
        k = pl.program_id(2)

        def finish(acc):
            res = epi(acc, *[e[...] for e in ex]) if epi is not None else (acc,)
            for r, oref in zip(res, o):
                oref[...] = r.astype(oref.dtype)

        if nk_total == 1:
            finish(lax.dot_general(ab[0][...], ab[1][...], dims[0], preferred_element_type=F32))
            return
        acc = refs[-1]
        for p in range(n_pairs):
            a_ref, b_ref = ab[2 * p], ab[2 * p + 1]

            @pl.when((k >= offs[p]) & (k < offs[p] + nks[p]))
            def _(a_ref=a_ref, b_ref=b_ref, pdims=dims[p]):
                prod = lax.dot_general(a_ref[...], b_ref[...], pdims, preferred_element_type=F32)

                @pl.when(k == 0)
                def _():
                    acc[...] = prod

                @pl.when(k > 0)
                def _():
                    acc[...] += prod

        @pl.when(k == nk_total - 1)
        def _():
            finish(acc[...])

    return pl.pallas_call(
        body,
        name=name,
        grid=(M // tm, N // tn, nk_total),
        in_specs=in_specs,
        out_specs=[pl.BlockSpec((tm, tn), lambda i, j, k: (i, j)) for _ in outs],
        out_shape=[jax.ShapeDtypeStruct((M, N), dt) for dt in outs],
        scratch_shapes=[] if nk_total == 1 else [pltpu.VMEM((tm, tn), F32)],
        compiler_params=_cparams("parallel", "parallel", "arbitrary"),
    )(*operands)


def _norm_fwd(x, g, name):
    S, D = x.shape
    tr = min(ROW_TILE, S)

    def body(x_ref, g_ref, o_ref):
        xv = x_ref[...]
        r = lax.rsqrt(jnp.mean(xv * xv, axis=-1, keepdims=True) + RMS_EPS)
        o_ref[...] = ((xv * r) * g_ref[...]).astype(o_ref.dtype)

    return pl.pallas_call(
        body, name=name, grid=(S // tr,),
        in_specs=[pl.BlockSpec((tr, D), lambda i: (i, 0)), pl.BlockSpec((1, D), lambda i: (0, 0))],
        out_specs=pl.BlockSpec((tr, D), lambda i: (i, 0)),
        out_shape=jax.ShapeDtypeStruct((S, D), BF16),
        compiler_params=_cparams("parallel"),
    )(x, g.reshape(1, D))


def _rms_bwd_rows(dy, xv, g):
    r = lax.rsqrt(jnp.mean(xv * xv, axis=-1, keepdims=True) + RMS_EPS)
    xn = xv * r
    dxn = dy * g
    dx = r * (dxn - xn * jnp.mean(dxn * xn, axis=-1, keepdims=True))
    dg = jnp.sum(dy * xn, axis=0, keepdims=True)
    return dx, dg


def _norm_bwd(dy, x, g, dres, name):
    S, D = x.shape
    tr = min(ROW_TILE, S)

    def body(dy_ref, x_ref, g_ref, dres_ref, dx_ref, dxb_ref, dg_ref):
        dx, dg = _rms_bwd_rows(dy_ref[...], x_ref[...], g_ref[...])
        dx = dres_ref[...] + dx
        dx_ref[...] = dx
        dxb_ref[...] = dx.astype(BF16)

        @pl.when(pl.program_id(0) == 0)
        def _():
            dg_ref[...] = jnp.zeros_like(dg_ref)

        dg_ref[...] += dg

    row = pl.BlockSpec((tr, D), lambda i: (i, 0))
    vec = pl.BlockSpec((1, D), lambda i: (0, 0))
    return pl.pallas_call(
        body, name=name, grid=(S // tr,),
        in_specs=[row, row, vec, row],
        out_specs=[row, row, vec],
        out_shape=[jax.ShapeDtypeStruct((S, D), F32), jax.ShapeDtypeStruct((S, D), BF16),
                   jax.ShapeDtypeStruct((1, D), F32)],
        compiler_params=_cparams("arbitrary"),
    )(dy, x, g.reshape(1, D), dres)


def _final_norm_loss(x2, target, g):
    S, D = x2.shape
    tr = min(ROW_TILE, S)

    def body(x_ref, t_ref, g_ref, loss_ref, dg_ref, dx_ref, dxb_ref):
        xv = x_ref[...]
        gv = g_ref[...]
        r = lax.rsqrt(jnp.mean(xv * xv, axis=-1, keepdims=True) + RMS_EPS)
        y = (xv * r) * gv
        err = y - t_ref[...]
        part = 0.5 * jnp.sum(jnp.mean(err * err, axis=-1, keepdims=True), axis=0, keepdims=True)
        dy = err * (1.0 / D)
        dx, dg = _rms_bwd_rows(dy, xv, gv)
        dx_ref[...] = dx
        dxb_ref[...] = dx.astype(BF16)

        @pl.when(pl.program_id(0) == 0)
        def _():
            dg_ref[...] = jnp.zeros_like(dg_ref)
            loss_ref[...] = jnp.zeros_like(loss_ref)

        dg_ref[...] += dg
        loss_ref[...] += jnp.broadcast_to(part, loss_ref.shape)

    row = pl.BlockSpec((tr, D), lambda i: (i, 0))
    vec = pl.BlockSpec((1, D), lambda i: (0, 0))
    return pl.pallas_call(
        body, name="final_norm_loss", grid=(S // tr,),
        in_specs=[row, row, vec],
        out_specs=[pl.BlockSpec((SUBLANES, LANES), lambda i: (0, 0)), vec, row, row],
        out_shape=[jax.ShapeDtypeStruct((SUBLANES, LANES), F32), jax.ShapeDtypeStruct((1, D), F32),
                   jax.ShapeDtypeStruct((S, D), F32), jax.ShapeDtypeStruct((S, D), BF16)],
        compiler_params=_cparams("arbitrary"),
    )(x2, target, g.reshape(1, D))


def _lru_gates(xa, bd_j, ba_j, bx_j, sp_j):
    z = jnp.dot(xa.astype(BF16), bd_j, preferred_element_type=F32)
    r = _sigmoid(z[:, :LANES] + ba_j)
    ig = _sigmoid(z[:, LANES:] + bx_j)
    log_a = (-LRU_C) * r * sp_j
    a = jnp.exp(log_a)
    mult = jnp.sqrt(-_expm1_neg(2.0 * log_a))
    return r, ig, a, mult


def _conv_rows(xpad, cw_ref, cb_ref, sl, tc):
    out = jnp.broadcast_to(cb_ref[:, sl], (tc, LANES))
    for k in range(CONV_W):
        out = out + xpad[pl.ds(SUBLANES - (CONV_W - 1) + k, tc), sl] * cw_ref[k:k + 1, sl]
    return out


def _lru_fwd(xg, cw, cb, bd, ba, bx, lam):
    S = xg.shape[0]
    D = D_MODEL
    tc = min(LRU_CHUNK, S)
    hb = tc // SUBLANES

    def body(xl_ref, halo_ref, g_ref, cw_ref, cb_ref, bd_ref, ba_ref, bx_ref, lam_ref,
             h_ref, y_ref, xpad, a_s, b_s, carry):
        i = pl.program_id(0)

        @pl.when(i == 0)
        def _():
            carry[...] = jnp.zeros_like(carry)

        xpad[0:SUBLANES, :] = jnp.where(i > 0, halo_ref[...], 0.0)
        xpad[SUBLANES:, :] = xl_ref[...]
        for j in range(N_GROUPS):
            sl = slice(LANES * j, LANES * (j + 1))
            xa = _conv_rows(xpad, cw_ref, cb_ref, sl, tc)
            sp = _softplus(-lam_ref[:, sl])
            _, ig, a, mult = _lru_gates(xa, bd_ref[j], ba_ref[:, sl], bx_ref[:, sl], sp)
            a_s[:, sl] = a
            b_s[:, sl] = mult * (ig * xa)

        row = lax.broadcasted_iota(jnp.int32, (SUBLANES, D), 0)

        def step(t, c):
            o = pl.multiple_of(t * SUBLANES, SUBLANES)
            A = a_s[pl.ds(o, SUBLANES), :]
            B = b_s[pl.ds(o, SUBLANES), :]
            for d in (1, 2, 4):
                keep = row >= d
                a_sh = jnp.where(keep, pltpu.roll(A, d, 0), 1.0)
                b_sh = jnp.where(keep, pltpu.roll(B, d, 0), 0.0)
                B = A * b_sh + B
                A = A * a_sh
            hh = A * c + B
            h_ref[pl.ds(o, SUBLANES), :] = hh
            return jnp.broadcast_to(hh[SUBLANES - 1:SUBLANES, :], (SUBLANES, D))

        carry[...] = lax.fori_loop(0, hb, step, carry[...])
        y_ref[...] = (_gelu(g_ref[...]) * h_ref[...]).astype(BF16)

    row_spec = lambda col: pl.BlockSpec((tc, D), lambda i, col=col: (i, col))
    halo = pl.BlockSpec((SUBLANES, D), lambda i: (jnp.maximum(i * hb - 1, 0), 0))
    full = lambda shape: pl.BlockSpec(shape, lambda i: tuple(0 for _ in shape))
    return pl.pallas_call(
        body, name="lru_fwd", grid=(S // tc,),
        in_specs=[row_spec(0), halo, row_spec(1), full((CONV_W, D)), full((1, D)),
                  full((N_GROUPS, LANES, 2 * LANES)), full((1, D)), full((1, D)), full((1, D))],
        out_specs=[pl.BlockSpec((tc, D), lambda i: (i, 0)), pl.BlockSpec((tc, D), lambda i: (i, 0))],
        out_shape=[jax.ShapeDtypeStruct((S, D), F32), jax.ShapeDtypeStruct((S, D), BF16)],
        scratch_shapes=[pltpu.VMEM((tc + SUBLANES, D), F32), pltpu.VMEM((tc, D), F32),
                        pltpu.VMEM((tc, D), F32), pltpu.VMEM((SUBLANES, D), F32)],
        compiler_params=_cparams("arbitrary"),
    )(xg, xg, xg, cw, cb, bd, ba, bx, lam)


def _lru_bwd(xg, h, dyain, cw, cb, bd, ba, bx, lam):
    S = xg.shape[0]
    D = D_MODEL
    tc = min(LRU_CHUNK, S)
    hb = tc // SUBLANES
    nc = S // tc

    def body(xl_ref, xhalo_ref, g_ref, h_ref, hhalo_ref, dy_ref, cw_ref, cb_ref, bd_ref, ba_ref, bx_ref,
             lam_ref, dxg_ref, dcw_ref, dcb_ref, dba_ref, dbx_ref, dlam_ref, dbd_ref,
             xpad, hpad, a_s, b_s, dh_s, g_s, xa_s, r_s, ig_s, m_s, dxa_pad, carry_e, dxa_head):
        i = pl.program_id(0)
        c = nc - 1 - i

        @pl.when(i == 0)
        def _():
            carry_e[...] = jnp.zeros_like(carry_e)
            dxa_head[...] = jnp.zeros_like(dxa_head)
            for ref in (dcw_ref, dcb_ref, dba_ref, dbx_ref, dlam_ref, dbd_ref):
                ref[...] = jnp.zeros_like(ref)

        xpad[0:SUBLANES, :] = jnp.where(c > 0, xhalo_ref[...], 0.0)
        xpad[SUBLANES:, :] = xl_ref[...]
        hpad[0:SUBLANES, :] = jnp.where(c > 0, hhalo_ref[...], 0.0)
        hpad[SUBLANES:, :] = h_ref[...]

        for j in range(N_GROUPS):
            sl = slice(LANES * j, LANES * (j + 1))
            xa = _conv_rows(xpad, cw_ref, cb_ref, sl, tc)
            sp = _softplus(-lam_ref[:, sl])
            r, ig, a, mult = _lru_gates(xa, bd_ref[j], ba_ref[:, sl], bx_ref[:, sl], sp)
            gl, dgl = _gelu_and_grad(g_ref[:, sl])
            dy = dy_ref[:, sl]
            dh = dy * gl
            dxg_ref[:, D + LANES * j:D + LANES * (j + 1)] = (dy * h_ref[:, sl] * dgl).astype(BF16)
            a_s[:, sl] = a
            b_s[:, sl] = a * dh
            dh_s[:, sl] = dh
            xa_s[:, sl] = xa
            r_s[:, sl] = r
            ig_s[:, sl] = ig
            m_s[:, sl] = mult

        row = lax.broadcasted_iota(jnp.int32, (SUBLANES, D), 0)

        def step(tt, ce):
            o = pl.multiple_of((hb - 1 - tt) * SUBLANES, SUBLANES)
            A = a_s[pl.ds(o, SUBLANES), :]
            B = b_s[pl.ds(o, SUBLANES), :]
            for d in (1, 2, 4):
                keep = row < SUBLANES - d
                a_sh = jnp.where(keep, pltpu.roll(A, SUBLANES - d, 0), 1.0)
                b_sh = jnp.where(keep, pltpu.roll(B, SUBLANES - d, 0), 0.0)
                B = A * b_sh + B
                A = A * a_sh
            e = A * ce + B
            e_next = jnp.where(row < SUBLANES - 1, pltpu.roll(e, SUBLANES - 1, 0), ce)
            g_s[pl.ds(o, SUBLANES), :] = dh_s[pl.ds(o, SUBLANES), :] + e_next
            return jnp.broadcast_to(e[0:1, :], (SUBLANES, D))

        carry_e[...] = lax.fori_loop(0, hb, step, carry_e[...])

        for j in range(N_GROUPS):
            sl = slice(LANES * j, LANES * (j + 1))
            gg = g_s[:, sl]
            xa, r, ig, mult, a = xa_s[:, sl], r_s[:, sl], ig_s[:, sl], m_s[:, sl], a_s[:, sl]
            hprev = hpad[pl.ds(SUBLANES - 1, tc), sl]
            sp = _softplus(-lam_ref[:, sl])
            da = gg * hprev
            dmult = gg * (ig * xa)
            dig = gg * (mult * xa)
            dxa = gg * (mult * ig)
            dla = da * a - dmult * ((a * a) / mult)
            dr = dla * ((-LRU_C) * sp)
            dlam_ref[:, sl] += jnp.sum(dla * r, axis=0, keepdims=True)
            dza = dr * r * (1.0 - r)
            dzx = dig * ig * (1.0 - ig)
            dba_ref[:, sl] += jnp.sum(dza, axis=0, keepdims=True)
            dbx_ref[:, sl] += jnp.sum(dzx, axis=0, keepdims=True)
            dz = jnp.concatenate([dza, dzx], axis=1).astype(BF16)
            dbd_ref[j] += lax.dot_general(xa.astype(BF16), dz, TN_DIMS, preferred_element_type=F32)
            dxa = dxa + lax.dot_general(dz, bd_ref[j], NT_DIMS, preferred_element_type=F32)
            dxa_pad[0:tc, sl] = dxa

        dxa_pad[tc:, :] = dxa_head[...]
        dxa_head[...] = dxa_pad[0:SUBLANES, :]

        for j in range(N_GROUPS):
            sl = slice(LANES * j, LANES * (j + 1))
            dxa = dxa_pad[0:tc, sl]
            dxl = jnp.zeros((tc, LANES), F32)
            for k in range(CONV_W):
                dxl = dxl + dxa_pad[pl.ds(CONV_W - 1 - k, tc), sl] * cw_ref[k:k + 1, sl]
                dcw_ref[k:k + 1, sl] += jnp.sum(
                    dxa * xpad[pl.ds(SUBLANES - (CONV_W - 1) + k, tc), sl], axis=0, keepdims=True)
            dxg_ref[:, sl] = dxl.astype(BF16)
            dcb_ref[:, sl] += jnp.sum(dxa, axis=0, keepdims=True)

        @pl.when(i == nc - 1)
        def _():
            dlam_ref[...] = dlam_ref[...] * (LRU_C * _sigmoid(-lam_ref[...]))

    rev = lambda col: pl.BlockSpec((tc, D), lambda i, col=col: (nc - 1 - i, col))
    halo = pl.BlockSpec((SUBLANES, D), lambda i: (jnp.maximum((nc - 1 - i) * hb - 1, 0), 0))
    full = lambda shape: pl.BlockSpec(shape, lambda i: tuple(0 for _ in shape))
    big = lambda: pltpu.VMEM((tc, D), F32)
    return pl.pallas_call(
        body, name="lru_bwd", grid=(nc,),
        in_specs=[rev(0), halo, rev(1), rev(0), halo, rev(0), full((CONV_W, D)), full((1, D)),
                  full((N_GROUPS, LANES, 2 * LANES)), full((1, D)), full((1, D)), full((1, D))],
        out_specs=[pl.BlockSpec((tc, 2 * D), lambda i: (nc - 1 - i, 0)), full((CONV_W, D)), full((1, D)),
                   full((1, D)), full((1, D)), full((1, D)), full((N_GROUPS, LANES, 2 * LANES))],
        out_shape=[jax.ShapeDtypeStruct((S, 2 * D), BF16), jax.ShapeDtypeStruct((CONV_W, D), F32),
                   jax.ShapeDtypeStruct((1, D), F32), jax.ShapeDtypeStruct((1, D), F32),
                   jax.ShapeDtypeStruct((1, D), F32), jax.ShapeDtypeStruct((1, D), F32),
                   jax.ShapeDtypeStruct((N_GROUPS, LANES, 2 * LANES), F32)],
        scratch_shapes=[pltpu.VMEM((tc + SUBLANES, D), F32), pltpu.VMEM((tc + SUBLANES, D), F32),
                        big(), big(), big(), big(), big(), big(), big(), big(),
                        pltpu.VMEM((tc + SUBLANES, D), F32), pltpu.VMEM((SUBLANES, D), F32),
                        pltpu.VMEM((SUBLANES, D), F32)],
        compiler_params=_cparams("arbitrary"),
    )(xg, xg, xg, h, h, dyain, cw, cb, bd, ba, bx, lam)


def _forget_cumsum(fl, fb):
    S = fl.shape[0]
    tr = min(ROW_TILE, S)
    hb = tr // SUBLANES

    def body(fl_ref, fb_ref, o_ref, rep_ref, lf_s, carry):
        @pl.when(pl.program_id(0) == 0)
        def _():
            carry[...] = jnp.zeros_like(carry)

        lf_s[...] = -_softplus(-(fl_ref[...] + fb_ref[...]))
        row = lax.broadcasted_iota(jnp.int32, (SUBLANES, LANES), 0)

        def step(t, c):
            o = pl.multiple_of(t * SUBLANES, SUBLANES)
            B = lf_s[pl.ds(o, SUBLANES), :]
            for d in (1, 2, 4):
                B = B + jnp.where(row >= d, pltpu.roll(B, d, 0), 0.0)
            B = B + c
            o_ref[pl.ds(o, SUBLANES), :] = B * LOG2E
            return jnp.broadcast_to(B[SUBLANES - 1:SUBLANES, :], (SUBLANES, LANES))

        carry[...] = lax.fori_loop(0, hb, step, carry[...])
        for h in range(N_HEADS):
            rep_ref[h] = jnp.broadcast_to(o_ref[:, h:h + 1], (tr, LANES))

    return pl.pallas_call(
        body, name="forget_cumsum", grid=(S // tr,),
        in_specs=[pl.BlockSpec((tr, LANES), lambda i: (i, 0)), pl.BlockSpec((1, LANES), lambda i: (0, 0))],
        out_specs=[pl.BlockSpec((tr, LANES), lambda i: (i, 0)),
                   pl.BlockSpec((N_HEADS, tr, LANES), lambda i: (0, i, 0))],
        out_shape=[jax.ShapeDtypeStruct((S, LANES), F32), jax.ShapeDtypeStruct((N_HEADS, S, LANES), F32)],
        scratch_shapes=[pltpu.VMEM((tr, LANES), F32), pltpu.VMEM((SUBLANES, LANES), F32)],
        compiler_params=_cparams("arbitrary"),
    )(fl, fb)


def _forget_bwd(dF, fl, fb):
    S = fl.shape[0]
    tr = min(ROW_TILE, S)
    hb = tr // SUBLANES
    nc = S // tr

    def body(df_ref, fl_ref, fb_ref, o_ref, dfb_ref, carry):
        @pl.when(pl.program_id(0) == 0)
        def _():
            carry[...] = jnp.zeros_like(carry)
            dfb_ref[...] = jnp.zeros_like(dfb_ref)

        row = lax.broadcasted_iota(jnp.int32, (SUBLANES, LANES), 0)

        def step(tt, carried):
            c, acc = carried
            o = pl.multiple_of((hb - 1 - tt) * SUBLANES, SUBLANES)
            B = df_ref[pl.ds(o, SUBLANES), :]
            for d in (1, 2, 4):
                B = B + jnp.where(row < SUBLANES - d, pltpu.roll(B, SUBLANES - d, 0), 0.0)
            B = B + c
            z = fl_ref[pl.ds(o, SUBLANES), :] + fb_ref[...]
            dz = B * _sigmoid(-z)
            o_ref[pl.ds(o, SUBLANES), :] = dz.astype(BF16)
            return jnp.broadcast_to(B[0:1, :], (SUBLANES, LANES)), acc + dz

        c, acc = lax.fori_loop(0, hb, step, (carry[...], jnp.zeros((SUBLANES, LANES), F32)))
        carry[...] = c
        dfb_ref[...] += jnp.sum(acc, axis=0, keepdims=True)

    rev = pl.BlockSpec((tr, LANES), lambda i: (nc - 1 - i, 0))
    vec = pl.BlockSpec((1, LANES), lambda i: (0, 0))
    return pl.pallas_call(
        body, name="forget_bwd", grid=(nc,),
        in_specs=[rev, rev, vec],
        out_specs=[rev, vec],
        out_shape=[jax.ShapeDtypeStruct((S, LANES), BF16), jax.ShapeDtypeStruct((1, LANES), F32)],
        scratch_shapes=[pltpu.VMEM((SUBLANES, LANES), F32)],
        compiler_params=_cparams("arbitrary"),
    )(dF, fl, fb)


def _triangle(n, key_major):
    pairs = [(q, k) for q in range(n) for k in range(q + 1)]
    if key_major:
        pairs.sort(key=lambda qk: (qk[1], qk[0]))
    return (jnp.asarray([q for q, _ in pairs], jnp.int32), jnp.asarray([k for _, k in pairs], jnp.int32))


def _strip_scores(k_ref, qt_ref, fk_ref, j, strip, nkeys, diagonal):
    cols = slice(strip * j, strip * (j + 1))
    s = jnp.dot(k_ref[0:nkeys, :], qt_ref[:, cols], preferred_element_type=F32) * (ATTN_SCALE * LOG2E)
    fk = fk_ref[0:nkeys, :]
    s = s - jnp.concatenate([fk] * (strip // LANES), axis=1)
    keep = None
    if diagonal:
        keys = lax.broadcasted_iota(jnp.int32, (nkeys, strip), 0)
        queries = lax.broadcasted_iota(jnp.int32, (nkeys, strip), 1) + strip * j
        keep = keys <= queries
    return s, keep


def _attn_fwd(kv, qkv_t, f_row, f_rep):
    S = kv.shape[0]
    blk = min(ATTN_BLOCK, S)
    strip = min(ATTN_STRIP, blk)
    n = S // blk
    tri_q, tri_k = _triangle(n, key_major=False)
    ones_rows = 2 * SUBLANES

    def body(tq_ref, tk_ref, k_ref, qt_ref, vt_ref, fq_ref, fk_ref, ot_ref, lse_ref, m_s, acc_s, vta_s):
        t = pl.program_id(1)
        qi, ki = tq_ref[t], tk_ref[t]

        @pl.when(ki == 0)
        def _():
            m_s[...] = jnp.full_like(m_s, NEG_BIG)
            acc_s[...] = jnp.zeros_like(acc_s)

        vta_s[0:HEAD_DIM, :] = vt_ref[...]
        vta_s[HEAD_DIM:, :] = jnp.ones((ones_rows, blk), BF16)

        def update(diagonal):
            n_strips = blk // strip
            keys_of = lambda j: strip * (j + 1) if diagonal else blk
            scores = lambda j: _strip_scores(k_ref, qt_ref, fk_ref, j, strip, keys_of(j), diagonal)
            def weighted_values(j, alpha, pb):
                cols = slice(strip * j, strip * (j + 1))
                acc_s[:, cols] = alpha * acc_s[:, cols] + jnp.dot(
                    vta_s[:, 0:keys_of(j)], pb, preferred_element_type=F32)

            ahead, behind = scores(0), None
            for j in range(n_strips):
                cols = slice(strip * j, strip * (j + 1))
                (s, keep), ahead = ahead, (scores(j + 1) if j + 1 < n_strips else None)
                if behind is not None:
                    weighted_values(*behind)
                if diagonal:
                    s = jnp.where(keep, s, NEG_BIG)
                fq = fq_ref[:, cols]
                m_old = m_s[:, cols]
                m_new = jnp.maximum(m_old, jnp.max(s, axis=0, keepdims=True) + fq)
                p = jnp.exp2(s - (m_new - fq))
                behind = (j, jnp.exp2(m_old - m_new), p.astype(BF16))
                m_s[:, cols] = m_new
            weighted_values(*behind)

        @pl.when(ki < qi)
        def _():
            update(False)

        @pl.when(ki == qi)
        def _():
            update(True)
            denom = acc_s[HEAD_DIM:HEAD_DIM + 1, :]
            ot_ref[...] = (acc_s[0:HEAD_DIM, :] / denom).astype(BF16)
            lse_ref[...] = m_s[...] + jnp.log2(denom)

    return pl.pallas_call(
        body, name="attn_fwd",
        grid_spec=pltpu.PrefetchScalarGridSpec(
            num_scalar_prefetch=2, grid=(N_HEADS, tri_q.shape[0]),
            in_specs=[pl.BlockSpec((blk, HEAD_DIM), lambda h, t, tq, tk: (tk[t], h)),
                      pl.BlockSpec((HEAD_DIM, blk), lambda h, t, tq, tk: (h, tq[t])),
                      pl.BlockSpec((HEAD_DIM, blk), lambda h, t, tq, tk: (2 * N_HEADS + h, tk[t])),
                      pl.BlockSpec((None, 1, blk), lambda h, t, tq, tk: (h, 0, tq[t])),
                      pl.BlockSpec((None, blk, LANES), lambda h, t, tq, tk: (h, tk[t], 0))],
            out_specs=[pl.BlockSpec((HEAD_DIM, blk), lambda h, t, tq, tk: (h, tq[t])),
                       pl.BlockSpec((None, 1, blk), lambda h, t, tq, tk: (h, 0, tq[t]))],
            scratch_shapes=[pltpu.VMEM((1, blk), F32), pltpu.VMEM((HEAD_DIM + ones_rows, blk), F32),
                            pltpu.VMEM((HEAD_DIM + ones_rows, blk), BF16)]),
        out_shape=[jax.ShapeDtypeStruct((N_HEADS * HEAD_DIM, S), BF16), jax.ShapeDtypeStruct((N_HEADS, 1, S), F32)],
        compiler_params=_cparams("parallel", "arbitrary"),
    )(tri_q, tri_k, kv, qkv_t, qkv_t, f_row, f_rep)


def _attn_bwd(kv, qkv_t, do_t, o_t, lse, f_row, f_rep):
    S = kv.shape[0]
    blk = min(ATTN_BLOCK, S)
    strip = min(ATTN_STRIP, blk)
    n = S // blk
    tri_q, tri_k = _triangle(n, key_major=True)
    n_tiles = tri_q.shape[0]

    def body(tq_ref, tk_ref, k_ref, v_ref, qt_ref, kt_ref, dot_ref, ot_ref, lse_ref, fq_ref, fk_ref,
             dqt_ref, dkt_ref, dvt_ref, dfk_ref, dfq_ref, dq_s, dk_s, dv_s, dfk_s, dfq_s, row_s):
        t = pl.program_id(1)
        qi, ki = tq_ref[t], tk_ref[t]

        @pl.when(t == 0)
        def _():
            dq_s[...] = jnp.zeros_like(dq_s)
            dfq_s[...] = jnp.zeros_like(dfq_s)

        @pl.when(qi == ki)
        def _():
            dk_s[...] = jnp.zeros_like(dk_s)
            dv_s[...] = jnp.zeros_like(dv_s)
            dfk_s[...] = jnp.zeros_like(dfk_s)

        def update(diagonal):
            row_s[...] = fq_ref[...] - lse_ref[...]
            n_strips = blk // strip
            keys_of = lambda j: strip * (j + 1) if diagonal else blk

            def matmuls_in(j):
                s, keep = _strip_scores(k_ref, qt_ref, fk_ref, j, strip, keys_of(j), diagonal)
                dp = jnp.dot(v_ref[0:keys_of(j), :], dot_ref[:, strip * j:strip * (j + 1)], preferred_element_type=F32)
                return s, keep, dp

            def matmuls_out(j, pb, dsb):
                cols = slice(strip * j, strip * (j + 1))
                nkeys = keys_of(j)
                dv_s[:, 0:nkeys] += lax.dot_general(dot_ref[:, cols], pb, NT_DIMS, preferred_element_type=F32)
                dk_s[:, 0:nkeys] += lax.dot_general(qt_ref[:, cols], dsb, NT_DIMS, preferred_element_type=F32)
                dq_s[qi, :, cols] += jnp.dot(kt_ref[:, 0:nkeys], dsb, preferred_element_type=F32)

            ahead, behind = matmuls_in(0), None
            for j in range(n_strips):
                cols = slice(strip * j, strip * (j + 1))
                nkeys = keys_of(j)
                (s, keep, dp), ahead = ahead, (matmuls_in(j + 1) if j + 1 < n_strips else None)
                if behind is not None:
                    matmuls_out(*behind)
                p = jnp.exp2(s + row_s[:, cols])
                if diagonal:
                    p = jnp.where(keep, p, 0.0)
                dot = dot_ref[:, cols]
                delta = jnp.sum(dot.astype(F32) * ot_ref[:, cols].astype(F32), axis=0, keepdims=True)
                ds = p * (dp - delta)
                behind = (j, p.astype(BF16), ds.astype(BF16))
                lane_part = ds[:, 0:LANES]
                for g in range(1, strip // LANES):
                    lane_part = lane_part + ds[:, LANES * g:LANES * (g + 1)]
                dfk_s[0:nkeys, :] += lane_part
                sub_part = ds[0:SUBLANES, :]
                for g in range(1, nkeys // SUBLANES):
                    sub_part = sub_part + ds[SUBLANES * g:SUBLANES * (g + 1), :]
                dfq_s[qi, :, cols] += sub_part
            matmuls_out(*behind)

        @pl.when(qi == ki)
        def _():
            update(True)

        @pl.when(qi > ki)
        def _():
            update(False)

        @pl.when(qi == n - 1)
        def _():
            dkt_ref[...] = (dk_s[...] * ATTN_SCALE).astype(BF16)
            dvt_ref[...] = dv_s[...].astype(BF16)
            dfk_ref[...] = -jnp.sum(dfk_s[...], axis=-1, keepdims=True)

        @pl.when(t == n_tiles - 1)
        def _():
            for j in range(n):
                dqt_ref[:, blk * j:blk * (j + 1)] = (dq_s[j] * ATTN_SCALE).astype(BF16)
                dfq_ref[:, blk * j:blk * (j + 1)] = jnp.sum(dfq_s[j], axis=0, keepdims=True)

    q_feat = pl.BlockSpec((HEAD_DIM, blk), lambda h, t, tq, tk: (h, tq[t]))
    q_row = pl.BlockSpec((None, 1, blk), lambda h, t, tq, tk: (h, 0, tq[t]))
    k_feat = pl.BlockSpec((HEAD_DIM, blk), lambda h, t, tq, tk: (h, tk[t]))
    return pl.pallas_call(
        body, name="attn_bwd",
        grid_spec=pltpu.PrefetchScalarGridSpec(
            num_scalar_prefetch=2, grid=(N_HEADS, n_tiles),
            in_specs=[pl.BlockSpec((blk, HEAD_DIM), lambda h, t, tq, tk: (tk[t], h)),
                      pl.BlockSpec((blk, HEAD_DIM), lambda h, t, tq, tk: (tk[t], N_HEADS + h)),
                      q_feat,
                      pl.BlockSpec((HEAD_DIM, blk), lambda h, t, tq, tk: (N_HEADS + h, tk[t])),
                      q_feat, q_feat, q_row, q_row,
                      pl.BlockSpec((None, blk, LANES), lambda h, t, tq, tk: (h, tk[t], 0))],
            out_specs=[pl.BlockSpec((HEAD_DIM, S), lambda h, t, tq, tk: (h, 0)), k_feat, k_feat,
                       pl.BlockSpec((None, blk, 1), lambda h, t, tq, tk: (h, tk[t], 0)),
                       pl.BlockSpec((None, 1, S), lambda h, t, tq, tk: (h, 0, 0))],
            scratch_shapes=[pltpu.VMEM((n, HEAD_DIM, blk), F32), pltpu.VMEM((HEAD_DIM, blk), F32),
                            pltpu.VMEM((HEAD_DIM, blk), F32), pltpu.VMEM((blk, LANES), F32),
                            pltpu.VMEM((n, SUBLANES, blk), F32), pltpu.VMEM((1, blk), F32)]),
        out_shape=[jax.ShapeDtypeStruct((N_HEADS * HEAD_DIM, S), BF16)] * 3
        + [jax.ShapeDtypeStruct((N_HEADS, S, 1), F32), jax.ShapeDtypeStruct((N_HEADS, 1, S), F32)],
        compiler_params=_cparams("parallel", "arbitrary"),
    )(tri_q, tri_k, kv, kv, qkv_t, qkv_t, do_t, o_t, lse, f_row, f_rep)


def _gate_mix(gates, ya, yb):
    S, D = ya.shape
    tr = min(ROW_TILE, S)

    def body(ga_ref, gb_ref, ya_ref, yb_ref, o_ref):
        o_ref[...] = (_sigmoid(ga_ref[...]) * ya_ref[...] + _sigmoid(gb_ref[...]) * yb_ref[...]).astype(BF16)

    col = lambda j: pl.BlockSpec((tr, D), lambda i, j=j: (i, j))
    return pl.pallas_call(
        body, name="gate_mix", grid=(S // tr,),
        in_specs=[col(0), col(1), col(0), col(0)],
        out_specs=col(0),
        out_shape=jax.ShapeDtypeStruct((S, D), BF16),
        compiler_params=_cparams("parallel"),
    )(gates, gates, ya, yb)


def _gate_bwd(dmix, gates, ya, yb):
    S, D = ya.shape
    tr = min(ROW_TILE, S)

    def body(dm_ref, ga_ref, gb_ref, ya_ref, yb_ref, dya_ref, dyb_ref, dg_ref):
        dm = dm_ref[...]
        sa, sb = _sigmoid(ga_ref[...]), _sigmoid(gb_ref[...])
        dya_ref[...] = (dm * sa).astype(BF16)
        dyb_ref[...] = (dm * sb).astype(BF16)
        dg_ref[:, 0:D] = ((dm * ya_ref[...]) * (sa * (1.0 - sa))).astype(BF16)
        dg_ref[:, D:] = ((dm * yb_ref[...]) * (sb * (1.0 - sb))).astype(BF16)

    col = lambda j: pl.BlockSpec((tr, D), lambda i, j=j: (i, j))
    return pl.pallas_call(
        body, name="gate_bwd", grid=(S // tr,),
        in_specs=[col(0), col(0), col(1), col(0), col(0)],
        out_specs=[col(0), col(0), pl.BlockSpec((tr, 2 * D), lambda i: (i, 0))],
        out_shape=[jax.ShapeDtypeStruct((S, D), BF16), jax.ShapeDtypeStruct((S, D), BF16),
                   jax.ShapeDtypeStruct((S, 2 * D), BF16)],
        compiler_params=_cparams("parallel"),
    )(dmix, gates, gates, ya, yb)


def _mesh_place():
    x, y, c = lax.axis_index("x"), lax.axis_index("y"), lax.axis_index("c")
    chips = [(1 - x, y), (x, 1 - y), (1 - x, 1 - y)]
    return x, y, c, chips


def _all_gather(shards):
    n = len(shards)

    def body(*refs):
        ins, outs = refs[:n], refs[n:2 * n]
        send_sems, recv_sems, local_sems = refs[2 * n:]
        x, y, c, chips = _mesh_place()
        me, sib = (x, y, c), (x, y, 1 - c)

        def copy(a, k, block, to, src=None):
            px, py, pc = block
            dst = outs[a].at[4 * px + 2 * py + pc]
            return pltpu.make_async_remote_copy(
                src_ref=dst if src is None else src, dst_ref=dst,
                send_sem=send_sems.at[a, k], recv_sem=recv_sems.at[a, k],
                device_id=to, device_id_type=MESH_ID)

        mine = [pltpu.make_async_copy(ins[a], outs[a].at[4 * x + 2 * y + c], local_sems.at[a]) for a in range(n)]
        for cp in mine:
            cp.start()
        first = []
        for a in range(n):
            first.append(copy(a, 0, me, sib, src=ins[a]))
            for j, chip in enumerate(chips):
                first.append(copy(a, 1 + j, me, (*chip, c), src=ins[a]))
        for cp in first:
            cp.start()
        passed = []
        for j, chip in enumerate(chips):
            for a in range(n):
                copy(a, 1 + j, (*chip, c), me).wait_recv()
                fwd = copy(a, 4 + j, (*chip, c), sib)
                fwd.start()
                passed.append(fwd)
        for a in range(n):
            copy(a, 0, sib, me).wait_recv()
            for j, chip in enumerate(chips):
                copy(a, 4 + j, (*chip, 1 - c), me).wait_recv()
        for cp in first + passed:
            cp.wait_send()
        for cp in mine:
            cp.wait()

    return pl.pallas_call(
        body, name="all_gather_weights",
        in_specs=[ANY] * n, out_specs=[ANY] * n,
        out_shape=[jax.ShapeDtypeStruct((N_DEV,) + s.shape, s.dtype) for s in shards],
        scratch_shapes=[pltpu.SemaphoreType.DMA((n, 7)), pltpu.SemaphoreType.DMA((n, 7)),
                        pltpu.SemaphoreType.DMA((n,))],
    )(*shards)


def _reduce_scatter_cores(grads):
    n = len(grads)

    def body(*refs):
        ins, gots = refs[:n], refs[n:2 * n]
        send_sems, recv_sems = refs[2 * n:]
        x, y, c, _ = _mesh_place()
        sib = (x, y, 1 - c)
        remote = []
        for a in range(n):
            for k in range(4):
                remote.append(pltpu.make_async_remote_copy(
                    src_ref=ins[a].at[2 * k + (1 - c)], dst_ref=gots[a].at[k],
                    send_sem=send_sems.at[a, k], recv_sem=recv_sems.at[a, k],
                    device_id=sib, device_id_type=MESH_ID))
        for cp in remote:
            cp.start()
        for cp in remote:
            cp.wait_recv()
        for cp in remote:
            cp.wait_send()

    return pl.pallas_call(
        body, name="reduce_scatter_cores",
        in_specs=[ANY] * n, out_specs=[ANY] * n,
        out_shape=[jax.ShapeDtypeStruct((4,) + g.shape[1:], g.dtype) for g in grads],
        scratch_shapes=[pltpu.SemaphoreType.DMA((n, 4)), pltpu.SemaphoreType.DMA((n, 4))],
    )(*grads)


def _chip_partial_sum(blocks, got, core):
    R, C = got.shape[1:]
    tr = min(256, R)
    assert R % tr == 0

    def body(core_ref, a_ref, b_ref, s_ref, sb_ref):
        s = a_ref[...] + b_ref[...]
        s_ref[...] = s
        sb_ref[...] = s.astype(BF16)

    blk = pl.BlockSpec((None, tr, C), lambda k, i, core_ref: (k, i, 0))
    return pl.pallas_call(
        body, name="chip_partial_sum",
        grid_spec=pltpu.PrefetchScalarGridSpec(
            num_scalar_prefetch=1, grid=(4, R // tr),
            in_specs=[pl.BlockSpec((None, tr, C), lambda k, i, core_ref: (2 * k + core_ref[0], i, 0)), blk],
            out_specs=[blk, blk]),
        out_shape=[jax.ShapeDtypeStruct(got.shape, F32), jax.ShapeDtypeStruct(got.shape, BF16)],
        compiler_params=_cparams("parallel", "parallel"),
    )(core, blocks, got)


def _reduce_scatter_chips(sums_bf16):
    n = len(sums_bf16)

    def body(*refs):
        bf16s, gots = refs[:n], refs[n:2 * n]
        send_sems, recv_sems = refs[2 * n:]
        x, y, c, chips = _mesh_place()
        remote = []
        for a in range(n):
            for j, (px, py) in enumerate(chips):
                remote.append(pltpu.make_async_remote_copy(
                    src_ref=bf16s[a].at[2 * px + py], dst_ref=gots[a].at[j],
                    send_sem=send_sems.at[a, j], recv_sem=recv_sems.at[a, j],
                    device_id=(px, py, c), device_id_type=MESH_ID))
        for cp in remote:
            cp.start()
        for cp in remote:
            cp.wait_recv()
        for cp in remote:
            cp.wait_send()

    return pl.pallas_call(
        body, name="reduce_scatter_chips",
        in_specs=[ANY] * n, out_specs=[ANY] * n,
        out_shape=[jax.ShapeDtypeStruct((3,) + s.shape[1:], BF16) for s in sums_bf16],
        scratch_shapes=[pltpu.SemaphoreType.DMA((n, 3)), pltpu.SemaphoreType.DMA((n, 3))],
    )(*sums_bf16)


def _all_reduce_small(vec):
    R = vec.shape[0]

    def body(v_ref, o_ref, sib_buf, chip_buf, send_sems, recv_sems):
        x, y, c, chips = _mesh_place()
        swap = pltpu.make_async_remote_copy(
            src_ref=v_ref, dst_ref=sib_buf, send_sem=send_sems.at[0], recv_sem=recv_sems.at[0],
            device_id=(x, y, 1 - c), device_id_type=MESH_ID)
        swap.start()
        swap.wait()
        my_chip = 2 * x + y
        chip_buf[my_chip] = v_ref[...] + sib_buf[...]
        sends = []
        for j, (px, py) in enumerate(chips):
            cp = pltpu.make_async_remote_copy(
                src_ref=chip_buf.at[my_chip], dst_ref=chip_buf.at[my_chip],
                send_sem=send_sems.at[1 + j], recv_sem=recv_sems.at[1 + j],
                device_id=(px, py, c), device_id_type=MESH_ID)
            cp.start()
            sends.append(cp)
        for j, (px, py) in enumerate(chips):
            pltpu.make_async_remote_copy(
                src_ref=chip_buf.at[2 * px + py], dst_ref=chip_buf.at[2 * px + py],
                send_sem=send_sems.at[1 + j], recv_sem=recv_sems.at[1 + j],
                device_id=(px, py, c), device_id_type=MESH_ID).wait_recv()
        for cp in sends:
            cp.wait_send()
        o_ref[...] = ((chip_buf[0] + chip_buf[1]) + chip_buf[2]) + chip_buf[3]

    vm = pl.BlockSpec(memory_space=pltpu.VMEM)
    return pl.pallas_call(
        body, name="all_reduce_small",
        in_specs=[vm], out_specs=vm,
        out_shape=jax.ShapeDtypeStruct(vec.shape, F32),
        scratch_shapes=[pltpu.VMEM((R, LANES), F32), pltpu.VMEM((4, R, LANES), F32),
                        pltpu.SemaphoreType.DMA((4,)), pltpu.SemaphoreType.DMA((4,))],
    )(vec)


def _adamw_math(w, g, m, v):
    m = ADAM_B1 * m + (1.0 - ADAM_B1) * g
    v = ADAM_B2 * v + (1.0 - ADAM_B2) * (g * g)
    m_hat = m / (1.0 - ADAM_B1 ** ADAM_STEP)
    v_hat = v / (1.0 - ADAM_B2 ** ADAM_STEP)
    delta = -ADAM_LR * (m_hat / (jnp.sqrt(v_hat) + ADAM_EPS) + ADAM_WD * w)
    return delta, m, v


def _adamw(w, m, v, g_own, g_got, chip, name):
    R, C = w.shape
    tr = R if R * C <= 256 * D_MODEL else 256
    assert R % tr == 0
    n_got = g_got.shape[0]

    def body(*refs):
        w_ref, m_ref, v_ref, go_ref = refs[1:5]
        got = refs[5:5 + n_got]
        g_ref, d_ref, nm_ref, nv_ref = refs[5 + n_got:]
        g = go_ref[...]
        for r in got:
            g = g + r[...].astype(F32)
        delta, m_new, v_new = _adamw_math(w_ref[...], g, m_ref[...], v_ref[...])
        g_ref[...] = g
        d_ref[...] = delta
        nm_ref[...] = m_new
        nv_ref[...] = v_new

    blk = pl.BlockSpec((tr, C), lambda i, chip_ref: (i, 0))
    own_spec = pl.BlockSpec((None, tr, C), lambda i, chip_ref: (chip_ref[0], i, 0))
    got_specs = [pl.BlockSpec((None, tr, C), lambda i, chip_ref, j=j: (j, i, 0)) for j in range(n_got)]
    return pl.pallas_call(
        body, name=name,
        grid_spec=pltpu.PrefetchScalarGridSpec(
            num_scalar_prefetch=1, grid=(R // tr,),
            in_specs=[blk] * 3 + [own_spec] + got_specs, out_specs=[blk] * 4),
        out_shape=[jax.ShapeDtypeStruct((R, C), F32)] * 4,
        compiler_params=_cparams("parallel"),
    )(chip, w, m, v, g_own, *([g_got] * n_got))


def _block_diag_pairs(wa, wx):
    def pairs(w):
        w = w.reshape(N_GROUPS, 2, LRU_BW, LRU_BW)
        z = jnp.zeros((N_GROUPS, LRU_BW, LRU_BW), w.dtype)
        top = jnp.concatenate([w[:, 0], z], axis=2)
        bot = jnp.concatenate([z, w[:, 1]], axis=2)
        return jnp.concatenate([top, bot], axis=1)
    return jnp.concatenate([pairs(wa), pairs(wx)], axis=2).astype(BF16)


def _block_diag_unpair(dbd):
    def unpair(g):
        blocks = jnp.stack([g[:, :LRU_BW, :LRU_BW], g[:, LRU_BW:, LRU_BW:]], axis=1)
        return blocks.reshape(LRU_BLOCKS, LRU_BW, LRU_BW)
    return unpair(dbd[:, :, :LANES]), unpair(dbd[:, :, LANES:])


def _local_step(x, target, W, small):
    S, D = x.shape
    g1, g2, g3 = small["norm_mix_g"], small["norm_mlp_g"], small["norm_final_g"]
    cw, cb = small["conv_w"], small["conv_b"].reshape(1, D)
    ba, bx, lam = (small[k].reshape(1, D) for k in ("lru_ba", "lru_bx", "lru_lambda"))
    fb = jnp.pad(small["forget_b"], (0, LANES - N_HEADS)).reshape(1, LANES)
    bd = _block_diag_pairs(small["lru_wa"], small["lru_wx"])
    big = dict(tm=1024, tn=1024)

    u = _norm_fwd(x, g1, "norm_mix")
    (xg,) = _mm([(u, W["in_xg"])], tks=[D], outs=[F32], name="proj_xg", **big)
    (qkv_t,) = _mm([(W["in_qkv_t"], u)], tb=True, tks=[D], outs=[BF16], name="proj_qkv_t", **big)
    (kv,) = _mm([(u, W["in_kv"])], tks=[D], outs=[BF16], name="proj_kv", **big)
    (gates,) = _mm([(u, W["in_gates"])], tks=[D], outs=[F32], name="proj_gates", **big)
    (fl,) = _mm([(u, W["in_f"])], tks=[D], outs=[F32], name="proj_forget", **big)
    h, yain = _lru_fwd(xg, cw, cb, bd, ba, bx, lam)
    fcum, f_rep = _forget_cumsum(fl, fb)
    f_row = fcum[:, :N_HEADS].T.reshape(N_HEADS, 1, S)
    ob_t, lse = _attn_fwd(kv, qkv_t, f_row, f_rep)
    (ya,) = _mm([(yain, W["branch_a"])], tks=[D], outs=[F32], name="branch_a", **big)
    (yb,) = _mm([(ob_t, W["branch_b"])], ta=True, tks=[D], outs=[F32], name="branch_b", **big)
    mix = _gate_mix(gates, ya, yb)
    (x1,) = _mm([(mix, W["out"])], tks=[D], outs=[F32], name="out_proj", extra=(x,),
                epi=lambda acc, res: (res + acc,), **big)
    m = _norm_fwd(x1, g2, "norm_mlp")
    relu, hh = _mm([(m, W["up"])], tks=[D], outs=[BF16, BF16], name="mlp_up",
                   epi=lambda acc: (jnp.maximum(acc, 0.0), jnp.square(jnp.maximum(acc, 0.0))), **big)
    (x2,) = _mm([(hh, W["down"])], tks=[1024], outs=[F32], name="mlp_down", extra=(x1,),
                epi=lambda acc, res: (res + acc,), **big)
    loss_acc, dg3, dx2, dx2b = _final_norm_loss(x2, target, g3)

    (dhpre,) = _mm([(dx2b, W["down"])], tb=True, tks=[D], outs=[BF16], name="d_mlp_act", extra=(relu,),
                   epi=lambda acc, r: (acc * (2.0 * r.astype(F32)),), **big)
    (dw_down,) = _mm([(hh, dx2b)], ta=True, tks=[min(1024, S)], outs=[F32], name="dw_down", **big)
    (dm,) = _mm([(dhpre, W["up"])], tb=True, tks=[1024], outs=[F32], name="d_mlp_in", **big)
    (dw_up,) = _mm([(m, dhpre)], ta=True, tks=[min(1024, S)], outs=[F32], name="dw_up", **big)
    dx1, dx1b, dg2 = _norm_bwd(dm, x1, g2, dx2, "norm_mlp_bwd")
    (dmix,) = _mm([(dx1b, W["out"])], tb=True, tks=[D], outs=[F32], name="d_mix", **big)
    (dw_out,) = _mm([(mix, dx1b)], ta=True, tks=[min(1024, S)], outs=[F32], name="dw_out", **big)
    dya, dyb, dgates = _gate_bwd(dmix, gates, ya, yb)
    (dob_t,) = _mm([(W["branch_b"], dyb)], tb=True, tks=[D], outs=[BF16], name="d_attn_out_t", **big)
    (dw_b,) = _mm([(ob_t, dyb)], tks=[min(1024, S)], outs=[F32], name="dw_branch_b", **big)
    (dyain,) = _mm([(dya, W["branch_a"])], tb=True, tks=[D], outs=[F32], name="d_lru_out", **big)
    (dw_a,) = _mm([(yain, dya)], ta=True, tks=[min(1024, S)], outs=[F32], name="dw_branch_a", **big)
    dq_t, dk_t, dv_t, dfk, dfq = _attn_bwd(kv, qkv_t, dob_t, ob_t, lse, f_row, f_rep)
    dF = jnp.pad((dfk.reshape(N_HEADS, S) + dfq.reshape(N_HEADS, S)).T, ((0, 0), (0, LANES - N_HEADS)))
    dfl, dfb = _forget_bwd(dF, fl, fb)
    dxg, dcw, dcb, dba, dbx, dlam, dbd = _lru_bwd(xg, h, dyain, cw, cb, bd, ba, bx, lam)
    wq_t, wk_t, wv_t = (W["in_qkv_t"][D * i:D * (i + 1)] for i in range(3))
    (du,) = _mm([(dxg, W["in_xg"]), (dq_t, wq_t), (dk_t, wk_t), (dv_t, wv_t), (dgates, W["in_gates"]),
                 (dfl, W["in_f"])],
                ta=[False, True, True, True, False, False], tb=[True, False, False, False, True, True],
                tks=[1024, D, D, D, 1024, LANES], outs=[F32], name="d_norm_mix_out", tm=1024, tn=512)
    tks = [min(1024, S)]
    dw_in_parts = [
        _mm([(u, dxg)], ta=True, tks=tks, outs=[F32], name="dw_in_xg", **big)[0],
        _mm([(dq_t, u)], tks=tks, outs=[F32], name="dw_in_q_t", **big)[0].T,
        _mm([(dk_t, u)], tks=tks, outs=[F32], name="dw_in_k_t", **big)[0].T,
        _mm([(dv_t, u)], tks=tks, outs=[F32], name="dw_in_v_t", **big)[0].T,
        _mm([(u, dgates)], ta=True, tks=tks, outs=[F32], name="dw_in_gates", **big)[0],
        _mm([(u, dfl)], ta=True, tks=tks, outs=[F32], name="dw_in_forget", **big)[0][:, :N_HEADS],
    ]
    grad_x, _, dg1 = _norm_bwd(du, x, g1, dx1, "norm_mix_bwd")

    dwa, dwx = _block_diag_unpair(dbd)
    big_grads = dict(w_in=jnp.concatenate(dw_in_parts, axis=1), w_branch_a=dw_a, w_branch_b=dw_b, w_out=dw_out,
                     w_up=dw_up, w_down=dw_down)
    small_grads = dict(norm_mix_g=dg1.reshape(D), conv_w=dcw, conv_b=dcb.reshape(D), lru_wa=dwa, lru_ba=dba.reshape(D),
                       lru_wx=dwx, lru_bx=dbx.reshape(D), lru_lambda=dlam.reshape(D), forget_b=dfb[0, :N_HEADS],
                       norm_mlp_g=dg2.reshape(D), norm_final_g=dg3.reshape(D))
    return loss_acc[0, 0], grad_x, big_grads, small_grads


SMALL_NAMES = ("norm_mix_g", "conv_b", "lru_wa", "lru_ba", "lru_wx", "lru_bx", "lru_lambda", "forget_b",
               "norm_mlp_g", "norm_final_g")
TILE_ELEMS = SUBLANES * LANES


def _pack_small(parts):
    rows = []
    for p in parts:
        flat = p.reshape(-1)
        flat = jnp.pad(flat, (0, (-flat.shape[0]) % TILE_ELEMS))
        rows.append(flat.reshape(-1, LANES))
    return jnp.concatenate(rows, axis=0)


def _packed_rows(shape):
    return -(-math.prod(shape) // TILE_ELEMS) * SUBLANES


def _adamw_small(g_packed, g_conv_w, weights, moms, vels):
    def rows_view(a):
        flat = a.reshape(-1)
        flat = jnp.pad(flat, (0, (-flat.shape[0]) % LANES))
        return flat.reshape(-1, LANES)

    names = SMALL_NAMES + ("conv_w",)
    views = [[rows_view(src[k]) for k in names] for src in (weights, moms, vels)]
    n = len(names)
    starts, r = [], 0
    for k in SMALL_NAMES:
        starts.append(r)
        r += _packed_rows(weights[k].shape)

    def body(*refs):
        gp_ref, gc_ref = refs[0], refs[1]
        w_refs, m_refs, v_refs = refs[2:2 + n], refs[2 + n:2 + 2 * n], refs[2 + 2 * n:2 + 3 * n]
        outs = refs[2 + 3 * n:]
        for i in range(n):
            rows = w_refs[i].shape[0]
            g = gc_ref[...] if i == n - 1 else gp_ref[starts[i]:starts[i] + rows, :]
            delta, m_new, v_new = _adamw_math(w_refs[i][...], g, m_refs[i][...], v_refs[i][...])
            for o_ref, val in zip(outs[4 * i:4 * i + 4], (g, delta, m_new, v_new)):
                o_ref[...] = val

    vm = pl.BlockSpec(memory_space=pltpu.VMEM)
    out_shape = [jax.ShapeDtypeStruct(v.shape, F32) for v in views[0] for _ in range(4)]
    res = pl.pallas_call(
        body, name="adamw_small",
        in_specs=[vm] * (2 + 3 * n), out_specs=[vm] * (4 * n), out_shape=out_shape,
    )(g_packed, g_conv_w, *views[0], *views[1], *views[2])
    dicts = ({}, {}, {}, {})
    for i, k in enumerate(names):
        size = math.prod(weights[k].shape)
        for d, arr in zip(dicts, res[4 * i:4 * i + 4]):
            d[k] = arr.reshape(-1)[:size].reshape(weights[k].shape)
    return dicts


BIG_NAMES = ("w_in", "w_branch_a", "w_branch_b", "w_out", "w_up", "w_down")
WEIGHT_ORDER = ("norm_mix_g", "w_in", "conv_w", "conv_b", "lru_wa", "lru_ba", "lru_wx", "lru_bx", "lru_lambda",
                "forget_b", "w_branch_a", "w_branch_b", "w_out", "norm_mlp_g", "w_up", "w_down", "norm_final_g")


def _to_dest_blocks(name, g):
    if name in ("w_in", "w_up"):
        return g.reshape(g.shape[0], N_DEV, g.shape[1] // N_DEV).transpose(1, 0, 2)
    return g.reshape(N_DEV, g.shape[0] // N_DEV, g.shape[1])


def kernel(x, norm_mix_g, w_in, conv_w, conv_b, lru_wa, lru_ba, lru_wx, lru_bx, lru_lambda, forget_b, w_branch_a, w_branch_b, w_out, norm_mlp_g, w_up, w_down, norm_final_g, loss_target, m_norm_mix_g, m_w_in, m_conv_w, m_conv_b, m_lru_wa, m_lru_ba, m_lru_wx, m_lru_bx, m_lru_lambda, m_forget_b, m_w_branch_a, m_w_branch_b, m_w_out, m_norm_mlp_g, m_w_up, m_w_down, m_norm_final_g, v_norm_mix_g, v_w_in, v_conv_w, v_conv_b, v_lru_wa, v_lru_ba, v_lru_wx, v_lru_bx, v_lru_lambda, v_forget_b, v_w_branch_a, v_w_branch_b, v_w_out, v_norm_mlp_g, v_w_up, v_w_down, v_norm_final_g):
    weights = dict(norm_mix_g=norm_mix_g, w_in=w_in, conv_w=conv_w, conv_b=conv_b, lru_wa=lru_wa, lru_ba=lru_ba,
                   lru_wx=lru_wx, lru_bx=lru_bx, lru_lambda=lru_lambda, forget_b=forget_b, w_branch_a=w_branch_a,
                   w_branch_b=w_branch_b, w_out=w_out, norm_mlp_g=norm_mlp_g, w_up=w_up, w_down=w_down,
                   norm_final_g=norm_final_g)
    moms = dict(norm_mix_g=m_norm_mix_g, w_in=m_w_in, conv_w=m_conv_w, conv_b=m_conv_b, lru_wa=m_lru_wa,
                lru_ba=m_lru_ba, lru_wx=m_lru_wx, lru_bx=m_lru_bx, lru_lambda=m_lru_lambda, forget_b=m_forget_b,
                w_branch_a=m_w_branch_a, w_branch_b=m_w_branch_b, w_out=m_w_out, norm_mlp_g=m_norm_mlp_g,
                w_up=m_w_up, w_down=m_w_down, norm_final_g=m_norm_final_g)
    vels = dict(norm_mix_g=v_norm_mix_g, w_in=v_w_in, conv_w=v_conv_w, conv_b=v_conv_b, lru_wa=v_lru_wa,
                lru_ba=v_lru_ba, lru_wx=v_lru_wx, lru_bx=v_lru_bx, lru_lambda=v_lru_lambda, forget_b=v_forget_b,
                w_branch_a=v_w_branch_a, w_branch_b=v_w_branch_b, w_out=v_w_out, norm_mlp_g=v_norm_mlp_g,
                w_up=v_w_up, w_down=v_w_down, norm_final_g=v_norm_final_g)
    S, D = x.shape[1], x.shape[2]
    me = 4 * lax.axis_index("x") + 2 * lax.axis_index("y") + lax.axis_index("c")

    gathered = _all_gather([weights[k].astype(BF16) for k in BIG_NAMES] + [conv_w])
    win_g, wa_g, wb_g, wo_g, wup_g, wdn_g, cw_g = gathered
    w_in_full = win_g.transpose(1, 0, 2).reshape(D, -1)
    cuts = (0, 2 * D, 5 * D, 7 * D)
    W = dict(in_xg=w_in_full[:, cuts[0]:cuts[1]], in_qkv_t=w_in_full[:, cuts[1]:cuts[2]].T,
             in_kv=w_in_full[:, cuts[1] + D:cuts[2]], in_gates=w_in_full[:, cuts[2]:cuts[3]],
             in_f=jnp.pad(w_in_full[:, cuts[3]:], ((0, 0), (0, LANES - N_HEADS))),
             branch_a=wa_g.reshape(D, D), branch_b=wb_g.reshape(D, D), out=wo_g.reshape(D, D),
             up=wup_g.transpose(1, 0, 2).reshape(D, D_FF), down=wdn_g.reshape(D_FF, D))
    small = {k: weights[k] for k in SMALL_NAMES}
    small["conv_w"] = cw_g.transpose(1, 0, 2).reshape(CONV_W, D)

    loss_part, grad_x, big_grads, small_grads = _local_step(x.reshape(S, D), loss_target.reshape(S, D), W, small)
    loss = lax.psum(loss_part, MESH_AXES)

    core = lax.axis_index("c").astype(jnp.int32).reshape(1)
    chip = (2 * lax.axis_index("x") + lax.axis_index("y")).astype(jnp.int32).reshape(1)
    blocks = [_to_dest_blocks(k, big_grads[k]) for k in BIG_NAMES]
    got = _reduce_scatter_cores(blocks)
    sums = [_chip_partial_sum(b, g, core) for b, g in zip(blocks, got)]
    others = _reduce_scatter_chips([s[1] for s in sums])

    reduced = _all_reduce_small(_pack_small([small_grads[k] for k in SMALL_NAMES] + [small_grads["conv_w"]]))
    cw_full = reduced[reduced.shape[0] - _packed_rows((CONV_W, D)):].reshape(CONV_W, D)
    cw_cols = lax.dynamic_slice_in_dim(cw_full, me * (D // N_DEV), D // N_DEV, axis=1)

    grads, deltas, new_m, new_v = _adamw_small(reduced, cw_cols, weights, moms, vels)
    for k, s, g_got in zip(BIG_NAMES, sums, others):
        grads[k], deltas[k], new_m[k], new_v[k] = _adamw(weights[k], moms[k], vels[k], s[0], g_got, chip, "adamw_" + k)

    return (loss, grad_x.reshape(1, S, D), *[grads[k] for k in WEIGHT_ORDER], *[deltas[k] for k in WEIGHT_ORDER],
            *[new_m[k] for k in WEIGHT_ORDER], *[new_v[k] for k in WEIGHT_ORDER])
```

```python
import functools
import math

import jax
import jax.numpy as jnp
from jax import lax
from jax.experimental import pallas as pl
from jax.experimental.pallas import tpu as pltpu

F32 = jnp.float32
BF16 = jnp.bfloat16

D_MODEL = 1024
N_HEADS = 8
HEAD_DIM = 128
D_FF = 4096
LRU_BLOCKS = 16
LRU_BW = 64
LRU_C = 8.0
CONV_W = 4
RMS_EPS = 1e-6
N_DEV = 8
LANES = 128
SUBLANES = 8
N_GROUPS = D_MODEL // LANES
VMEM_LIMIT_BYTES = 52 * 1024 * 1024
ATTN_SCALE = 1.0 / math.sqrt(HEAD_DIM)
LOG2E = math.log2(math.e)
NEG_BIG = -1e30
ADAM_LR = 0.001
ADAM_B1 = 0.9
ADAM_B2 = 0.999
ADAM_EPS = 1e-08
ADAM_WD = 0.01
ADAM_STEP = 10
ATTN_BLOCK = 1024
ATTN_STRIP = 256
LRU_CHUNK = 256
ROW_TILE = 512
MESH_AXES = ("x", "y", "c")
MESH_ID = pl.DeviceIdType.MESH
ANY = pl.BlockSpec(memory_space=pl.ANY)

NT_DIMS = (((1,), (1,)), ((), ()))
TN_DIMS = (((0,), (0,)), ((), ()))
NN_DIMS = (((1,), (0,)), ((), ()))


def _cparams(*sem):
    return pltpu.CompilerParams(dimension_semantics=sem if sem else None, vmem_limit_bytes=VMEM_LIMIT_BYTES)


def _sigmoid(x):
    return 0.5 * (jnp.tanh(0.5 * x) + 1.0)


def _log1p_pos(e):
    u = 1.0 + e
    return jnp.where(u == 1.0, e, jnp.log(u) * (e / (u - 1.0)))


def _softplus(z):
    return jnp.maximum(z, 0.0) + _log1p_pos(jnp.exp(-jnp.abs(z)))


def _expm1_neg(x):
    series = x * (1.0 + x * 0.5 * (1.0 + x * (1.0 / 3.0) * (1.0 + x * 0.25)))
    return jnp.where(x > -0.03, series, jnp.exp(x) - 1.0)


GELU_C = math.sqrt(2.0 / math.pi)
GELU_K = 0.044715


def _gelu(x):
    return 0.5 * x * (1.0 + jnp.tanh(GELU_C * (x + GELU_K * (x * x * x))))


def _gelu_and_grad(x):
    t = jnp.tanh(GELU_C * (x + GELU_K * (x * x * x)))
    g = 0.5 * x * (1.0 + t)
    dg = 0.5 * (1.0 + t) + 0.5 * x * (1.0 - t * t) * (GELU_C * (1.0 + 3.0 * GELU_K * (x * x)))
    return g, dg


def _mm(pairs, *, ta=False, tb=False, tm, tn, tks, outs, name, epi=None, extra=()):
    n_pairs, n_extra, n_out = len(pairs), len(extra), len(outs)
    tas = list(ta) if isinstance(ta, (list, tuple)) else [ta] * n_pairs
    tbs = list(tb) if isinstance(tb, (list, tuple)) else [tb] * n_pairs
    a0, b0 = pairs[0]
    M = a0.shape[1] if tas[0] else a0.shape[0]
    N = b0.shape[0] if tbs[0] else b0.shape[1]
    tm, tn = min(tm, M), min(tn, N)
    nks, offs = [], []
    for (a, b), tk, pta in zip(pairs, tks, tas):
        K = a.shape[0] if pta else a.shape[1]
        assert K % tk == 0 and M % tm == 0 and N % tn == 0
        offs.append(sum(nks))
        nks.append(K // tk)
    nk_total = sum(nks)
    dims = [(((0 if pta else 1,), (1 if ptb else 0,)), ((), ())) for pta, ptb in zip(tas, tbs)]

    def kmap(off, nk):
        return lambda k: jnp.clip(k - off, 0, nk - 1)

    in_specs, operands = [], []
    for (a, b), tk, off, nk, pta, ptb in zip(pairs, tks, offs, nks, tas, tbs):
        km = kmap(off, nk)
        if pta:
            in_specs.append(pl.BlockSpec((tk, tm), lambda i, j, k, km=km: (km(k), i)))
        else:
            in_specs.append(pl.BlockSpec((tm, tk), lambda i, j, k, km=km: (i, km(k))))
        if ptb:
            in_specs.append(pl.BlockSpec((tn, tk), lambda i, j, k, km=km: (j, km(k))))
        else:
            in_specs.append(pl.BlockSpec((tk, tn), lambda i, j, k, km=km: (km(k), j)))
        operands += [a, b]
    for e in extra:
        in_specs.append(pl.BlockSpec((tm, tn), lambda i, j, k: (i, j)))
        operands.append(e)

    def body(*refs):
        ab = refs[:2 * n_pairs]
        ex = refs[2 * n_pairs:2 * n_pairs + n_extra]
        o = refs[2 * n_pairs + n_extra:2 * n_pairs + n_extra + n_out]
        k = pl.program_id(2)

        def finish(acc):
            res = epi(acc, *[e[...] for e in ex]) if epi is not None else (acc,)
            for r, oref in zip(res, o):
                oref[...] = r.astype(oref.dtype)

        if nk_total == 1:
            finish(lax.dot_general(ab[0][...], ab[1][...], dims[0], preferred_element_type=F32))
            return
        acc = refs[-1]
        for p in range(n_pairs):
            a_ref, b_ref = ab[2 * p], ab[2 * p + 1]

            @pl.when((k >= offs[p]) & (k < offs[p] + nks[p]))
            def _(a_ref=a_ref, b_ref=b_ref, pdims=dims[p]):
                prod = lax.dot_general(a_ref[...], b_ref[...], pdims, preferred_element_type=F32)

                @pl.when(k == 0)
                def _():
                    acc[...] = prod

                @pl.when(k > 0)
                def _():
                    acc[...] += prod

        @pl.when(k == nk_total - 1)
        def _():
            finish(acc[...])

    return pl.pallas_call(
        body,
        name=name,
        grid=(M // tm, N // tn, nk_total),
        in_specs=in_specs,
        out_specs=[pl.BlockSpec((tm, tn), lambda i, j, k: (i, j)) for _ in outs],
        out_shape=[jax.ShapeDtypeStruct((M, N), dt) for dt in outs],
        scratch_shapes=[] if nk_total == 1 else [pltpu.VMEM((tm, tn), F32)],
        compiler_params=_cparams("parallel", "parallel", "arbitrary"),
    )(*operands)


def _norm_fwd(x, g, name):
    S, D = x.shape
    tr = min(ROW_TILE, S)

    def body(x_ref, g_ref, o_ref):
        xv = x_ref[...]
        r = lax.rsqrt(jnp.mean(xv * xv, axis=-1, keepdims=True) + RMS_EPS)
        o_ref[...] = ((xv * r) * g_ref[...]).astype(o_ref.dtype)

    return pl.pallas_call(
        body, name=name, grid=(S // tr,),
        in_specs=[pl.BlockSpec((tr, D), lambda i: (i, 0)), pl.BlockSpec((1, D), lambda i: (0, 0))],
        out_specs=pl.BlockSpec((tr, D), lambda i: (i, 0)),
        out_shape=jax.ShapeDtypeStruct((S, D), BF16),
        compiler_params=_cparams("parallel"),
    )(x, g.reshape(1, D))


def _rms_bwd_rows(dy, xv, g):
    r = lax.rsqrt(jnp.mean(xv * xv, axis=-1, keepdims=True) + RMS_EPS)
    xn = xv * r
    dxn = dy * g
    dx = r * (dxn - xn * jnp.mean(dxn * xn, axis=-1, keepdims=True))
    dg = jnp.sum(dy * xn, axis=0, keepdims=True)
    return dx, dg


def _norm_bwd(dy, x, g, dres, name):
    S, D = x.shape
    tr = min(ROW_TILE, S)

    def body(dy_ref, x_ref, g_ref, dres_ref, dx_ref, dxb_ref, dg_ref):
        dx, dg = _rms_bwd_rows(dy_ref[...], x_ref[...], g_ref[...])
        dx = dres_ref[...] + dx
        dx_ref[...] = dx
        dxb_ref[...] = dx.astype(BF16)

        @pl.when(pl.program_id(0) == 0)
        def _():
            dg_ref[...] = jnp.zeros_like(dg_ref)

        dg_ref[...] += dg

    row = pl.BlockSpec((tr, D), lambda i: (i, 0))
    vec = pl.BlockSpec((1, D), lambda i: (0, 0))
    return pl.pallas_call(
        body, name=name, grid=(S // tr,),
        in_specs=[row, row, vec, row],
        out_specs=[row, row, vec],
        out_shape=[jax.ShapeDtypeStruct((S, D), F32), jax.ShapeDtypeStruct((S, D), BF16),
                   jax.ShapeDtypeStruct((1, D), F32)],
        compiler_params=_cparams("arbitrary"),
    )(dy, x, g.reshape(1, D), dres)


def _final_norm_loss(x2, target, g):
    S, D = x2.shape
    tr = min(ROW_TILE, S)

    def body(x_ref, t_ref, g_ref, loss_ref, dg_ref, dx_ref, dxb_ref):
        xv = x_ref[...]
        gv = g_ref[...]
        r = lax.rsqrt(jnp.mean(xv * xv, axis=-1, keepdims=True) + RMS_EPS)
        y = (xv * r) * gv
        err = y - t_ref[...]
        part = 0.5 * jnp.sum(jnp.mean(err * err, axis=-1, keepdims=True), axis=0, keepdims=True)
        dy = err * (1.0 / D)
        dx, dg = _rms_bwd_rows(dy, xv, gv)
        dx_ref[...] = dx
        dxb_ref[...] = dx.astype(BF16)

        @pl.when(pl.program_id(0) == 0)
        def _():
            dg_ref[...] = jnp.zeros_like(dg_ref)
            loss_ref[...] = jnp.zeros_like(loss_ref)

        dg_ref[...] += dg
        loss_ref[...] += jnp.broadcast_to(part, loss_ref.shape)

    row = pl.BlockSpec((tr, D), lambda i: (i, 0))
    vec = pl.BlockSpec((1, D), lambda i: (0, 0))
    return pl.pallas_call(
        body, name="final_norm_loss", grid=(S // tr,),
        in_specs=[row, row, vec],
        out_specs=[pl.BlockSpec((SUBLANES, LANES), lambda i: (0, 0)), vec, row, row],
        out_shape=[jax.ShapeDtypeStruct((SUBLANES, LANES), F32), jax.ShapeDtypeStruct((1, D), F32),
                   jax.ShapeDtypeStruct((S, D), F32), jax.ShapeDtypeStruct((S, D), BF16)],
        compiler_params=_cparams("arbitrary"),
    )(x2, target, g.reshape(1, D))


def _lru_gates(xa, bd_j, ba_j, bx_j, sp_j):
    z = jnp.dot(xa.astype(BF16), bd_j, preferred_element_type=F32)
    r = _sigmoid(z[:, :LANES] + ba_j)
    ig = _sigmoid(z[:, LANES:] + bx_j)
    log_a = (-LRU_C) * r * sp_j
    a = jnp.exp(log_a)
    mult = jnp.sqrt(-_expm1_neg(2.0 * log_a))
    return r, ig, a, mult


def _conv_rows(xpad, cw_ref, cb_ref, sl, tc):
    out = jnp.broadcast_to(cb_ref[:, sl], (tc, LANES))
    for k in range(CONV_W):
        out = out + xpad[pl.ds(SUBLANES - (CONV_W - 1) + k, tc), sl] * cw_ref[k:k + 1, sl]
    return out


def _lru_fwd(xg, cw, cb, bd, ba, bx, lam):
    S = xg.shape[0]
    D = D_MODEL
    tc = min(LRU_CHUNK, S)
    hb = tc // SUBLANES

    def body(xl_ref, halo_ref, g_ref, cw_ref, cb_ref, bd_ref, ba_ref, bx_ref, lam_ref,
             h_ref, y_ref, xpad, a_s, b_s, carry):
        i = pl.program_id(0)

        @pl.when(i == 0)
        def _():
            carry[...] = jnp.zeros_like(carry)

        xpad[0:SUBLANES, :] = jnp.where(i > 0, halo_ref[...], 0.0)
        xpad[SUBLANES:, :] = xl_ref[...]
        for j in range(N_GROUPS):
            sl = slice(LANES * j, LANES * (j + 1))
            xa = _conv_rows(xpad, cw_ref, cb_ref, sl, tc)
            sp = _softplus(-lam_ref[:, sl])
            _, ig, a, mult = _lru_gates(xa, bd_ref[j], ba_ref[:, sl], bx_ref[:, sl], sp)
            a_s[:, sl] = a
            b_s[:, sl] = mult * (ig * xa)

        row = lax.broadcasted_iota(jnp.int32, (SUBLANES, D), 0)

        def step(t, c):
            o = pl.multiple_of(t * SUBLANES, SUBLANES)
            A = a_s[pl.ds(o, SUBLANES), :]
            B = b_s[pl.ds(o, SUBLANES), :]
            for d in (1, 2, 4):
                keep = row >= d
                a_sh = jnp.where(keep, pltpu.roll(A, d, 0), 1.0)
                b_sh = jnp.where(keep, pltpu.roll(B, d, 0), 0.0)
                B = A * b_sh + B
                A = A * a_sh
            hh = A * c + B
            h_ref[pl.ds(o, SUBLANES), :] = hh
            return jnp.broadcast_to(hh[SUBLANES - 1:SUBLANES, :], (SUBLANES, D))

        carry[...] = lax.fori_loop(0, hb, step, carry[...])
        y_ref[...] = (_gelu(g_ref[...]) * h_ref[...]).astype(BF16)

    row_spec = lambda col: pl.BlockSpec((tc, D), lambda i, col=col: (i, col))
    halo = pl.BlockSpec((SUBLANES, D), lambda i: (jnp.maximum(i * hb - 1, 0), 0))
    full = lambda shape: pl.BlockSpec(shape, lambda i: tuple(0 for _ in shape))
    return pl.pallas_call(
        body, name="lru_fwd", grid=(S // tc,),
        in_specs=[row_spec(0), halo, row_spec(1), full((CONV_W, D)), full((1, D)),
                  full((N_GROUPS, LANES, 2 * LANES)), full((1, D)), full((1, D)), full((1, D))],
        out_specs=[pl.BlockSpec((tc, D), lambda i: (i, 0)), pl.BlockSpec((tc, D), lambda i: (i, 0))],
        out_shape=[jax.ShapeDtypeStruct((S, D), F32), jax.ShapeDtypeStruct((S, D), BF16)],
        scratch_shapes=[pltpu.VMEM((tc + SUBLANES, D), F32), pltpu.VMEM((tc, D), F32),
                        pltpu.VMEM((tc, D), F32), pltpu.VMEM((SUBLANES, D), F32)],
        compiler_params=_cparams("arbitrary"),
    )(xg, xg, xg, cw, cb, bd, ba, bx, lam)


def _lru_bwd(xg, h, dyain, cw, cb, bd, ba, bx, lam):
    S = xg.shape[0]
    D = D_MODEL
    tc = min(LRU_CHUNK, S)
    hb = tc // SUBLANES
    nc = S // tc

    def body(xl_ref, xhalo_ref, g_ref, h_ref, hhalo_ref, dy_ref, cw_ref, cb_ref, bd_ref, ba_ref, bx_ref,
             lam_ref, dxg_ref, dcw_ref, dcb_ref, dba_ref, dbx_ref, dlam_ref, dbd_ref,
             xpad, hpad, a_s, b_s, dh_s, g_s, xa_s, r_s, ig_s, m_s, dxa_pad, carry_e, dxa_head):
        i = pl.program_id(0)
        c = nc - 1 - i

        @pl.when(i == 0)
        def _():
            carry_e[...] = jnp.zeros_like(carry_e)
            dxa_head[...] = jnp.zeros_like(dxa_head)
            for ref in (dcw_ref, dcb_ref, dba_ref, dbx_ref, dlam_ref, dbd_ref):
                ref[...] = jnp.zeros_like(ref)

        xpad[0:SUBLANES, :] = jnp.where(c > 0, xhalo_ref[...], 0.0)
        xpad[SUBLANES:, :] = xl_ref[...]
        hpad[0:SUBLANES, :] = jnp.where(c > 0, hhalo_ref[...], 0.0)
        hpad[SUBLANES:, :] = h_ref[...]

        for j in range(N_GROUPS):
            sl = slice(LANES * j, LANES * (j + 1))
            xa = _conv_rows(xpad, cw_ref, cb_ref, sl, tc)
            sp = _softplus(-lam_ref[:, sl])
            r, ig, a, mult = _lru_gates(xa, bd_ref[j], ba_ref[:, sl], bx_ref[:, sl], sp)
            gl, dgl = _gelu_and_grad(g_ref[:, sl])
            dy = dy_ref[:, sl]
            dh = dy * gl
            dxg_ref[:, D + LANES * j:D + LANES * (j + 1)] = (dy * h_ref[:, sl] * dgl).astype(BF16)
            a_s[:, sl] = a
            b_s[:, sl] = a * dh
            dh_s[:, sl] = dh
            xa_s[:, sl] = xa
            r_s[:, sl] = r
            ig_s[:, sl] = ig
            m_s[:, sl] = mult

        row = lax.broadcasted_iota(jnp.int32, (SUBLANES, D), 0)

        def step(tt, ce):
            o = pl.multiple_of((hb - 1 - tt) * SUBLANES, SUBLANES)
            A = a_s[pl.ds(o, SUBLANES), :]
            B = b_s[pl.ds(o, SUBLANES), :]
            for d in (1, 2, 4):
                keep = row < SUBLANES - d
                a_sh = jnp.where(keep, pltpu.roll(A, SUBLANES - d, 0), 1.0)
                b_sh = jnp.where(keep, pltpu.roll(B, SUBLANES - d, 0), 0.0)
                B = A * b_sh + B
                A = A * a_sh
            e = A * ce + B
            e_next = jnp.where(row < SUBLANES - 1, pltpu.roll(e, SUBLANES - 1, 0), ce)
            g_s[pl.ds(o, SUBLANES), :] = dh_s[pl.ds(o, SUBLANES), :] + e_next
            return jnp.broadcast_to(e[0:1, :], (SUBLANES, D))

        carry_e[...] = lax.fori_loop(0, hb, step, carry_e[...])

        for j in range(N_GROUPS):
            sl = slice(LANES * j, LANES * (j + 1))
            gg = g_s[:, sl]
            xa, r, ig, mult, a = xa_s[:, sl], r_s[:, sl], ig_s[:, sl], m_s[:, sl], a_s[:, sl]
            hprev = hpad[pl.ds(SUBLANES - 1, tc), sl]
            sp = _softplus(-lam_ref[:, sl])
            da = gg * hprev
            dmult = gg * (ig * xa)
            dig = gg * (mult * xa)
            dxa = gg * (mult * ig)
            dla = da * a - dmult * ((a * a) / mult)
            dr = dla * ((-LRU_C) * sp)
            dlam_ref[:, sl] += jnp.sum(dla * r, axis=0, keepdims=True)
            dza = dr * r * (1.0 - r)
            dzx = dig * ig * (1.0 - ig)
            dba_ref[:, sl] += jnp.sum(dza, axis=0, keepdims=True)
            dbx_ref[:, sl] += jnp.sum(dzx, axis=0, keepdims=True)
            dz = jnp.concatenate([dza, dzx], axis=1).astype(BF16)
            dbd_ref[j] += lax.dot_general(xa.astype(BF16), dz, TN_DIMS, preferred_element_type=F32)
            dxa = dxa + lax.dot_general(dz, bd_ref[j], NT_DIMS, preferred_element_type=F32)
            dxa_pad[0:tc, sl] = dxa

        dxa_pad[tc:, :] = dxa_head[...]
        dxa_head[...] = dxa_pad[0:SUBLANES, :]

        for j in range(N_GROUPS):
            sl = slice(LANES * j, LANES * (j + 1))
            dxa = dxa_pad[0:tc, sl]
            dxl = jnp.zeros((tc, LANES), F32)
            for k in range(CONV_W):
                dxl = dxl + dxa_pad[pl.ds(CONV_W - 1 - k, tc), sl] * cw_ref[k:k + 1, sl]
                dcw_ref[k:k + 1, sl] += jnp.sum(
                    dxa * xpad[pl.ds(SUBLANES - (CONV_W - 1) + k, tc), sl], axis=0, keepdims=True)
            dxg_ref[:, sl] = dxl.astype(BF16)
            dcb_ref[:, sl] += jnp.sum(dxa, axis=0, keepdims=True)

        @pl.when(i == nc - 1)
        def _():
            dlam_ref[...] = dlam_ref[...] * (LRU_C * _sigmoid(-lam_ref[...]))

    rev = lambda col: pl.BlockSpec((tc, D), lambda i, col=col: (nc - 1 - i, col))
    halo = pl.BlockSpec((SUBLANES, D), lambda i: (jnp.maximum((nc - 1 - i) * hb - 1, 0), 0))
    full = lambda shape: pl.BlockSpec(shape, lambda i: tuple(0 for _ in shape))
    big = lambda: pltpu.VMEM((tc, D), F32)
    return pl.pallas_call(
        body, name="lru_bwd", grid=(nc,),
        in_specs=[rev(0), halo, rev(1), rev(0), halo, rev(0), full((CONV_W, D)), full((1, D)),
                  full((N_GROUPS, LANES, 2 * LANES)), full((1, D)), full((1, D)), full((1, D))],
        out_specs=[pl.BlockSpec((tc, 2 * D), lambda i: (nc - 1 - i, 0)), full((CONV_W, D)), full((1, D)),
                   full((1, D)), full((1, D)), full((1, D)), full((N_GROUPS, LANES, 2 * LANES))],
        out_shape=[jax.ShapeDtypeStruct((S, 2 * D), BF16), jax.ShapeDtypeStruct((CONV_W, D), F32),
                   jax.ShapeDtypeStruct((1, D), F32), jax.ShapeDtypeStruct((1, D), F32),
                   jax.ShapeDtypeStruct((1, D), F32), jax.ShapeDtypeStruct((1, D), F32),
                   jax.ShapeDtypeStruct((N_GROUPS, LANES, 2 * LANES), F32)],
        scratch_shapes=[pltpu.VMEM((tc + SUBLANES, D), F32), pltpu.VMEM((tc + SUBLANES, D), F32),
                        big(), big(), big(), big(), big(), big(), big(), big(),
                        pltpu.VMEM((tc + SUBLANES, D), F32), pltpu.VMEM((SUBLANES, D), F32),
                        pltpu.VMEM((SUBLANES, D), F32)],
        compiler_params=_cparams("arbitrary"),
    )(xg, xg, xg, h, h, dyain, cw, cb, bd, ba, bx, lam)


def _forget_cumsum(fl, fb):
    S = fl.shape[0]
    tr = min(ROW_TILE, S)
    hb = tr // SUBLANES

    def body(fl_ref, fb_ref, o_ref, rep_ref, lf_s, carry):
        @pl.when(pl.program_id(0) == 0)
        def _():
            carry[...] = jnp.zeros_like(carry)

        lf_s[...] = -_softplus(-(fl_ref[...] + fb_ref[...]))
        row = lax.broadcasted_iota(jnp.int32, (SUBLANES, LANES), 0)

        def step(t, c):
            o = pl.multiple_of(t * SUBLANES, SUBLANES)
            B = lf_s[pl.ds(o, SUBLANES), :]
            for d in (1, 2, 4):
                B = B + jnp.where(row >= d, pltpu.roll(B, d, 0), 0.0)
            B = B + c
            o_ref[pl.ds(o, SUBLANES), :] = B * LOG2E
            return jnp.broadcast_to(B[SUBLANES - 1:SUBLANES, :], (SUBLANES, LANES))

        carry[...] = lax.fori_loop(0, hb, step, carry[...])
        for h in range(N_HEADS):
            rep_ref[h] = jnp.broadcast_to(o_ref[:, h:h + 1], (tr, LANES))

    return pl.pallas_call(
        body, name="forget_cumsum", grid=(S // tr,),
        in_specs=[pl.BlockSpec((tr, LANES), lambda i: (i, 0)), pl.BlockSpec((1, LANES), lambda i: (0, 0))],
        out_specs=[pl.BlockSpec((tr, LANES), lambda i: (i, 0)),
                   pl.BlockSpec((N_HEADS, tr, LANES), lambda i: (0, i, 0))],
        out_shape=[jax.ShapeDtypeStruct((S, LANES), F32), jax.ShapeDtypeStruct((N_HEADS, S, LANES), F32)],
        scratch_shapes=[pltpu.VMEM((tr, LANES), F32), pltpu.VMEM((SUBLANES, LANES), F32)],
        compiler_params=_cparams("arbitrary"),
    )(fl, fb)


def _forget_bwd(dF, fl, fb):
    S = fl.shape[0]
    tr = min(ROW_TILE, S)
    hb = tr // SUBLANES
    nc = S // tr

    def body(df_ref, fl_ref, fb_ref, o_ref, dfb_ref, carry):
        @pl.when(pl.program_id(0) == 0)
        def _():
            carry[...] = jnp.zeros_like(carry)
            dfb_ref[...] = jnp.zeros_like(dfb_ref)

        row = lax.broadcasted_iota(jnp.int32, (SUBLANES, LANES), 0)

        def step(tt, carried):
            c, acc = carried
            o = pl.multiple_of((hb - 1 - tt) * SUBLANES, SUBLANES)
            B = df_ref[pl.ds(o, SUBLANES), :]
            for d in (1, 2, 4):
                B = B + jnp.where(row < SUBLANES - d, pltpu.roll(B, SUBLANES - d, 0), 0.0)
            B = B + c
            z = fl_ref[pl.ds(o, SUBLANES), :] + fb_ref[...]
            dz = B * _sigmoid(-z)
            o_ref[pl.ds(o, SUBLANES), :] = dz.astype(BF16)
            return jnp.broadcast_to(B[0:1, :], (SUBLANES, LANES)), acc + dz

        c, acc = lax.fori_loop(0, hb, step, (carry[...], jnp.zeros((SUBLANES, LANES), F32)))
        carry[...] = c
        dfb_ref[...] += jnp.sum(acc, axis=0, keepdims=True)

    rev = pl.BlockSpec((tr, LANES), lambda i: (nc - 1 - i, 0))
    vec = pl.BlockSpec((1, LANES), lambda i: (0, 0))
    return pl.pallas_call(
        body, name="forget_bwd", grid=(nc,),
        in_specs=[rev, rev, vec],
        out_specs=[rev, vec],
        out_shape=[jax.ShapeDtypeStruct((S, LANES), BF16), jax.ShapeDtypeStruct((1, LANES), F32)],
        scratch_shapes=[pltpu.VMEM((SUBLANES, LANES), F32)],
        compiler_params=_cparams("arbitrary"),
    )(dF, fl, fb)


def _triangle(n, key_major):
    pairs = [(q, k) for q in range(n) for k in range(q + 1)]
    if key_major:
        pairs.sort(key=lambda qk: (qk[1], qk[0]))
    return (jnp.asarray([q for q, _ in pairs], jnp.int32), jnp.asarray([k for _, k in pairs], jnp.int32))


def _strip_scores(k_ref, qt_ref, fk_ref, j, strip, nkeys, diagonal):
    cols = slice(strip * j, strip * (j + 1))
    s = jnp.dot(k_ref[0:nkeys, :], qt_ref[:, cols], preferred_element_type=F32) * (ATTN_SCALE * LOG2E)
    fk = fk_ref[0:nkeys, :]
    s = s - jnp.concatenate([fk] * (strip // LANES), axis=1)
    keep = None
    if diagonal:
        keys = lax.broadcasted_iota(jnp.int32, (nkeys, strip), 0)
        queries = lax.broadcasted_iota(jnp.int32, (nkeys, strip), 1) + strip * j
        keep = keys <= queries
    return s, keep


def _attn_fwd(kv, qkv_t, f_row, f_rep):
    S = kv.shape[0]
    blk = min(ATTN_BLOCK, S)
    strip = min(ATTN_STRIP, blk)
    n = S // blk
    tri_q, tri_k = _triangle(n, key_major=False)
    ones_rows = 2 * SUBLANES

    def body(tq_ref, tk_ref, k_ref, qt_ref, vt_ref, fq_ref, fk_ref, ot_ref, lse_ref, m_s, acc_s, vta_s):
        t = pl.program_id(1)
        qi, ki = tq_ref[t], tk_ref[t]

        @pl.when(ki == 0)
        def _():
            m_s[...] = jnp.full_like(m_s, NEG_BIG)
            acc_s[...] = jnp.zeros_like(acc_s)

        vta_s[0:HEAD_DIM, :] = vt_ref[...]
        vta_s[HEAD_DIM:, :] = jnp.ones((ones_rows, blk), BF16)

        def update(diagonal):
            n_strips = blk // strip
            keys_of = lambda j: strip * (j + 1) if diagonal else blk
            scores = lambda j: _strip_scores(k_ref, qt_ref, fk_ref, j, strip, keys_of(j), diagonal)
            def weighted_values(j, alpha, pb):
                cols = slice(strip * j, strip * (j + 1))
                acc_s[:, cols] = alpha * acc_s[:, cols] + jnp.dot(
                    vta_s[:, 0:keys_of(j)], pb, preferred_element_type=F32)

            ahead, behind = scores(0), None
            for j in range(n_strips):
                cols = slice(strip * j, strip * (j + 1))
                (s, keep), ahead = ahead, (scores(j + 1) if j + 1 < n_strips else None)
                if behind is not None:
                    weighted_values(*behind)
                if diagonal:
                    s = jnp.where(keep, s, NEG_BIG)
                fq = fq_ref[:, cols]
                m_old = m_s[:, cols]
                m_new = jnp.maximum(m_old, jnp.max(s, axis=0, keepdims=True) + fq)
                p = jnp.exp2(s - (m_new - fq))
                behind = (j, jnp.exp2(m_old - m_new), p.astype(BF16))
                m_s[:, cols] = m_new
            weighted_values(*behind)

        @pl.when(ki < qi)
        def _():
            update(False)

        @pl.when(ki == qi)
        def _():
            update(True)
            denom = acc_s[HEAD_DIM:HEAD_DIM + 1, :]
            ot_ref[...] = (acc_s[0:HEAD_DIM, :] / denom).astype(BF16)
            lse_ref[...] = m_s[...] + jnp.log2(denom)

    return pl.pallas_call(
        body, name="attn_fwd",
        grid_spec=pltpu.PrefetchScalarGridSpec(
            num_scalar_prefetch=2, grid=(N_HEADS, tri_q.shape[0]),
            in_specs=[pl.BlockSpec((blk, HEAD_DIM), lambda h, t, tq, tk: (tk[t], h)),
                      pl.BlockSpec((HEAD_DIM, blk), lambda h, t, tq, tk: (h, tq[t])),
                      pl.BlockSpec((HEAD_DIM, blk), lambda h, t, tq, tk: (2 * N_HEADS + h, tk[t])),
                      pl.BlockSpec((None, 1, blk), lambda h, t, tq, tk: (h, 0, tq[t])),
                      pl.BlockSpec((None, blk, LANES), lambda h, t, tq, tk: (h, tk[t], 0))],
            out_specs=[pl.BlockSpec((HEAD_DIM, blk), lambda h, t, tq, tk: (h, tq[t])),
                       pl.BlockSpec((None, 1, blk), lambda h, t, tq, tk: (h, 0, tq[t]))],
            scratch_shapes=[pltpu.VMEM((1, blk), F32), pltpu.VMEM((HEAD_DIM + ones_rows, blk), F32),
                            pltpu.VMEM((HEAD_DIM + ones_rows, blk), BF16)]),
        out_shape=[jax.ShapeDtypeStruct((N_HEADS * HEAD_DIM, S), BF16), jax.ShapeDtypeStruct((N_HEADS, 1, S), F32)],
        compiler_params=_cparams("parallel", "arbitrary"),
    )(tri_q, tri_k, kv, qkv_t, qkv_t, f_row, f_rep)


def _attn_bwd(kv, qkv_t, do_t, o_t, lse, f_row, f_rep):
    S = kv.shape[0]
    blk = min(ATTN_BLOCK, S)
    strip = min(ATTN_STRIP, blk)
    n = S // blk
    tri_q, tri_k = _triangle(n, key_major=True)
    n_tiles = tri_q.shape[0]

    def body(tq_ref, tk_ref, k_ref, v_ref, qt_ref, kt_ref, dot_ref, ot_ref, lse_ref, fq_ref, fk_ref,
             dqt_ref, dkt_ref, dvt_ref, dfk_ref, dfq_ref, dq_s, dk_s, dv_s, dfk_s, dfq_s, row_s):
        t = pl.program_id(1)
        qi, ki = tq_ref[t], tk_ref[t]

        @pl.when(t == 0)
        def _():
            dq_s[...] = jnp.zeros_like(dq_s)
            dfq_s[...] = jnp.zeros_like(dfq_s)

        @pl.when(qi == ki)
        def _():
            dk_s[...] = jnp.zeros_like(dk_s)
            dv_s[...] = jnp.zeros_like(dv_s)
            dfk_s[...] = jnp.zeros_like(dfk_s)

        def update(diagonal):
            row_s[...] = fq_ref[...] - lse_ref[...]
            n_strips = blk // strip
            keys_of = lambda j: strip * (j + 1) if diagonal else blk

            def matmuls_in(j):
                s, keep = _strip_scores(k_ref, qt_ref, fk_ref, j, strip, keys_of(j), diagonal)
                dp = jnp.dot(v_ref[0:keys_of(j), :], dot_ref[:, strip * j:strip * (j + 1)], preferred_element_type=F32)
                return s, keep, dp

            def matmuls_out(j, pb, dsb):
                cols = slice(strip * j, strip * (j + 1))
                nkeys = keys_of(j)
                dv_s[:, 0:nkeys] += lax.dot_general(dot_ref[:, cols], pb, NT_DIMS, preferred_element_type=F32)
                dk_s[:, 0:nkeys] += lax.dot_general(qt_ref[:, cols], dsb, NT_DIMS, preferred_element_type=F32)
                dq_s[qi, :, cols] += jnp.dot(kt_ref[:, 0:nkeys], dsb, preferred_element_type=F32)

            ahead, behind = matmuls_in(0), None
            for j in range(n_strips):
                cols = slice(strip * j, strip * (j + 1))
                nkeys = keys_of(j)
                (s, keep, dp), ahead = ahead, (matmuls_in(j + 1) if j + 1 < n_strips else None)
                if behind is not None:
                    matmuls_out(*behind)
                p = jnp.exp2(s + row_s[:, cols])
                if diagonal:
                    p = jnp.where(keep, p, 0.0)
                dot = dot_ref[:, cols]
                delta = jnp.sum(dot.astype(F32) * ot_ref[:, cols].astype(F32), axis=0, keepdims=True)
                ds = p * (dp - delta)
                behind = (j, p.astype(BF16), ds.astype(BF16))
                lane_part = ds[:, 0:LANES]
                for g in range(1, strip // LANES):
                    lane_part = lane_part + ds[:, LANES * g:LANES * (g + 1)]
                dfk_s[0:nkeys, :] += lane_part
                sub_part = ds[0:SUBLANES, :]
                for g in range(1, nkeys // SUBLANES):
                    sub_part = sub_part + ds[SUBLANES * g:SUBLANES * (g + 1), :]
                dfq_s[qi, :, cols] += sub_part
            matmuls_out(*behind)

        @pl.when(qi == ki)
        def _():
            update(True)

        @pl.when(qi > ki)
        def _():
            update(False)

        @pl.when(qi == n - 1)
        def _():
            dkt_ref[...] = (dk_s[...] * ATTN_SCALE).astype(BF16)
            dvt_ref[...] = dv_s[...].astype(BF16)
            dfk_ref[...] = -jnp.sum(dfk_s[...], axis=-1, keepdims=True)

        @pl.when(t == n_tiles - 1)
        def _():
            for j in range(n):
                dqt_ref[:, blk * j:blk * (j + 1)] = (dq_s[j] * ATTN_SCALE).astype(BF16)
                dfq_ref[:, blk * j:blk * (j + 1)] = jnp.sum(dfq_s[j], axis=0, keepdims=True)

    q_feat = pl.BlockSpec((HEAD_DIM, blk), lambda h, t, tq, tk: (h, tq[t]))
    q_row = pl.BlockSpec((None, 1, blk), lambda h, t, tq, tk: (h, 0, tq[t]))
    k_feat = pl.BlockSpec((HEAD_DIM, blk), lambda h, t, tq, tk: (h, tk[t]))
    return pl.pallas_call(
        body, name="attn_bwd",
        grid_spec=pltpu.PrefetchScalarGridSpec(
            num_scalar_prefetch=2, grid=(N_HEADS, n_tiles),
            in_specs=[pl.BlockSpec((blk, HEAD_DIM), lambda h, t, tq, tk: (tk[t], h)),
                      pl.BlockSpec((blk, HEAD_DIM), lambda h, t, tq, tk: (tk[t], N_HEADS + h)),
                      q_feat,
                      pl.BlockSpec((HEAD_DIM, blk), lambda h, t, tq, tk: (N_HEADS + h, tk[t])),
                      q_feat, q_feat, q_row, q_row,
                      pl.BlockSpec((None, blk, LANES), lambda h, t, tq, tk: (h, tk[t], 0))],
            out_specs=[pl.BlockSpec((HEAD_DIM, S), lambda h, t, tq, tk: (h, 0)), k_feat, k_feat,
                       pl.BlockSpec((None, blk, 1), lambda h, t, tq, tk: (h, tk[t], 0)),
                       pl.BlockSpec((None, 1, S), lambda h, t, tq, tk: (h, 0, 0))],
            scratch_shapes=[pltpu.VMEM((n, HEAD_DIM, blk), F32), pltpu.VMEM((HEAD_DIM, blk), F32),
                            pltpu.VMEM((HEAD_DIM, blk), F32), pltpu.VMEM((blk, LANES), F32),
                            pltpu.VMEM((n, SUBLANES, blk), F32), pltpu.VMEM((1, blk), F32)]),
        out_shape=[jax.ShapeDtypeStruct((N_HEADS * HEAD_DIM, S), BF16)] * 3
        + [jax.ShapeDtypeStruct((N_HEADS, S, 1), F32), jax.ShapeDtypeStruct((N_HEADS, 1, S), F32)],
        compiler_params=_cparams("parallel", "arbitrary"),
    )(tri_q, tri_k, kv, kv, qkv_t, qkv_t, do_t, o_t, lse, f_row, f_rep)


def _gate_mix(gates, ya, yb):
    S, D = ya.shape
    tr = min(ROW_TILE, S)

    def body(ga_ref, gb_ref, ya_ref, yb_ref, o_ref):
        o_ref[...] = (_sigmoid(ga_ref[...]) * ya_ref[...] + _sigmoid(gb_ref[...]) * yb_ref[...]).astype(BF16)

    col = lambda j: pl.BlockSpec((tr, D), lambda i, j=j: (i, j))
    return pl.pallas_call(
        body, name="gate_mix", grid=(S // tr,),
        in_specs=[col(0), col(1), col(0), col(0)],
        out_specs=col(0),
        out_shape=jax.ShapeDtypeStruct((S, D), BF16),
        compiler_params=_cparams("parallel"),
    )(gates, gates, ya, yb)


def _gate_bwd(dmix, gates, ya, yb):
    S, D = ya.shape
    tr = min(ROW_TILE, S)

    def body(dm_ref, ga_ref, gb_ref, ya_ref, yb_ref, dya_ref, dyb_ref, dg_ref):
        dm = dm_ref[...]
        sa, sb = _sigmoid(ga_ref[...]), _sigmoid(gb_ref[...])
        dya_ref[...] = (dm * sa).astype(BF16)
        dyb_ref[...] = (dm * sb).astype(BF16)
        dg_ref[:, 0:D] = ((dm * ya_ref[...]) * (sa * (1.0 - sa))).astype(BF16)
        dg_ref[:, D:] = ((dm * yb_ref[...]) * (sb * (1.0 - sb))).astype(BF16)

    col = lambda j: pl.BlockSpec((tr, D), lambda i, j=j: (i, j))
    return pl.pallas_call(
        body, name="gate_bwd", grid=(S // tr,),
        in_specs=[col(0), col(0), col(1), col(0), col(0)],
        out_specs=[col(0), col(0), pl.BlockSpec((tr, 2 * D), lambda i: (i, 0))],
        out_shape=[jax.ShapeDtypeStruct((S, D), BF16), jax.ShapeDtypeStruct((S, D), BF16),
                   jax.ShapeDtypeStruct((S, 2 * D), BF16)],
        compiler_params=_cparams("parallel"),
    )(dmix, gates, gates, ya, yb)


def _mesh_place():
    x, y, c = lax.axis_index("x"), lax.axis_index("y"), lax.axis_index("c")
    chips = [(1 - x, y), (x, 1 - y), (1 - x, 1 - y)]
    return x, y, c, chips


def _all_gather(shards):
    n = len(shards)

    def body(*refs):
        ins, outs = refs[:n], refs[n:2 * n]
        send_sems, recv_sems, local_sems = refs[2 * n:]
        x, y, c, chips = _mesh_place()
        me, sib = (x, y, c), (x, y, 1 - c)

        def copy(a, k, block, to, src=None):
            px, py, pc = block
            dst = outs[a].at[4 * px + 2 * py + pc]
            return pltpu.make_async_remote_copy(
                src_ref=dst if src is None else src, dst_ref=dst,
                send_sem=send_sems.at[a, k], recv_sem=recv_sems.at[a, k],
                device_id=to, device_id_type=MESH_ID)

        mine = [pltpu.make_async_copy(ins[a], outs[a].at[4 * x + 2 * y + c], local_sems.at[a]) for a in range(n)]
        for cp in mine:
            cp.start()
        first = []
        for a in range(n):
            first.append(copy(a, 0, me, sib, src=ins[a]))
            for j, chip in enumerate(chips):
                first.append(copy(a, 1 + j, me, (*chip, c), src=ins[a]))
        for cp in first:
            cp.start()
        passed = []
        for j, chip in enumerate(chips):
            for a in range(n):
                copy(a, 1 + j, (*chip, c), me).wait_recv()
                fwd = copy(a, 4 + j, (*chip, c), sib)
                fwd.start()
                passed.append(fwd)
        for a in range(n):
            copy(a, 0, sib, me).wait_recv()
            for j, chip in enumerate(chips):
                copy(a, 4 + j, (*chip, 1 - c), me).wait_recv()
        for cp in first + passed:
            cp.wait_send()
        for cp in mine:
            cp.wait()

    return pl.pallas_call(
        body, name="all_gather_weights",
        in_specs=[ANY] * n, out_specs=[ANY] * n,
        out_shape=[jax.ShapeDtypeStruct((N_DEV,) + s.shape, s.dtype) for s in shards],
        scratch_shapes=[pltpu.SemaphoreType.DMA((n, 7)), pltpu.SemaphoreType.DMA((n, 7)),
                        pltpu.SemaphoreType.DMA((n,))],
    )(*shards)


def _reduce_scatter_cores(grads, name):
    n = len(grads)

    def body(*refs):
        ins, gots = refs[:n], refs[n:2 * n]
        send_sems, recv_sems = refs[2 * n:]
        x, y, c, _ = _mesh_place()
        sib = (x, y, 1 - c)
        remote = []
        for a in range(n):
            for k in range(4):
                remote.append(pltpu.make_async_remote_copy(
                    src_ref=ins[a].at[2 * k + (1 - c)], dst_ref=gots[a].at[k],
                    send_sem=send_sems.at[a, k], recv_sem=recv_sems.at[a, k],
                    device_id=sib, device_id_type=MESH_ID))
        for cp in remote:
            cp.start()
        for cp in remote:
            cp.wait_recv()
        for cp in remote:
            cp.wait_send()

    return pl.pallas_call(
        body, name=name,
        in_specs=[ANY] * n, out_specs=[ANY] * n,
        out_shape=[jax.ShapeDtypeStruct((4,) + g.shape[1:], g.dtype) for g in grads],
        scratch_shapes=[pltpu.SemaphoreType.DMA((n, 4)), pltpu.SemaphoreType.DMA((n, 4))],
    )(*grads)


def _chip_partial_sum(blocks, got, core):
    R, C = got.shape[1:]
    tr = min(256, R)
    assert R % tr == 0

    def body(core_ref, a_ref, b_ref, s_ref, sb_ref):
        s = a_ref[...] + b_ref[...]
        s_ref[...] = s
        sb_ref[...] = s.astype(BF16)

    blk = pl.BlockSpec((None, tr, C), lambda k, i, core_ref: (k, i, 0))
    return pl.pallas_call(
        body, name="chip_partial_sum",
        grid_spec=pltpu.PrefetchScalarGridSpec(
            num_scalar_prefetch=1, grid=(4, R // tr),
            in_specs=[pl.BlockSpec((None, tr, C), lambda k, i, core_ref: (2 * k + core_ref[0], i, 0)), blk],
            out_specs=[blk, blk]),
        out_shape=[jax.ShapeDtypeStruct(got.shape, F32), jax.ShapeDtypeStruct(got.shape, BF16)],
        compiler_params=_cparams("parallel", "parallel"),
    )(core, blocks, got)


def _reduce_scatter_chips(sums_bf16):
    n = len(sums_bf16)

    def body(*refs):
        bf16s, gots = refs[:n], refs[n:2 * n]
        send_sems, recv_sems = refs[2 * n:]
        x, y, c, chips = _mesh_place()
        remote = []
        for a in range(n):
            for j, (px, py) in enumerate(chips):
                remote.append(pltpu.make_async_remote_copy(
                    src_ref=bf16s[a].at[2 * px + py], dst_ref=gots[a].at[j],
                    send_sem=send_sems.at[a, j], recv_sem=recv_sems.at[a, j],
                    device_id=(px, py, c), device_id_type=MESH_ID))
        for cp in remote:
            cp.start()
        for cp in remote:
            cp.wait_recv()
        for cp in remote:
            cp.wait_send()

    return pl.pallas_call(
        body, name="reduce_scatter_chips",
        in_specs=[ANY] * n, out_specs=[ANY] * n,
        out_shape=[jax.ShapeDtypeStruct((3,) + s.shape[1:], BF16) for s in sums_bf16],
        scratch_shapes=[pltpu.SemaphoreType.DMA((n, 3)), pltpu.SemaphoreType.DMA((n, 3))],
    )(*sums_bf16)


HBM_SPEC = pl.BlockSpec(memory_space=pltpu.HBM)
SEM_SPEC = pl.BlockSpec(memory_space=pltpu.SEMAPHORE)
FLIPS = [(dx, dy, dc) for dx in (0, 1) for dy in (0, 1) for dc in (0, 1) if (dx, dy, dc) != (0, 0, 0)]


def _flip(v, d):
    return 1 - v if d else v


def _gather_copies(srcs, lands, send_sems, recv_sems):
    x, y, c, _ = _mesh_place()
    sends, recvs = [], []
    for a in range(len(srcs)):
        for k, (dx, dy, dc) in enumerate(FLIPS):
            px, py, pc = _flip(x, dx), _flip(y, dy), _flip(c, dc)
            sem = len(FLIPS) * a + k
            common = dict(send_sem=send_sems.at[sem], recv_sem=recv_sems.at[sem],
                          device_id=(px, py, pc), device_id_type=MESH_ID)
            sends.append(pltpu.make_async_remote_copy(
                src_ref=srcs[a], dst_ref=lands[a].at[4 * x + 2 * y + c], **common))
            recvs.append(pltpu.make_async_remote_copy(
                src_ref=srcs[a], dst_ref=lands[a].at[4 * px + 2 * py + pc], **common))
    return sends, recvs


def _scatter_copies(srcs, lands, send_sems, recv_sems):
    x, y, c, chips = _mesh_place()
    sends = []
    for a in range(len(srcs)):
        for j, (px, py) in enumerate(chips):
            sends.append(pltpu.make_async_remote_copy(
                src_ref=srcs[a].at[2 * px + py], dst_ref=lands[a].at[j],
                send_sem=send_sems.at[3 * a + j], recv_sem=recv_sems.at[3 * a + j],
                device_id=(px, py, c), device_id_type=MESH_ID))
    return sends, sends


def _exchange_start(srcs, land_shapes, copies, n_copies, name):
    n = len(srcs)

    def body(*refs):
        src_refs, land_refs = refs[:n], refs[n:2 * n]
        send_sems, recv_sems = refs[2 * n], refs[2 * n + 1]
        token = refs[-1]
        sends, _ = copies(src_refs, land_refs, send_sems, recv_sems)
        for cp in sends:
            cp.start()
        token[...] = jnp.zeros_like(token)

    lands = [pltpu.with_memory_space_constraint(lax.empty(s.shape, s.dtype), pltpu.HBM) for s in land_shapes]
    srcs = [pltpu.with_memory_space_constraint(s, pltpu.HBM) for s in srcs]
    res = pl.pallas_call(
        body, name=name,
        out_shape=(pltpu.SemaphoreType.DMA((n * n_copies,)), pltpu.SemaphoreType.DMA((n * n_copies,)),
                   *[pltpu.HBM(s.shape, s.dtype) for s in srcs], *[pltpu.HBM(s.shape, s.dtype) for s in land_shapes],
                   jax.ShapeDtypeStruct((SUBLANES, LANES), F32)),
        in_specs=[HBM_SPEC] * (2 * n),
        out_specs=(SEM_SPEC, SEM_SPEC, *[HBM_SPEC] * (2 * n), pl.BlockSpec(memory_space=pltpu.VMEM)),
        input_output_aliases={i: 2 + i for i in range(2 * n)},
        compiler_params=pltpu.CompilerParams(has_side_effects=pltpu.SideEffectType.DATAFLOW_SIDE_EFFECTING),
    )(*srcs, *lands)
    return res[0], res[1], list(res[2:2 + n]), list(res[2 + n:2 + 2 * n]), res[-1]


def _exchange_wait(started, copies, after, name):
    send_sems, recv_sems, srcs, lands, _ = started
    n = len(srcs)

    def body(*refs):
        src_refs, land_refs = refs[:n], refs[n:2 * n]
        send_ref, recv_ref = refs[2 * n], refs[2 * n + 1]
        sends, recvs = copies(src_refs, land_refs, send_ref, recv_ref)
        for cp in sends:
            cp.wait_send()
        for cp in recvs:
            cp.wait_recv()

    res = pl.pallas_call(
        body, name=name,
        out_shape=tuple(pltpu.HBM(s.shape, s.dtype) for s in srcs + lands),
        in_specs=[HBM_SPEC] * (2 * n) + [SEM_SPEC, SEM_SPEC, ANY],
        out_specs=tuple([HBM_SPEC] * (2 * n)),
        input_output_aliases={i: i for i in range(2 * n)},
        compiler_params=pltpu.CompilerParams(has_side_effects=pltpu.SideEffectType.DATAFLOW_SIDE_EFFECTING),
    )(*srcs, *lands, send_sems, recv_sems, after)
    return list(res[:n]), list(res[n:])


def _all_reduce_small(vec):
    R = vec.shape[0]

    def body(v_ref, o_ref, sib_buf, chip_buf, send_sems, recv_sems):
        x, y, c, chips = _mesh_place()
        swap = pltpu.make_async_remote_copy(
            src_ref=v_ref, dst_ref=sib_buf, send_sem=send_sems.at[0], recv_sem=recv_sems.at[0],
            device_id=(x, y, 1 - c), device_id_type=MESH_ID)
        swap.start()
        swap.wait()
        my_chip = 2 * x + y
        chip_buf[my_chip] = v_ref[...] + sib_buf[...]
        sends = []
        for j, (px, py) in enumerate(chips):
            cp = pltpu.make_async_remote_copy(
                src_ref=chip_buf.at[my_chip], dst_ref=chip_buf.at[my_chip],
                send_sem=send_sems.at[1 + j], recv_sem=recv_sems.at[1 + j],
                device_id=(px, py, c), device_id_type=MESH_ID)
            cp.start()
            sends.append(cp)
        for j, (px, py) in enumerate(chips):
            pltpu.make_async_remote_copy(
                src_ref=chip_buf.at[2 * px + py], dst_ref=chip_buf.at[2 * px + py],
                send_sem=send_sems.at[1 + j], recv_sem=recv_sems.at[1 + j],
                device_id=(px, py, c), device_id_type=MESH_ID).wait_recv()
        for cp in sends:
            cp.wait_send()
        o_ref[...] = ((chip_buf[0] + chip_buf[1]) + chip_buf[2]) + chip_buf[3]

    vm = pl.BlockSpec(memory_space=pltpu.VMEM)
    return pl.pallas_call(
        body, name="all_reduce_small",
        in_specs=[vm], out_specs=vm,
        out_shape=jax.ShapeDtypeStruct(vec.shape, F32),
        scratch_shapes=[pltpu.VMEM((R, LANES), F32), pltpu.VMEM((4, R, LANES), F32),
                        pltpu.SemaphoreType.DMA((4,)), pltpu.SemaphoreType.DMA((4,))],
    )(vec)


def _adamw_math(w, g, m, v):
    m = ADAM_B1 * m + (1.0 - ADAM_B1) * g
    v = ADAM_B2 * v + (1.0 - ADAM_B2) * (g * g)
    m_hat = m / (1.0 - ADAM_B1 ** ADAM_STEP)
    v_hat = v / (1.0 - ADAM_B2 ** ADAM_STEP)
    delta = -ADAM_LR * (m_hat / (jnp.sqrt(v_hat) + ADAM_EPS) + ADAM_WD * w)
    return delta, m, v


def _adamw(w, m, v, g_own, g_got, chip, name):
    R, C = w.shape
    tr = R if R * C <= 256 * D_MODEL else 256
    assert R % tr == 0
    n_got = g_got.shape[0]

    def body(*refs):
        w_ref, m_ref, v_ref, go_ref = refs[1:5]
        got = refs[5:5 + n_got]
        g_ref, d_ref, nm_ref, nv_ref = refs[5 + n_got:]
        g = go_ref[...]
        for r in got:
            g = g + r[...].astype(F32)
        delta, m_new, v_new = _adamw_math(w_ref[...], g, m_ref[...], v_ref[...])
        g_ref[...] = g
        d_ref[...] = delta
        nm_ref[...] = m_new
        nv_ref[...] = v_new

    blk = pl.BlockSpec((tr, C), lambda i, chip_ref: (i, 0))
    own_spec = pl.BlockSpec((None, tr, C), lambda i, chip_ref: (chip_ref[0], i, 0))
    got_specs = [pl.BlockSpec((None, tr, C), lambda i, chip_ref, j=j: (j, i, 0)) for j in range(n_got)]
    return pl.pallas_call(
        body, name=name,
        grid_spec=pltpu.PrefetchScalarGridSpec(
            num_scalar_prefetch=1, grid=(R // tr,),
            in_specs=[blk] * 3 + [own_spec] + got_specs, out_specs=[blk] * 4),
        out_shape=[jax.ShapeDtypeStruct((R, C), F32)] * 4,
        compiler_params=_cparams("parallel"),
    )(chip, w, m, v, g_own, *([g_got] * n_got))


def _block_diag_pairs(wa, wx):
    def pairs(w):
        w = w.reshape(N_GROUPS, 2, LRU_BW, LRU_BW)
        z = jnp.zeros((N_GROUPS, LRU_BW, LRU_BW), w.dtype)
        top = jnp.concatenate([w[:, 0], z], axis=2)
        bot = jnp.concatenate([z, w[:, 1]], axis=2)
        return jnp.concatenate([top, bot], axis=1)
    return jnp.concatenate([pairs(wa), pairs(wx)], axis=2).astype(BF16)


def _block_diag_unpair(dbd):
    def unpair(g):
        blocks = jnp.stack([g[:, :LRU_BW, :LRU_BW], g[:, LRU_BW:, LRU_BW:]], axis=1)
        return blocks.reshape(LRU_BLOCKS, LRU_BW, LRU_BW)
    return unpair(dbd[:, :, :LANES]), unpair(dbd[:, :, LANES:])


def _local_step(x, target, W, small, late_weights=None, early_grads=None):
    S, D = x.shape
    g1, g2, g3 = small["norm_mix_g"], small["norm_mlp_g"], small["norm_final_g"]
    cw, cb = small["conv_w"], small["conv_b"].reshape(1, D)
    ba, bx, lam = (small[k].reshape(1, D) for k in ("lru_ba", "lru_bx", "lru_lambda"))
    fb = jnp.pad(small["forget_b"], (0, LANES - N_HEADS)).reshape(1, LANES)
    bd = _block_diag_pairs(small["lru_wa"], small["lru_wx"])
    big = dict(tm=1024, tn=1024)

    u = _norm_fwd(x, g1, "norm_mix")
    (xg,) = _mm([(u, W["in_xg"])], tks=[D], outs=[F32], name="proj_xg", **big)
    (qkv_t,) = _mm([(W["in_qkv_t"], u)], tb=True, tks=[D], outs=[BF16], name="proj_qkv_t", **big)
    (kv,) = _mm([(u, W["in_kv"])], tks=[D], outs=[BF16], name="proj_kv", **big)
    (gates,) = _mm([(u, W["in_gates"])], tks=[D], outs=[F32], name="proj_gates", **big)
    (fl,) = _mm([(u, W["in_f"])], tks=[D], outs=[F32], name="proj_forget", **big)
    h, yain = _lru_fwd(xg, cw, cb, bd, ba, bx, lam)
    fcum, f_rep = _forget_cumsum(fl, fb)
    f_row = fcum[:, :N_HEADS].T.reshape(N_HEADS, 1, S)
    ob_t, lse = _attn_fwd(kv, qkv_t, f_row, f_rep)
    if late_weights is not None:
        W = {**W, **late_weights(lse)}
    (ya,) = _mm([(yain, W["branch_a"])], tks=[D], outs=[F32], name="branch_a", **big)
    (yb,) = _mm([(ob_t, W["branch_b"])], ta=True, tks=[D], outs=[F32], name="branch_b", **big)
    mix = _gate_mix(gates, ya, yb)
    (x1,) = _mm([(mix, W["out"])], tks=[D], outs=[F32], name="out_proj", extra=(x,),
                epi=lambda acc, res: (res + acc,), **big)
    m = _norm_fwd(x1, g2, "norm_mlp")
    relu, hh = _mm([(m, W["up"])], tks=[D], outs=[BF16, BF16], name="mlp_up",
                   epi=lambda acc: (jnp.maximum(acc, 0.0), jnp.square(jnp.maximum(acc, 0.0))), **big)
    (x2,) = _mm([(hh, W["down"])], tks=[1024], outs=[F32], name="mlp_down", extra=(x1,),
                epi=lambda acc, res: (res + acc,), **big)
    loss_acc, dg3, dx2, dx2b = _final_norm_loss(x2, target, g3)

    (dhpre,) = _mm([(dx2b, W["down"])], tb=True, tks=[D], outs=[BF16], name="d_mlp_act", extra=(relu,),
                   epi=lambda acc, r: (acc * (2.0 * r.astype(F32)),), **big)
    (dw_down,) = _mm([(hh, dx2b)], ta=True, tks=[min(1024, S)], outs=[F32], name="dw_down", **big)
    (dm,) = _mm([(dhpre, W["up"])], tb=True, tks=[1024], outs=[F32], name="d_mlp_in", **big)
    (dw_up,) = _mm([(m, dhpre)], ta=True, tks=[min(1024, S)], outs=[F32], name="dw_up", **big)
    dx1, dx1b, dg2 = _norm_bwd(dm, x1, g2, dx2, "norm_mlp_bwd")
    (dmix,) = _mm([(dx1b, W["out"])], tb=True, tks=[D], outs=[F32], name="d_mix", **big)
    (dw_out,) = _mm([(mix, dx1b)], ta=True, tks=[min(1024, S)], outs=[F32], name="dw_out", **big)
    dya, dyb, dgates = _gate_bwd(dmix, gates, ya, yb)
    (dob_t,) = _mm([(W["branch_b"], dyb)], tb=True, tks=[D], outs=[BF16], name="d_attn_out_t", **big)
    (dw_b,) = _mm([(ob_t, dyb)], tks=[min(1024, S)], outs=[F32], name="dw_branch_b", **big)
    (dyain,) = _mm([(dya, W["branch_a"])], tb=True, tks=[D], outs=[F32], name="d_lru_out", **big)
    (dw_a,) = _mm([(yain, dya)], ta=True, tks=[min(1024, S)], outs=[F32], name="dw_branch_a", **big)
    early = dict(w_branch_a=dw_a, w_branch_b=dw_b, w_out=dw_out, w_up=dw_up, w_down=dw_down)
    early_state = early_grads(early) if early_grads is not None else None
    dq_t, dk_t, dv_t, dfk, dfq = _attn_bwd(kv, qkv_t, dob_t, ob_t, lse, f_row, f_rep)
    dF = jnp.pad((dfk.reshape(N_HEADS, S) + dfq.reshape(N_HEADS, S)).T, ((0, 0), (0, LANES - N_HEADS)))
    dfl, dfb = _forget_bwd(dF, fl, fb)
    dxg, dcw, dcb, dba, dbx, dlam, dbd = _lru_bwd(xg, h, dyain, cw, cb, bd, ba, bx, lam)
    wq_t, wk_t, wv_t = (W["in_qkv_t"][D * i:D * (i + 1)] for i in range(3))
    (du,) = _mm([(dxg, W["in_xg"]), (dq_t, wq_t), (dk_t, wk_t), (dv_t, wv_t), (dgates, W["in_gates"]),
                 (dfl, W["in_f"])],
                ta=[False, True, True, True, False, False], tb=[True, False, False, False, True, True],
                tks=[1024, D, D, D, 1024, LANES], outs=[F32], name="d_norm_mix_out", tm=1024, tn=512)
    tks = [min(1024, S)]
    dw_in_parts = [
        _mm([(u, dxg)], ta=True, tks=tks, outs=[F32], name="dw_in_xg", **big)[0],
        _mm([(dq_t, u)], tks=tks, outs=[F32], name="dw_in_q_t", **big)[0].T,
        _mm([(dk_t, u)], tks=tks, outs=[F32], name="dw_in_k_t", **big)[0].T,
        _mm([(dv_t, u)], tks=tks, outs=[F32], name="dw_in_v_t", **big)[0].T,
        _mm([(u, dgates)], ta=True, tks=tks, outs=[F32], name="dw_in_gates", **big)[0],
        _mm([(u, dfl)], ta=True, tks=tks, outs=[F32], name="dw_in_forget", **big)[0][:, :N_HEADS],
    ]
    grad_x, _, dg1 = _norm_bwd(du, x, g1, dx1, "norm_mix_bwd")

    dwa, dwx = _block_diag_unpair(dbd)
    big_grads = dict(early, w_in=jnp.concatenate(dw_in_parts, axis=1))
    small_grads = dict(norm_mix_g=dg1.reshape(D), conv_w=dcw, conv_b=dcb.reshape(D), lru_wa=dwa, lru_ba=dba.reshape(D),
                       lru_wx=dwx, lru_bx=dbx.reshape(D), lru_lambda=dlam.reshape(D), forget_b=dfb[0, :N_HEADS],
                       norm_mlp_g=dg2.reshape(D), norm_final_g=dg3.reshape(D))
    return loss_acc[0, 0], grad_x, big_grads, small_grads, early_state


SMALL_NAMES = ("norm_mix_g", "conv_b", "lru_wa", "lru_ba", "lru_wx", "lru_bx", "lru_lambda", "forget_b",
               "norm_mlp_g", "norm_final_g")
TILE_ELEMS = SUBLANES * LANES


def _pack_small(parts):
    rows = []
    for p in parts:
        flat = p.reshape(-1)
        flat = jnp.pad(flat, (0, (-flat.shape[0]) % TILE_ELEMS))
        rows.append(flat.reshape(-1, LANES))
    return jnp.concatenate(rows, axis=0)


def _packed_rows(shape):
    return -(-math.prod(shape) // TILE_ELEMS) * SUBLANES


def _adamw_small(g_packed, g_conv_w, weights, moms, vels):
    def rows_view(a):
        flat = a.reshape(-1)
        flat = jnp.pad(flat, (0, (-flat.shape[0]) % LANES))
        return flat.reshape(-1, LANES)

    names = SMALL_NAMES + ("conv_w",)
    views = [[rows_view(src[k]) for k in names] for src in (weights, moms, vels)]
    n = len(names)
    starts, r = [], 0
    for k in SMALL_NAMES:
        starts.append(r)
        r += _packed_rows(weights[k].shape)

    def body(*refs):
        gp_ref, gc_ref = refs[0], refs[1]
        w_refs, m_refs, v_refs = refs[2:2 + n], refs[2 + n:2 + 2 * n], refs[2 + 2 * n:2 + 3 * n]
        outs = refs[2 + 3 * n:]
        for i in range(n):
            rows = w_refs[i].shape[0]
            g = gc_ref[...] if i == n - 1 else gp_ref[starts[i]:starts[i] + rows, :]
            delta, m_new, v_new = _adamw_math(w_refs[i][...], g, m_refs[i][...], v_refs[i][...])
            for o_ref, val in zip(outs[4 * i:4 * i + 4], (g, delta, m_new, v_new)):
                o_ref[...] = val

    vm = pl.BlockSpec(memory_space=pltpu.VMEM)
    out_shape = [jax.ShapeDtypeStruct(v.shape, F32) for v in views[0] for _ in range(4)]
    res = pl.pallas_call(
        body, name="adamw_small",
        in_specs=[vm] * (2 + 3 * n), out_specs=[vm] * (4 * n), out_shape=out_shape,
    )(g_packed, g_conv_w, *views[0], *views[1], *views[2])
    dicts = ({}, {}, {}, {})
    for i, k in enumerate(names):
        size = math.prod(weights[k].shape)
        for d, arr in zip(dicts, res[4 * i:4 * i + 4]):
            d[k] = arr.reshape(-1)[:size].reshape(weights[k].shape)
    return dicts


BIG_NAMES = ("w_in", "w_branch_a", "w_branch_b", "w_out", "w_up", "w_down")
WEIGHT_ORDER = ("norm_mix_g", "w_in", "conv_w", "conv_b", "lru_wa", "lru_ba", "lru_wx", "lru_bx", "lru_lambda",
                "forget_b", "w_branch_a", "w_branch_b", "w_out", "norm_mlp_g", "w_up", "w_down", "norm_final_g")


def _to_dest_blocks(name, g):
    if name in ("w_in", "w_up"):
        return g.reshape(g.shape[0], N_DEV, g.shape[1] // N_DEV).transpose(1, 0, 2)
    return g.reshape(N_DEV, g.shape[0] // N_DEV, g.shape[1])


def kernel(x, norm_mix_g, w_in, conv_w, conv_b, lru_wa, lru_ba, lru_wx, lru_bx, lru_lambda, forget_b, w_branch_a, w_branch_b, w_out, norm_mlp_g, w_up, w_down, norm_final_g, loss_target, m_norm_mix_g, m_w_in, m_conv_w, m_conv_b, m_lru_wa, m_lru_ba, m_lru_wx, m_lru_bx, m_lru_lambda, m_forget_b, m_w_branch_a, m_w_branch_b, m_w_out, m_norm_mlp_g, m_w_up, m_w_down, m_norm_final_g, v_norm_mix_g, v_w_in, v_conv_w, v_conv_b, v_lru_wa, v_lru_ba, v_lru_wx, v_lru_bx, v_lru_lambda, v_forget_b, v_w_branch_a, v_w_branch_b, v_w_out, v_norm_mlp_g, v_w_up, v_w_down, v_norm_final_g):
    weights = dict(norm_mix_g=norm_mix_g, w_in=w_in, conv_w=conv_w, conv_b=conv_b, lru_wa=lru_wa, lru_ba=lru_ba,
                   lru_wx=lru_wx, lru_bx=lru_bx, lru_lambda=lru_lambda, forget_b=forget_b, w_branch_a=w_branch_a,
                   w_branch_b=w_branch_b, w_out=w_out, norm_mlp_g=norm_mlp_g, w_up=w_up, w_down=w_down,
                   norm_final_g=norm_final_g)
    moms = dict(norm_mix_g=m_norm_mix_g, w_in=m_w_in, conv_w=m_conv_w, conv_b=m_conv_b, lru_wa=m_lru_wa,
                lru_ba=m_lru_ba, lru_wx=m_lru_wx, lru_bx=m_lru_bx, lru_lambda=m_lru_lambda, forget_b=m_forget_b,
                w_branch_a=m_w_branch_a, w_branch_b=m_w_branch_b, w_out=m_w_out, norm_mlp_g=m_norm_mlp_g,
                w_up=m_w_up, w_down=m_w_down, norm_final_g=m_norm_final_g)
    vels = dict(norm_mix_g=v_norm_mix_g, w_in=v_w_in, conv_w=v_conv_w, conv_b=v_conv_b, lru_wa=v_lru_wa,
                lru_ba=v_lru_ba, lru_wx=v_lru_wx, lru_bx=v_lru_bx, lru_lambda=v_lru_lambda, forget_b=v_forget_b,
                w_branch_a=v_w_branch_a, w_branch_b=v_w_branch_b, w_out=v_w_out, norm_mlp_g=v_norm_mlp_g,
                w_up=v_w_up, w_down=v_w_down, norm_final_g=v_norm_final_g)
    S, D = x.shape[1], x.shape[2]
    me = 4 * lax.axis_index("x") + 2 * lax.axis_index("y") + lax.axis_index("c")

    core = lax.axis_index("c").astype(jnp.int32).reshape(1)
    chip = (2 * lax.axis_index("x") + lax.axis_index("y")).astype(jnp.int32).reshape(1)
    late_names = BIG_NAMES[1:]

    win_g, cw_g = _all_gather([w_in.astype(BF16), conv_w])
    late_shards = [weights[k].astype(BF16) for k in late_names]
    gather = _exchange_start(late_shards, [jax.ShapeDtypeStruct((N_DEV,) + s.shape, BF16) for s in late_shards],
                             _gather_copies, len(FLIPS), "gather_late_start")
    w_in_full = win_g.transpose(1, 0, 2).reshape(D, -1)
    cuts = (0, 2 * D, 5 * D, 7 * D)
    W = dict(in_xg=w_in_full[:, cuts[0]:cuts[1]], in_qkv_t=w_in_full[:, cuts[1]:cuts[2]].T,
             in_kv=w_in_full[:, cuts[1] + D:cuts[2]], in_gates=w_in_full[:, cuts[2]:cuts[3]],
             in_f=jnp.pad(w_in_full[:, cuts[3]:], ((0, 0), (0, LANES - N_HEADS))))
    small = {k: weights[k] for k in SMALL_NAMES}
    small["conv_w"] = cw_g.transpose(1, 0, 2).reshape(CONV_W, D)
    small["norm_mix_g"] = norm_mix_g + gather[4][0, 0]

    def late_weights(after):
        shards, lands = _exchange_wait(gather, _gather_copies, after, "gather_late_wait")
        wa_g, wb_g, wo_g, wup_g, wdn_g = (
            lax.dynamic_update_slice_in_dim(land, shard[None], me, axis=0) for land, shard in zip(lands, shards))
        return dict(branch_a=wa_g.reshape(D, D), branch_b=wb_g.reshape(D, D), out=wo_g.reshape(D, D),
                    up=wup_g.transpose(1, 0, 2).reshape(D, D_FF), down=wdn_g.reshape(D_FF, D))

    def core_stage(names, grads_by_name, tag):
        blocks = [_to_dest_blocks(k, grads_by_name[k]) for k in names]
        got = _reduce_scatter_cores(blocks, "reduce_scatter_cores_" + tag)
        return [_chip_partial_sum(b, g, core) for b, g in zip(blocks, got)]

    def early_grads(grads_by_name):
        sums = core_stage(late_names, grads_by_name, "early")
        wire = [s[1] for s in sums]
        scatter = _exchange_start(wire, [jax.ShapeDtypeStruct((3,) + s.shape[1:], BF16) for s in wire],
                                  _scatter_copies, 3, "scatter_early_start")
        return sums, scatter

    loss_part, grad_x, big_grads, small_grads, (early_sums, scatter) = _local_step(
        x.reshape(S, D), loss_target.reshape(S, D), W, small, late_weights, early_grads)
    loss = lax.psum(loss_part, MESH_AXES)
    in_sums = core_stage(BIG_NAMES[:1], big_grads, "w_in")
    in_others = _reduce_scatter_chips([s[1] for s in in_sums])
    _, early_others = _exchange_wait(scatter, _scatter_copies, grad_x, "scatter_early_wait")
    sums = list(in_sums) + list(early_sums)
    others = list(in_others) + list(early_others)

    reduced = _all_reduce_small(_pack_small([small_grads[k] for k in SMALL_NAMES] + [small_grads["conv_w"]]))
    cw_full = reduced[reduced.shape[0] - _packed_rows((CONV_W, D)):].reshape(CONV_W, D)
    cw_cols = lax.dynamic_slice_in_dim(cw_full, me * (D // N_DEV), D // N_DEV, axis=1)

    grads, deltas, new_m, new_v = _adamw_small(reduced, cw_cols, weights, moms, vels)
    for k, s, g_got in zip(BIG_NAMES, sums, others):
        grads[k], deltas[k], new_m[k], new_v[k] = _adamw(weights[k], moms[k], vels[k], s[0], g_got, chip, "adamw_" + k)

    return (loss, grad_x.reshape(1, S, D), *[grads[k] for k in WEIGHT_ORDER], *[deltas[k] for k in WEIGHT_ORDER],
            *[new_m[k] for k in WEIGHT_ORDER], *[new_v[k] for k in WEIGHT_ORDER])
```

```python
import functools
import math

import jax
import jax.numpy as jnp
from jax import lax
from jax.experimental import pallas as pl
from jax.experimental.pallas import tpu as pltpu

F32 = jnp.float32
BF16 = jnp.bfloat16

D_MODEL = 1024
N_HEADS = 8
HEAD_DIM = 128
D_FF = 4096
LRU_BLOCKS = 16
LRU_BW = 64
LRU_C = 8.0
CONV_W = 4
RMS_EPS = 1e-6
N_DEV = 8
LANES = 128
SUBLANES = 8
N_GROUPS = D_MODEL // LANES
VMEM_LIMIT_BYTES = 52 * 1024 * 1024
ATTN_SCALE = 1.0 / math.sqrt(HEAD_DIM)
LOG2E = math.log2(math.e)
NEG_BIG = -1e30
ADAM_LR = 0.001
ADAM_B1 = 0.9
ADAM_B2 = 0.999
ADAM_EPS = 1e-08
ADAM_WD = 0.01
ADAM_STEP = 10
ATTN_BLOCK = 1024
ATTN_STRIP = 256
LRU_CHUNK = 256
ROW_TILE = 512
MESH_AXES = ("x", "y", "c")
MESH_ID = pl.DeviceIdType.MESH
ANY = pl.BlockSpec(memory_space=pl.ANY)

NT_DIMS = (((1,), (1,)), ((), ()))
TN_DIMS = (((0,), (0,)), ((), ()))
NN_DIMS = (((1,), (0,)), ((), ()))


def _cparams(*sem):
    return pltpu.CompilerParams(dimension_semantics=sem if sem else None, vmem_limit_bytes=VMEM_LIMIT_BYTES)


def _sigmoid(x):
    return 0.5 * (jnp.tanh(0.5 * x) + 1.0)


def _log1p_pos(e):
    u = 1.0 + e
    return jnp.where(u == 1.0, e, jnp.log(u) * (e / (u - 1.0)))


def _softplus(z):
    return jnp.maximum(z, 0.0) + _log1p_pos(jnp.exp(-jnp.abs(z)))


def _expm1_neg(x):
    series = x * (1.0 + x * 0.5 * (1.0 + x * (1.0 / 3.0) * (1.0 + x * 0.25)))
    return jnp.where(x > -0.03, series, jnp.exp(x) - 1.0)


GELU_C = math.sqrt(2.0 / math.pi)
GELU_K = 0.044715


def _gelu(x):
    return 0.5 * x * (1.0 + jnp.tanh(GELU_C * (x + GELU_K * (x * x * x))))


def _gelu_and_grad(x):
    t = jnp.tanh(GELU_C * (x + GELU_K * (x * x * x)))
    g = 0.5 * x * (1.0 + t)
    dg = 0.5 * (1.0 + t) + 0.5 * x * (1.0 - t * t) * (GELU_C * (1.0 + 3.0 * GELU_K * (x * x)))
    return g, dg


def _mm(pairs, *, ta=False, tb=False, tm, tn, tks, outs, name, epi=None, extra=(), col_blocked=False):
    n_pairs, n_extra, n_out = len(pairs), len(extra), len(outs)
    tas = list(ta) if isinstance(ta, (list, tuple)) else [ta] * n_pairs
    tbs = list(tb) if isinstance(tb, (list, tuple)) else [tb] * n_pairs
    a0, b0 = pairs[0]
    M = a0.shape[1] if tas[0] else a0.shape[0]
    N = b0.shape[0] if tbs[0] else b0.shape[1]
    tm, tn = min(tm, M), min(tn, N)
    nks, offs = [], []
    for (a, b), tk, pta in zip(pairs, tks, tas):
        K = a.shape[0] if pta else a.shape[1]
        assert K % tk == 0 and M % tm == 0 and N % tn == 0
        offs.append(sum(nks))
        nks.append(K // tk)
    nk_total = sum(nks)
    dims = [(((0 if pta else 1,), (1 if ptb else 0,)), ((), ())) for pta, ptb in zip(tas, tbs)]

    def kmap(off, nk):
        return lambda k: jnp.clip(k - off, 0, nk - 1)

    in_specs, operands = [], []
    for (a, b), tk, off, nk, pta, ptb in zip(pairs, tks, offs, nks, tas, tbs):
        km = kmap(off, nk)
        if pta:
            in_specs.append(pl.BlockSpec((tk, tm), lambda i, j, k, km=km: (km(k), i)))
        else:
            in_specs.append(pl.BlockSpec((tm, tk), lambda i, j, k, km=km: (i, km(k))))
        if ptb:
            in_specs.append(pl.BlockSpec((tn, tk), lambda i, j, k, km=km: (j, km(k))))
        else:
            in_specs.append(pl.BlockSpec((tk, tn), lambda i, j, k, km=km: (km(k), j)))
        operands += [a, b]
    for e in extra:
        in_specs.append(pl.BlockSpec((tm, tn), lambda i, j, k: (i, j)))
        operands.append(e)

    def body(*refs):
        ab = refs[:2 * n_pairs]
        ex = refs[2 * n_pairs:2 * n_pairs + n_extra]
        o = refs[2 * n_pairs + n_extra:2 * n_pairs + n_extra + n_out]
        k = pl.program_id(2)

        def finish(acc):
            res = epi(acc, *[e[...] for e in ex]) if epi is not None else (acc,)
            for r, oref in zip(res, o):
                oref[...] = r.astype(oref.dtype)

        if nk_total == 1:
            finish(lax.dot_general(ab[0][...], ab[1][...], dims[0], preferred_element_type=F32))
            return
        acc = refs[-1]
        for p in range(n_pairs):
            a_ref, b_ref = ab[2 * p], ab[2 * p + 1]

            @pl.when((k >= offs[p]) & (k < offs[p] + nks[p]))
            def _(a_ref=a_ref, b_ref=b_ref, pdims=dims[p]):
                prod = lax.dot_general(a_ref[...], b_ref[...], pdims, preferred_element_type=F32)

                @pl.when(k == 0)
                def _():
                    acc[...] = prod

                @pl.when(k > 0)
                def _():
                    acc[...] += prod

        @pl.when(k == nk_total - 1)
        def _():
            finish(acc[...])

    return pl.pallas_call(
        body,
        name=name,
        grid=(M // tm, N // tn, nk_total),
        in_specs=in_specs,
        out_specs=[pl.BlockSpec((None, tm, tn), lambda i, j, k: (j, i, 0)) if col_blocked
                   else pl.BlockSpec((tm, tn), lambda i, j, k: (i, j)) for _ in outs],
        out_shape=[jax.ShapeDtypeStruct((N // tn, M, tn) if col_blocked else (M, N), dt) for dt in outs],
        scratch_shapes=[] if nk_total == 1 else [pltpu.VMEM((tm, tn), F32)],
        compiler_params=_cparams("parallel", "parallel", "arbitrary"),
    )(*operands)


def _norm_fwd(x, g, name):
    S, D = x.shape
    tr = min(ROW_TILE, S)

    def body(x_ref, g_ref, o_ref):
        xv = x_ref[...]
        r = lax.rsqrt(jnp.mean(xv * xv, axis=-1, keepdims=True) + RMS_EPS)
        o_ref[...] = ((xv * r) * g_ref[...]).astype(o_ref.dtype)

    return pl.pallas_call(
        body, name=name, grid=(S // tr,),
        in_specs=[pl.BlockSpec((tr, D), lambda i: (i, 0)), pl.BlockSpec((1, D), lambda i: (0, 0))],
        out_specs=pl.BlockSpec((tr, D), lambda i: (i, 0)),
        out_shape=jax.ShapeDtypeStruct((S, D), BF16),
        compiler_params=_cparams("parallel"),
    )(x, g.reshape(1, D))


def _rms_bwd_rows(dy, xv, g):
    r = lax.rsqrt(jnp.mean(xv * xv, axis=-1, keepdims=True) + RMS_EPS)
    xn = xv * r
    dxn = dy * g
    dx = r * (dxn - xn * jnp.mean(dxn * xn, axis=-1, keepdims=True))
    dg = jnp.sum(dy * xn, axis=0, keepdims=True)
    return dx, dg


def _norm_bwd(dy, x, g, dres, name):
    S, D = x.shape
    tr = min(ROW_TILE, S)

    def body(dy_ref, x_ref, g_ref, dres_ref, dx_ref, dxb_ref, dg_ref):
        dx, dg = _rms_bwd_rows(dy_ref[...], x_ref[...], g_ref[...])
        dx = dres_ref[...] + dx
        dx_ref[...] = dx
        dxb_ref[...] = dx.astype(BF16)

        @pl.when(pl.program_id(0) == 0)
        def _():
            dg_ref[...] = jnp.zeros_like(dg_ref)

        dg_ref[...] += dg

    row = pl.BlockSpec((tr, D), lambda i: (i, 0))
    vec = pl.BlockSpec((1, D), lambda i: (0, 0))
    return pl.pallas_call(
        body, name=name, grid=(S // tr,),
        in_specs=[row, row, vec, row],
        out_specs=[row, row, vec],
        out_shape=[jax.ShapeDtypeStruct((S, D), F32), jax.ShapeDtypeStruct((S, D), BF16),
                   jax.ShapeDtypeStruct((1, D), F32)],
        compiler_params=_cparams("arbitrary"),
    )(dy, x, g.reshape(1, D), dres)


def _final_norm_loss(x2, target, g):
    S, D = x2.shape
    tr = min(ROW_TILE, S)

    def body(x_ref, t_ref, g_ref, loss_ref, dg_ref, dx_ref, dxb_ref):
        xv = x_ref[...]
        gv = g_ref[...]
        r = lax.rsqrt(jnp.mean(xv * xv, axis=-1, keepdims=True) + RMS_EPS)
        y = (xv * r) * gv
        err = y - t_ref[...]
        part = 0.5 * jnp.sum(jnp.mean(err * err, axis=-1, keepdims=True), axis=0, keepdims=True)
        dy = err * (1.0 / D)
        dx, dg = _rms_bwd_rows(dy, xv, gv)
        dx_ref[...] = dx
        dxb_ref[...] = dx.astype(BF16)

        @pl.when(pl.program_id(0) == 0)
        def _():
            dg_ref[...] = jnp.zeros_like(dg_ref)
            loss_ref[...] = jnp.zeros_like(loss_ref)

        dg_ref[...] += dg
        loss_ref[...] += jnp.broadcast_to(part, loss_ref.shape)

    row = pl.BlockSpec((tr, D), lambda i: (i, 0))
    vec = pl.BlockSpec((1, D), lambda i: (0, 0))
    return pl.pallas_call(
        body, name="final_norm_loss", grid=(S // tr,),
        in_specs=[row, row, vec],
        out_specs=[pl.BlockSpec((SUBLANES, LANES), lambda i: (0, 0)), vec, row, row],
        out_shape=[jax.ShapeDtypeStruct((SUBLANES, LANES), F32), jax.ShapeDtypeStruct((1, D), F32),
                   jax.ShapeDtypeStruct((S, D), F32), jax.ShapeDtypeStruct((S, D), BF16)],
        compiler_params=_cparams("arbitrary"),
    )(x2, target, g.reshape(1, D))


def _lru_gates(xa, bd_j, ba_j, bx_j, sp_j):
    z = jnp.dot(xa.astype(BF16), bd_j, preferred_element_type=F32)
    r = _sigmoid(z[:, :LANES] + ba_j)
    ig = _sigmoid(z[:, LANES:] + bx_j)
    log_a = (-LRU_C) * r * sp_j
    a = jnp.exp(log_a)
    mult = jnp.sqrt(-_expm1_neg(2.0 * log_a))
    return r, ig, a, mult


def _conv_rows(xpad, cw_ref, cb_ref, sl, tc):
    out = jnp.broadcast_to(cb_ref[:, sl], (tc, LANES))
    for k in range(CONV_W):
        out = out + xpad[pl.ds(SUBLANES - (CONV_W - 1) + k, tc), sl] * cw_ref[k:k + 1, sl]
    return out


def _lru_fwd(xg, cw, cb, bd, ba, bx, lam):
    S = xg.shape[0]
    D = D_MODEL
    tc = min(LRU_CHUNK, S)
    hb = tc // SUBLANES

    def body(xl_ref, halo_ref, g_ref, cw_ref, cb_ref, bd_ref, ba_ref, bx_ref, lam_ref,
             h_ref, y_ref, xpad, a_s, b_s, carry):
        i = pl.program_id(0)

        @pl.when(i == 0)
        def _():
            carry[...] = jnp.zeros_like(carry)

        xpad[0:SUBLANES, :] = jnp.where(i > 0, halo_ref[...], 0.0)
        xpad[SUBLANES:, :] = xl_ref[...]
        for j in range(N_GROUPS):
            sl = slice(LANES * j, LANES * (j + 1))
            xa = _conv_rows(xpad, cw_ref, cb_ref, sl, tc)
            sp = _softplus(-lam_ref[:, sl])
            _, ig, a, mult = _lru_gates(xa, bd_ref[j], ba_ref[:, sl], bx_ref[:, sl], sp)
            a_s[:, sl] = a
            b_s[:, sl] = mult * (ig * xa)

        row = lax.broadcasted_iota(jnp.int32, (SUBLANES, D), 0)

        def step(t, c):
            o = pl.multiple_of(t * SUBLANES, SUBLANES)
            A = a_s[pl.ds(o, SUBLANES), :]
            B = b_s[pl.ds(o, SUBLANES), :]
            for d in (1, 2, 4):
                keep = row >= d
                a_sh = jnp.where(keep, pltpu.roll(A, d, 0), 1.0)
                b_sh = jnp.where(keep, pltpu.roll(B, d, 0), 0.0)
                B = A * b_sh + B
                A = A * a_sh
            hh = A * c + B
            h_ref[pl.ds(o, SUBLANES), :] = hh
            return jnp.broadcast_to(hh[SUBLANES - 1:SUBLANES, :], (SUBLANES, D))

        carry[...] = lax.fori_loop(0, hb, step, carry[...])
        y_ref[...] = (_gelu(g_ref[...]) * h_ref[...]).astype(BF16)

    row_spec = lambda col: pl.BlockSpec((tc, D), lambda i, col=col: (i, col))
    halo = pl.BlockSpec((SUBLANES, D), lambda i: (jnp.maximum(i * hb - 1, 0), 0))
    full = lambda shape: pl.BlockSpec(shape, lambda i: tuple(0 for _ in shape))
    return pl.pallas_call(
        body, name="lru_fwd", grid=(S // tc,),
        in_specs=[row_spec(0), halo, row_spec(1), full((CONV_W, D)), full((1, D)),
                  full((N_GROUPS, LANES, 2 * LANES)), full((1, D)), full((1, D)), full((1, D))],
        out_specs=[pl.BlockSpec((tc, D), lambda i: (i, 0)), pl.BlockSpec((tc, D), lambda i: (i, 0))],
        out_shape=[jax.ShapeDtypeStruct((S, D), F32), jax.ShapeDtypeStruct((S, D), BF16)],
        scratch_shapes=[pltpu.VMEM((tc + SUBLANES, D), F32), pltpu.VMEM((tc, D), F32),
                        pltpu.VMEM((tc, D), F32), pltpu.VMEM((SUBLANES, D), F32)],
        compiler_params=_cparams("arbitrary"),
    )(xg, xg, xg, cw, cb, bd, ba, bx, lam)


def _lru_bwd(xg, h, dyain, cw, cb, bd, ba, bx, lam):
    S = xg.shape[0]
    D = D_MODEL
    tc = min(LRU_CHUNK, S)
    hb = tc // SUBLANES
    nc = S // tc

    def body(xl_ref, xhalo_ref, g_ref, h_ref, hhalo_ref, dy_ref, cw_ref, cb_ref, bd_ref, ba_ref, bx_ref,
             lam_ref, dxg_ref, dcw_ref, dcb_ref, dba_ref, dbx_ref, dlam_ref, dbd_ref,
             xpad, hpad, a_s, b_s, dh_s, g_s, xa_s, r_s, ig_s, m_s, dxa_pad, carry_e, dxa_head):
        i = pl.program_id(0)
        c = nc - 1 - i

        @pl.when(i == 0)
        def _():
            carry_e[...] = jnp.zeros_like(carry_e)
            dxa_head[...] = jnp.zeros_like(dxa_head)
            for ref in (dcw_ref, dcb_ref, dba_ref, dbx_ref, dlam_ref, dbd_ref):
                ref[...] = jnp.zeros_like(ref)

        xpad[0:SUBLANES, :] = jnp.where(c > 0, xhalo_ref[...], 0.0)
        xpad[SUBLANES:, :] = xl_ref[...]
        hpad[0:SUBLANES, :] = jnp.where(c > 0, hhalo_ref[...], 0.0)
        hpad[SUBLANES:, :] = h_ref[...]

        for j in range(N_GROUPS):
            sl = slice(LANES * j, LANES * (j + 1))
            xa = _conv_rows(xpad, cw_ref, cb_ref, sl, tc)
            sp = _softplus(-lam_ref[:, sl])
            r, ig, a, mult = _lru_gates(xa, bd_ref[j], ba_ref[:, sl], bx_ref[:, sl], sp)
            gl, dgl = _gelu_and_grad(g_ref[:, sl])
            dy = dy_ref[:, sl]
            dh = dy * gl
            dxg_ref[:, D + LANES * j:D + LANES * (j + 1)] = (dy * h_ref[:, sl] * dgl).astype(BF16)
            a_s[:, sl] = a
            b_s[:, sl] = a * dh
            dh_s[:, sl] = dh
            xa_s[:, sl] = xa
            r_s[:, sl] = r
            ig_s[:, sl] = ig
            m_s[:, sl] = mult

        row = lax.broadcasted_iota(jnp.int32, (SUBLANES, D), 0)

        def step(tt, ce):
            o = pl.multiple_of((hb - 1 - tt) * SUBLANES, SUBLANES)
            A = a_s[pl.ds(o, SUBLANES), :]
            B = b_s[pl.ds(o, SUBLANES), :]
            for d in (1, 2, 4):
                keep = row < SUBLANES - d
                a_sh = jnp.where(keep, pltpu.roll(A, SUBLANES - d, 0), 1.0)
                b_sh = jnp.where(keep, pltpu.roll(B, SUBLANES - d, 0), 0.0)
                B = A * b_sh + B
                A = A * a_sh
            e = A * ce + B
            e_next = jnp.where(row < SUBLANES - 1, pltpu.roll(e, SUBLANES - 1, 0), ce)
            g_s[pl.ds(o, SUBLANES), :] = dh_s[pl.ds(o, SUBLANES), :] + e_next
            return jnp.broadcast_to(e[0:1, :], (SUBLANES, D))

        carry_e[...] = lax.fori_loop(0, hb, step, carry_e[...])

        for j in range(N_GROUPS):
            sl = slice(LANES * j, LANES * (j + 1))
            gg = g_s[:, sl]
            xa, r, ig, mult, a = xa_s[:, sl], r_s[:, sl], ig_s[:, sl], m_s[:, sl], a_s[:, sl]
            hprev = hpad[pl.ds(SUBLANES - 1, tc), sl]
            sp = _softplus(-lam_ref[:, sl])
            da = gg * hprev
            dmult = gg * (ig * xa)
            dig = gg * (mult * xa)
            dxa = gg * (mult * ig)
            dla = da * a - dmult * ((a * a) / mult)
            dr = dla * ((-LRU_C) * sp)
            dlam_ref[:, sl] += jnp.sum(dla * r, axis=0, keepdims=True)
            dza = dr * r * (1.0 - r)
            dzx = dig * ig * (1.0 - ig)
            dba_ref[:, sl] += jnp.sum(dza, axis=0, keepdims=True)
            dbx_ref[:, sl] += jnp.sum(dzx, axis=0, keepdims=True)
            dz = jnp.concatenate([dza, dzx], axis=1).astype(BF16)
            dbd_ref[j] += lax.dot_general(xa.astype(BF16), dz, TN_DIMS, preferred_element_type=F32)
            dxa = dxa + lax.dot_general(dz, bd_ref[j], NT_DIMS, preferred_element_type=F32)
            dxa_pad[0:tc, sl] = dxa

        dxa_pad[tc:, :] = dxa_head[...]
        dxa_head[...] = dxa_pad[0:SUBLANES, :]

        for j in range(N_GROUPS):
            sl = slice(LANES * j, LANES * (j + 1))
            dxa = dxa_pad[0:tc, sl]
            dxl = jnp.zeros((tc, LANES), F32)
            for k in range(CONV_W):
                dxl = dxl + dxa_pad[pl.ds(CONV_W - 1 - k, tc), sl] * cw_ref[k:k + 1, sl]
                dcw_ref[k:k + 1, sl] += jnp.sum(
                    dxa * xpad[pl.ds(SUBLANES - (CONV_W - 1) + k, tc), sl], axis=0, keepdims=True)
            dxg_ref[:, sl] = dxl.astype(BF16)
            dcb_ref[:, sl] += jnp.sum(dxa, axis=0, keepdims=True)

        @pl.when(i == nc - 1)
        def _():
            dlam_ref[...] = dlam_ref[...] * (LRU_C * _sigmoid(-lam_ref[...]))

    rev = lambda col: pl.BlockSpec((tc, D), lambda i, col=col: (nc - 1 - i, col))
    halo = pl.BlockSpec((SUBLANES, D), lambda i: (jnp.maximum((nc - 1 - i) * hb - 1, 0), 0))
    full = lambda shape: pl.BlockSpec(shape, lambda i: tuple(0 for _ in shape))
    big = lambda: pltpu.VMEM((tc, D), F32)
    return pl.pallas_call(
        body, name="lru_bwd", grid=(nc,),
        in_specs=[rev(0), halo, rev(1), rev(0), halo, rev(0), full((CONV_W, D)), full((1, D)),
                  full((N_GROUPS, LANES, 2 * LANES)), full((1, D)), full((1, D)), full((1, D))],
        out_specs=[pl.BlockSpec((tc, 2 * D), lambda i: (nc - 1 - i, 0)), full((CONV_W, D)), full((1, D)),
                   full((1, D)), full((1, D)), full((1, D)), full((N_GROUPS, LANES, 2 * LANES))],
        out_shape=[jax.ShapeDtypeStruct((S, 2 * D), BF16), jax.ShapeDtypeStruct((CONV_W, D), F32),
                   jax.ShapeDtypeStruct((1, D), F32), jax.ShapeDtypeStruct((1, D), F32),
                   jax.ShapeDtypeStruct((1, D), F32), jax.ShapeDtypeStruct((1, D), F32),
                   jax.ShapeDtypeStruct((N_GROUPS, LANES, 2 * LANES), F32)],
        scratch_shapes=[pltpu.VMEM((tc + SUBLANES, D), F32), pltpu.VMEM((tc + SUBLANES, D), F32),
                        big(), big(), big(), big(), big(), big(), big(), big(),
                        pltpu.VMEM((tc + SUBLANES, D), F32), pltpu.VMEM((SUBLANES, D), F32),
                        pltpu.VMEM((SUBLANES, D), F32)],
        compiler_params=_cparams("arbitrary"),
    )(xg, xg, xg, h, h, dyain, cw, cb, bd, ba, bx, lam)


def _forget_cumsum(fl, fb):
    S = fl.shape[0]
    tr = min(ROW_TILE, S)
    hb = tr // SUBLANES

    def body(fl_ref, fb_ref, o_ref, rep_ref, lf_s, carry):
        @pl.when(pl.program_id(0) == 0)
        def _():
            carry[...] = jnp.zeros_like(carry)

        lf_s[...] = -_softplus(-(fl_ref[...] + fb_ref[...]))
        row = lax.broadcasted_iota(jnp.int32, (SUBLANES, LANES), 0)

        def step(t, c):
            o = pl.multiple_of(t * SUBLANES, SUBLANES)
            B = lf_s[pl.ds(o, SUBLANES), :]
            for d in (1, 2, 4):
                B = B + jnp.where(row >= d, pltpu.roll(B, d, 0), 0.0)
            B = B + c
            o_ref[pl.ds(o, SUBLANES), :] = B * LOG2E
            return jnp.broadcast_to(B[SUBLANES - 1:SUBLANES, :], (SUBLANES, LANES))

        carry[...] = lax.fori_loop(0, hb, step, carry[...])
        for h in range(N_HEADS):
            rep_ref[h] = jnp.broadcast_to(o_ref[:, h:h + 1], (tr, LANES))

    return pl.pallas_call(
        body, name="forget_cumsum", grid=(S // tr,),
        in_specs=[pl.BlockSpec((tr, LANES), lambda i: (i, 0)), pl.BlockSpec((1, LANES), lambda i: (0, 0))],
        out_specs=[pl.BlockSpec((tr, LANES), lambda i: (i, 0)),
                   pl.BlockSpec((N_HEADS, tr, LANES), lambda i: (0, i, 0))],
        out_shape=[jax.ShapeDtypeStruct((S, LANES), F32), jax.ShapeDtypeStruct((N_HEADS, S, LANES), F32)],
        scratch_shapes=[pltpu.VMEM((tr, LANES), F32), pltpu.VMEM((SUBLANES, LANES), F32)],
        compiler_params=_cparams("arbitrary"),
    )(fl, fb)


def _forget_bwd(dF, fl, fb):
    S = fl.shape[0]
    tr = min(ROW_TILE, S)
    hb = tr // SUBLANES
    nc = S // tr

    def body(df_ref, fl_ref, fb_ref, o_ref, dfb_ref, carry):
        @pl.when(pl.program_id(0) == 0)
        def _():
            carry[...] = jnp.zeros_like(carry)
            dfb_ref[...] = jnp.zeros_like(dfb_ref)

        row = lax.broadcasted_iota(jnp.int32, (SUBLANES, LANES), 0)

        def step(tt, carried):
            c, acc = carried
            o = pl.multiple_of((hb - 1 - tt) * SUBLANES, SUBLANES)
            B = df_ref[pl.ds(o, SUBLANES), :]
            for d in (1, 2, 4):
                B = B + jnp.where(row < SUBLANES - d, pltpu.roll(B, SUBLANES - d, 0), 0.0)
            B = B + c
            z = fl_ref[pl.ds(o, SUBLANES), :] + fb_ref[...]
            dz = B * _sigmoid(-z)
            o_ref[pl.ds(o, SUBLANES), :] = dz.astype(BF16)
            return jnp.broadcast_to(B[0:1, :], (SUBLANES, LANES)), acc + dz

        c, acc = lax.fori_loop(0, hb, step, (carry[...], jnp.zeros((SUBLANES, LANES), F32)))
        carry[...] = c
        dfb_ref[...] += jnp.sum(acc, axis=0, keepdims=True)

    rev = pl.BlockSpec((tr, LANES), lambda i: (nc - 1 - i, 0))
    vec = pl.BlockSpec((1, LANES), lambda i: (0, 0))
    return pl.pallas_call(
        body, name="forget_bwd", grid=(nc,),
        in_specs=[rev, rev, vec],
        out_specs=[rev, vec],
        out_shape=[jax.ShapeDtypeStruct((S, LANES), BF16), jax.ShapeDtypeStruct((1, LANES), F32)],
        scratch_shapes=[pltpu.VMEM((SUBLANES, LANES), F32)],
        compiler_params=_cparams("arbitrary"),
    )(dF, fl, fb)


def _triangle(n, key_major):
    pairs = [(q, k) for q in range(n) for k in range(q + 1)]
    if key_major:
        pairs.sort(key=lambda qk: (qk[1], qk[0]))
    return (jnp.asarray([q for q, _ in pairs], jnp.int32), jnp.asarray([k for _, k in pairs], jnp.int32))


def _strip_scores(k_ref, qt_ref, fk_ref, j, strip, nkeys, diagonal):
    cols = slice(strip * j, strip * (j + 1))
    s = jnp.dot(k_ref[0:nkeys, :], qt_ref[:, cols], preferred_element_type=F32) * (ATTN_SCALE * LOG2E)
    fk = fk_ref[0:nkeys, :]
    s = s - jnp.concatenate([fk] * (strip // LANES), axis=1)
    keep = None
    if diagonal:
        keys = lax.broadcasted_iota(jnp.int32, (nkeys, strip), 0)
        queries = lax.broadcasted_iota(jnp.int32, (nkeys, strip), 1) + strip * j
        keep = keys <= queries
    return s, keep


def _attn_fwd(kv, qkv_t, f_row, f_rep):
    S = kv.shape[0]
    blk = min(ATTN_BLOCK, S)
    strip = min(ATTN_STRIP, blk)
    n = S // blk
    tri_q, tri_k = _triangle(n, key_major=False)
    ones_rows = 2 * SUBLANES

    def body(tq_ref, tk_ref, k_ref, qt_ref, vt_ref, fq_ref, fk_ref, ot_ref, lse_ref, m_s, acc_s, vta_s):
        t = pl.program_id(1)
        qi, ki = tq_ref[t], tk_ref[t]

        @pl.when(ki == 0)
        def _():
            m_s[...] = jnp.full_like(m_s, NEG_BIG)
            acc_s[...] = jnp.zeros_like(acc_s)

        vta_s[0:HEAD_DIM, :] = vt_ref[...]
        vta_s[HEAD_DIM:, :] = jnp.ones((ones_rows, blk), BF16)

        def update(diagonal):
            n_strips = blk // strip
            keys_of = lambda j: strip * (j + 1) if diagonal else blk
            scores = lambda j: _strip_scores(k_ref, qt_ref, fk_ref, j, strip, keys_of(j), diagonal)
            def weighted_values(j, alpha, pb):
                cols = slice(strip * j, strip * (j + 1))
                acc_s[:, cols] = alpha * acc_s[:, cols] + jnp.dot(
                    vta_s[:, 0:keys_of(j)], pb, preferred_element_type=F32)

            ahead, behind = scores(0), None
            for j in range(n_strips):
                cols = slice(strip * j, strip * (j + 1))
                (s, keep), ahead = ahead, (scores(j + 1) if j + 1 < n_strips else None)
                if behind is not None:
                    weighted_values(*behind)
                if diagonal:
                    s = jnp.where(keep, s, NEG_BIG)
                fq = fq_ref[:, cols]
                m_old = m_s[:, cols]
                m_new = jnp.maximum(m_old, jnp.max(s, axis=0, keepdims=True) + fq)
                p = jnp.exp2(s - (m_new - fq))
                behind = (j, jnp.exp2(m_old - m_new), p.astype(BF16))
                m_s[:, cols] = m_new
            weighted_values(*behind)

        @pl.when(ki < qi)
        def _():
            update(False)

        @pl.when(ki == qi)
        def _():
            update(True)
            denom = acc_s[HEAD_DIM:HEAD_DIM + 1, :]
            ot_ref[...] = (acc_s[0:HEAD_DIM, :] / denom).astype(BF16)
            lse_ref[...] = m_s[...] + jnp.log2(denom)

    return pl.pallas_call(
        body, name="attn_fwd",
        grid_spec=pltpu.PrefetchScalarGridSpec(
            num_scalar_prefetch=2, grid=(N_HEADS, tri_q.shape[0]),
            in_specs=[pl.BlockSpec((blk, HEAD_DIM), lambda h, t, tq, tk: (tk[t], h)),
                      pl.BlockSpec((HEAD_DIM, blk), lambda h, t, tq, tk: (h, tq[t])),
                      pl.BlockSpec((HEAD_DIM, blk), lambda h, t, tq, tk: (2 * N_HEADS + h, tk[t])),
                      pl.BlockSpec((None, 1, blk), lambda h, t, tq, tk: (h, 0, tq[t])),
                      pl.BlockSpec((None, blk, LANES), lambda h, t, tq, tk: (h, tk[t], 0))],
            out_specs=[pl.BlockSpec((HEAD_DIM, blk), lambda h, t, tq, tk: (h, tq[t])),
                       pl.BlockSpec((None, 1, blk), lambda h, t, tq, tk: (h, 0, tq[t]))],
            scratch_shapes=[pltpu.VMEM((1, blk), F32), pltpu.VMEM((HEAD_DIM + ones_rows, blk), F32),
                            pltpu.VMEM((HEAD_DIM + ones_rows, blk), BF16)]),
        out_shape=[jax.ShapeDtypeStruct((N_HEADS * HEAD_DIM, S), BF16), jax.ShapeDtypeStruct((N_HEADS, 1, S), F32)],
        compiler_params=_cparams("parallel", "arbitrary"),
    )(tri_q, tri_k, kv, qkv_t, qkv_t, f_row, f_rep)


def _attn_bwd(kv, qkv_t, do_t, o_t, lse, f_row, f_rep):
    S = kv.shape[0]
    blk = min(ATTN_BLOCK, S)
    strip = min(ATTN_STRIP, blk)
    n = S // blk
    tri_q, tri_k = _triangle(n, key_major=True)
    n_tiles = tri_q.shape[0]

    def body(tq_ref, tk_ref, k_ref, v_ref, qt_ref, kt_ref, dot_ref, ot_ref, lse_ref, fq_ref, fk_ref,
             dqt_ref, dkt_ref, dvt_ref, dfk_ref, dfq_ref, dq_s, dk_s, dv_s, dfk_s, dfq_s, row_s):
        t = pl.program_id(1)
        qi, ki = tq_ref[t], tk_ref[t]

        @pl.when(t == 0)
        def _():
            dq_s[...] = jnp.zeros_like(dq_s)
            dfq_s[...] = jnp.zeros_like(dfq_s)

        @pl.when(qi == ki)
        def _():
            dk_s[...] = jnp.zeros_like(dk_s)
            dv_s[...] = jnp.zeros_like(dv_s)
            dfk_s[...] = jnp.zeros_like(dfk_s)

        def update(diagonal):
            row_s[...] = fq_ref[...] - lse_ref[...]
            n_strips = blk // strip
            keys_of = lambda j: strip * (j + 1) if diagonal else blk

            def matmuls_in(j):
                s, keep = _strip_scores(k_ref, qt_ref, fk_ref, j, strip, keys_of(j), diagonal)
                dp = jnp.dot(v_ref[0:keys_of(j), :], dot_ref[:, strip * j:strip * (j + 1)], preferred_element_type=F32)
                return s, keep, dp

            def matmuls_out(j, pb, dsb):
                cols = slice(strip * j, strip * (j + 1))
                nkeys = keys_of(j)
                dv_s[:, 0:nkeys] += lax.dot_general(dot_ref[:, cols], pb, NT_DIMS, preferred_element_type=F32)
                dk_s[:, 0:nkeys] += lax.dot_general(qt_ref[:, cols], dsb, NT_DIMS, preferred_element_type=F32)
                dq_s[qi, :, cols] += jnp.dot(kt_ref[:, 0:nkeys], dsb, preferred_element_type=F32)

            ahead, behind = matmuls_in(0), None
            for j in range(n_strips):
                cols = slice(strip * j, strip * (j + 1))
                nkeys = keys_of(j)
                (s, keep, dp), ahead = ahead, (matmuls_in(j + 1) if j + 1 < n_strips else None)
                if behind is not None:
                    matmuls_out(*behind)
                p = jnp.exp2(s + row_s[:, cols])
                if diagonal:
                    p = jnp.where(keep, p, 0.0)
                dot = dot_ref[:, cols]
                delta = jnp.sum(dot.astype(F32) * ot_ref[:, cols].astype(F32), axis=0, keepdims=True)
                ds = p * (dp - delta)
                behind = (j, p.astype(BF16), ds.astype(BF16))
                lane_part = ds[:, 0:LANES]
                for g in range(1, strip // LANES):
                    lane_part = lane_part + ds[:, LANES * g:LANES * (g + 1)]
                dfk_s[0:nkeys, :] += lane_part
                sub_part = ds[0:SUBLANES, :]
                for g in range(1, nkeys // SUBLANES):
                    sub_part = sub_part + ds[SUBLANES * g:SUBLANES * (g + 1), :]
                dfq_s[qi, :, cols] += sub_part
            matmuls_out(*behind)

        @pl.when(qi == ki)
        def _():
            update(True)

        @pl.when(qi > ki)
        def _():
            update(False)

        @pl.when(qi == n - 1)
        def _():
            dkt_ref[...] = (dk_s[...] * ATTN_SCALE).astype(BF16)
            dvt_ref[...] = dv_s[...].astype(BF16)
            dfk_ref[...] = -jnp.sum(dfk_s[...], axis=-1, keepdims=True)

        @pl.when(t == n_tiles - 1)
        def _():
            for j in range(n):
                dqt_ref[:, blk * j:blk * (j + 1)] = (dq_s[j] * ATTN_SCALE).astype(BF16)
                dfq_ref[:, blk * j:blk * (j + 1)] = jnp.sum(dfq_s[j], axis=0, keepdims=True)

    q_feat = pl.BlockSpec((HEAD_DIM, blk), lambda h, t, tq, tk: (h, tq[t]))
    q_row = pl.BlockSpec((None, 1, blk), lambda h, t, tq, tk: (h, 0, tq[t]))
    k_feat = pl.BlockSpec((HEAD_DIM, blk), lambda h, t, tq, tk: (h, tk[t]))
    return pl.pallas_call(
        body, name="attn_bwd",
        grid_spec=pltpu.PrefetchScalarGridSpec(
            num_scalar_prefetch=2, grid=(N_HEADS, n_tiles),
            in_specs=[pl.BlockSpec((blk, HEAD_DIM), lambda h, t, tq, tk: (tk[t], h)),
                      pl.BlockSpec((blk, HEAD_DIM), lambda h, t, tq, tk: (tk[t], N_HEADS + h)),
                      q_feat,
                      pl.BlockSpec((HEAD_DIM, blk), lambda h, t, tq, tk: (N_HEADS + h, tk[t])),
                      q_feat, q_feat, q_row, q_row,
                      pl.BlockSpec((None, blk, LANES), lambda h, t, tq, tk: (h, tk[t], 0))],
            out_specs=[pl.BlockSpec((HEAD_DIM, S), lambda h, t, tq, tk: (h, 0)), k_feat, k_feat,
                       pl.BlockSpec((None, blk, 1), lambda h, t, tq, tk: (h, tk[t], 0)),
                       pl.BlockSpec((None, 1, S), lambda h, t, tq, tk: (h, 0, 0))],
            scratch_shapes=[pltpu.VMEM((n, HEAD_DIM, blk), F32), pltpu.VMEM((HEAD_DIM, blk), F32),
                            pltpu.VMEM((HEAD_DIM, blk), F32), pltpu.VMEM((blk, LANES), F32),
                            pltpu.VMEM((n, SUBLANES, blk), F32), pltpu.VMEM((1, blk), F32)]),
        out_shape=[jax.ShapeDtypeStruct((N_HEADS * HEAD_DIM, S), BF16)] * 3
        + [jax.ShapeDtypeStruct((N_HEADS, S, 1), F32), jax.ShapeDtypeStruct((N_HEADS, 1, S), F32)],
        compiler_params=_cparams("parallel", "arbitrary"),
    )(tri_q, tri_k, kv, kv, qkv_t, qkv_t, do_t, o_t, lse, f_row, f_rep)


def _gate_mix(gates, ya, yb):
    S, D = ya.shape
    tr = min(ROW_TILE, S)

    def body(ga_ref, gb_ref, ya_ref, yb_ref, o_ref):
        o_ref[...] = (_sigmoid(ga_ref[...]) * ya_ref[...] + _sigmoid(gb_ref[...]) * yb_ref[...]).astype(BF16)

    col = lambda j: pl.BlockSpec((tr, D), lambda i, j=j: (i, j))
    return pl.pallas_call(
        body, name="gate_mix", grid=(S // tr,),
        in_specs=[col(0), col(1), col(0), col(0)],
        out_specs=col(0),
        out_shape=jax.ShapeDtypeStruct((S, D), BF16),
        compiler_params=_cparams("parallel"),
    )(gates, gates, ya, yb)


def _gate_bwd(dmix, gates, ya, yb):
    S, D = ya.shape
    tr = min(ROW_TILE, S)

    def body(dm_ref, ga_ref, gb_ref, ya_ref, yb_ref, dya_ref, dyb_ref, dg_ref):
        dm = dm_ref[...]
        sa, sb = _sigmoid(ga_ref[...]), _sigmoid(gb_ref[...])
        dya_ref[...] = (dm * sa).astype(BF16)
        dyb_ref[...] = (dm * sb).astype(BF16)
        dg_ref[:, 0:D] = ((dm * ya_ref[...]) * (sa * (1.0 - sa))).astype(BF16)
        dg_ref[:, D:] = ((dm * yb_ref[...]) * (sb * (1.0 - sb))).astype(BF16)

    col = lambda j: pl.BlockSpec((tr, D), lambda i, j=j: (i, j))
    return pl.pallas_call(
        body, name="gate_bwd", grid=(S // tr,),
        in_specs=[col(0), col(0), col(1), col(0), col(0)],
        out_specs=[col(0), col(0), pl.BlockSpec((tr, 2 * D), lambda i: (i, 0))],
        out_shape=[jax.ShapeDtypeStruct((S, D), BF16), jax.ShapeDtypeStruct((S, D), BF16),
                   jax.ShapeDtypeStruct((S, 2 * D), BF16)],
        compiler_params=_cparams("parallel"),
    )(dmix, gates, gates, ya, yb)


def _mesh_place():
    x, y, c = lax.axis_index("x"), lax.axis_index("y"), lax.axis_index("c")
    chips = [(1 - x, y), (x, 1 - y), (1 - x, 1 - y)]
    return x, y, c, chips


def _all_gather(shards):
    n = len(shards)

    def body(*refs):
        ins, outs = refs[:n], refs[n:2 * n]
        send_sems, recv_sems, local_sems = refs[2 * n:]
        x, y, c, chips = _mesh_place()
        me, sib = (x, y, c), (x, y, 1 - c)

        def copy(a, k, block, to, src=None):
            px, py, pc = block
            dst = outs[a].at[4 * px + 2 * py + pc]
            return pltpu.make_async_remote_copy(
                src_ref=dst if src is None else src, dst_ref=dst,
                send_sem=send_sems.at[a, k], recv_sem=recv_sems.at[a, k],
                device_id=to, device_id_type=MESH_ID)

        mine = [pltpu.make_async_copy(ins[a], outs[a].at[4 * x + 2 * y + c], local_sems.at[a]) for a in range(n)]
        for cp in mine:
            cp.start()
        first = []
        for a in range(n):
            first.append(copy(a, 0, me, sib, src=ins[a]))
            for j, chip in enumerate(chips):
                first.append(copy(a, 1 + j, me, (*chip, c), src=ins[a]))
        for cp in first:
            cp.start()
        passed = []
        for j, chip in enumerate(chips):
            for a in range(n):
                copy(a, 1 + j, (*chip, c), me).wait_recv()
                fwd = copy(a, 4 + j, (*chip, c), sib)
                fwd.start()
                passed.append(fwd)
        for a in range(n):
            copy(a, 0, sib, me).wait_recv()
            for j, chip in enumerate(chips):
                copy(a, 4 + j, (*chip, 1 - c), me).wait_recv()
        for cp in first + passed:
            cp.wait_send()
        for cp in mine:
            cp.wait()

    return pl.pallas_call(
        body, name="all_gather_weights",
        in_specs=[ANY] * n, out_specs=[ANY] * n,
        out_shape=[jax.ShapeDtypeStruct((N_DEV,) + s.shape, s.dtype) for s in shards],
        scratch_shapes=[pltpu.SemaphoreType.DMA((n, 7)), pltpu.SemaphoreType.DMA((n, 7)),
                        pltpu.SemaphoreType.DMA((n,))],
    )(*shards)


def _reduce_scatter_cores(grads, name):
    n = len(grads)

    def body(*refs):
        ins, gots = refs[:n], refs[n:2 * n]
        send_sems, recv_sems = refs[2 * n:]
        x, y, c, _ = _mesh_place()
        sib = (x, y, 1 - c)
        remote = []
        for a in range(n):
            for k in range(4):
                remote.append(pltpu.make_async_remote_copy(
                    src_ref=ins[a].at[2 * k + (1 - c)], dst_ref=gots[a].at[k],
                    send_sem=send_sems.at[a, k], recv_sem=recv_sems.at[a, k],
                    device_id=sib, device_id_type=MESH_ID))
        for cp in remote:
            cp.start()
        for cp in remote:
            cp.wait_recv()
        for cp in remote:
            cp.wait_send()

    return pl.pallas_call(
        body, name=name,
        in_specs=[ANY] * n, out_specs=[ANY] * n,
        out_shape=[jax.ShapeDtypeStruct((4,) + g.shape[1:], g.dtype) for g in grads],
        scratch_shapes=[pltpu.SemaphoreType.DMA((n, 4)), pltpu.SemaphoreType.DMA((n, 4))],
    )(*grads)


def _chip_partial_sum(blocks, got, core):
    R, C = got.shape[1:]
    tr = min(256, R)
    assert R % tr == 0

    def body(core_ref, a_ref, b_ref, s_ref, sb_ref):
        s = a_ref[...] + b_ref[...]
        s_ref[...] = s
        sb_ref[...] = s.astype(BF16)

    blk = pl.BlockSpec((None, tr, C), lambda k, i, core_ref: (k, i, 0))
    return pl.pallas_call(
        body, name="chip_partial_sum",
        grid_spec=pltpu.PrefetchScalarGridSpec(
            num_scalar_prefetch=1, grid=(4, R // tr),
            in_specs=[pl.BlockSpec((None, tr, C), lambda k, i, core_ref: (2 * k + core_ref[0], i, 0)), blk],
            out_specs=[blk, blk]),
        out_shape=[jax.ShapeDtypeStruct(got.shape, F32), jax.ShapeDtypeStruct(got.shape, BF16)],
        compiler_params=_cparams("parallel", "parallel"),
    )(core, blocks, got)


def _reduce_scatter_chips(sums_bf16):
    n = len(sums_bf16)

    def body(*refs):
        bf16s, gots = refs[:n], refs[n:2 * n]
        send_sems, recv_sems = refs[2 * n:]
        x, y, c, chips = _mesh_place()
        remote = []
        for a in range(n):
            for j, (px, py) in enumerate(chips):
                remote.append(pltpu.make_async_remote_copy(
                    src_ref=bf16s[a].at[2 * px + py], dst_ref=gots[a].at[j],
                    send_sem=send_sems.at[a, j], recv_sem=recv_sems.at[a, j],
                    device_id=(px, py, c), device_id_type=MESH_ID))
        for cp in remote:
            cp.start()
        for cp in remote:
            cp.wait_recv()
        for cp in remote:
            cp.wait_send()

    return pl.pallas_call(
        body, name="reduce_scatter_chips",
        in_specs=[ANY] * n, out_specs=[ANY] * n,
        out_shape=[jax.ShapeDtypeStruct((3,) + s.shape[1:], BF16) for s in sums_bf16],
        scratch_shapes=[pltpu.SemaphoreType.DMA((n, 3)), pltpu.SemaphoreType.DMA((n, 3))],
    )(*sums_bf16)


HBM_SPEC = pl.BlockSpec(memory_space=pltpu.HBM)
SEM_SPEC = pl.BlockSpec(memory_space=pltpu.SEMAPHORE)
FLIPS = [(dx, dy, dc) for dx in (0, 1) for dy in (0, 1) for dc in (0, 1) if (dx, dy, dc) != (0, 0, 0)]


def _flip(v, d):
    return 1 - v if d else v


def _gather_copies(srcs, lands, send_sems, recv_sems):
    x, y, c, _ = _mesh_place()
    sends, recvs = [], []
    for a in range(len(srcs)):
        for k, (dx, dy, dc) in enumerate(FLIPS):
            px, py, pc = _flip(x, dx), _flip(y, dy), _flip(c, dc)
            sem = len(FLIPS) * a + k
            common = dict(send_sem=send_sems.at[sem], recv_sem=recv_sems.at[sem],
                          device_id=(px, py, pc), device_id_type=MESH_ID)
            sends.append(pltpu.make_async_remote_copy(
                src_ref=srcs[a], dst_ref=lands[a].at[4 * x + 2 * y + c], **common))
            recvs.append(pltpu.make_async_remote_copy(
                src_ref=srcs[a], dst_ref=lands[a].at[4 * px + 2 * py + pc], **common))
    return sends, recvs


def _scatter_copies(srcs, lands, send_sems, recv_sems):
    x, y, c, chips = _mesh_place()
    sends = []
    for a in range(len(srcs)):
        for j, (px, py) in enumerate(chips):
            sends.append(pltpu.make_async_remote_copy(
                src_ref=srcs[a].at[2 * px + py], dst_ref=lands[a].at[j],
                send_sem=send_sems.at[3 * a + j], recv_sem=recv_sems.at[3 * a + j],
                device_id=(px, py, c), device_id_type=MESH_ID))
    return sends, sends


def _exchange_start(srcs, land_shapes, copies, n_copies, name):
    n = len(srcs)

    def body(*refs):
        src_refs, land_refs = refs[:n], refs[n:2 * n]
        send_sems, recv_sems = refs[2 * n], refs[2 * n + 1]
        token = refs[-1]
        sends, _ = copies(src_refs, land_refs, send_sems, recv_sems)
        for cp in sends:
            cp.start()
        token[...] = jnp.zeros_like(token)

    lands = [pltpu.with_memory_space_constraint(lax.empty(s.shape, s.dtype), pltpu.HBM) for s in land_shapes]
    srcs = [pltpu.with_memory_space_constraint(s, pltpu.HBM) for s in srcs]
    res = pl.pallas_call(
        body, name=name,
        out_shape=(pltpu.SemaphoreType.DMA((n * n_copies,)), pltpu.SemaphoreType.DMA((n * n_copies,)),
                   *[pltpu.HBM(s.shape, s.dtype) for s in srcs], *[pltpu.HBM(s.shape, s.dtype) for s in land_shapes],
                   jax.ShapeDtypeStruct((SUBLANES, LANES), F32)),
        in_specs=[HBM_SPEC] * (2 * n),
        out_specs=(SEM_SPEC, SEM_SPEC, *[HBM_SPEC] * (2 * n), pl.BlockSpec(memory_space=pltpu.VMEM)),
        input_output_aliases={i: 2 + i for i in range(2 * n)},
        compiler_params=pltpu.CompilerParams(has_side_effects=pltpu.SideEffectType.DATAFLOW_SIDE_EFFECTING),
    )(*srcs, *lands)
    return res[0], res[1], list(res[2:2 + n]), list(res[2 + n:2 + 2 * n]), res[-1]


def _exchange_wait(started, copies, after, name):
    send_sems, recv_sems, srcs, lands, _ = started
    n = len(srcs)

    def body(*refs):
        src_refs, land_refs = refs[:n], refs[n:2 * n]
        send_ref, recv_ref = refs[2 * n], refs[2 * n + 1]
        sends, recvs = copies(src_refs, land_refs, send_ref, recv_ref)
        for cp in sends:
            cp.wait_send()
        for cp in recvs:
            cp.wait_recv()

    res = pl.pallas_call(
        body, name=name,
        out_shape=tuple(pltpu.HBM(s.shape, s.dtype) for s in srcs + lands),
        in_specs=[HBM_SPEC] * (2 * n) + [SEM_SPEC, SEM_SPEC, ANY],
        out_specs=tuple([HBM_SPEC] * (2 * n)),
        input_output_aliases={i: i for i in range(2 * n)},
        compiler_params=pltpu.CompilerParams(has_side_effects=pltpu.SideEffectType.DATAFLOW_SIDE_EFFECTING),
    )(*srcs, *lands, send_sems, recv_sems, after)
    return list(res[:n]), list(res[n:])


def _all_reduce_small(vec):
    R = vec.shape[0]

    def body(v_ref, o_ref, sib_buf, chip_buf, send_sems, recv_sems):
        x, y, c, chips = _mesh_place()
        swap = pltpu.make_async_remote_copy(
            src_ref=v_ref, dst_ref=sib_buf, send_sem=send_sems.at[0], recv_sem=recv_sems.at[0],
            device_id=(x, y, 1 - c), device_id_type=MESH_ID)
        swap.start()
        swap.wait()
        my_chip = 2 * x + y
        chip_buf[my_chip] = v_ref[...] + sib_buf[...]
        sends = []
        for j, (px, py) in enumerate(chips):
            cp = pltpu.make_async_remote_copy(
                src_ref=chip_buf.at[my_chip], dst_ref=chip_buf.at[my_chip],
                send_sem=send_sems.at[1 + j], recv_sem=recv_sems.at[1 + j],
                device_id=(px, py, c), device_id_type=MESH_ID)
            cp.start()
            sends.append(cp)
        for j, (px, py) in enumerate(chips):
            pltpu.make_async_remote_copy(
                src_ref=chip_buf.at[2 * px + py], dst_ref=chip_buf.at[2 * px + py],
                send_sem=send_sems.at[1 + j], recv_sem=recv_sems.at[1 + j],
                device_id=(px, py, c), device_id_type=MESH_ID).wait_recv()
        for cp in sends:
            cp.wait_send()
        o_ref[...] = ((chip_buf[0] + chip_buf[1]) + chip_buf[2]) + chip_buf[3]

    vm = pl.BlockSpec(memory_space=pltpu.VMEM)
    return pl.pallas_call(
        body, name="all_reduce_small",
        in_specs=[vm], out_specs=vm,
        out_shape=jax.ShapeDtypeStruct(vec.shape, F32),
        scratch_shapes=[pltpu.VMEM((R, LANES), F32), pltpu.VMEM((4, R, LANES), F32),
                        pltpu.SemaphoreType.DMA((4,)), pltpu.SemaphoreType.DMA((4,))],
    )(vec)


def _adamw_math(w, g, m, v):
    m = ADAM_B1 * m + (1.0 - ADAM_B1) * g
    v = ADAM_B2 * v + (1.0 - ADAM_B2) * (g * g)
    m_hat = m / (1.0 - ADAM_B1 ** ADAM_STEP)
    v_hat = v / (1.0 - ADAM_B2 ** ADAM_STEP)
    delta = -ADAM_LR * (m_hat / (jnp.sqrt(v_hat) + ADAM_EPS) + ADAM_WD * w)
    return delta, m, v


def _adamw(w, m, v, g_own, g_got, chip, name):
    R, C = w.shape
    tr = R if R * C <= 256 * D_MODEL else 256
    assert R % tr == 0
    n_got = g_got.shape[0]

    def body(*refs):
        w_ref, m_ref, v_ref, go_ref = refs[1:5]
        got = refs[5:5 + n_got]
        g_ref, d_ref, nm_ref, nv_ref = refs[5 + n_got:]
        g = go_ref[...]
        for r in got:
            g = g + r[...].astype(F32)
        delta, m_new, v_new = _adamw_math(w_ref[...], g, m_ref[...], v_ref[...])
        g_ref[...] = g
        d_ref[...] = delta
        nm_ref[...] = m_new
        nv_ref[...] = v_new

    blk = pl.BlockSpec((tr, C), lambda i, chip_ref: (i, 0))
    own_spec = pl.BlockSpec((None, tr, C), lambda i, chip_ref: (chip_ref[0], i, 0))
    got_specs = [pl.BlockSpec((None, tr, C), lambda i, chip_ref, j=j: (j, i, 0)) for j in range(n_got)]
    return pl.pallas_call(
        body, name=name,
        grid_spec=pltpu.PrefetchScalarGridSpec(
            num_scalar_prefetch=1, grid=(R // tr,),
            in_specs=[blk] * 3 + [own_spec] + got_specs, out_specs=[blk] * 4),
        out_shape=[jax.ShapeDtypeStruct((R, C), F32)] * 4,
        compiler_params=_cparams("parallel"),
    )(chip, w, m, v, g_own, *([g_got] * n_got))


def _block_diag_pairs(wa, wx):
    def pairs(w):
        w = w.reshape(N_GROUPS, 2, LRU_BW, LRU_BW)
        z = jnp.zeros((N_GROUPS, LRU_BW, LRU_BW), w.dtype)
        top = jnp.concatenate([w[:, 0], z], axis=2)
        bot = jnp.concatenate([z, w[:, 1]], axis=2)
        return jnp.concatenate([top, bot], axis=1)
    return jnp.concatenate([pairs(wa), pairs(wx)], axis=2).astype(BF16)


def _block_diag_unpair(dbd):
    def unpair(g):
        blocks = jnp.stack([g[:, :LRU_BW, :LRU_BW], g[:, LRU_BW:, LRU_BW:]], axis=1)
        return blocks.reshape(LRU_BLOCKS, LRU_BW, LRU_BW)
    return unpair(dbd[:, :, :LANES]), unpair(dbd[:, :, LANES:])


def _local_step(x, target, W, small, late_weights=None, early_grads=None):
    S, D = x.shape
    g1, g2, g3 = small["norm_mix_g"], small["norm_mlp_g"], small["norm_final_g"]
    cw, cb = small["conv_w"], small["conv_b"].reshape(1, D)
    ba, bx, lam = (small[k].reshape(1, D) for k in ("lru_ba", "lru_bx", "lru_lambda"))
    fb = jnp.pad(small["forget_b"], (0, LANES - N_HEADS)).reshape(1, LANES)
    bd = _block_diag_pairs(small["lru_wa"], small["lru_wx"])
    big = dict(tm=1024, tn=1024)

    u = _norm_fwd(x, g1, "norm_mix")
    (xg,) = _mm([(u, W["in_xg"])], tks=[D], outs=[F32], name="proj_xg", **big)
    (qkv_t,) = _mm([(W["in_qkv_t"], u)], tb=True, tks=[D], outs=[BF16], name="proj_qkv_t", **big)
    (kv,) = _mm([(u, W["in_kv"])], tks=[D], outs=[BF16], name="proj_kv", **big)
    (gates,) = _mm([(u, W["in_gates"])], tks=[D], outs=[F32], name="proj_gates", **big)
    (fl,) = _mm([(u, W["in_f"])], tks=[D], outs=[F32], name="proj_forget", **big)
    h, yain = _lru_fwd(xg, cw, cb, bd, ba, bx, lam)
    fcum, f_rep = _forget_cumsum(fl, fb)
    f_row = fcum[:, :N_HEADS].T.reshape(N_HEADS, 1, S)
    ob_t, lse = _attn_fwd(kv, qkv_t, f_row, f_rep)
    if late_weights is not None:
        W = {**W, **late_weights(lse)}
    (ya,) = _mm([(yain, W["branch_a"])], tks=[D], outs=[F32], name="branch_a", **big)
    (yb,) = _mm([(ob_t, W["branch_b"])], ta=True, tks=[D], outs=[F32], name="branch_b", **big)
    mix = _gate_mix(gates, ya, yb)
    (x1,) = _mm([(mix, W["out"])], tks=[D], outs=[F32], name="out_proj", extra=(x,),
                epi=lambda acc, res: (res + acc,), **big)
    m = _norm_fwd(x1, g2, "norm_mlp")
    relu, hh = _mm([(m, W["up"])], tks=[D], outs=[BF16, BF16], name="mlp_up",
                   epi=lambda acc: (jnp.maximum(acc, 0.0), jnp.square(jnp.maximum(acc, 0.0))), **big)
    (x2,) = _mm([(hh, W["down"])], tks=[1024], outs=[F32], name="mlp_down", extra=(x1,),
                epi=lambda acc, res: (res + acc,), **big)
    loss_acc, dg3, dx2, dx2b = _final_norm_loss(x2, target, g3)

    (dhpre,) = _mm([(dx2b, W["down"])], tb=True, tks=[D], outs=[BF16], name="d_mlp_act", extra=(relu,),
                   epi=lambda acc, r: (acc * (2.0 * r.astype(F32)),), **big)
    (dw_down,) = _mm([(hh, dx2b)], ta=True, tks=[min(1024, S)], outs=[F32], name="dw_down", **big)
    (dm,) = _mm([(dhpre, W["up"])], tb=True, tks=[1024], outs=[F32], name="d_mlp_in", **big)
    (dw_up,) = _mm([(m, dhpre)], ta=True, tks=[min(1024, S)], outs=[F32], name="dw_up", tm=1024, tn=D_FF // N_DEV,
                   col_blocked=True)
    dx1, dx1b, dg2 = _norm_bwd(dm, x1, g2, dx2, "norm_mlp_bwd")
    (dmix,) = _mm([(dx1b, W["out"])], tb=True, tks=[D], outs=[F32], name="d_mix", **big)
    (dw_out,) = _mm([(mix, dx1b)], ta=True, tks=[min(1024, S)], outs=[F32], name="dw_out", **big)
    dya, dyb, dgates = _gate_bwd(dmix, gates, ya, yb)
    (dob_t,) = _mm([(W["branch_b"], dyb)], tb=True, tks=[D], outs=[BF16], name="d_attn_out_t", **big)
    (dw_b,) = _mm([(ob_t, dyb)], tks=[min(1024, S)], outs=[F32], name="dw_branch_b", **big)
    (dyain,) = _mm([(dya, W["branch_a"])], tb=True, tks=[D], outs=[F32], name="d_lru_out", **big)
    (dw_a,) = _mm([(yain, dya)], ta=True, tks=[min(1024, S)], outs=[F32], name="dw_branch_a", **big)
    early = dict(w_branch_a=dw_a, w_branch_b=dw_b, w_out=dw_out, w_up=dw_up, w_down=dw_down)
    early_state, started = early_grads(early) if early_grads is not None else (None, 0.0)
    dq_t, dk_t, dv_t, dfk, dfq = _attn_bwd(kv, qkv_t, dob_t, ob_t, lse + started, f_row, f_rep)
    dF = jnp.pad((dfk.reshape(N_HEADS, S) + dfq.reshape(N_HEADS, S)).T, ((0, 0), (0, LANES - N_HEADS)))
    dfl, dfb = _forget_bwd(dF, fl, fb)
    dxg, dcw, dcb, dba, dbx, dlam, dbd = _lru_bwd(xg, h, dyain, cw, cb, bd, ba, bx, lam)
    wq_t, wk_t, wv_t = (W["in_qkv_t"][D * i:D * (i + 1)] for i in range(3))
    (du,) = _mm([(dxg, W["in_xg"]), (dq_t, wq_t), (dk_t, wk_t), (dv_t, wv_t), (dgates, W["in_gates"]),
                 (dfl, W["in_f"])],
                ta=[False, True, True, True, False, False], tb=[True, False, False, False, True, True],
                tks=[1024, D, D, D, 1024, LANES], outs=[F32], name="d_norm_mix_out", tm=1024, tn=512)
    tks = [min(1024, S)]
    dw_in_parts = [
        _mm([(u, dxg)], ta=True, tks=tks, outs=[F32], name="dw_in_xg", **big)[0],
        _mm([(dq_t, u)], tks=tks, outs=[F32], name="dw_in_q_t", **big)[0].T,
        _mm([(dk_t, u)], tks=tks, outs=[F32], name="dw_in_k_t", **big)[0].T,
        _mm([(dv_t, u)], tks=tks, outs=[F32], name="dw_in_v_t", **big)[0].T,
        _mm([(u, dgates)], ta=True, tks=tks, outs=[F32], name="dw_in_gates", **big)[0],
        _mm([(u, dfl)], ta=True, tks=tks, outs=[F32], name="dw_in_forget", **big)[0][:, :N_HEADS],
    ]
    grad_x, _, dg1 = _norm_bwd(du, x, g1, dx1, "norm_mix_bwd")

    dwa, dwx = _block_diag_unpair(dbd)
    big_grads = dict(early, w_in=jnp.concatenate(dw_in_parts, axis=1))
    small_grads = dict(norm_mix_g=dg1.reshape(D), conv_w=dcw, conv_b=dcb.reshape(D), lru_wa=dwa, lru_ba=dba.reshape(D),
                       lru_wx=dwx, lru_bx=dbx.reshape(D), lru_lambda=dlam.reshape(D), forget_b=dfb[0, :N_HEADS],
                       norm_mlp_g=dg2.reshape(D), norm_final_g=dg3.reshape(D))
    return loss_acc[0, 0], grad_x, big_grads, small_grads, early_state


SMALL_NAMES = ("norm_mix_g", "conv_b", "lru_wa", "lru_ba", "lru_wx", "lru_bx", "lru_lambda", "forget_b",
               "norm_mlp_g", "norm_final_g")
TILE_ELEMS = SUBLANES * LANES


def _pack_small(parts):
    rows = []
    for p in parts:
        flat = p.reshape(-1)
        flat = jnp.pad(flat, (0, (-flat.shape[0]) % TILE_ELEMS))
        rows.append(flat.reshape(-1, LANES))
    return jnp.concatenate(rows, axis=0)


def _packed_rows(shape):
    return -(-math.prod(shape) // TILE_ELEMS) * SUBLANES


def _adamw_small(g_packed, g_conv_w, weights, moms, vels):
    def rows_view(a):
        flat = a.reshape(-1)
        flat = jnp.pad(flat, (0, (-flat.shape[0]) % LANES))
        return flat.reshape(-1, LANES)

    names = SMALL_NAMES + ("conv_w",)
    views = [[rows_view(src[k]) for k in names] for src in (weights, moms, vels)]
    n = len(names)
    starts, r = [], 0
    for k in SMALL_NAMES:
        starts.append(r)
        r += _packed_rows(weights[k].shape)

    def body(*refs):
        gp_ref, gc_ref = refs[0], refs[1]
        w_refs, m_refs, v_refs = refs[2:2 + n], refs[2 + n:2 + 2 * n], refs[2 + 2 * n:2 + 3 * n]
        outs = refs[2 + 3 * n:]
        for i in range(n):
            rows = w_refs[i].shape[0]
            g = gc_ref[...] if i == n - 1 else gp_ref[starts[i]:starts[i] + rows, :]
            delta, m_new, v_new = _adamw_math(w_refs[i][...], g, m_refs[i][...], v_refs[i][...])
            for o_ref, val in zip(outs[4 * i:4 * i + 4], (g, delta, m_new, v_new)):
                o_ref[...] = val

    vm = pl.BlockSpec(memory_space=pltpu.VMEM)
    out_shape = [jax.ShapeDtypeStruct(v.shape, F32) for v in views[0] for _ in range(4)]
    res = pl.pallas_call(
        body, name="adamw_small",
        in_specs=[vm] * (2 + 3 * n), out_specs=[vm] * (4 * n), out_shape=out_shape,
    )(g_packed, g_conv_w, *views[0], *views[1], *views[2])
    dicts = ({}, {}, {}, {})
    for i, k in enumerate(names):
        size = math.prod(weights[k].shape)
        for d, arr in zip(dicts, res[4 * i:4 * i + 4]):
            d[k] = arr.reshape(-1)[:size].reshape(weights[k].shape)
    return dicts


BIG_NAMES = ("w_in", "w_branch_a", "w_branch_b", "w_out", "w_up", "w_down")
WEIGHT_ORDER = ("norm_mix_g", "w_in", "conv_w", "conv_b", "lru_wa", "lru_ba", "lru_wx", "lru_bx", "lru_lambda",
                "forget_b", "w_branch_a", "w_branch_b", "w_out", "norm_mlp_g", "w_up", "w_down", "norm_final_g")


def _to_dest_blocks(name, g):
    if g.ndim == 3:
        return g
    if name in ("w_in", "w_up"):
        return g.reshape(g.shape[0], N_DEV, g.shape[1] // N_DEV).transpose(1, 0, 2)
    return g.reshape(N_DEV, g.shape[0] // N_DEV, g.shape[1])


def kernel(x, norm_mix_g, w_in, conv_w, conv_b, lru_wa, lru_ba, lru_wx, lru_bx, lru_lambda, forget_b, w_branch_a, w_branch_b, w_out, norm_mlp_g, w_up, w_down, norm_final_g, loss_target, m_norm_mix_g, m_w_in, m_conv_w, m_conv_b, m_lru_wa, m_lru_ba, m_lru_wx, m_lru_bx, m_lru_lambda, m_forget_b, m_w_branch_a, m_w_branch_b, m_w_out, m_norm_mlp_g, m_w_up, m_w_down, m_norm_final_g, v_norm_mix_g, v_w_in, v_conv_w, v_conv_b, v_lru_wa, v_lru_ba, v_lru_wx, v_lru_bx, v_lru_lambda, v_forget_b, v_w_branch_a, v_w_branch_b, v_w_out, v_norm_mlp_g, v_w_up, v_w_down, v_norm_final_g):
    weights = dict(norm_mix_g=norm_mix_g, w_in=w_in, conv_w=conv_w, conv_b=conv_b, lru_wa=lru_wa, lru_ba=lru_ba,
                   lru_wx=lru_wx, lru_bx=lru_bx, lru_lambda=lru_lambda, forget_b=forget_b, w_branch_a=w_branch_a,
                   w_branch_b=w_branch_b, w_out=w_out, norm_mlp_g=norm_mlp_g, w_up=w_up, w_down=w_down,
                   norm_final_g=norm_final_g)
    moms = dict(norm_mix_g=m_norm_mix_g, w_in=m_w_in, conv_w=m_conv_w, conv_b=m_conv_b, lru_wa=m_lru_wa,
                lru_ba=m_lru_ba, lru_wx=m_lru_wx, lru_bx=m_lru_bx, lru_lambda=m_lru_lambda, forget_b=m_forget_b,
                w_branch_a=m_w_branch_a, w_branch_b=m_w_branch_b, w_out=m_w_out, norm_mlp_g=m_norm_mlp_g,
                w_up=m_w_up, w_down=m_w_down, norm_final_g=m_norm_final_g)
    vels = dict(norm_mix_g=v_norm_mix_g, w_in=v_w_in, conv_w=v_conv_w, conv_b=v_conv_b, lru_wa=v_lru_wa,
                lru_ba=v_lru_ba, lru_wx=v_lru_wx, lru_bx=v_lru_bx, lru_lambda=v_lru_lambda, forget_b=v_forget_b,
                w_branch_a=v_w_branch_a, w_branch_b=v_w_branch_b, w_out=v_w_out, norm_mlp_g=v_norm_mlp_g,
                w_up=v_w_up, w_down=v_w_down, norm_final_g=v_norm_final_g)
    S, D = x.shape[1], x.shape[2]
    me = 4 * lax.axis_index("x") + 2 * lax.axis_index("y") + lax.axis_index("c")

    core = lax.axis_index("c").astype(jnp.int32).reshape(1)
    chip = (2 * lax.axis_index("x") + lax.axis_index("y")).astype(jnp.int32).reshape(1)
    late_names = BIG_NAMES[1:]

    win_g, cw_g = _all_gather([w_in.astype(BF16), conv_w])
    late_shards = [weights[k].astype(BF16) for k in late_names]
    gather = _exchange_start(late_shards, [jax.ShapeDtypeStruct((N_DEV,) + s.shape, BF16) for s in late_shards],
                             _gather_copies, len(FLIPS), "gather_late_start")
    w_in_full = win_g.transpose(1, 0, 2).reshape(D, -1)
    cuts = (0, 2 * D, 5 * D, 7 * D)
    W = dict(in_xg=w_in_full[:, cuts[0]:cuts[1]], in_qkv_t=w_in_full[:, cuts[1]:cuts[2]].T,
             in_kv=w_in_full[:, cuts[1] + D:cuts[2]], in_gates=w_in_full[:, cuts[2]:cuts[3]],
             in_f=jnp.pad(w_in_full[:, cuts[3]:], ((0, 0), (0, LANES - N_HEADS))))
    small = {k: weights[k] for k in SMALL_NAMES}
    small["conv_w"] = cw_g.transpose(1, 0, 2).reshape(CONV_W, D)
    small["norm_mix_g"] = norm_mix_g + gather[4][0, 0]

    def late_weights(after):
        shards, lands = _exchange_wait(gather, _gather_copies, after, "gather_late_wait")
        wa_g, wb_g, wo_g, wup_g, wdn_g = (
            lax.dynamic_update_slice_in_dim(land, shard[None], me, axis=0) for land, shard in zip(lands, shards))
        return dict(branch_a=wa_g.reshape(D, D), branch_b=wb_g.reshape(D, D), out=wo_g.reshape(D, D),
                    up=wup_g.transpose(1, 0, 2).reshape(D, D_FF), down=wdn_g.reshape(D_FF, D))

    def core_stage(names, grads_by_name, tag):
        blocks = [_to_dest_blocks(k, grads_by_name[k]) for k in names]
        got = _reduce_scatter_cores(blocks, "reduce_scatter_cores_" + tag)
        return [_chip_partial_sum(b, g, core) for b, g in zip(blocks, got)]

    def early_grads(grads_by_name):
        sums = core_stage(late_names, grads_by_name, "early")
        wire = [s[1] for s in sums]
        scatter = _exchange_start(wire, [jax.ShapeDtypeStruct((3,) + s.shape[1:], BF16) for s in wire],
                                  _scatter_copies, 3, "scatter_early_start")
        return (sums, scatter), scatter[4][0, 0]

    loss_part, grad_x, big_grads, small_grads, (early_sums, scatter) = _local_step(
        x.reshape(S, D), loss_target.reshape(S, D), W, small, late_weights, early_grads)
    loss = lax.psum(loss_part, MESH_AXES)
    in_sums = core_stage(BIG_NAMES[:1], big_grads, "w_in")
    in_others = _reduce_scatter_chips([s[1] for s in in_sums])
    _, early_others = _exchange_wait(scatter, _scatter_copies, grad_x, "scatter_early_wait")
    sums = list(in_sums) + list(early_sums)
    others = list(in_others) + list(early_others)

    reduced = _all_reduce_small(_pack_small([small_grads[k] for k in SMALL_NAMES] + [small_grads["conv_w"]]))
    cw_full = reduced[reduced.shape[0] - _packed_rows((CONV_W, D)):].reshape(CONV_W, D)
    cw_cols = lax.dynamic_slice_in_dim(cw_full, me * (D // N_DEV), D // N_DEV, axis=1)

    grads, deltas, new_m, new_v = _adamw_small(reduced, cw_cols, weights, moms, vels)
    for k, s, g_got in zip(BIG_NAMES, sums, others):
        grads[k], deltas[k], new_m[k], new_v[k] = _adamw(weights[k], moms[k], vels[k], s[0], g_got, chip, "adamw_" + k)

    return (loss, grad_x.reshape(1, S, D), *[grads[k] for k in WEIGHT_ORDER], *[deltas[k] for k in WEIGHT_ORDER],
            *[new_m[k] for k in WEIGHT_ORDER], *[new_v[k] for k in WEIGHT_ORDER])
```

```python
import functools
import math

import jax
import jax.numpy as jnp
from jax import lax
from jax.experimental import pallas as pl
from jax.experimental.pallas import tpu as pltpu

F32 = jnp.float32
BF16 = jnp.bfloat16

D_MODEL = 1024
N_HEADS = 8
HEAD_DIM = 128
D_FF = 4096
LRU_BLOCKS = 16
LRU_BW = 64
LRU_C = 8.0
CONV_W = 4
RMS_EPS = 1e-6
N_DEV = 8
LANES = 128
SUBLANES = 8
N_GROUPS = D_MODEL // LANES
VMEM_LIMIT_BYTES = 52 * 1024 * 1024
ATTN_SCALE = 1.0 / math.sqrt(HEAD_DIM)
LOG2E = math.log2(math.e)
NEG_BIG = -1e30
ADAM_LR = 0.001
ADAM_B1 = 0.9
ADAM_B2 = 0.999
ADAM_EPS = 1e-08
ADAM_WD = 0.01
ADAM_STEP = 10
ATTN_BLOCK = 1024
ATTN_STRIP = 256
LRU_CHUNK = 256
ROW_TILE = 512
MESH_AXES = ("x", "y", "c")
MESH_ID = pl.DeviceIdType.MESH
ANY = pl.BlockSpec(memory_space=pl.ANY)

NT_DIMS = (((1,), (1,)), ((), ()))
TN_DIMS = (((0,), (0,)), ((), ()))
NN_DIMS = (((1,), (0,)), ((), ()))


def _cparams(*sem):
    return pltpu.CompilerParams(dimension_semantics=sem if sem else None, vmem_limit_bytes=VMEM_LIMIT_BYTES)


def _sigmoid(x):
    return 0.5 * (jnp.tanh(0.5 * x) + 1.0)


def _log1p_pos(e):
    u = 1.0 + e
    return jnp.where(u == 1.0, e, jnp.log(u) * (e / (u - 1.0)))


def _softplus(z):
    return jnp.maximum(z, 0.0) + _log1p_pos(jnp.exp(-jnp.abs(z)))


def _expm1_neg(x):
    series = x * (1.0 + x * 0.5 * (1.0 + x * (1.0 / 3.0) * (1.0 + x * 0.25)))
    return jnp.where(x > -0.03, series, jnp.exp(x) - 1.0)


GELU_C = math.sqrt(2.0 / math.pi)
GELU_K = 0.044715


def _gelu(x):
    return 0.5 * x * (1.0 + jnp.tanh(GELU_C * (x + GELU_K * (x * x * x))))


def _gelu_and_grad(x):
    t = jnp.tanh(GELU_C * (x + GELU_K * (x * x * x)))
    g = 0.5 * x * (1.0 + t)
    dg = 0.5 * (1.0 + t) + 0.5 * x * (1.0 - t * t) * (GELU_C * (1.0 + 3.0 * GELU_K * (x * x)))
    return g, dg


def _mm(pairs, *, ta=False, tb=False, tm, tn, tks, outs, name, epi=None, extra=(), col_blocked=False):
    n_pairs, n_extra, n_out = len(pairs), len(extra), len(outs)
    tas = list(ta) if isinstance(ta, (list, tuple)) else [ta] * n_pairs
    tbs = list(tb) if isinstance(tb, (list, tuple)) else [tb] * n_pairs
    a0, b0 = pairs[0]
    M = a0.shape[1] if tas[0] else a0.shape[0]
    N = b0.shape[0] if tbs[0] else b0.shape[1]
    tm, tn = min(tm, M), min(tn, N)
    nks, offs = [], []
    for (a, b), tk, pta in zip(pairs, tks, tas):
        K = a.shape[0] if pta else a.shape[1]
        assert K % tk == 0 and M % tm == 0 and N % tn == 0
        offs.append(sum(nks))
        nks.append(K // tk)
    nk_total = sum(nks)
    dims = [(((0 if pta else 1,), (1 if ptb else 0,)), ((), ())) for pta, ptb in zip(tas, tbs)]

    def kmap(off, nk):
        return lambda k: jnp.clip(k - off, 0, nk - 1)

    in_specs, operands = [], []
    for (a, b), tk, off, nk, pta, ptb in zip(pairs, tks, offs, nks, tas, tbs):
        km = kmap(off, nk)
        if pta:
            in_specs.append(pl.BlockSpec((tk, tm), lambda i, j, k, km=km: (km(k), i)))
        else:
            in_specs.append(pl.BlockSpec((tm, tk), lambda i, j, k, km=km: (i, km(k))))
        if ptb:
            in_specs.append(pl.BlockSpec((tn, tk), lambda i, j, k, km=km: (j, km(k))))
        else:
            in_specs.append(pl.BlockSpec((tk, tn), lambda i, j, k, km=km: (km(k), j)))
        operands += [a, b]
    for e in extra:
        in_specs.append(pl.BlockSpec((tm, tn), lambda i, j, k: (i, j)))
        operands.append(e)

    def body(*refs):
        ab = refs[:2 * n_pairs]
        ex = refs[2 * n_pairs:2 * n_pairs + n_extra]
        o = refs[2 * n_pairs + n_extra:2 * n_pairs + n_extra + n_out]
        k = pl.program_id(2)

        def finish(acc):
            res = epi(acc, *[e[...] for e in ex]) if epi is not None else (acc,)
            for r, oref in zip(res, o):
                oref[...] = r.astype(oref.dtype)

        if nk_total == 1:
            finish(lax.dot_general(ab[0][...], ab[1][...], dims[0], preferred_element_type=F32))
            return
        acc = refs[-1]
        for p in range(n_pairs):
            a_ref, b_ref = ab[2 * p], ab[2 * p + 1]

            @pl.when((k >= offs[p]) & (k < offs[p] + nks[p]))
            def _(a_ref=a_ref, b_ref=b_ref, pdims=dims[p]):
                prod = lax.dot_general(a_ref[...], b_ref[...], pdims, preferred_element_type=F32)

                @pl.when(k == 0)
                def _():
                    acc[...] = prod

                @pl.when(k > 0)
                def _():
                    acc[...] += prod

        @pl.when(k == nk_total - 1)
        def _():
            finish(acc[...])

    return pl.pallas_call(
        body,
        name=name,
        grid=(M // tm, N // tn, nk_total),
        in_specs=in_specs,
        out_specs=[pl.BlockSpec((None, tm, tn), lambda i, j, k: (j, i, 0)) if col_blocked
                   else pl.BlockSpec((tm, tn), lambda i, j, k: (i, j)) for _ in outs],
        out_shape=[jax.ShapeDtypeStruct((N // tn, M, tn) if col_blocked else (M, N), dt) for dt in outs],
        scratch_shapes=[] if nk_total == 1 else [pltpu.VMEM((tm, tn), F32)],
        compiler_params=_cparams("parallel", "parallel", "arbitrary"),
    )(*operands)


def _norm_fwd(x, g, name):
    S, D = x.shape
    tr = min(ROW_TILE, S)

    def body(x_ref, g_ref, o_ref):
        xv = x_ref[...]
        r = lax.rsqrt(jnp.mean(xv * xv, axis=-1, keepdims=True) + RMS_EPS)
        o_ref[...] = ((xv * r) * g_ref[...]).astype(o_ref.dtype)

    return pl.pallas_call(
        body, name=name, grid=(S // tr,),
        in_specs=[pl.BlockSpec((tr, D), lambda i: (i, 0)), pl.BlockSpec((1, D), lambda i: (0, 0))],
        out_specs=pl.BlockSpec((tr, D), lambda i: (i, 0)),
        out_shape=jax.ShapeDtypeStruct((S, D), BF16),
        compiler_params=_cparams("parallel"),
    )(x, g.reshape(1, D))


def _rms_bwd_rows(dy, xv, g):
    r = lax.rsqrt(jnp.mean(xv * xv, axis=-1, keepdims=True) + RMS_EPS)
    xn = xv * r
    dxn = dy * g
    dx = r * (dxn - xn * jnp.mean(dxn * xn, axis=-1, keepdims=True))
    dg = jnp.sum(dy * xn, axis=0, keepdims=True)
    return dx, dg


def _norm_bwd(dy, x, g, dres, name):
    S, D = x.shape
    tr = min(ROW_TILE, S)

    def body(dy_ref, x_ref, g_ref, dres_ref, dx_ref, dxb_ref, dg_ref):
        dx, dg = _rms_bwd_rows(dy_ref[...], x_ref[...], g_ref[...])
        dx = dres_ref[...] + dx
        dx_ref[...] = dx
        dxb_ref[...] = dx.astype(BF16)

        @pl.when(pl.program_id(0) == 0)
        def _():
            dg_ref[...] = jnp.zeros_like(dg_ref)

        dg_ref[...] += dg

    row = pl.BlockSpec((tr, D), lambda i: (i, 0))
    vec = pl.BlockSpec((1, D), lambda i: (0, 0))
    return pl.pallas_call(
        body, name=name, grid=(S // tr,),
        in_specs=[row, row, vec, row],
        out_specs=[row, row, vec],
        out_shape=[jax.ShapeDtypeStruct((S, D), F32), jax.ShapeDtypeStruct((S, D), BF16),
                   jax.ShapeDtypeStruct((1, D), F32)],
        compiler_params=_cparams("arbitrary"),
    )(dy, x, g.reshape(1, D), dres)


def _final_norm_loss(x2, target, g):
    S, D = x2.shape
    tr = min(ROW_TILE, S)

    def body(x_ref, t_ref, g_ref, loss_ref, dg_ref, dx_ref, dxb_ref):
        xv = x_ref[...]
        gv = g_ref[...]
        r = lax.rsqrt(jnp.mean(xv * xv, axis=-1, keepdims=True) + RMS_EPS)
        y = (xv * r) * gv
        err = y - t_ref[...]
        part = 0.5 * jnp.sum(jnp.mean(err * err, axis=-1, keepdims=True), axis=0, keepdims=True)
        dy = err * (1.0 / D)
        dx, dg = _rms_bwd_rows(dy, xv, gv)
        dx_ref[...] = dx
        dxb_ref[...] = dx.astype(BF16)

        @pl.when(pl.program_id(0) == 0)
        def _():
            dg_ref[...] = jnp.zeros_like(dg_ref)
            loss_ref[...] = jnp.zeros_like(loss_ref)

        dg_ref[...] += dg
        loss_ref[...] += jnp.broadcast_to(part, loss_ref.shape)

    row = pl.BlockSpec((tr, D), lambda i: (i, 0))
    vec = pl.BlockSpec((1, D), lambda i: (0, 0))
    return pl.pallas_call(
        body, name="final_norm_loss", grid=(S // tr,),
        in_specs=[row, row, vec],
        out_specs=[pl.BlockSpec((SUBLANES, LANES), lambda i: (0, 0)), vec, row, row],
        out_shape=[jax.ShapeDtypeStruct((SUBLANES, LANES), F32), jax.ShapeDtypeStruct((1, D), F32),
                   jax.ShapeDtypeStruct((S, D), F32), jax.ShapeDtypeStruct((S, D), BF16)],
        compiler_params=_cparams("arbitrary"),
    )(x2, target, g.reshape(1, D))


def _lru_gates(xa, bd_j, ba_j, bx_j, sp_j):
    z = jnp.dot(xa.astype(BF16), bd_j, preferred_element_type=F32)
    r = _sigmoid(z[:, :LANES] + ba_j)
    ig = _sigmoid(z[:, LANES:] + bx_j)
    log_a = (-LRU_C) * r * sp_j
    a = jnp.exp(log_a)
    mult = jnp.sqrt(-_expm1_neg(2.0 * log_a))
    return r, ig, a, mult


def _conv_rows(xpad, cw_ref, cb_ref, sl, tc):
    out = jnp.broadcast_to(cb_ref[:, sl], (tc, LANES))
    for k in range(CONV_W):
        out = out + xpad[pl.ds(SUBLANES - (CONV_W - 1) + k, tc), sl] * cw_ref[k:k + 1, sl]
    return out


def _lru_fwd(xg, cw, cb, bd, ba, bx, lam):
    S = xg.shape[0]
    D = D_MODEL
    tc = min(LRU_CHUNK, S)
    hb = tc // SUBLANES

    def body(xl_ref, halo_ref, g_ref, cw_ref, cb_ref, bd_ref, ba_ref, bx_ref, lam_ref,
             h_ref, y_ref, xpad, a_s, b_s, carry):
        i = pl.program_id(0)

        @pl.when(i == 0)
        def _():
            carry[...] = jnp.zeros_like(carry)

        xpad[0:SUBLANES, :] = jnp.where(i > 0, halo_ref[...], 0.0)
        xpad[SUBLANES:, :] = xl_ref[...]
        for j in range(N_GROUPS):
            sl = slice(LANES * j, LANES * (j + 1))
            xa = _conv_rows(xpad, cw_ref, cb_ref, sl, tc)
            sp = _softplus(-lam_ref[:, sl])
            _, ig, a, mult = _lru_gates(xa, bd_ref[j], ba_ref[:, sl], bx_ref[:, sl], sp)
            a_s[:, sl] = a
            b_s[:, sl] = mult * (ig * xa)

        row = lax.broadcasted_iota(jnp.int32, (SUBLANES, D), 0)

        def step(t, c):
            o = pl.multiple_of(t * SUBLANES, SUBLANES)
            A = a_s[pl.ds(o, SUBLANES), :]
            B = b_s[pl.ds(o, SUBLANES), :]
            for d in (1, 2, 4):
                keep = row >= d
                a_sh = jnp.where(keep, pltpu.roll(A, d, 0), 1.0)
                b_sh = jnp.where(keep, pltpu.roll(B, d, 0), 0.0)
                B = A * b_sh + B
                A = A * a_sh
            hh = A * c + B
            h_ref[pl.ds(o, SUBLANES), :] = hh
            return jnp.broadcast_to(hh[SUBLANES - 1:SUBLANES, :], (SUBLANES, D))

        carry[...] = lax.fori_loop(0, hb, step, carry[...])
        y_ref[...] = (_gelu(g_ref[...]) * h_ref[...]).astype(BF16)

    row_spec = lambda col: pl.BlockSpec((tc, D), lambda i, col=col: (i, col))
    halo = pl.BlockSpec((SUBLANES, D), lambda i: (jnp.maximum(i * hb - 1, 0), 0))
    full = lambda shape: pl.BlockSpec(shape, lambda i: tuple(0 for _ in shape))
    return pl.pallas_call(
        body, name="lru_fwd", grid=(S // tc,),
        in_specs=[row_spec(0), halo, row_spec(1), full((CONV_W, D)), full((1, D)),
                  full((N_GROUPS, LANES, 2 * LANES)), full((1, D)), full((1, D)), full((1, D))],
        out_specs=[pl.BlockSpec((tc, D), lambda i: (i, 0)), pl.BlockSpec((tc, D), lambda i: (i, 0))],
        out_shape=[jax.ShapeDtypeStruct((S, D), F32), jax.ShapeDtypeStruct((S, D), BF16)],
        scratch_shapes=[pltpu.VMEM((tc + SUBLANES, D), F32), pltpu.VMEM((tc, D), F32),
                        pltpu.VMEM((tc, D), F32), pltpu.VMEM((SUBLANES, D), F32)],
        compiler_params=_cparams("arbitrary"),
    )(xg, xg, xg, cw, cb, bd, ba, bx, lam)


def _lru_bwd(xg, h, dyain, cw, cb, bd, ba, bx, lam):
    S = xg.shape[0]
    D = D_MODEL
    tc = min(LRU_CHUNK, S)
    hb = tc // SUBLANES
    nc = S // tc

    def body(xl_ref, xhalo_ref, g_ref, h_ref, hhalo_ref, dy_ref, cw_ref, cb_ref, bd_ref, ba_ref, bx_ref,
             lam_ref, dxg_ref, dcw_ref, dcb_ref, dba_ref, dbx_ref, dlam_ref, dbd_ref,
             xpad, hpad, a_s, b_s, dh_s, g_s, xa_s, r_s, ig_s, m_s, dxa_pad, carry_e, dxa_head):
        i = pl.program_id(0)
        c = nc - 1 - i

        @pl.when(i == 0)
        def _():
            carry_e[...] = jnp.zeros_like(carry_e)
            dxa_head[...] = jnp.zeros_like(dxa_head)
            for ref in (dcw_ref, dcb_ref, dba_ref, dbx_ref, dlam_ref, dbd_ref):
                ref[...] = jnp.zeros_like(ref)

        xpad[0:SUBLANES, :] = jnp.where(c > 0, xhalo_ref[...], 0.0)
        xpad[SUBLANES:, :] = xl_ref[...]
        hpad[0:SUBLANES, :] = jnp.where(c > 0, hhalo_ref[...], 0.0)
        hpad[SUBLANES:, :] = h_ref[...]

        for j in range(N_GROUPS):
            sl = slice(LANES * j, LANES * (j + 1))
            xa = _conv_rows(xpad, cw_ref, cb_ref, sl, tc)
            sp = _softplus(-lam_ref[:, sl])
            r, ig, a, mult = _lru_gates(xa, bd_ref[j], ba_ref[:, sl], bx_ref[:, sl], sp)
            gl, dgl = _gelu_and_grad(g_ref[:, sl])
            dy = dy_ref[:, sl]
            dh = dy * gl
            dxg_ref[:, D + LANES * j:D + LANES * (j + 1)] = (dy * h_ref[:, sl] * dgl).astype(BF16)
            a_s[:, sl] = a
            b_s[:, sl] = a * dh
            dh_s[:, sl] = dh
            xa_s[:, sl] = xa
            r_s[:, sl] = r
            ig_s[:, sl] = ig
            m_s[:, sl] = mult

        row = lax.broadcasted_iota(jnp.int32, (SUBLANES, D), 0)

        def step(tt, ce):
            o = pl.multiple_of((hb - 1 - tt) * SUBLANES, SUBLANES)
            A = a_s[pl.ds(o, SUBLANES), :]
            B = b_s[pl.ds(o, SUBLANES), :]
            for d in (1, 2, 4):
                keep = row < SUBLANES - d
                a_sh = jnp.where(keep, pltpu.roll(A, SUBLANES - d, 0), 1.0)
                b_sh = jnp.where(keep, pltpu.roll(B, SUBLANES - d, 0), 0.0)
                B = A * b_sh + B
                A = A * a_sh
            e = A * ce + B
            e_next = jnp.where(row < SUBLANES - 1, pltpu.roll(e, SUBLANES - 1, 0), ce)
            g_s[pl.ds(o, SUBLANES), :] = dh_s[pl.ds(o, SUBLANES), :] + e_next
            return jnp.broadcast_to(e[0:1, :], (SUBLANES, D))

        carry_e[...] = lax.fori_loop(0, hb, step, carry_e[...])

        for j in range(N_GROUPS):
            sl = slice(LANES * j, LANES * (j + 1))
            gg = g_s[:, sl]
            xa, r, ig, mult, a = xa_s[:, sl], r_s[:, sl], ig_s[:, sl], m_s[:, sl], a_s[:, sl]
            hprev = hpad[pl.ds(SUBLANES - 1, tc), sl]
            sp = _softplus(-lam_ref[:, sl])
            da = gg * hprev
            dmult = gg * (ig * xa)
            dig = gg * (mult * xa)
            dxa = gg * (mult * ig)
            dla = da * a - dmult * ((a * a) / mult)
            dr = dla * ((-LRU_C) * sp)
            dlam_ref[:, sl] += jnp.sum(dla * r, axis=0, keepdims=True)
            dza = dr * r * (1.0 - r)
            dzx = dig * ig * (1.0 - ig)
            dba_ref[:, sl] += jnp.sum(dza, axis=0, keepdims=True)
            dbx_ref[:, sl] += jnp.sum(dzx, axis=0, keepdims=True)
            dz = jnp.concatenate([dza, dzx], axis=1).astype(BF16)
            dbd_ref[j] += lax.dot_general(xa.astype(BF16), dz, TN_DIMS, preferred_element_type=F32)
            dxa = dxa + lax.dot_general(dz, bd_ref[j], NT_DIMS, preferred_element_type=F32)
            dxa_pad[0:tc, sl] = dxa

        dxa_pad[tc:, :] = dxa_head[...]
        dxa_head[...] = dxa_pad[0:SUBLANES, :]

        for j in range(N_GROUPS):
            sl = slice(LANES * j, LANES * (j + 1))
            dxa = dxa_pad[0:tc, sl]
            dxl = jnp.zeros((tc, LANES), F32)
            for k in range(CONV_W):
                dxl = dxl + dxa_pad[pl.ds(CONV_W - 1 - k, tc), sl] * cw_ref[k:k + 1, sl]
                dcw_ref[k:k + 1, sl] += jnp.sum(
                    dxa * xpad[pl.ds(SUBLANES - (CONV_W - 1) + k, tc), sl], axis=0, keepdims=True)
            dxg_ref[:, sl] = dxl.astype(BF16)
            dcb_ref[:, sl] += jnp.sum(dxa, axis=0, keepdims=True)

        @pl.when(i == nc - 1)
        def _():
            dlam_ref[...] = dlam_ref[...] * (LRU_C * _sigmoid(-lam_ref[...]))

    rev = lambda col: pl.BlockSpec((tc, D), lambda i, col=col: (nc - 1 - i, col))
    halo = pl.BlockSpec((SUBLANES, D), lambda i: (jnp.maximum((nc - 1 - i) * hb - 1, 0), 0))
    full = lambda shape: pl.BlockSpec(shape, lambda i: tuple(0 for _ in shape))
    big = lambda: pltpu.VMEM((tc, D), F32)
    return pl.pallas_call(
        body, name="lru_bwd", grid=(nc,),
        in_specs=[rev(0), halo, rev(1), rev(0), halo, rev(0), full((CONV_W, D)), full((1, D)),
                  full((N_GROUPS, LANES, 2 * LANES)), full((1, D)), full((1, D)), full((1, D))],
        out_specs=[pl.BlockSpec((tc, 2 * D), lambda i: (nc - 1 - i, 0)), full((CONV_W, D)), full((1, D)),
                   full((1, D)), full((1, D)), full((1, D)), full((N_GROUPS, LANES, 2 * LANES))],
        out_shape=[jax.ShapeDtypeStruct((S, 2 * D), BF16), jax.ShapeDtypeStruct((CONV_W, D), F32),
                   jax.ShapeDtypeStruct((1, D), F32), jax.ShapeDtypeStruct((1, D), F32),
                   jax.ShapeDtypeStruct((1, D), F32), jax.ShapeDtypeStruct((1, D), F32),
                   jax.ShapeDtypeStruct((N_GROUPS, LANES, 2 * LANES), F32)],
        scratch_shapes=[pltpu.VMEM((tc + SUBLANES, D), F32), pltpu.VMEM((tc + SUBLANES, D), F32),
                        big(), big(), big(), big(), big(), big(), big(), big(),
                        pltpu.VMEM((tc + SUBLANES, D), F32), pltpu.VMEM((SUBLANES, D), F32),
                        pltpu.VMEM((SUBLANES, D), F32)],
        compiler_params=_cparams("arbitrary"),
    )(xg, xg, xg, h, h, dyain, cw, cb, bd, ba, bx, lam)


def _forget_cumsum(fl, fb):
    S = fl.shape[0]
    tr = min(ROW_TILE, S)
    hb = tr // SUBLANES

    def body(fl_ref, fb_ref, o_ref, rep_ref, lf_s, carry):
        @pl.when(pl.program_id(0) == 0)
        def _():
            carry[...] = jnp.zeros_like(carry)

        lf_s[...] = -_softplus(-(fl_ref[...] + fb_ref[...]))
        row = lax.broadcasted_iota(jnp.int32, (SUBLANES, LANES), 0)

        def step(t, c):
            o = pl.multiple_of(t * SUBLANES, SUBLANES)
            B = lf_s[pl.ds(o, SUBLANES), :]
            for d in (1, 2, 4):
                B = B + jnp.where(row >= d, pltpu.roll(B, d, 0), 0.0)
            B = B + c
            o_ref[pl.ds(o, SUBLANES), :] = B * LOG2E
            return jnp.broadcast_to(B[SUBLANES - 1:SUBLANES, :], (SUBLANES, LANES))

        carry[...] = lax.fori_loop(0, hb, step, carry[...])
        for h in range(N_HEADS):
            rep_ref[h] = jnp.broadcast_to(o_ref[:, h:h + 1], (tr, LANES))

    return pl.pallas_call(
        body, name="forget_cumsum", grid=(S // tr,),
        in_specs=[pl.BlockSpec((tr, LANES), lambda i: (i, 0)), pl.BlockSpec((1, LANES), lambda i: (0, 0))],
        out_specs=[pl.BlockSpec((tr, LANES), lambda i: (i, 0)),
                   pl.BlockSpec((N_HEADS, tr, LANES), lambda i: (0, i, 0))],
        out_shape=[jax.ShapeDtypeStruct((S, LANES), F32), jax.ShapeDtypeStruct((N_HEADS, S, LANES), F32)],
        scratch_shapes=[pltpu.VMEM((tr, LANES), F32), pltpu.VMEM((SUBLANES, LANES), F32)],
        compiler_params=_cparams("arbitrary"),
    )(fl, fb)


def _forget_bwd(dF, fl, fb):
    S = fl.shape[0]
    tr = min(ROW_TILE, S)
    hb = tr // SUBLANES
    nc = S // tr

    def body(df_ref, fl_ref, fb_ref, o_ref, dfb_ref, carry):
        @pl.when(pl.program_id(0) == 0)
        def _():
            carry[...] = jnp.zeros_like(carry)
            dfb_ref[...] = jnp.zeros_like(dfb_ref)

        row = lax.broadcasted_iota(jnp.int32, (SUBLANES, LANES), 0)

        def step(tt, carried):
            c, acc = carried
            o = pl.multiple_of((hb - 1 - tt) * SUBLANES, SUBLANES)
            B = df_ref[pl.ds(o, SUBLANES), :]
            for d in (1, 2, 4):
                B = B + jnp.where(row < SUBLANES - d, pltpu.roll(B, SUBLANES - d, 0), 0.0)
            B = B + c
            z = fl_ref[pl.ds(o, SUBLANES), :] + fb_ref[...]
            dz = B * _sigmoid(-z)
            o_ref[pl.ds(o, SUBLANES), :] = dz.astype(BF16)
            return jnp.broadcast_to(B[0:1, :], (SUBLANES, LANES)), acc + dz

        c, acc = lax.fori_loop(0, hb, step, (carry[...], jnp.zeros((SUBLANES, LANES), F32)))
        carry[...] = c
        dfb_ref[...] += jnp.sum(acc, axis=0, keepdims=True)

    rev = pl.BlockSpec((tr, LANES), lambda i: (nc - 1 - i, 0))
    vec = pl.BlockSpec((1, LANES), lambda i: (0, 0))
    return pl.pallas_call(
        body, name="forget_bwd", grid=(nc,),
        in_specs=[rev, rev, vec],
        out_specs=[rev, vec],
        out_shape=[jax.ShapeDtypeStruct((S, LANES), BF16), jax.ShapeDtypeStruct((1, LANES), F32)],
        scratch_shapes=[pltpu.VMEM((SUBLANES, LANES), F32)],
        compiler_params=_cparams("arbitrary"),
    )(dF, fl, fb)


def _triangle(n, key_major):
    pairs = [(q, k) for q in range(n) for k in range(q + 1)]
    if key_major:
        pairs.sort(key=lambda qk: (qk[1], qk[0]))
    return (jnp.asarray([q for q, _ in pairs], jnp.int32), jnp.asarray([k for _, k in pairs], jnp.int32))


def _strip_scores(k_ref, qt_ref, fk_ref, j, strip, nkeys, diagonal):
    cols = slice(strip * j, strip * (j + 1))
    s = jnp.dot(k_ref[0:nkeys, :], qt_ref[:, cols], preferred_element_type=F32) * (ATTN_SCALE * LOG2E)
    fk = fk_ref[0:nkeys, :]
    s = s - jnp.concatenate([fk] * (strip // LANES), axis=1)
    keep = None
    if diagonal:
        keys = lax.broadcasted_iota(jnp.int32, (nkeys, strip), 0)
        queries = lax.broadcasted_iota(jnp.int32, (nkeys, strip), 1) + strip * j
        keep = keys <= queries
    return s, keep


def _attn_fwd(kv, qkv_t, f_row, f_rep):
    S = kv.shape[0]
    blk = min(ATTN_BLOCK, S)
    strip = min(ATTN_STRIP, blk)
    n = S // blk
    tri_q, tri_k = _triangle(n, key_major=False)
    ones_rows = 2 * SUBLANES

    def body(tq_ref, tk_ref, k_ref, qt_ref, vt_ref, fq_ref, fk_ref, ot_ref, lse_ref, m_s, acc_s, vta_s):
        t = pl.program_id(1)
        qi, ki = tq_ref[t], tk_ref[t]

        @pl.when(ki == 0)
        def _():
            m_s[...] = jnp.full_like(m_s, NEG_BIG)
            acc_s[...] = jnp.zeros_like(acc_s)

        vta_s[0:HEAD_DIM, :] = vt_ref[...]
        vta_s[HEAD_DIM:, :] = jnp.ones((ones_rows, blk), BF16)

        def update(diagonal):
            n_strips = blk // strip
            keys_of = lambda j: strip * (j + 1) if diagonal else blk
            scores = lambda j: _strip_scores(k_ref, qt_ref, fk_ref, j, strip, keys_of(j), diagonal)
            def weighted_values(j, alpha, pb):
                cols = slice(strip * j, strip * (j + 1))
                acc_s[:, cols] = alpha * acc_s[:, cols] + jnp.dot(
                    vta_s[:, 0:keys_of(j)], pb, preferred_element_type=F32)

            ahead, behind = scores(0), None
            for j in range(n_strips):
                cols = slice(strip * j, strip * (j + 1))
                (s, keep), ahead = ahead, (scores(j + 1) if j + 1 < n_strips else None)
                if behind is not None:
                    weighted_values(*behind)
                if diagonal:
                    s = jnp.where(keep, s, NEG_BIG)
                fq = fq_ref[:, cols]
                m_old = m_s[:, cols]
                m_new = jnp.maximum(m_old, jnp.max(s, axis=0, keepdims=True) + fq)
                p = jnp.exp2(s - (m_new - fq))
                behind = (j, jnp.exp2(m_old - m_new), p.astype(BF16))
                m_s[:, cols] = m_new
            weighted_values(*behind)

        @pl.when(ki < qi)
        def _():
            update(False)

        @pl.when(ki == qi)
        def _():
            update(True)
            denom = acc_s[HEAD_DIM:HEAD_DIM + 1, :]
            ot_ref[...] = (acc_s[0:HEAD_DIM, :] / denom).astype(BF16)
            lse_ref[...] = m_s[...] + jnp.log2(denom)

    return pl.pallas_call(
        body, name="attn_fwd",
        grid_spec=pltpu.PrefetchScalarGridSpec(
            num_scalar_prefetch=2, grid=(N_HEADS, tri_q.shape[0]),
            in_specs=[pl.BlockSpec((blk, HEAD_DIM), lambda h, t, tq, tk: (tk[t], h)),
                      pl.BlockSpec((HEAD_DIM, blk), lambda h, t, tq, tk: (h, tq[t])),
                      pl.BlockSpec((HEAD_DIM, blk), lambda h, t, tq, tk: (2 * N_HEADS + h, tk[t])),
                      pl.BlockSpec((None, 1, blk), lambda h, t, tq, tk: (h, 0, tq[t])),
                      pl.BlockSpec((None, blk, LANES), lambda h, t, tq, tk: (h, tk[t], 0))],
            out_specs=[pl.BlockSpec((HEAD_DIM, blk), lambda h, t, tq, tk: (h, tq[t])),
                       pl.BlockSpec((None, 1, blk), lambda h, t, tq, tk: (h, 0, tq[t]))],
            scratch_shapes=[pltpu.VMEM((1, blk), F32), pltpu.VMEM((HEAD_DIM + ones_rows, blk), F32),
                            pltpu.VMEM((HEAD_DIM + ones_rows, blk), BF16)]),
        out_shape=[jax.ShapeDtypeStruct((N_HEADS * HEAD_DIM, S), BF16), jax.ShapeDtypeStruct((N_HEADS, 1, S), F32)],
        compiler_params=_cparams("parallel", "arbitrary"),
    )(tri_q, tri_k, kv, qkv_t, qkv_t, f_row, f_rep)


def _attn_bwd(kv, qkv_t, do_t, o_t, lse, f_row, f_rep):
    S = kv.shape[0]
    blk = min(ATTN_BLOCK, S)
    strip = min(ATTN_STRIP, blk)
    n = S // blk
    tri_q, tri_k = _triangle(n, key_major=True)
    n_tiles = tri_q.shape[0]

    def body(tq_ref, tk_ref, k_ref, v_ref, qt_ref, kt_ref, dot_ref, ot_ref, lse_ref, fq_ref, fk_ref,
             dqt_ref, dkt_ref, dvt_ref, dfk_ref, dfq_ref, dq_s, dk_s, dv_s, dfk_s, dfq_s, row_s):
        t = pl.program_id(1)
        qi, ki = tq_ref[t], tk_ref[t]

        @pl.when(t == 0)
        def _():
            dq_s[...] = jnp.zeros_like(dq_s)
            dfq_s[...] = jnp.zeros_like(dfq_s)

        @pl.when(qi == ki)
        def _():
            dk_s[...] = jnp.zeros_like(dk_s)
            dv_s[...] = jnp.zeros_like(dv_s)
            dfk_s[...] = jnp.zeros_like(dfk_s)

        def update(diagonal):
            row_s[...] = fq_ref[...] - lse_ref[...]
            n_strips = blk // strip
            keys_of = lambda j: strip * (j + 1) if diagonal else blk

            def matmuls_in(j):
                s, keep = _strip_scores(k_ref, qt_ref, fk_ref, j, strip, keys_of(j), diagonal)
                dp = jnp.dot(v_ref[0:keys_of(j), :], dot_ref[:, strip * j:strip * (j + 1)], preferred_element_type=F32)
                return s, keep, dp

            def matmuls_out(j, pb, dsb):
                cols = slice(strip * j, strip * (j + 1))
                nkeys = keys_of(j)
                dv_s[:, 0:nkeys] += lax.dot_general(dot_ref[:, cols], pb, NT_DIMS, preferred_element_type=F32)
                dk_s[:, 0:nkeys] += lax.dot_general(qt_ref[:, cols], dsb, NT_DIMS, preferred_element_type=F32)
                dq_s[qi, :, cols] += jnp.dot(kt_ref[:, 0:nkeys], dsb, preferred_element_type=F32)

            ahead, behind = matmuls_in(0), None
            for j in range(n_strips):
                cols = slice(strip * j, strip * (j + 1))
                nkeys = keys_of(j)
                (s, keep, dp), ahead = ahead, (matmuls_in(j + 1) if j + 1 < n_strips else None)
                if behind is not None:
                    matmuls_out(*behind)
                p = jnp.exp2(s + row_s[:, cols])
                if diagonal:
                    p = jnp.where(keep, p, 0.0)
                dot = dot_ref[:, cols]
                delta = jnp.sum(dot.astype(F32) * ot_ref[:, cols].astype(F32), axis=0, keepdims=True)
                ds = p * (dp - delta)
                behind = (j, p.astype(BF16), ds.astype(BF16))
                lane_part = ds[:, 0:LANES]
                for g in range(1, strip // LANES):
                    lane_part = lane_part + ds[:, LANES * g:LANES * (g + 1)]
                dfk_s[0:nkeys, :] += lane_part
                sub_part = ds[0:SUBLANES, :]
                for g in range(1, nkeys // SUBLANES):
                    sub_part = sub_part + ds[SUBLANES * g:SUBLANES * (g + 1), :]
                dfq_s[qi, :, cols] += sub_part
            matmuls_out(*behind)

        @pl.when(qi == ki)
        def _():
            update(True)

        @pl.when(qi > ki)
        def _():
            update(False)

        @pl.when(qi == n - 1)
        def _():
            dkt_ref[...] = (dk_s[...] * ATTN_SCALE).astype(BF16)
            dvt_ref[...] = dv_s[...].astype(BF16)
            dfk_ref[...] = -jnp.sum(dfk_s[...].T, axis=0, keepdims=True)

        @pl.when(t == n_tiles - 1)
        def _():
            for j in range(n):
                dqt_ref[:, blk * j:blk * (j + 1)] = (dq_s[j] * ATTN_SCALE).astype(BF16)
                dfq_ref[:, blk * j:blk * (j + 1)] = jnp.sum(dfq_s[j], axis=0, keepdims=True)

    q_feat = pl.BlockSpec((HEAD_DIM, blk), lambda h, t, tq, tk: (h, tq[t]))
    q_row = pl.BlockSpec((None, 1, blk), lambda h, t, tq, tk: (h, 0, tq[t]))
    k_feat = pl.BlockSpec((HEAD_DIM, blk), lambda h, t, tq, tk: (h, tk[t]))
    return pl.pallas_call(
        body, name="attn_bwd",
        grid_spec=pltpu.PrefetchScalarGridSpec(
            num_scalar_prefetch=2, grid=(N_HEADS, n_tiles),
            in_specs=[pl.BlockSpec((blk, HEAD_DIM), lambda h, t, tq, tk: (tk[t], h)),
                      pl.BlockSpec((blk, HEAD_DIM), lambda h, t, tq, tk: (tk[t], N_HEADS + h)),
                      q_feat,
                      pl.BlockSpec((HEAD_DIM, blk), lambda h, t, tq, tk: (N_HEADS + h, tk[t])),
                      q_feat, q_feat, q_row, q_row,
                      pl.BlockSpec((None, blk, LANES), lambda h, t, tq, tk: (h, tk[t], 0))],
            out_specs=[pl.BlockSpec((HEAD_DIM, S), lambda h, t, tq, tk: (h, 0)), k_feat, k_feat,
                       pl.BlockSpec((None, 1, blk), lambda h, t, tq, tk: (h, 0, tk[t])),
                       pl.BlockSpec((None, 1, S), lambda h, t, tq, tk: (h, 0, 0))],
            scratch_shapes=[pltpu.VMEM((n, HEAD_DIM, blk), F32), pltpu.VMEM((HEAD_DIM, blk), F32),
                            pltpu.VMEM((HEAD_DIM, blk), F32), pltpu.VMEM((blk, LANES), F32),
                            pltpu.VMEM((n, SUBLANES, blk), F32), pltpu.VMEM((1, blk), F32)]),
        out_shape=[jax.ShapeDtypeStruct((N_HEADS * HEAD_DIM, S), BF16)] * 3
        + [jax.ShapeDtypeStruct((N_HEADS, 1, S), F32), jax.ShapeDtypeStruct((N_HEADS, 1, S), F32)],
        compiler_params=_cparams("parallel", "arbitrary"),
    )(tri_q, tri_k, kv, kv, qkv_t, qkv_t, do_t, o_t, lse, f_row, f_rep)


def _gate_mix(gates, ya, yb):
    S, D = ya.shape
    tr = min(ROW_TILE, S)

    def body(ga_ref, gb_ref, ya_ref, yb_ref, o_ref):
        o_ref[...] = (_sigmoid(ga_ref[...]) * ya_ref[...] + _sigmoid(gb_ref[...]) * yb_ref[...]).astype(BF16)

    col = lambda j: pl.BlockSpec((tr, D), lambda i, j=j: (i, j))
    return pl.pallas_call(
        body, name="gate_mix", grid=(S // tr,),
        in_specs=[col(0), col(1), col(0), col(0)],
        out_specs=col(0),
        out_shape=jax.ShapeDtypeStruct((S, D), BF16),
        compiler_params=_cparams("parallel"),
    )(gates, gates, ya, yb)


def _gate_bwd(dmix, gates, ya, yb):
    S, D = ya.shape
    tr = min(ROW_TILE, S)

    def body(dm_ref, ga_ref, gb_ref, ya_ref, yb_ref, dya_ref, dyb_ref, dg_ref):
        dm = dm_ref[...]
        sa, sb = _sigmoid(ga_ref[...]), _sigmoid(gb_ref[...])
        dya_ref[...] = (dm * sa).astype(BF16)
        dyb_ref[...] = (dm * sb).astype(BF16)
        dg_ref[:, 0:D] = ((dm * ya_ref[...]) * (sa * (1.0 - sa))).astype(BF16)
        dg_ref[:, D:] = ((dm * yb_ref[...]) * (sb * (1.0 - sb))).astype(BF16)

    col = lambda j: pl.BlockSpec((tr, D), lambda i, j=j: (i, j))
    return pl.pallas_call(
        body, name="gate_bwd", grid=(S // tr,),
        in_specs=[col(0), col(0), col(1), col(0), col(0)],
        out_specs=[col(0), col(0), pl.BlockSpec((tr, 2 * D), lambda i: (i, 0))],
        out_shape=[jax.ShapeDtypeStruct((S, D), BF16), jax.ShapeDtypeStruct((S, D), BF16),
                   jax.ShapeDtypeStruct((S, 2 * D), BF16)],
        compiler_params=_cparams("parallel"),
    )(dmix, gates, gates, ya, yb)


def _mesh_place():
    x, y, c = lax.axis_index("x"), lax.axis_index("y"), lax.axis_index("c")
    chips = [(1 - x, y), (x, 1 - y), (1 - x, 1 - y)]
    return x, y, c, chips


def _all_gather(shards):
    n = len(shards)

    def body(*refs):
        ins, outs = refs[:n], refs[n:2 * n]
        send_sems, recv_sems, local_sems = refs[2 * n:]
        x, y, c, chips = _mesh_place()
        me, sib = (x, y, c), (x, y, 1 - c)

        def copy(a, k, block, to, src=None):
            px, py, pc = block
            dst = outs[a].at[4 * px + 2 * py + pc]
            return pltpu.make_async_remote_copy(
                src_ref=dst if src is None else src, dst_ref=dst,
                send_sem=send_sems.at[a, k], recv_sem=recv_sems.at[a, k],
                device_id=to, device_id_type=MESH_ID)

        mine = [pltpu.make_async_copy(ins[a], outs[a].at[4 * x + 2 * y + c], local_sems.at[a]) for a in range(n)]
        for cp in mine:
            cp.start()
        first = []
        for a in range(n):
            first.append(copy(a, 0, me, sib, src=ins[a]))
            for j, chip in enumerate(chips):
                first.append(copy(a, 1 + j, me, (*chip, c), src=ins[a]))
        for cp in first:
            cp.start()
        passed = []
        for j, chip in enumerate(chips):
            for a in range(n):
                copy(a, 1 + j, (*chip, c), me).wait_recv()
                fwd = copy(a, 4 + j, (*chip, c), sib)
                fwd.start()
                passed.append(fwd)
        for a in range(n):
            copy(a, 0, sib, me).wait_recv()
            for j, chip in enumerate(chips):
                copy(a, 4 + j, (*chip, 1 - c), me).wait_recv()
        for cp in first + passed:
            cp.wait_send()
        for cp in mine:
            cp.wait()

    return pl.pallas_call(
        body, name="all_gather_weights",
        in_specs=[ANY] * n, out_specs=[ANY] * n,
        out_shape=[jax.ShapeDtypeStruct((N_DEV,) + s.shape, s.dtype) for s in shards],
        scratch_shapes=[pltpu.SemaphoreType.DMA((n, 7)), pltpu.SemaphoreType.DMA((n, 7)),
                        pltpu.SemaphoreType.DMA((n,))],
    )(*shards)


def _reduce_scatter_cores(grads, name):
    n = len(grads)

    def body(*refs):
        ins, gots = refs[:n], refs[n:2 * n]
        send_sems, recv_sems = refs[2 * n:]
        x, y, c, _ = _mesh_place()
        sib = (x, y, 1 - c)
        remote = []
        for a in range(n):
            for k in range(4):
                remote.append(pltpu.make_async_remote_copy(
                    src_ref=ins[a].at[2 * k + (1 - c)], dst_ref=gots[a].at[k],
                    send_sem=send_sems.at[a, k], recv_sem=recv_sems.at[a, k],
                    device_id=sib, device_id_type=MESH_ID))
        for cp in remote:
            cp.start()
        for cp in remote:
            cp.wait_recv()
        for cp in remote:
            cp.wait_send()

    return pl.pallas_call(
        body, name=name,
        in_specs=[ANY] * n, out_specs=[ANY] * n,
        out_shape=[jax.ShapeDtypeStruct((4,) + g.shape[1:], g.dtype) for g in grads],
        scratch_shapes=[pltpu.SemaphoreType.DMA((n, 4)), pltpu.SemaphoreType.DMA((n, 4))],
    )(*grads)


def _chip_partial_sum(blocks, got, core):
    R, C = got.shape[1:]
    tr = min(256, R)
    assert R % tr == 0

    def body(core_ref, a_ref, b_ref, s_ref, sb_ref):
        s = a_ref[...] + b_ref[...]
        s_ref[...] = s
        sb_ref[...] = s.astype(BF16)

    blk = pl.BlockSpec((None, tr, C), lambda k, i, core_ref: (k, i, 0))
    return pl.pallas_call(
        body, name="chip_partial_sum",
        grid_spec=pltpu.PrefetchScalarGridSpec(
            num_scalar_prefetch=1, grid=(4, R // tr),
            in_specs=[pl.BlockSpec((None, tr, C), lambda k, i, core_ref: (2 * k + core_ref[0], i, 0)), blk],
            out_specs=[blk, blk]),
        out_shape=[jax.ShapeDtypeStruct(got.shape, F32), jax.ShapeDtypeStruct(got.shape, BF16)],
        compiler_params=_cparams("parallel", "parallel"),
    )(core, blocks, got)


HBM_SPEC = pl.BlockSpec(memory_space=pltpu.HBM)
SEM_SPEC = pl.BlockSpec(memory_space=pltpu.SEMAPHORE)
FLIPS = [(dx, dy, dc) for dx in (0, 1) for dy in (0, 1) for dc in (0, 1) if (dx, dy, dc) != (0, 0, 0)]


def _flip(v, d):
    return 1 - v if d else v


def _gather_copies(srcs, lands, send_sems, recv_sems):
    x, y, c, _ = _mesh_place()
    sends, recvs = [], []
    for a in range(len(srcs)):
        for k, (dx, dy, dc) in enumerate(FLIPS):
            px, py, pc = _flip(x, dx), _flip(y, dy), _flip(c, dc)
            sem = len(FLIPS) * a + k
            common = dict(send_sem=send_sems.at[sem], recv_sem=recv_sems.at[sem],
                          device_id=(px, py, pc), device_id_type=MESH_ID)
            sends.append(pltpu.make_async_remote_copy(
                src_ref=srcs[a], dst_ref=lands[a].at[4 * x + 2 * y + c], **common))
            recvs.append(pltpu.make_async_remote_copy(
                src_ref=srcs[a], dst_ref=lands[a].at[4 * px + 2 * py + pc], **common))
    return sends, recvs


def _scatter_copies(srcs, lands, send_sems, recv_sems):
    x, y, c, chips = _mesh_place()
    sends = []
    for a in range(len(srcs)):
        for j, (px, py) in enumerate(chips):
            sends.append(pltpu.make_async_remote_copy(
                src_ref=srcs[a].at[2 * px + py], dst_ref=lands[a].at[j],
                send_sem=send_sems.at[3 * a + j], recv_sem=recv_sems.at[3 * a + j],
                device_id=(px, py, c), device_id_type=MESH_ID))
    return sends, sends


def _exchange_start(srcs, land_shapes, copies, n_copies, name):
    n = len(srcs)

    def body(*refs):
        src_refs, land_refs = refs[:n], refs[n:2 * n]
        send_sems, recv_sems = refs[2 * n], refs[2 * n + 1]
        token = refs[-1]
        sends, _ = copies(src_refs, land_refs, send_sems, recv_sems)
        for cp in sends:
            cp.start()
        token[...] = jnp.zeros_like(token)

    lands = [pltpu.with_memory_space_constraint(lax.empty(s.shape, s.dtype), pltpu.HBM) for s in land_shapes]
    srcs = [pltpu.with_memory_space_constraint(s, pltpu.HBM) for s in srcs]
    res = pl.pallas_call(
        body, name=name,
        out_shape=(pltpu.SemaphoreType.DMA((n * n_copies,)), pltpu.SemaphoreType.DMA((n * n_copies,)),
                   *[pltpu.HBM(s.shape, s.dtype) for s in srcs], *[pltpu.HBM(s.shape, s.dtype) for s in land_shapes],
                   jax.ShapeDtypeStruct((SUBLANES, LANES), F32)),
        in_specs=[HBM_SPEC] * (2 * n),
        out_specs=(SEM_SPEC, SEM_SPEC, *[HBM_SPEC] * (2 * n), pl.BlockSpec(memory_space=pltpu.VMEM)),
        input_output_aliases={i: 2 + i for i in range(2 * n)},
        compiler_params=pltpu.CompilerParams(has_side_effects=pltpu.SideEffectType.DATAFLOW_SIDE_EFFECTING),
    )(*srcs, *lands)
    return res[0], res[1], list(res[2:2 + n]), list(res[2 + n:2 + 2 * n]), res[-1]


def _exchange_wait(started, copies, after, name):
    send_sems, recv_sems, srcs, lands, _ = started
    n = len(srcs)

    def body(*refs):
        src_refs, land_refs = refs[:n], refs[n:2 * n]
        send_ref, recv_ref = refs[2 * n], refs[2 * n + 1]
        sends, recvs = copies(src_refs, land_refs, send_ref, recv_ref)
        for cp in sends:
            cp.wait_send()
        for cp in recvs:
            cp.wait_recv()

    res = pl.pallas_call(
        body, name=name,
        out_shape=tuple(pltpu.HBM(s.shape, s.dtype) for s in srcs + lands),
        in_specs=[HBM_SPEC] * (2 * n) + [SEM_SPEC, SEM_SPEC, ANY],
        out_specs=tuple([HBM_SPEC] * (2 * n)),
        input_output_aliases={i: i for i in range(2 * n)},
        compiler_params=pltpu.CompilerParams(has_side_effects=pltpu.SideEffectType.DATAFLOW_SIDE_EFFECTING),
    )(*srcs, *lands, send_sems, recv_sems, after)
    return list(res[:n]), list(res[n:])


def _all_reduce_small(vec):
    R = vec.shape[0]

    def body(v_ref, o_ref, sib_buf, chip_buf, send_sems, recv_sems):
        x, y, c, chips = _mesh_place()
        swap = pltpu.make_async_remote_copy(
            src_ref=v_ref, dst_ref=sib_buf, send_sem=send_sems.at[0], recv_sem=recv_sems.at[0],
            device_id=(x, y, 1 - c), device_id_type=MESH_ID)
        swap.start()
        swap.wait()
        my_chip = 2 * x + y
        chip_buf[my_chip] = v_ref[...] + sib_buf[...]
        sends = []
        for j, (px, py) in enumerate(chips):
            cp = pltpu.make_async_remote_copy(
                src_ref=chip_buf.at[my_chip], dst_ref=chip_buf.at[my_chip],
                send_sem=send_sems.at[1 + j], recv_sem=recv_sems.at[1 + j],
                device_id=(px, py, c), device_id_type=MESH_ID)
            cp.start()
            sends.append(cp)
        for j, (px, py) in enumerate(chips):
            pltpu.make_async_remote_copy(
                src_ref=chip_buf.at[2 * px + py], dst_ref=chip_buf.at[2 * px + py],
                send_sem=send_sems.at[1 + j], recv_sem=recv_sems.at[1 + j],
                device_id=(px, py, c), device_id_type=MESH_ID).wait_recv()
        for cp in sends:
            cp.wait_send()
        o_ref[...] = ((chip_buf[0] + chip_buf[1]) + chip_buf[2]) + chip_buf[3]

    vm = pl.BlockSpec(memory_space=pltpu.VMEM)
    return pl.pallas_call(
        body, name="all_reduce_small",
        in_specs=[vm], out_specs=vm,
        out_shape=jax.ShapeDtypeStruct(vec.shape, F32),
        scratch_shapes=[pltpu.VMEM((R, LANES), F32), pltpu.VMEM((4, R, LANES), F32),
                        pltpu.SemaphoreType.DMA((4,)), pltpu.SemaphoreType.DMA((4,))],
    )(vec)


def _adamw_math(w, g, m, v):
    m = ADAM_B1 * m + (1.0 - ADAM_B1) * g
    v = ADAM_B2 * v + (1.0 - ADAM_B2) * (g * g)
    m_hat = m / (1.0 - ADAM_B1 ** ADAM_STEP)
    v_hat = v / (1.0 - ADAM_B2 ** ADAM_STEP)
    delta = -ADAM_LR * (m_hat / (jnp.sqrt(v_hat) + ADAM_EPS) + ADAM_WD * w)
    return delta, m, v


def _adamw(w, m, v, g_own, g_got, chip, name):
    R, C = w.shape
    tr = R if R * C <= 256 * D_MODEL else 256
    assert R % tr == 0
    n_got = g_got.shape[0]

    def body(*refs):
        w_ref, m_ref, v_ref, go_ref = refs[1:5]
        got = refs[5:5 + n_got]
        g_ref, d_ref, nm_ref, nv_ref = refs[5 + n_got:]
        g = go_ref[...]
        for r in got:
            g = g + r[...].astype(F32)
        delta, m_new, v_new = _adamw_math(w_ref[...], g, m_ref[...], v_ref[...])
        g_ref[...] = g
        d_ref[...] = delta
        nm_ref[...] = m_new
        nv_ref[...] = v_new

    blk = pl.BlockSpec((tr, C), lambda i, chip_ref: (i, 0))
    own_spec = pl.BlockSpec((None, tr, C), lambda i, chip_ref: (chip_ref[0], i, 0))
    got_specs = [pl.BlockSpec((None, tr, C), lambda i, chip_ref, j=j: (j, i, 0)) for j in range(n_got)]
    return pl.pallas_call(
        body, name=name,
        grid_spec=pltpu.PrefetchScalarGridSpec(
            num_scalar_prefetch=1, grid=(R // tr,),
            in_specs=[blk] * 3 + [own_spec] + got_specs, out_specs=[blk] * 4),
        out_shape=[jax.ShapeDtypeStruct((R, C), F32)] * 4,
        compiler_params=_cparams("parallel"),
    )(chip, w, m, v, g_own, *([g_got] * n_got))


def _block_diag_pairs(wa, wx):
    def pairs(w):
        w = w.reshape(N_GROUPS, 2, LRU_BW, LRU_BW)
        z = jnp.zeros((N_GROUPS, LRU_BW, LRU_BW), w.dtype)
        top = jnp.concatenate([w[:, 0], z], axis=2)
        bot = jnp.concatenate([z, w[:, 1]], axis=2)
        return jnp.concatenate([top, bot], axis=1)
    return jnp.concatenate([pairs(wa), pairs(wx)], axis=2).astype(BF16)


def _block_diag_unpair(dbd):
    def unpair(g):
        blocks = jnp.stack([g[:, :LRU_BW, :LRU_BW], g[:, LRU_BW:, LRU_BW:]], axis=1)
        return blocks.reshape(LRU_BLOCKS, LRU_BW, LRU_BW)
    return unpair(dbd[:, :, :LANES]), unpair(dbd[:, :, LANES:])


def _local_step(x, target, W, small, late_weights=None, early_grads=None, in_grads=None):
    S, D = x.shape
    g1, g2, g3 = small["norm_mix_g"], small["norm_mlp_g"], small["norm_final_g"]
    cw, cb = small["conv_w"], small["conv_b"].reshape(1, D)
    ba, bx, lam = (small[k].reshape(1, D) for k in ("lru_ba", "lru_bx", "lru_lambda"))
    fb = jnp.pad(small["forget_b"], (0, LANES - N_HEADS)).reshape(1, LANES)
    bd = _block_diag_pairs(small["lru_wa"], small["lru_wx"])
    big = dict(tm=1024, tn=1024)

    u = _norm_fwd(x, g1, "norm_mix")
    (xg,) = _mm([(u, W["in_xg"])], tks=[D], outs=[F32], name="proj_xg", **big)
    (qkv_t,) = _mm([(W["in_qkv_t"], u)], tb=True, tks=[D], outs=[BF16], name="proj_qkv_t", **big)
    (kv,) = _mm([(u, W["in_kv"])], tks=[D], outs=[BF16], name="proj_kv", **big)
    (gates,) = _mm([(u, W["in_gates"])], tks=[D], outs=[F32], name="proj_gates", **big)
    (fl,) = _mm([(u, W["in_f"])], tks=[D], outs=[F32], name="proj_forget", **big)
    h, yain = _lru_fwd(xg, cw, cb, bd, ba, bx, lam)
    fcum, f_rep = _forget_cumsum(fl, fb)
    f_row = fcum[:, :N_HEADS].T.reshape(N_HEADS, 1, S)
    ob_t, lse = _attn_fwd(kv, qkv_t, f_row, f_rep)
    if late_weights is not None:
        W = {**W, **late_weights(lse)}
    (ya,) = _mm([(yain, W["branch_a"])], tks=[D], outs=[F32], name="branch_a", **big)
    (yb,) = _mm([(ob_t, W["branch_b"])], ta=True, tks=[D], outs=[F32], name="branch_b", **big)
    mix = _gate_mix(gates, ya, yb)
    (x1,) = _mm([(mix, W["out"])], tks=[D], outs=[F32], name="out_proj", extra=(x,),
                epi=lambda acc, res: (res + acc,), **big)
    m = _norm_fwd(x1, g2, "norm_mlp")
    relu, hh = _mm([(m, W["up"])], tks=[D], outs=[BF16, BF16], name="mlp_up",
                   epi=lambda acc: (jnp.maximum(acc, 0.0), jnp.square(jnp.maximum(acc, 0.0))), **big)
    (x2,) = _mm([(hh, W["down"])], tks=[1024], outs=[F32], name="mlp_down", extra=(x1,),
                epi=lambda acc, res: (res + acc,), **big)
    loss_acc, dg3, dx2, dx2b = _final_norm_loss(x2, target, g3)

    (dhpre,) = _mm([(dx2b, W["down"])], tb=True, tks=[D], outs=[BF16], name="d_mlp_act", extra=(relu,),
                   epi=lambda acc, r: (acc * (2.0 * r.astype(F32)),), **big)
    (dw_down,) = _mm([(hh, dx2b)], ta=True, tks=[min(1024, S)], outs=[F32], name="dw_down", **big)
    (dm,) = _mm([(dhpre, W["up"])], tb=True, tks=[1024], outs=[F32], name="d_mlp_in", **big)
    (dw_up,) = _mm([(m, dhpre)], ta=True, tks=[min(1024, S)], outs=[F32], name="dw_up", tm=1024, tn=D_FF // N_DEV,
                   col_blocked=True)
    dx1, dx1b, dg2 = _norm_bwd(dm, x1, g2, dx2, "norm_mlp_bwd")
    (dmix,) = _mm([(dx1b, W["out"])], tb=True, tks=[D], outs=[F32], name="d_mix", **big)
    (dw_out,) = _mm([(mix, dx1b)], ta=True, tks=[min(1024, S)], outs=[F32], name="dw_out", **big)
    dya, dyb, dgates = _gate_bwd(dmix, gates, ya, yb)
    (dob_t,) = _mm([(W["branch_b"], dyb)], tb=True, tks=[D], outs=[BF16], name="d_attn_out_t", **big)
    (dw_b,) = _mm([(ob_t, dyb)], tks=[min(1024, S)], outs=[F32], name="dw_branch_b", **big)
    (dyain,) = _mm([(dya, W["branch_a"])], tb=True, tks=[D], outs=[F32], name="d_lru_out", **big)
    (dw_a,) = _mm([(yain, dya)], ta=True, tks=[min(1024, S)], outs=[F32], name="dw_branch_a", **big)
    early = dict(w_branch_a=dw_a, w_branch_b=dw_b, w_out=dw_out, w_up=dw_up, w_down=dw_down)
    early_state, started = early_grads(early) if early_grads is not None else (None, 0.0)
    dq_t, dk_t, dv_t, dfk, dfq = _attn_bwd(kv, qkv_t, dob_t, ob_t, lse + started, f_row, f_rep)
    dF = jnp.pad((dfk.reshape(N_HEADS, S) + dfq.reshape(N_HEADS, S)).T, ((0, 0), (0, LANES - N_HEADS)))
    dfl, dfb = _forget_bwd(dF, fl, fb)
    dxg, dcw, dcb, dba, dbx, dlam, dbd = _lru_bwd(xg, h, dyain, cw, cb, bd, ba, bx, lam)
    tks = [min(1024, S)]
    dw_in_parts = [
        _mm([(u, dxg)], ta=True, tks=tks, outs=[F32], name="dw_in_xg", **big)[0],
        _mm([(dq_t, u)], tks=tks, outs=[F32], name="dw_in_q_t", **big)[0].T,
        _mm([(dk_t, u)], tks=tks, outs=[F32], name="dw_in_k_t", **big)[0].T,
        _mm([(dv_t, u)], tks=tks, outs=[F32], name="dw_in_v_t", **big)[0].T,
        _mm([(u, dgates)], ta=True, tks=tks, outs=[F32], name="dw_in_gates", **big)[0],
        _mm([(u, dfl)], ta=True, tks=tks, outs=[F32], name="dw_in_forget", **big)[0][:, :N_HEADS],
    ]
    dw_in = jnp.concatenate(dw_in_parts, axis=1)
    in_state, started = in_grads(dw_in) if in_grads is not None else (None, 0.0)
    wq_t, wk_t, wv_t = (W["in_qkv_t"][D * i:D * (i + 1)] for i in range(3))
    (du,) = _mm([(dxg, W["in_xg"]), (dq_t, wq_t), (dk_t, wk_t), (dv_t, wv_t), (dgates, W["in_gates"]),
                 (dfl, W["in_f"] + jnp.asarray(started, BF16))],
                ta=[False, True, True, True, False, False], tb=[True, False, False, False, True, True],
                tks=[1024, D, D, D, 1024, LANES], outs=[F32], name="d_norm_mix_out", tm=1024, tn=512)
    grad_x, _, dg1 = _norm_bwd(du, x, g1, dx1, "norm_mix_bwd")

    dwa, dwx = _block_diag_unpair(dbd)
    big_grads = dict(early, w_in=dw_in)
    small_grads = dict(norm_mix_g=dg1.reshape(D), conv_w=dcw, conv_b=dcb.reshape(D), lru_wa=dwa, lru_ba=dba.reshape(D),
                       lru_wx=dwx, lru_bx=dbx.reshape(D), lru_lambda=dlam.reshape(D), forget_b=dfb[0, :N_HEADS],
                       norm_mlp_g=dg2.reshape(D), norm_final_g=dg3.reshape(D))
    return loss_acc[0, 0], grad_x, big_grads, small_grads, (early_state, in_state)


SMALL_NAMES = ("norm_mix_g", "conv_b", "lru_wa", "lru_ba", "lru_wx", "lru_bx", "lru_lambda", "forget_b",
               "norm_mlp_g", "norm_final_g")
TILE_ELEMS = SUBLANES * LANES


def _pack_small(parts):
    rows = []
    for p in parts:
        flat = p.reshape(-1)
        flat = jnp.pad(flat, (0, (-flat.shape[0]) % TILE_ELEMS))
        rows.append(flat.reshape(-1, LANES))
    return jnp.concatenate(rows, axis=0)


def _packed_rows(shape):
    return -(-math.prod(shape) // TILE_ELEMS) * SUBLANES


def _adamw_small(g_packed, g_conv_w, weights, moms, vels):
    def rows_view(a):
        flat = a.reshape(-1)
        flat = jnp.pad(flat, (0, (-flat.shape[0]) % LANES))
        return flat.reshape(-1, LANES)

    names = SMALL_NAMES + ("conv_w",)
    views = [[rows_view(src[k]) for k in names] for src in (weights, moms, vels)]
    n = len(names)
    starts, r = [], 0
    for k in SMALL_NAMES:
        starts.append(r)
        r += _packed_rows(weights[k].shape)

    def body(*refs):
        gp_ref, gc_ref = refs[0], refs[1]
        w_refs, m_refs, v_refs = refs[2:2 + n], refs[2 + n:2 + 2 * n], refs[2 + 2 * n:2 + 3 * n]
        outs = refs[2 + 3 * n:]
        for i in range(n):
            rows = w_refs[i].shape[0]
            g = gc_ref[...] if i == n - 1 else gp_ref[starts[i]:starts[i] + rows, :]
            delta, m_new, v_new = _adamw_math(w_refs[i][...], g, m_refs[i][...], v_refs[i][...])
            for o_ref, val in zip(outs[4 * i:4 * i + 4], (g, delta, m_new, v_new)):
                o_ref[...] = val

    vm = pl.BlockSpec(memory_space=pltpu.VMEM)
    out_shape = [jax.ShapeDtypeStruct(v.shape, F32) for v in views[0] for _ in range(4)]
    res = pl.pallas_call(
        body, name="adamw_small",
        in_specs=[vm] * (2 + 3 * n), out_specs=[vm] * (4 * n), out_shape=out_shape,
    )(g_packed, g_conv_w, *views[0], *views[1], *views[2])
    dicts = ({}, {}, {}, {})
    for i, k in enumerate(names):
        size = math.prod(weights[k].shape)
        for d, arr in zip(dicts, res[4 * i:4 * i + 4]):
            d[k] = arr.reshape(-1)[:size].reshape(weights[k].shape)
    return dicts


BIG_NAMES = ("w_in", "w_branch_a", "w_branch_b", "w_out", "w_up", "w_down")
WEIGHT_ORDER = ("norm_mix_g", "w_in", "conv_w", "conv_b", "lru_wa", "lru_ba", "lru_wx", "lru_bx", "lru_lambda",
                "forget_b", "w_branch_a", "w_branch_b", "w_out", "norm_mlp_g", "w_up", "w_down", "norm_final_g")


def _to_dest_blocks(name, g):
    if g.ndim == 3:
        return g
    if name in ("w_in", "w_up"):
        return g.reshape(g.shape[0], N_DEV, g.shape[1] // N_DEV).transpose(1, 0, 2)
    return g.reshape(N_DEV, g.shape[0] // N_DEV, g.shape[1])


def kernel(x, norm_mix_g, w_in, conv_w, conv_b, lru_wa, lru_ba, lru_wx, lru_bx, lru_lambda, forget_b, w_branch_a, w_branch_b, w_out, norm_mlp_g, w_up, w_down, norm_final_g, loss_target, m_norm_mix_g, m_w_in, m_conv_w, m_conv_b, m_lru_wa, m_lru_ba, m_lru_wx, m_lru_bx, m_lru_lambda, m_forget_b, m_w_branch_a, m_w_branch_b, m_w_out, m_norm_mlp_g, m_w_up, m_w_down, m_norm_final_g, v_norm_mix_g, v_w_in, v_conv_w, v_conv_b, v_lru_wa, v_lru_ba, v_lru_wx, v_lru_bx, v_lru_lambda, v_forget_b, v_w_branch_a, v_w_branch_b, v_w_out, v_norm_mlp_g, v_w_up, v_w_down, v_norm_final_g):
    weights = dict(norm_mix_g=norm_mix_g, w_in=w_in, conv_w=conv_w, conv_b=conv_b, lru_wa=lru_wa, lru_ba=lru_ba,
                   lru_wx=lru_wx, lru_bx=lru_bx, lru_lambda=lru_lambda, forget_b=forget_b, w_branch_a=w_branch_a,
                   w_branch_b=w_branch_b, w_out=w_out, norm_mlp_g=norm_mlp_g, w_up=w_up, w_down=w_down,
                   norm_final_g=norm_final_g)
    moms = dict(norm_mix_g=m_norm_mix_g, w_in=m_w_in, conv_w=m_conv_w, conv_b=m_conv_b, lru_wa=m_lru_wa,
                lru_ba=m_lru_ba, lru_wx=m_lru_wx, lru_bx=m_lru_bx, lru_lambda=m_lru_lambda, forget_b=m_forget_b,
                w_branch_a=m_w_branch_a, w_branch_b=m_w_branch_b, w_out=m_w_out, norm_mlp_g=m_norm_mlp_g,
                w_up=m_w_up, w_down=m_w_down, norm_final_g=m_norm_final_g)
    vels = dict(norm_mix_g=v_norm_mix_g, w_in=v_w_in, conv_w=v_conv_w, conv_b=v_conv_b, lru_wa=v_lru_wa,
                lru_ba=v_lru_ba, lru_wx=v_lru_wx, lru_bx=v_lru_bx, lru_lambda=v_lru_lambda, forget_b=v_forget_b,
                w_branch_a=v_w_branch_a, w_branch_b=v_w_branch_b, w_out=v_w_out, norm_mlp_g=v_norm_mlp_g,
                w_up=v_w_up, w_down=v_w_down, norm_final_g=v_norm_final_g)
    S, D = x.shape[1], x.shape[2]
    me = 4 * lax.axis_index("x") + 2 * lax.axis_index("y") + lax.axis_index("c")

    core = lax.axis_index("c").astype(jnp.int32).reshape(1)
    chip = (2 * lax.axis_index("x") + lax.axis_index("y")).astype(jnp.int32).reshape(1)
    late_names = BIG_NAMES[1:]

    win_g, cw_g = _all_gather([w_in.astype(BF16), conv_w])
    late_shards = [weights[k].astype(BF16) for k in late_names]
    gather = _exchange_start(late_shards, [jax.ShapeDtypeStruct((N_DEV,) + s.shape, BF16) for s in late_shards],
                             _gather_copies, len(FLIPS), "gather_late_start")
    w_in_full = win_g.transpose(1, 0, 2).reshape(D, -1)
    cuts = (0, 2 * D, 5 * D, 7 * D)
    W = dict(in_xg=w_in_full[:, cuts[0]:cuts[1]], in_qkv_t=w_in_full[:, cuts[1]:cuts[2]].T,
             in_kv=w_in_full[:, cuts[1] + D:cuts[2]], in_gates=w_in_full[:, cuts[2]:cuts[3]],
             in_f=jnp.pad(w_in_full[:, cuts[3]:], ((0, 0), (0, LANES - N_HEADS))))
    small = {k: weights[k] for k in SMALL_NAMES}
    small["conv_w"] = cw_g.transpose(1, 0, 2).reshape(CONV_W, D)
    small["norm_mix_g"] = norm_mix_g + gather[4][0, 0]

    def late_weights(after):
        shards, lands = _exchange_wait(gather, _gather_copies, after, "gather_late_wait")
        wa_g, wb_g, wo_g, wup_g, wdn_g = (
            lax.dynamic_update_slice_in_dim(land, shard[None], me, axis=0) for land, shard in zip(lands, shards))
        return dict(branch_a=wa_g.reshape(D, D), branch_b=wb_g.reshape(D, D), out=wo_g.reshape(D, D),
                    up=wup_g.transpose(1, 0, 2).reshape(D, D_FF), down=wdn_g.reshape(D_FF, D))

    def core_stage(names, grads_by_name, tag):
        blocks = [_to_dest_blocks(k, grads_by_name[k]) for k in names]
        got = _reduce_scatter_cores(blocks, "reduce_scatter_cores_" + tag)
        return [_chip_partial_sum(b, g, core) for b, g in zip(blocks, got)]

    def reduce_behind(names, grads_by_name, tag):
        sums = core_stage(names, grads_by_name, tag)
        wire = [s[1] for s in sums]
        scatter = _exchange_start(wire, [jax.ShapeDtypeStruct((3,) + s.shape[1:], BF16) for s in wire],
                                  _scatter_copies, 3, "scatter_" + tag + "_start")
        return (sums, scatter), scatter[4][0, 0]

    loss_part, grad_x, _, small_grads, ((early_sums, early_scatter), (in_sums, in_scatter)) = _local_step(
        x.reshape(S, D), loss_target.reshape(S, D), W, small, late_weights,
        lambda g: reduce_behind(late_names, g, "early"), lambda g: reduce_behind(BIG_NAMES[:1], dict(w_in=g), "w_in"))
    loss = lax.psum(loss_part, MESH_AXES)
    _, early_others = _exchange_wait(early_scatter, _scatter_copies, grad_x, "scatter_early_wait")
    _, in_others = _exchange_wait(in_scatter, _scatter_copies, grad_x, "scatter_w_in_wait")
    sums = list(in_sums) + list(early_sums)
    others = list(in_others) + list(early_others)

    reduced = _all_reduce_small(_pack_small([small_grads[k] for k in SMALL_NAMES] + [small_grads["conv_w"]]))
    cw_full = reduced[reduced.shape[0] - _packed_rows((CONV_W, D)):].reshape(CONV_W, D)
    cw_cols = lax.dynamic_slice_in_dim(cw_full, me * (D // N_DEV), D // N_DEV, axis=1)

    grads, deltas, new_m, new_v = _adamw_small(reduced, cw_cols, weights, moms, vels)
    for k, s, g_got in zip(BIG_NAMES, sums, others):
        grads[k], deltas[k], new_m[k], new_v[k] = _adamw(weights[k], moms[k], vels[k], s[0], g_got, chip, "adamw_" + k)

    return (loss, grad_x.reshape(1, S, D), *[grads[k] for k in WEIGHT_ORDER], *[deltas[k] for k in WEIGHT_ORDER],
            *[new_m[k] for k in WEIGHT_ORDER], *[new_v[k] for k in WEIGHT_ORDER])
```

```python
import functools
import math

import jax
import jax.numpy as jnp
from jax import lax
from jax.experimental import pallas as pl
from jax.experimental.pallas import tpu as pltpu

F32 = jnp.float32
BF16 = jnp.bfloat16

D_MODEL = 1024
N_HEADS = 8
HEAD_DIM = 128
D_FF = 4096
LRU_BLOCKS = 16
LRU_BW = 64
LRU_C = 8.0
CONV_W = 4
RMS_EPS = 1e-6
N_DEV = 8
LANES = 128
SUBLANES = 8
N_GROUPS = D_MODEL // LANES
VMEM_LIMIT_BYTES = 52 * 1024 * 1024
ATTN_SCALE = 1.0 / math.sqrt(HEAD_DIM)
LOG2E = math.log2(math.e)
NEG_BIG = -1e30
ADAM_LR = 0.001
ADAM_B1 = 0.9
ADAM_B2 = 0.999
ADAM_EPS = 1e-08
ADAM_WD = 0.01
ADAM_STEP = 10
ATTN_BLOCK = 1024
ATTN_STRIP = 256
LRU_CHUNK = 256
ROW_TILE = 512
MESH_AXES = ("x", "y", "c")
MESH_ID = pl.DeviceIdType.MESH
ANY = pl.BlockSpec(memory_space=pl.ANY)

NT_DIMS = (((1,), (1,)), ((), ()))
TN_DIMS = (((0,), (0,)), ((), ()))
NN_DIMS = (((1,), (0,)), ((), ()))


def _cparams(*sem):
    return pltpu.CompilerParams(dimension_semantics=sem if sem else None, vmem_limit_bytes=VMEM_LIMIT_BYTES)


def _sigmoid(x):
    return 0.5 * (jnp.tanh(0.5 * x) + 1.0)


def _log1p_pos(e):
    u = 1.0 + e
    return jnp.where(u == 1.0, e, jnp.log(u) * (e / (u - 1.0)))


def _softplus(z):
    return jnp.maximum(z, 0.0) + _log1p_pos(jnp.exp(-jnp.abs(z)))


def _expm1_neg(x):
    series = x * (1.0 + x * 0.5 * (1.0 + x * (1.0 / 3.0) * (1.0 + x * 0.25)))
    return jnp.where(x > -0.03, series, jnp.exp(x) - 1.0)


GELU_C = math.sqrt(2.0 / math.pi)
GELU_K = 0.044715


def _gelu(x):
    return 0.5 * x * (1.0 + jnp.tanh(GELU_C * (x + GELU_K * (x * x * x))))


def _gelu_and_grad(x):
    t = jnp.tanh(GELU_C * (x + GELU_K * (x * x * x)))
    g = 0.5 * x * (1.0 + t)
    dg = 0.5 * (1.0 + t) + 0.5 * x * (1.0 - t * t) * (GELU_C * (1.0 + 3.0 * GELU_K * (x * x)))
    return g, dg


def _mm(pairs, *, ta=False, tb=False, tm, tn, tks, outs, name, epi=None, extra=(), col_blocked=False):
    n_pairs, n_extra, n_out = len(pairs), len(extra), len(outs)
    tas = list(ta) if isinstance(ta, (list, tuple)) else [ta] * n_pairs
    tbs = list(tb) if isinstance(tb, (list, tuple)) else [tb] * n_pairs
    a0, b0 = pairs[0]
    M = a0.shape[1] if tas[0] else a0.shape[0]
    N = b0.shape[0] if tbs[0] else b0.shape[1]
    tm, tn = min(tm, M), min(tn, N)
    nks, offs = [], []
    for (a, b), tk, pta in zip(pairs, tks, tas):
        K = a.shape[0] if pta else a.shape[1]
        assert K % tk == 0 and M % tm == 0 and N % tn == 0
        offs.append(sum(nks))
        nks.append(K // tk)
    nk_total = sum(nks)
    dims = [(((0 if pta else 1,), (1 if ptb else 0,)), ((), ())) for pta, ptb in zip(tas, tbs)]

    def kmap(off, nk):
        return lambda k: jnp.clip(k - off, 0, nk - 1)

    in_specs, operands = [], []
    for (a, b), tk, off, nk, pta, ptb in zip(pairs, tks, offs, nks, tas, tbs):
        km = kmap(off, nk)
        if pta:
            in_specs.append(pl.BlockSpec((tk, tm), lambda i, j, k, km=km: (km(k), i)))
        else:
            in_specs.append(pl.BlockSpec((tm, tk), lambda i, j, k, km=km: (i, km(k))))
        if ptb:
            in_specs.append(pl.BlockSpec((tn, tk), lambda i, j, k, km=km: (j, km(k))))
        else:
            in_specs.append(pl.BlockSpec((tk, tn), lambda i, j, k, km=km: (km(k), j)))
        operands += [a, b]
    for e in extra:
        in_specs.append(pl.BlockSpec((tm, tn), lambda i, j, k: (i, j)))
        operands.append(e)

    def body(*refs):
        ab = refs[:2 * n_pairs]
        ex = refs[2 * n_pairs:2 * n_pairs + n_extra]
        o = refs[2 * n_pairs + n_extra:2 * n_pairs + n_extra + n_out]
        k = pl.program_id(2)

        def finish(acc):
            res = epi(acc, *[e[...] for e in ex]) if epi is not None else (acc,)
            for r, oref in zip(res, o):
                oref[...] = r.astype(oref.dtype)

        if nk_total == 1:
            finish(lax.dot_general(ab[0][...], ab[1][...], dims[0], preferred_element_type=F32))
            return
        acc = refs[-1]
        for p in range(n_pairs):
            a_ref, b_ref = ab[2 * p], ab[2 * p + 1]

            @pl.when((k >= offs[p]) & (k < offs[p] + nks[p]))
            def _(a_ref=a_ref, b_ref=b_ref, pdims=dims[p]):
                prod = lax.dot_general(a_ref[...], b_ref[...], pdims, preferred_element_type=F32)

                @pl.when(k == 0)
                def _():
                    acc[...] = prod

                @pl.when(k > 0)
                def _():
                    acc[...] += prod

        @pl.when(k == nk_total - 1)
        def _():
            finish(acc[...])

    return pl.pallas_call(
        body,
        name=name,
        grid=(M // tm, N // tn, nk_total),
        in_specs=in_specs,
        out_specs=[pl.BlockSpec((None, tm, tn), lambda i, j, k: (j, i, 0)) if col_blocked
                   else pl.BlockSpec((tm, tn), lambda i, j, k: (i, j)) for _ in outs],
        out_shape=[jax.ShapeDtypeStruct((N // tn, M, tn) if col_blocked else (M, N), dt) for dt in outs],
        scratch_shapes=[] if nk_total == 1 else [pltpu.VMEM((tm, tn), F32)],
        compiler_params=_cparams("parallel", "parallel", "arbitrary"),
    )(*operands)


def _norm_fwd(x, g, name):
    S, D = x.shape
    tr = min(ROW_TILE, S)

    def body(x_ref, g_ref, o_ref):
        xv = x_ref[...]
        r = lax.rsqrt(jnp.mean(xv * xv, axis=-1, keepdims=True) + RMS_EPS)
        o_ref[...] = ((xv * r) * g_ref[...]).astype(o_ref.dtype)

    return pl.pallas_call(
        body, name=name, grid=(S // tr,),
        in_specs=[pl.BlockSpec((tr, D), lambda i: (i, 0)), pl.BlockSpec((1, D), lambda i: (0, 0))],
        out_specs=pl.BlockSpec((tr, D), lambda i: (i, 0)),
        out_shape=jax.ShapeDtypeStruct((S, D), BF16),
        compiler_params=_cparams("parallel"),
    )(x, g.reshape(1, D))


def _rms_bwd_rows(dy, xv, g):
    r = lax.rsqrt(jnp.mean(xv * xv, axis=-1, keepdims=True) + RMS_EPS)
    xn = xv * r
    dxn = dy * g
    dx = r * (dxn - xn * jnp.mean(dxn * xn, axis=-1, keepdims=True))
    dg = jnp.sum(dy * xn, axis=0, keepdims=True)
    return dx, dg


def _norm_bwd(dy, x, g, dres, name):
    S, D = x.shape
    tr = min(ROW_TILE, S)

    def body(dy_ref, x_ref, g_ref, dres_ref, dx_ref, dxb_ref, dg_ref):
        dx, dg = _rms_bwd_rows(dy_ref[...], x_ref[...], g_ref[...])
        dx = dres_ref[...] + dx
        dx_ref[...] = dx
        dxb_ref[...] = dx.astype(BF16)

        @pl.when(pl.program_id(0) == 0)
        def _():
            dg_ref[...] = jnp.zeros_like(dg_ref)

        dg_ref[...] += dg

    row = pl.BlockSpec((tr, D), lambda i: (i, 0))
    vec = pl.BlockSpec((1, D), lambda i: (0, 0))
    return pl.pallas_call(
        body, name=name, grid=(S // tr,),
        in_specs=[row, row, vec, row],
        out_specs=[row, row, vec],
        out_shape=[jax.ShapeDtypeStruct((S, D), F32), jax.ShapeDtypeStruct((S, D), BF16),
                   jax.ShapeDtypeStruct((1, D), F32)],
        compiler_params=_cparams("arbitrary"),
    )(dy, x, g.reshape(1, D), dres)


def _final_norm_loss(x2, target, g):
    S, D = x2.shape
    tr = min(ROW_TILE, S)

    def body(x_ref, t_ref, g_ref, loss_ref, dg_ref, dx_ref, dxb_ref):
        xv = x_ref[...]
        gv = g_ref[...]
        r = lax.rsqrt(jnp.mean(xv * xv, axis=-1, keepdims=True) + RMS_EPS)
        y = (xv * r) * gv
        err = y - t_ref[...]
        part = 0.5 * jnp.sum(jnp.mean(err * err, axis=-1, keepdims=True), axis=0, keepdims=True)
        dy = err * (1.0 / D)
        dx, dg = _rms_bwd_rows(dy, xv, gv)
        dx_ref[...] = dx
        dxb_ref[...] = dx.astype(BF16)

        @pl.when(pl.program_id(0) == 0)
        def _():
            dg_ref[...] = jnp.zeros_like(dg_ref)
            loss_ref[...] = jnp.zeros_like(loss_ref)

        dg_ref[...] += dg
        loss_ref[...] += jnp.broadcast_to(part, loss_ref.shape)

    row = pl.BlockSpec((tr, D), lambda i: (i, 0))
    vec = pl.BlockSpec((1, D), lambda i: (0, 0))
    return pl.pallas_call(
        body, name="final_norm_loss", grid=(S // tr,),
        in_specs=[row, row, vec],
        out_specs=[pl.BlockSpec((SUBLANES, LANES), lambda i: (0, 0)), vec, row, row],
        out_shape=[jax.ShapeDtypeStruct((SUBLANES, LANES), F32), jax.ShapeDtypeStruct((1, D), F32),
                   jax.ShapeDtypeStruct((S, D), F32), jax.ShapeDtypeStruct((S, D), BF16)],
        compiler_params=_cparams("arbitrary"),
    )(x2, target, g.reshape(1, D))


def _lru_gates(xa, bd_j, ba_j, bx_j, sp_j):
    z = jnp.dot(xa.astype(BF16), bd_j, preferred_element_type=F32)
    r = _sigmoid(z[:, :LANES] + ba_j)
    ig = _sigmoid(z[:, LANES:] + bx_j)
    log_a = (-LRU_C) * r * sp_j
    a = jnp.exp(log_a)
    mult = jnp.sqrt(-_expm1_neg(2.0 * log_a))
    return r, ig, a, mult


def _conv_rows(xpad, cw_ref, cb_ref, sl, tc):
    out = jnp.broadcast_to(cb_ref[:, sl], (tc, LANES))
    for k in range(CONV_W):
        out = out + xpad[pl.ds(SUBLANES - (CONV_W - 1) + k, tc), sl] * cw_ref[k:k + 1, sl]
    return out


def _lru_fwd(xg, cw, cb, bd, ba, bx, lam):
    S = xg.shape[0]
    D = D_MODEL
    tc = min(LRU_CHUNK, S)
    hb = tc // SUBLANES

    def body(xl_ref, halo_ref, g_ref, cw_ref, cb_ref, bd_ref, ba_ref, bx_ref, lam_ref,
             h_ref, y_ref, xpad, a_s, b_s, carry):
        i = pl.program_id(0)

        @pl.when(i == 0)
        def _():
            carry[...] = jnp.zeros_like(carry)

        xpad[0:SUBLANES, :] = jnp.where(i > 0, halo_ref[...], 0.0)
        xpad[SUBLANES:, :] = xl_ref[...]
        for j in range(N_GROUPS):
            sl = slice(LANES * j, LANES * (j + 1))
            xa = _conv_rows(xpad, cw_ref, cb_ref, sl, tc)
            sp = _softplus(-lam_ref[:, sl])
            _, ig, a, mult = _lru_gates(xa, bd_ref[j], ba_ref[:, sl], bx_ref[:, sl], sp)
            a_s[:, sl] = a
            b_s[:, sl] = mult * (ig * xa)

        row = lax.broadcasted_iota(jnp.int32, (SUBLANES, D), 0)

        def step(t, c):
            o = pl.multiple_of(t * SUBLANES, SUBLANES)
            A = a_s[pl.ds(o, SUBLANES), :]
            B = b_s[pl.ds(o, SUBLANES), :]
            for d in (1, 2, 4):
                keep = row >= d
                a_sh = jnp.where(keep, pltpu.roll(A, d, 0), 1.0)
                b_sh = jnp.where(keep, pltpu.roll(B, d, 0), 0.0)
                B = A * b_sh + B
                A = A * a_sh
            hh = A * c + B
            h_ref[pl.ds(o, SUBLANES), :] = hh
            return jnp.broadcast_to(hh[SUBLANES - 1:SUBLANES, :], (SUBLANES, D))

        carry[...] = lax.fori_loop(0, hb, step, carry[...])
        y_ref[...] = (_gelu(g_ref[...]) * h_ref[...]).astype(BF16)

    row_spec = lambda col: pl.BlockSpec((tc, D), lambda i, col=col: (i, col))
    halo = pl.BlockSpec((SUBLANES, D), lambda i: (jnp.maximum(i * hb - 1, 0), 0))
    full = lambda shape: pl.BlockSpec(shape, lambda i: tuple(0 for _ in shape))
    return pl.pallas_call(
        body, name="lru_fwd", grid=(S // tc,),
        in_specs=[row_spec(0), halo, row_spec(1), full((CONV_W, D)), full((1, D)),
                  full((N_GROUPS, LANES, 2 * LANES)), full((1, D)), full((1, D)), full((1, D))],
        out_specs=[pl.BlockSpec((tc, D), lambda i: (i, 0)), pl.BlockSpec((tc, D), lambda i: (i, 0))],
        out_shape=[jax.ShapeDtypeStruct((S, D), F32), jax.ShapeDtypeStruct((S, D), BF16)],
        scratch_shapes=[pltpu.VMEM((tc + SUBLANES, D), F32), pltpu.VMEM((tc, D), F32),
                        pltpu.VMEM((tc, D), F32), pltpu.VMEM((SUBLANES, D), F32)],
        compiler_params=_cparams("arbitrary"),
    )(xg, xg, xg, cw, cb, bd, ba, bx, lam)


def _lru_bwd(xg, h, dyain, cw, cb, bd, ba, bx, lam):
    S = xg.shape[0]
    D = D_MODEL
    tc = min(LRU_CHUNK, S)
    hb = tc // SUBLANES
    nc = S // tc

    def body(xl_ref, xhalo_ref, g_ref, h_ref, hhalo_ref, dy_ref, cw_ref, cb_ref, bd_ref, ba_ref, bx_ref,
             lam_ref, dxg_ref, dcw_ref, dcb_ref, dba_ref, dbx_ref, dlam_ref, dbd_ref,
             xpad, hpad, a_s, b_s, dh_s, g_s, xa_s, r_s, ig_s, m_s, dxa_pad, carry_e, dxa_head):
        i = pl.program_id(0)
        c = nc - 1 - i

        @pl.when(i == 0)
        def _():
            carry_e[...] = jnp.zeros_like(carry_e)
            dxa_head[...] = jnp.zeros_like(dxa_head)
            for ref in (dcw_ref, dcb_ref, dba_ref, dbx_ref, dlam_ref, dbd_ref):
                ref[...] = jnp.zeros_like(ref)

        xpad[0:SUBLANES, :] = jnp.where(c > 0, xhalo_ref[...], 0.0)
        xpad[SUBLANES:, :] = xl_ref[...]
        hpad[0:SUBLANES, :] = jnp.where(c > 0, hhalo_ref[...], 0.0)
        hpad[SUBLANES:, :] = h_ref[...]

        for j in range(N_GROUPS):
            sl = slice(LANES * j, LANES * (j + 1))
            xa = _conv_rows(xpad, cw_ref, cb_ref, sl, tc)
            sp = _softplus(-lam_ref[:, sl])
            r, ig, a, mult = _lru_gates(xa, bd_ref[j], ba_ref[:, sl], bx_ref[:, sl], sp)
            gl, dgl = _gelu_and_grad(g_ref[:, sl])
            dy = dy_ref[:, sl]
            dh = dy * gl
            dxg_ref[:, D + LANES * j:D + LANES * (j + 1)] = (dy * h_ref[:, sl] * dgl).astype(BF16)
            a_s[:, sl] = a
            b_s[:, sl] = a * dh
            dh_s[:, sl] = dh
            xa_s[:, sl] = xa
            r_s[:, sl] = r
            ig_s[:, sl] = ig
            m_s[:, sl] = mult

        row = lax.broadcasted_iota(jnp.int32, (SUBLANES, D), 0)

        def step(tt, ce):
            o = pl.multiple_of((hb - 1 - tt) * SUBLANES, SUBLANES)
            A = a_s[pl.ds(o, SUBLANES), :]
            B = b_s[pl.ds(o, SUBLANES), :]
            for d in (1, 2, 4):
                keep = row < SUBLANES - d
                a_sh = jnp.where(keep, pltpu.roll(A, SUBLANES - d, 0), 1.0)
                b_sh = jnp.where(keep, pltpu.roll(B, SUBLANES - d, 0), 0.0)
                B = A * b_sh + B
                A = A * a_sh
            e = A * ce + B
            e_next = jnp.where(row < SUBLANES - 1, pltpu.roll(e, SUBLANES - 1, 0), ce)
            g_s[pl.ds(o, SUBLANES), :] = dh_s[pl.ds(o, SUBLANES), :] + e_next
            return jnp.broadcast_to(e[0:1, :], (SUBLANES, D))

        carry_e[...] = lax.fori_loop(0, hb, step, carry_e[...])

        for j in range(N_GROUPS):
            sl = slice(LANES * j, LANES * (j + 1))
            gg = g_s[:, sl]
            xa, r, ig, mult, a = xa_s[:, sl], r_s[:, sl], ig_s[:, sl], m_s[:, sl], a_s[:, sl]
            hprev = hpad[pl.ds(SUBLANES - 1, tc), sl]
            sp = _softplus(-lam_ref[:, sl])
            da = gg * hprev
            dmult = gg * (ig * xa)
            dig = gg * (mult * xa)
            dxa = gg * (mult * ig)
            dla = da * a - dmult * ((a * a) / mult)
            dr = dla * ((-LRU_C) * sp)
            dlam_ref[:, sl] += jnp.sum(dla * r, axis=0, keepdims=True)
            dza = dr * r * (1.0 - r)
            dzx = dig * ig * (1.0 - ig)
            dba_ref[:, sl] += jnp.sum(dza, axis=0, keepdims=True)
            dbx_ref[:, sl] += jnp.sum(dzx, axis=0, keepdims=True)
            dz = jnp.concatenate([dza, dzx], axis=1).astype(BF16)
            dbd_ref[j] += lax.dot_general(xa.astype(BF16), dz, TN_DIMS, preferred_element_type=F32)
            dxa = dxa + lax.dot_general(dz, bd_ref[j], NT_DIMS, preferred_element_type=F32)
            dxa_pad[0:tc, sl] = dxa

        dxa_pad[tc:, :] = dxa_head[...]
        dxa_head[...] = dxa_pad[0:SUBLANES, :]

        for j in range(N_GROUPS):
            sl = slice(LANES * j, LANES * (j + 1))
            dxa = dxa_pad[0:tc, sl]
            dxl = jnp.zeros((tc, LANES), F32)
            for k in range(CONV_W):
                dxl = dxl + dxa_pad[pl.ds(CONV_W - 1 - k, tc), sl] * cw_ref[k:k + 1, sl]
                dcw_ref[k:k + 1, sl] += jnp.sum(
                    dxa * xpad[pl.ds(SUBLANES - (CONV_W - 1) + k, tc), sl], axis=0, keepdims=True)
            dxg_ref[:, sl] = dxl.astype(BF16)
            dcb_ref[:, sl] += jnp.sum(dxa, axis=0, keepdims=True)

        @pl.when(i == nc - 1)
        def _():
            dlam_ref[...] = dlam_ref[...] * (LRU_C * _sigmoid(-lam_ref[...]))

    rev = lambda col: pl.BlockSpec((tc, D), lambda i, col=col: (nc - 1 - i, col))
    halo = pl.BlockSpec((SUBLANES, D), lambda i: (jnp.maximum((nc - 1 - i) * hb - 1, 0), 0))
    full = lambda shape: pl.BlockSpec(shape, lambda i: tuple(0 for _ in shape))
    big = lambda: pltpu.VMEM((tc, D), F32)
    return pl.pallas_call(
        body, name="lru_bwd", grid=(nc,),
        in_specs=[rev(0), halo, rev(1), rev(0), halo, rev(0), full((CONV_W, D)), full((1, D)),
                  full((N_GROUPS, LANES, 2 * LANES)), full((1, D)), full((1, D)), full((1, D))],
        out_specs=[pl.BlockSpec((tc, 2 * D), lambda i: (nc - 1 - i, 0)), full((CONV_W, D)), full((1, D)),
                   full((1, D)), full((1, D)), full((1, D)), full((N_GROUPS, LANES, 2 * LANES))],
        out_shape=[jax.ShapeDtypeStruct((S, 2 * D), BF16), jax.ShapeDtypeStruct((CONV_W, D), F32),
                   jax.ShapeDtypeStruct((1, D), F32), jax.ShapeDtypeStruct((1, D), F32),
                   jax.ShapeDtypeStruct((1, D), F32), jax.ShapeDtypeStruct((1, D), F32),
                   jax.ShapeDtypeStruct((N_GROUPS, LANES, 2 * LANES), F32)],
        scratch_shapes=[pltpu.VMEM((tc + SUBLANES, D), F32), pltpu.VMEM((tc + SUBLANES, D), F32),
                        big(), big(), big(), big(), big(), big(), big(), big(),
                        pltpu.VMEM((tc + SUBLANES, D), F32), pltpu.VMEM((SUBLANES, D), F32),
                        pltpu.VMEM((SUBLANES, D), F32)],
        compiler_params=_cparams("arbitrary"),
    )(xg, xg, xg, h, h, dyain, cw, cb, bd, ba, bx, lam)


def _forget_cumsum(fl, fb):
    S = fl.shape[0]
    tr = min(ROW_TILE, S)
    hb = tr // SUBLANES

    def body(fl_ref, fb_ref, o_ref, rep_ref, lf_s, carry):
        @pl.when(pl.program_id(0) == 0)
        def _():
            carry[...] = jnp.zeros_like(carry)

        lf_s[...] = -_softplus(-(fl_ref[...] + fb_ref[...]))
        row = lax.broadcasted_iota(jnp.int32, (SUBLANES, LANES), 0)

        def step(t, c):
            o = pl.multiple_of(t * SUBLANES, SUBLANES)
            B = lf_s[pl.ds(o, SUBLANES), :]
            for d in (1, 2, 4):
                B = B + jnp.where(row >= d, pltpu.roll(B, d, 0), 0.0)
            B = B + c
            o_ref[pl.ds(o, SUBLANES), :] = B * LOG2E
            return jnp.broadcast_to(B[SUBLANES - 1:SUBLANES, :], (SUBLANES, LANES))

        carry[...] = lax.fori_loop(0, hb, step, carry[...])
        for h in range(N_HEADS):
            rep_ref[h] = jnp.broadcast_to(o_ref[:, h:h + 1], (tr, LANES))

    return pl.pallas_call(
        body, name="forget_cumsum", grid=(S // tr,),
        in_specs=[pl.BlockSpec((tr, LANES), lambda i: (i, 0)), pl.BlockSpec((1, LANES), lambda i: (0, 0))],
        out_specs=[pl.BlockSpec((tr, LANES), lambda i: (i, 0)),
                   pl.BlockSpec((N_HEADS, tr, LANES), lambda i: (0, i, 0))],
        out_shape=[jax.ShapeDtypeStruct((S, LANES), F32), jax.ShapeDtypeStruct((N_HEADS, S, LANES), F32)],
        scratch_shapes=[pltpu.VMEM((tr, LANES), F32), pltpu.VMEM((SUBLANES, LANES), F32)],
        compiler_params=_cparams("arbitrary"),
    )(fl, fb)


def _forget_bwd(dF, fl, fb):
    S = fl.shape[0]
    tr = min(ROW_TILE, S)
    hb = tr // SUBLANES
    nc = S // tr

    def body(df_ref, fl_ref, fb_ref, o_ref, dfb_ref, carry):
        @pl.when(pl.program_id(0) == 0)
        def _():
            carry[...] = jnp.zeros_like(carry)
            dfb_ref[...] = jnp.zeros_like(dfb_ref)

        row = lax.broadcasted_iota(jnp.int32, (SUBLANES, LANES), 0)

        def step(tt, carried):
            c, acc = carried
            o = pl.multiple_of((hb - 1 - tt) * SUBLANES, SUBLANES)
            B = df_ref[pl.ds(o, SUBLANES), :]
            for d in (1, 2, 4):
                B = B + jnp.where(row < SUBLANES - d, pltpu.roll(B, SUBLANES - d, 0), 0.0)
            B = B + c
            z = fl_ref[pl.ds(o, SUBLANES), :] + fb_ref[...]
            dz = B * _sigmoid(-z)
            o_ref[pl.ds(o, SUBLANES), :] = dz.astype(BF16)
            return jnp.broadcast_to(B[0:1, :], (SUBLANES, LANES)), acc + dz

        c, acc = lax.fori_loop(0, hb, step, (carry[...], jnp.zeros((SUBLANES, LANES), F32)))
        carry[...] = c
        dfb_ref[...] += jnp.sum(acc, axis=0, keepdims=True)

    rev = pl.BlockSpec((tr, LANES), lambda i: (nc - 1 - i, 0))
    vec = pl.BlockSpec((1, LANES), lambda i: (0, 0))
    return pl.pallas_call(
        body, name="forget_bwd", grid=(nc,),
        in_specs=[rev, rev, vec],
        out_specs=[rev, vec],
        out_shape=[jax.ShapeDtypeStruct((S, LANES), BF16), jax.ShapeDtypeStruct((1, LANES), F32)],
        scratch_shapes=[pltpu.VMEM((SUBLANES, LANES), F32)],
        compiler_params=_cparams("arbitrary"),
    )(dF, fl, fb)


def _triangle(n, key_major):
    pairs = [(q, k) for q in range(n) for k in range(q + 1)]
    if key_major:
        pairs.sort(key=lambda qk: (qk[1], qk[0]))
    return (jnp.asarray([q for q, _ in pairs], jnp.int32), jnp.asarray([k for _, k in pairs], jnp.int32))


def _strip_scores(k_ref, qt_ref, fk_ref, j, strip, nkeys, diagonal):
    cols = slice(strip * j, strip * (j + 1))
    s = jnp.dot(k_ref[0:nkeys, :], qt_ref[:, cols], preferred_element_type=F32) * (ATTN_SCALE * LOG2E)
    fk = fk_ref[0:nkeys, :]
    s = s - jnp.concatenate([fk] * (strip // LANES), axis=1)
    keep = None
    if diagonal:
        keys = lax.broadcasted_iota(jnp.int32, (nkeys, strip), 0)
        queries = lax.broadcasted_iota(jnp.int32, (nkeys, strip), 1) + strip * j
        keep = keys <= queries
    return s, keep


def _attn_fwd(kv, qkv_t, f_row, f_rep):
    S = kv.shape[0]
    blk = min(ATTN_BLOCK, S)
    strip = min(ATTN_STRIP, blk)
    n = S // blk
    tri_q, tri_k = _triangle(n, key_major=False)
    ones_rows = 2 * SUBLANES

    def body(tq_ref, tk_ref, k_ref, qt_ref, vt_ref, fq_ref, fk_ref, ot_ref, lse_ref, m_s, acc_s, vta_s):
        t = pl.program_id(1)
        qi, ki = tq_ref[t], tk_ref[t]

        @pl.when(ki == 0)
        def _():
            m_s[...] = jnp.full_like(m_s, NEG_BIG)
            acc_s[...] = jnp.zeros_like(acc_s)

        vta_s[0:HEAD_DIM, :] = vt_ref[...]
        vta_s[HEAD_DIM:, :] = jnp.ones((ones_rows, blk), BF16)

        def update(diagonal):
            n_strips = blk // strip
            keys_of = lambda j: strip * (j + 1) if diagonal else blk
            scores = lambda j: _strip_scores(k_ref, qt_ref, fk_ref, j, strip, keys_of(j), diagonal)
            def weighted_values(j, alpha, pb):
                cols = slice(strip * j, strip * (j + 1))
                acc_s[:, cols] = alpha * acc_s[:, cols] + jnp.dot(
                    vta_s[:, 0:keys_of(j)], pb, preferred_element_type=F32)

            ahead, behind = scores(0), None
            for j in range(n_strips):
                cols = slice(strip * j, strip * (j + 1))
                (s, keep), ahead = ahead, (scores(j + 1) if j + 1 < n_strips else None)
                if behind is not None:
                    weighted_values(*behind)
                if diagonal:
                    s = jnp.where(keep, s, NEG_BIG)
                fq = fq_ref[:, cols]
                m_old = m_s[:, cols]
                m_new = jnp.maximum(m_old, jnp.max(s, axis=0, keepdims=True) + fq)
                p = jnp.exp2(s - (m_new - fq))
                behind = (j, jnp.exp2(m_old - m_new), p.astype(BF16))
                m_s[:, cols] = m_new
            weighted_values(*behind)

        @pl.when(ki < qi)
        def _():
            update(False)

        @pl.when(ki == qi)
        def _():
            update(True)
            denom = acc_s[HEAD_DIM:HEAD_DIM + 1, :]
            ot_ref[...] = (acc_s[0:HEAD_DIM, :] / denom).astype(BF16)
            lse_ref[...] = m_s[...] + jnp.log2(denom)

    return pl.pallas_call(
        body, name="attn_fwd",
        grid_spec=pltpu.PrefetchScalarGridSpec(
            num_scalar_prefetch=2, grid=(N_HEADS, tri_q.shape[0]),
            in_specs=[pl.BlockSpec((blk, HEAD_DIM), lambda h, t, tq, tk: (tk[t], h)),
                      pl.BlockSpec((HEAD_DIM, blk), lambda h, t, tq, tk: (h, tq[t])),
                      pl.BlockSpec((HEAD_DIM, blk), lambda h, t, tq, tk: (2 * N_HEADS + h, tk[t])),
                      pl.BlockSpec((None, 1, blk), lambda h, t, tq, tk: (h, 0, tq[t])),
                      pl.BlockSpec((None, blk, LANES), lambda h, t, tq, tk: (h, tk[t], 0))],
            out_specs=[pl.BlockSpec((HEAD_DIM, blk), lambda h, t, tq, tk: (h, tq[t])),
                       pl.BlockSpec((None, 1, blk), lambda h, t, tq, tk: (h, 0, tq[t]))],
            scratch_shapes=[pltpu.VMEM((1, blk), F32), pltpu.VMEM((HEAD_DIM + ones_rows, blk), F32),
                            pltpu.VMEM((HEAD_DIM + ones_rows, blk), BF16)]),
        out_shape=[jax.ShapeDtypeStruct((N_HEADS * HEAD_DIM, S), BF16), jax.ShapeDtypeStruct((N_HEADS, 1, S), F32)],
        compiler_params=_cparams("parallel", "arbitrary"),
    )(tri_q, tri_k, kv, qkv_t, qkv_t, f_row, f_rep)


def _attn_bwd(kv, qkv_t, do_t, o_t, lse, f_row, f_rep):
    S = kv.shape[0]
    blk = min(ATTN_BLOCK, S)
    strip = min(ATTN_STRIP, blk)
    n = S // blk
    tri_q, tri_k = _triangle(n, key_major=True)
    n_tiles = tri_q.shape[0]

    def body(tq_ref, tk_ref, k_ref, v_ref, qt_ref, kt_ref, dot_ref, ot_ref, lse_ref, fq_ref, fk_ref,
             dqt_ref, dkt_ref, dvt_ref, dfk_ref, dfq_ref, dq_s, dk_s, dv_s, dfk_s, dfq_s, row_s):
        t = pl.program_id(1)
        qi, ki = tq_ref[t], tk_ref[t]

        @pl.when(t == 0)
        def _():
            dq_s[...] = jnp.zeros_like(dq_s)
            dfq_s[...] = jnp.zeros_like(dfq_s)

        @pl.when(qi == ki)
        def _():
            dk_s[...] = jnp.zeros_like(dk_s)
            dv_s[...] = jnp.zeros_like(dv_s)
            dfk_s[...] = jnp.zeros_like(dfk_s)

        def update(diagonal):
            row_s[...] = fq_ref[...] - lse_ref[...]
            n_strips = blk // strip
            keys_of = lambda j: strip * (j + 1) if diagonal else blk

            def matmuls_in(j):
                s, keep = _strip_scores(k_ref, qt_ref, fk_ref, j, strip, keys_of(j), diagonal)
                dp = jnp.dot(v_ref[0:keys_of(j), :], dot_ref[:, strip * j:strip * (j + 1)], preferred_element_type=F32)
                return s, keep, dp

            def matmuls_out(j, pb, dsb):
                cols = slice(strip * j, strip * (j + 1))
                nkeys = keys_of(j)
                dv_s[:, 0:nkeys] += lax.dot_general(dot_ref[:, cols], pb, NT_DIMS, preferred_element_type=F32)
                dk_s[:, 0:nkeys] += lax.dot_general(qt_ref[:, cols], dsb, NT_DIMS, preferred_element_type=F32)
                dq_s[qi, :, cols] += jnp.dot(kt_ref[:, 0:nkeys], dsb, preferred_element_type=F32)

            ahead, behind = matmuls_in(0), None
            for j in range(n_strips):
                cols = slice(strip * j, strip * (j + 1))
                nkeys = keys_of(j)
                (s, keep, dp), ahead = ahead, (matmuls_in(j + 1) if j + 1 < n_strips else None)
                if behind is not None:
                    matmuls_out(*behind)
                p = jnp.exp2(s + row_s[:, cols])
                if diagonal:
                    p = jnp.where(keep, p, 0.0)
                dot = dot_ref[:, cols]
                delta = jnp.sum(dot.astype(F32) * ot_ref[:, cols].astype(F32), axis=0, keepdims=True)
                ds = p * (dp - delta)
                behind = (j, p.astype(BF16), ds.astype(BF16))
                lane_part = ds[:, 0:LANES]
                for g in range(1, strip // LANES):
                    lane_part = lane_part + ds[:, LANES * g:LANES * (g + 1)]
                dfk_s[0:nkeys, :] += lane_part
                sub_part = ds[0:SUBLANES, :]
                for g in range(1, nkeys // SUBLANES):
                    sub_part = sub_part + ds[SUBLANES * g:SUBLANES * (g + 1), :]
                dfq_s[qi, :, cols] += sub_part
            matmuls_out(*behind)

        @pl.when(qi == ki)
        def _():
            update(True)

        @pl.when(qi > ki)
        def _():
            update(False)

        @pl.when(qi == n - 1)
        def _():
            dkt_ref[...] = (dk_s[...] * ATTN_SCALE).astype(BF16)
            dvt_ref[...] = dv_s[...].astype(BF16)
            dfk_ref[...] = -jnp.sum(dfk_s[...].T, axis=0, keepdims=True)

        @pl.when(t == n_tiles - 1)
        def _():
            for j in range(n):
                dqt_ref[:, blk * j:blk * (j + 1)] = (dq_s[j] * ATTN_SCALE).astype(BF16)
                dfq_ref[:, blk * j:blk * (j + 1)] = jnp.sum(dfq_s[j], axis=0, keepdims=True)

    q_feat = pl.BlockSpec((HEAD_DIM, blk), lambda h, t, tq, tk: (h, tq[t]))
    q_row = pl.BlockSpec((None, 1, blk), lambda h, t, tq, tk: (h, 0, tq[t]))
    k_feat = pl.BlockSpec((HEAD_DIM, blk), lambda h, t, tq, tk: (h, tk[t]))
    return pl.pallas_call(
        body, name="attn_bwd",
        grid_spec=pltpu.PrefetchScalarGridSpec(
            num_scalar_prefetch=2, grid=(N_HEADS, n_tiles),
            in_specs=[pl.BlockSpec((blk, HEAD_DIM), lambda h, t, tq, tk: (tk[t], h)),
                      pl.BlockSpec((blk, HEAD_DIM), lambda h, t, tq, tk: (tk[t], N_HEADS + h)),
                      q_feat,
                      pl.BlockSpec((HEAD_DIM, blk), lambda h, t, tq, tk: (N_HEADS + h, tk[t])),
                      q_feat, q_feat, q_row, q_row,
                      pl.BlockSpec((None, blk, LANES), lambda h, t, tq, tk: (h, tk[t], 0))],
            out_specs=[pl.BlockSpec((HEAD_DIM, S), lambda h, t, tq, tk: (h, 0)), k_feat, k_feat,
                       pl.BlockSpec((None, 1, blk), lambda h, t, tq, tk: (h, 0, tk[t])),
                       pl.BlockSpec((None, 1, S), lambda h, t, tq, tk: (h, 0, 0))],
            scratch_shapes=[pltpu.VMEM((n, HEAD_DIM, blk), F32), pltpu.VMEM((HEAD_DIM, blk), F32),
                            pltpu.VMEM((HEAD_DIM, blk), F32), pltpu.VMEM((blk, LANES), F32),
                            pltpu.VMEM((n, SUBLANES, blk), F32), pltpu.VMEM((1, blk), F32)]),
        out_shape=[jax.ShapeDtypeStruct((N_HEADS * HEAD_DIM, S), BF16)] * 3
        + [jax.ShapeDtypeStruct((N_HEADS, 1, S), F32), jax.ShapeDtypeStruct((N_HEADS, 1, S), F32)],
        compiler_params=_cparams("parallel", "arbitrary"),
    )(tri_q, tri_k, kv, kv, qkv_t, qkv_t, do_t, o_t, lse, f_row, f_rep)


def _gate_mix(gates, ya, yb):
    S, D = ya.shape
    tr = min(ROW_TILE, S)

    def body(ga_ref, gb_ref, ya_ref, yb_ref, o_ref):
        o_ref[...] = (_sigmoid(ga_ref[...]) * ya_ref[...] + _sigmoid(gb_ref[...]) * yb_ref[...]).astype(BF16)

    col = lambda j: pl.BlockSpec((tr, D), lambda i, j=j: (i, j))
    return pl.pallas_call(
        body, name="gate_mix", grid=(S // tr,),
        in_specs=[col(0), col(1), col(0), col(0)],
        out_specs=col(0),
        out_shape=jax.ShapeDtypeStruct((S, D), BF16),
        compiler_params=_cparams("parallel"),
    )(gates, gates, ya, yb)


def _gate_bwd(dmix, gates, ya, yb):
    S, D = ya.shape
    tr = min(ROW_TILE, S)

    def body(dm_ref, ga_ref, gb_ref, ya_ref, yb_ref, dya_ref, dyb_ref, dg_ref):
        dm = dm_ref[...]
        sa, sb = _sigmoid(ga_ref[...]), _sigmoid(gb_ref[...])
        dya_ref[...] = (dm * sa).astype(BF16)
        dyb_ref[...] = (dm * sb).astype(BF16)
        dg_ref[:, 0:D] = ((dm * ya_ref[...]) * (sa * (1.0 - sa))).astype(BF16)
        dg_ref[:, D:] = ((dm * yb_ref[...]) * (sb * (1.0 - sb))).astype(BF16)

    col = lambda j: pl.BlockSpec((tr, D), lambda i, j=j: (i, j))
    return pl.pallas_call(
        body, name="gate_bwd", grid=(S // tr,),
        in_specs=[col(0), col(0), col(1), col(0), col(0)],
        out_specs=[col(0), col(0), pl.BlockSpec((tr, 2 * D), lambda i: (i, 0))],
        out_shape=[jax.ShapeDtypeStruct((S, D), BF16), jax.ShapeDtypeStruct((S, D), BF16),
                   jax.ShapeDtypeStruct((S, 2 * D), BF16)],
        compiler_params=_cparams("parallel"),
    )(dmix, gates, gates, ya, yb)


def _mesh_place():
    x, y, c = lax.axis_index("x"), lax.axis_index("y"), lax.axis_index("c")
    chips = [(1 - x, y), (x, 1 - y), (1 - x, 1 - y)]
    return x, y, c, chips


def _all_gather(shards):
    n = len(shards)

    def body(*refs):
        ins, outs = refs[:n], refs[n:2 * n]
        send_sems, recv_sems, local_sems = refs[2 * n:]
        x, y, c, chips = _mesh_place()
        me, sib = (x, y, c), (x, y, 1 - c)

        def copy(a, k, block, to, src=None):
            px, py, pc = block
            dst = outs[a].at[4 * px + 2 * py + pc]
            return pltpu.make_async_remote_copy(
                src_ref=dst if src is None else src, dst_ref=dst,
                send_sem=send_sems.at[a, k], recv_sem=recv_sems.at[a, k],
                device_id=to, device_id_type=MESH_ID)

        mine = [pltpu.make_async_copy(ins[a], outs[a].at[4 * x + 2 * y + c], local_sems.at[a]) for a in range(n)]
        for cp in mine:
            cp.start()
        first = []
        for a in range(n):
            first.append(copy(a, 0, me, sib, src=ins[a]))
            for j, chip in enumerate(chips):
                first.append(copy(a, 1 + j, me, (*chip, c), src=ins[a]))
        for cp in first:
            cp.start()
        passed = []
        for j, chip in enumerate(chips):
            for a in range(n):
                copy(a, 1 + j, (*chip, c), me).wait_recv()
                fwd = copy(a, 4 + j, (*chip, c), sib)
                fwd.start()
                passed.append(fwd)
        for a in range(n):
            copy(a, 0, sib, me).wait_recv()
            for j, chip in enumerate(chips):
                copy(a, 4 + j, (*chip, 1 - c), me).wait_recv()
        for cp in first + passed:
            cp.wait_send()
        for cp in mine:
            cp.wait()

    return pl.pallas_call(
        body, name="all_gather_weights",
        in_specs=[ANY] * n, out_specs=[ANY] * n,
        out_shape=[jax.ShapeDtypeStruct((N_DEV,) + s.shape, s.dtype) for s in shards],
        scratch_shapes=[pltpu.SemaphoreType.DMA((n, 7)), pltpu.SemaphoreType.DMA((n, 7)),
                        pltpu.SemaphoreType.DMA((n,))],
    )(*shards)


def _reduce_scatter_cores(grads, name):
    n = len(grads)

    def body(*refs):
        ins, gots = refs[:n], refs[n:2 * n]
        send_sems, recv_sems = refs[2 * n:]
        x, y, c, _ = _mesh_place()
        sib = (x, y, 1 - c)
        remote = []
        for a in range(n):
            for k in range(4):
                remote.append(pltpu.make_async_remote_copy(
                    src_ref=ins[a].at[2 * k + (1 - c)], dst_ref=gots[a].at[k],
                    send_sem=send_sems.at[a, k], recv_sem=recv_sems.at[a, k],
                    device_id=sib, device_id_type=MESH_ID))
        for cp in remote:
            cp.start()
        for cp in remote:
            cp.wait_recv()
        for cp in remote:
            cp.wait_send()

    return pl.pallas_call(
        body, name=name,
        in_specs=[ANY] * n, out_specs=[ANY] * n,
        out_shape=[jax.ShapeDtypeStruct((4,) + g.shape[1:], g.dtype) for g in grads],
        scratch_shapes=[pltpu.SemaphoreType.DMA((n, 4)), pltpu.SemaphoreType.DMA((n, 4))],
    )(*grads)


def _chip_partial_sum(blocks, got, core):
    R, C = got.shape[1:]
    tr = min(256, R)
    assert R % tr == 0

    def body(core_ref, a_ref, b_ref, s_ref, sb_ref):
        s = a_ref[...] + b_ref[...]
        s_ref[...] = s
        sb_ref[...] = s.astype(BF16)

    blk = pl.BlockSpec((None, tr, C), lambda k, i, core_ref: (k, i, 0))
    return pl.pallas_call(
        body, name="chip_partial_sum",
        grid_spec=pltpu.PrefetchScalarGridSpec(
            num_scalar_prefetch=1, grid=(4, R // tr),
            in_specs=[pl.BlockSpec((None, tr, C), lambda k, i, core_ref: (2 * k + core_ref[0], i, 0)), blk],
            out_specs=[blk, blk]),
        out_shape=[jax.ShapeDtypeStruct(got.shape, F32), jax.ShapeDtypeStruct(got.shape, BF16)],
        compiler_params=_cparams("parallel", "parallel"),
    )(core, blocks, got)


HBM_SPEC = pl.BlockSpec(memory_space=pltpu.HBM)
SEM_SPEC = pl.BlockSpec(memory_space=pltpu.SEMAPHORE)
FLIPS = [(dx, dy, dc) for dx in (0, 1) for dy in (0, 1) for dc in (0, 1) if (dx, dy, dc) != (0, 0, 0)]


def _flip(v, d):
    return 1 - v if d else v


def _gather_copies(srcs, lands, send_sems, recv_sems):
    x, y, c, _ = _mesh_place()
    sends, recvs = [], []
    for a in range(len(srcs)):
        for k, (dx, dy, dc) in enumerate(FLIPS):
            px, py, pc = _flip(x, dx), _flip(y, dy), _flip(c, dc)
            sem = len(FLIPS) * a + k
            common = dict(send_sem=send_sems.at[sem], recv_sem=recv_sems.at[sem],
                          device_id=(px, py, pc), device_id_type=MESH_ID)
            sends.append(pltpu.make_async_remote_copy(
                src_ref=srcs[a], dst_ref=lands[a].at[4 * x + 2 * y + c], **common))
            recvs.append(pltpu.make_async_remote_copy(
                src_ref=srcs[a], dst_ref=lands[a].at[4 * px + 2 * py + pc], **common))
    return sends, recvs


def _scatter_copies(srcs, lands, send_sems, recv_sems):
    x, y, c, chips = _mesh_place()
    sends = []
    for a in range(len(srcs)):
        for j, (px, py) in enumerate(chips):
            sends.append(pltpu.make_async_remote_copy(
                src_ref=srcs[a].at[2 * px + py], dst_ref=lands[a].at[j],
                send_sem=send_sems.at[3 * a + j], recv_sem=recv_sems.at[3 * a + j],
                device_id=(px, py, c), device_id_type=MESH_ID))
    return sends, sends


def _exchange_start(srcs, land_shapes, copies, n_copies, name):
    n = len(srcs)

    def body(*refs):
        src_refs, land_refs = refs[:n], refs[n:2 * n]
        send_sems, recv_sems = refs[2 * n], refs[2 * n + 1]
        token = refs[-1]
        sends, _ = copies(src_refs, land_refs, send_sems, recv_sems)
        for cp in sends:
            cp.start()
        token[...] = jnp.zeros_like(token)

    lands = [pltpu.with_memory_space_constraint(lax.empty(s.shape, s.dtype), pltpu.HBM) for s in land_shapes]
    srcs = [pltpu.with_memory_space_constraint(s, pltpu.HBM) for s in srcs]
    res = pl.pallas_call(
        body, name=name,
        out_shape=(pltpu.SemaphoreType.DMA((n * n_copies,)), pltpu.SemaphoreType.DMA((n * n_copies,)),
                   *[pltpu.HBM(s.shape, s.dtype) for s in srcs], *[pltpu.HBM(s.shape, s.dtype) for s in land_shapes],
                   jax.ShapeDtypeStruct((SUBLANES, LANES), F32)),
        in_specs=[HBM_SPEC] * (2 * n),
        out_specs=(SEM_SPEC, SEM_SPEC, *[HBM_SPEC] * (2 * n), pl.BlockSpec(memory_space=pltpu.VMEM)),
        input_output_aliases={i: 2 + i for i in range(2 * n)},
        compiler_params=pltpu.CompilerParams(has_side_effects=pltpu.SideEffectType.DATAFLOW_SIDE_EFFECTING),
    )(*srcs, *lands)
    return res[0], res[1], list(res[2:2 + n]), list(res[2 + n:2 + 2 * n]), res[-1]


def _exchange_wait(started, copies, after, name):
    send_sems, recv_sems, srcs, lands, _ = started
    n = len(srcs)

    def body(*refs):
        src_refs, land_refs = refs[:n], refs[n:2 * n]
        send_ref, recv_ref = refs[2 * n], refs[2 * n + 1]
        sends, recvs = copies(src_refs, land_refs, send_ref, recv_ref)
        for cp in sends:
            cp.wait_send()
        for cp in recvs:
            cp.wait_recv()

    res = pl.pallas_call(
        body, name=name,
        out_shape=tuple(pltpu.HBM(s.shape, s.dtype) for s in srcs + lands),
        in_specs=[HBM_SPEC] * (2 * n) + [SEM_SPEC, SEM_SPEC, ANY],
        out_specs=tuple([HBM_SPEC] * (2 * n)),
        input_output_aliases={i: i for i in range(2 * n)},
        compiler_params=pltpu.CompilerParams(has_side_effects=pltpu.SideEffectType.DATAFLOW_SIDE_EFFECTING),
    )(*srcs, *lands, send_sems, recv_sems, after)
    return list(res[:n]), list(res[n:])


def _all_reduce_small(vec):
    R = vec.shape[0]

    def body(v_ref, o_ref, sib_buf, chip_buf, send_sems, recv_sems):
        x, y, c, chips = _mesh_place()
        swap = pltpu.make_async_remote_copy(
            src_ref=v_ref, dst_ref=sib_buf, send_sem=send_sems.at[0], recv_sem=recv_sems.at[0],
            device_id=(x, y, 1 - c), device_id_type=MESH_ID)
        swap.start()
        swap.wait()
        my_chip = 2 * x + y
        chip_buf[my_chip] = v_ref[...] + sib_buf[...]
        sends = []
        for j, (px, py) in enumerate(chips):
            cp = pltpu.make_async_remote_copy(
                src_ref=chip_buf.at[my_chip], dst_ref=chip_buf.at[my_chip],
                send_sem=send_sems.at[1 + j], recv_sem=recv_sems.at[1 + j],
                device_id=(px, py, c), device_id_type=MESH_ID)
            cp.start()
            sends.append(cp)
        for j, (px, py) in enumerate(chips):
            pltpu.make_async_remote_copy(
                src_ref=chip_buf.at[2 * px + py], dst_ref=chip_buf.at[2 * px + py],
                send_sem=send_sems.at[1 + j], recv_sem=recv_sems.at[1 + j],
                device_id=(px, py, c), device_id_type=MESH_ID).wait_recv()
        for cp in sends:
            cp.wait_send()
        o_ref[...] = ((chip_buf[0] + chip_buf[1]) + chip_buf[2]) + chip_buf[3]

    vm = pl.BlockSpec(memory_space=pltpu.VMEM)
    return pl.pallas_call(
        body, name="all_reduce_small",
        in_specs=[vm], out_specs=vm,
        out_shape=jax.ShapeDtypeStruct(vec.shape, F32),
        scratch_shapes=[pltpu.VMEM((R, LANES), F32), pltpu.VMEM((4, R, LANES), F32),
                        pltpu.SemaphoreType.DMA((4,)), pltpu.SemaphoreType.DMA((4,))],
    )(vec)


def _adamw_math(w, g, m, v):
    m = ADAM_B1 * m + (1.0 - ADAM_B1) * g
    v = ADAM_B2 * v + (1.0 - ADAM_B2) * (g * g)
    m_hat = m / (1.0 - ADAM_B1 ** ADAM_STEP)
    v_hat = v / (1.0 - ADAM_B2 ** ADAM_STEP)
    delta = -ADAM_LR * (m_hat / (jnp.sqrt(v_hat) + ADAM_EPS) + ADAM_WD * w)
    return delta, m, v


def _adamw(w, m, v, g_own, g_got, chip, name):
    R, C = w.shape
    tr = R if R * C <= 256 * D_MODEL else 256
    assert R % tr == 0
    n_got = g_got.shape[0]

    def body(*refs):
        w_ref, m_ref, v_ref, go_ref = refs[1:5]
        got = refs[5:5 + n_got]
        g_ref, d_ref, nm_ref, nv_ref = refs[5 + n_got:]
        g = go_ref[...]
        for r in got:
            g = g + r[...].astype(F32)
        delta, m_new, v_new = _adamw_math(w_ref[...], g, m_ref[...], v_ref[...])
        g_ref[...] = g
        d_ref[...] = delta
        nm_ref[...] = m_new
        nv_ref[...] = v_new

    blk = pl.BlockSpec((tr, C), lambda i, chip_ref: (i, 0))
    own_spec = pl.BlockSpec((None, tr, C), lambda i, chip_ref: (chip_ref[0], i, 0))
    got_specs = [pl.BlockSpec((None, tr, C), lambda i, chip_ref, j=j: (j, i, 0)) for j in range(n_got)]
    return pl.pallas_call(
        body, name=name,
        grid_spec=pltpu.PrefetchScalarGridSpec(
            num_scalar_prefetch=1, grid=(R // tr,),
            in_specs=[blk] * 3 + [own_spec] + got_specs, out_specs=[blk] * 4),
        out_shape=[jax.ShapeDtypeStruct((R, C), F32)] * 4,
        compiler_params=_cparams("parallel"),
    )(chip, w, m, v, g_own, *([g_got] * n_got))


def _block_diag_pairs(wa, wx):
    def pairs(w):
        w = w.reshape(N_GROUPS, 2, LRU_BW, LRU_BW)
        z = jnp.zeros((N_GROUPS, LRU_BW, LRU_BW), w.dtype)
        top = jnp.concatenate([w[:, 0], z], axis=2)
        bot = jnp.concatenate([z, w[:, 1]], axis=2)
        return jnp.concatenate([top, bot], axis=1)
    return jnp.concatenate([pairs(wa), pairs(wx)], axis=2).astype(BF16)


def _block_diag_unpair(dbd):
    def unpair(g):
        blocks = jnp.stack([g[:, :LRU_BW, :LRU_BW], g[:, LRU_BW:, LRU_BW:]], axis=1)
        return blocks.reshape(LRU_BLOCKS, LRU_BW, LRU_BW)
    return unpair(dbd[:, :, :LANES]), unpair(dbd[:, :, LANES:])


def _local_step(x, target, W, small, late_weights=None, early_grads=None, in_grads=None):
    S, D = x.shape
    g1, g2, g3 = small["norm_mix_g"], small["norm_mlp_g"], small["norm_final_g"]
    cw, cb = small["conv_w"], small["conv_b"].reshape(1, D)
    ba, bx, lam = (small[k].reshape(1, D) for k in ("lru_ba", "lru_bx", "lru_lambda"))
    fb = jnp.pad(small["forget_b"], (0, LANES - N_HEADS)).reshape(1, LANES)
    bd = _block_diag_pairs(small["lru_wa"], small["lru_wx"])
    big = dict(tm=1024, tn=1024)

    u = _norm_fwd(x, g1, "norm_mix")
    (xg,) = _mm([(u, W["in_xg"])], tks=[D], outs=[F32], name="proj_xg", **big)
    (qkv_t,) = _mm([(W["in_qkv_t"], u)], tb=True, tks=[D], outs=[BF16], name="proj_qkv_t", **big)
    (kv,) = _mm([(u, W["in_kv"])], tks=[D], outs=[BF16], name="proj_kv", **big)
    (gates,) = _mm([(u, W["in_gates"])], tks=[D], outs=[F32], name="proj_gates", **big)
    (fl,) = _mm([(u, W["in_f"])], tks=[D], outs=[F32], name="proj_forget", **big)
    h, yain = _lru_fwd(xg, cw, cb, bd, ba, bx, lam)
    fcum, f_rep = _forget_cumsum(fl, fb)
    f_row = fcum[:, :N_HEADS].T.reshape(N_HEADS, 1, S)
    ob_t, lse = _attn_fwd(kv, qkv_t, f_row, f_rep)
    if late_weights is not None:
        W = {**W, **late_weights(lse)}
    (ya,) = _mm([(yain, W["branch_a"])], tks=[D], outs=[F32], name="branch_a", **big)
    (yb,) = _mm([(ob_t, W["branch_b"])], ta=True, tks=[D], outs=[F32], name="branch_b", **big)
    mix = _gate_mix(gates, ya, yb)
    (x1,) = _mm([(mix, W["out"])], tks=[D], outs=[F32], name="out_proj", extra=(x,),
                epi=lambda acc, res: (res + acc,), **big)
    m = _norm_fwd(x1, g2, "norm_mlp")
    relu, hh = _mm([(m, W["up"])], tks=[D], outs=[BF16, BF16], name="mlp_up",
                   epi=lambda acc: (jnp.maximum(acc, 0.0), jnp.square(jnp.maximum(acc, 0.0))), **big)
    deep = dict(tm=512, tn=1024, tks=[D_FF])
    wgrad = dict(tm=1024, tn=512, tks=[min(4096, S)])
    (x2,) = _mm([(hh, W["down"])], outs=[F32], name="mlp_down", extra=(x1,),
                epi=lambda acc, res: (res + acc,), **deep)
    loss_acc, dg3, dx2, dx2b = _final_norm_loss(x2, target, g3)

    (dhpre,) = _mm([(dx2b, W["down"])], tb=True, tks=[D], outs=[BF16], name="d_mlp_act", extra=(relu,),
                   epi=lambda acc, r: (acc * (2.0 * r.astype(F32)),), **big)
    (dw_down,) = _mm([(hh, dx2b)], ta=True, outs=[F32], name="dw_down", **wgrad)
    (dm,) = _mm([(dhpre, W["up"])], tb=True, outs=[F32], name="d_mlp_in", **deep)
    assert wgrad["tn"] == D_FF // N_DEV
    (dw_up,) = _mm([(m, dhpre)], ta=True, outs=[F32], name="dw_up", col_blocked=True, **wgrad)
    dx1, dx1b, dg2 = _norm_bwd(dm, x1, g2, dx2, "norm_mlp_bwd")
    (dmix,) = _mm([(dx1b, W["out"])], tb=True, tks=[D], outs=[F32], name="d_mix", **big)
    (dw_out,) = _mm([(mix, dx1b)], ta=True, outs=[F32], name="dw_out", **wgrad)
    dya, dyb, dgates = _gate_bwd(dmix, gates, ya, yb)
    (dob_t,) = _mm([(W["branch_b"], dyb)], tb=True, tks=[D], outs=[BF16], name="d_attn_out_t", **big)
    (dw_b,) = _mm([(ob_t, dyb)], outs=[F32], name="dw_branch_b", **wgrad)
    (dyain,) = _mm([(dya, W["branch_a"])], tb=True, tks=[D], outs=[F32], name="d_lru_out", **big)
    (dw_a,) = _mm([(yain, dya)], ta=True, outs=[F32], name="dw_branch_a", **wgrad)
    early = dict(w_branch_a=dw_a, w_branch_b=dw_b, w_out=dw_out, w_up=dw_up, w_down=dw_down)
    early_state, started = early_grads(early) if early_grads is not None else (None, 0.0)
    dq_t, dk_t, dv_t, dfk, dfq = _attn_bwd(kv, qkv_t, dob_t, ob_t, lse + started, f_row, f_rep)
    dF = jnp.pad((dfk.reshape(N_HEADS, S) + dfq.reshape(N_HEADS, S)).T, ((0, 0), (0, LANES - N_HEADS)))
    dfl, dfb = _forget_bwd(dF, fl, fb)
    dxg, dcw, dcb, dba, dbx, dlam, dbd = _lru_bwd(xg, h, dyain, cw, cb, bd, ba, bx, lam)
    dw_in_parts = [
        _mm([(u, dxg)], ta=True, outs=[F32], name="dw_in_xg", **wgrad)[0],
        _mm([(dq_t, u)], outs=[F32], name="dw_in_q_t", **wgrad)[0].T,
        _mm([(dk_t, u)], outs=[F32], name="dw_in_k_t", **wgrad)[0].T,
        _mm([(dv_t, u)], outs=[F32], name="dw_in_v_t", **wgrad)[0].T,
        _mm([(u, dgates)], ta=True, outs=[F32], name="dw_in_gates", **wgrad)[0],
        _mm([(u, dfl)], ta=True, outs=[F32], name="dw_in_forget", **wgrad)[0][:, :N_HEADS],
    ]
    dw_in = jnp.concatenate(dw_in_parts, axis=1)
    in_state, started = in_grads(dw_in) if in_grads is not None else (None, 0.0)
    wq_t, wk_t, wv_t = (W["in_qkv_t"][D * i:D * (i + 1)] for i in range(3))
    (du,) = _mm([(dxg, W["in_xg"]), (dq_t, wq_t), (dk_t, wk_t), (dv_t, wv_t), (dgates, W["in_gates"]),
                 (dfl, W["in_f"] + jnp.asarray(started, BF16))],
                ta=[False, True, True, True, False, False], tb=[True, False, False, False, True, True],
                tks=[1024, D, D, D, 1024, LANES], outs=[F32], name="d_norm_mix_out", tm=1024, tn=512)
    grad_x, _, dg1 = _norm_bwd(du, x, g1, dx1, "norm_mix_bwd")

    dwa, dwx = _block_diag_unpair(dbd)
    big_grads = dict(early, w_in=dw_in)
    small_grads = dict(norm_mix_g=dg1.reshape(D), conv_w=dcw, conv_b=dcb.reshape(D), lru_wa=dwa, lru_ba=dba.reshape(D),
                       lru_wx=dwx, lru_bx=dbx.reshape(D), lru_lambda=dlam.reshape(D), forget_b=dfb[0, :N_HEADS],
                       norm_mlp_g=dg2.reshape(D), norm_final_g=dg3.reshape(D))
    return loss_acc[0, 0], grad_x, big_grads, small_grads, (early_state, in_state)


SMALL_NAMES = ("norm_mix_g", "conv_b", "lru_wa", "lru_ba", "lru_wx", "lru_bx", "lru_lambda", "forget_b",
               "norm_mlp_g", "norm_final_g")
TILE_ELEMS = SUBLANES * LANES


def _pack_small(parts):
    rows = []
    for p in parts:
        flat = p.reshape(-1)
        flat = jnp.pad(flat, (0, (-flat.shape[0]) % TILE_ELEMS))
        rows.append(flat.reshape(-1, LANES))
    return jnp.concatenate(rows, axis=0)


def _packed_rows(shape):
    return -(-math.prod(shape) // TILE_ELEMS) * SUBLANES


def _adamw_small(g_packed, g_conv_w, weights, moms, vels):
    def rows_view(a):
        flat = a.reshape(-1)
        flat = jnp.pad(flat, (0, (-flat.shape[0]) % LANES))
        return flat.reshape(-1, LANES)

    names = SMALL_NAMES + ("conv_w",)
    views = [[rows_view(src[k]) for k in names] for src in (weights, moms, vels)]
    n = len(names)
    starts, r = [], 0
    for k in SMALL_NAMES:
        starts.append(r)
        r += _packed_rows(weights[k].shape)

    def body(*refs):
        gp_ref, gc_ref = refs[0], refs[1]
        w_refs, m_refs, v_refs = refs[2:2 + n], refs[2 + n:2 + 2 * n], refs[2 + 2 * n:2 + 3 * n]
        outs = refs[2 + 3 * n:]
        for i in range(n):
            rows = w_refs[i].shape[0]
            g = gc_ref[...] if i == n - 1 else gp_ref[starts[i]:starts[i] + rows, :]
            delta, m_new, v_new = _adamw_math(w_refs[i][...], g, m_refs[i][...], v_refs[i][...])
            for o_ref, val in zip(outs[4 * i:4 * i + 4], (g, delta, m_new, v_new)):
                o_ref[...] = val

    vm = pl.BlockSpec(memory_space=pltpu.VMEM)
    out_shape = [jax.ShapeDtypeStruct(v.shape, F32) for v in views[0] for _ in range(4)]
    res = pl.pallas_call(
        body, name="adamw_small",
        in_specs=[vm] * (2 + 3 * n), out_specs=[vm] * (4 * n), out_shape=out_shape,
    )(g_packed, g_conv_w, *views[0], *views[1], *views[2])
    dicts = ({}, {}, {}, {})
    for i, k in enumerate(names):
        size = math.prod(weights[k].shape)
        for d, arr in zip(dicts, res[4 * i:4 * i + 4]):
            d[k] = arr.reshape(-1)[:size].reshape(weights[k].shape)
    return dicts


BIG_NAMES = ("w_in", "w_branch_a", "w_branch_b", "w_out", "w_up", "w_down")
WEIGHT_ORDER = ("norm_mix_g", "w_in", "conv_w", "conv_b", "lru_wa", "lru_ba", "lru_wx", "lru_bx", "lru_lambda",
                "forget_b", "w_branch_a", "w_branch_b", "w_out", "norm_mlp_g", "w_up", "w_down", "norm_final_g")


def _to_dest_blocks(name, g):
    if g.ndim == 3:
        return g
    if name in ("w_in", "w_up"):
        return g.reshape(g.shape[0], N_DEV, g.shape[1] // N_DEV).transpose(1, 0, 2)
    return g.reshape(N_DEV, g.shape[0] // N_DEV, g.shape[1])


def kernel(x, norm_mix_g, w_in, conv_w, conv_b, lru_wa, lru_ba, lru_wx, lru_bx, lru_lambda, forget_b, w_branch_a, w_branch_b, w_out, norm_mlp_g, w_up, w_down, norm_final_g, loss_target, m_norm_mix_g, m_w_in, m_conv_w, m_conv_b, m_lru_wa, m_lru_ba, m_lru_wx, m_lru_bx, m_lru_lambda, m_forget_b, m_w_branch_a, m_w_branch_b, m_w_out, m_norm_mlp_g, m_w_up, m_w_down, m_norm_final_g, v_norm_mix_g, v_w_in, v_conv_w, v_conv_b, v_lru_wa, v_lru_ba, v_lru_wx, v_lru_bx, v_lru_lambda, v_forget_b, v_w_branch_a, v_w_branch_b, v_w_out, v_norm_mlp_g, v_w_up, v_w_down, v_norm_final_g):
    weights = dict(norm_mix_g=norm_mix_g, w_in=w_in, conv_w=conv_w, conv_b=conv_b, lru_wa=lru_wa, lru_ba=lru_ba,
                   lru_wx=lru_wx, lru_bx=lru_bx, lru_lambda=lru_lambda, forget_b=forget_b, w_branch_a=w_branch_a,
                   w_branch_b=w_branch_b, w_out=w_out, norm_mlp_g=norm_mlp_g, w_up=w_up, w_down=w_down,
                   norm_final_g=norm_final_g)
    moms = dict(norm_mix_g=m_norm_mix_g, w_in=m_w_in, conv_w=m_conv_w, conv_b=m_conv_b, lru_wa=m_lru_wa,
                lru_ba=m_lru_ba, lru_wx=m_lru_wx, lru_bx=m_lru_bx, lru_lambda=m_lru_lambda, forget_b=m_forget_b,
                w_branch_a=m_w_branch_a, w_branch_b=m_w_branch_b, w_out=m_w_out, norm_mlp_g=m_norm_mlp_g,
                w_up=m_w_up, w_down=m_w_down, norm_final_g=m_norm_final_g)
    vels = dict(norm_mix_g=v_norm_mix_g, w_in=v_w_in, conv_w=v_conv_w, conv_b=v_conv_b, lru_wa=v_lru_wa,
                lru_ba=v_lru_ba, lru_wx=v_lru_wx, lru_bx=v_lru_bx, lru_lambda=v_lru_lambda, forget_b=v_forget_b,
                w_branch_a=v_w_branch_a, w_branch_b=v_w_branch_b, w_out=v_w_out, norm_mlp_g=v_norm_mlp_g,
                w_up=v_w_up, w_down=v_w_down, norm_final_g=v_norm_final_g)
    S, D = x.shape[1], x.shape[2]
    me = 4 * lax.axis_index("x") + 2 * lax.axis_index("y") + lax.axis_index("c")

    core = lax.axis_index("c").astype(jnp.int32).reshape(1)
    chip = (2 * lax.axis_index("x") + lax.axis_index("y")).astype(jnp.int32).reshape(1)
    late_names = BIG_NAMES[1:]

    win_g, cw_g = _all_gather([w_in.astype(BF16), conv_w])
    late_shards = [weights[k].astype(BF16) for k in late_names]
    gather = _exchange_start(late_shards, [jax.ShapeDtypeStruct((N_DEV,) + s.shape, BF16) for s in late_shards],
                             _gather_copies, len(FLIPS), "gather_late_start")
    w_in_full = win_g.transpose(1, 0, 2).reshape(D, -1)
    cuts = (0, 2 * D, 5 * D, 7 * D)
    W = dict(in_xg=w_in_full[:, cuts[0]:cuts[1]], in_qkv_t=w_in_full[:, cuts[1]:cuts[2]].T,
             in_kv=w_in_full[:, cuts[1] + D:cuts[2]], in_gates=w_in_full[:, cuts[2]:cuts[3]],
             in_f=jnp.pad(w_in_full[:, cuts[3]:], ((0, 0), (0, LANES - N_HEADS))))
    small = {k: weights[k] for k in SMALL_NAMES}
    small["conv_w"] = cw_g.transpose(1, 0, 2).reshape(CONV_W, D)
    small["norm_mix_g"] = norm_mix_g + gather[4][0, 0]

    def late_weights(after):
        shards, lands = _exchange_wait(gather, _gather_copies, after, "gather_late_wait")
        wa_g, wb_g, wo_g, wup_g, wdn_g = (
            lax.dynamic_update_slice_in_dim(land, shard[None], me, axis=0) for land, shard in zip(lands, shards))
        return dict(branch_a=wa_g.reshape(D, D), branch_b=wb_g.reshape(D, D), out=wo_g.reshape(D, D),
                    up=wup_g.transpose(1, 0, 2).reshape(D, D_FF), down=wdn_g.reshape(D_FF, D))

    def core_stage(names, grads_by_name, tag):
        blocks = [_to_dest_blocks(k, grads_by_name[k]) for k in names]
        got = _reduce_scatter_cores(blocks, "reduce_scatter_cores_" + tag)
        return [_chip_partial_sum(b, g, core) for b, g in zip(blocks, got)]

    def reduce_behind(names, grads_by_name, tag):
        sums = core_stage(names, grads_by_name, tag)
        wire = [s[1] for s in sums]
        scatter = _exchange_start(wire, [jax.ShapeDtypeStruct((3,) + s.shape[1:], BF16) for s in wire],
                                  _scatter_copies, 3, "scatter_" + tag + "_start")
        return (sums, scatter), scatter[4][0, 0]

    loss_part, grad_x, _, small_grads, ((early_sums, early_scatter), (in_sums, in_scatter)) = _local_step(
        x.reshape(S, D), loss_target.reshape(S, D), W, small, late_weights,
        lambda g: reduce_behind(late_names, g, "early"), lambda g: reduce_behind(BIG_NAMES[:1], dict(w_in=g), "w_in"))
    loss = lax.psum(loss_part, MESH_AXES)
    _, early_others = _exchange_wait(early_scatter, _scatter_copies, grad_x, "scatter_early_wait")
    _, in_others = _exchange_wait(in_scatter, _scatter_copies, grad_x, "scatter_w_in_wait")
    sums = list(in_sums) + list(early_sums)
    others = list(in_others) + list(early_others)

    reduced = _all_reduce_small(_pack_small([small_grads[k] for k in SMALL_NAMES] + [small_grads["conv_w"]]))
    cw_full = reduced[reduced.shape[0] - _packed_rows((CONV_W, D)):].reshape(CONV_W, D)
    cw_cols = lax.dynamic_slice_in_dim(cw_full, me * (D // N_DEV), D // N_DEV, axis=1)

    grads, deltas, new_m, new_v = _adamw_small(reduced, cw_cols, weights, moms, vels)
    for k, s, g_got in zip(BIG_NAMES, sums, others):
        grads[k], deltas[k], new_m[k], new_v[k] = _adamw(weights[k], moms[k], vels[k], s[0], g_got, chip, "adamw_" + k)

    return (loss, grad_x.reshape(1, S, D), *[grads[k] for k in WEIGHT_ORDER], *[deltas[k] for k in WEIGHT_ORDER],
            *[new_m[k] for k in WEIGHT_ORDER], *[new_v[k] for k in WEIGHT_ORDER])
```

```python
import functools
import math

import jax
import jax.numpy as jnp
from jax import lax
from jax.experimental import pallas as pl
from jax.experimental.pallas import tpu as pltpu

F32 = jnp.float32
BF16 = jnp.bfloat16

D_MODEL = 1024
N_HEADS = 8
HEAD_DIM = 128
D_FF = 4096
LRU_BLOCKS = 16
LRU_BW = 64
LRU_C = 8.0
CONV_W = 4
RMS_EPS = 1e-6
N_DEV = 8
LANES = 128
SUBLANES = 8
N_GROUPS = D_MODEL // LANES
VMEM_LIMIT_BYTES = 52 * 1024 * 1024
ATTN_SCALE = 1.0 / math.sqrt(HEAD_DIM)
LOG2E = math.log2(math.e)
NEG_BIG = -1e30
ADAM_LR = 0.001
ADAM_B1 = 0.9
ADAM_B2 = 0.999
ADAM_EPS = 1e-08
ADAM_WD = 0.01
ADAM_STEP = 10
ATTN_BLOCK = 1024
ATTN_STRIP = 256
LRU_CHUNK = 256
ROW_TILE = 512
MESH_AXES = ("x", "y", "c")
MESH_ID = pl.DeviceIdType.MESH
ANY = pl.BlockSpec(memory_space=pl.ANY)

NT_DIMS = (((1,), (1,)), ((), ()))
TN_DIMS = (((0,), (0,)), ((), ()))
NN_DIMS = (((1,), (0,)), ((), ()))


def _cparams(*sem):
    return pltpu.CompilerParams(dimension_semantics=sem if sem else None, vmem_limit_bytes=VMEM_LIMIT_BYTES)


def _sigmoid(x):
    return 0.5 * (jnp.tanh(0.5 * x) + 1.0)


def _log1p_pos(e):
    u = 1.0 + e
    return jnp.where(u == 1.0, e, jnp.log(u) * (e / (u - 1.0)))


def _softplus(z):
    return jnp.maximum(z, 0.0) + _log1p_pos(jnp.exp(-jnp.abs(z)))


def _expm1_neg(x):
    series = x * (1.0 + x * 0.5 * (1.0 + x * (1.0 / 3.0) * (1.0 + x * 0.25)))
    return jnp.where(x > -0.03, series, jnp.exp(x) - 1.0)


GELU_C = math.sqrt(2.0 / math.pi)
GELU_K = 0.044715


def _gelu(x):
    return 0.5 * x * (1.0 + jnp.tanh(GELU_C * (x + GELU_K * (x * x * x))))


def _gelu_and_grad(x):
    t = jnp.tanh(GELU_C * (x + GELU_K * (x * x * x)))
    g = 0.5 * x * (1.0 + t)
    dg = 0.5 * (1.0 + t) + 0.5 * x * (1.0 - t * t) * (GELU_C * (1.0 + 3.0 * GELU_K * (x * x)))
    return g, dg


def _mm(pairs, *, ta=False, tb=False, tm, tn, tks, outs, name, epi=None, extra=(), col_blocked=False):
    n_pairs, n_extra, n_out = len(pairs), len(extra), len(outs)
    tas = list(ta) if isinstance(ta, (list, tuple)) else [ta] * n_pairs
    tbs = list(tb) if isinstance(tb, (list, tuple)) else [tb] * n_pairs
    a0, b0 = pairs[0]
    M = a0.shape[1] if tas[0] else a0.shape[0]
    N = b0.shape[0] if tbs[0] else b0.shape[1]
    tm, tn = min(tm, M), min(tn, N)
    nks, offs = [], []
    for (a, b), tk, pta in zip(pairs, tks, tas):
        K = a.shape[0] if pta else a.shape[1]
        assert K % tk == 0 and M % tm == 0 and N % tn == 0
        offs.append(sum(nks))
        nks.append(K // tk)
    nk_total = sum(nks)
    dims = [(((0 if pta else 1,), (1 if ptb else 0,)), ((), ())) for pta, ptb in zip(tas, tbs)]

    def kmap(off, nk):
        return lambda k: jnp.clip(k - off, 0, nk - 1)

    in_specs, operands = [], []
    for (a, b), tk, off, nk, pta, ptb in zip(pairs, tks, offs, nks, tas, tbs):
        km = kmap(off, nk)
        if pta:
            in_specs.append(pl.BlockSpec((tk, tm), lambda i, j, k, km=km: (km(k), i)))
        else:
            in_specs.append(pl.BlockSpec((tm, tk), lambda i, j, k, km=km: (i, km(k))))
        if ptb:
            in_specs.append(pl.BlockSpec((tn, tk), lambda i, j, k, km=km: (j, km(k))))
        else:
            in_specs.append(pl.BlockSpec((tk, tn), lambda i, j, k, km=km: (km(k), j)))
        operands += [a, b]
    for e in extra:
        in_specs.append(pl.BlockSpec((tm, tn), lambda i, j, k: (i, j)))
        operands.append(e)

    def body(*refs):
        ab = refs[:2 * n_pairs]
        ex = refs[2 * n_pairs:2 * n_pairs + n_extra]
        o = refs[2 * n_pairs + n_extra:2 * n_pairs + n_extra + n_out]
        k = pl.program_id(2)

        def finish(acc):
            res = epi(acc, *[e[...] for e in ex]) if epi is not None else (acc,)
            for r, oref in zip(res, o):
                oref[...] = r.astype(oref.dtype)

        if nk_total == 1:
            finish(lax.dot_general(ab[0][...], ab[1][...], dims[0], preferred_element_type=F32))
            return
        acc = refs[-1]
        for p in range(n_pairs):
            a_ref, b_ref = ab[2 * p], ab[2 * p + 1]

            @pl.when((k >= offs[p]) & (k < offs[p] + nks[p]))
            def _(a_ref=a_ref, b_ref=b_ref, pdims=dims[p]):
                prod = lax.dot_general(a_ref[...], b_ref[...], pdims, preferred_element_type=F32)

                @pl.when(k == 0)
                def _():
                    acc[...] = prod

                @pl.when(k > 0)
                def _():
                    acc[...] += prod

        @pl.when(k == nk_total - 1)
        def _():
            finish(acc[...])

    return pl.pallas_call(
        body,
        name=name,
        grid=(M // tm, N // tn, nk_total),
        in_specs=in_specs,
        out_specs=[pl.BlockSpec((None, tm, tn), lambda i, j, k: (j, i, 0)) if col_blocked
                   else pl.BlockSpec((tm, tn), lambda i, j, k: (i, j)) for _ in outs],
        out_shape=[jax.ShapeDtypeStruct((N // tn, M, tn) if col_blocked else (M, N), dt) for dt in outs],
        scratch_shapes=[] if nk_total == 1 else [pltpu.VMEM((tm, tn), F32)],
        compiler_params=_cparams("parallel", "parallel", "arbitrary"),
    )(*operands)


def _norm_fwd(x, g, name):
    S, D = x.shape
    tr = min(ROW_TILE, S)

    def body(x_ref, g_ref, o_ref):
        xv = x_ref[...]
        r = lax.rsqrt(jnp.mean(xv * xv, axis=-1, keepdims=True) + RMS_EPS)
        o_ref[...] = ((xv * r) * g_ref[...]).astype(o_ref.dtype)

    return pl.pallas_call(
        body, name=name, grid=(S // tr,),
        in_specs=[pl.BlockSpec((tr, D), lambda i: (i, 0)), pl.BlockSpec((1, D), lambda i: (0, 0))],
        out_specs=pl.BlockSpec((tr, D), lambda i: (i, 0)),
        out_shape=jax.ShapeDtypeStruct((S, D), BF16),
        compiler_params=_cparams("parallel"),
    )(x, g.reshape(1, D))


def _rms_bwd_rows(dy, xv, g):
    r = lax.rsqrt(jnp.mean(xv * xv, axis=-1, keepdims=True) + RMS_EPS)
    xn = xv * r
    dxn = dy * g
    dx = r * (dxn - xn * jnp.mean(dxn * xn, axis=-1, keepdims=True))
    dg = jnp.sum(dy * xn, axis=0, keepdims=True)
    return dx, dg


def _norm_bwd(dy, x, g, dres, name):
    S, D = x.shape
    tr = min(ROW_TILE, S)

    def body(dy_ref, x_ref, g_ref, dres_ref, dx_ref, dxb_ref, dg_ref):
        dx, dg = _rms_bwd_rows(dy_ref[...], x_ref[...], g_ref[...])
        dx = dres_ref[...] + dx
        dx_ref[...] = dx
        dxb_ref[...] = dx.astype(BF16)

        @pl.when(pl.program_id(0) == 0)
        def _():
            dg_ref[...] = jnp.zeros_like(dg_ref)

        dg_ref[...] += dg

    row = pl.BlockSpec((tr, D), lambda i: (i, 0))
    vec = pl.BlockSpec((1, D), lambda i: (0, 0))
    return pl.pallas_call(
        body, name=name, grid=(S // tr,),
        in_specs=[row, row, vec, row],
        out_specs=[row, row, vec],
        out_shape=[jax.ShapeDtypeStruct((S, D), F32), jax.ShapeDtypeStruct((S, D), BF16),
                   jax.ShapeDtypeStruct((1, D), F32)],
        compiler_params=_cparams("arbitrary"),
    )(dy, x, g.reshape(1, D), dres)


def _final_norm_loss(x2, target, g):
    S, D = x2.shape
    tr = min(ROW_TILE, S)

    def body(x_ref, t_ref, g_ref, loss_ref, dg_ref, dx_ref, dxb_ref):
        xv = x_ref[...]
        gv = g_ref[...]
        r = lax.rsqrt(jnp.mean(xv * xv, axis=-1, keepdims=True) + RMS_EPS)
        y = (xv * r) * gv
        err = y - t_ref[...]
        part = 0.5 * jnp.sum(jnp.mean(err * err, axis=-1, keepdims=True), axis=0, keepdims=True)
        dy = err * (1.0 / D)
        dx, dg = _rms_bwd_rows(dy, xv, gv)
        dx_ref[...] = dx
        dxb_ref[...] = dx.astype(BF16)

        @pl.when(pl.program_id(0) == 0)
        def _():
            dg_ref[...] = jnp.zeros_like(dg_ref)
            loss_ref[...] = jnp.zeros_like(loss_ref)

        dg_ref[...] += dg
        loss_ref[...] += jnp.broadcast_to(part, loss_ref.shape)

    row = pl.BlockSpec((tr, D), lambda i: (i, 0))
    vec = pl.BlockSpec((1, D), lambda i: (0, 0))
    return pl.pallas_call(
        body, name="final_norm_loss", grid=(S // tr,),
        in_specs=[row, row, vec],
        out_specs=[pl.BlockSpec((SUBLANES, LANES), lambda i: (0, 0)), vec, row, row],
        out_shape=[jax.ShapeDtypeStruct((SUBLANES, LANES), F32), jax.ShapeDtypeStruct((1, D), F32),
                   jax.ShapeDtypeStruct((S, D), F32), jax.ShapeDtypeStruct((S, D), BF16)],
        compiler_params=_cparams("arbitrary"),
    )(x2, target, g.reshape(1, D))


def _lru_gates(xa, bd_j, ba_j, bx_j, sp_j):
    z = jnp.dot(xa.astype(BF16), bd_j, preferred_element_type=F32)
    r = _sigmoid(z[:, :LANES] + ba_j)
    ig = _sigmoid(z[:, LANES:] + bx_j)
    log_a = (-LRU_C) * r * sp_j
    a = jnp.exp(log_a)
    mult = jnp.sqrt(-_expm1_neg(2.0 * log_a))
    return r, ig, a, mult


def _conv_rows(xpad, cw_ref, cb_ref, sl, tc):
    out = jnp.broadcast_to(cb_ref[:, sl], (tc, LANES))
    for k in range(CONV_W):
        out = out + xpad[pl.ds(SUBLANES - (CONV_W - 1) + k, tc), sl] * cw_ref[k:k + 1, sl]
    return out


def _lru_fwd(xg, cw, cb, bd, ba, bx, lam):
    S = xg.shape[0]
    D = D_MODEL
    tc = min(LRU_CHUNK, S)
    hb = tc // SUBLANES

    def body(xl_ref, halo_ref, g_ref, cw_ref, cb_ref, bd_ref, ba_ref, bx_ref, lam_ref,
             h_ref, y_ref, xpad, a_s, b_s, carry):
        i = pl.program_id(0)

        @pl.when(i == 0)
        def _():
            carry[...] = jnp.zeros_like(carry)

        xpad[0:SUBLANES, :] = jnp.where(i > 0, halo_ref[...], 0.0)
        xpad[SUBLANES:, :] = xl_ref[...]
        for j in range(N_GROUPS):
            sl = slice(LANES * j, LANES * (j + 1))
            xa = _conv_rows(xpad, cw_ref, cb_ref, sl, tc)
            sp = _softplus(-lam_ref[:, sl])
            _, ig, a, mult = _lru_gates(xa, bd_ref[j], ba_ref[:, sl], bx_ref[:, sl], sp)
            a_s[:, sl] = a
            b_s[:, sl] = mult * (ig * xa)

        row = lax.broadcasted_iota(jnp.int32, (SUBLANES, D), 0)

        def step(t, c):
            o = pl.multiple_of(t * SUBLANES, SUBLANES)
            A = a_s[pl.ds(o, SUBLANES), :]
            B = b_s[pl.ds(o, SUBLANES), :]
            for d in (1, 2, 4):
                keep = row >= d
                a_sh = jnp.where(keep, pltpu.roll(A, d, 0), 1.0)
                b_sh = jnp.where(keep, pltpu.roll(B, d, 0), 0.0)
                B = A * b_sh + B
                A = A * a_sh
            hh = A * c + B
            h_ref[pl.ds(o, SUBLANES), :] = hh
            return jnp.broadcast_to(hh[SUBLANES - 1:SUBLANES, :], (SUBLANES, D))

        carry[...] = lax.fori_loop(0, hb, step, carry[...])
        y_ref[...] = (_gelu(g_ref[...]) * h_ref[...]).astype(BF16)

    row_spec = lambda col: pl.BlockSpec((tc, D), lambda i, col=col: (i, col))
    halo = pl.BlockSpec((SUBLANES, D), lambda i: (jnp.maximum(i * hb - 1, 0), 0))
    full = lambda shape: pl.BlockSpec(shape, lambda i: tuple(0 for _ in shape))
    return pl.pallas_call(
        body, name="lru_fwd", grid=(S // tc,),
        in_specs=[row_spec(0), halo, row_spec(1), full((CONV_W, D)), full((1, D)),
                  full((N_GROUPS, LANES, 2 * LANES)), full((1, D)), full((1, D)), full((1, D))],
        out_specs=[pl.BlockSpec((tc, D), lambda i: (i, 0)), pl.BlockSpec((tc, D), lambda i: (i, 0))],
        out_shape=[jax.ShapeDtypeStruct((S, D), F32), jax.ShapeDtypeStruct((S, D), BF16)],
        scratch_shapes=[pltpu.VMEM((tc + SUBLANES, D), F32), pltpu.VMEM((tc, D), F32),
                        pltpu.VMEM((tc, D), F32), pltpu.VMEM((SUBLANES, D), F32)],
        compiler_params=_cparams("arbitrary"),
    )(xg, xg, xg, cw, cb, bd, ba, bx, lam)


def _lru_bwd(xg, h, dyain, cw, cb, bd, ba, bx, lam):
    S = xg.shape[0]
    D = D_MODEL
    tc = min(LRU_CHUNK, S)
    hb = tc // SUBLANES
    nc = S // tc

    def body(xl_ref, xhalo_ref, g_ref, h_ref, hhalo_ref, dy_ref, cw_ref, cb_ref, bd_ref, ba_ref, bx_ref,
             lam_ref, dxg_ref, dcw_ref, dcb_ref, dba_ref, dbx_ref, dlam_ref, dbd_ref,
             xpad, hpad, a_s, b_s, dh_s, g_s, xa_s, r_s, ig_s, m_s, dxa_pad, carry_e, dxa_head):
        i = pl.program_id(0)
        c = nc - 1 - i

        @pl.when(i == 0)
        def _():
            carry_e[...] = jnp.zeros_like(carry_e)
            dxa_head[...] = jnp.zeros_like(dxa_head)
            for ref in (dcw_ref, dcb_ref, dba_ref, dbx_ref, dlam_ref, dbd_ref):
                ref[...] = jnp.zeros_like(ref)

        xpad[0:SUBLANES, :] = jnp.where(c > 0, xhalo_ref[...], 0.0)
        xpad[SUBLANES:, :] = xl_ref[...]
        hpad[0:SUBLANES, :] = jnp.where(c > 0, hhalo_ref[...], 0.0)
        hpad[SUBLANES:, :] = h_ref[...]

        for j in range(N_GROUPS):
            sl = slice(LANES * j, LANES * (j + 1))
            xa = _conv_rows(xpad, cw_ref, cb_ref, sl, tc)
            sp = _softplus(-lam_ref[:, sl])
            r, ig, a, mult = _lru_gates(xa, bd_ref[j], ba_ref[:, sl], bx_ref[:, sl], sp)
            gl, dgl = _gelu_and_grad(g_ref[:, sl])
            dy = dy_ref[:, sl]
            dh = dy * gl
            dxg_ref[:, D + LANES * j:D + LANES * (j + 1)] = (dy * h_ref[:, sl] * dgl).astype(BF16)
            a_s[:, sl] = a
            b_s[:, sl] = a * dh
            dh_s[:, sl] = dh
            xa_s[:, sl] = xa
            r_s[:, sl] = r
            ig_s[:, sl] = ig
            m_s[:, sl] = mult

        row = lax.broadcasted_iota(jnp.int32, (SUBLANES, D), 0)

        def step(tt, ce):
            o = pl.multiple_of((hb - 1 - tt) * SUBLANES, SUBLANES)
            A = a_s[pl.ds(o, SUBLANES), :]
            B = b_s[pl.ds(o, SUBLANES), :]
            for d in (1, 2, 4):
                keep = row < SUBLANES - d
                a_sh = jnp.where(keep, pltpu.roll(A, SUBLANES - d, 0), 1.0)
                b_sh = jnp.where(keep, pltpu.roll(B, SUBLANES - d, 0), 0.0)
                B = A * b_sh + B
                A = A * a_sh
            e = A * ce + B
            e_next = jnp.where(row < SUBLANES - 1, pltpu.roll(e, SUBLANES - 1, 0), ce)
            g_s[pl.ds(o, SUBLANES), :] = dh_s[pl.ds(o, SUBLANES), :] + e_next
            return jnp.broadcast_to(e[0:1, :], (SUBLANES, D))

        carry_e[...] = lax.fori_loop(0, hb, step, carry_e[...])

        for j in range(N_GROUPS):
            sl = slice(LANES * j, LANES * (j + 1))
            gg = g_s[:, sl]
            xa, r, ig, mult, a = xa_s[:, sl], r_s[:, sl], ig_s[:, sl], m_s[:, sl], a_s[:, sl]
            hprev = hpad[pl.ds(SUBLANES - 1, tc), sl]
            sp = _softplus(-lam_ref[:, sl])
            da = gg * hprev
            dmult = gg * (ig * xa)
            dig = gg * (mult * xa)
            dxa = gg * (mult * ig)
            dla = da * a - dmult * ((a * a) / mult)
            dr = dla * ((-LRU_C) * sp)
            dlam_ref[:, sl] += jnp.sum(dla * r, axis=0, keepdims=True)
            dza = dr * r * (1.0 - r)
            dzx = dig * ig * (1.0 - ig)
            dba_ref[:, sl] += jnp.sum(dza, axis=0, keepdims=True)
            dbx_ref[:, sl] += jnp.sum(dzx, axis=0, keepdims=True)
            dz = jnp.concatenate([dza, dzx], axis=1).astype(BF16)
            dbd_ref[j] += lax.dot_general(xa.astype(BF16), dz, TN_DIMS, preferred_element_type=F32)
            dxa = dxa + lax.dot_general(dz, bd_ref[j], NT_DIMS, preferred_element_type=F32)
            dxa_pad[0:tc, sl] = dxa

        dxa_pad[tc:, :] = dxa_head[...]
        dxa_head[...] = dxa_pad[0:SUBLANES, :]

        for j in range(N_GROUPS):
            sl = slice(LANES * j, LANES * (j + 1))
            dxa = dxa_pad[0:tc, sl]
            dxl = jnp.zeros((tc, LANES), F32)
            for k in range(CONV_W):
                dxl = dxl + dxa_pad[pl.ds(CONV_W - 1 - k, tc), sl] * cw_ref[k:k + 1, sl]
                dcw_ref[k:k + 1, sl] += jnp.sum(
                    dxa * xpad[pl.ds(SUBLANES - (CONV_W - 1) + k, tc), sl], axis=0, keepdims=True)
            dxg_ref[:, sl] = dxl.astype(BF16)
            dcb_ref[:, sl] += jnp.sum(dxa, axis=0, keepdims=True)

        @pl.when(i == nc - 1)
        def _():
            dlam_ref[...] = dlam_ref[...] * (LRU_C * _sigmoid(-lam_ref[...]))

    rev = lambda col: pl.BlockSpec((tc, D), lambda i, col=col: (nc - 1 - i, col))
    halo = pl.BlockSpec((SUBLANES, D), lambda i: (jnp.maximum((nc - 1 - i) * hb - 1, 0), 0))
    full = lambda shape: pl.BlockSpec(shape, lambda i: tuple(0 for _ in shape))
    big = lambda: pltpu.VMEM((tc, D), F32)
    return pl.pallas_call(
        body, name="lru_bwd", grid=(nc,),
        in_specs=[rev(0), halo, rev(1), rev(0), halo, rev(0), full((CONV_W, D)), full((1, D)),
                  full((N_GROUPS, LANES, 2 * LANES)), full((1, D)), full((1, D)), full((1, D))],
        out_specs=[pl.BlockSpec((tc, 2 * D), lambda i: (nc - 1 - i, 0)), full((CONV_W, D)), full((1, D)),
                   full((1, D)), full((1, D)), full((1, D)), full((N_GROUPS, LANES, 2 * LANES))],
        out_shape=[jax.ShapeDtypeStruct((S, 2 * D), BF16), jax.ShapeDtypeStruct((CONV_W, D), F32),
                   jax.ShapeDtypeStruct((1, D), F32), jax.ShapeDtypeStruct((1, D), F32),
                   jax.ShapeDtypeStruct((1, D), F32), jax.ShapeDtypeStruct((1, D), F32),
                   jax.ShapeDtypeStruct((N_GROUPS, LANES, 2 * LANES), F32)],
        scratch_shapes=[pltpu.VMEM((tc + SUBLANES, D), F32), pltpu.VMEM((tc + SUBLANES, D), F32),
                        big(), big(), big(), big(), big(), big(), big(), big(),
                        pltpu.VMEM((tc + SUBLANES, D), F32), pltpu.VMEM((SUBLANES, D), F32),
                        pltpu.VMEM((SUBLANES, D), F32)],
        compiler_params=_cparams("arbitrary"),
    )(xg, xg, xg, h, h, dyain, cw, cb, bd, ba, bx, lam)


def _forget_cumsum(fl, fb):
    S = fl.shape[0]
    tr = min(ROW_TILE, S)
    hb = tr // SUBLANES

    def body(fl_ref, fb_ref, o_ref, rep_ref, lf_s, carry):
        @pl.when(pl.program_id(0) == 0)
        def _():
            carry[...] = jnp.zeros_like(carry)

        lf_s[...] = -_softplus(-(fl_ref[...] + fb_ref[...]))
        row = lax.broadcasted_iota(jnp.int32, (SUBLANES, LANES), 0)

        def step(t, c):
            o = pl.multiple_of(t * SUBLANES, SUBLANES)
            B = lf_s[pl.ds(o, SUBLANES), :]
            for d in (1, 2, 4):
                B = B + jnp.where(row >= d, pltpu.roll(B, d, 0), 0.0)
            B = B + c
            o_ref[pl.ds(o, SUBLANES), :] = B * LOG2E
            return jnp.broadcast_to(B[SUBLANES - 1:SUBLANES, :], (SUBLANES, LANES))

        carry[...] = lax.fori_loop(0, hb, step, carry[...])
        for h in range(N_HEADS):
            rep_ref[h] = jnp.broadcast_to(o_ref[:, h:h + 1], (tr, LANES))

    return pl.pallas_call(
        body, name="forget_cumsum", grid=(S // tr,),
        in_specs=[pl.BlockSpec((tr, LANES), lambda i: (i, 0)), pl.BlockSpec((1, LANES), lambda i: (0, 0))],
        out_specs=[pl.BlockSpec((tr, LANES), lambda i: (i, 0)),
                   pl.BlockSpec((N_HEADS, tr, LANES), lambda i: (0, i, 0))],
        out_shape=[jax.ShapeDtypeStruct((S, LANES), F32), jax.ShapeDtypeStruct((N_HEADS, S, LANES), F32)],
        scratch_shapes=[pltpu.VMEM((tr, LANES), F32), pltpu.VMEM((SUBLANES, LANES), F32)],
        compiler_params=_cparams("arbitrary"),
    )(fl, fb)


def _forget_bwd(dF, fl, fb):
    S = fl.shape[0]
    tr = min(ROW_TILE, S)
    hb = tr // SUBLANES
    nc = S // tr

    def body(df_ref, fl_ref, fb_ref, o_ref, dfb_ref, carry):
        @pl.when(pl.program_id(0) == 0)
        def _():
            carry[...] = jnp.zeros_like(carry)
            dfb_ref[...] = jnp.zeros_like(dfb_ref)

        row = lax.broadcasted_iota(jnp.int32, (SUBLANES, LANES), 0)

        def step(tt, carried):
            c, acc = carried
            o = pl.multiple_of((hb - 1 - tt) * SUBLANES, SUBLANES)
            B = df_ref[pl.ds(o, SUBLANES), :]
            for d in (1, 2, 4):
                B = B + jnp.where(row < SUBLANES - d, pltpu.roll(B, SUBLANES - d, 0), 0.0)
            B = B + c
            z = fl_ref[pl.ds(o, SUBLANES), :] + fb_ref[...]
            dz = B * _sigmoid(-z)
            o_ref[pl.ds(o, SUBLANES), :] = dz.astype(BF16)
            return jnp.broadcast_to(B[0:1, :], (SUBLANES, LANES)), acc + dz

        c, acc = lax.fori_loop(0, hb, step, (carry[...], jnp.zeros((SUBLANES, LANES), F32)))
        carry[...] = c
        dfb_ref[...] += jnp.sum(acc, axis=0, keepdims=True)

    rev = pl.BlockSpec((tr, LANES), lambda i: (nc - 1 - i, 0))
    vec = pl.BlockSpec((1, LANES), lambda i: (0, 0))
    return pl.pallas_call(
        body, name="forget_bwd", grid=(nc,),
        in_specs=[rev, rev, vec],
        out_specs=[rev, vec],
        out_shape=[jax.ShapeDtypeStruct((S, LANES), BF16), jax.ShapeDtypeStruct((1, LANES), F32)],
        scratch_shapes=[pltpu.VMEM((SUBLANES, LANES), F32)],
        compiler_params=_cparams("arbitrary"),
    )(dF, fl, fb)


def _triangle(n, key_major):
    pairs = [(q, k) for q in range(n) for k in range(q + 1)]
    if key_major:
        pairs.sort(key=lambda qk: (qk[1], qk[0]))
    return (jnp.asarray([q for q, _ in pairs], jnp.int32), jnp.asarray([k for _, k in pairs], jnp.int32))


def _strip_scores(k_ref, qt_ref, fk_ref, j, strip, nkeys, diagonal):
    cols = slice(strip * j, strip * (j + 1))
    s = jnp.dot(k_ref[0:nkeys, :], qt_ref[:, cols], preferred_element_type=F32) * (ATTN_SCALE * LOG2E)
    fk = fk_ref[0:nkeys, :]
    s = s - jnp.concatenate([fk] * (strip // LANES), axis=1)
    keep = None
    if diagonal:
        keys = lax.broadcasted_iota(jnp.int32, (nkeys, strip), 0)
        queries = lax.broadcasted_iota(jnp.int32, (nkeys, strip), 1) + strip * j
        keep = keys <= queries
    return s, keep


def _attn_fwd(kv, qkv_t, f_row, f_rep):
    S = kv.shape[0]
    blk = min(ATTN_BLOCK, S)
    strip = min(ATTN_STRIP, blk)
    n = S // blk
    tri_q, tri_k = _triangle(n, key_major=False)
    ones_rows = 2 * SUBLANES

    def body(tq_ref, tk_ref, k_ref, qt_ref, vt_ref, fq_ref, fk_ref, ot_ref, lse_ref, m_s, acc_s, vta_s):
        t = pl.program_id(1)
        qi, ki = tq_ref[t], tk_ref[t]

        @pl.when(ki == 0)
        def _():
            m_s[...] = jnp.full_like(m_s, NEG_BIG)
            acc_s[...] = jnp.zeros_like(acc_s)

        vta_s[0:HEAD_DIM, :] = vt_ref[...]
        vta_s[HEAD_DIM:, :] = jnp.ones((ones_rows, blk), BF16)

        def update(diagonal):
            n_strips = blk // strip
            keys_of = lambda j: strip * (j + 1) if diagonal else blk
            scores = lambda j: _strip_scores(k_ref, qt_ref, fk_ref, j, strip, keys_of(j), diagonal)
            def weighted_values(j, alpha, pb):
                cols = slice(strip * j, strip * (j + 1))
                acc_s[:, cols] = alpha * acc_s[:, cols] + jnp.dot(
                    vta_s[:, 0:keys_of(j)], pb, preferred_element_type=F32)

            ahead, behind = scores(0), None
            for j in range(n_strips):
                cols = slice(strip * j, strip * (j + 1))
                (s, keep), ahead = ahead, (scores(j + 1) if j + 1 < n_strips else None)
                if behind is not None:
                    weighted_values(*behind)
                if diagonal:
                    s = jnp.where(keep, s, NEG_BIG)
                fq = fq_ref[:, cols]
                m_old = m_s[:, cols]
                m_new = jnp.maximum(m_old, jnp.max(s, axis=0, keepdims=True) + fq)
                p = jnp.exp2(s - (m_new - fq))
                behind = (j, jnp.exp2(m_old - m_new), p.astype(BF16))
                m_s[:, cols] = m_new
            weighted_values(*behind)

        @pl.when(ki < qi)
        def _():
            update(False)

        @pl.when(ki == qi)
        def _():
            update(True)
            denom = acc_s[HEAD_DIM:HEAD_DIM + 1, :]
            ot_ref[...] = (acc_s[0:HEAD_DIM, :] / denom).astype(BF16)
            lse_ref[...] = m_s[...] + jnp.log2(denom)

    return pl.pallas_call(
        body, name="attn_fwd",
        grid_spec=pltpu.PrefetchScalarGridSpec(
            num_scalar_prefetch=2, grid=(N_HEADS, tri_q.shape[0]),
            in_specs=[pl.BlockSpec((blk, HEAD_DIM), lambda h, t, tq, tk: (tk[t], h)),
                      pl.BlockSpec((HEAD_DIM, blk), lambda h, t, tq, tk: (h, tq[t])),
                      pl.BlockSpec((HEAD_DIM, blk), lambda h, t, tq, tk: (2 * N_HEADS + h, tk[t])),
                      pl.BlockSpec((None, 1, blk), lambda h, t, tq, tk: (h, 0, tq[t])),
                      pl.BlockSpec((None, blk, LANES), lambda h, t, tq, tk: (h, tk[t], 0))],
            out_specs=[pl.BlockSpec((HEAD_DIM, blk), lambda h, t, tq, tk: (h, tq[t])),
                       pl.BlockSpec((None, 1, blk), lambda h, t, tq, tk: (h, 0, tq[t]))],
            scratch_shapes=[pltpu.VMEM((1, blk), F32), pltpu.VMEM((HEAD_DIM + ones_rows, blk), F32),
                            pltpu.VMEM((HEAD_DIM + ones_rows, blk), BF16)]),
        out_shape=[jax.ShapeDtypeStruct((N_HEADS * HEAD_DIM, S), BF16), jax.ShapeDtypeStruct((N_HEADS, 1, S), F32)],
        compiler_params=_cparams("parallel", "arbitrary"),
    )(tri_q, tri_k, kv, qkv_t, qkv_t, f_row, f_rep)


def _attn_bwd(kv, qkv_t, do_t, o_t, lse, f_row, f_rep):
    S = kv.shape[0]
    blk = min(ATTN_BLOCK, S)
    strip = min(ATTN_STRIP, blk)
    n = S // blk
    tri_q, tri_k = _triangle(n, key_major=True)
    n_tiles = tri_q.shape[0]

    def body(tq_ref, tk_ref, k_ref, v_ref, qt_ref, kt_ref, dot_ref, ot_ref, lse_ref, fq_ref, fk_ref,
             dqt_ref, dkt_ref, dvt_ref, dfk_ref, dfq_ref, dq_s, dk_s, dv_s, dfk_s, dfq_s, row_s):
        t = pl.program_id(1)
        qi, ki = tq_ref[t], tk_ref[t]

        @pl.when(t == 0)
        def _():
            dq_s[...] = jnp.zeros_like(dq_s)
            dfq_s[...] = jnp.zeros_like(dfq_s)

        @pl.when(qi == ki)
        def _():
            dk_s[...] = jnp.zeros_like(dk_s)
            dv_s[...] = jnp.zeros_like(dv_s)
            dfk_s[...] = jnp.zeros_like(dfk_s)

        def update(diagonal):
            row_s[...] = fq_ref[...] - lse_ref[...]
            n_strips = blk // strip
            keys_of = lambda j: strip * (j + 1) if diagonal else blk

            def matmuls_in(j):
                s, keep = _strip_scores(k_ref, qt_ref, fk_ref, j, strip, keys_of(j), diagonal)
                dp = jnp.dot(v_ref[0:keys_of(j), :], dot_ref[:, strip * j:strip * (j + 1)], preferred_element_type=F32)
                return s, keep, dp

            def matmuls_out(j, pb, dsb):
                cols = slice(strip * j, strip * (j + 1))
                nkeys = keys_of(j)
                dv_s[:, 0:nkeys] += lax.dot_general(dot_ref[:, cols], pb, NT_DIMS, preferred_element_type=F32)
                dk_s[:, 0:nkeys] += lax.dot_general(qt_ref[:, cols], dsb, NT_DIMS, preferred_element_type=F32)
                dq_s[qi, :, cols] += jnp.dot(kt_ref[:, 0:nkeys], dsb, preferred_element_type=F32)

            ahead, behind = matmuls_in(0), None
            for j in range(n_strips):
                cols = slice(strip * j, strip * (j + 1))
                nkeys = keys_of(j)
                (s, keep, dp), ahead = ahead, (matmuls_in(j + 1) if j + 1 < n_strips else None)
                if behind is not None:
                    matmuls_out(*behind)
                p = jnp.exp2(s + row_s[:, cols])
                if diagonal:
                    p = jnp.where(keep, p, 0.0)
                dot = dot_ref[:, cols]
                delta = jnp.sum(dot.astype(F32) * ot_ref[:, cols].astype(F32), axis=0, keepdims=True)
                ds = p * (dp - delta)
                behind = (j, p.astype(BF16), ds.astype(BF16))
                lane_part = ds[:, 0:LANES]
                for g in range(1, strip // LANES):
                    lane_part = lane_part + ds[:, LANES * g:LANES * (g + 1)]
                dfk_s[0:nkeys, :] += lane_part
                sub_part = ds[0:SUBLANES, :]
                for g in range(1, nkeys // SUBLANES):
                    sub_part = sub_part + ds[SUBLANES * g:SUBLANES * (g + 1), :]
                dfq_s[qi, :, cols] += sub_part
            matmuls_out(*behind)

        @pl.when(qi == ki)
        def _():
            update(True)

        @pl.when(qi > ki)
        def _():
            update(False)

        @pl.when(qi == n - 1)
        def _():
            dkt_ref[...] = (dk_s[...] * ATTN_SCALE).astype(BF16)
            dvt_ref[...] = dv_s[...].astype(BF16)
            dfk_ref[...] = -jnp.sum(dfk_s[...].T, axis=0, keepdims=True)

        @pl.when(t == n_tiles - 1)
        def _():
            for j in range(n):
                dqt_ref[:, blk * j:blk * (j + 1)] = (dq_s[j] * ATTN_SCALE).astype(BF16)
                dfq_ref[:, blk * j:blk * (j + 1)] = jnp.sum(dfq_s[j], axis=0, keepdims=True)

    q_feat = pl.BlockSpec((HEAD_DIM, blk), lambda h, t, tq, tk: (h, tq[t]))
    q_row = pl.BlockSpec((None, 1, blk), lambda h, t, tq, tk: (h, 0, tq[t]))
    k_feat = pl.BlockSpec((HEAD_DIM, blk), lambda h, t, tq, tk: (h, tk[t]))
    return pl.pallas_call(
        body, name="attn_bwd",
        grid_spec=pltpu.PrefetchScalarGridSpec(
            num_scalar_prefetch=2, grid=(N_HEADS, n_tiles),
            in_specs=[pl.BlockSpec((blk, HEAD_DIM), lambda h, t, tq, tk: (tk[t], h)),
                      pl.BlockSpec((blk, HEAD_DIM), lambda h, t, tq, tk: (tk[t], N_HEADS + h)),
                      q_feat,
                      pl.BlockSpec((HEAD_DIM, blk), lambda h, t, tq, tk: (N_HEADS + h, tk[t])),
                      q_feat, q_feat, q_row, q_row,
                      pl.BlockSpec((None, blk, LANES), lambda h, t, tq, tk: (h, tk[t], 0))],
            out_specs=[pl.BlockSpec((HEAD_DIM, S), lambda h, t, tq, tk: (h, 0)), k_feat, k_feat,
                       pl.BlockSpec((None, 1, blk), lambda h, t, tq, tk: (h, 0, tk[t])),
                       pl.BlockSpec((None, 1, S), lambda h, t, tq, tk: (h, 0, 0))],
            scratch_shapes=[pltpu.VMEM((n, HEAD_DIM, blk), F32), pltpu.VMEM((HEAD_DIM, blk), F32),
                            pltpu.VMEM((HEAD_DIM, blk), F32), pltpu.VMEM((blk, LANES), F32),
                            pltpu.VMEM((n, SUBLANES, blk), F32), pltpu.VMEM((1, blk), F32)]),
        out_shape=[jax.ShapeDtypeStruct((N_HEADS * HEAD_DIM, S), BF16)] * 3
        + [jax.ShapeDtypeStruct((N_HEADS, 1, S), F32), jax.ShapeDtypeStruct((N_HEADS, 1, S), F32)],
        compiler_params=_cparams("parallel", "arbitrary"),
    )(tri_q, tri_k, kv, kv, qkv_t, qkv_t, do_t, o_t, lse, f_row, f_rep)


def _gate_mix(gates, ya, yb):
    S, D = ya.shape
    tr = min(ROW_TILE, S)

    def body(ga_ref, gb_ref, ya_ref, yb_ref, o_ref):
        o_ref[...] = (_sigmoid(ga_ref[...]) * ya_ref[...] + _sigmoid(gb_ref[...]) * yb_ref[...]).astype(BF16)

    col = lambda j: pl.BlockSpec((tr, D), lambda i, j=j: (i, j))
    return pl.pallas_call(
        body, name="gate_mix", grid=(S // tr,),
        in_specs=[col(0), col(1), col(0), col(0)],
        out_specs=col(0),
        out_shape=jax.ShapeDtypeStruct((S, D), BF16),
        compiler_params=_cparams("parallel"),
    )(gates, gates, ya, yb)


def _gate_bwd(dmix, gates, ya, yb):
    S, D = ya.shape
    tr = min(ROW_TILE, S)

    def body(dm_ref, ga_ref, gb_ref, ya_ref, yb_ref, dya_ref, dyb_ref, dg_ref):
        dm = dm_ref[...]
        sa, sb = _sigmoid(ga_ref[...]), _sigmoid(gb_ref[...])
        dya_ref[...] = (dm * sa).astype(BF16)
        dyb_ref[...] = (dm * sb).astype(BF16)
        dg_ref[:, 0:D] = ((dm * ya_ref[...]) * (sa * (1.0 - sa))).astype(BF16)
        dg_ref[:, D:] = ((dm * yb_ref[...]) * (sb * (1.0 - sb))).astype(BF16)

    col = lambda j: pl.BlockSpec((tr, D), lambda i, j=j: (i, j))
    return pl.pallas_call(
        body, name="gate_bwd", grid=(S // tr,),
        in_specs=[col(0), col(0), col(1), col(0), col(0)],
        out_specs=[col(0), col(0), pl.BlockSpec((tr, 2 * D), lambda i: (i, 0))],
        out_shape=[jax.ShapeDtypeStruct((S, D), BF16), jax.ShapeDtypeStruct((S, D), BF16),
                   jax.ShapeDtypeStruct((S, 2 * D), BF16)],
        compiler_params=_cparams("parallel"),
    )(dmix, gates, gates, ya, yb)


def _mesh_place():
    x, y, c = lax.axis_index("x"), lax.axis_index("y"), lax.axis_index("c")
    chips = [(1 - x, y), (x, 1 - y), (1 - x, 1 - y)]
    return x, y, c, chips


def _all_gather(shards):
    n = len(shards)

    def body(*refs):
        ins, outs = refs[:n], refs[n:2 * n]
        send_sems, recv_sems, local_sems = refs[2 * n:]
        x, y, c, chips = _mesh_place()
        me, sib = (x, y, c), (x, y, 1 - c)

        def copy(a, k, block, to, src=None):
            px, py, pc = block
            dst = outs[a].at[4 * px + 2 * py + pc]
            return pltpu.make_async_remote_copy(
                src_ref=dst if src is None else src, dst_ref=dst,
                send_sem=send_sems.at[a, k], recv_sem=recv_sems.at[a, k],
                device_id=to, device_id_type=MESH_ID)

        mine = [pltpu.make_async_copy(ins[a], outs[a].at[4 * x + 2 * y + c], local_sems.at[a]) for a in range(n)]
        for cp in mine:
            cp.start()
        first = []
        for a in range(n):
            first.append(copy(a, 0, me, sib, src=ins[a]))
            for j, chip in enumerate(chips):
                first.append(copy(a, 1 + j, me, (*chip, c), src=ins[a]))
        for cp in first:
            cp.start()
        passed = []
        for j, chip in enumerate(chips):
            for a in range(n):
                copy(a, 1 + j, (*chip, c), me).wait_recv()
                fwd = copy(a, 4 + j, (*chip, c), sib)
                fwd.start()
                passed.append(fwd)
        for a in range(n):
            copy(a, 0, sib, me).wait_recv()
            for j, chip in enumerate(chips):
                copy(a, 4 + j, (*chip, 1 - c), me).wait_recv()
        for cp in first + passed:
            cp.wait_send()
        for cp in mine:
            cp.wait()

    return pl.pallas_call(
        body, name="all_gather_weights",
        in_specs=[ANY] * n, out_specs=[ANY] * n,
        out_shape=[jax.ShapeDtypeStruct((N_DEV,) + s.shape, s.dtype) for s in shards],
        scratch_shapes=[pltpu.SemaphoreType.DMA((n, 7)), pltpu.SemaphoreType.DMA((n, 7)),
                        pltpu.SemaphoreType.DMA((n,))],
    )(*shards)


def _reduce_scatter_cores(grads, name):
    n = len(grads)

    def body(*refs):
        ins, gots = refs[:n], refs[n:2 * n]
        send_sems, recv_sems = refs[2 * n:]
        x, y, c, _ = _mesh_place()
        sib = (x, y, 1 - c)
        remote = []
        for a in range(n):
            for k in range(4):
                remote.append(pltpu.make_async_remote_copy(
                    src_ref=ins[a].at[2 * k + (1 - c)], dst_ref=gots[a].at[k],
                    send_sem=send_sems.at[a, k], recv_sem=recv_sems.at[a, k],
                    device_id=sib, device_id_type=MESH_ID))
        for cp in remote:
            cp.start()
        for cp in remote:
            cp.wait_recv()
        for cp in remote:
            cp.wait_send()

    return pl.pallas_call(
        body, name=name,
        in_specs=[ANY] * n, out_specs=[ANY] * n,
        out_shape=[jax.ShapeDtypeStruct((4,) + g.shape[1:], g.dtype) for g in grads],
        scratch_shapes=[pltpu.SemaphoreType.DMA((n, 4)), pltpu.SemaphoreType.DMA((n, 4))],
    )(*grads)


def _chip_partial_sum(blocks, got, core):
    R, C = got.shape[1:]
    tr = min(256, R)
    assert R % tr == 0

    def body(core_ref, a_ref, b_ref, s_ref, sb_ref):
        s = a_ref[...] + b_ref[...]
        s_ref[...] = s
        sb_ref[...] = s.astype(BF16)

    blk = pl.BlockSpec((None, tr, C), lambda k, i, core_ref: (k, i, 0))
    return pl.pallas_call(
        body, name="chip_partial_sum",
        grid_spec=pltpu.PrefetchScalarGridSpec(
            num_scalar_prefetch=1, grid=(4, R // tr),
            in_specs=[pl.BlockSpec((None, tr, C), lambda k, i, core_ref: (2 * k + core_ref[0], i, 0)), blk],
            out_specs=[blk, blk]),
        out_shape=[jax.ShapeDtypeStruct(got.shape, F32), jax.ShapeDtypeStruct(got.shape, BF16)],
        compiler_params=_cparams("parallel", "parallel"),
    )(core, blocks, got)


HBM_SPEC = pl.BlockSpec(memory_space=pltpu.HBM)
SEM_SPEC = pl.BlockSpec(memory_space=pltpu.SEMAPHORE)
FLIPS = [(dx, dy, dc) for dx in (0, 1) for dy in (0, 1) for dc in (0, 1) if (dx, dy, dc) != (0, 0, 0)]


def _flip(v, d):
    return 1 - v if d else v


def _gather_copies(srcs, lands, send_sems, recv_sems):
    x, y, c, _ = _mesh_place()
    sends, recvs = [], []
    for a in range(len(srcs)):
        for k, (dx, dy, dc) in enumerate(FLIPS):
            px, py, pc = _flip(x, dx), _flip(y, dy), _flip(c, dc)
            sem = len(FLIPS) * a + k
            common = dict(send_sem=send_sems.at[sem], recv_sem=recv_sems.at[sem],
                          device_id=(px, py, pc), device_id_type=MESH_ID)
            sends.append(pltpu.make_async_remote_copy(
                src_ref=srcs[a], dst_ref=lands[a].at[4 * x + 2 * y + c], **common))
            recvs.append(pltpu.make_async_remote_copy(
                src_ref=srcs[a], dst_ref=lands[a].at[4 * px + 2 * py + pc], **common))
    return sends, recvs


def _scatter_copies(srcs, lands, send_sems, recv_sems):
    x, y, c, chips = _mesh_place()
    sends = []
    for a in range(len(srcs)):
        for j, (px, py) in enumerate(chips):
            sends.append(pltpu.make_async_remote_copy(
                src_ref=srcs[a].at[2 * px + py], dst_ref=lands[a].at[j],
                send_sem=send_sems.at[3 * a + j], recv_sem=recv_sems.at[3 * a + j],
                device_id=(px, py, c), device_id_type=MESH_ID))
    return sends, sends


def _exchange_start(srcs, land_shapes, copies, n_copies, name):
    n = len(srcs)

    def body(*refs):
        src_refs, land_refs = refs[:n], refs[n:2 * n]
        send_sems, recv_sems = refs[2 * n], refs[2 * n + 1]
        token = refs[-1]
        sends, _ = copies(src_refs, land_refs, send_sems, recv_sems)
        for cp in sends:
            cp.start()
        token[...] = jnp.zeros_like(token)

    lands = [pltpu.with_memory_space_constraint(lax.empty(s.shape, s.dtype), pltpu.HBM) for s in land_shapes]
    srcs = [pltpu.with_memory_space_constraint(s, pltpu.HBM) for s in srcs]
    res = pl.pallas_call(
        body, name=name,
        out_shape=(pltpu.SemaphoreType.DMA((n * n_copies,)), pltpu.SemaphoreType.DMA((n * n_copies,)),
                   *[pltpu.HBM(s.shape, s.dtype) for s in srcs], *[pltpu.HBM(s.shape, s.dtype) for s in land_shapes],
                   jax.ShapeDtypeStruct((SUBLANES, LANES), F32)),
        in_specs=[HBM_SPEC] * (2 * n),
        out_specs=(SEM_SPEC, SEM_SPEC, *[HBM_SPEC] * (2 * n), pl.BlockSpec(memory_space=pltpu.VMEM)),
        input_output_aliases={i: 2 + i for i in range(2 * n)},
        compiler_params=pltpu.CompilerParams(has_side_effects=pltpu.SideEffectType.DATAFLOW_SIDE_EFFECTING),
    )(*srcs, *lands)
    return res[0], res[1], list(res[2:2 + n]), list(res[2 + n:2 + 2 * n]), res[-1]


def _exchange_wait(started, copies, after, name):
    send_sems, recv_sems, srcs, lands, _ = started
    n = len(srcs)

    def body(*refs):
        src_refs, land_refs = refs[:n], refs[n:2 * n]
        send_ref, recv_ref = refs[2 * n], refs[2 * n + 1]
        sends, recvs = copies(src_refs, land_refs, send_ref, recv_ref)
        for cp in sends:
            cp.wait_send()
        for cp in recvs:
            cp.wait_recv()

    res = pl.pallas_call(
        body, name=name,
        out_shape=tuple(pltpu.HBM(s.shape, s.dtype) for s in srcs + lands),
        in_specs=[HBM_SPEC] * (2 * n) + [SEM_SPEC, SEM_SPEC, ANY],
        out_specs=tuple([HBM_SPEC] * (2 * n)),
        input_output_aliases={i: i for i in range(2 * n)},
        compiler_params=pltpu.CompilerParams(has_side_effects=pltpu.SideEffectType.DATAFLOW_SIDE_EFFECTING),
    )(*srcs, *lands, send_sems, recv_sems, after)
    return list(res[:n]), list(res[n:])


def _all_reduce_small(vec):
    R = vec.shape[0]

    def body(v_ref, o_ref, sib_buf, chip_buf, send_sems, recv_sems):
        x, y, c, chips = _mesh_place()
        swap = pltpu.make_async_remote_copy(
            src_ref=v_ref, dst_ref=sib_buf, send_sem=send_sems.at[0], recv_sem=recv_sems.at[0],
            device_id=(x, y, 1 - c), device_id_type=MESH_ID)
        swap.start()
        swap.wait()
        my_chip = 2 * x + y
        chip_buf[my_chip] = v_ref[...] + sib_buf[...]
        sends = []
        for j, (px, py) in enumerate(chips):
            cp = pltpu.make_async_remote_copy(
                src_ref=chip_buf.at[my_chip], dst_ref=chip_buf.at[my_chip],
                send_sem=send_sems.at[1 + j], recv_sem=recv_sems.at[1 + j],
                device_id=(px, py, c), device_id_type=MESH_ID)
            cp.start()
            sends.append(cp)
        for j, (px, py) in enumerate(chips):
            pltpu.make_async_remote_copy(
                src_ref=chip_buf.at[2 * px + py], dst_ref=chip_buf.at[2 * px + py],
                send_sem=send_sems.at[1 + j], recv_sem=recv_sems.at[1 + j],
                device_id=(px, py, c), device_id_type=MESH_ID).wait_recv()
        for cp in sends:
            cp.wait_send()
        o_ref[...] = ((chip_buf[0] + chip_buf[1]) + chip_buf[2]) + chip_buf[3]

    vm = pl.BlockSpec(memory_space=pltpu.VMEM)
    return pl.pallas_call(
        body, name="all_reduce_small",
        in_specs=[vm], out_specs=vm,
        out_shape=jax.ShapeDtypeStruct(vec.shape, F32),
        scratch_shapes=[pltpu.VMEM((R, LANES), F32), pltpu.VMEM((4, R, LANES), F32),
                        pltpu.SemaphoreType.DMA((4,)), pltpu.SemaphoreType.DMA((4,))],
    )(vec)


def _adamw_math(w, g, m, v):
    m = ADAM_B1 * m + (1.0 - ADAM_B1) * g
    v = ADAM_B2 * v + (1.0 - ADAM_B2) * (g * g)
    m_hat = m / (1.0 - ADAM_B1 ** ADAM_STEP)
    v_hat = v / (1.0 - ADAM_B2 ** ADAM_STEP)
    delta = -ADAM_LR * (m_hat / (jnp.sqrt(v_hat) + ADAM_EPS) + ADAM_WD * w)
    return delta, m, v


def _adamw(w, m, v, g_own, g_got, chip, name):
    R, C = w.shape
    tr = R if R * C <= 256 * D_MODEL else 256
    assert R % tr == 0
    n_got = g_got.shape[0]

    def body(*refs):
        w_ref, m_ref, v_ref, go_ref = refs[1:5]
        got = refs[5:5 + n_got]
        g_ref, d_ref, nm_ref, nv_ref = refs[5 + n_got:]
        g = go_ref[...]
        for r in got:
            g = g + r[...].astype(F32)
        delta, m_new, v_new = _adamw_math(w_ref[...], g, m_ref[...], v_ref[...])
        g_ref[...] = g
        d_ref[...] = delta
        nm_ref[...] = m_new
        nv_ref[...] = v_new

    blk = pl.BlockSpec((tr, C), lambda i, chip_ref: (i, 0))
    own_spec = pl.BlockSpec((None, tr, C), lambda i, chip_ref: (chip_ref[0], i, 0))
    got_specs = [pl.BlockSpec((None, tr, C), lambda i, chip_ref, j=j: (j, i, 0)) for j in range(n_got)]
    return pl.pallas_call(
        body, name=name,
        grid_spec=pltpu.PrefetchScalarGridSpec(
            num_scalar_prefetch=1, grid=(R // tr,),
            in_specs=[blk] * 3 + [own_spec] + got_specs, out_specs=[blk] * 4),
        out_shape=[jax.ShapeDtypeStruct((R, C), F32)] * 4,
        compiler_params=_cparams("parallel"),
    )(chip, w, m, v, g_own, *([g_got] * n_got))


def _block_diag_pairs(wa, wx):
    def pairs(w):
        w = w.reshape(N_GROUPS, 2, LRU_BW, LRU_BW)
        z = jnp.zeros((N_GROUPS, LRU_BW, LRU_BW), w.dtype)
        top = jnp.concatenate([w[:, 0], z], axis=2)
        bot = jnp.concatenate([z, w[:, 1]], axis=2)
        return jnp.concatenate([top, bot], axis=1)
    return jnp.concatenate([pairs(wa), pairs(wx)], axis=2).astype(BF16)


def _block_diag_unpair(dbd):
    def unpair(g):
        blocks = jnp.stack([g[:, :LRU_BW, :LRU_BW], g[:, LRU_BW:, LRU_BW:]], axis=1)
        return blocks.reshape(LRU_BLOCKS, LRU_BW, LRU_BW)
    return unpair(dbd[:, :, :LANES]), unpair(dbd[:, :, LANES:])


def _local_step(x, target, W, small, late_weights=None, early_grads=None, in_grads=None):
    S, D = x.shape
    g1, g2, g3 = small["norm_mix_g"], small["norm_mlp_g"], small["norm_final_g"]
    cw, cb = small["conv_w"], small["conv_b"].reshape(1, D)
    ba, bx, lam = (small[k].reshape(1, D) for k in ("lru_ba", "lru_bx", "lru_lambda"))
    fb = jnp.pad(small["forget_b"], (0, LANES - N_HEADS)).reshape(1, LANES)
    bd = _block_diag_pairs(small["lru_wa"], small["lru_wx"])
    big = dict(tm=1024, tn=1024)

    u = _norm_fwd(x, g1, "norm_mix")
    (xg,) = _mm([(u, W["in_xg"])], tks=[D], outs=[F32], name="proj_xg", **big)
    (qkv_t,) = _mm([(W["in_qkv_t"], u)], tb=True, tks=[D], outs=[BF16], name="proj_qkv_t", **big)
    (kv,) = _mm([(u, W["in_kv"])], tks=[D], outs=[BF16], name="proj_kv", **big)
    (gates,) = _mm([(u, W["in_gates"])], tks=[D], outs=[F32], name="proj_gates", **big)
    (fl,) = _mm([(u, W["in_f"])], tks=[D], outs=[F32], name="proj_forget", **big)
    h, yain = _lru_fwd(xg, cw, cb, bd, ba, bx, lam)
    fcum, f_rep = _forget_cumsum(fl, fb)
    f_row = fcum[:, :N_HEADS].T.reshape(N_HEADS, 1, S)
    ob_t, lse = _attn_fwd(kv, qkv_t, f_row, f_rep)
    if late_weights is not None:
        W = {**W, **late_weights(lse)}
    (ya,) = _mm([(yain, W["branch_a"])], tks=[D], outs=[F32], name="branch_a", **big)
    (yb,) = _mm([(ob_t, W["branch_b"])], ta=True, tks=[D], outs=[F32], name="branch_b", **big)
    mix = _gate_mix(gates, ya, yb)
    (x1,) = _mm([(mix, W["out"])], tks=[D], outs=[F32], name="out_proj", extra=(x,),
                epi=lambda acc, res: (res + acc,), **big)
    m = _norm_fwd(x1, g2, "norm_mlp")
    relu, hh = _mm([(m, W["up"])], tks=[D], outs=[BF16, BF16], name="mlp_up",
                   epi=lambda acc: (jnp.maximum(acc, 0.0), jnp.square(jnp.maximum(acc, 0.0))), **big)
    deep = dict(tm=512, tn=1024, tks=[D_FF])
    wgrad = dict(tm=1024, tn=512, tks=[min(4096, S)])
    (x2,) = _mm([(hh, W["down"])], outs=[F32], name="mlp_down", extra=(x1,),
                epi=lambda acc, res: (res + acc,), **deep)
    loss_acc, dg3, dx2, dx2b = _final_norm_loss(x2, target, g3)

    (dhpre,) = _mm([(dx2b, W["down"])], tb=True, tks=[D], outs=[BF16], name="d_mlp_act", extra=(relu,),
                   epi=lambda acc, r: (acc * (2.0 * r.astype(F32)),), **big)
    (dw_down,) = _mm([(hh, dx2b)], ta=True, outs=[F32], name="dw_down", **wgrad)
    (dm,) = _mm([(dhpre, W["up"])], tb=True, outs=[F32], name="d_mlp_in", **deep)
    assert wgrad["tn"] == D_FF // N_DEV
    (dw_up,) = _mm([(m, dhpre)], ta=True, outs=[F32], name="dw_up", col_blocked=True, **wgrad)
    dx1, dx1b, dg2 = _norm_bwd(dm, x1, g2, dx2, "norm_mlp_bwd")
    (dmix,) = _mm([(dx1b, W["out"])], tb=True, tks=[D], outs=[F32], name="d_mix", **big)
    (dw_out,) = _mm([(mix, dx1b)], ta=True, outs=[F32], name="dw_out", **wgrad)
    dya, dyb, dgates = _gate_bwd(dmix, gates, ya, yb)
    (dob_t,) = _mm([(W["branch_b"], dyb)], tb=True, tks=[D], outs=[BF16], name="d_attn_out_t", **big)
    (dw_b,) = _mm([(ob_t, dyb)], outs=[F32], name="dw_branch_b", **wgrad)
    (dyain,) = _mm([(dya, W["branch_a"])], tb=True, tks=[D], outs=[F32], name="d_lru_out", **big)
    (dw_a,) = _mm([(yain, dya)], ta=True, outs=[F32], name="dw_branch_a", **wgrad)
    early = dict(w_branch_a=dw_a, w_branch_b=dw_b, w_out=dw_out, w_up=dw_up, w_down=dw_down)
    early_state, started = early_grads(early) if early_grads is not None else (None, 0.0)
    dq_t, dk_t, dv_t, dfk, dfq = _attn_bwd(kv, qkv_t, dob_t, ob_t, lse + started, f_row, f_rep)
    dF = jnp.pad((dfk.reshape(N_HEADS, S) + dfq.reshape(N_HEADS, S)).T, ((0, 0), (0, LANES - N_HEADS)))
    dfl, dfb = _forget_bwd(dF, fl, fb)
    dxg, dcw, dcb, dba, dbx, dlam, dbd = _lru_bwd(xg, h, dyain, cw, cb, bd, ba, bx, lam)
    dw_in_parts = [
        _mm([(u, dxg)], ta=True, outs=[F32], name="dw_in_xg", **wgrad)[0],
        _mm([(dq_t, u)], outs=[F32], name="dw_in_q_t", **wgrad)[0].T,
        _mm([(dk_t, u)], outs=[F32], name="dw_in_k_t", **wgrad)[0].T,
        _mm([(dv_t, u)], outs=[F32], name="dw_in_v_t", **wgrad)[0].T,
        _mm([(u, dgates)], ta=True, outs=[F32], name="dw_in_gates", **wgrad)[0],
        _mm([(u, dfl)], ta=True, outs=[F32], name="dw_in_forget", **wgrad)[0][:, :N_HEADS],
    ]
    dw_in = jnp.concatenate(dw_in_parts, axis=1)
    in_state, started = in_grads(dw_in) if in_grads is not None else (None, 0.0)
    wq_t, wk_t, wv_t = (W["in_qkv_t"][D * i:D * (i + 1)] for i in range(3))
    (du_tok,) = _mm([(dxg, W["in_xg"]), (dgates, W["in_gates"]), (dfl, W["in_f"] + jnp.asarray(started, BF16))],
                    tb=True, tks=[2 * D, 2 * D, LANES], outs=[F32], name="d_norm_mix_out_tok", tm=1024, tn=512)
    (du,) = _mm([(dq_t, wq_t), (dk_t, wk_t), (dv_t, wv_t)], ta=True, tks=[D, D, D], outs=[F32],
                name="d_norm_mix_out", extra=(du_tok,), epi=lambda acc, prev: (prev + acc,), tm=1024, tn=512)
    grad_x, _, dg1 = _norm_bwd(du, x, g1, dx1, "norm_mix_bwd")

    dwa, dwx = _block_diag_unpair(dbd)
    big_grads = dict(early, w_in=dw_in)
    small_grads = dict(norm_mix_g=dg1.reshape(D), conv_w=dcw, conv_b=dcb.reshape(D), lru_wa=dwa, lru_ba=dba.reshape(D),
                       lru_wx=dwx, lru_bx=dbx.reshape(D), lru_lambda=dlam.reshape(D), forget_b=dfb[0, :N_HEADS],
                       norm_mlp_g=dg2.reshape(D), norm_final_g=dg3.reshape(D))
    return loss_acc[0, 0], grad_x, big_grads, small_grads, (early_state, in_state)


SMALL_NAMES = ("norm_mix_g", "conv_b", "lru_wa", "lru_ba", "lru_wx", "lru_bx", "lru_lambda", "forget_b",
               "norm_mlp_g", "norm_final_g")
TILE_ELEMS = SUBLANES * LANES


def _pack_small(parts):
    rows = []
    for p in parts:
        flat = p.reshape(-1)
        flat = jnp.pad(flat, (0, (-flat.shape[0]) % TILE_ELEMS))
        rows.append(flat.reshape(-1, LANES))
    return jnp.concatenate(rows, axis=0)


def _packed_rows(shape):
    return -(-math.prod(shape) // TILE_ELEMS) * SUBLANES


def _adamw_small(g_packed, g_conv_w, weights, moms, vels):
    def rows_view(a):
        flat = a.reshape(-1)
        flat = jnp.pad(flat, (0, (-flat.shape[0]) % LANES))
        return flat.reshape(-1, LANES)

    names = SMALL_NAMES + ("conv_w",)
    views = [[rows_view(src[k]) for k in names] for src in (weights, moms, vels)]
    n = len(names)
    starts, r = [], 0
    for k in SMALL_NAMES:
        starts.append(r)
        r += _packed_rows(weights[k].shape)

    def body(*refs):
        gp_ref, gc_ref = refs[0], refs[1]
        w_refs, m_refs, v_refs = refs[2:2 + n], refs[2 + n:2 + 2 * n], refs[2 + 2 * n:2 + 3 * n]
        outs = refs[2 + 3 * n:]
        for i in range(n):
            rows = w_refs[i].shape[0]
            g = gc_ref[...] if i == n - 1 else gp_ref[starts[i]:starts[i] + rows, :]
            delta, m_new, v_new = _adamw_math(w_refs[i][...], g, m_refs[i][...], v_refs[i][...])
            for o_ref, val in zip(outs[4 * i:4 * i + 4], (g, delta, m_new, v_new)):
                o_ref[...] = val

    vm = pl.BlockSpec(memory_space=pltpu.VMEM)
    out_shape = [jax.ShapeDtypeStruct(v.shape, F32) for v in views[0] for _ in range(4)]
    res = pl.pallas_call(
        body, name="adamw_small",
        in_specs=[vm] * (2 + 3 * n), out_specs=[vm] * (4 * n), out_shape=out_shape,
    )(g_packed, g_conv_w, *views[0], *views[1], *views[2])
    dicts = ({}, {}, {}, {})
    for i, k in enumerate(names):
        size = math.prod(weights[k].shape)
        for d, arr in zip(dicts, res[4 * i:4 * i + 4]):
            d[k] = arr.reshape(-1)[:size].reshape(weights[k].shape)
    return dicts


BIG_NAMES = ("w_in", "w_branch_a", "w_branch_b", "w_out", "w_up", "w_down")
WEIGHT_ORDER = ("norm_mix_g", "w_in", "conv_w", "conv_b", "lru_wa", "lru_ba", "lru_wx", "lru_bx", "lru_lambda",
                "forget_b", "w_branch_a", "w_branch_b", "w_out", "norm_mlp_g", "w_up", "w_down", "norm_final_g")


def _to_dest_blocks(name, g):
    if g.ndim == 3:
        return g
    if name in ("w_in", "w_up"):
        return g.reshape(g.shape[0], N_DEV, g.shape[1] // N_DEV).transpose(1, 0, 2)
    return g.reshape(N_DEV, g.shape[0] // N_DEV, g.shape[1])


def kernel(x, norm_mix_g, w_in, conv_w, conv_b, lru_wa, lru_ba, lru_wx, lru_bx, lru_lambda, forget_b, w_branch_a, w_branch_b, w_out, norm_mlp_g, w_up, w_down, norm_final_g, loss_target, m_norm_mix_g, m_w_in, m_conv_w, m_conv_b, m_lru_wa, m_lru_ba, m_lru_wx, m_lru_bx, m_lru_lambda, m_forget_b, m_w_branch_a, m_w_branch_b, m_w_out, m_norm_mlp_g, m_w_up, m_w_down, m_norm_final_g, v_norm_mix_g, v_w_in, v_conv_w, v_conv_b, v_lru_wa, v_lru_ba, v_lru_wx, v_lru_bx, v_lru_lambda, v_forget_b, v_w_branch_a, v_w_branch_b, v_w_out, v_norm_mlp_g, v_w_up, v_w_down, v_norm_final_g):
    weights = dict(norm_mix_g=norm_mix_g, w_in=w_in, conv_w=conv_w, conv_b=conv_b, lru_wa=lru_wa, lru_ba=lru_ba,
                   lru_wx=lru_wx, lru_bx=lru_bx, lru_lambda=lru_lambda, forget_b=forget_b, w_branch_a=w_branch_a,
                   w_branch_b=w_branch_b, w_out=w_out, norm_mlp_g=norm_mlp_g, w_up=w_up, w_down=w_down,
                   norm_final_g=norm_final_g)
    moms = dict(norm_mix_g=m_norm_mix_g, w_in=m_w_in, conv_w=m_conv_w, conv_b=m_conv_b, lru_wa=m_lru_wa,
                lru_ba=m_lru_ba, lru_wx=m_lru_wx, lru_bx=m_lru_bx, lru_lambda=m_lru_lambda, forget_b=m_forget_b,
                w_branch_a=m_w_branch_a, w_branch_b=m_w_branch_b, w_out=m_w_out, norm_mlp_g=m_norm_mlp_g,
                w_up=m_w_up, w_down=m_w_down, norm_final_g=m_norm_final_g)
    vels = dict(norm_mix_g=v_norm_mix_g, w_in=v_w_in, conv_w=v_conv_w, conv_b=v_conv_b, lru_wa=v_lru_wa,
                lru_ba=v_lru_ba, lru_wx=v_lru_wx, lru_bx=v_lru_bx, lru_lambda=v_lru_lambda, forget_b=v_forget_b,
                w_branch_a=v_w_branch_a, w_branch_b=v_w_branch_b, w_out=v_w_out, norm_mlp_g=v_norm_mlp_g,
                w_up=v_w_up, w_down=v_w_down, norm_final_g=v_norm_final_g)
    S, D = x.shape[1], x.shape[2]
    me = 4 * lax.axis_index("x") + 2 * lax.axis_index("y") + lax.axis_index("c")

    core = lax.axis_index("c").astype(jnp.int32).reshape(1)
    chip = (2 * lax.axis_index("x") + lax.axis_index("y")).astype(jnp.int32).reshape(1)
    late_names = BIG_NAMES[1:]

    win_g, cw_g = _all_gather([w_in.astype(BF16), conv_w])
    late_shards = [weights[k].astype(BF16) for k in late_names]
    gather = _exchange_start(late_shards, [jax.ShapeDtypeStruct((N_DEV,) + s.shape, BF16) for s in late_shards],
                             _gather_copies, len(FLIPS), "gather_late_start")
    w_in_full = win_g.transpose(1, 0, 2).reshape(D, -1)
    cuts = (0, 2 * D, 5 * D, 7 * D)
    W = dict(in_xg=w_in_full[:, cuts[0]:cuts[1]], in_qkv_t=w_in_full[:, cuts[1]:cuts[2]].T,
             in_kv=w_in_full[:, cuts[1] + D:cuts[2]], in_gates=w_in_full[:, cuts[2]:cuts[3]],
             in_f=jnp.pad(w_in_full[:, cuts[3]:], ((0, 0), (0, LANES - N_HEADS))))
    small = {k: weights[k] for k in SMALL_NAMES}
    small["conv_w"] = cw_g.transpose(1, 0, 2).reshape(CONV_W, D)
    small["norm_mix_g"] = norm_mix_g + gather[4][0, 0]

    def late_weights(after):
        shards, lands = _exchange_wait(gather, _gather_copies, after, "gather_late_wait")
        wa_g, wb_g, wo_g, wup_g, wdn_g = (
            lax.dynamic_update_slice_in_dim(land, shard[None], me, axis=0) for land, shard in zip(lands, shards))
        return dict(branch_a=wa_g.reshape(D, D), branch_b=wb_g.reshape(D, D), out=wo_g.reshape(D, D),
                    up=wup_g.transpose(1, 0, 2).reshape(D, D_FF), down=wdn_g.reshape(D_FF, D))

    def core_stage(names, grads_by_name, tag):
        blocks = [_to_dest_blocks(k, grads_by_name[k]) for k in names]
        got = _reduce_scatter_cores(blocks, "reduce_scatter_cores_" + tag)
        return [_chip_partial_sum(b, g, core) for b, g in zip(blocks, got)]

    def reduce_behind(names, grads_by_name, tag):
        sums = core_stage(names, grads_by_name, tag)
        wire = [s[1] for s in sums]
        scatter = _exchange_start(wire, [jax.ShapeDtypeStruct((3,) + s.shape[1:], BF16) for s in wire],
                                  _scatter_copies, 3, "scatter_" + tag + "_start")
        return (sums, scatter), scatter[4][0, 0]

    loss_part, grad_x, _, small_grads, ((early_sums, early_scatter), (in_sums, in_scatter)) = _local_step(
        x.reshape(S, D), loss_target.reshape(S, D), W, small, late_weights,
        lambda g: reduce_behind(late_names, g, "early"), lambda g: reduce_behind(BIG_NAMES[:1], dict(w_in=g), "w_in"))
    loss = lax.psum(loss_part, MESH_AXES)
    _, early_others = _exchange_wait(early_scatter, _scatter_copies, grad_x, "scatter_early_wait")
    _, in_others = _exchange_wait(in_scatter, _scatter_copies, grad_x, "scatter_w_in_wait")
    sums = list(in_sums) + list(early_sums)
    others = list(in_others) + list(early_others)

    reduced = _all_reduce_small(_pack_small([small_grads[k] for k in SMALL_NAMES] + [small_grads["conv_w"]]))
    cw_full = reduced[reduced.shape[0] - _packed_rows((CONV_W, D)):].reshape(CONV_W, D)
    cw_cols = lax.dynamic_slice_in_dim(cw_full, me * (D // N_DEV), D // N_DEV, axis=1)

    grads, deltas, new_m, new_v = _adamw_small(reduced, cw_cols, weights, moms, vels)
    for k, s, g_got in zip(BIG_NAMES, sums, others):
        grads[k], deltas[k], new_m[k], new_v[k] = _adamw(weights[k], moms[k], vels[k], s[0], g_got, chip, "adamw_" + k)

    return (loss, grad_x.reshape(1, S, D), *[grads[k] for k in WEIGHT_ORDER], *[deltas[k] for k in WEIGHT_ORDER],
            *[new_m[k] for k in WEIGHT_ORDER], *[new_v[k] for k in WEIGHT_ORDER])
```

```python
import functools
import math

import jax
import jax.numpy as jnp
from jax import lax
from jax.experimental import pallas as pl
from jax.experimental.pallas import tpu as pltpu

F32 = jnp.float32
BF16 = jnp.bfloat16

D_MODEL = 1024
N_HEADS = 8
HEAD_DIM = 128
D_FF = 4096
LRU_BLOCKS = 16
LRU_BW = 64
LRU_C = 8.0
CONV_W = 4
RMS_EPS = 1e-6
N_DEV = 8
LANES = 128
SUBLANES = 8
N_GROUPS = D_MODEL // LANES
VMEM_LIMIT_BYTES = 52 * 1024 * 1024
ATTN_SCALE = 1.0 / math.sqrt(HEAD_DIM)
LOG2E = math.log2(math.e)
NEG_BIG = -1e30
ADAM_LR = 0.001
ADAM_B1 = 0.9
ADAM_B2 = 0.999
ADAM_EPS = 1e-08
ADAM_WD = 0.01
ADAM_STEP = 10
ATTN_BLOCK = 1024
ATTN_STRIP = 256
LRU_CHUNK = 256
ROW_TILE = 512
MESH_AXES = ("x", "y", "c")
MESH_ID = pl.DeviceIdType.MESH
ANY = pl.BlockSpec(memory_space=pl.ANY)

NT_DIMS = (((1,), (1,)), ((), ()))
TN_DIMS = (((0,), (0,)), ((), ()))
NN_DIMS = (((1,), (0,)), ((), ()))


def _cparams(*sem):
    return pltpu.CompilerParams(dimension_semantics=sem if sem else None, vmem_limit_bytes=VMEM_LIMIT_BYTES)


def _sigmoid(x):
    return 0.5 * (jnp.tanh(0.5 * x) + 1.0)


def _log1p_pos(e):
    u = 1.0 + e
    return jnp.where(u == 1.0, e, jnp.log(u) * (e / (u - 1.0)))


def _softplus(z):
    return jnp.maximum(z, 0.0) + _log1p_pos(jnp.exp(-jnp.abs(z)))


def _expm1_neg(x):
    series = x * (1.0 + x * 0.5 * (1.0 + x * (1.0 / 3.0) * (1.0 + x * 0.25)))
    return jnp.where(x > -0.03, series, jnp.exp(x) - 1.0)


GELU_C = math.sqrt(2.0 / math.pi)
GELU_K = 0.044715


def _gelu(x):
    return 0.5 * x * (1.0 + jnp.tanh(GELU_C * (x + GELU_K * (x * x * x))))


def _gelu_and_grad(x):
    t = jnp.tanh(GELU_C * (x + GELU_K * (x * x * x)))
    g = 0.5 * x * (1.0 + t)
    dg = 0.5 * (1.0 + t) + 0.5 * x * (1.0 - t * t) * (GELU_C * (1.0 + 3.0 * GELU_K * (x * x)))
    return g, dg


def _mm(pairs, *, ta=False, tb=False, tm, tn, tks, outs, name, epi=None, extra=(), col_blocked=False):
    n_pairs, n_extra, n_out = len(pairs), len(extra), len(outs)
    tas = list(ta) if isinstance(ta, (list, tuple)) else [ta] * n_pairs
    tbs = list(tb) if isinstance(tb, (list, tuple)) else [tb] * n_pairs
    a0, b0 = pairs[0]
    M = a0.shape[1] if tas[0] else a0.shape[0]
    N = b0.shape[0] if tbs[0] else b0.shape[1]
    tm, tn = min(tm, M), min(tn, N)
    nks, offs = [], []
    for (a, b), tk, pta in zip(pairs, tks, tas):
        K = a.shape[0] if pta else a.shape[1]
        assert K % tk == 0 and M % tm == 0 and N % tn == 0
        offs.append(sum(nks))
        nks.append(K // tk)
    nk_total = sum(nks)
    dims = [(((0 if pta else 1,), (1 if ptb else 0,)), ((), ())) for pta, ptb in zip(tas, tbs)]

    def kmap(off, nk):
        return lambda k: jnp.clip(k - off, 0, nk - 1)

    in_specs, operands = [], []
    for (a, b), tk, off, nk, pta, ptb in zip(pairs, tks, offs, nks, tas, tbs):
        km = kmap(off, nk)
        if pta:
            in_specs.append(pl.BlockSpec((tk, tm), lambda i, j, k, km=km: (km(k), i)))
        else:
            in_specs.append(pl.BlockSpec((tm, tk), lambda i, j, k, km=km: (i, km(k))))
        if ptb:
            in_specs.append(pl.BlockSpec((tn, tk), lambda i, j, k, km=km: (j, km(k))))
        else:
            in_specs.append(pl.BlockSpec((tk, tn), lambda i, j, k, km=km: (km(k), j)))
        operands += [a, b]
    for e in extra:
        in_specs.append(pl.BlockSpec((tm, tn), lambda i, j, k: (i, j)))
        operands.append(e)

    def body(*refs):
        ab = refs[:2 * n_pairs]
        ex = refs[2 * n_pairs:2 * n_pairs + n_extra]
        o = refs[2 * n_pairs + n_extra:2 * n_pairs + n_extra + n_out]
        k = pl.program_id(2)

        def finish(acc):
            res = epi(acc, *[e[...] for e in ex]) if epi is not None else (acc,)
            for r, oref in zip(res, o):
                oref[...] = r.astype(oref.dtype)

        if nk_total == 1:
            finish(lax.dot_general(ab[0][...], ab[1][...], dims[0], preferred_element_type=F32))
            return
        acc = refs[-1]
        for p in range(n_pairs):
            a_ref, b_ref = ab[2 * p], ab[2 * p + 1]

            @pl.when((k >= offs[p]) & (k < offs[p] + nks[p]))
            def _(a_ref=a_ref, b_ref=b_ref, pdims=dims[p]):
                prod = lax.dot_general(a_ref[...], b_ref[...], pdims, preferred_element_type=F32)

                @pl.when(k == 0)
                def _():
                    acc[...] = prod

                @pl.when(k > 0)
                def _():
                    acc[...] += prod

        @pl.when(k == nk_total - 1)
        def _():
            finish(acc[...])

    return pl.pallas_call(
        body,
        name=name,
        grid=(M // tm, N // tn, nk_total),
        in_specs=in_specs,
        out_specs=[pl.BlockSpec((None, tm, tn), lambda i, j, k: (j, i, 0)) if col_blocked
                   else pl.BlockSpec((tm, tn), lambda i, j, k: (i, j)) for _ in outs],
        out_shape=[jax.ShapeDtypeStruct((N // tn, M, tn) if col_blocked else (M, N), dt) for dt in outs],
        scratch_shapes=[] if nk_total == 1 else [pltpu.VMEM((tm, tn), F32)],
        compiler_params=_cparams("parallel", "parallel", "arbitrary"),
    )(*operands)


def _norm_fwd(x, g, name):
    S, D = x.shape
    tr = min(ROW_TILE, S)

    def body(x_ref, g_ref, o_ref):
        xv = x_ref[...]
        r = lax.rsqrt(jnp.mean(xv * xv, axis=-1, keepdims=True) + RMS_EPS)
        o_ref[...] = ((xv * r) * g_ref[...]).astype(o_ref.dtype)

    return pl.pallas_call(
        body, name=name, grid=(S // tr,),
        in_specs=[pl.BlockSpec((tr, D), lambda i: (i, 0)), pl.BlockSpec((1, D), lambda i: (0, 0))],
        out_specs=pl.BlockSpec((tr, D), lambda i: (i, 0)),
        out_shape=jax.ShapeDtypeStruct((S, D), BF16),
        compiler_params=_cparams("parallel"),
    )(x, g.reshape(1, D))


def _rms_bwd_rows(dy, xv, g):
    r = lax.rsqrt(jnp.mean(xv * xv, axis=-1, keepdims=True) + RMS_EPS)
    xn = xv * r
    dxn = dy * g
    dx = r * (dxn - xn * jnp.mean(dxn * xn, axis=-1, keepdims=True))
    dg = jnp.sum(dy * xn, axis=0, keepdims=True)
    return dx, dg


def _norm_bwd(dy, x, g, dres, name):
    S, D = x.shape
    tr = min(ROW_TILE, S)

    def body(dy_ref, x_ref, g_ref, dres_ref, dx_ref, dxb_ref, dg_ref):
        dx, dg = _rms_bwd_rows(dy_ref[...], x_ref[...], g_ref[...])
        dx = dres_ref[...] + dx
        dx_ref[...] = dx
        dxb_ref[...] = dx.astype(BF16)

        @pl.when(pl.program_id(0) == 0)
        def _():
            dg_ref[...] = jnp.zeros_like(dg_ref)

        dg_ref[...] += dg

    row = pl.BlockSpec((tr, D), lambda i: (i, 0))
    vec = pl.BlockSpec((1, D), lambda i: (0, 0))
    return pl.pallas_call(
        body, name=name, grid=(S // tr,),
        in_specs=[row, row, vec, row],
        out_specs=[row, row, vec],
        out_shape=[jax.ShapeDtypeStruct((S, D), F32), jax.ShapeDtypeStruct((S, D), BF16),
                   jax.ShapeDtypeStruct((1, D), F32)],
        compiler_params=_cparams("arbitrary"),
    )(dy, x, g.reshape(1, D), dres)


def _final_norm_loss(x2, target, g):
    S, D = x2.shape
    tr = min(ROW_TILE, S)

    def body(x_ref, t_ref, g_ref, loss_ref, dg_ref, dx_ref, dxb_ref):
        xv = x_ref[...]
        gv = g_ref[...]
        r = lax.rsqrt(jnp.mean(xv * xv, axis=-1, keepdims=True) + RMS_EPS)
        y = (xv * r) * gv
        err = y - t_ref[...]
        part = 0.5 * jnp.sum(jnp.mean(err * err, axis=-1, keepdims=True), axis=0, keepdims=True)
        dy = err * (1.0 / D)
        dx, dg = _rms_bwd_rows(dy, xv, gv)
        dx_ref[...] = dx
        dxb_ref[...] = dx.astype(BF16)

        @pl.when(pl.program_id(0) == 0)
        def _():
            dg_ref[...] = jnp.zeros_like(dg_ref)
            loss_ref[...] = jnp.zeros_like(loss_ref)

        dg_ref[...] += dg
        loss_ref[...] += jnp.broadcast_to(part, loss_ref.shape)

    row = pl.BlockSpec((tr, D), lambda i: (i, 0))
    vec = pl.BlockSpec((1, D), lambda i: (0, 0))
    return pl.pallas_call(
        body, name="final_norm_loss", grid=(S // tr,),
        in_specs=[row, row, vec],
        out_specs=[pl.BlockSpec((SUBLANES, LANES), lambda i: (0, 0)), vec, row, row],
        out_shape=[jax.ShapeDtypeStruct((SUBLANES, LANES), F32), jax.ShapeDtypeStruct((1, D), F32),
                   jax.ShapeDtypeStruct((S, D), F32), jax.ShapeDtypeStruct((S, D), BF16)],
        compiler_params=_cparams("arbitrary"),
    )(x2, target, g.reshape(1, D))


def _lru_gates(xa, bd_j, ba_j, bx_j, sp_j):
    z = jnp.dot(xa.astype(BF16), bd_j, preferred_element_type=F32)
    r = _sigmoid(z[:, :LANES] + ba_j)
    ig = _sigmoid(z[:, LANES:] + bx_j)
    log_a = (-LRU_C) * r * sp_j
    a = jnp.exp(log_a)
    mult = jnp.sqrt(-_expm1_neg(2.0 * log_a))
    return r, ig, a, mult


def _conv_rows(xpad, cw_ref, cb_ref, sl, tc):
    out = jnp.broadcast_to(cb_ref[:, sl], (tc, LANES))
    for k in range(CONV_W):
        out = out + xpad[pl.ds(SUBLANES - (CONV_W - 1) + k, tc), sl] * cw_ref[k:k + 1, sl]
    return out


def _lru_fwd(xg, cw, cb, bd, ba, bx, lam):
    S = xg.shape[0]
    D = D_MODEL
    tc = min(LRU_CHUNK, S)
    hb = tc // SUBLANES

    def body(xl_ref, halo_ref, g_ref, cw_ref, cb_ref, bd_ref, ba_ref, bx_ref, lam_ref,
             h_ref, y_ref, xpad, a_s, b_s, carry):
        i = pl.program_id(0)

        @pl.when(i == 0)
        def _():
            carry[...] = jnp.zeros_like(carry)

        xpad[0:SUBLANES, :] = jnp.where(i > 0, halo_ref[...], 0.0)
        xpad[SUBLANES:, :] = xl_ref[...]
        for j in range(N_GROUPS):
            sl = slice(LANES * j, LANES * (j + 1))
            xa = _conv_rows(xpad, cw_ref, cb_ref, sl, tc)
            sp = _softplus(-lam_ref[:, sl])
            _, ig, a, mult = _lru_gates(xa, bd_ref[j], ba_ref[:, sl], bx_ref[:, sl], sp)
            a_s[:, sl] = a
            b_s[:, sl] = mult * (ig * xa)

        row = lax.broadcasted_iota(jnp.int32, (SUBLANES, D), 0)

        def step(t, c):
            o = pl.multiple_of(t * SUBLANES, SUBLANES)
            A = a_s[pl.ds(o, SUBLANES), :]
            B = b_s[pl.ds(o, SUBLANES), :]
            for d in (1, 2, 4):
                keep = row >= d
                a_sh = jnp.where(keep, pltpu.roll(A, d, 0), 1.0)
                b_sh = jnp.where(keep, pltpu.roll(B, d, 0), 0.0)
                B = A * b_sh + B
                A = A * a_sh
            hh = A * c + B
            h_ref[pl.ds(o, SUBLANES), :] = hh
            return jnp.broadcast_to(hh[SUBLANES - 1:SUBLANES, :], (SUBLANES, D))

        carry[...] = lax.fori_loop(0, hb, step, carry[...])
        y_ref[...] = (_gelu(g_ref[...]) * h_ref[...]).astype(BF16)

    row_spec = lambda col: pl.BlockSpec((tc, D), lambda i, col=col: (i, col))
    halo = pl.BlockSpec((SUBLANES, D), lambda i: (jnp.maximum(i * hb - 1, 0), 0))
    full = lambda shape: pl.BlockSpec(shape, lambda i: tuple(0 for _ in shape))
    return pl.pallas_call(
        body, name="lru_fwd", grid=(S // tc,),
        in_specs=[row_spec(0), halo, row_spec(1), full((CONV_W, D)), full((1, D)),
                  full((N_GROUPS, LANES, 2 * LANES)), full((1, D)), full((1, D)), full((1, D))],
        out_specs=[pl.BlockSpec((tc, D), lambda i: (i, 0)), pl.BlockSpec((tc, D), lambda i: (i, 0))],
        out_shape=[jax.ShapeDtypeStruct((S, D), F32), jax.ShapeDtypeStruct((S, D), BF16)],
        scratch_shapes=[pltpu.VMEM((tc + SUBLANES, D), F32), pltpu.VMEM((tc, D), F32),
                        pltpu.VMEM((tc, D), F32), pltpu.VMEM((SUBLANES, D), F32)],
        compiler_params=_cparams("arbitrary"),
    )(xg, xg, xg, cw, cb, bd, ba, bx, lam)


def _lru_bwd(xg, h, dyain, cw, cb, bd, ba, bx, lam):
    S = xg.shape[0]
    D = D_MODEL
    tc = min(LRU_CHUNK, S)
    hb = tc // SUBLANES
    nc = S // tc

    def body(xl_ref, xhalo_ref, g_ref, h_ref, hhalo_ref, dy_ref, cw_ref, cb_ref, bd_ref, ba_ref, bx_ref,
             lam_ref, dxg_ref, dcw_ref, dcb_ref, dba_ref, dbx_ref, dlam_ref, dbd_ref,
             xpad, hpad, a_s, b_s, dh_s, g_s, xa_s, r_s, ig_s, m_s, dxa_pad, carry_e, dxa_head):
        i = pl.program_id(0)
        c = nc - 1 - i

        @pl.when(i == 0)
        def _():
            carry_e[...] = jnp.zeros_like(carry_e)
            dxa_head[...] = jnp.zeros_like(dxa_head)
            for ref in (dcw_ref, dcb_ref, dba_ref, dbx_ref, dlam_ref, dbd_ref):
                ref[...] = jnp.zeros_like(ref)

        xpad[0:SUBLANES, :] = jnp.where(c > 0, xhalo_ref[...], 0.0)
        xpad[SUBLANES:, :] = xl_ref[...]
        hpad[0:SUBLANES, :] = jnp.where(c > 0, hhalo_ref[...], 0.0)
        hpad[SUBLANES:, :] = h_ref[...]

        for j in range(N_GROUPS):
            sl = slice(LANES * j, LANES * (j + 1))
            xa = _conv_rows(xpad, cw_ref, cb_ref, sl, tc)
            sp = _softplus(-lam_ref[:, sl])
            r, ig, a, mult = _lru_gates(xa, bd_ref[j], ba_ref[:, sl], bx_ref[:, sl], sp)
            gl, dgl = _gelu_and_grad(g_ref[:, sl])
            dy = dy_ref[:, sl]
            dh = dy * gl
            dxg_ref[:, D + LANES * j:D + LANES * (j + 1)] = (dy * h_ref[:, sl] * dgl).astype(BF16)
            a_s[:, sl] = a
            b_s[:, sl] = a * dh
            dh_s[:, sl] = dh
            xa_s[:, sl] = xa
            r_s[:, sl] = r
            ig_s[:, sl] = ig
            m_s[:, sl] = mult

        row = lax.broadcasted_iota(jnp.int32, (SUBLANES, D), 0)

        def step(tt, ce):
            o = pl.multiple_of((hb - 1 - tt) * SUBLANES, SUBLANES)
            A = a_s[pl.ds(o, SUBLANES), :]
            B = b_s[pl.ds(o, SUBLANES), :]
            for d in (1, 2, 4):
                keep = row < SUBLANES - d
                a_sh = jnp.where(keep, pltpu.roll(A, SUBLANES - d, 0), 1.0)
                b_sh = jnp.where(keep, pltpu.roll(B, SUBLANES - d, 0), 0.0)
                B = A * b_sh + B
                A = A * a_sh
            e = A * ce + B
            e_next = jnp.where(row < SUBLANES - 1, pltpu.roll(e, SUBLANES - 1, 0), ce)
            g_s[pl.ds(o, SUBLANES), :] = dh_s[pl.ds(o, SUBLANES), :] + e_next
            return jnp.broadcast_to(e[0:1, :], (SUBLANES, D))

        carry_e[...] = lax.fori_loop(0, hb, step, carry_e[...])

        for j in range(N_GROUPS):
            sl = slice(LANES * j, LANES * (j + 1))
            gg = g_s[:, sl]
            xa, r, ig, mult, a = xa_s[:, sl], r_s[:, sl], ig_s[:, sl], m_s[:, sl], a_s[:, sl]
            hprev = hpad[pl.ds(SUBLANES - 1, tc), sl]
            sp = _softplus(-lam_ref[:, sl])
            da = gg * hprev
            dmult = gg * (ig * xa)
            dig = gg * (mult * xa)
            dxa = gg * (mult * ig)
            dla = da * a - dmult * ((a * a) / mult)
            dr = dla * ((-LRU_C) * sp)
            dlam_ref[:, sl] += jnp.sum(dla * r, axis=0, keepdims=True)
            dza = dr * r * (1.0 - r)
            dzx = dig * ig * (1.0 - ig)
            dba_ref[:, sl] += jnp.sum(dza, axis=0, keepdims=True)
            dbx_ref[:, sl] += jnp.sum(dzx, axis=0, keepdims=True)
            dz = jnp.concatenate([dza, dzx], axis=1).astype(BF16)
            dbd_ref[j] += lax.dot_general(xa.astype(BF16), dz, TN_DIMS, preferred_element_type=F32)
            dxa = dxa + lax.dot_general(dz, bd_ref[j], NT_DIMS, preferred_element_type=F32)
            dxa_pad[0:tc, sl] = dxa

        dxa_pad[tc:, :] = dxa_head[...]
        dxa_head[...] = dxa_pad[0:SUBLANES, :]

        for j in range(N_GROUPS):
            sl = slice(LANES * j, LANES * (j + 1))
            dxa = dxa_pad[0:tc, sl]
            dxl = jnp.zeros((tc, LANES), F32)
            for k in range(CONV_W):
                dxl = dxl + dxa_pad[pl.ds(CONV_W - 1 - k, tc), sl] * cw_ref[k:k + 1, sl]
                dcw_ref[k:k + 1, sl] += jnp.sum(
                    dxa * xpad[pl.ds(SUBLANES - (CONV_W - 1) + k, tc), sl], axis=0, keepdims=True)
            dxg_ref[:, sl] = dxl.astype(BF16)
            dcb_ref[:, sl] += jnp.sum(dxa, axis=0, keepdims=True)

        @pl.when(i == nc - 1)
        def _():
            dlam_ref[...] = dlam_ref[...] * (LRU_C * _sigmoid(-lam_ref[...]))

    rev = lambda col: pl.BlockSpec((tc, D), lambda i, col=col: (nc - 1 - i, col))
    halo = pl.BlockSpec((SUBLANES, D), lambda i: (jnp.maximum((nc - 1 - i) * hb - 1, 0), 0))
    full = lambda shape: pl.BlockSpec(shape, lambda i: tuple(0 for _ in shape))
    big = lambda: pltpu.VMEM((tc, D), F32)
    return pl.pallas_call(
        body, name="lru_bwd", grid=(nc,),
        in_specs=[rev(0), halo, rev(1), rev(0), halo, rev(0), full((CONV_W, D)), full((1, D)),
                  full((N_GROUPS, LANES, 2 * LANES)), full((1, D)), full((1, D)), full((1, D))],
        out_specs=[pl.BlockSpec((tc, 2 * D), lambda i: (nc - 1 - i, 0)), full((CONV_W, D)), full((1, D)),
                   full((1, D)), full((1, D)), full((1, D)), full((N_GROUPS, LANES, 2 * LANES))],
        out_shape=[jax.ShapeDtypeStruct((S, 2 * D), BF16), jax.ShapeDtypeStruct((CONV_W, D), F32),
                   jax.ShapeDtypeStruct((1, D), F32), jax.ShapeDtypeStruct((1, D), F32),
                   jax.ShapeDtypeStruct((1, D), F32), jax.ShapeDtypeStruct((1, D), F32),
                   jax.ShapeDtypeStruct((N_GROUPS, LANES, 2 * LANES), F32)],
        scratch_shapes=[pltpu.VMEM((tc + SUBLANES, D), F32), pltpu.VMEM((tc + SUBLANES, D), F32),
                        big(), big(), big(), big(), big(), big(), big(), big(),
                        pltpu.VMEM((tc + SUBLANES, D), F32), pltpu.VMEM((SUBLANES, D), F32),
                        pltpu.VMEM((SUBLANES, D), F32)],
        compiler_params=_cparams("arbitrary"),
    )(xg, xg, xg, h, h, dyain, cw, cb, bd, ba, bx, lam)


def _forget_cumsum(fl, fb):
    S = fl.shape[0]
    tr = min(ROW_TILE, S)
    hb = tr // SUBLANES

    def body(fl_ref, fb_ref, o_ref, rep_ref, lf_s, carry):
        @pl.when(pl.program_id(0) == 0)
        def _():
            carry[...] = jnp.zeros_like(carry)

        lf_s[...] = -_softplus(-(fl_ref[...] + fb_ref[...]))
        row = lax.broadcasted_iota(jnp.int32, (SUBLANES, LANES), 0)

        def step(t, c):
            o = pl.multiple_of(t * SUBLANES, SUBLANES)
            B = lf_s[pl.ds(o, SUBLANES), :]
            for d in (1, 2, 4):
                B = B + jnp.where(row >= d, pltpu.roll(B, d, 0), 0.0)
            B = B + c
            o_ref[pl.ds(o, SUBLANES), :] = B * LOG2E
            return jnp.broadcast_to(B[SUBLANES - 1:SUBLANES, :], (SUBLANES, LANES))

        carry[...] = lax.fori_loop(0, hb, step, carry[...])
        for h in range(N_HEADS):
            rep_ref[h] = jnp.broadcast_to(o_ref[:, h:h + 1], (tr, LANES))

    return pl.pallas_call(
        body, name="forget_cumsum", grid=(S // tr,),
        in_specs=[pl.BlockSpec((tr, LANES), lambda i: (i, 0)), pl.BlockSpec((1, LANES), lambda i: (0, 0))],
        out_specs=[pl.BlockSpec((tr, LANES), lambda i: (i, 0)),
                   pl.BlockSpec((N_HEADS, tr, LANES), lambda i: (0, i, 0))],
        out_shape=[jax.ShapeDtypeStruct((S, LANES), F32), jax.ShapeDtypeStruct((N_HEADS, S, LANES), F32)],
        scratch_shapes=[pltpu.VMEM((tr, LANES), F32), pltpu.VMEM((SUBLANES, LANES), F32)],
        compiler_params=_cparams("arbitrary"),
    )(fl, fb)


def _forget_bwd(dF, fl, fb):
    S = fl.shape[0]
    tr = min(ROW_TILE, S)
    hb = tr // SUBLANES
    nc = S // tr

    def body(df_ref, fl_ref, fb_ref, o_ref, dfb_ref, carry):
        @pl.when(pl.program_id(0) == 0)
        def _():
            carry[...] = jnp.zeros_like(carry)
            dfb_ref[...] = jnp.zeros_like(dfb_ref)

        row = lax.broadcasted_iota(jnp.int32, (SUBLANES, LANES), 0)

        def step(tt, carried):
            c, acc = carried
            o = pl.multiple_of((hb - 1 - tt) * SUBLANES, SUBLANES)
            B = df_ref[pl.ds(o, SUBLANES), :]
            for d in (1, 2, 4):
                B = B + jnp.where(row < SUBLANES - d, pltpu.roll(B, SUBLANES - d, 0), 0.0)
            B = B + c
            z = fl_ref[pl.ds(o, SUBLANES), :] + fb_ref[...]
            dz = B * _sigmoid(-z)
            o_ref[pl.ds(o, SUBLANES), :] = dz.astype(BF16)
            return jnp.broadcast_to(B[0:1, :], (SUBLANES, LANES)), acc + dz

        c, acc = lax.fori_loop(0, hb, step, (carry[...], jnp.zeros((SUBLANES, LANES), F32)))
        carry[...] = c
        dfb_ref[...] += jnp.sum(acc, axis=0, keepdims=True)

    rev = pl.BlockSpec((tr, LANES), lambda i: (nc - 1 - i, 0))
    vec = pl.BlockSpec((1, LANES), lambda i: (0, 0))
    return pl.pallas_call(
        body, name="forget_bwd", grid=(nc,),
        in_specs=[rev, rev, vec],
        out_specs=[rev, vec],
        out_shape=[jax.ShapeDtypeStruct((S, LANES), BF16), jax.ShapeDtypeStruct((1, LANES), F32)],
        scratch_shapes=[pltpu.VMEM((SUBLANES, LANES), F32)],
        compiler_params=_cparams("arbitrary"),
    )(dF, fl, fb)


def _triangle(n, key_major):
    pairs = [(q, k) for q in range(n) for k in range(q + 1)]
    if key_major:
        pairs.sort(key=lambda qk: (qk[1], qk[0]))
    return (jnp.asarray([q for q, _ in pairs], jnp.int32), jnp.asarray([k for _, k in pairs], jnp.int32))


def _strip_scores(k_ref, qt_ref, fk_ref, j, strip, nkeys, diagonal):
    cols = slice(strip * j, strip * (j + 1))
    s = jnp.dot(k_ref[0:nkeys, :], qt_ref[:, cols], preferred_element_type=F32) * (ATTN_SCALE * LOG2E)
    fk = fk_ref[0:nkeys, :]
    s = s - jnp.concatenate([fk] * (strip // LANES), axis=1)
    keep = None
    if diagonal:
        keys = lax.broadcasted_iota(jnp.int32, (nkeys, strip), 0)
        queries = lax.broadcasted_iota(jnp.int32, (nkeys, strip), 1) + strip * j
        keep = keys <= queries
    return s, keep


def _attn_fwd(kv, qkv_t, f_row, f_rep):
    S = kv.shape[0]
    blk = min(ATTN_BLOCK, S)
    strip = min(ATTN_STRIP, blk)
    n = S // blk
    tri_q, tri_k = _triangle(n, key_major=False)
    ones_rows = 2 * SUBLANES

    def body(tq_ref, tk_ref, k_ref, qt_ref, vt_ref, fq_ref, fk_ref, ot_ref, lse_ref, m_s, acc_s, vta_s):
        t = pl.program_id(1)
        qi, ki = tq_ref[t], tk_ref[t]

        @pl.when(ki == 0)
        def _():
            m_s[...] = jnp.full_like(m_s, NEG_BIG)
            acc_s[...] = jnp.zeros_like(acc_s)

        vta_s[0:HEAD_DIM, :] = vt_ref[...]
        vta_s[HEAD_DIM:, :] = jnp.ones((ones_rows, blk), BF16)

        def update(diagonal):
            n_strips = blk // strip
            keys_of = lambda j: strip * (j + 1) if diagonal else blk
            scores = lambda j: _strip_scores(k_ref, qt_ref, fk_ref, j, strip, keys_of(j), diagonal)
            def weighted_values(j, alpha, pb):
                cols = slice(strip * j, strip * (j + 1))
                acc_s[:, cols] = alpha * acc_s[:, cols] + jnp.dot(
                    vta_s[:, 0:keys_of(j)], pb, preferred_element_type=F32)

            ahead, behind = scores(0), None
            for j in range(n_strips):
                cols = slice(strip * j, strip * (j + 1))
                (s, keep), ahead = ahead, (scores(j + 1) if j + 1 < n_strips else None)
                if behind is not None:
                    weighted_values(*behind)
                if diagonal:
                    s = jnp.where(keep, s, NEG_BIG)
                fq = fq_ref[:, cols]
                m_old = m_s[:, cols]
                m_new = jnp.maximum(m_old, jnp.max(s, axis=0, keepdims=True) + fq)
                p = jnp.exp2(s - (m_new - fq))
                behind = (j, jnp.exp2(m_old - m_new), p.astype(BF16))
                m_s[:, cols] = m_new
            weighted_values(*behind)

        @pl.when(ki < qi)
        def _():
            update(False)

        @pl.when(ki == qi)
        def _():
            update(True)
            denom = acc_s[HEAD_DIM:HEAD_DIM + 1, :]
            ot_ref[...] = (acc_s[0:HEAD_DIM, :] / denom).astype(BF16)
            lse_ref[...] = m_s[...] + jnp.log2(denom)

    return pl.pallas_call(
        body, name="attn_fwd",
        grid_spec=pltpu.PrefetchScalarGridSpec(
            num_scalar_prefetch=2, grid=(N_HEADS, tri_q.shape[0]),
            in_specs=[pl.BlockSpec((blk, HEAD_DIM), lambda h, t, tq, tk: (tk[t], h)),
                      pl.BlockSpec((HEAD_DIM, blk), lambda h, t, tq, tk: (h, tq[t])),
                      pl.BlockSpec((HEAD_DIM, blk), lambda h, t, tq, tk: (2 * N_HEADS + h, tk[t])),
                      pl.BlockSpec((None, 1, blk), lambda h, t, tq, tk: (h, 0, tq[t])),
                      pl.BlockSpec((None, blk, LANES), lambda h, t, tq, tk: (h, tk[t], 0))],
            out_specs=[pl.BlockSpec((HEAD_DIM, blk), lambda h, t, tq, tk: (h, tq[t])),
                       pl.BlockSpec((None, 1, blk), lambda h, t, tq, tk: (h, 0, tq[t]))],
            scratch_shapes=[pltpu.VMEM((1, blk), F32), pltpu.VMEM((HEAD_DIM + ones_rows, blk), F32),
                            pltpu.VMEM((HEAD_DIM + ones_rows, blk), BF16)]),
        out_shape=[jax.ShapeDtypeStruct((N_HEADS * HEAD_DIM, S), BF16), jax.ShapeDtypeStruct((N_HEADS, 1, S), F32)],
        compiler_params=_cparams("parallel", "arbitrary"),
    )(tri_q, tri_k, kv, qkv_t, qkv_t, f_row, f_rep)


def _attn_bwd(kv, qkv_t, do_t, o_t, lse, f_row, f_rep):
    S = kv.shape[0]
    blk = min(ATTN_BLOCK, S)
    strip = min(ATTN_STRIP, blk)
    n = S // blk
    tri_q, tri_k = _triangle(n, key_major=True)
    n_tiles = tri_q.shape[0]

    def body(tq_ref, tk_ref, k_ref, v_ref, qt_ref, kt_ref, dot_ref, ot_ref, lse_ref, fq_ref, fk_ref,
             dqt_ref, dkt_ref, dvt_ref, dfk_ref, dfq_ref, dq_s, dk_s, dv_s, dfk_s, dfq_s, row_s):
        t = pl.program_id(1)
        qi, ki = tq_ref[t], tk_ref[t]

        @pl.when(t == 0)
        def _():
            dq_s[...] = jnp.zeros_like(dq_s)
            dfq_s[...] = jnp.zeros_like(dfq_s)

        @pl.when(qi == ki)
        def _():
            dk_s[...] = jnp.zeros_like(dk_s)
            dv_s[...] = jnp.zeros_like(dv_s)
            dfk_s[...] = jnp.zeros_like(dfk_s)

        def update(diagonal):
            row_s[...] = fq_ref[...] - lse_ref[...]
            n_strips = blk // strip
            keys_of = lambda j: strip * (j + 1) if diagonal else blk

            def matmuls_in(j):
                s, keep = _strip_scores(k_ref, qt_ref, fk_ref, j, strip, keys_of(j), diagonal)
                dp = jnp.dot(v_ref[0:keys_of(j), :], dot_ref[:, strip * j:strip * (j + 1)], preferred_element_type=F32)
                return s, keep, dp

            def matmuls_out(j, pb, dsb):
                cols = slice(strip * j, strip * (j + 1))
                nkeys = keys_of(j)
                dv_s[:, 0:nkeys] += lax.dot_general(dot_ref[:, cols], pb, NT_DIMS, preferred_element_type=F32)
                dk_s[:, 0:nkeys] += lax.dot_general(qt_ref[:, cols], dsb, NT_DIMS, preferred_element_type=F32)
                dq_s[qi, :, cols] += jnp.dot(kt_ref[:, 0:nkeys], dsb, preferred_element_type=F32)

            ahead, behind = matmuls_in(0), None
            for j in range(n_strips):
                cols = slice(strip * j, strip * (j + 1))
                nkeys = keys_of(j)
                (s, keep, dp), ahead = ahead, (matmuls_in(j + 1) if j + 1 < n_strips else None)
                if behind is not None:
                    matmuls_out(*behind)
                p = jnp.exp2(s + row_s[:, cols])
                if diagonal:
                    p = jnp.where(keep, p, 0.0)
                dot = dot_ref[:, cols]
                delta = jnp.sum(dot.astype(F32) * ot_ref[:, cols].astype(F32), axis=0, keepdims=True)
                ds = p * (dp - delta)
                behind = (j, p.astype(BF16), ds.astype(BF16))
                lane_part = ds[:, 0:LANES]
                for g in range(1, strip // LANES):
                    lane_part = lane_part + ds[:, LANES * g:LANES * (g + 1)]
                dfk_s[0:nkeys, :] += lane_part
                sub_part = ds[0:SUBLANES, :]
                for g in range(1, nkeys // SUBLANES):
                    sub_part = sub_part + ds[SUBLANES * g:SUBLANES * (g + 1), :]
                dfq_s[qi, :, cols] += sub_part
            matmuls_out(*behind)

        @pl.when(qi == ki)
        def _():
            update(True)

        @pl.when(qi > ki)
        def _():
            update(False)

        @pl.when(qi == n - 1)
        def _():
            dkt_ref[...] = (dk_s[...] * ATTN_SCALE).astype(BF16)
            dvt_ref[...] = dv_s[...].astype(BF16)
            dfk_ref[...] = -jnp.sum(dfk_s[...].T, axis=0, keepdims=True)

        @pl.when(t == n_tiles - 1)
        def _():
            for j in range(n):
                dqt_ref[:, blk * j:blk * (j + 1)] = (dq_s[j] * ATTN_SCALE).astype(BF16)
                dfq_ref[:, blk * j:blk * (j + 1)] = jnp.sum(dfq_s[j], axis=0, keepdims=True)

    q_feat = pl.BlockSpec((HEAD_DIM, blk), lambda h, t, tq, tk: (h, tq[t]))
    q_row = pl.BlockSpec((None, 1, blk), lambda h, t, tq, tk: (h, 0, tq[t]))
    k_feat = pl.BlockSpec((HEAD_DIM, blk), lambda h, t, tq, tk: (h, tk[t]))
    return pl.pallas_call(
        body, name="attn_bwd",
        grid_spec=pltpu.PrefetchScalarGridSpec(
            num_scalar_prefetch=2, grid=(N_HEADS, n_tiles),
            in_specs=[pl.BlockSpec((blk, HEAD_DIM), lambda h, t, tq, tk: (tk[t], h)),
                      pl.BlockSpec((blk, HEAD_DIM), lambda h, t, tq, tk: (tk[t], N_HEADS + h)),
                      q_feat,
                      pl.BlockSpec((HEAD_DIM, blk), lambda h, t, tq, tk: (N_HEADS + h, tk[t])),
                      q_feat, q_feat, q_row, q_row,
                      pl.BlockSpec((None, blk, LANES), lambda h, t, tq, tk: (h, tk[t], 0))],
            out_specs=[pl.BlockSpec((HEAD_DIM, S), lambda h, t, tq, tk: (h, 0)), k_feat, k_feat,
                       pl.BlockSpec((None, 1, blk), lambda h, t, tq, tk: (h, 0, tk[t])),
                       pl.BlockSpec((None, 1, S), lambda h, t, tq, tk: (h, 0, 0))],
            scratch_shapes=[pltpu.VMEM((n, HEAD_DIM, blk), F32), pltpu.VMEM((HEAD_DIM, blk), F32),
                            pltpu.VMEM((HEAD_DIM, blk), F32), pltpu.VMEM((blk, LANES), F32),
                            pltpu.VMEM((n, SUBLANES, blk), F32), pltpu.VMEM((1, blk), F32)]),
        out_shape=[jax.ShapeDtypeStruct((N_HEADS * HEAD_DIM, S), BF16)] * 3
        + [jax.ShapeDtypeStruct((N_HEADS, 1, S), F32), jax.ShapeDtypeStruct((N_HEADS, 1, S), F32)],
        compiler_params=_cparams("parallel", "arbitrary"),
    )(tri_q, tri_k, kv, kv, qkv_t, qkv_t, do_t, o_t, lse, f_row, f_rep)


def _gate_mix(gates, ya, yb):
    S, D = ya.shape
    tr = min(ROW_TILE, S)

    def body(ga_ref, gb_ref, ya_ref, yb_ref, o_ref):
        o_ref[...] = (_sigmoid(ga_ref[...]) * ya_ref[...] + _sigmoid(gb_ref[...]) * yb_ref[...]).astype(BF16)

    col = lambda j: pl.BlockSpec((tr, D), lambda i, j=j: (i, j))
    return pl.pallas_call(
        body, name="gate_mix", grid=(S // tr,),
        in_specs=[col(0), col(1), col(0), col(0)],
        out_specs=col(0),
        out_shape=jax.ShapeDtypeStruct((S, D), BF16),
        compiler_params=_cparams("parallel"),
    )(gates, gates, ya, yb)


def _gate_bwd(dmix, gates, ya, yb):
    S, D = ya.shape
    tr = min(ROW_TILE, S)

    def body(dm_ref, ga_ref, gb_ref, ya_ref, yb_ref, dya_ref, dyb_ref, dg_ref):
        dm = dm_ref[...]
        sa, sb = _sigmoid(ga_ref[...]), _sigmoid(gb_ref[...])
        dya_ref[...] = (dm * sa).astype(BF16)
        dyb_ref[...] = (dm * sb).astype(BF16)
        dg_ref[:, 0:D] = ((dm * ya_ref[...]) * (sa * (1.0 - sa))).astype(BF16)
        dg_ref[:, D:] = ((dm * yb_ref[...]) * (sb * (1.0 - sb))).astype(BF16)

    col = lambda j: pl.BlockSpec((tr, D), lambda i, j=j: (i, j))
    return pl.pallas_call(
        body, name="gate_bwd", grid=(S // tr,),
        in_specs=[col(0), col(0), col(1), col(0), col(0)],
        out_specs=[col(0), col(0), pl.BlockSpec((tr, 2 * D), lambda i: (i, 0))],
        out_shape=[jax.ShapeDtypeStruct((S, D), BF16), jax.ShapeDtypeStruct((S, D), BF16),
                   jax.ShapeDtypeStruct((S, 2 * D), BF16)],
        compiler_params=_cparams("parallel"),
    )(dmix, gates, gates, ya, yb)


def _mesh_place():
    x, y, c = lax.axis_index("x"), lax.axis_index("y"), lax.axis_index("c")
    chips = [(1 - x, y), (x, 1 - y), (1 - x, 1 - y)]
    return x, y, c, chips


def _all_gather(shards):
    n = len(shards)

    def body(*refs):
        ins, outs = refs[:n], refs[n:2 * n]
        send_sems, recv_sems, local_sems = refs[2 * n:]
        x, y, c, chips = _mesh_place()
        me, sib = (x, y, c), (x, y, 1 - c)

        def copy(a, k, block, to, src=None):
            px, py, pc = block
            dst = outs[a].at[4 * px + 2 * py + pc]
            return pltpu.make_async_remote_copy(
                src_ref=dst if src is None else src, dst_ref=dst,
                send_sem=send_sems.at[a, k], recv_sem=recv_sems.at[a, k],
                device_id=to, device_id_type=MESH_ID)

        mine = [pltpu.make_async_copy(ins[a], outs[a].at[4 * x + 2 * y + c], local_sems.at[a]) for a in range(n)]
        for cp in mine:
            cp.start()
        first = []
        for a in range(n):
            first.append(copy(a, 0, me, sib, src=ins[a]))
            for j, chip in enumerate(chips):
                first.append(copy(a, 1 + j, me, (*chip, c), src=ins[a]))
        for cp in first:
            cp.start()
        passed = []
        for j, chip in enumerate(chips):
            for a in range(n):
                copy(a, 1 + j, (*chip, c), me).wait_recv()
                fwd = copy(a, 4 + j, (*chip, c), sib)
                fwd.start()
                passed.append(fwd)
        for a in range(n):
            copy(a, 0, sib, me).wait_recv()
            for j, chip in enumerate(chips):
                copy(a, 4 + j, (*chip, 1 - c), me).wait_recv()
        for cp in first + passed:
            cp.wait_send()
        for cp in mine:
            cp.wait()

    return pl.pallas_call(
        body, name="all_gather_weights",
        in_specs=[ANY] * n, out_specs=[ANY] * n,
        out_shape=[jax.ShapeDtypeStruct((N_DEV,) + s.shape, s.dtype) for s in shards],
        scratch_shapes=[pltpu.SemaphoreType.DMA((n, 7)), pltpu.SemaphoreType.DMA((n, 7)),
                        pltpu.SemaphoreType.DMA((n,))],
    )(*shards)


def _chip_partial_sum(blocks, got, core):
    R, C = got.shape[1:]
    tr = min(256, R)
    assert R % tr == 0

    def body(core_ref, a_ref, b_ref, s_ref, sb_ref):
        s = a_ref[...] + b_ref[...]
        s_ref[...] = s
        sb_ref[...] = s.astype(BF16)

    blk = pl.BlockSpec((None, tr, C), lambda k, i, core_ref: (k, i, 0))
    return pl.pallas_call(
        body, name="chip_partial_sum",
        grid_spec=pltpu.PrefetchScalarGridSpec(
            num_scalar_prefetch=1, grid=(4, R // tr),
            in_specs=[pl.BlockSpec((None, tr, C), lambda k, i, core_ref: (2 * k + core_ref[0], i, 0)), blk],
            out_specs=[blk, blk]),
        out_shape=[jax.ShapeDtypeStruct(got.shape, F32), jax.ShapeDtypeStruct(got.shape, BF16)],
        compiler_params=_cparams("parallel", "parallel"),
    )(core, blocks, got)


HBM_SPEC = pl.BlockSpec(memory_space=pltpu.HBM)
SEM_SPEC = pl.BlockSpec(memory_space=pltpu.SEMAPHORE)
FLIPS = [(dx, dy, dc) for dx in (0, 1) for dy in (0, 1) for dc in (0, 1) if (dx, dy, dc) != (0, 0, 0)]


def _flip(v, d):
    return 1 - v if d else v


def _gather_copies(srcs, lands, send_sems, recv_sems):
    x, y, c, _ = _mesh_place()
    sends, recvs = [], []
    for a in range(len(srcs)):
        for k, (dx, dy, dc) in enumerate(FLIPS):
            px, py, pc = _flip(x, dx), _flip(y, dy), _flip(c, dc)
            sem = len(FLIPS) * a + k
            common = dict(send_sem=send_sems.at[sem], recv_sem=recv_sems.at[sem],
                          device_id=(px, py, pc), device_id_type=MESH_ID)
            sends.append(pltpu.make_async_remote_copy(
                src_ref=srcs[a], dst_ref=lands[a].at[4 * x + 2 * y + c], **common))
            recvs.append(pltpu.make_async_remote_copy(
                src_ref=srcs[a], dst_ref=lands[a].at[4 * px + 2 * py + pc], **common))
    return sends, recvs


def _cores_copies(srcs, lands, send_sems, recv_sems):
    x, y, c, _ = _mesh_place()
    copies = []
    for a in range(len(srcs)):
        for k in range(4):
            copies.append(pltpu.make_async_remote_copy(
                src_ref=srcs[a].at[2 * k + (1 - c)], dst_ref=lands[a].at[k],
                send_sem=send_sems.at[4 * a + k], recv_sem=recv_sems.at[4 * a + k],
                device_id=(x, y, 1 - c), device_id_type=MESH_ID))
    return copies, copies


def _scatter_copies(srcs, lands, send_sems, recv_sems):
    x, y, c, chips = _mesh_place()
    sends = []
    for a in range(len(srcs)):
        for j, (px, py) in enumerate(chips):
            sends.append(pltpu.make_async_remote_copy(
                src_ref=srcs[a].at[2 * px + py], dst_ref=lands[a].at[j],
                send_sem=send_sems.at[3 * a + j], recv_sem=recv_sems.at[3 * a + j],
                device_id=(px, py, c), device_id_type=MESH_ID))
    return sends, sends


def _exchange_start(srcs, land_shapes, copies, n_copies, name):
    n = len(srcs)

    def body(*refs):
        src_refs, land_refs = refs[:n], refs[n:2 * n]
        send_sems, recv_sems = refs[2 * n], refs[2 * n + 1]
        token = refs[-1]
        sends, _ = copies(src_refs, land_refs, send_sems, recv_sems)
        for cp in sends:
            cp.start()
        token[...] = jnp.zeros_like(token)

    lands = [pltpu.with_memory_space_constraint(lax.empty(s.shape, s.dtype), pltpu.HBM) for s in land_shapes]
    srcs = [pltpu.with_memory_space_constraint(s, pltpu.HBM) for s in srcs]
    res = pl.pallas_call(
        body, name=name,
        out_shape=(pltpu.SemaphoreType.DMA((n * n_copies,)), pltpu.SemaphoreType.DMA((n * n_copies,)),
                   *[pltpu.HBM(s.shape, s.dtype) for s in srcs], *[pltpu.HBM(s.shape, s.dtype) for s in land_shapes],
                   jax.ShapeDtypeStruct((SUBLANES, LANES), F32)),
        in_specs=[HBM_SPEC] * (2 * n),
        out_specs=(SEM_SPEC, SEM_SPEC, *[HBM_SPEC] * (2 * n), pl.BlockSpec(memory_space=pltpu.VMEM)),
        input_output_aliases={i: 2 + i for i in range(2 * n)},
        compiler_params=pltpu.CompilerParams(has_side_effects=pltpu.SideEffectType.DATAFLOW_SIDE_EFFECTING),
    )(*srcs, *lands)
    return res[0], res[1], list(res[2:2 + n]), list(res[2 + n:2 + 2 * n]), res[-1]


def _exchange_wait(started, copies, after, name):
    send_sems, recv_sems, srcs, lands, _ = started
    n = len(srcs)

    def body(*refs):
        src_refs, land_refs = refs[:n], refs[n:2 * n]
        send_ref, recv_ref = refs[2 * n], refs[2 * n + 1]
        sends, recvs = copies(src_refs, land_refs, send_ref, recv_ref)
        for cp in sends:
            cp.wait_send()
        for cp in recvs:
            cp.wait_recv()

    res = pl.pallas_call(
        body, name=name,
        out_shape=tuple(pltpu.HBM(s.shape, s.dtype) for s in srcs + lands),
        in_specs=[HBM_SPEC] * (2 * n) + [SEM_SPEC, SEM_SPEC, ANY],
        out_specs=tuple([HBM_SPEC] * (2 * n)),
        input_output_aliases={i: i for i in range(2 * n)},
        compiler_params=pltpu.CompilerParams(has_side_effects=pltpu.SideEffectType.DATAFLOW_SIDE_EFFECTING),
    )(*srcs, *lands, send_sems, recv_sems, after)
    return list(res[:n]), list(res[n:])


def _all_reduce_small(vec):
    R = vec.shape[0]

    def body(v_ref, o_ref, sib_buf, chip_buf, send_sems, recv_sems):
        x, y, c, chips = _mesh_place()
        swap = pltpu.make_async_remote_copy(
            src_ref=v_ref, dst_ref=sib_buf, send_sem=send_sems.at[0], recv_sem=recv_sems.at[0],
            device_id=(x, y, 1 - c), device_id_type=MESH_ID)
        swap.start()
        swap.wait()
        my_chip = 2 * x + y
        chip_buf[my_chip] = v_ref[...] + sib_buf[...]
        sends = []
        for j, (px, py) in enumerate(chips):
            cp = pltpu.make_async_remote_copy(
                src_ref=chip_buf.at[my_chip], dst_ref=chip_buf.at[my_chip],
                send_sem=send_sems.at[1 + j], recv_sem=recv_sems.at[1 + j],
                device_id=(px, py, c), device_id_type=MESH_ID)
            cp.start()
            sends.append(cp)
        for j, (px, py) in enumerate(chips):
            pltpu.make_async_remote_copy(
                src_ref=chip_buf.at[2 * px + py], dst_ref=chip_buf.at[2 * px + py],
                send_sem=send_sems.at[1 + j], recv_sem=recv_sems.at[1 + j],
                device_id=(px, py, c), device_id_type=MESH_ID).wait_recv()
        for cp in sends:
            cp.wait_send()
        o_ref[...] = ((chip_buf[0] + chip_buf[1]) + chip_buf[2]) + chip_buf[3]

    vm = pl.BlockSpec(memory_space=pltpu.VMEM)
    return pl.pallas_call(
        body, name="all_reduce_small",
        in_specs=[vm], out_specs=vm,
        out_shape=jax.ShapeDtypeStruct(vec.shape, F32),
        scratch_shapes=[pltpu.VMEM((R, LANES), F32), pltpu.VMEM((4, R, LANES), F32),
                        pltpu.SemaphoreType.DMA((4,)), pltpu.SemaphoreType.DMA((4,))],
    )(vec)


def _adamw_math(w, g, m, v):
    m = ADAM_B1 * m + (1.0 - ADAM_B1) * g
    v = ADAM_B2 * v + (1.0 - ADAM_B2) * (g * g)
    m_hat = m / (1.0 - ADAM_B1 ** ADAM_STEP)
    v_hat = v / (1.0 - ADAM_B2 ** ADAM_STEP)
    delta = -ADAM_LR * (m_hat / (jnp.sqrt(v_hat) + ADAM_EPS) + ADAM_WD * w)
    return delta, m, v


def _adamw(w, m, v, g_own, g_got, chip, name):
    R, C = w.shape
    tr = R if R * C <= 256 * D_MODEL else 256
    assert R % tr == 0
    n_got = g_got.shape[0]

    def body(*refs):
        w_ref, m_ref, v_ref, go_ref = refs[1:5]
        got = refs[5:5 + n_got]
        g_ref, d_ref, nm_ref, nv_ref = refs[5 + n_got:]
        g = go_ref[...]
        for r in got:
            g = g + r[...].astype(F32)
        delta, m_new, v_new = _adamw_math(w_ref[...], g, m_ref[...], v_ref[...])
        g_ref[...] = g
        d_ref[...] = delta
        nm_ref[...] = m_new
        nv_ref[...] = v_new

    blk = pl.BlockSpec((tr, C), lambda i, chip_ref: (i, 0))
    own_spec = pl.BlockSpec((None, tr, C), lambda i, chip_ref: (chip_ref[0], i, 0))
    got_specs = [pl.BlockSpec((None, tr, C), lambda i, chip_ref, j=j: (j, i, 0)) for j in range(n_got)]
    return pl.pallas_call(
        body, name=name,
        grid_spec=pltpu.PrefetchScalarGridSpec(
            num_scalar_prefetch=1, grid=(R // tr,),
            in_specs=[blk] * 3 + [own_spec] + got_specs, out_specs=[blk] * 4),
        out_shape=[jax.ShapeDtypeStruct((R, C), F32)] * 4,
        compiler_params=_cparams("parallel"),
    )(chip, w, m, v, g_own, *([g_got] * n_got))


def _block_diag_pairs(wa, wx):
    def pairs(w):
        w = w.reshape(N_GROUPS, 2, LRU_BW, LRU_BW)
        z = jnp.zeros((N_GROUPS, LRU_BW, LRU_BW), w.dtype)
        top = jnp.concatenate([w[:, 0], z], axis=2)
        bot = jnp.concatenate([z, w[:, 1]], axis=2)
        return jnp.concatenate([top, bot], axis=1)
    return jnp.concatenate([pairs(wa), pairs(wx)], axis=2).astype(BF16)


def _block_diag_unpair(dbd):
    def unpair(g):
        blocks = jnp.stack([g[:, :LRU_BW, :LRU_BW], g[:, LRU_BW:, LRU_BW:]], axis=1)
        return blocks.reshape(LRU_BLOCKS, LRU_BW, LRU_BW)
    return unpair(dbd[:, :, :LANES]), unpair(dbd[:, :, LANES:])


def _local_step(x, target, W, small, late_weights=None, hooks=None):
    def hook(name, *args):
        return hooks[name](*args) if hooks is not None else (None, 0.0)

    S, D = x.shape
    g1, g2, g3 = small["norm_mix_g"], small["norm_mlp_g"], small["norm_final_g"]
    cw, cb = small["conv_w"], small["conv_b"].reshape(1, D)
    ba, bx, lam = (small[k].reshape(1, D) for k in ("lru_ba", "lru_bx", "lru_lambda"))
    fb = jnp.pad(small["forget_b"], (0, LANES - N_HEADS)).reshape(1, LANES)
    bd = _block_diag_pairs(small["lru_wa"], small["lru_wx"])
    big = dict(tm=1024, tn=1024)

    u = _norm_fwd(x, g1, "norm_mix")
    (xg,) = _mm([(u, W["in_xg"])], tks=[D], outs=[F32], name="proj_xg", **big)
    (qkv_t,) = _mm([(W["in_qkv_t"], u)], tb=True, tks=[D], outs=[BF16], name="proj_qkv_t", **big)
    (kv,) = _mm([(u, W["in_kv"])], tks=[D], outs=[BF16], name="proj_kv", **big)
    (gates,) = _mm([(u, W["in_gates"])], tks=[D], outs=[F32], name="proj_gates", **big)
    (fl,) = _mm([(u, W["in_f"])], tks=[D], outs=[F32], name="proj_forget", **big)
    h, yain = _lru_fwd(xg, cw, cb, bd, ba, bx, lam)
    fcum, f_rep = _forget_cumsum(fl, fb)
    f_row = fcum[:, :N_HEADS].T.reshape(N_HEADS, 1, S)
    ob_t, lse = _attn_fwd(kv, qkv_t, f_row, f_rep)
    if late_weights is not None:
        W = {**W, **late_weights(lse)}
    (ya,) = _mm([(yain, W["branch_a"])], tks=[D], outs=[F32], name="branch_a", **big)
    (yb,) = _mm([(ob_t, W["branch_b"])], ta=True, tks=[D], outs=[F32], name="branch_b", **big)
    mix = _gate_mix(gates, ya, yb)
    (x1,) = _mm([(mix, W["out"])], tks=[D], outs=[F32], name="out_proj", extra=(x,),
                epi=lambda acc, res: (res + acc,), **big)
    m = _norm_fwd(x1, g2, "norm_mlp")
    relu, hh = _mm([(m, W["up"])], tks=[D], outs=[BF16, BF16], name="mlp_up",
                   epi=lambda acc: (jnp.maximum(acc, 0.0), jnp.square(jnp.maximum(acc, 0.0))), **big)
    deep = dict(tm=512, tn=1024, tks=[D_FF])
    wgrad = dict(tm=1024, tn=512, tks=[min(4096, S)])
    (x2,) = _mm([(hh, W["down"])], outs=[F32], name="mlp_down", extra=(x1,),
                epi=lambda acc, res: (res + acc,), **deep)
    loss_acc, dg3, dx2, dx2b = _final_norm_loss(x2, target, g3)

    (dhpre,) = _mm([(dx2b, W["down"])], tb=True, tks=[D], outs=[BF16], name="d_mlp_act", extra=(relu,),
                   epi=lambda acc, r: (acc * (2.0 * r.astype(F32)),), **big)
    (dw_down,) = _mm([(hh, dx2b)], ta=True, outs=[F32], name="dw_down", **wgrad)
    (dm,) = _mm([(dhpre, W["up"])], tb=True, outs=[F32], name="d_mlp_in", **deep)
    assert wgrad["tn"] == D_FF // N_DEV
    (dw_up,) = _mm([(m, dhpre)], ta=True, outs=[F32], name="dw_up", col_blocked=True, **wgrad)
    dx1, dx1b, dg2 = _norm_bwd(dm, x1, g2, dx2, "norm_mlp_bwd")
    (dmix,) = _mm([(dx1b, W["out"])], tb=True, tks=[D], outs=[F32], name="d_mix", **big)
    (dw_out,) = _mm([(mix, dx1b)], ta=True, outs=[F32], name="dw_out", **wgrad)
    dya, dyb, dgates = _gate_bwd(dmix, gates, ya, yb)
    (dob_t,) = _mm([(W["branch_b"], dyb)], tb=True, tks=[D], outs=[BF16], name="d_attn_out_t", **big)
    (dw_b,) = _mm([(ob_t, dyb)], outs=[F32], name="dw_branch_b", **wgrad)
    (dyain,) = _mm([(dya, W["branch_a"])], tb=True, tks=[D], outs=[F32], name="d_lru_out", **big)
    (dw_a,) = _mm([(yain, dya)], ta=True, outs=[F32], name="dw_branch_a", **wgrad)
    early = dict(w_branch_a=dw_a, w_branch_b=dw_b, w_out=dw_out, w_up=dw_up, w_down=dw_down)
    early_state, zero = hook("early_start", early)
    dq_t, dk_t, dv_t, dfk, dfq = _attn_bwd(kv, qkv_t, dob_t, ob_t, lse + zero, f_row, f_rep)
    early_state, zero = hook("early_mid", early_state, dfq)
    dF = jnp.pad((dfk.reshape(N_HEADS, S) + dfq.reshape(N_HEADS, S)).T, ((0, 0), (0, LANES - N_HEADS)))
    dfl, dfb = _forget_bwd(dF, fl, fb)
    dxg, dcw, dcb, dba, dbx, dlam, dbd = _lru_bwd(xg, h, dyain, cw, cb, bd, ba, bx, lam + zero)
    dw_in_parts = [
        _mm([(u, dxg)], ta=True, outs=[F32], name="dw_in_xg", **wgrad)[0],
        _mm([(dq_t, u)], outs=[F32], name="dw_in_q_t", **wgrad)[0].T,
        _mm([(dk_t, u)], outs=[F32], name="dw_in_k_t", **wgrad)[0].T,
        _mm([(dv_t, u)], outs=[F32], name="dw_in_v_t", **wgrad)[0].T,
        _mm([(u, dgates)], ta=True, outs=[F32], name="dw_in_gates", **wgrad)[0],
        _mm([(u, dfl)], ta=True, outs=[F32], name="dw_in_forget", **wgrad)[0][:, :N_HEADS],
    ]
    dw_in = jnp.concatenate(dw_in_parts, axis=1)
    in_state, zero = hook("in_start", dw_in)
    wq_t, wk_t, wv_t = (W["in_qkv_t"][D * i:D * (i + 1)] for i in range(3))
    (du_tok,) = _mm([(dxg, W["in_xg"]), (dgates, W["in_gates"]), (dfl, W["in_f"] + jnp.asarray(zero, BF16))],
                    tb=True, tks=[2 * D, 2 * D, LANES], outs=[F32], name="d_norm_mix_out_tok", tm=1024, tn=512)
    in_state, zero = hook("in_mid", in_state, du_tok)
    (du,) = _mm([(dq_t, wq_t + jnp.asarray(zero, BF16)), (dk_t, wk_t), (dv_t, wv_t)], ta=True, tks=[D, D, D],
                outs=[F32], name="d_norm_mix_out", extra=(du_tok,), epi=lambda acc, prev: (prev + acc,),
                tm=1024, tn=512)
    grad_x, _, dg1 = _norm_bwd(du, x, g1, dx1, "norm_mix_bwd")

    dwa, dwx = _block_diag_unpair(dbd)
    big_grads = dict(early, w_in=dw_in)
    small_grads = dict(norm_mix_g=dg1.reshape(D), conv_w=dcw, conv_b=dcb.reshape(D), lru_wa=dwa, lru_ba=dba.reshape(D),
                       lru_wx=dwx, lru_bx=dbx.reshape(D), lru_lambda=dlam.reshape(D), forget_b=dfb[0, :N_HEADS],
                       norm_mlp_g=dg2.reshape(D), norm_final_g=dg3.reshape(D))
    return loss_acc[0, 0], grad_x, big_grads, small_grads, (early_state, in_state)


SMALL_NAMES = ("norm_mix_g", "conv_b", "lru_wa", "lru_ba", "lru_wx", "lru_bx", "lru_lambda", "forget_b",
               "norm_mlp_g", "norm_final_g")
TILE_ELEMS = SUBLANES * LANES


def _pack_small(parts):
    rows = []
    for p in parts:
        flat = p.reshape(-1)
        flat = jnp.pad(flat, (0, (-flat.shape[0]) % TILE_ELEMS))
        rows.append(flat.reshape(-1, LANES))
    return jnp.concatenate(rows, axis=0)


def _packed_rows(shape):
    return -(-math.prod(shape) // TILE_ELEMS) * SUBLANES


def _adamw_small(g_packed, g_conv_w, weights, moms, vels):
    def rows_view(a):
        flat = a.reshape(-1)
        flat = jnp.pad(flat, (0, (-flat.shape[0]) % LANES))
        return flat.reshape(-1, LANES)

    names = SMALL_NAMES + ("conv_w",)
    views = [[rows_view(src[k]) for k in names] for src in (weights, moms, vels)]
    n = len(names)
    starts, r = [], 0
    for k in SMALL_NAMES:
        starts.append(r)
        r += _packed_rows(weights[k].shape)

    def body(*refs):
        gp_ref, gc_ref = refs[0], refs[1]
        w_refs, m_refs, v_refs = refs[2:2 + n], refs[2 + n:2 + 2 * n], refs[2 + 2 * n:2 + 3 * n]
        outs = refs[2 + 3 * n:]
        for i in range(n):
            rows = w_refs[i].shape[0]
            g = gc_ref[...] if i == n - 1 else gp_ref[starts[i]:starts[i] + rows, :]
            delta, m_new, v_new = _adamw_math(w_refs[i][...], g, m_refs[i][...], v_refs[i][...])
            for o_ref, val in zip(outs[4 * i:4 * i + 4], (g, delta, m_new, v_new)):
                o_ref[...] = val

    vm = pl.BlockSpec(memory_space=pltpu.VMEM)
    out_shape = [jax.ShapeDtypeStruct(v.shape, F32) for v in views[0] for _ in range(4)]
    res = pl.pallas_call(
        body, name="adamw_small",
        in_specs=[vm] * (2 + 3 * n), out_specs=[vm] * (4 * n), out_shape=out_shape,
    )(g_packed, g_conv_w, *views[0], *views[1], *views[2])
    dicts = ({}, {}, {}, {})
    for i, k in enumerate(names):
        size = math.prod(weights[k].shape)
        for d, arr in zip(dicts, res[4 * i:4 * i + 4]):
            d[k] = arr.reshape(-1)[:size].reshape(weights[k].shape)
    return dicts


BIG_NAMES = ("w_in", "w_branch_a", "w_branch_b", "w_out", "w_up", "w_down")
WEIGHT_ORDER = ("norm_mix_g", "w_in", "conv_w", "conv_b", "lru_wa", "lru_ba", "lru_wx", "lru_bx", "lru_lambda",
                "forget_b", "w_branch_a", "w_branch_b", "w_out", "norm_mlp_g", "w_up", "w_down", "norm_final_g")


def _to_dest_blocks(name, g):
    if g.ndim == 3:
        return g
    if name in ("w_in", "w_up"):
        return g.reshape(g.shape[0], N_DEV, g.shape[1] // N_DEV).transpose(1, 0, 2)
    return g.reshape(N_DEV, g.shape[0] // N_DEV, g.shape[1])


def kernel(x, norm_mix_g, w_in, conv_w, conv_b, lru_wa, lru_ba, lru_wx, lru_bx, lru_lambda, forget_b, w_branch_a, w_branch_b, w_out, norm_mlp_g, w_up, w_down, norm_final_g, loss_target, m_norm_mix_g, m_w_in, m_conv_w, m_conv_b, m_lru_wa, m_lru_ba, m_lru_wx, m_lru_bx, m_lru_lambda, m_forget_b, m_w_branch_a, m_w_branch_b, m_w_out, m_norm_mlp_g, m_w_up, m_w_down, m_norm_final_g, v_norm_mix_g, v_w_in, v_conv_w, v_conv_b, v_lru_wa, v_lru_ba, v_lru_wx, v_lru_bx, v_lru_lambda, v_forget_b, v_w_branch_a, v_w_branch_b, v_w_out, v_norm_mlp_g, v_w_up, v_w_down, v_norm_final_g):
    weights = dict(norm_mix_g=norm_mix_g, w_in=w_in, conv_w=conv_w, conv_b=conv_b, lru_wa=lru_wa, lru_ba=lru_ba,
                   lru_wx=lru_wx, lru_bx=lru_bx, lru_lambda=lru_lambda, forget_b=forget_b, w_branch_a=w_branch_a,
                   w_branch_b=w_branch_b, w_out=w_out, norm_mlp_g=norm_mlp_g, w_up=w_up, w_down=w_down,
                   norm_final_g=norm_final_g)
    moms = dict(norm_mix_g=m_norm_mix_g, w_in=m_w_in, conv_w=m_conv_w, conv_b=m_conv_b, lru_wa=m_lru_wa,
                lru_ba=m_lru_ba, lru_wx=m_lru_wx, lru_bx=m_lru_bx, lru_lambda=m_lru_lambda, forget_b=m_forget_b,
                w_branch_a=m_w_branch_a, w_branch_b=m_w_branch_b, w_out=m_w_out, norm_mlp_g=m_norm_mlp_g,
                w_up=m_w_up, w_down=m_w_down, norm_final_g=m_norm_final_g)
    vels = dict(norm_mix_g=v_norm_mix_g, w_in=v_w_in, conv_w=v_conv_w, conv_b=v_conv_b, lru_wa=v_lru_wa,
                lru_ba=v_lru_ba, lru_wx=v_lru_wx, lru_bx=v_lru_bx, lru_lambda=v_lru_lambda, forget_b=v_forget_b,
                w_branch_a=v_w_branch_a, w_branch_b=v_w_branch_b, w_out=v_w_out, norm_mlp_g=v_norm_mlp_g,
                w_up=v_w_up, w_down=v_w_down, norm_final_g=v_norm_final_g)
    S, D = x.shape[1], x.shape[2]
    me = 4 * lax.axis_index("x") + 2 * lax.axis_index("y") + lax.axis_index("c")

    core = lax.axis_index("c").astype(jnp.int32).reshape(1)
    chip = (2 * lax.axis_index("x") + lax.axis_index("y")).astype(jnp.int32).reshape(1)
    late_names = BIG_NAMES[1:]

    win_g, cw_g = _all_gather([w_in.astype(BF16), conv_w])
    late_shards = [weights[k].astype(BF16) for k in late_names]
    gather = _exchange_start(late_shards, [jax.ShapeDtypeStruct((N_DEV,) + s.shape, BF16) for s in late_shards],
                             _gather_copies, len(FLIPS), "gather_late_start")
    w_in_full = win_g.transpose(1, 0, 2).reshape(D, -1)
    cuts = (0, 2 * D, 5 * D, 7 * D)
    W = dict(in_xg=w_in_full[:, cuts[0]:cuts[1]], in_qkv_t=w_in_full[:, cuts[1]:cuts[2]].T,
             in_kv=w_in_full[:, cuts[1] + D:cuts[2]], in_gates=w_in_full[:, cuts[2]:cuts[3]],
             in_f=jnp.pad(w_in_full[:, cuts[3]:], ((0, 0), (0, LANES - N_HEADS))))
    small = {k: weights[k] for k in SMALL_NAMES}
    small["conv_w"] = cw_g.transpose(1, 0, 2).reshape(CONV_W, D)
    small["norm_mix_g"] = norm_mix_g + gather[4][0, 0]

    def late_weights(after):
        shards, lands = _exchange_wait(gather, _gather_copies, after, "gather_late_wait")
        wa_g, wb_g, wo_g, wup_g, wdn_g = (
            lax.dynamic_update_slice_in_dim(land, shard[None], me, axis=0) for land, shard in zip(lands, shards))
        return dict(branch_a=wa_g.reshape(D, D), branch_b=wb_g.reshape(D, D), out=wo_g.reshape(D, D),
                    up=wup_g.transpose(1, 0, 2).reshape(D, D_FF), down=wdn_g.reshape(D_FF, D))

    def cores_start(names, grads_by_name, tag):
        blocks = [_to_dest_blocks(k, grads_by_name[k]) for k in names]
        started = _exchange_start(blocks, [jax.ShapeDtypeStruct((4,) + b.shape[1:], F32) for b in blocks],
                                  _cores_copies, 4, "cores_" + tag + "_start")
        return started, started[4][0, 0]

    def chips_start(started, after, tag):
        blocks, got = _exchange_wait(started, _cores_copies, after, "cores_" + tag + "_wait")
        sums = [_chip_partial_sum(b, g, core) for b, g in zip(blocks, got)]
        wire = [s[1] for s in sums]
        scatter = _exchange_start(wire, [jax.ShapeDtypeStruct((3,) + s.shape[1:], BF16) for s in wire],
                                  _scatter_copies, 3, "scatter_" + tag + "_start")
        return (sums, scatter), scatter[4][0, 0]

    hooks = dict(early_start=lambda g: cores_start(late_names, g, "early"),
                 early_mid=lambda st, after: chips_start(st, after, "early"),
                 in_start=lambda g: cores_start(BIG_NAMES[:1], dict(w_in=g), "w_in"),
                 in_mid=lambda st, after: chips_start(st, after, "w_in"))
    loss_part, grad_x, _, small_grads, ((early_sums, early_scatter), (in_sums, in_scatter)) = _local_step(
        x.reshape(S, D), loss_target.reshape(S, D), W, small, late_weights, hooks)
    loss = lax.psum(loss_part, MESH_AXES)
    _, early_others = _exchange_wait(early_scatter, _scatter_copies, grad_x, "scatter_early_wait")
    _, in_others = _exchange_wait(in_scatter, _scatter_copies, grad_x, "scatter_w_in_wait")
    sums = list(in_sums) + list(early_sums)
    others = list(in_others) + list(early_others)

    reduced = _all_reduce_small(_pack_small([small_grads[k] for k in SMALL_NAMES] + [small_grads["conv_w"]]))
    cw_full = reduced[reduced.shape[0] - _packed_rows((CONV_W, D)):].reshape(CONV_W, D)
    cw_cols = lax.dynamic_slice_in_dim(cw_full, me * (D // N_DEV), D // N_DEV, axis=1)

    grads, deltas, new_m, new_v = _adamw_small(reduced, cw_cols, weights, moms, vels)
    for k, s, g_got in zip(BIG_NAMES, sums, others):
        grads[k], deltas[k], new_m[k], new_v[k] = _adamw(weights[k], moms[k], vels[k], s[0], g_got, chip, "adamw_" + k)

    return (loss, grad_x.reshape(1, S, D), *[grads[k] for k in WEIGHT_ORDER], *[deltas[k] for k in WEIGHT_ORDER],
            *[new_m[k] for k in WEIGHT_ORDER], *[new_v[k] for k in WEIGHT_ORDER])
```

```python
import functools
import math

import jax
import jax.numpy as jnp
from jax import lax
from jax.experimental import pallas as pl
from jax.experimental.pallas import tpu as pltpu

F32 = jnp.float32
BF16 = jnp.bfloat16

D_MODEL = 1024
N_HEADS = 8
HEAD_DIM = 128
D_FF = 4096
LRU_BLOCKS = 16
LRU_BW = 64
LRU_C = 8.0
CONV_W = 4
RMS_EPS = 1e-6
N_DEV = 8
LANES = 128
SUBLANES = 8
N_GROUPS = D_MODEL // LANES
VMEM_LIMIT_BYTES = 52 * 1024 * 1024
ATTN_SCALE = 1.0 / math.sqrt(HEAD_DIM)
LOG2E = math.log2(math.e)
NEG_BIG = -1e30
ADAM_LR = 0.001
ADAM_B1 = 0.9
ADAM_B2 = 0.999
ADAM_EPS = 1e-08
ADAM_WD = 0.01
ADAM_STEP = 10
ATTN_BLOCK = 1024
ATTN_Q_TILES = 2
ATTN_STRIP = 256
LRU_CHUNK = 256
ROW_TILE = 512
MESH_AXES = ("x", "y", "c")
MESH_ID = pl.DeviceIdType.MESH
ANY = pl.BlockSpec(memory_space=pl.ANY)

NT_DIMS = (((1,), (1,)), ((), ()))
TN_DIMS = (((0,), (0,)), ((), ()))
NN_DIMS = (((1,), (0,)), ((), ()))


def _cparams(*sem):
    return pltpu.CompilerParams(dimension_semantics=sem if sem else None, vmem_limit_bytes=VMEM_LIMIT_BYTES)


def _sigmoid(x):
    return 0.5 * (jnp.tanh(0.5 * x) + 1.0)


def _log1p_pos(e):
    u = 1.0 + e
    return jnp.where(u == 1.0, e, jnp.log(u) * (e / (u - 1.0)))


def _softplus(z):
    return jnp.maximum(z, 0.0) + _log1p_pos(jnp.exp(-jnp.abs(z)))


def _expm1_neg(x):
    series = x * (1.0 + x * 0.5 * (1.0 + x * (1.0 / 3.0) * (1.0 + x * 0.25)))
    return jnp.where(x > -0.03, series, jnp.exp(x) - 1.0)


GELU_C = math.sqrt(2.0 / math.pi)
GELU_K = 0.044715


def _gelu(x):
    return 0.5 * x * (1.0 + jnp.tanh(GELU_C * (x + GELU_K * (x * x * x))))


def _gelu_and_grad(x):
    t = jnp.tanh(GELU_C * (x + GELU_K * (x * x * x)))
    g = 0.5 * x * (1.0 + t)
    dg = 0.5 * (1.0 + t) + 0.5 * x * (1.0 - t * t) * (GELU_C * (1.0 + 3.0 * GELU_K * (x * x)))
    return g, dg


def _mm(pairs, *, ta=False, tb=False, tm, tn, tks, outs, name, epi=None, extra=(), col_blocked=False):
    n_pairs, n_extra, n_out = len(pairs), len(extra), len(outs)
    tas = list(ta) if isinstance(ta, (list, tuple)) else [ta] * n_pairs
    tbs = list(tb) if isinstance(tb, (list, tuple)) else [tb] * n_pairs
    a0, b0 = pairs[0]
    M = a0.shape[1] if tas[0] else a0.shape[0]
    N = b0.shape[0] if tbs[0] else b0.shape[1]
    tm, tn = min(tm, M), min(tn, N)
    nks, offs = [], []
    for (a, b), tk, pta in zip(pairs, tks, tas):
        K = a.shape[0] if pta else a.shape[1]
        assert K % tk == 0 and M % tm == 0 and N % tn == 0
        offs.append(sum(nks))
        nks.append(K // tk)
    nk_total = sum(nks)
    dims = [(((0 if pta else 1,), (1 if ptb else 0,)), ((), ())) for pta, ptb in zip(tas, tbs)]

    def kmap(off, nk):
        return lambda k: jnp.clip(k - off, 0, nk - 1)

    in_specs, operands = [], []
    for (a, b), tk, off, nk, pta, ptb in zip(pairs, tks, offs, nks, tas, tbs):
        km = kmap(off, nk)
        if pta:
            in_specs.append(pl.BlockSpec((tk, tm), lambda i, j, k, km=km: (km(k), i)))
        else:
            in_specs.append(pl.BlockSpec((tm, tk), lambda i, j, k, km=km: (i, km(k))))
        if ptb:
            in_specs.append(pl.BlockSpec((tn, tk), lambda i, j, k, km=km: (j, km(k))))
        else:
            in_specs.append(pl.BlockSpec((tk, tn), lambda i, j, k, km=km: (km(k), j)))
        operands += [a, b]
    for e in extra:
        in_specs.append(pl.BlockSpec((tm, tn), lambda i, j, k: (i, j)))
        operands.append(e)

    def body(*refs):
        ab = refs[:2 * n_pairs]
        ex = refs[2 * n_pairs:2 * n_pairs + n_extra]
        o = refs[2 * n_pairs + n_extra:2 * n_pairs + n_extra + n_out]
        k = pl.program_id(2)

        def finish(acc):
            res = epi(acc, *[e[...] for e in ex]) if epi is not None else (acc,)
            for r, oref in zip(res, o):
                oref[...] = r.astype(oref.dtype)

        if nk_total == 1:
            finish(lax.dot_general(ab[0][...], ab[1][...], dims[0], preferred_element_type=F32))
            return
        acc = refs[-1]
        for p in range(n_pairs):
            a_ref, b_ref = ab[2 * p], ab[2 * p + 1]

            @pl.when((k >= offs[p]) & (k < offs[p] + nks[p]))
            def _(a_ref=a_ref, b_ref=b_ref, pdims=dims[p]):
                prod = lax.dot_general(a_ref[...], b_ref[...], pdims, preferred_element_type=F32)

                @pl.when(k == 0)
                def _():
                    acc[...] = prod

                @pl.when(k > 0)
                def _():
                    acc[...] += prod

        @pl.when(k == nk_total - 1)
        def _():
            finish(acc[...])

    return pl.pallas_call(
        body,
        name=name,
        grid=(M // tm, N // tn, nk_total),
        in_specs=in_specs,
        out_specs=[pl.BlockSpec((None, tm, tn), lambda i, j, k: (j, i, 0)) if col_blocked
                   else pl.BlockSpec((tm, tn), lambda i, j, k: (i, j)) for _ in outs],
        out_shape=[jax.ShapeDtypeStruct((N // tn, M, tn) if col_blocked else (M, N), dt) for dt in outs],
        scratch_shapes=[] if nk_total == 1 else [pltpu.VMEM((tm, tn), F32)],
        compiler_params=_cparams("parallel", "parallel", "arbitrary"),
    )(*operands)


def _norm_fwd(x, g, name):
    S, D = x.shape
    tr = min(ROW_TILE, S)

    def body(x_ref, g_ref, o_ref):
        xv = x_ref[...]
        r = lax.rsqrt(jnp.mean(xv * xv, axis=-1, keepdims=True) + RMS_EPS)
        o_ref[...] = ((xv * r) * g_ref[...]).astype(o_ref.dtype)

    return pl.pallas_call(
        body, name=name, grid=(S // tr,),
        in_specs=[pl.BlockSpec((tr, D), lambda i: (i, 0)), pl.BlockSpec((1, D), lambda i: (0, 0))],
        out_specs=pl.BlockSpec((tr, D), lambda i: (i, 0)),
        out_shape=jax.ShapeDtypeStruct((S, D), BF16),
        compiler_params=_cparams("parallel"),
    )(x, g.reshape(1, D))


def _rms_bwd_rows(dy, xv, g):
    r = lax.rsqrt(jnp.mean(xv * xv, axis=-1, keepdims=True) + RMS_EPS)
    xn = xv * r
    dxn = dy * g
    dx = r * (dxn - xn * jnp.mean(dxn * xn, axis=-1, keepdims=True))
    dg = jnp.sum(dy * xn, axis=0, keepdims=True)
    return dx, dg


def _norm_bwd(dy, x, g, dres, name):
    S, D = x.shape
    tr = min(ROW_TILE, S)

    def body(dy_ref, x_ref, g_ref, dres_ref, dx_ref, dxb_ref, dg_ref):
        dx, dg = _rms_bwd_rows(dy_ref[...], x_ref[...], g_ref[...])
        dx = dres_ref[...] + dx
        dx_ref[...] = dx
        dxb_ref[...] = dx.astype(BF16)

        @pl.when(pl.program_id(0) == 0)
        def _():
            dg_ref[...] = jnp.zeros_like(dg_ref)

        dg_ref[...] += dg

    row = pl.BlockSpec((tr, D), lambda i: (i, 0))
    vec = pl.BlockSpec((1, D), lambda i: (0, 0))
    return pl.pallas_call(
        body, name=name, grid=(S // tr,),
        in_specs=[row, row, vec, row],
        out_specs=[row, row, vec],
        out_shape=[jax.ShapeDtypeStruct((S, D), F32), jax.ShapeDtypeStruct((S, D), BF16),
                   jax.ShapeDtypeStruct((1, D), F32)],
        compiler_params=_cparams("arbitrary"),
    )(dy, x, g.reshape(1, D), dres)


def _final_norm_loss(x2, target, g):
    S, D = x2.shape
    tr = min(ROW_TILE, S)

    def body(x_ref, t_ref, g_ref, loss_ref, dg_ref, dx_ref, dxb_ref):
        xv = x_ref[...]
        gv = g_ref[...]
        r = lax.rsqrt(jnp.mean(xv * xv, axis=-1, keepdims=True) + RMS_EPS)
        y = (xv * r) * gv
        err = y - t_ref[...]
        part = 0.5 * jnp.sum(jnp.mean(err * err, axis=-1, keepdims=True), axis=0, keepdims=True)
        dy = err * (1.0 / D)
        dx, dg = _rms_bwd_rows(dy, xv, gv)
        dx_ref[...] = dx
        dxb_ref[...] = dx.astype(BF16)

        @pl.when(pl.program_id(0) == 0)
        def _():
            dg_ref[...] = jnp.zeros_like(dg_ref)
            loss_ref[...] = jnp.zeros_like(loss_ref)

        dg_ref[...] += dg
        loss_ref[...] += jnp.broadcast_to(part, loss_ref.shape)

    row = pl.BlockSpec((tr, D), lambda i: (i, 0))
    vec = pl.BlockSpec((1, D), lambda i: (0, 0))
    return pl.pallas_call(
        body, name="final_norm_loss", grid=(S // tr,),
        in_specs=[row, row, vec],
        out_specs=[pl.BlockSpec((SUBLANES, LANES), lambda i: (0, 0)), vec, row, row],
        out_shape=[jax.ShapeDtypeStruct((SUBLANES, LANES), F32), jax.ShapeDtypeStruct((1, D), F32),
                   jax.ShapeDtypeStruct((S, D), F32), jax.ShapeDtypeStruct((S, D), BF16)],
        compiler_params=_cparams("arbitrary"),
    )(x2, target, g.reshape(1, D))


def _lru_gates(xa, bd_j, ba_j, bx_j, sp_j):
    z = jnp.dot(xa.astype(BF16), bd_j, preferred_element_type=F32)
    r = _sigmoid(z[:, :LANES] + ba_j)
    ig = _sigmoid(z[:, LANES:] + bx_j)
    log_a = (-LRU_C) * r * sp_j
    a = jnp.exp(log_a)
    mult = jnp.sqrt(-_expm1_neg(2.0 * log_a))
    return r, ig, a, mult


def _conv_rows(xpad, cw_ref, cb_ref, sl, tc):
    out = jnp.broadcast_to(cb_ref[:, sl], (tc, LANES))
    for k in range(CONV_W):
        out = out + xpad[pl.ds(SUBLANES - (CONV_W - 1) + k, tc), sl] * cw_ref[k:k + 1, sl]
    return out


def _lru_fwd(xg, cw, cb, bd, ba, bx, lam):
    S = xg.shape[0]
    D = D_MODEL
    tc = min(LRU_CHUNK, S)
    hb = tc // SUBLANES

    def body(xl_ref, halo_ref, g_ref, cw_ref, cb_ref, bd_ref, ba_ref, bx_ref, lam_ref,
             h_ref, y_ref, xpad, a_s, b_s, carry):
        i = pl.program_id(0)

        @pl.when(i == 0)
        def _():
            carry[...] = jnp.zeros_like(carry)

        xpad[0:SUBLANES, :] = jnp.where(i > 0, halo_ref[...], 0.0)
        xpad[SUBLANES:, :] = xl_ref[...]
        for j in range(N_GROUPS):
            sl = slice(LANES * j, LANES * (j + 1))
            xa = _conv_rows(xpad, cw_ref, cb_ref, sl, tc)
            sp = _softplus(-lam_ref[:, sl])
            _, ig, a, mult = _lru_gates(xa, bd_ref[j], ba_ref[:, sl], bx_ref[:, sl], sp)
            a_s[:, sl] = a
            b_s[:, sl] = mult * (ig * xa)

        row = lax.broadcasted_iota(jnp.int32, (SUBLANES, D), 0)

        def step(t, c):
            o = pl.multiple_of(t * SUBLANES, SUBLANES)
            A = a_s[pl.ds(o, SUBLANES), :]
            B = b_s[pl.ds(o, SUBLANES), :]
            for d in (1, 2, 4):
                keep = row >= d
                a_sh = jnp.where(keep, pltpu.roll(A, d, 0), 1.0)
                b_sh = jnp.where(keep, pltpu.roll(B, d, 0), 0.0)
                B = A * b_sh + B
                A = A * a_sh
            hh = A * c + B
            h_ref[pl.ds(o, SUBLANES), :] = hh
            return jnp.broadcast_to(hh[SUBLANES - 1:SUBLANES, :], (SUBLANES, D))

        carry[...] = lax.fori_loop(0, hb, step, carry[...])
        y_ref[...] = (_gelu(g_ref[...]) * h_ref[...]).astype(BF16)

    row_spec = lambda col: pl.BlockSpec((tc, D), lambda i, col=col: (i, col))
    halo = pl.BlockSpec((SUBLANES, D), lambda i: (jnp.maximum(i * hb - 1, 0), 0))
    full = lambda shape: pl.BlockSpec(shape, lambda i: tuple(0 for _ in shape))
    return pl.pallas_call(
        body, name="lru_fwd", grid=(S // tc,),
        in_specs=[row_spec(0), halo, row_spec(1), full((CONV_W, D)), full((1, D)),
                  full((N_GROUPS, LANES, 2 * LANES)), full((1, D)), full((1, D)), full((1, D))],
        out_specs=[pl.BlockSpec((tc, D), lambda i: (i, 0)), pl.BlockSpec((tc, D), lambda i: (i, 0))],
        out_shape=[jax.ShapeDtypeStruct((S, D), F32), jax.ShapeDtypeStruct((S, D), BF16)],
        scratch_shapes=[pltpu.VMEM((tc + SUBLANES, D), F32), pltpu.VMEM((tc, D), F32),
                        pltpu.VMEM((tc, D), F32), pltpu.VMEM((SUBLANES, D), F32)],
        compiler_params=_cparams("arbitrary"),
    )(xg, xg, xg, cw, cb, bd, ba, bx, lam)


def _lru_bwd(xg, h, dyain, cw, cb, bd, ba, bx, lam):
    S = xg.shape[0]
    D = D_MODEL
    tc = min(LRU_CHUNK, S)
    hb = tc // SUBLANES
    nc = S // tc

    def body(xl_ref, xhalo_ref, g_ref, h_ref, hhalo_ref, dy_ref, cw_ref, cb_ref, bd_ref, ba_ref, bx_ref,
             lam_ref, dxg_ref, dcw_ref, dcb_ref, dba_ref, dbx_ref, dlam_ref, dbd_ref,
             xpad, hpad, a_s, b_s, dh_s, g_s, xa_s, r_s, ig_s, m_s, dxa_pad, carry_e, dxa_head):
        i = pl.program_id(0)
        c = nc - 1 - i

        @pl.when(i == 0)
        def _():
            carry_e[...] = jnp.zeros_like(carry_e)
            dxa_head[...] = jnp.zeros_like(dxa_head)
            for ref in (dcw_ref, dcb_ref, dba_ref, dbx_ref, dlam_ref, dbd_ref):
                ref[...] = jnp.zeros_like(ref)

        xpad[0:SUBLANES, :] = jnp.where(c > 0, xhalo_ref[...], 0.0)
        xpad[SUBLANES:, :] = xl_ref[...]
        hpad[0:SUBLANES, :] = jnp.where(c > 0, hhalo_ref[...], 0.0)
        hpad[SUBLANES:, :] = h_ref[...]

        for j in range(N_GROUPS):
            sl = slice(LANES * j, LANES * (j + 1))
            xa = _conv_rows(xpad, cw_ref, cb_ref, sl, tc)
            sp = _softplus(-lam_ref[:, sl])
            r, ig, a, mult = _lru_gates(xa, bd_ref[j], ba_ref[:, sl], bx_ref[:, sl], sp)
            gl, dgl = _gelu_and_grad(g_ref[:, sl])
            dy = dy_ref[:, sl]
            dh = dy * gl
            dxg_ref[:, D + LANES * j:D + LANES * (j + 1)] = (dy * h_ref[:, sl] * dgl).astype(BF16)
            a_s[:, sl] = a
            b_s[:, sl] = a * dh
            dh_s[:, sl] = dh
            xa_s[:, sl] = xa
            r_s[:, sl] = r
            ig_s[:, sl] = ig
            m_s[:, sl] = mult

        row = lax.broadcasted_iota(jnp.int32, (SUBLANES, D), 0)

        def step(tt, ce):
            o = pl.multiple_of((hb - 1 - tt) * SUBLANES, SUBLANES)
            A = a_s[pl.ds(o, SUBLANES), :]
            B = b_s[pl.ds(o, SUBLANES), :]
            for d in (1, 2, 4):
                keep = row < SUBLANES - d
                a_sh = jnp.where(keep, pltpu.roll(A, SUBLANES - d, 0), 1.0)
                b_sh = jnp.where(keep, pltpu.roll(B, SUBLANES - d, 0), 0.0)
                B = A * b_sh + B
                A = A * a_sh
            e = A * ce + B
            e_next = jnp.where(row < SUBLANES - 1, pltpu.roll(e, SUBLANES - 1, 0), ce)
            g_s[pl.ds(o, SUBLANES), :] = dh_s[pl.ds(o, SUBLANES), :] + e_next
            return jnp.broadcast_to(e[0:1, :], (SUBLANES, D))

        carry_e[...] = lax.fori_loop(0, hb, step, carry_e[...])

        for j in range(N_GROUPS):
            sl = slice(LANES * j, LANES * (j + 1))
            gg = g_s[:, sl]
            xa, r, ig, mult, a = xa_s[:, sl], r_s[:, sl], ig_s[:, sl], m_s[:, sl], a_s[:, sl]
            hprev = hpad[pl.ds(SUBLANES - 1, tc), sl]
            sp = _softplus(-lam_ref[:, sl])
            da = gg * hprev
            dmult = gg * (ig * xa)
            dig = gg * (mult * xa)
            dxa = gg * (mult * ig)
            dla = da * a - dmult * ((a * a) / mult)
            dr = dla * ((-LRU_C) * sp)
            dlam_ref[:, sl] += jnp.sum(dla * r, axis=0, keepdims=True)
            dza = dr * r * (1.0 - r)
            dzx = dig * ig * (1.0 - ig)
            dba_ref[:, sl] += jnp.sum(dza, axis=0, keepdims=True)
            dbx_ref[:, sl] += jnp.sum(dzx, axis=0, keepdims=True)
            dz = jnp.concatenate([dza, dzx], axis=1).astype(BF16)
            dbd_ref[j] += lax.dot_general(xa.astype(BF16), dz, TN_DIMS, preferred_element_type=F32)
            dxa = dxa + lax.dot_general(dz, bd_ref[j], NT_DIMS, preferred_element_type=F32)
            dxa_pad[0:tc, sl] = dxa

        dxa_pad[tc:, :] = dxa_head[...]
        dxa_head[...] = dxa_pad[0:SUBLANES, :]

        for j in range(N_GROUPS):
            sl = slice(LANES * j, LANES * (j + 1))
            dxa = dxa_pad[0:tc, sl]
            dxl = jnp.zeros((tc, LANES), F32)
            for k in range(CONV_W):
                dxl = dxl + dxa_pad[pl.ds(CONV_W - 1 - k, tc), sl] * cw_ref[k:k + 1, sl]
                dcw_ref[k:k + 1, sl] += jnp.sum(
                    dxa * xpad[pl.ds(SUBLANES - (CONV_W - 1) + k, tc), sl], axis=0, keepdims=True)
            dxg_ref[:, sl] = dxl.astype(BF16)
            dcb_ref[:, sl] += jnp.sum(dxa, axis=0, keepdims=True)

        @pl.when(i == nc - 1)
        def _():
            dlam_ref[...] = dlam_ref[...] * (LRU_C * _sigmoid(-lam_ref[...]))

    rev = lambda col: pl.BlockSpec((tc, D), lambda i, col=col: (nc - 1 - i, col))
    halo = pl.BlockSpec((SUBLANES, D), lambda i: (jnp.maximum((nc - 1 - i) * hb - 1, 0), 0))
    full = lambda shape: pl.BlockSpec(shape, lambda i: tuple(0 for _ in shape))
    big = lambda: pltpu.VMEM((tc, D), F32)
    return pl.pallas_call(
        body, name="lru_bwd", grid=(nc,),
        in_specs=[rev(0), halo, rev(1), rev(0), halo, rev(0), full((CONV_W, D)), full((1, D)),
                  full((N_GROUPS, LANES, 2 * LANES)), full((1, D)), full((1, D)), full((1, D))],
        out_specs=[pl.BlockSpec((tc, 2 * D), lambda i: (nc - 1 - i, 0)), full((CONV_W, D)), full((1, D)),
                   full((1, D)), full((1, D)), full((1, D)), full((N_GROUPS, LANES, 2 * LANES))],
        out_shape=[jax.ShapeDtypeStruct((S, 2 * D), BF16), jax.ShapeDtypeStruct((CONV_W, D), F32),
                   jax.ShapeDtypeStruct((1, D), F32), jax.ShapeDtypeStruct((1, D), F32),
                   jax.ShapeDtypeStruct((1, D), F32), jax.ShapeDtypeStruct((1, D), F32),
                   jax.ShapeDtypeStruct((N_GROUPS, LANES, 2 * LANES), F32)],
        scratch_shapes=[pltpu.VMEM((tc + SUBLANES, D), F32), pltpu.VMEM((tc + SUBLANES, D), F32),
                        big(), big(), big(), big(), big(), big(), big(), big(),
                        pltpu.VMEM((tc + SUBLANES, D), F32), pltpu.VMEM((SUBLANES, D), F32),
                        pltpu.VMEM((SUBLANES, D), F32)],
        compiler_params=_cparams("arbitrary"),
    )(xg, xg, xg, h, h, dyain, cw, cb, bd, ba, bx, lam)


def _forget_cumsum(fl, fb):
    S = fl.shape[0]
    tr = min(ROW_TILE, S)
    hb = tr // SUBLANES

    def body(fl_ref, fb_ref, o_ref, rep_ref, lf_s, carry):
        @pl.when(pl.program_id(0) == 0)
        def _():
            carry[...] = jnp.zeros_like(carry)

        lf_s[...] = -_softplus(-(fl_ref[...] + fb_ref[...]))
        row = lax.broadcasted_iota(jnp.int32, (SUBLANES, LANES), 0)

        def step(t, c):
            o = pl.multiple_of(t * SUBLANES, SUBLANES)
            B = lf_s[pl.ds(o, SUBLANES), :]
            for d in (1, 2, 4):
                B = B + jnp.where(row >= d, pltpu.roll(B, d, 0), 0.0)
            B = B + c
            o_ref[pl.ds(o, SUBLANES), :] = B * LOG2E
            return jnp.broadcast_to(B[SUBLANES - 1:SUBLANES, :], (SUBLANES, LANES))

        carry[...] = lax.fori_loop(0, hb, step, carry[...])
        for h in range(N_HEADS):
            rep_ref[h] = jnp.broadcast_to(o_ref[:, h:h + 1], (tr, LANES))

    return pl.pallas_call(
        body, name="forget_cumsum", grid=(S // tr,),
        in_specs=[pl.BlockSpec((tr, LANES), lambda i: (i, 0)), pl.BlockSpec((1, LANES), lambda i: (0, 0))],
        out_specs=[pl.BlockSpec((tr, LANES), lambda i: (i, 0)),
                   pl.BlockSpec((N_HEADS, tr, LANES), lambda i: (0, i, 0))],
        out_shape=[jax.ShapeDtypeStruct((S, LANES), F32), jax.ShapeDtypeStruct((N_HEADS, S, LANES), F32)],
        scratch_shapes=[pltpu.VMEM((tr, LANES), F32), pltpu.VMEM((SUBLANES, LANES), F32)],
        compiler_params=_cparams("arbitrary"),
    )(fl, fb)


def _forget_bwd(dF, fl, fb):
    S = fl.shape[0]
    tr = min(ROW_TILE, S)
    hb = tr // SUBLANES
    nc = S // tr

    def body(df_ref, fl_ref, fb_ref, o_ref, dfb_ref, carry):
        @pl.when(pl.program_id(0) == 0)
        def _():
            carry[...] = jnp.zeros_like(carry)
            dfb_ref[...] = jnp.zeros_like(dfb_ref)

        row = lax.broadcasted_iota(jnp.int32, (SUBLANES, LANES), 0)

        def step(tt, carried):
            c, acc = carried
            o = pl.multiple_of((hb - 1 - tt) * SUBLANES, SUBLANES)
            B = df_ref[pl.ds(o, SUBLANES), :]
            for d in (1, 2, 4):
                B = B + jnp.where(row < SUBLANES - d, pltpu.roll(B, SUBLANES - d, 0), 0.0)
            B = B + c
            z = fl_ref[pl.ds(o, SUBLANES), :] + fb_ref[...]
            dz = B * _sigmoid(-z)
            o_ref[pl.ds(o, SUBLANES), :] = dz.astype(BF16)
            return jnp.broadcast_to(B[0:1, :], (SUBLANES, LANES)), acc + dz

        c, acc = lax.fori_loop(0, hb, step, (carry[...], jnp.zeros((SUBLANES, LANES), F32)))
        carry[...] = c
        dfb_ref[...] += jnp.sum(acc, axis=0, keepdims=True)

    rev = pl.BlockSpec((tr, LANES), lambda i: (nc - 1 - i, 0))
    vec = pl.BlockSpec((1, LANES), lambda i: (0, 0))
    return pl.pallas_call(
        body, name="forget_bwd", grid=(nc,),
        in_specs=[rev, rev, vec],
        out_specs=[rev, vec],
        out_shape=[jax.ShapeDtypeStruct((S, LANES), BF16), jax.ShapeDtypeStruct((1, LANES), F32)],
        scratch_shapes=[pltpu.VMEM((SUBLANES, LANES), F32)],
        compiler_params=_cparams("arbitrary"),
    )(dF, fl, fb)


def _triangle(n, qw, key_major):
    pairs = [(q, k) for q in range(n) for k in range(qw * (q + 1))]
    if key_major:
        pairs.sort(key=lambda qk: (qk[1], qk[0]))
    return (jnp.asarray([q for q, _ in pairs], jnp.int32), jnp.asarray([k for _, k in pairs], jnp.int32))


def _strip_plan(bk, bq, strip, rel):
    plan = []
    for j in range(bq // strip):
        if rel is None:
            plan.append((j, bk, None))
            continue
        reach = strip * (j + 1) - rel * bk
        if reach > 0:
            plan.append((j, min(reach, bk), strip * j - rel * bk if reach <= bk else None))
    return plan


def _strip_scores(k_ref, qt_ref, fk_ref, strip, j, nkeys, mask_off):
    cols = slice(strip * j, strip * (j + 1))
    s = jnp.dot(k_ref[0:nkeys, :], qt_ref[:, cols], preferred_element_type=F32) * (ATTN_SCALE * LOG2E)
    fk = fk_ref[0:nkeys, :]
    s = s - jnp.concatenate([fk] * (strip // LANES), axis=1)
    keep = None
    if mask_off is not None:
        keys = lax.broadcasted_iota(jnp.int32, (nkeys, strip), 0)
        queries = lax.broadcasted_iota(jnp.int32, (nkeys, strip), 1) + mask_off
        keep = keys <= queries
    return s, keep


def _attn_fwd(kv, qkv_t, f_row, f_rep):
    S = kv.shape[0]
    bk = min(ATTN_BLOCK, S)
    strip = min(ATTN_STRIP, bk)
    qw = min(ATTN_Q_TILES, S // bk)
    bq = qw * bk
    tri_q, tri_k = _triangle(S // bq, qw, key_major=False)
    ones_rows = 2 * SUBLANES

    def body(tq_ref, tk_ref, k_ref, qt_ref, vt_ref, fq_ref, fk_ref, ot_ref, lse_ref, m_s, acc_s, vta_s):
        t = pl.program_id(1)
        qi, ki = tq_ref[t], tk_ref[t]
        rel = ki - qw * qi

        @pl.when(ki == 0)
        def _():
            m_s[...] = jnp.full_like(m_s, NEG_BIG)
            acc_s[...] = jnp.zeros_like(acc_s)

        vta_s[0:HEAD_DIM, :] = vt_ref[...]
        vta_s[HEAD_DIM:, :] = jnp.ones((ones_rows, bk), BF16)

        def update(plan):
            scores = lambda entry: _strip_scores(k_ref, qt_ref, fk_ref, strip, *entry)

            def weighted_values(j, nkeys, alpha, pb):
                cols = slice(strip * j, strip * (j + 1))
                acc_s[:, cols] = alpha * acc_s[:, cols] + jnp.dot(
                    vta_s[:, 0:nkeys], pb, preferred_element_type=F32)

            ahead, behind = scores(plan[0]), None
            for i, (j, nkeys, mask_off) in enumerate(plan):
                cols = slice(strip * j, strip * (j + 1))
                (s, keep), ahead = ahead, (scores(plan[i + 1]) if i + 1 < len(plan) else None)
                if behind is not None:
                    weighted_values(*behind)
                if keep is not None:
                    s = jnp.where(keep, s, NEG_BIG)
                fq = fq_ref[:, cols]
                m_old = m_s[:, cols]
                m_new = jnp.maximum(m_old, jnp.max(s, axis=0, keepdims=True) + fq)
                p = jnp.exp2(s - (m_new - fq))
                behind = (j, nkeys, jnp.exp2(m_old - m_new), p.astype(BF16))
                m_s[:, cols] = m_new
            weighted_values(*behind)

        @pl.when(rel < 0)
        def _():
            update(_strip_plan(bk, bq, strip, None))

        for d in range(qw):
            @pl.when(rel == d)
            def _(d=d):
                update(_strip_plan(bk, bq, strip, d))
                if d == qw - 1:
                    denom = acc_s[HEAD_DIM:HEAD_DIM + 1, :]
                    ot_ref[...] = (acc_s[0:HEAD_DIM, :] / denom).astype(BF16)
                    lse_ref[...] = m_s[...] + jnp.log2(denom)

    return pl.pallas_call(
        body, name="attn_fwd",
        grid_spec=pltpu.PrefetchScalarGridSpec(
            num_scalar_prefetch=2, grid=(N_HEADS, tri_q.shape[0]),
            in_specs=[pl.BlockSpec((bk, HEAD_DIM), lambda h, t, tq, tk: (tk[t], h)),
                      pl.BlockSpec((HEAD_DIM, bq), lambda h, t, tq, tk: (h, tq[t])),
                      pl.BlockSpec((HEAD_DIM, bk), lambda h, t, tq, tk: (2 * N_HEADS + h, tk[t])),
                      pl.BlockSpec((None, 1, bq), lambda h, t, tq, tk: (h, 0, tq[t])),
                      pl.BlockSpec((None, bk, LANES), lambda h, t, tq, tk: (h, tk[t], 0))],
            out_specs=[pl.BlockSpec((HEAD_DIM, bq), lambda h, t, tq, tk: (h, tq[t])),
                       pl.BlockSpec((None, 1, bq), lambda h, t, tq, tk: (h, 0, tq[t]))],
            scratch_shapes=[pltpu.VMEM((1, bq), F32), pltpu.VMEM((HEAD_DIM + ones_rows, bq), F32),
                            pltpu.VMEM((HEAD_DIM + ones_rows, bk), BF16)]),
        out_shape=[jax.ShapeDtypeStruct((N_HEADS * HEAD_DIM, S), BF16), jax.ShapeDtypeStruct((N_HEADS, 1, S), F32)],
        compiler_params=_cparams("parallel", "arbitrary"),
    )(tri_q, tri_k, kv, qkv_t, qkv_t, f_row, f_rep)


def _attn_bwd(kv, qkv_t, do_t, o_t, lse, f_row, f_rep):
    S = kv.shape[0]
    bk = min(ATTN_BLOCK, S)
    strip = min(ATTN_STRIP, bk)
    qw = min(ATTN_Q_TILES, S // bk)
    bq = qw * bk
    nq = S // bq
    tri_q, tri_k = _triangle(nq, qw, key_major=True)
    n_tiles = tri_q.shape[0]

    def body(tq_ref, tk_ref, k_ref, v_ref, qt_ref, kt_ref, dot_ref, ot_ref, lse_ref, fq_ref, fk_ref,
             dqt_ref, dkt_ref, dvt_ref, dfk_ref, dfq_ref, dq_s, dk_s, dv_s, dfk_s, dfq_s, row_s):
        t = pl.program_id(1)
        qi, ki = tq_ref[t], tk_ref[t]
        rel = ki - qw * qi

        @pl.when(t == 0)
        def _():
            dq_s[...] = jnp.zeros_like(dq_s)
            dfq_s[...] = jnp.zeros_like(dfq_s)

        @pl.when(rel >= 0)
        def _():
            dk_s[...] = jnp.zeros_like(dk_s)
            dv_s[...] = jnp.zeros_like(dv_s)
            dfk_s[...] = jnp.zeros_like(dfk_s)

        def update(plan):
            row_s[...] = fq_ref[...] - lse_ref[...]

            def matmuls_in(j, nkeys, mask_off):
                s, keep = _strip_scores(k_ref, qt_ref, fk_ref, strip, j, nkeys, mask_off)
                dp = jnp.dot(v_ref[0:nkeys, :], dot_ref[:, strip * j:strip * (j + 1)], preferred_element_type=F32)
                return s, keep, dp

            def matmuls_out(j, nkeys, pb, dsb):
                cols = slice(strip * j, strip * (j + 1))
                dv_s[:, 0:nkeys] += lax.dot_general(dot_ref[:, cols], pb, NT_DIMS, preferred_element_type=F32)
                dk_s[:, 0:nkeys] += lax.dot_general(qt_ref[:, cols], dsb, NT_DIMS, preferred_element_type=F32)
                dq_s[qi, :, cols] += jnp.dot(kt_ref[:, 0:nkeys], dsb, preferred_element_type=F32)

            ahead, behind = matmuls_in(*plan[0]), None
            for i, (j, nkeys, mask_off) in enumerate(plan):
                cols = slice(strip * j, strip * (j + 1))
                (s, keep, dp), ahead = ahead, (matmuls_in(*plan[i + 1]) if i + 1 < len(plan) else None)
                if behind is not None:
                    matmuls_out(*behind)
                p = jnp.exp2(s + row_s[:, cols])
                if keep is not None:
                    p = jnp.where(keep, p, 0.0)
                dot = dot_ref[:, cols]
                delta = jnp.sum(dot.astype(F32) * ot_ref[:, cols].astype(F32), axis=0, keepdims=True)
                ds = p * (dp - delta)
                behind = (j, nkeys, p.astype(BF16), ds.astype(BF16))
                lane_part = ds[:, 0:LANES]
                for g in range(1, strip // LANES):
                    lane_part = lane_part + ds[:, LANES * g:LANES * (g + 1)]
                dfk_s[0:nkeys, :] += lane_part
                sub_part = ds[0:SUBLANES, :]
                for g in range(1, nkeys // SUBLANES):
                    sub_part = sub_part + ds[SUBLANES * g:SUBLANES * (g + 1), :]
                dfq_s[qi, :, cols] += sub_part
            matmuls_out(*behind)

        @pl.when(rel < 0)
        def _():
            update(_strip_plan(bk, bq, strip, None))

        for d in range(qw):
            @pl.when(rel == d)
            def _(d=d):
                update(_strip_plan(bk, bq, strip, d))

        @pl.when(qi == nq - 1)
        def _():
            dkt_ref[...] = (dk_s[...] * ATTN_SCALE).astype(BF16)
            dvt_ref[...] = dv_s[...].astype(BF16)
            dfk_ref[...] = -jnp.sum(dfk_s[...].T, axis=0, keepdims=True)

        @pl.when(t == n_tiles - 1)
        def _():
            for j in range(nq):
                dqt_ref[:, bq * j:bq * (j + 1)] = (dq_s[j] * ATTN_SCALE).astype(BF16)
                dfq_ref[:, bq * j:bq * (j + 1)] = jnp.sum(dfq_s[j], axis=0, keepdims=True)

    q_feat = pl.BlockSpec((HEAD_DIM, bq), lambda h, t, tq, tk: (h, tq[t]))
    q_row = pl.BlockSpec((None, 1, bq), lambda h, t, tq, tk: (h, 0, tq[t]))
    k_feat = pl.BlockSpec((HEAD_DIM, bk), lambda h, t, tq, tk: (h, tk[t]))
    return pl.pallas_call(
        body, name="attn_bwd",
        grid_spec=pltpu.PrefetchScalarGridSpec(
            num_scalar_prefetch=2, grid=(N_HEADS, n_tiles),
            in_specs=[pl.BlockSpec((bk, HEAD_DIM), lambda h, t, tq, tk: (tk[t], h)),
                      pl.BlockSpec((bk, HEAD_DIM), lambda h, t, tq, tk: (tk[t], N_HEADS + h)),
                      q_feat,
                      pl.BlockSpec((HEAD_DIM, bk), lambda h, t, tq, tk: (N_HEADS + h, tk[t])),
                      q_feat, q_feat, q_row, q_row,
                      pl.BlockSpec((None, bk, LANES), lambda h, t, tq, tk: (h, tk[t], 0))],
            out_specs=[pl.BlockSpec((HEAD_DIM, S), lambda h, t, tq, tk: (h, 0)), k_feat, k_feat,
                       pl.BlockSpec((None, 1, bk), lambda h, t, tq, tk: (h, 0, tk[t])),
                       pl.BlockSpec((None, 1, S), lambda h, t, tq, tk: (h, 0, 0))],
            scratch_shapes=[pltpu.VMEM((nq, HEAD_DIM, bq), F32), pltpu.VMEM((HEAD_DIM, bk), F32),
                            pltpu.VMEM((HEAD_DIM, bk), F32), pltpu.VMEM((bk, LANES), F32),
                            pltpu.VMEM((nq, SUBLANES, bq), F32), pltpu.VMEM((1, bq), F32)]),
        out_shape=[jax.ShapeDtypeStruct((N_HEADS * HEAD_DIM, S), BF16)] * 3
        + [jax.ShapeDtypeStruct((N_HEADS, 1, S), F32), jax.ShapeDtypeStruct((N_HEADS, 1, S), F32)],
        compiler_params=_cparams("parallel", "arbitrary"),
    )(tri_q, tri_k, kv, kv, qkv_t, qkv_t, do_t, o_t, lse, f_row, f_rep)


def _gate_mix(gates, ya, yb):
    S, D = ya.shape
    tr = min(ROW_TILE, S)

    def body(ga_ref, gb_ref, ya_ref, yb_ref, o_ref):
        o_ref[...] = (_sigmoid(ga_ref[...]) * ya_ref[...] + _sigmoid(gb_ref[...]) * yb_ref[...]).astype(BF16)

    col = lambda j: pl.BlockSpec((tr, D), lambda i, j=j: (i, j))
    return pl.pallas_call(
        body, name="gate_mix", grid=(S // tr,),
        in_specs=[col(0), col(1), col(0), col(0)],
        out_specs=col(0),
        out_shape=jax.ShapeDtypeStruct((S, D), BF16),
        compiler_params=_cparams("parallel"),
    )(gates, gates, ya, yb)


def _gate_bwd(dmix, gates, ya, yb):
    S, D = ya.shape
    tr = min(ROW_TILE, S)

    def body(dm_ref, ga_ref, gb_ref, ya_ref, yb_ref, dya_ref, dyb_ref, dg_ref):
        dm = dm_ref[...]
        sa, sb = _sigmoid(ga_ref[...]), _sigmoid(gb_ref[...])
        dya_ref[...] = (dm * sa).astype(BF16)
        dyb_ref[...] = (dm * sb).astype(BF16)
        dg_ref[:, 0:D] = ((dm * ya_ref[...]) * (sa * (1.0 - sa))).astype(BF16)
        dg_ref[:, D:] = ((dm * yb_ref[...]) * (sb * (1.0 - sb))).astype(BF16)

    col = lambda j: pl.BlockSpec((tr, D), lambda i, j=j: (i, j))
    return pl.pallas_call(
        body, name="gate_bwd", grid=(S // tr,),
        in_specs=[col(0), col(0), col(1), col(0), col(0)],
        out_specs=[col(0), col(0), pl.BlockSpec((tr, 2 * D), lambda i: (i, 0))],
        out_shape=[jax.ShapeDtypeStruct((S, D), BF16), jax.ShapeDtypeStruct((S, D), BF16),
                   jax.ShapeDtypeStruct((S, 2 * D), BF16)],
        compiler_params=_cparams("parallel"),
    )(dmix, gates, gates, ya, yb)


def _mesh_place():
    x, y, c = lax.axis_index("x"), lax.axis_index("y"), lax.axis_index("c")
    chips = [(1 - x, y), (x, 1 - y), (1 - x, 1 - y)]
    return x, y, c, chips


def _all_gather(shards):
    n = len(shards)

    def body(*refs):
        ins, outs = refs[:n], refs[n:2 * n]
        send_sems, recv_sems, local_sems = refs[2 * n:]
        x, y, c, chips = _mesh_place()
        me, sib = (x, y, c), (x, y, 1 - c)

        def copy(a, k, block, to, src=None):
            px, py, pc = block
            dst = outs[a].at[4 * px + 2 * py + pc]
            return pltpu.make_async_remote_copy(
                src_ref=dst if src is None else src, dst_ref=dst,
                send_sem=send_sems.at[a, k], recv_sem=recv_sems.at[a, k],
                device_id=to, device_id_type=MESH_ID)

        mine = [pltpu.make_async_copy(ins[a], outs[a].at[4 * x + 2 * y + c], local_sems.at[a]) for a in range(n)]
        for cp in mine:
            cp.start()
        first = []
        for a in range(n):
            first.append(copy(a, 0, me, sib, src=ins[a]))
            for j, chip in enumerate(chips):
                first.append(copy(a, 1 + j, me, (*chip, c), src=ins[a]))
        for cp in first:
            cp.start()
        passed = []
        for j, chip in enumerate(chips):
            for a in range(n):
                copy(a, 1 + j, (*chip, c), me).wait_recv()
                fwd = copy(a, 4 + j, (*chip, c), sib)
                fwd.start()
                passed.append(fwd)
        for a in range(n):
            copy(a, 0, sib, me).wait_recv()
            for j, chip in enumerate(chips):
                copy(a, 4 + j, (*chip, 1 - c), me).wait_recv()
        for cp in first + passed:
            cp.wait_send()
        for cp in mine:
            cp.wait()

    return pl.pallas_call(
        body, name="all_gather_weights",
        in_specs=[ANY] * n, out_specs=[ANY] * n,
        out_shape=[jax.ShapeDtypeStruct((N_DEV,) + s.shape, s.dtype) for s in shards],
        scratch_shapes=[pltpu.SemaphoreType.DMA((n, 7)), pltpu.SemaphoreType.DMA((n, 7)),
                        pltpu.SemaphoreType.DMA((n,))],
    )(*shards)


def _chip_partial_sum(blocks, got, core):
    R, C = got.shape[1:]
    tr = min(256, R)
    assert R % tr == 0

    def body(core_ref, a_ref, b_ref, s_ref, sb_ref):
        s = a_ref[...] + b_ref[...]
        s_ref[...] = s
        sb_ref[...] = s.astype(BF16)

    blk = pl.BlockSpec((None, tr, C), lambda k, i, core_ref: (k, i, 0))
    return pl.pallas_call(
        body, name="chip_partial_sum",
        grid_spec=pltpu.PrefetchScalarGridSpec(
            num_scalar_prefetch=1, grid=(4, R // tr),
            in_specs=[pl.BlockSpec((None, tr, C), lambda k, i, core_ref: (2 * k + core_ref[0], i, 0)), blk],
            out_specs=[blk, blk]),
        out_shape=[jax.ShapeDtypeStruct(got.shape, F32), jax.ShapeDtypeStruct(got.shape, BF16)],
        compiler_params=_cparams("parallel", "parallel"),
    )(core, blocks, got)


HBM_SPEC = pl.BlockSpec(memory_space=pltpu.HBM)
SEM_SPEC = pl.BlockSpec(memory_space=pltpu.SEMAPHORE)
FLIPS = [(dx, dy, dc) for dx in (0, 1) for dy in (0, 1) for dc in (0, 1) if (dx, dy, dc) != (0, 0, 0)]


def _flip(v, d):
    return 1 - v if d else v


def _gather_copies(srcs, lands, send_sems, recv_sems):
    x, y, c, _ = _mesh_place()
    sends, recvs = [], []
    for a in range(len(srcs)):
        for k, (dx, dy, dc) in enumerate(FLIPS):
            px, py, pc = _flip(x, dx), _flip(y, dy), _flip(c, dc)
            sem = len(FLIPS) * a + k
            common = dict(send_sem=send_sems.at[sem], recv_sem=recv_sems.at[sem],
                          device_id=(px, py, pc), device_id_type=MESH_ID)
            sends.append(pltpu.make_async_remote_copy(
                src_ref=srcs[a], dst_ref=lands[a].at[4 * x + 2 * y + c], **common))
            recvs.append(pltpu.make_async_remote_copy(
                src_ref=srcs[a], dst_ref=lands[a].at[4 * px + 2 * py + pc], **common))
    return sends, recvs


def _cores_copies(srcs, lands, send_sems, recv_sems):
    x, y, c, _ = _mesh_place()
    copies = []
    for a in range(len(srcs)):
        for k in range(4):
            copies.append(pltpu.make_async_remote_copy(
                src_ref=srcs[a].at[2 * k + (1 - c)], dst_ref=lands[a].at[k],
                send_sem=send_sems.at[4 * a + k], recv_sem=recv_sems.at[4 * a + k],
                device_id=(x, y, 1 - c), device_id_type=MESH_ID))
    return copies, copies


def _scatter_copies(srcs, lands, send_sems, recv_sems):
    x, y, c, chips = _mesh_place()
    sends = []
    for a in range(len(srcs)):
        for j, (px, py) in enumerate(chips):
            sends.append(pltpu.make_async_remote_copy(
                src_ref=srcs[a].at[2 * px + py], dst_ref=lands[a].at[j],
                send_sem=send_sems.at[3 * a + j], recv_sem=recv_sems.at[3 * a + j],
                device_id=(px, py, c), device_id_type=MESH_ID))
    return sends, sends


def _exchange_start(srcs, land_shapes, copies, n_copies, name):
    n = len(srcs)

    def body(*refs):
        src_refs, land_refs = refs[:n], refs[n:2 * n]
        send_sems, recv_sems = refs[2 * n], refs[2 * n + 1]
        token = refs[-1]
        sends, _ = copies(src_refs, land_refs, send_sems, recv_sems)
        for cp in sends:
            cp.start()
        token[...] = jnp.zeros_like(token)

    lands = [pltpu.with_memory_space_constraint(lax.empty(s.shape, s.dtype), pltpu.HBM) for s in land_shapes]
    srcs = [pltpu.with_memory_space_constraint(s, pltpu.HBM) for s in srcs]
    res = pl.pallas_call(
        body, name=name,
        out_shape=(pltpu.SemaphoreType.DMA((n * n_copies,)), pltpu.SemaphoreType.DMA((n * n_copies,)),
                   *[pltpu.HBM(s.shape, s.dtype) for s in srcs], *[pltpu.HBM(s.shape, s.dtype) for s in land_shapes],
                   jax.ShapeDtypeStruct((SUBLANES, LANES), F32)),
        in_specs=[HBM_SPEC] * (2 * n),
        out_specs=(SEM_SPEC, SEM_SPEC, *[HBM_SPEC] * (2 * n), pl.BlockSpec(memory_space=pltpu.VMEM)),
        input_output_aliases={i: 2 + i for i in range(2 * n)},
        compiler_params=pltpu.CompilerParams(has_side_effects=pltpu.SideEffectType.DATAFLOW_SIDE_EFFECTING),
    )(*srcs, *lands)
    return res[0], res[1], list(res[2:2 + n]), list(res[2 + n:2 + 2 * n]), res[-1]


def _exchange_wait(started, copies, after, name):
    send_sems, recv_sems, srcs, lands, _ = started
    n = len(srcs)

    def body(*refs):
        src_refs, land_refs = refs[:n], refs[n:2 * n]
        send_ref, recv_ref = refs[2 * n], refs[2 * n + 1]
        sends, recvs = copies(src_refs, land_refs, send_ref, recv_ref)
        for cp in sends:
            cp.wait_send()
        for cp in recvs:
            cp.wait_recv()

    res = pl.pallas_call(
        body, name=name,
        out_shape=tuple(pltpu.HBM(s.shape, s.dtype) for s in srcs + lands),
        in_specs=[HBM_SPEC] * (2 * n) + [SEM_SPEC, SEM_SPEC, ANY],
        out_specs=tuple([HBM_SPEC] * (2 * n)),
        input_output_aliases={i: i for i in range(2 * n)},
        compiler_params=pltpu.CompilerParams(has_side_effects=pltpu.SideEffectType.DATAFLOW_SIDE_EFFECTING),
    )(*srcs, *lands, send_sems, recv_sems, after)
    return list(res[:n]), list(res[n:])


def _all_reduce_small(vec):
    R = vec.shape[0]

    def body(v_ref, o_ref, sib_buf, chip_buf, send_sems, recv_sems):
        x, y, c, chips = _mesh_place()
        swap = pltpu.make_async_remote_copy(
            src_ref=v_ref, dst_ref=sib_buf, send_sem=send_sems.at[0], recv_sem=recv_sems.at[0],
            device_id=(x, y, 1 - c), device_id_type=MESH_ID)
        swap.start()
        swap.wait()
        my_chip = 2 * x + y
        chip_buf[my_chip] = v_ref[...] + sib_buf[...]
        sends = []
        for j, (px, py) in enumerate(chips):
            cp = pltpu.make_async_remote_copy(
                src_ref=chip_buf.at[my_chip], dst_ref=chip_buf.at[my_chip],
                send_sem=send_sems.at[1 + j], recv_sem=recv_sems.at[1 + j],
                device_id=(px, py, c), device_id_type=MESH_ID)
            cp.start()
            sends.append(cp)
        for j, (px, py) in enumerate(chips):
            pltpu.make_async_remote_copy(
                src_ref=chip_buf.at[2 * px + py], dst_ref=chip_buf.at[2 * px + py],
                send_sem=send_sems.at[1 + j], recv_sem=recv_sems.at[1 + j],
                device_id=(px, py, c), device_id_type=MESH_ID).wait_recv()
        for cp in sends:
            cp.wait_send()
        o_ref[...] = ((chip_buf[0] + chip_buf[1]) + chip_buf[2]) + chip_buf[3]

    vm = pl.BlockSpec(memory_space=pltpu.VMEM)
    return pl.pallas_call(
        body, name="all_reduce_small",
        in_specs=[vm], out_specs=vm,
        out_shape=jax.ShapeDtypeStruct(vec.shape, F32),
        scratch_shapes=[pltpu.VMEM((R, LANES), F32), pltpu.VMEM((4, R, LANES), F32),
                        pltpu.SemaphoreType.DMA((4,)), pltpu.SemaphoreType.DMA((4,))],
    )(vec)


def _adamw_math(w, g, m, v):
    m = ADAM_B1 * m + (1.0 - ADAM_B1) * g
    v = ADAM_B2 * v + (1.0 - ADAM_B2) * (g * g)
    m_hat = m / (1.0 - ADAM_B1 ** ADAM_STEP)
    v_hat = v / (1.0 - ADAM_B2 ** ADAM_STEP)
    delta = -ADAM_LR * (m_hat / (jnp.sqrt(v_hat) + ADAM_EPS) + ADAM_WD * w)
    return delta, m, v


def _adamw(w, m, v, g_own, g_got, chip, name):
    R, C = w.shape
    tr = R if R * C <= 256 * D_MODEL else 256
    assert R % tr == 0
    n_got = g_got.shape[0]

    def body(*refs):
        w_ref, m_ref, v_ref, go_ref = refs[1:5]
        got = refs[5:5 + n_got]
        g_ref, d_ref, nm_ref, nv_ref = refs[5 + n_got:]
        g = go_ref[...]
        for r in got:
            g = g + r[...].astype(F32)
        delta, m_new, v_new = _adamw_math(w_ref[...], g, m_ref[...], v_ref[...])
        g_ref[...] = g
        d_ref[...] = delta
        nm_ref[...] = m_new
        nv_ref[...] = v_new

    blk = pl.BlockSpec((tr, C), lambda i, chip_ref: (i, 0))
    own_spec = pl.BlockSpec((None, tr, C), lambda i, chip_ref: (chip_ref[0], i, 0))
    got_specs = [pl.BlockSpec((None, tr, C), lambda i, chip_ref, j=j: (j, i, 0)) for j in range(n_got)]
    return pl.pallas_call(
        body, name=name,
        grid_spec=pltpu.PrefetchScalarGridSpec(
            num_scalar_prefetch=1, grid=(R // tr,),
            in_specs=[blk] * 3 + [own_spec] + got_specs, out_specs=[blk] * 4),
        out_shape=[jax.ShapeDtypeStruct((R, C), F32)] * 4,
        compiler_params=_cparams("parallel"),
    )(chip, w, m, v, g_own, *([g_got] * n_got))


def _block_diag_pairs(wa, wx):
    def pairs(w):
        w = w.reshape(N_GROUPS, 2, LRU_BW, LRU_BW)
        z = jnp.zeros((N_GROUPS, LRU_BW, LRU_BW), w.dtype)
        top = jnp.concatenate([w[:, 0], z], axis=2)
        bot = jnp.concatenate([z, w[:, 1]], axis=2)
        return jnp.concatenate([top, bot], axis=1)
    return jnp.concatenate([pairs(wa), pairs(wx)], axis=2).astype(BF16)


def _block_diag_unpair(dbd):
    def unpair(g):
        blocks = jnp.stack([g[:, :LRU_BW, :LRU_BW], g[:, LRU_BW:, LRU_BW:]], axis=1)
        return blocks.reshape(LRU_BLOCKS, LRU_BW, LRU_BW)
    return unpair(dbd[:, :, :LANES]), unpair(dbd[:, :, LANES:])


def _local_step(x, target, W, small, late_weights=None, hooks=None):
    def hook(name, *args):
        return hooks[name](*args) if hooks is not None else (None, 0.0)

    S, D = x.shape
    g1, g2, g3 = small["norm_mix_g"], small["norm_mlp_g"], small["norm_final_g"]
    cw, cb = small["conv_w"], small["conv_b"].reshape(1, D)
    ba, bx, lam = (small[k].reshape(1, D) for k in ("lru_ba", "lru_bx", "lru_lambda"))
    fb = jnp.pad(small["forget_b"], (0, LANES - N_HEADS)).reshape(1, LANES)
    bd = _block_diag_pairs(small["lru_wa"], small["lru_wx"])
    big = dict(tm=1024, tn=1024)

    u = _norm_fwd(x, g1, "norm_mix")
    (xg,) = _mm([(u, W["in_xg"])], tks=[D], outs=[F32], name="proj_xg", **big)
    (qkv_t,) = _mm([(W["in_qkv_t"], u)], tb=True, tks=[D], outs=[BF16], name="proj_qkv_t", **big)
    (kv,) = _mm([(u, W["in_kv"])], tks=[D], outs=[BF16], name="proj_kv", **big)
    (gates,) = _mm([(u, W["in_gates"])], tks=[D], outs=[F32], name="proj_gates", **big)
    (fl,) = _mm([(u, W["in_f"])], tks=[D], outs=[F32], name="proj_forget", **big)
    h, yain = _lru_fwd(xg, cw, cb, bd, ba, bx, lam)
    fcum, f_rep = _forget_cumsum(fl, fb)
    f_row = fcum[:, :N_HEADS].T.reshape(N_HEADS, 1, S)
    ob_t, lse = _attn_fwd(kv, qkv_t, f_row, f_rep)
    if late_weights is not None:
        W = {**W, **late_weights(lse)}
    (ya,) = _mm([(yain, W["branch_a"])], tks=[D], outs=[F32], name="branch_a", **big)
    (yb,) = _mm([(ob_t, W["branch_b"])], ta=True, tks=[D], outs=[F32], name="branch_b", **big)
    mix = _gate_mix(gates, ya, yb)
    (x1,) = _mm([(mix, W["out"])], tks=[D], outs=[F32], name="out_proj", extra=(x,),
                epi=lambda acc, res: (res + acc,), **big)
    m = _norm_fwd(x1, g2, "norm_mlp")
    relu, hh = _mm([(m, W["up"])], tks=[D], outs=[BF16, BF16], name="mlp_up",
                   epi=lambda acc: (jnp.maximum(acc, 0.0), jnp.square(jnp.maximum(acc, 0.0))), **big)
    deep = dict(tm=512, tn=1024, tks=[D_FF])
    wgrad = dict(tm=1024, tn=512, tks=[min(4096, S)])
    (x2,) = _mm([(hh, W["down"])], outs=[F32], name="mlp_down", extra=(x1,),
                epi=lambda acc, res: (res + acc,), **deep)
    loss_acc, dg3, dx2, dx2b = _final_norm_loss(x2, target, g3)

    (dhpre,) = _mm([(dx2b, W["down"])], tb=True, tks=[D], outs=[BF16], name="d_mlp_act", extra=(relu,),
                   epi=lambda acc, r: (acc * (2.0 * r.astype(F32)),), **big)
    (dw_down,) = _mm([(hh, dx2b)], ta=True, outs=[F32], name="dw_down", **wgrad)
    (dm,) = _mm([(dhpre, W["up"])], tb=True, outs=[F32], name="d_mlp_in", **deep)
    assert wgrad["tn"] == D_FF // N_DEV
    (dw_up,) = _mm([(m, dhpre)], ta=True, outs=[F32], name="dw_up", col_blocked=True, **wgrad)
    dx1, dx1b, dg2 = _norm_bwd(dm, x1, g2, dx2, "norm_mlp_bwd")
    (dmix,) = _mm([(dx1b, W["out"])], tb=True, tks=[D], outs=[F32], name="d_mix", **big)
    (dw_out,) = _mm([(mix, dx1b)], ta=True, outs=[F32], name="dw_out", **wgrad)
    dya, dyb, dgates = _gate_bwd(dmix, gates, ya, yb)
    (dob_t,) = _mm([(W["branch_b"], dyb)], tb=True, tks=[D], outs=[BF16], name="d_attn_out_t", **big)
    (dw_b,) = _mm([(ob_t, dyb)], outs=[F32], name="dw_branch_b", **wgrad)
    (dyain,) = _mm([(dya, W["branch_a"])], tb=True, tks=[D], outs=[F32], name="d_lru_out", **big)
    (dw_a,) = _mm([(yain, dya)], ta=True, outs=[F32], name="dw_branch_a", **wgrad)
    early = dict(w_branch_a=dw_a, w_branch_b=dw_b, w_out=dw_out, w_up=dw_up, w_down=dw_down)
    early_state, zero = hook("early_start", early)
    dq_t, dk_t, dv_t, dfk, dfq = _attn_bwd(kv, qkv_t, dob_t, ob_t, lse + zero, f_row, f_rep)
    early_state, zero = hook("early_mid", early_state, dfq)
    dF = jnp.pad((dfk.reshape(N_HEADS, S) + dfq.reshape(N_HEADS, S)).T, ((0, 0), (0, LANES - N_HEADS)))
    dfl, dfb = _forget_bwd(dF, fl, fb)
    dxg, dcw, dcb, dba, dbx, dlam, dbd = _lru_bwd(xg, h, dyain, cw, cb, bd, ba, bx, lam + zero)
    dw_in_parts = [
        _mm([(u, dxg)], ta=True, outs=[F32], name="dw_in_xg", **wgrad)[0],
        _mm([(dq_t, u)], outs=[F32], name="dw_in_q_t", **wgrad)[0].T,
        _mm([(dk_t, u)], outs=[F32], name="dw_in_k_t", **wgrad)[0].T,
        _mm([(dv_t, u)], outs=[F32], name="dw_in_v_t", **wgrad)[0].T,
        _mm([(u, dgates)], ta=True, outs=[F32], name="dw_in_gates", **wgrad)[0],
        _mm([(u, dfl)], ta=True, outs=[F32], name="dw_in_forget", **wgrad)[0][:, :N_HEADS],
    ]
    dw_in = jnp.concatenate(dw_in_parts, axis=1)
    in_state, zero = hook("in_start", dw_in)
    wq_t, wk_t, wv_t = (W["in_qkv_t"][D * i:D * (i + 1)] for i in range(3))
    (du_tok,) = _mm([(dxg, W["in_xg"]), (dgates, W["in_gates"]), (dfl, W["in_f"] + jnp.asarray(zero, BF16))],
                    tb=True, tks=[2 * D, 2 * D, LANES], outs=[F32], name="d_norm_mix_out_tok", tm=1024, tn=512)
    in_state, zero = hook("in_mid", in_state, du_tok)
    (du,) = _mm([(dq_t, wq_t + jnp.asarray(zero, BF16)), (dk_t, wk_t), (dv_t, wv_t)], ta=True, tks=[D, D, D],
                outs=[F32], name="d_norm_mix_out", extra=(du_tok,), epi=lambda acc, prev: (prev + acc,),
                tm=1024, tn=512)
    grad_x, _, dg1 = _norm_bwd(du, x, g1, dx1, "norm_mix_bwd")

    dwa, dwx = _block_diag_unpair(dbd)
    big_grads = dict(early, w_in=dw_in)
    small_grads = dict(norm_mix_g=dg1.reshape(D), conv_w=dcw, conv_b=dcb.reshape(D), lru_wa=dwa, lru_ba=dba.reshape(D),
                       lru_wx=dwx, lru_bx=dbx.reshape(D), lru_lambda=dlam.reshape(D), forget_b=dfb[0, :N_HEADS],
                       norm_mlp_g=dg2.reshape(D), norm_final_g=dg3.reshape(D))
    return loss_acc[0, 0], grad_x, big_grads, small_grads, (early_state, in_state)


SMALL_NAMES = ("norm_mix_g", "conv_b", "lru_wa", "lru_ba", "lru_wx", "lru_bx", "lru_lambda", "forget_b",
               "norm_mlp_g", "norm_final_g")
TILE_ELEMS = SUBLANES * LANES


def _pack_small(parts):
    rows = []
    for p in parts:
        flat = p.reshape(-1)
        flat = jnp.pad(flat, (0, (-flat.shape[0]) % TILE_ELEMS))
        rows.append(flat.reshape(-1, LANES))
    return jnp.concatenate(rows, axis=0)


def _packed_rows(shape):
    return -(-math.prod(shape) // TILE_ELEMS) * SUBLANES


def _adamw_small(g_packed, g_conv_w, weights, moms, vels):
    def rows_view(a):
        flat = a.reshape(-1)
        flat = jnp.pad(flat, (0, (-flat.shape[0]) % LANES))
        return flat.reshape(-1, LANES)

    names = SMALL_NAMES + ("conv_w",)
    views = [[rows_view(src[k]) for k in names] for src in (weights, moms, vels)]
    n = len(names)
    starts, r = [], 0
    for k in SMALL_NAMES:
        starts.append(r)
        r += _packed_rows(weights[k].shape)

    def body(*refs):
        gp_ref, gc_ref = refs[0], refs[1]
        w_refs, m_refs, v_refs = refs[2:2 + n], refs[2 + n:2 + 2 * n], refs[2 + 2 * n:2 + 3 * n]
        outs = refs[2 + 3 * n:]
        for i in range(n):
            rows = w_refs[i].shape[0]
            g = gc_ref[...] if i == n - 1 else gp_ref[starts[i]:starts[i] + rows, :]
            delta, m_new, v_new = _adamw_math(w_refs[i][...], g, m_refs[i][...], v_refs[i][...])
            for o_ref, val in zip(outs[4 * i:4 * i + 4], (g, delta, m_new, v_new)):
                o_ref[...] = val

    vm = pl.BlockSpec(memory_space=pltpu.VMEM)
    out_shape = [jax.ShapeDtypeStruct(v.shape, F32) for v in views[0] for _ in range(4)]
    res = pl.pallas_call(
        body, name="adamw_small",
        in_specs=[vm] * (2 + 3 * n), out_specs=[vm] * (4 * n), out_shape=out_shape,
    )(g_packed, g_conv_w, *views[0], *views[1], *views[2])
    dicts = ({}, {}, {}, {})
    for i, k in enumerate(names):
        size = math.prod(weights[k].shape)
        for d, arr in zip(dicts, res[4 * i:4 * i + 4]):
            d[k] = arr.reshape(-1)[:size].reshape(weights[k].shape)
    return dicts


BIG_NAMES = ("w_in", "w_branch_a", "w_branch_b", "w_out", "w_up", "w_down")
WEIGHT_ORDER = ("norm_mix_g", "w_in", "conv_w", "conv_b", "lru_wa", "lru_ba", "lru_wx", "lru_bx", "lru_lambda",
                "forget_b", "w_branch_a", "w_branch_b", "w_out", "norm_mlp_g", "w_up", "w_down", "norm_final_g")


def _to_dest_blocks(name, g):
    if g.ndim == 3:
        return g
    if name in ("w_in", "w_up"):
        return g.reshape(g.shape[0], N_DEV, g.shape[1] // N_DEV).transpose(1, 0, 2)
    return g.reshape(N_DEV, g.shape[0] // N_DEV, g.shape[1])


def kernel(x, norm_mix_g, w_in, conv_w, conv_b, lru_wa, lru_ba, lru_wx, lru_bx, lru_lambda, forget_b, w_branch_a, w_branch_b, w_out, norm_mlp_g, w_up, w_down, norm_final_g, loss_target, m_norm_mix_g, m_w_in, m_conv_w, m_conv_b, m_lru_wa, m_lru_ba, m_lru_wx, m_lru_bx, m_lru_lambda, m_forget_b, m_w_branch_a, m_w_branch_b, m_w_out, m_norm_mlp_g, m_w_up, m_w_down, m_norm_final_g, v_norm_mix_g, v_w_in, v_conv_w, v_conv_b, v_lru_wa, v_lru_ba, v_lru_wx, v_lru_bx, v_lru_lambda, v_forget_b, v_w_branch_a, v_w_branch_b, v_w_out, v_norm_mlp_g, v_w_up, v_w_down, v_norm_final_g):
    weights = dict(norm_mix_g=norm_mix_g, w_in=w_in, conv_w=conv_w, conv_b=conv_b, lru_wa=lru_wa, lru_ba=lru_ba,
                   lru_wx=lru_wx, lru_bx=lru_bx, lru_lambda=lru_lambda, forget_b=forget_b, w_branch_a=w_branch_a,
                   w_branch_b=w_branch_b, w_out=w_out, norm_mlp_g=norm_mlp_g, w_up=w_up, w_down=w_down,
                   norm_final_g=norm_final_g)
    moms = dict(norm_mix_g=m_norm_mix_g, w_in=m_w_in, conv_w=m_conv_w, conv_b=m_conv_b, lru_wa=m_lru_wa,
                lru_ba=m_lru_ba, lru_wx=m_lru_wx, lru_bx=m_lru_bx, lru_lambda=m_lru_lambda, forget_b=m_forget_b,
                w_branch_a=m_w_branch_a, w_branch_b=m_w_branch_b, w_out=m_w_out, norm_mlp_g=m_norm_mlp_g,
                w_up=m_w_up, w_down=m_w_down, norm_final_g=m_norm_final_g)
    vels = dict(norm_mix_g=v_norm_mix_g, w_in=v_w_in, conv_w=v_conv_w, conv_b=v_conv_b, lru_wa=v_lru_wa,
                lru_ba=v_lru_ba, lru_wx=v_lru_wx, lru_bx=v_lru_bx, lru_lambda=v_lru_lambda, forget_b=v_forget_b,
                w_branch_a=v_w_branch_a, w_branch_b=v_w_branch_b, w_out=v_w_out, norm_mlp_g=v_norm_mlp_g,
                w_up=v_w_up, w_down=v_w_down, norm_final_g=v_norm_final_g)
    S, D = x.shape[1], x.shape[2]
    me = 4 * lax.axis_index("x") + 2 * lax.axis_index("y") + lax.axis_index("c")

    core = lax.axis_index("c").astype(jnp.int32).reshape(1)
    chip = (2 * lax.axis_index("x") + lax.axis_index("y")).astype(jnp.int32).reshape(1)
    late_names = BIG_NAMES[1:]

    win_g, cw_g = _all_gather([w_in.astype(BF16), conv_w])
    late_shards = [weights[k].astype(BF16) for k in late_names]
    gather = _exchange_start(late_shards, [jax.ShapeDtypeStruct((N_DEV,) + s.shape, BF16) for s in late_shards],
                             _gather_copies, len(FLIPS), "gather_late_start")
    w_in_full = win_g.transpose(1, 0, 2).reshape(D, -1)
    cuts = (0, 2 * D, 5 * D, 7 * D)
    W = dict(in_xg=w_in_full[:, cuts[0]:cuts[1]], in_qkv_t=w_in_full[:, cuts[1]:cuts[2]].T,
             in_kv=w_in_full[:, cuts[1] + D:cuts[2]], in_gates=w_in_full[:, cuts[2]:cuts[3]],
             in_f=jnp.pad(w_in_full[:, cuts[3]:], ((0, 0), (0, LANES - N_HEADS))))
    small = {k: weights[k] for k in SMALL_NAMES}
    small["conv_w"] = cw_g.transpose(1, 0, 2).reshape(CONV_W, D)
    small["norm_mix_g"] = norm_mix_g + gather[4][0, 0]

    def late_weights(after):
        shards, lands = _exchange_wait(gather, _gather_copies, after, "gather_late_wait")
        wa_g, wb_g, wo_g, wup_g, wdn_g = (
            lax.dynamic_update_slice_in_dim(land, shard[None], me, axis=0) for land, shard in zip(lands, shards))
        return dict(branch_a=wa_g.reshape(D, D), branch_b=wb_g.reshape(D, D), out=wo_g.reshape(D, D),
                    up=wup_g.transpose(1, 0, 2).reshape(D, D_FF), down=wdn_g.reshape(D_FF, D))

    def cores_start(names, grads_by_name, tag):
        blocks = [_to_dest_blocks(k, grads_by_name[k]) for k in names]
        started = _exchange_start(blocks, [jax.ShapeDtypeStruct((4,) + b.shape[1:], F32) for b in blocks],
                                  _cores_copies, 4, "cores_" + tag + "_start")
        return started, started[4][0, 0]

    def chips_start(started, after, tag):
        blocks, got = _exchange_wait(started, _cores_copies, after, "cores_" + tag + "_wait")
        sums = [_chip_partial_sum(b, g, core) for b, g in zip(blocks, got)]
        wire = [s[1] for s in sums]
        scatter = _exchange_start(wire, [jax.ShapeDtypeStruct((3,) + s.shape[1:], BF16) for s in wire],
                                  _scatter_copies, 3, "scatter_" + tag + "_start")
        return (sums, scatter), scatter[4][0, 0]

    hooks = dict(early_start=lambda g: cores_start(late_names, g, "early"),
                 early_mid=lambda st, after: chips_start(st, after, "early"),
                 in_start=lambda g: cores_start(BIG_NAMES[:1], dict(w_in=g), "w_in"),
                 in_mid=lambda st, after: chips_start(st, after, "w_in"))
    loss_part, grad_x, _, small_grads, ((early_sums, early_scatter), (in_sums, in_scatter)) = _local_step(
        x.reshape(S, D), loss_target.reshape(S, D), W, small, late_weights, hooks)
    loss = lax.psum(loss_part, MESH_AXES)
    _, early_others = _exchange_wait(early_scatter, _scatter_copies, grad_x, "scatter_early_wait")
    _, in_others = _exchange_wait(in_scatter, _scatter_copies, grad_x, "scatter_w_in_wait")
    sums = list(in_sums) + list(early_sums)
    others = list(in_others) + list(early_others)

    reduced = _all_reduce_small(_pack_small([small_grads[k] for k in SMALL_NAMES] + [small_grads["conv_w"]]))
    cw_full = reduced[reduced.shape[0] - _packed_rows((CONV_W, D)):].reshape(CONV_W, D)
    cw_cols = lax.dynamic_slice_in_dim(cw_full, me * (D // N_DEV), D // N_DEV, axis=1)

    grads, deltas, new_m, new_v = _adamw_small(reduced, cw_cols, weights, moms, vels)
    for k, s, g_got in zip(BIG_NAMES, sums, others):
        grads[k], deltas[k], new_m[k], new_v[k] = _adamw(weights[k], moms[k], vels[k], s[0], g_got, chip, "adamw_" + k)

    return (loss, grad_x.reshape(1, S, D), *[grads[k] for k in WEIGHT_ORDER], *[deltas[k] for k in WEIGHT_ORDER],
            *[new_m[k] for k in WEIGHT_ORDER], *[new_v[k] for k in WEIGHT_ORDER])
```

```python
import functools
import math

import jax
import jax.numpy as jnp
from jax import lax
from jax.experimental import pallas as pl
from jax.experimental.pallas import tpu as pltpu

F32 = jnp.float32
BF16 = jnp.bfloat16

D_MODEL = 1024
N_HEADS = 8
HEAD_DIM = 128
D_FF = 4096
LRU_BLOCKS = 16
LRU_BW = 64
LRU_C = 8.0
CONV_W = 4
RMS_EPS = 1e-6
N_DEV = 8
LANES = 128
SUBLANES = 8
N_GROUPS = D_MODEL // LANES
VMEM_LIMIT_BYTES = 52 * 1024 * 1024
ATTN_SCALE = 1.0 / math.sqrt(HEAD_DIM)
LOG2E = math.log2(math.e)
NEG_BIG = -1e30
ADAM_LR = 0.001
ADAM_B1 = 0.9
ADAM_B2 = 0.999
ADAM_EPS = 1e-08
ADAM_WD = 0.01
ADAM_STEP = 10
ATTN_BLOCK = 1024
ATTN_Q_TILES = 4
ATTN_STRIP = 256
LRU_CHUNK = 256
ROW_TILE = 512
MESH_AXES = ("x", "y", "c")
MESH_ID = pl.DeviceIdType.MESH
ANY = pl.BlockSpec(memory_space=pl.ANY)

NT_DIMS = (((1,), (1,)), ((), ()))
TN_DIMS = (((0,), (0,)), ((), ()))
NN_DIMS = (((1,), (0,)), ((), ()))


def _cparams(*sem):
    return pltpu.CompilerParams(dimension_semantics=sem if sem else None, vmem_limit_bytes=VMEM_LIMIT_BYTES)


def _sigmoid(x):
    return 0.5 * (jnp.tanh(0.5 * x) + 1.0)


def _log1p_pos(e):
    u = 1.0 + e
    return jnp.where(u == 1.0, e, jnp.log(u) * (e / (u - 1.0)))


def _softplus(z):
    return jnp.maximum(z, 0.0) + _log1p_pos(jnp.exp(-jnp.abs(z)))


def _expm1_neg(x):
    series = x * (1.0 + x * 0.5 * (1.0 + x * (1.0 / 3.0) * (1.0 + x * 0.25)))
    return jnp.where(x > -0.03, series, jnp.exp(x) - 1.0)


GELU_C = math.sqrt(2.0 / math.pi)
GELU_K = 0.044715


def _gelu(x):
    return 0.5 * x * (1.0 + jnp.tanh(GELU_C * (x + GELU_K * (x * x * x))))


def _gelu_and_grad(x):
    t = jnp.tanh(GELU_C * (x + GELU_K * (x * x * x)))
    g = 0.5 * x * (1.0 + t)
    dg = 0.5 * (1.0 + t) + 0.5 * x * (1.0 - t * t) * (GELU_C * (1.0 + 3.0 * GELU_K * (x * x)))
    return g, dg


def _mm(pairs, *, ta=False, tb=False, tm, tn, tks, outs, name, epi=None, extra=(), col_blocked=False):
    n_pairs, n_extra, n_out = len(pairs), len(extra), len(outs)
    tas = list(ta) if isinstance(ta, (list, tuple)) else [ta] * n_pairs
    tbs = list(tb) if isinstance(tb, (list, tuple)) else [tb] * n_pairs
    a0, b0 = pairs[0]
    M = a0.shape[1] if tas[0] else a0.shape[0]
    N = b0.shape[0] if tbs[0] else b0.shape[1]
    tm, tn = min(tm, M), min(tn, N)
    nks, offs = [], []
    for (a, b), tk, pta in zip(pairs, tks, tas):
        K = a.shape[0] if pta else a.shape[1]
        assert K % tk == 0 and M % tm == 0 and N % tn == 0
        offs.append(sum(nks))
        nks.append(K // tk)
    nk_total = sum(nks)
    dims = [(((0 if pta else 1,), (1 if ptb else 0,)), ((), ())) for pta, ptb in zip(tas, tbs)]

    def kmap(off, nk):
        return lambda k: jnp.clip(k - off, 0, nk - 1)

    in_specs, operands = [], []
    for (a, b), tk, off, nk, pta, ptb in zip(pairs, tks, offs, nks, tas, tbs):
        km = kmap(off, nk)
        if pta:
            in_specs.append(pl.BlockSpec((tk, tm), lambda i, j, k, km=km: (km(k), i)))
        else:
            in_specs.append(pl.BlockSpec((tm, tk), lambda i, j, k, km=km: (i, km(k))))
        if ptb:
            in_specs.append(pl.BlockSpec((tn, tk), lambda i, j, k, km=km: (j, km(k))))
        else:
            in_specs.append(pl.BlockSpec((tk, tn), lambda i, j, k, km=km: (km(k), j)))
        operands += [a, b]
    for e in extra:
        in_specs.append(pl.BlockSpec((tm, tn), lambda i, j, k: (i, j)))
        operands.append(e)

    def body(*refs):
        ab = refs[:2 * n_pairs]
        ex = refs[2 * n_pairs:2 * n_pairs + n_extra]
        o = refs[2 * n_pairs + n_extra:2 * n_pairs + n_extra + n_out]
        k = pl.program_id(2)

        def finish(acc):
            res = epi(acc, *[e[...] for e in ex]) if epi is not None else (acc,)
            for r, oref in zip(res, o):
                oref[...] = r.astype(oref.dtype)

        if nk_total == 1:
            finish(lax.dot_general(ab[0][...], ab[1][...], dims[0], preferred_element_type=F32))
            return
        acc = refs[-1]
        for p in range(n_pairs):
            a_ref, b_ref = ab[2 * p], ab[2 * p + 1]

            @pl.when((k >= offs[p]) & (k < offs[p] + nks[p]))
            def _(a_ref=a_ref, b_ref=b_ref, pdims=dims[p]):
                prod = lax.dot_general(a_ref[...], b_ref[...], pdims, preferred_element_type=F32)

                @pl.when(k == 0)
                def _():
                    acc[...] = prod

                @pl.when(k > 0)
                def _():
                    acc[...] += prod

        @pl.when(k == nk_total - 1)
        def _():
            finish(acc[...])

    return pl.pallas_call(
        body,
        name=name,
        grid=(M // tm, N // tn, nk_total),
        in_specs=in_specs,
        out_specs=[pl.BlockSpec((None, tm, tn), lambda i, j, k: (j, i, 0)) if col_blocked
                   else pl.BlockSpec((tm, tn), lambda i, j, k: (i, j)) for _ in outs],
        out_shape=[jax.ShapeDtypeStruct((N // tn, M, tn) if col_blocked else (M, N), dt) for dt in outs],
        scratch_shapes=[] if nk_total == 1 else [pltpu.VMEM((tm, tn), F32)],
        compiler_params=_cparams("parallel", "parallel", "arbitrary"),
    )(*operands)


def _norm_fwd(x, g, name):
    S, D = x.shape
    tr = min(ROW_TILE, S)

    def body(x_ref, g_ref, o_ref):
        xv = x_ref[...]
        r = lax.rsqrt(jnp.mean(xv * xv, axis=-1, keepdims=True) + RMS_EPS)
        o_ref[...] = ((xv * r) * g_ref[...]).astype(o_ref.dtype)

    return pl.pallas_call(
        body, name=name, grid=(S // tr,),
        in_specs=[pl.BlockSpec((tr, D), lambda i: (i, 0)), pl.BlockSpec((1, D), lambda i: (0, 0))],
        out_specs=pl.BlockSpec((tr, D), lambda i: (i, 0)),
        out_shape=jax.ShapeDtypeStruct((S, D), BF16),
        compiler_params=_cparams("parallel"),
    )(x, g.reshape(1, D))


def _rms_bwd_rows(dy, xv, g):
    r = lax.rsqrt(jnp.mean(xv * xv, axis=-1, keepdims=True) + RMS_EPS)
    xn = xv * r
    dxn = dy * g
    dx = r * (dxn - xn * jnp.mean(dxn * xn, axis=-1, keepdims=True))
    dg = jnp.sum(dy * xn, axis=0, keepdims=True)
    return dx, dg


def _norm_bwd(dy, x, g, dres, name):
    S, D = x.shape
    tr = min(ROW_TILE, S)

    def body(dy_ref, x_ref, g_ref, dres_ref, dx_ref, dxb_ref, dg_ref):
        dx, dg = _rms_bwd_rows(dy_ref[...], x_ref[...], g_ref[...])
        dx = dres_ref[...] + dx
        dx_ref[...] = dx
        dxb_ref[...] = dx.astype(BF16)

        @pl.when(pl.program_id(0) == 0)
        def _():
            dg_ref[...] = jnp.zeros_like(dg_ref)

        dg_ref[...] += dg

    row = pl.BlockSpec((tr, D), lambda i: (i, 0))
    vec = pl.BlockSpec((1, D), lambda i: (0, 0))
    return pl.pallas_call(
        body, name=name, grid=(S // tr,),
        in_specs=[row, row, vec, row],
        out_specs=[row, row, vec],
        out_shape=[jax.ShapeDtypeStruct((S, D), F32), jax.ShapeDtypeStruct((S, D), BF16),
                   jax.ShapeDtypeStruct((1, D), F32)],
        compiler_params=_cparams("arbitrary"),
    )(dy, x, g.reshape(1, D), dres)


def _final_norm_loss(x2, target, g):
    S, D = x2.shape
    tr = min(ROW_TILE, S)

    def body(x_ref, t_ref, g_ref, loss_ref, dg_ref, dx_ref, dxb_ref):
        xv = x_ref[...]
        gv = g_ref[...]
        r = lax.rsqrt(jnp.mean(xv * xv, axis=-1, keepdims=True) + RMS_EPS)
        y = (xv * r) * gv
        err = y - t_ref[...]
        part = 0.5 * jnp.sum(jnp.mean(err * err, axis=-1, keepdims=True), axis=0, keepdims=True)
        dy = err * (1.0 / D)
        dx, dg = _rms_bwd_rows(dy, xv, gv)
        dx_ref[...] = dx
        dxb_ref[...] = dx.astype(BF16)

        @pl.when(pl.program_id(0) == 0)
        def _():
            dg_ref[...] = jnp.zeros_like(dg_ref)
            loss_ref[...] = jnp.zeros_like(loss_ref)

        dg_ref[...] += dg
        loss_ref[...] += jnp.broadcast_to(part, loss_ref.shape)

    row = pl.BlockSpec((tr, D), lambda i: (i, 0))
    vec = pl.BlockSpec((1, D), lambda i: (0, 0))
    return pl.pallas_call(
        body, name="final_norm_loss", grid=(S // tr,),
        in_specs=[row, row, vec],
        out_specs=[pl.BlockSpec((SUBLANES, LANES), lambda i: (0, 0)), vec, row, row],
        out_shape=[jax.ShapeDtypeStruct((SUBLANES, LANES), F32), jax.ShapeDtypeStruct((1, D), F32),
                   jax.ShapeDtypeStruct((S, D), F32), jax.ShapeDtypeStruct((S, D), BF16)],
        compiler_params=_cparams("arbitrary"),
    )(x2, target, g.reshape(1, D))


def _lru_gates(xa, bd_j, ba_j, bx_j, sp_j):
    z = jnp.dot(xa.astype(BF16), bd_j, preferred_element_type=F32)
    r = _sigmoid(z[:, :LANES] + ba_j)
    ig = _sigmoid(z[:, LANES:] + bx_j)
    log_a = (-LRU_C) * r * sp_j
    a = jnp.exp(log_a)
    mult = jnp.sqrt(-_expm1_neg(2.0 * log_a))
    return r, ig, a, mult


def _conv_rows(xpad, cw_ref, cb_ref, sl, tc):
    out = jnp.broadcast_to(cb_ref[:, sl], (tc, LANES))
    for k in range(CONV_W):
        out = out + xpad[pl.ds(SUBLANES - (CONV_W - 1) + k, tc), sl] * cw_ref[k:k + 1, sl]
    return out


def _lru_fwd(xg, cw, cb, bd, ba, bx, lam):
    S = xg.shape[0]
    D = D_MODEL
    tc = min(LRU_CHUNK, S)
    hb = tc // SUBLANES

    def body(xl_ref, halo_ref, g_ref, cw_ref, cb_ref, bd_ref, ba_ref, bx_ref, lam_ref,
             h_ref, y_ref, xpad, a_s, b_s, carry):
        i = pl.program_id(0)

        @pl.when(i == 0)
        def _():
            carry[...] = jnp.zeros_like(carry)

        xpad[0:SUBLANES, :] = jnp.where(i > 0, halo_ref[...], 0.0)
        xpad[SUBLANES:, :] = xl_ref[...]
        for j in range(N_GROUPS):
            sl = slice(LANES * j, LANES * (j + 1))
            xa = _conv_rows(xpad, cw_ref, cb_ref, sl, tc)
            sp = _softplus(-lam_ref[:, sl])
            _, ig, a, mult = _lru_gates(xa, bd_ref[j], ba_ref[:, sl], bx_ref[:, sl], sp)
            a_s[:, sl] = a
            b_s[:, sl] = mult * (ig * xa)

        row = lax.broadcasted_iota(jnp.int32, (SUBLANES, D), 0)

        def step(t, c):
            o = pl.multiple_of(t * SUBLANES, SUBLANES)
            A = a_s[pl.ds(o, SUBLANES), :]
            B = b_s[pl.ds(o, SUBLANES), :]
            for d in (1, 2, 4):
                keep = row >= d
                a_sh = jnp.where(keep, pltpu.roll(A, d, 0), 1.0)
                b_sh = jnp.where(keep, pltpu.roll(B, d, 0), 0.0)
                B = A * b_sh + B
                A = A * a_sh
            hh = A * c + B
            h_ref[pl.ds(o, SUBLANES), :] = hh
            return jnp.broadcast_to(hh[SUBLANES - 1:SUBLANES, :], (SUBLANES, D))

        carry[...] = lax.fori_loop(0, hb, step, carry[...])
        y_ref[...] = (_gelu(g_ref[...]) * h_ref[...]).astype(BF16)

    row_spec = lambda col: pl.BlockSpec((tc, D), lambda i, col=col: (i, col))
    halo = pl.BlockSpec((SUBLANES, D), lambda i: (jnp.maximum(i * hb - 1, 0), 0))
    full = lambda shape: pl.BlockSpec(shape, lambda i: tuple(0 for _ in shape))
    return pl.pallas_call(
        body, name="lru_fwd", grid=(S // tc,),
        in_specs=[row_spec(0), halo, row_spec(1), full((CONV_W, D)), full((1, D)),
                  full((N_GROUPS, LANES, 2 * LANES)), full((1, D)), full((1, D)), full((1, D))],
        out_specs=[pl.BlockSpec((tc, D), lambda i: (i, 0)), pl.BlockSpec((tc, D), lambda i: (i, 0))],
        out_shape=[jax.ShapeDtypeStruct((S, D), F32), jax.ShapeDtypeStruct((S, D), BF16)],
        scratch_shapes=[pltpu.VMEM((tc + SUBLANES, D), F32), pltpu.VMEM((tc, D), F32),
                        pltpu.VMEM((tc, D), F32), pltpu.VMEM((SUBLANES, D), F32)],
        compiler_params=_cparams("arbitrary"),
    )(xg, xg, xg, cw, cb, bd, ba, bx, lam)


def _lru_bwd(xg, h, dyain, cw, cb, bd, ba, bx, lam):
    S = xg.shape[0]
    D = D_MODEL
    tc = min(LRU_CHUNK, S)
    hb = tc // SUBLANES
    nc = S // tc

    def body(xl_ref, xhalo_ref, g_ref, h_ref, hhalo_ref, dy_ref, cw_ref, cb_ref, bd_ref, ba_ref, bx_ref,
             lam_ref, dxg_ref, dcw_ref, dcb_ref, dba_ref, dbx_ref, dlam_ref, dbd_ref,
             xpad, hpad, a_s, b_s, dh_s, g_s, xa_s, r_s, ig_s, m_s, dxa_pad, carry_e, dxa_head):
        i = pl.program_id(0)
        c = nc - 1 - i

        @pl.when(i == 0)
        def _():
            carry_e[...] = jnp.zeros_like(carry_e)
            dxa_head[...] = jnp.zeros_like(dxa_head)
            for ref in (dcw_ref, dcb_ref, dba_ref, dbx_ref, dlam_ref, dbd_ref):
                ref[...] = jnp.zeros_like(ref)

        xpad[0:SUBLANES, :] = jnp.where(c > 0, xhalo_ref[...], 0.0)
        xpad[SUBLANES:, :] = xl_ref[...]
        hpad[0:SUBLANES, :] = jnp.where(c > 0, hhalo_ref[...], 0.0)
        hpad[SUBLANES:, :] = h_ref[...]

        for j in range(N_GROUPS):
            sl = slice(LANES * j, LANES * (j + 1))
            xa = _conv_rows(xpad, cw_ref, cb_ref, sl, tc)
            sp = _softplus(-lam_ref[:, sl])
            r, ig, a, mult = _lru_gates(xa, bd_ref[j], ba_ref[:, sl], bx_ref[:, sl], sp)
            gl, dgl = _gelu_and_grad(g_ref[:, sl])
            dy = dy_ref[:, sl]
            dh = dy * gl
            dxg_ref[:, D + LANES * j:D + LANES * (j + 1)] = (dy * h_ref[:, sl] * dgl).astype(BF16)
            a_s[:, sl] = a
            b_s[:, sl] = a * dh
            dh_s[:, sl] = dh
            xa_s[:, sl] = xa
            r_s[:, sl] = r
            ig_s[:, sl] = ig
            m_s[:, sl] = mult

        row = lax.broadcasted_iota(jnp.int32, (SUBLANES, D), 0)

        def step(tt, ce):
            o = pl.multiple_of((hb - 1 - tt) * SUBLANES, SUBLANES)
            A = a_s[pl.ds(o, SUBLANES), :]
            B = b_s[pl.ds(o, SUBLANES), :]
            for d in (1, 2, 4):
                keep = row < SUBLANES - d
                a_sh = jnp.where(keep, pltpu.roll(A, SUBLANES - d, 0), 1.0)
                b_sh = jnp.where(keep, pltpu.roll(B, SUBLANES - d, 0), 0.0)
                B = A * b_sh + B
                A = A * a_sh
            e = A * ce + B
            e_next = jnp.where(row < SUBLANES - 1, pltpu.roll(e, SUBLANES - 1, 0), ce)
            g_s[pl.ds(o, SUBLANES), :] = dh_s[pl.ds(o, SUBLANES), :] + e_next
            return jnp.broadcast_to(e[0:1, :], (SUBLANES, D))

        carry_e[...] = lax.fori_loop(0, hb, step, carry_e[...])

        for j in range(N_GROUPS):
            sl = slice(LANES * j, LANES * (j + 1))
            gg = g_s[:, sl]
            xa, r, ig, mult, a = xa_s[:, sl], r_s[:, sl], ig_s[:, sl], m_s[:, sl], a_s[:, sl]
            hprev = hpad[pl.ds(SUBLANES - 1, tc), sl]
            sp = _softplus(-lam_ref[:, sl])
            da = gg * hprev
            dmult = gg * (ig * xa)
            dig = gg * (mult * xa)
            dxa = gg * (mult * ig)
            dla = da * a - dmult * ((a * a) / mult)
            dr = dla * ((-LRU_C) * sp)
            dlam_ref[:, sl] += jnp.sum(dla * r, axis=0, keepdims=True)
            dza = dr * r * (1.0 - r)
            dzx = dig * ig * (1.0 - ig)
            dba_ref[:, sl] += jnp.sum(dza, axis=0, keepdims=True)
            dbx_ref[:, sl] += jnp.sum(dzx, axis=0, keepdims=True)
            dz = jnp.concatenate([dza, dzx], axis=1).astype(BF16)
            dbd_ref[j] += lax.dot_general(xa.astype(BF16), dz, TN_DIMS, preferred_element_type=F32)
            dxa = dxa + lax.dot_general(dz, bd_ref[j], NT_DIMS, preferred_element_type=F32)
            dxa_pad[0:tc, sl] = dxa

        dxa_pad[tc:, :] = dxa_head[...]
        dxa_head[...] = dxa_pad[0:SUBLANES, :]

        for j in range(N_GROUPS):
            sl = slice(LANES * j, LANES * (j + 1))
            dxa = dxa_pad[0:tc, sl]
            dxl = jnp.zeros((tc, LANES), F32)
            for k in range(CONV_W):
                dxl = dxl + dxa_pad[pl.ds(CONV_W - 1 - k, tc), sl] * cw_ref[k:k + 1, sl]
                dcw_ref[k:k + 1, sl] += jnp.sum(
                    dxa * xpad[pl.ds(SUBLANES - (CONV_W - 1) + k, tc), sl], axis=0, keepdims=True)
            dxg_ref[:, sl] = dxl.astype(BF16)
            dcb_ref[:, sl] += jnp.sum(dxa, axis=0, keepdims=True)

        @pl.when(i == nc - 1)
        def _():
            dlam_ref[...] = dlam_ref[...] * (LRU_C * _sigmoid(-lam_ref[...]))

    rev = lambda col: pl.BlockSpec((tc, D), lambda i, col=col: (nc - 1 - i, col))
    halo = pl.BlockSpec((SUBLANES, D), lambda i: (jnp.maximum((nc - 1 - i) * hb - 1, 0), 0))
    full = lambda shape: pl.BlockSpec(shape, lambda i: tuple(0 for _ in shape))
    big = lambda: pltpu.VMEM((tc, D), F32)
    return pl.pallas_call(
        body, name="lru_bwd", grid=(nc,),
        in_specs=[rev(0), halo, rev(1), rev(0), halo, rev(0), full((CONV_W, D)), full((1, D)),
                  full((N_GROUPS, LANES, 2 * LANES)), full((1, D)), full((1, D)), full((1, D))],
        out_specs=[pl.BlockSpec((tc, 2 * D), lambda i: (nc - 1 - i, 0)), full((CONV_W, D)), full((1, D)),
                   full((1, D)), full((1, D)), full((1, D)), full((N_GROUPS, LANES, 2 * LANES))],
        out_shape=[jax.ShapeDtypeStruct((S, 2 * D), BF16), jax.ShapeDtypeStruct((CONV_W, D), F32),
                   jax.ShapeDtypeStruct((1, D), F32), jax.ShapeDtypeStruct((1, D), F32),
                   jax.ShapeDtypeStruct((1, D), F32), jax.ShapeDtypeStruct((1, D), F32),
                   jax.ShapeDtypeStruct((N_GROUPS, LANES, 2 * LANES), F32)],
        scratch_shapes=[pltpu.VMEM((tc + SUBLANES, D), F32), pltpu.VMEM((tc + SUBLANES, D), F32),
                        big(), big(), big(), big(), big(), big(), big(), big(),
                        pltpu.VMEM((tc + SUBLANES, D), F32), pltpu.VMEM((SUBLANES, D), F32),
                        pltpu.VMEM((SUBLANES, D), F32)],
        compiler_params=_cparams("arbitrary"),
    )(xg, xg, xg, h, h, dyain, cw, cb, bd, ba, bx, lam)


def _forget_cumsum(fl, fb):
    S = fl.shape[0]
    tr = min(ROW_TILE, S)
    hb = tr // SUBLANES

    def body(fl_ref, fb_ref, o_ref, rep_ref, lf_s, carry):
        @pl.when(pl.program_id(0) == 0)
        def _():
            carry[...] = jnp.zeros_like(carry)

        lf_s[...] = -_softplus(-(fl_ref[...] + fb_ref[...]))
        row = lax.broadcasted_iota(jnp.int32, (SUBLANES, LANES), 0)

        def step(t, c):
            o = pl.multiple_of(t * SUBLANES, SUBLANES)
            B = lf_s[pl.ds(o, SUBLANES), :]
            for d in (1, 2, 4):
                B = B + jnp.where(row >= d, pltpu.roll(B, d, 0), 0.0)
            B = B + c
            o_ref[pl.ds(o, SUBLANES), :] = B * LOG2E
            return jnp.broadcast_to(B[SUBLANES - 1:SUBLANES, :], (SUBLANES, LANES))

        carry[...] = lax.fori_loop(0, hb, step, carry[...])
        for h in range(N_HEADS):
            rep_ref[h] = jnp.broadcast_to(o_ref[:, h:h + 1], (tr, LANES))

    return pl.pallas_call(
        body, name="forget_cumsum", grid=(S // tr,),
        in_specs=[pl.BlockSpec((tr, LANES), lambda i: (i, 0)), pl.BlockSpec((1, LANES), lambda i: (0, 0))],
        out_specs=[pl.BlockSpec((tr, LANES), lambda i: (i, 0)),
                   pl.BlockSpec((N_HEADS, tr, LANES), lambda i: (0, i, 0))],
        out_shape=[jax.ShapeDtypeStruct((S, LANES), F32), jax.ShapeDtypeStruct((N_HEADS, S, LANES), F32)],
        scratch_shapes=[pltpu.VMEM((tr, LANES), F32), pltpu.VMEM((SUBLANES, LANES), F32)],
        compiler_params=_cparams("arbitrary"),
    )(fl, fb)


def _forget_bwd(dF, fl, fb):
    S = fl.shape[0]
    tr = min(ROW_TILE, S)
    hb = tr // SUBLANES
    nc = S // tr

    def body(df_ref, fl_ref, fb_ref, o_ref, dfb_ref, carry):
        @pl.when(pl.program_id(0) == 0)
        def _():
            carry[...] = jnp.zeros_like(carry)
            dfb_ref[...] = jnp.zeros_like(dfb_ref)

        row = lax.broadcasted_iota(jnp.int32, (SUBLANES, LANES), 0)

        def step(tt, carried):
            c, acc = carried
            o = pl.multiple_of((hb - 1 - tt) * SUBLANES, SUBLANES)
            B = df_ref[pl.ds(o, SUBLANES), :]
            for d in (1, 2, 4):
                B = B + jnp.where(row < SUBLANES - d, pltpu.roll(B, SUBLANES - d, 0), 0.0)
            B = B + c
            z = fl_ref[pl.ds(o, SUBLANES), :] + fb_ref[...]
            dz = B * _sigmoid(-z)
            o_ref[pl.ds(o, SUBLANES), :] = dz.astype(BF16)
            return jnp.broadcast_to(B[0:1, :], (SUBLANES, LANES)), acc + dz

        c, acc = lax.fori_loop(0, hb, step, (carry[...], jnp.zeros((SUBLANES, LANES), F32)))
        carry[...] = c
        dfb_ref[...] += jnp.sum(acc, axis=0, keepdims=True)

    rev = pl.BlockSpec((tr, LANES), lambda i: (nc - 1 - i, 0))
    vec = pl.BlockSpec((1, LANES), lambda i: (0, 0))
    return pl.pallas_call(
        body, name="forget_bwd", grid=(nc,),
        in_specs=[rev, rev, vec],
        out_specs=[rev, vec],
        out_shape=[jax.ShapeDtypeStruct((S, LANES), BF16), jax.ShapeDtypeStruct((1, LANES), F32)],
        scratch_shapes=[pltpu.VMEM((SUBLANES, LANES), F32)],
        compiler_params=_cparams("arbitrary"),
    )(dF, fl, fb)


def _triangle(n, qw, key_major):
    pairs = [(q, k) for q in range(n) for k in range(qw * (q + 1))]
    if key_major:
        pairs.sort(key=lambda qk: (qk[1], qk[0]))
    return (jnp.asarray([q for q, _ in pairs], jnp.int32), jnp.asarray([k for _, k in pairs], jnp.int32))


def _strip_plan(bk, bq, strip, rel):
    plan = []
    for j in range(bq // strip):
        if rel is None:
            plan.append((j, bk, None))
            continue
        reach = strip * (j + 1) - rel * bk
        if reach > 0:
            plan.append((j, min(reach, bk), strip * j - rel * bk if reach <= bk else None))
    return plan


def _strip_scores(k_ref, qt_ref, fk_ref, strip, j, nkeys, mask_off):
    cols = slice(strip * j, strip * (j + 1))
    s = jnp.dot(k_ref[0:nkeys, :], qt_ref[:, cols], preferred_element_type=F32) * (ATTN_SCALE * LOG2E)
    fk = fk_ref[0:nkeys, :]
    s = s - jnp.concatenate([fk] * (strip // LANES), axis=1)
    keep = None
    if mask_off is not None:
        keys = lax.broadcasted_iota(jnp.int32, (nkeys, strip), 0)
        queries = lax.broadcasted_iota(jnp.int32, (nkeys, strip), 1) + mask_off
        keep = keys <= queries
    return s, keep


def _attn_fwd(kv, qkv_t, f_row, f_rep):
    S = kv.shape[0]
    bk = min(ATTN_BLOCK, S)
    strip = min(ATTN_STRIP, bk)
    qw = min(ATTN_Q_TILES, S // bk)
    bq = qw * bk
    tri_q, tri_k = _triangle(S // bq, qw, key_major=False)
    ones_rows = 2 * SUBLANES

    def body(tq_ref, tk_ref, k_ref, qt_ref, vt_ref, fq_ref, fk_ref, ot_ref, lse_ref, m_s, acc_s, vta_s):
        t = pl.program_id(1)
        qi, ki = tq_ref[t], tk_ref[t]
        rel = ki - qw * qi

        @pl.when(ki == 0)
        def _():
            m_s[...] = jnp.full_like(m_s, NEG_BIG)
            acc_s[...] = jnp.zeros_like(acc_s)

        vta_s[0:HEAD_DIM, :] = vt_ref[...]
        vta_s[HEAD_DIM:, :] = jnp.ones((ones_rows, bk), BF16)

        def update(plan):
            scores = lambda entry: _strip_scores(k_ref, qt_ref, fk_ref, strip, *entry)

            def weighted_values(j, nkeys, alpha, pb):
                cols = slice(strip * j, strip * (j + 1))
                acc_s[:, cols] = alpha * acc_s[:, cols] + jnp.dot(
                    vta_s[:, 0:nkeys], pb, preferred_element_type=F32)

            ahead, behind = scores(plan[0]), None
            for i, (j, nkeys, mask_off) in enumerate(plan):
                cols = slice(strip * j, strip * (j + 1))
                (s, keep), ahead = ahead, (scores(plan[i + 1]) if i + 1 < len(plan) else None)
                if behind is not None:
                    weighted_values(*behind)
                if keep is not None:
                    s = jnp.where(keep, s, NEG_BIG)
                fq = fq_ref[:, cols]
                m_old = m_s[:, cols]
                m_new = jnp.maximum(m_old, jnp.max(s, axis=0, keepdims=True) + fq)
                p = jnp.exp2(s - (m_new - fq))
                behind = (j, nkeys, jnp.exp2(m_old - m_new), p.astype(BF16))
                m_s[:, cols] = m_new
            weighted_values(*behind)

        @pl.when(rel < 0)
        def _():
            update(_strip_plan(bk, bq, strip, None))

        for d in range(qw):
            @pl.when(rel == d)
            def _(d=d):
                update(_strip_plan(bk, bq, strip, d))
                if d == qw - 1:
                    denom = acc_s[HEAD_DIM:HEAD_DIM + 1, :]
                    ot_ref[...] = (acc_s[0:HEAD_DIM, :] / denom).astype(BF16)
                    lse_ref[...] = m_s[...] + jnp.log2(denom)

    return pl.pallas_call(
        body, name="attn_fwd",
        grid_spec=pltpu.PrefetchScalarGridSpec(
            num_scalar_prefetch=2, grid=(N_HEADS, tri_q.shape[0]),
            in_specs=[pl.BlockSpec((bk, HEAD_DIM), lambda h, t, tq, tk: (tk[t], h)),
                      pl.BlockSpec((HEAD_DIM, bq), lambda h, t, tq, tk: (h, tq[t])),
                      pl.BlockSpec((HEAD_DIM, bk), lambda h, t, tq, tk: (2 * N_HEADS + h, tk[t])),
                      pl.BlockSpec((None, 1, bq), lambda h, t, tq, tk: (h, 0, tq[t])),
                      pl.BlockSpec((None, bk, LANES), lambda h, t, tq, tk: (h, tk[t], 0))],
            out_specs=[pl.BlockSpec((HEAD_DIM, bq), lambda h, t, tq, tk: (h, tq[t])),
                       pl.BlockSpec((None, 1, bq), lambda h, t, tq, tk: (h, 0, tq[t]))],
            scratch_shapes=[pltpu.VMEM((1, bq), F32), pltpu.VMEM((HEAD_DIM + ones_rows, bq), F32),
                            pltpu.VMEM((HEAD_DIM + ones_rows, bk), BF16)]),
        out_shape=[jax.ShapeDtypeStruct((N_HEADS * HEAD_DIM, S), BF16), jax.ShapeDtypeStruct((N_HEADS, 1, S), F32)],
        compiler_params=_cparams("parallel", "arbitrary"),
    )(tri_q, tri_k, kv, qkv_t, qkv_t, f_row, f_rep)


def _attn_bwd(kv, qkv_t, do_t, o_t, lse, f_row, f_rep):
    S = kv.shape[0]
    bk = min(ATTN_BLOCK, S)
    strip = min(ATTN_STRIP, bk)
    qw = min(ATTN_Q_TILES, S // bk)
    bq = qw * bk
    nq = S // bq
    tri_q, tri_k = _triangle(nq, qw, key_major=True)
    n_tiles = tri_q.shape[0]

    def body(tq_ref, tk_ref, k_ref, v_ref, qt_ref, kt_ref, dot_ref, ot_ref, lse_ref, fq_ref, fk_ref,
             dqt_ref, dkt_ref, dvt_ref, dfk_ref, dfq_ref, dq_s, dk_s, dv_s, dfk_s, dfq_s, row_s):
        t = pl.program_id(1)
        qi, ki = tq_ref[t], tk_ref[t]
        rel = ki - qw * qi

        @pl.when(t == 0)
        def _():
            dq_s[...] = jnp.zeros_like(dq_s)
            dfq_s[...] = jnp.zeros_like(dfq_s)

        @pl.when(rel >= 0)
        def _():
            dk_s[...] = jnp.zeros_like(dk_s)
            dv_s[...] = jnp.zeros_like(dv_s)
            dfk_s[...] = jnp.zeros_like(dfk_s)

        def update(plan):
            row_s[...] = fq_ref[...] - lse_ref[...]

            def matmuls_in(j, nkeys, mask_off):
                s, keep = _strip_scores(k_ref, qt_ref, fk_ref, strip, j, nkeys, mask_off)
                dp = jnp.dot(v_ref[0:nkeys, :], dot_ref[:, strip * j:strip * (j + 1)], preferred_element_type=F32)
                return s, keep, dp

            def matmuls_out(j, nkeys, pb, dsb):
                cols = slice(strip * j, strip * (j + 1))
                dv_s[:, 0:nkeys] += lax.dot_general(dot_ref[:, cols], pb, NT_DIMS, preferred_element_type=F32)
                dk_s[:, 0:nkeys] += lax.dot_general(qt_ref[:, cols], dsb, NT_DIMS, preferred_element_type=F32)
                dq_s[qi, :, cols] += jnp.dot(kt_ref[:, 0:nkeys], dsb, preferred_element_type=F32)

            ahead, behind = matmuls_in(*plan[0]), None
            for i, (j, nkeys, mask_off) in enumerate(plan):
                cols = slice(strip * j, strip * (j + 1))
                (s, keep, dp), ahead = ahead, (matmuls_in(*plan[i + 1]) if i + 1 < len(plan) else None)
                if behind is not None:
                    matmuls_out(*behind)
                p = jnp.exp2(s + row_s[:, cols])
                if keep is not None:
                    p = jnp.where(keep, p, 0.0)
                dot = dot_ref[:, cols]
                delta = jnp.sum(dot.astype(F32) * ot_ref[:, cols].astype(F32), axis=0, keepdims=True)
                ds = p * (dp - delta)
                behind = (j, nkeys, p.astype(BF16), ds.astype(BF16))
                lane_part = ds[:, 0:LANES]
                for g in range(1, strip // LANES):
                    lane_part = lane_part + ds[:, LANES * g:LANES * (g + 1)]
                dfk_s[0:nkeys, :] += lane_part
                sub_part = ds[0:SUBLANES, :]
                for g in range(1, nkeys // SUBLANES):
                    sub_part = sub_part + ds[SUBLANES * g:SUBLANES * (g + 1), :]
                dfq_s[qi, :, cols] += sub_part
            matmuls_out(*behind)

        @pl.when(rel < 0)
        def _():
            update(_strip_plan(bk, bq, strip, None))

        for d in range(qw):
            @pl.when(rel == d)
            def _(d=d):
                update(_strip_plan(bk, bq, strip, d))

        @pl.when(qi == nq - 1)
        def _():
            dkt_ref[...] = (dk_s[...] * ATTN_SCALE).astype(BF16)
            dvt_ref[...] = dv_s[...].astype(BF16)
            dfk_ref[...] = -jnp.sum(dfk_s[...].T, axis=0, keepdims=True)

        @pl.when(t == n_tiles - 1)
        def _():
            for j in range(nq):
                dqt_ref[:, bq * j:bq * (j + 1)] = (dq_s[j] * ATTN_SCALE).astype(BF16)
                dfq_ref[:, bq * j:bq * (j + 1)] = jnp.sum(dfq_s[j], axis=0, keepdims=True)

    q_feat = pl.BlockSpec((HEAD_DIM, bq), lambda h, t, tq, tk: (h, tq[t]))
    q_row = pl.BlockSpec((None, 1, bq), lambda h, t, tq, tk: (h, 0, tq[t]))
    k_feat = pl.BlockSpec((HEAD_DIM, bk), lambda h, t, tq, tk: (h, tk[t]))
    return pl.pallas_call(
        body, name="attn_bwd",
        grid_spec=pltpu.PrefetchScalarGridSpec(
            num_scalar_prefetch=2, grid=(N_HEADS, n_tiles),
            in_specs=[pl.BlockSpec((bk, HEAD_DIM), lambda h, t, tq, tk: (tk[t], h)),
                      pl.BlockSpec((bk, HEAD_DIM), lambda h, t, tq, tk: (tk[t], N_HEADS + h)),
                      q_feat,
                      pl.BlockSpec((HEAD_DIM, bk), lambda h, t, tq, tk: (N_HEADS + h, tk[t])),
                      q_feat, q_feat, q_row, q_row,
                      pl.BlockSpec((None, bk, LANES), lambda h, t, tq, tk: (h, tk[t], 0))],
            out_specs=[pl.BlockSpec((HEAD_DIM, S), lambda h, t, tq, tk: (h, 0)), k_feat, k_feat,
                       pl.BlockSpec((None, 1, bk), lambda h, t, tq, tk: (h, 0, tk[t])),
                       pl.BlockSpec((None, 1, S), lambda h, t, tq, tk: (h, 0, 0))],
            scratch_shapes=[pltpu.VMEM((nq, HEAD_DIM, bq), F32), pltpu.VMEM((HEAD_DIM, bk), F32),
                            pltpu.VMEM((HEAD_DIM, bk), F32), pltpu.VMEM((bk, LANES), F32),
                            pltpu.VMEM((nq, SUBLANES, bq), F32), pltpu.VMEM((1, bq), F32)]),
        out_shape=[jax.ShapeDtypeStruct((N_HEADS * HEAD_DIM, S), BF16)] * 3
        + [jax.ShapeDtypeStruct((N_HEADS, 1, S), F32), jax.ShapeDtypeStruct((N_HEADS, 1, S), F32)],
        compiler_params=_cparams("parallel", "arbitrary"),
    )(tri_q, tri_k, kv, kv, qkv_t, qkv_t, do_t, o_t, lse, f_row, f_rep)


def _gate_mix(gates, ya, yb):
    S, D = ya.shape
    tr = min(ROW_TILE, S)

    def body(ga_ref, gb_ref, ya_ref, yb_ref, o_ref):
        o_ref[...] = (_sigmoid(ga_ref[...]) * ya_ref[...] + _sigmoid(gb_ref[...]) * yb_ref[...]).astype(BF16)

    col = lambda j: pl.BlockSpec((tr, D), lambda i, j=j: (i, j))
    return pl.pallas_call(
        body, name="gate_mix", grid=(S // tr,),
        in_specs=[col(0), col(1), col(0), col(0)],
        out_specs=col(0),
        out_shape=jax.ShapeDtypeStruct((S, D), BF16),
        compiler_params=_cparams("parallel"),
    )(gates, gates, ya, yb)


def _gate_bwd(dmix, gates, ya, yb):
    S, D = ya.shape
    tr = min(ROW_TILE, S)

    def body(dm_ref, ga_ref, gb_ref, ya_ref, yb_ref, dya_ref, dyb_ref, dg_ref):
        dm = dm_ref[...]
        sa, sb = _sigmoid(ga_ref[...]), _sigmoid(gb_ref[...])
        dya_ref[...] = (dm * sa).astype(BF16)
        dyb_ref[...] = (dm * sb).astype(BF16)
        dg_ref[:, 0:D] = ((dm * ya_ref[...]) * (sa * (1.0 - sa))).astype(BF16)
        dg_ref[:, D:] = ((dm * yb_ref[...]) * (sb * (1.0 - sb))).astype(BF16)

    col = lambda j: pl.BlockSpec((tr, D), lambda i, j=j: (i, j))
    return pl.pallas_call(
        body, name="gate_bwd", grid=(S // tr,),
        in_specs=[col(0), col(0), col(1), col(0), col(0)],
        out_specs=[col(0), col(0), pl.BlockSpec((tr, 2 * D), lambda i: (i, 0))],
        out_shape=[jax.ShapeDtypeStruct((S, D), BF16), jax.ShapeDtypeStruct((S, D), BF16),
                   jax.ShapeDtypeStruct((S, 2 * D), BF16)],
        compiler_params=_cparams("parallel"),
    )(dmix, gates, gates, ya, yb)


def _mesh_place():
    x, y, c = lax.axis_index("x"), lax.axis_index("y"), lax.axis_index("c")
    chips = [(1 - x, y), (x, 1 - y), (1 - x, 1 - y)]
    return x, y, c, chips


def _all_gather(shards):
    n = len(shards)

    def body(*refs):
        ins, outs = refs[:n], refs[n:2 * n]
        send_sems, recv_sems, local_sems = refs[2 * n:]
        x, y, c, chips = _mesh_place()
        me, sib = (x, y, c), (x, y, 1 - c)

        def copy(a, k, block, to, src=None):
            px, py, pc = block
            dst = outs[a].at[4 * px + 2 * py + pc]
            return pltpu.make_async_remote_copy(
                src_ref=dst if src is None else src, dst_ref=dst,
                send_sem=send_sems.at[a, k], recv_sem=recv_sems.at[a, k],
                device_id=to, device_id_type=MESH_ID)

        mine = [pltpu.make_async_copy(ins[a], outs[a].at[4 * x + 2 * y + c], local_sems.at[a]) for a in range(n)]
        for cp in mine:
            cp.start()
        first = []
        for a in range(n):
            first.append(copy(a, 0, me, sib, src=ins[a]))
            for j, chip in enumerate(chips):
                first.append(copy(a, 1 + j, me, (*chip, c), src=ins[a]))
        for cp in first:
            cp.start()
        passed = []
        for j, chip in enumerate(chips):
            for a in range(n):
                copy(a, 1 + j, (*chip, c), me).wait_recv()
                fwd = copy(a, 4 + j, (*chip, c), sib)
                fwd.start()
                passed.append(fwd)
        for a in range(n):
            copy(a, 0, sib, me).wait_recv()
            for j, chip in enumerate(chips):
                copy(a, 4 + j, (*chip, 1 - c), me).wait_recv()
        for cp in first + passed:
            cp.wait_send()
        for cp in mine:
            cp.wait()

    return pl.pallas_call(
        body, name="all_gather_weights",
        in_specs=[ANY] * n, out_specs=[ANY] * n,
        out_shape=[jax.ShapeDtypeStruct((N_DEV,) + s.shape, s.dtype) for s in shards],
        scratch_shapes=[pltpu.SemaphoreType.DMA((n, 7)), pltpu.SemaphoreType.DMA((n, 7)),
                        pltpu.SemaphoreType.DMA((n,))],
    )(*shards)


def _chip_partial_sum(blocks, got, core):
    R, C = got.shape[1:]
    tr = min(256, R)
    assert R % tr == 0

    def body(core_ref, a_ref, b_ref, s_ref, sb_ref):
        s = a_ref[...] + b_ref[...]
        s_ref[...] = s
        sb_ref[...] = s.astype(BF16)

    blk = pl.BlockSpec((None, tr, C), lambda k, i, core_ref: (k, i, 0))
    return pl.pallas_call(
        body, name="chip_partial_sum",
        grid_spec=pltpu.PrefetchScalarGridSpec(
            num_scalar_prefetch=1, grid=(4, R // tr),
            in_specs=[pl.BlockSpec((None, tr, C), lambda k, i, core_ref: (2 * k + core_ref[0], i, 0)), blk],
            out_specs=[blk, blk]),
        out_shape=[jax.ShapeDtypeStruct(got.shape, F32), jax.ShapeDtypeStruct(got.shape, BF16)],
        compiler_params=_cparams("parallel", "parallel"),
    )(core, blocks, got)


HBM_SPEC = pl.BlockSpec(memory_space=pltpu.HBM)
SEM_SPEC = pl.BlockSpec(memory_space=pltpu.SEMAPHORE)
FLIPS = [(dx, dy, dc) for dx in (0, 1) for dy in (0, 1) for dc in (0, 1) if (dx, dy, dc) != (0, 0, 0)]


def _flip(v, d):
    return 1 - v if d else v


def _gather_copies(srcs, lands, send_sems, recv_sems):
    x, y, c, _ = _mesh_place()
    sends, recvs = [], []
    for a in range(len(srcs)):
        for k, (dx, dy, dc) in enumerate(FLIPS):
            px, py, pc = _flip(x, dx), _flip(y, dy), _flip(c, dc)
            sem = len(FLIPS) * a + k
            common = dict(send_sem=send_sems.at[sem], recv_sem=recv_sems.at[sem],
                          device_id=(px, py, pc), device_id_type=MESH_ID)
            sends.append(pltpu.make_async_remote_copy(
                src_ref=srcs[a], dst_ref=lands[a].at[4 * x + 2 * y + c], **common))
            recvs.append(pltpu.make_async_remote_copy(
                src_ref=srcs[a], dst_ref=lands[a].at[4 * px + 2 * py + pc], **common))
    return sends, recvs


def _cores_copies(srcs, lands, send_sems, recv_sems):
    x, y, c, _ = _mesh_place()
    copies = []
    for a in range(len(srcs)):
        for k in range(4):
            copies.append(pltpu.make_async_remote_copy(
                src_ref=srcs[a].at[2 * k + (1 - c)], dst_ref=lands[a].at[k],
                send_sem=send_sems.at[4 * a + k], recv_sem=recv_sems.at[4 * a + k],
                device_id=(x, y, 1 - c), device_id_type=MESH_ID))
    return copies, copies


def _scatter_copies(srcs, lands, send_sems, recv_sems):
    x, y, c, chips = _mesh_place()
    sends = []
    for a in range(len(srcs)):
        for j, (px, py) in enumerate(chips):
            sends.append(pltpu.make_async_remote_copy(
                src_ref=srcs[a].at[2 * px + py], dst_ref=lands[a].at[j],
                send_sem=send_sems.at[3 * a + j], recv_sem=recv_sems.at[3 * a + j],
                device_id=(px, py, c), device_id_type=MESH_ID))
    return sends, sends


def _exchange_start(srcs, land_shapes, copies, n_copies, name):
    n = len(srcs)

    def body(*refs):
        src_refs, land_refs = refs[:n], refs[n:2 * n]
        send_sems, recv_sems = refs[2 * n], refs[2 * n + 1]
        token = refs[-1]
        sends, _ = copies(src_refs, land_refs, send_sems, recv_sems)
        for cp in sends:
            cp.start()
        token[...] = jnp.zeros_like(token)

    lands = [pltpu.with_memory_space_constraint(lax.empty(s.shape, s.dtype), pltpu.HBM) for s in land_shapes]
    srcs = [pltpu.with_memory_space_constraint(s, pltpu.HBM) for s in srcs]
    res = pl.pallas_call(
        body, name=name,
        out_shape=(pltpu.SemaphoreType.DMA((n * n_copies,)), pltpu.SemaphoreType.DMA((n * n_copies,)),
                   *[pltpu.HBM(s.shape, s.dtype) for s in srcs], *[pltpu.HBM(s.shape, s.dtype) for s in land_shapes],
                   jax.ShapeDtypeStruct((SUBLANES, LANES), F32)),
        in_specs=[HBM_SPEC] * (2 * n),
        out_specs=(SEM_SPEC, SEM_SPEC, *[HBM_SPEC] * (2 * n), pl.BlockSpec(memory_space=pltpu.VMEM)),
        input_output_aliases={i: 2 + i for i in range(2 * n)},
        compiler_params=pltpu.CompilerParams(has_side_effects=pltpu.SideEffectType.DATAFLOW_SIDE_EFFECTING),
    )(*srcs, *lands)
    return res[0], res[1], list(res[2:2 + n]), list(res[2 + n:2 + 2 * n]), res[-1]


def _exchange_wait(started, copies, after, name):
    send_sems, recv_sems, srcs, lands, _ = started
    n = len(srcs)

    def body(*refs):
        src_refs, land_refs = refs[:n], refs[n:2 * n]
        send_ref, recv_ref = refs[2 * n], refs[2 * n + 1]
        sends, recvs = copies(src_refs, land_refs, send_ref, recv_ref)
        for cp in sends:
            cp.wait_send()
        for cp in recvs:
            cp.wait_recv()

    res = pl.pallas_call(
        body, name=name,
        out_shape=tuple(pltpu.HBM(s.shape, s.dtype) for s in srcs + lands),
        in_specs=[HBM_SPEC] * (2 * n) + [SEM_SPEC, SEM_SPEC, ANY],
        out_specs=tuple([HBM_SPEC] * (2 * n)),
        input_output_aliases={i: i for i in range(2 * n)},
        compiler_params=pltpu.CompilerParams(has_side_effects=pltpu.SideEffectType.DATAFLOW_SIDE_EFFECTING),
    )(*srcs, *lands, send_sems, recv_sems, after)
    return list(res[:n]), list(res[n:])


def _all_reduce_small(vec):
    R = vec.shape[0]

    def body(v_ref, o_ref, sib_buf, chip_buf, send_sems, recv_sems):
        x, y, c, chips = _mesh_place()
        swap = pltpu.make_async_remote_copy(
            src_ref=v_ref, dst_ref=sib_buf, send_sem=send_sems.at[0], recv_sem=recv_sems.at[0],
            device_id=(x, y, 1 - c), device_id_type=MESH_ID)
        swap.start()
        swap.wait()
        my_chip = 2 * x + y
        chip_buf[my_chip] = v_ref[...] + sib_buf[...]
        sends = []
        for j, (px, py) in enumerate(chips):
            cp = pltpu.make_async_remote_copy(
                src_ref=chip_buf.at[my_chip], dst_ref=chip_buf.at[my_chip],
                send_sem=send_sems.at[1 + j], recv_sem=recv_sems.at[1 + j],
                device_id=(px, py, c), device_id_type=MESH_ID)
            cp.start()
            sends.append(cp)
        for j, (px, py) in enumerate(chips):
            pltpu.make_async_remote_copy(
                src_ref=chip_buf.at[2 * px + py], dst_ref=chip_buf.at[2 * px + py],
                send_sem=send_sems.at[1 + j], recv_sem=recv_sems.at[1 + j],
                device_id=(px, py, c), device_id_type=MESH_ID).wait_recv()
        for cp in sends:
            cp.wait_send()
        o_ref[...] = ((chip_buf[0] + chip_buf[1]) + chip_buf[2]) + chip_buf[3]

    vm = pl.BlockSpec(memory_space=pltpu.VMEM)
    return pl.pallas_call(
        body, name="all_reduce_small",
        in_specs=[vm], out_specs=vm,
        out_shape=jax.ShapeDtypeStruct(vec.shape, F32),
        scratch_shapes=[pltpu.VMEM((R, LANES), F32), pltpu.VMEM((4, R, LANES), F32),
                        pltpu.SemaphoreType.DMA((4,)), pltpu.SemaphoreType.DMA((4,))],
    )(vec)


def _adamw_math(w, g, m, v):
    m = ADAM_B1 * m + (1.0 - ADAM_B1) * g
    v = ADAM_B2 * v + (1.0 - ADAM_B2) * (g * g)
    m_hat = m / (1.0 - ADAM_B1 ** ADAM_STEP)
    v_hat = v / (1.0 - ADAM_B2 ** ADAM_STEP)
    delta = -ADAM_LR * (m_hat / (jnp.sqrt(v_hat) + ADAM_EPS) + ADAM_WD * w)
    return delta, m, v


def _adamw(w, m, v, g_own, g_got, chip, name):
    R, C = w.shape
    tr = R if R * C <= 256 * D_MODEL else 256
    assert R % tr == 0
    n_got = g_got.shape[0]

    def body(*refs):
        w_ref, m_ref, v_ref, go_ref = refs[1:5]
        got = refs[5:5 + n_got]
        g_ref, d_ref, nm_ref, nv_ref = refs[5 + n_got:]
        g = go_ref[...]
        for r in got:
            g = g + r[...].astype(F32)
        delta, m_new, v_new = _adamw_math(w_ref[...], g, m_ref[...], v_ref[...])
        g_ref[...] = g
        d_ref[...] = delta
        nm_ref[...] = m_new
        nv_ref[...] = v_new

    blk = pl.BlockSpec((tr, C), lambda i, chip_ref: (i, 0))
    own_spec = pl.BlockSpec((None, tr, C), lambda i, chip_ref: (chip_ref[0], i, 0))
    got_specs = [pl.BlockSpec((None, tr, C), lambda i, chip_ref, j=j: (j, i, 0)) for j in range(n_got)]
    return pl.pallas_call(
        body, name=name,
        grid_spec=pltpu.PrefetchScalarGridSpec(
            num_scalar_prefetch=1, grid=(R // tr,),
            in_specs=[blk] * 3 + [own_spec] + got_specs, out_specs=[blk] * 4),
        out_shape=[jax.ShapeDtypeStruct((R, C), F32)] * 4,
        compiler_params=_cparams("parallel"),
    )(chip, w, m, v, g_own, *([g_got] * n_got))


def _block_diag_pairs(wa, wx):
    def pairs(w):
        w = w.reshape(N_GROUPS, 2, LRU_BW, LRU_BW)
        z = jnp.zeros((N_GROUPS, LRU_BW, LRU_BW), w.dtype)
        top = jnp.concatenate([w[:, 0], z], axis=2)
        bot = jnp.concatenate([z, w[:, 1]], axis=2)
        return jnp.concatenate([top, bot], axis=1)
    return jnp.concatenate([pairs(wa), pairs(wx)], axis=2).astype(BF16)


def _block_diag_unpair(dbd):
    def unpair(g):
        blocks = jnp.stack([g[:, :LRU_BW, :LRU_BW], g[:, LRU_BW:, LRU_BW:]], axis=1)
        return blocks.reshape(LRU_BLOCKS, LRU_BW, LRU_BW)
    return unpair(dbd[:, :, :LANES]), unpair(dbd[:, :, LANES:])


def _local_step(x, target, W, small, late_weights=None, hooks=None):
    def hook(name, *args):
        return hooks[name](*args) if hooks is not None else (None, 0.0)

    S, D = x.shape
    g1, g2, g3 = small["norm_mix_g"], small["norm_mlp_g"], small["norm_final_g"]
    cw, cb = small["conv_w"], small["conv_b"].reshape(1, D)
    ba, bx, lam = (small[k].reshape(1, D) for k in ("lru_ba", "lru_bx", "lru_lambda"))
    fb = jnp.pad(small["forget_b"], (0, LANES - N_HEADS)).reshape(1, LANES)
    bd = _block_diag_pairs(small["lru_wa"], small["lru_wx"])
    big = dict(tm=1024, tn=1024)

    u = _norm_fwd(x, g1, "norm_mix")
    (xg,) = _mm([(u, W["in_xg"])], tks=[D], outs=[F32], name="proj_xg", **big)
    (qkv_t,) = _mm([(W["in_qkv_t"], u)], tb=True, tks=[D], outs=[BF16], name="proj_qkv_t", **big)
    (kv,) = _mm([(u, W["in_kv"])], tks=[D], outs=[BF16], name="proj_kv", **big)
    (gates,) = _mm([(u, W["in_gates"])], tks=[D], outs=[F32], name="proj_gates", **big)
    (fl,) = _mm([(u, W["in_f"])], tks=[D], outs=[F32], name="proj_forget", **big)
    h, yain = _lru_fwd(xg, cw, cb, bd, ba, bx, lam)
    fcum, f_rep = _forget_cumsum(fl, fb)
    f_row = fcum[:, :N_HEADS].T.reshape(N_HEADS, 1, S)
    ob_t, lse = _attn_fwd(kv, qkv_t, f_row, f_rep)
    if late_weights is not None:
        W = {**W, **late_weights(lse)}
    (ya,) = _mm([(yain, W["branch_a"])], tks=[D], outs=[F32], name="branch_a", **big)
    (yb,) = _mm([(ob_t, W["branch_b"])], ta=True, tks=[D], outs=[F32], name="branch_b", **big)
    mix = _gate_mix(gates, ya, yb)
    (x1,) = _mm([(mix, W["out"])], tks=[D], outs=[F32], name="out_proj", extra=(x,),
                epi=lambda acc, res: (res + acc,), **big)
    m = _norm_fwd(x1, g2, "norm_mlp")
    relu, hh = _mm([(m, W["up"])], tks=[D], outs=[BF16, BF16], name="mlp_up",
                   epi=lambda acc: (jnp.maximum(acc, 0.0), jnp.square(jnp.maximum(acc, 0.0))), **big)
    deep = dict(tm=512, tn=1024, tks=[D_FF])
    wgrad = dict(tm=1024, tn=512, tks=[min(4096, S)])
    (x2,) = _mm([(hh, W["down"])], outs=[F32], name="mlp_down", extra=(x1,),
                epi=lambda acc, res: (res + acc,), **deep)
    loss_acc, dg3, dx2, dx2b = _final_norm_loss(x2, target, g3)

    (dhpre,) = _mm([(dx2b, W["down"])], tb=True, tks=[D], outs=[BF16], name="d_mlp_act", extra=(relu,),
                   epi=lambda acc, r: (acc * (2.0 * r.astype(F32)),), **big)
    (dw_down,) = _mm([(hh, dx2b)], ta=True, outs=[F32], name="dw_down", **wgrad)
    (dm,) = _mm([(dhpre, W["up"])], tb=True, outs=[F32], name="d_mlp_in", **deep)
    assert wgrad["tn"] == D_FF // N_DEV
    (dw_up,) = _mm([(m, dhpre)], ta=True, outs=[F32], name="dw_up", col_blocked=True, **wgrad)
    dx1, dx1b, dg2 = _norm_bwd(dm, x1, g2, dx2, "norm_mlp_bwd")
    (dmix,) = _mm([(dx1b, W["out"])], tb=True, tks=[D], outs=[F32], name="d_mix", **big)
    (dw_out,) = _mm([(mix, dx1b)], ta=True, outs=[F32], name="dw_out", **wgrad)
    dya, dyb, dgates = _gate_bwd(dmix, gates, ya, yb)
    (dob_t,) = _mm([(W["branch_b"], dyb)], tb=True, tks=[D], outs=[BF16], name="d_attn_out_t", **big)
    (dw_b,) = _mm([(ob_t, dyb)], outs=[F32], name="dw_branch_b", **wgrad)
    (dyain,) = _mm([(dya, W["branch_a"])], tb=True, tks=[D], outs=[F32], name="d_lru_out", **big)
    (dw_a,) = _mm([(yain, dya)], ta=True, outs=[F32], name="dw_branch_a", **wgrad)
    early = dict(w_branch_a=dw_a, w_branch_b=dw_b, w_out=dw_out, w_up=dw_up, w_down=dw_down)
    early_state, zero = hook("early_start", early)
    dq_t, dk_t, dv_t, dfk, dfq = _attn_bwd(kv, qkv_t, dob_t, ob_t, lse + zero, f_row, f_rep)
    early_state, zero = hook("early_mid", early_state, dfq)
    dF = jnp.pad((dfk.reshape(N_HEADS, S) + dfq.reshape(N_HEADS, S)).T, ((0, 0), (0, LANES - N_HEADS)))
    dfl, dfb = _forget_bwd(dF, fl, fb)
    dxg, dcw, dcb, dba, dbx, dlam, dbd = _lru_bwd(xg, h, dyain, cw, cb, bd, ba, bx, lam + zero)
    dw_in_parts = [
        _mm([(u, dxg)], ta=True, outs=[F32], name="dw_in_xg", **wgrad)[0],
        _mm([(dq_t, u)], outs=[F32], name="dw_in_q_t", **wgrad)[0].T,
        _mm([(dk_t, u)], outs=[F32], name="dw_in_k_t", **wgrad)[0].T,
        _mm([(dv_t, u)], outs=[F32], name="dw_in_v_t", **wgrad)[0].T,
        _mm([(u, dgates)], ta=True, outs=[F32], name="dw_in_gates", **wgrad)[0],
        _mm([(u, dfl)], ta=True, outs=[F32], name="dw_in_forget", **wgrad)[0][:, :N_HEADS],
    ]
    dw_in = jnp.concatenate(dw_in_parts, axis=1)
    in_state, zero = hook("in_start", dw_in)
    wq_t, wk_t, wv_t = (W["in_qkv_t"][D * i:D * (i + 1)] for i in range(3))
    (du_tok,) = _mm([(dxg, W["in_xg"]), (dgates, W["in_gates"]), (dfl, W["in_f"] + jnp.asarray(zero, BF16))],
                    tb=True, tks=[2 * D, 2 * D, LANES], outs=[F32], name="d_norm_mix_out_tok", tm=1024, tn=512)
    in_state, zero = hook("in_mid", in_state, du_tok)
    (du,) = _mm([(dq_t, wq_t + jnp.asarray(zero, BF16)), (dk_t, wk_t), (dv_t, wv_t)], ta=True, tks=[D, D, D],
                outs=[F32], name="d_norm_mix_out", extra=(du_tok,), epi=lambda acc, prev: (prev + acc,),
                tm=1024, tn=512)
    grad_x, _, dg1 = _norm_bwd(du, x, g1, dx1, "norm_mix_bwd")

    dwa, dwx = _block_diag_unpair(dbd)
    big_grads = dict(early, w_in=dw_in)
    small_grads = dict(norm_mix_g=dg1.reshape(D), conv_w=dcw, conv_b=dcb.reshape(D), lru_wa=dwa, lru_ba=dba.reshape(D),
                       lru_wx=dwx, lru_bx=dbx.reshape(D), lru_lambda=dlam.reshape(D), forget_b=dfb[0, :N_HEADS],
                       norm_mlp_g=dg2.reshape(D), norm_final_g=dg3.reshape(D))
    return loss_acc[0, 0], grad_x, big_grads, small_grads, (early_state, in_state)


SMALL_NAMES = ("norm_mix_g", "conv_b", "lru_wa", "lru_ba", "lru_wx", "lru_bx", "lru_lambda", "forget_b",
               "norm_mlp_g", "norm_final_g")
TILE_ELEMS = SUBLANES * LANES


def _pack_small(parts):
    rows = []
    for p in parts:
        flat = p.reshape(-1)
        flat = jnp.pad(flat, (0, (-flat.shape[0]) % TILE_ELEMS))
        rows.append(flat.reshape(-1, LANES))
    return jnp.concatenate(rows, axis=0)


def _packed_rows(shape):
    return -(-math.prod(shape) // TILE_ELEMS) * SUBLANES


def _adamw_small(g_packed, g_conv_w, weights, moms, vels):
    def rows_view(a):
        flat = a.reshape(-1)
        flat = jnp.pad(flat, (0, (-flat.shape[0]) % LANES))
        return flat.reshape(-1, LANES)

    names = SMALL_NAMES + ("conv_w",)
    views = [[rows_view(src[k]) for k in names] for src in (weights, moms, vels)]
    n = len(names)
    starts, r = [], 0
    for k in SMALL_NAMES:
        starts.append(r)
        r += _packed_rows(weights[k].shape)

    def body(*refs):
        gp_ref, gc_ref = refs[0], refs[1]
        w_refs, m_refs, v_refs = refs[2:2 + n], refs[2 + n:2 + 2 * n], refs[2 + 2 * n:2 + 3 * n]
        outs = refs[2 + 3 * n:]
        for i in range(n):
            rows = w_refs[i].shape[0]
            g = gc_ref[...] if i == n - 1 else gp_ref[starts[i]:starts[i] + rows, :]
            delta, m_new, v_new = _adamw_math(w_refs[i][...], g, m_refs[i][...], v_refs[i][...])
            for o_ref, val in zip(outs[4 * i:4 * i + 4], (g, delta, m_new, v_new)):
                o_ref[...] = val

    vm = pl.BlockSpec(memory_space=pltpu.VMEM)
    out_shape = [jax.ShapeDtypeStruct(v.shape, F32) for v in views[0] for _ in range(4)]
    res = pl.pallas_call(
        body, name="adamw_small",
        in_specs=[vm] * (2 + 3 * n), out_specs=[vm] * (4 * n), out_shape=out_shape,
    )(g_packed, g_conv_w, *views[0], *views[1], *views[2])
    dicts = ({}, {}, {}, {})
    for i, k in enumerate(names):
        size = math.prod(weights[k].shape)
        for d, arr in zip(dicts, res[4 * i:4 * i + 4]):
            d[k] = arr.reshape(-1)[:size].reshape(weights[k].shape)
    return dicts


BIG_NAMES = ("w_in", "w_branch_a", "w_branch_b", "w_out", "w_up", "w_down")
WEIGHT_ORDER = ("norm_mix_g", "w_in", "conv_w", "conv_b", "lru_wa", "lru_ba", "lru_wx", "lru_bx", "lru_lambda",
                "forget_b", "w_branch_a", "w_branch_b", "w_out", "norm_mlp_g", "w_up", "w_down", "norm_final_g")


def _to_dest_blocks(name, g):
    if g.ndim == 3:
        return g
    if name in ("w_in", "w_up"):
        return g.reshape(g.shape[0], N_DEV, g.shape[1] // N_DEV).transpose(1, 0, 2)
    return g.reshape(N_DEV, g.shape[0] // N_DEV, g.shape[1])


def kernel(x, norm_mix_g, w_in, conv_w, conv_b, lru_wa, lru_ba, lru_wx, lru_bx, lru_lambda, forget_b, w_branch_a, w_branch_b, w_out, norm_mlp_g, w_up, w_down, norm_final_g, loss_target, m_norm_mix_g, m_w_in, m_conv_w, m_conv_b, m_lru_wa, m_lru_ba, m_lru_wx, m_lru_bx, m_lru_lambda, m_forget_b, m_w_branch_a, m_w_branch_b, m_w_out, m_norm_mlp_g, m_w_up, m_w_down, m_norm_final_g, v_norm_mix_g, v_w_in, v_conv_w, v_conv_b, v_lru_wa, v_lru_ba, v_lru_wx, v_lru_bx, v_lru_lambda, v_forget_b, v_w_branch_a, v_w_branch_b, v_w_out, v_norm_mlp_g, v_w_up, v_w_down, v_norm_final_g):
    weights = dict(norm_mix_g=norm_mix_g, w_in=w_in, conv_w=conv_w, conv_b=conv_b, lru_wa=lru_wa, lru_ba=lru_ba,
                   lru_wx=lru_wx, lru_bx=lru_bx, lru_lambda=lru_lambda, forget_b=forget_b, w_branch_a=w_branch_a,
                   w_branch_b=w_branch_b, w_out=w_out, norm_mlp_g=norm_mlp_g, w_up=w_up, w_down=w_down,
                   norm_final_g=norm_final_g)
    moms = dict(norm_mix_g=m_norm_mix_g, w_in=m_w_in, conv_w=m_conv_w, conv_b=m_conv_b, lru_wa=m_lru_wa,
                lru_ba=m_lru_ba, lru_wx=m_lru_wx, lru_bx=m_lru_bx, lru_lambda=m_lru_lambda, forget_b=m_forget_b,
                w_branch_a=m_w_branch_a, w_branch_b=m_w_branch_b, w_out=m_w_out, norm_mlp_g=m_norm_mlp_g,
                w_up=m_w_up, w_down=m_w_down, norm_final_g=m_norm_final_g)
    vels = dict(norm_mix_g=v_norm_mix_g, w_in=v_w_in, conv_w=v_conv_w, conv_b=v_conv_b, lru_wa=v_lru_wa,
                lru_ba=v_lru_ba, lru_wx=v_lru_wx, lru_bx=v_lru_bx, lru_lambda=v_lru_lambda, forget_b=v_forget_b,
                w_branch_a=v_w_branch_a, w_branch_b=v_w_branch_b, w_out=v_w_out, norm_mlp_g=v_norm_mlp_g,
                w_up=v_w_up, w_down=v_w_down, norm_final_g=v_norm_final_g)
    S, D = x.shape[1], x.shape[2]
    me = 4 * lax.axis_index("x") + 2 * lax.axis_index("y") + lax.axis_index("c")

    core = lax.axis_index("c").astype(jnp.int32).reshape(1)
    chip = (2 * lax.axis_index("x") + lax.axis_index("y")).astype(jnp.int32).reshape(1)
    late_names = BIG_NAMES[1:]

    win_g, cw_g = _all_gather([w_in.astype(BF16), conv_w])
    late_shards = [weights[k].astype(BF16) for k in late_names]
    gather = _exchange_start(late_shards, [jax.ShapeDtypeStruct((N_DEV,) + s.shape, BF16) for s in late_shards],
                             _gather_copies, len(FLIPS), "gather_late_start")
    w_in_full = win_g.transpose(1, 0, 2).reshape(D, -1)
    cuts = (0, 2 * D, 5 * D, 7 * D)
    W = dict(in_xg=w_in_full[:, cuts[0]:cuts[1]], in_qkv_t=w_in_full[:, cuts[1]:cuts[2]].T,
             in_kv=w_in_full[:, cuts[1] + D:cuts[2]], in_gates=w_in_full[:, cuts[2]:cuts[3]],
             in_f=jnp.pad(w_in_full[:, cuts[3]:], ((0, 0), (0, LANES - N_HEADS))))
    small = {k: weights[k] for k in SMALL_NAMES}
    small["conv_w"] = cw_g.transpose(1, 0, 2).reshape(CONV_W, D)
    small["norm_mix_g"] = norm_mix_g + gather[4][0, 0]

    def late_weights(after):
        shards, lands = _exchange_wait(gather, _gather_copies, after, "gather_late_wait")
        wa_g, wb_g, wo_g, wup_g, wdn_g = (
            lax.dynamic_update_slice_in_dim(land, shard[None], me, axis=0) for land, shard in zip(lands, shards))
        return dict(branch_a=wa_g.reshape(D, D), branch_b=wb_g.reshape(D, D), out=wo_g.reshape(D, D),
                    up=wup_g.transpose(1, 0, 2).reshape(D, D_FF), down=wdn_g.reshape(D_FF, D))

    def cores_start(names, grads_by_name, tag):
        blocks = [_to_dest_blocks(k, grads_by_name[k]) for k in names]
        started = _exchange_start(blocks, [jax.ShapeDtypeStruct((4,) + b.shape[1:], F32) for b in blocks],
                                  _cores_copies, 4, "cores_" + tag + "_start")
        return started, started[4][0, 0]

    def chips_start(started, after, tag):
        blocks, got = _exchange_wait(started, _cores_copies, after, "cores_" + tag + "_wait")
        sums = [_chip_partial_sum(b, g, core) for b, g in zip(blocks, got)]
        wire = [s[1] for s in sums]
        scatter = _exchange_start(wire, [jax.ShapeDtypeStruct((3,) + s.shape[1:], BF16) for s in wire],
                                  _scatter_copies, 3, "scatter_" + tag + "_start")
        return (sums, scatter), scatter[4][0, 0]

    hooks = dict(early_start=lambda g: cores_start(late_names, g, "early"),
                 early_mid=lambda st, after: chips_start(st, after, "early"),
                 in_start=lambda g: cores_start(BIG_NAMES[:1], dict(w_in=g), "w_in"),
                 in_mid=lambda st, after: chips_start(st, after, "w_in"))
    loss_part, grad_x, _, small_grads, ((early_sums, early_scatter), (in_sums, in_scatter)) = _local_step(
        x.reshape(S, D), loss_target.reshape(S, D), W, small, late_weights, hooks)
    loss = lax.psum(loss_part, MESH_AXES)
    _, early_others = _exchange_wait(early_scatter, _scatter_copies, grad_x, "scatter_early_wait")
    _, in_others = _exchange_wait(in_scatter, _scatter_copies, grad_x, "scatter_w_in_wait")
    sums = list(in_sums) + list(early_sums)
    others = list(in_others) + list(early_others)

    reduced = _all_reduce_small(_pack_small([small_grads[k] for k in SMALL_NAMES] + [small_grads["conv_w"]]))
    cw_full = reduced[reduced.shape[0] - _packed_rows((CONV_W, D)):].reshape(CONV_W, D)
    cw_cols = lax.dynamic_slice_in_dim(cw_full, me * (D // N_DEV), D // N_DEV, axis=1)

    grads, deltas, new_m, new_v = _adamw_small(reduced, cw_cols, weights, moms, vels)
    for k, s, g_got in zip(BIG_NAMES, sums, others):
        grads[k], deltas[k], new_m[k], new_v[k] = _adamw(weights[k], moms[k], vels[k], s[0], g_got, chip, "adamw_" + k)

    return (loss, grad_x.reshape(1, S, D), *[grads[k] for k in WEIGHT_ORDER], *[deltas[k] for k in WEIGHT_ORDER],
            *[new_m[k] for k in WEIGHT_ORDER], *[new_v[k] for k in WEIGHT_ORDER])
```

```python
import functools
import math

import jax
import jax.numpy as jnp
from jax import lax
from jax.experimental import pallas as pl
from jax.experimental.pallas import tpu as pltpu

F32 = jnp.float32
BF16 = jnp.bfloat16

D_MODEL = 1024
N_HEADS = 8
HEAD_DIM = 128
D_FF = 4096
LRU_BLOCKS = 16
LRU_BW = 64
LRU_C = 8.0
CONV_W = 4
RMS_EPS = 1e-6
N_DEV = 8
LANES = 128
SUBLANES = 8
N_GROUPS = D_MODEL // LANES
VMEM_LIMIT_BYTES = 52 * 1024 * 1024
ATTN_SCALE = 1.0 / math.sqrt(HEAD_DIM)
LOG2E = math.log2(math.e)
NEG_BIG = -1e30
ADAM_LR = 0.001
ADAM_B1 = 0.9
ADAM_B2 = 0.999
ADAM_EPS = 1e-08
ADAM_WD = 0.01
ADAM_STEP = 10
ATTN_BLOCK = 2048
ATTN_Q_TILES = 2
ATTN_STRIP = 256
LRU_CHUNK = 256
ROW_TILE = 512
MESH_AXES = ("x", "y", "c")
MESH_ID = pl.DeviceIdType.MESH
ANY = pl.BlockSpec(memory_space=pl.ANY)

NT_DIMS = (((1,), (1,)), ((), ()))
TN_DIMS = (((0,), (0,)), ((), ()))
NN_DIMS = (((1,), (0,)), ((), ()))


def _cparams(*sem):
    return pltpu.CompilerParams(dimension_semantics=sem if sem else None, vmem_limit_bytes=VMEM_LIMIT_BYTES)


def _sigmoid(x):
    return 0.5 * (jnp.tanh(0.5 * x) + 1.0)


def _log1p_pos(e):
    u = 1.0 + e
    return jnp.where(u == 1.0, e, jnp.log(u) * (e / (u - 1.0)))


def _softplus(z):
    return jnp.maximum(z, 0.0) + _log1p_pos(jnp.exp(-jnp.abs(z)))


def _expm1_neg(x):
    series = x * (1.0 + x * 0.5 * (1.0 + x * (1.0 / 3.0) * (1.0 + x * 0.25)))
    return jnp.where(x > -0.03, series, jnp.exp(x) - 1.0)


GELU_C = math.sqrt(2.0 / math.pi)
GELU_K = 0.044715


def _gelu(x):
    return 0.5 * x * (1.0 + jnp.tanh(GELU_C * (x + GELU_K * (x * x * x))))


def _gelu_and_grad(x):
    t = jnp.tanh(GELU_C * (x + GELU_K * (x * x * x)))
    g = 0.5 * x * (1.0 + t)
    dg = 0.5 * (1.0 + t) + 0.5 * x * (1.0 - t * t) * (GELU_C * (1.0 + 3.0 * GELU_K * (x * x)))
    return g, dg


def _mm(pairs, *, ta=False, tb=False, tm, tn, tks, outs, name, epi=None, extra=(), col_blocked=False):
    n_pairs, n_extra, n_out = len(pairs), len(extra), len(outs)
    tas = list(ta) if isinstance(ta, (list, tuple)) else [ta] * n_pairs
    tbs = list(tb) if isinstance(tb, (list, tuple)) else [tb] * n_pairs
    a0, b0 = pairs[0]
    M = a0.shape[1] if tas[0] else a0.shape[0]
    N = b0.shape[0] if tbs[0] else b0.shape[1]
    tm, tn = min(tm, M), min(tn, N)
    nks, offs = [], []
    for (a, b), tk, pta in zip(pairs, tks, tas):
        K = a.shape[0] if pta else a.shape[1]
        assert K % tk == 0 and M % tm == 0 and N % tn == 0
        offs.append(sum(nks))
        nks.append(K // tk)
    nk_total = sum(nks)
    dims = [(((0 if pta else 1,), (1 if ptb else 0,)), ((), ())) for pta, ptb in zip(tas, tbs)]

    def kmap(off, nk):
        return lambda k: jnp.clip(k - off, 0, nk - 1)

    in_specs, operands = [], []
    for (a, b), tk, off, nk, pta, ptb in zip(pairs, tks, offs, nks, tas, tbs):
        km = kmap(off, nk)
        if pta:
            in_specs.append(pl.BlockSpec((tk, tm), lambda i, j, k, km=km: (km(k), i)))
        else:
            in_specs.append(pl.BlockSpec((tm, tk), lambda i, j, k, km=km: (i, km(k))))
        if ptb:
            in_specs.append(pl.BlockSpec((tn, tk), lambda i, j, k, km=km: (j, km(k))))
        else:
            in_specs.append(pl.BlockSpec((tk, tn), lambda i, j, k, km=km: (km(k), j)))
        operands += [a, b]
    for e in extra:
        in_specs.append(pl.BlockSpec((tm, tn), lambda i, j, k: (i, j)))
        operands.append(e)

    def body(*refs):
        ab = refs[:2 * n_pairs]
        ex = refs[2 * n_pairs:2 * n_pairs + n_extra]
        o = refs[2 * n_pairs + n_extra:2 * n_pairs + n_extra + n_out]
        k = pl.program_id(2)

        def finish(acc):
            res = epi(acc, *[e[...] for e in ex]) if epi is not None else (acc,)
            for r, oref in zip(res, o):
                oref[...] = r.astype(oref.dtype)

        if nk_total == 1:
            finish(lax.dot_general(ab[0][...], ab[1][...], dims[0], preferred_element_type=F32))
            return
        acc = refs[-1]
        for p in range(n_pairs):
            a_ref, b_ref = ab[2 * p], ab[2 * p + 1]

            @pl.when((k >= offs[p]) & (k < offs[p] + nks[p]))
            def _(a_ref=a_ref, b_ref=b_ref, pdims=dims[p]):
                prod = lax.dot_general(a_ref[...], b_ref[...], pdims, preferred_element_type=F32)

                @pl.when(k == 0)
                def _():
                    acc[...] = prod

                @pl.when(k > 0)
                def _():
                    acc[...] += prod

        @pl.when(k == nk_total - 1)
        def _():
            finish(acc[...])

    return pl.pallas_call(
        body,
        name=name,
        grid=(M // tm, N // tn, nk_total),
        in_specs=in_specs,
        out_specs=[pl.BlockSpec((None, tm, tn), lambda i, j, k: (j, i, 0)) if col_blocked
                   else pl.BlockSpec((tm, tn), lambda i, j, k: (i, j)) for _ in outs],
        out_shape=[jax.ShapeDtypeStruct((N // tn, M, tn) if col_blocked else (M, N), dt) for dt in outs],
        scratch_shapes=[] if nk_total == 1 else [pltpu.VMEM((tm, tn), F32)],
        compiler_params=_cparams("parallel", "parallel", "arbitrary"),
    )(*operands)


def _norm_fwd(x, g, name):
    S, D = x.shape
    tr = min(ROW_TILE, S)

    def body(x_ref, g_ref, o_ref):
        xv = x_ref[...]
        r = lax.rsqrt(jnp.mean(xv * xv, axis=-1, keepdims=True) + RMS_EPS)
        o_ref[...] = ((xv * r) * g_ref[...]).astype(o_ref.dtype)

    return pl.pallas_call(
        body, name=name, grid=(S // tr,),
        in_specs=[pl.BlockSpec((tr, D), lambda i: (i, 0)), pl.BlockSpec((1, D), lambda i: (0, 0))],
        out_specs=pl.BlockSpec((tr, D), lambda i: (i, 0)),
        out_shape=jax.ShapeDtypeStruct((S, D), BF16),
        compiler_params=_cparams("parallel"),
    )(x, g.reshape(1, D))


def _rms_bwd_rows(dy, xv, g):
    r = lax.rsqrt(jnp.mean(xv * xv, axis=-1, keepdims=True) + RMS_EPS)
    xn = xv * r
    dxn = dy * g
    dx = r * (dxn - xn * jnp.mean(dxn * xn, axis=-1, keepdims=True))
    dg = jnp.sum(dy * xn, axis=0, keepdims=True)
    return dx, dg


def _norm_bwd(dy, x, g, dres, name):
    S, D = x.shape
    tr = min(ROW_TILE, S)

    def body(dy_ref, x_ref, g_ref, dres_ref, dx_ref, dxb_ref, dg_ref):
        dx, dg = _rms_bwd_rows(dy_ref[...], x_ref[...], g_ref[...])
        dx = dres_ref[...] + dx
        dx_ref[...] = dx
        dxb_ref[...] = dx.astype(BF16)

        @pl.when(pl.program_id(0) == 0)
        def _():
            dg_ref[...] = jnp.zeros_like(dg_ref)

        dg_ref[...] += dg

    row = pl.BlockSpec((tr, D), lambda i: (i, 0))
    vec = pl.BlockSpec((1, D), lambda i: (0, 0))
    return pl.pallas_call(
        body, name=name, grid=(S // tr,),
        in_specs=[row, row, vec, row],
        out_specs=[row, row, vec],
        out_shape=[jax.ShapeDtypeStruct((S, D), F32), jax.ShapeDtypeStruct((S, D), BF16),
                   jax.ShapeDtypeStruct((1, D), F32)],
        compiler_params=_cparams("arbitrary"),
    )(dy, x, g.reshape(1, D), dres)


def _final_norm_loss(x2, target, g):
    S, D = x2.shape
    tr = min(ROW_TILE, S)

    def body(x_ref, t_ref, g_ref, loss_ref, dg_ref, dx_ref, dxb_ref):
        xv = x_ref[...]
        gv = g_ref[...]
        r = lax.rsqrt(jnp.mean(xv * xv, axis=-1, keepdims=True) + RMS_EPS)
        y = (xv * r) * gv
        err = y - t_ref[...]
        part = 0.5 * jnp.sum(jnp.mean(err * err, axis=-1, keepdims=True), axis=0, keepdims=True)
        dy = err * (1.0 / D)
        dx, dg = _rms_bwd_rows(dy, xv, gv)
        dx_ref[...] = dx
        dxb_ref[...] = dx.astype(BF16)

        @pl.when(pl.program_id(0) == 0)
        def _():
            dg_ref[...] = jnp.zeros_like(dg_ref)
            loss_ref[...] = jnp.zeros_like(loss_ref)

        dg_ref[...] += dg
        loss_ref[...] += jnp.broadcast_to(part, loss_ref.shape)

    row = pl.BlockSpec((tr, D), lambda i: (i, 0))
    vec = pl.BlockSpec((1, D), lambda i: (0, 0))
    return pl.pallas_call(
        body, name="final_norm_loss", grid=(S // tr,),
        in_specs=[row, row, vec],
        out_specs=[pl.BlockSpec((SUBLANES, LANES), lambda i: (0, 0)), vec, row, row],
        out_shape=[jax.ShapeDtypeStruct((SUBLANES, LANES), F32), jax.ShapeDtypeStruct((1, D), F32),
                   jax.ShapeDtypeStruct((S, D), F32), jax.ShapeDtypeStruct((S, D), BF16)],
        compiler_params=_cparams("arbitrary"),
    )(x2, target, g.reshape(1, D))


def _lru_gates(xa, bd_j, ba_j, bx_j, sp_j):
    z = jnp.dot(xa.astype(BF16), bd_j, preferred_element_type=F32)
    r = _sigmoid(z[:, :LANES] + ba_j)
    ig = _sigmoid(z[:, LANES:] + bx_j)
    log_a = (-LRU_C) * r * sp_j
    a = jnp.exp(log_a)
    mult = jnp.sqrt(-_expm1_neg(2.0 * log_a))
    return r, ig, a, mult


def _conv_rows(xpad, cw_ref, cb_ref, sl, tc):
    out = jnp.broadcast_to(cb_ref[:, sl], (tc, LANES))
    for k in range(CONV_W):
        out = out + xpad[pl.ds(SUBLANES - (CONV_W - 1) + k, tc), sl] * cw_ref[k:k + 1, sl]
    return out


def _lru_fwd(xg, cw, cb, bd, ba, bx, lam):
    S = xg.shape[0]
    D = D_MODEL
    tc = min(LRU_CHUNK, S)
    hb = tc // SUBLANES

    def body(xl_ref, halo_ref, g_ref, cw_ref, cb_ref, bd_ref, ba_ref, bx_ref, lam_ref,
             h_ref, y_ref, xpad, a_s, b_s, carry):
        i = pl.program_id(0)

        @pl.when(i == 0)
        def _():
            carry[...] = jnp.zeros_like(carry)

        xpad[0:SUBLANES, :] = jnp.where(i > 0, halo_ref[...], 0.0)
        xpad[SUBLANES:, :] = xl_ref[...]
        for j in range(N_GROUPS):
            sl = slice(LANES * j, LANES * (j + 1))
            xa = _conv_rows(xpad, cw_ref, cb_ref, sl, tc)
            sp = _softplus(-lam_ref[:, sl])
            _, ig, a, mult = _lru_gates(xa, bd_ref[j], ba_ref[:, sl], bx_ref[:, sl], sp)
            a_s[:, sl] = a
            b_s[:, sl] = mult * (ig * xa)

        row = lax.broadcasted_iota(jnp.int32, (SUBLANES, D), 0)

        def step(t, c):
            o = pl.multiple_of(t * SUBLANES, SUBLANES)
            A = a_s[pl.ds(o, SUBLANES), :]
            B = b_s[pl.ds(o, SUBLANES), :]
            for d in (1, 2, 4):
                keep = row >= d
                a_sh = jnp.where(keep, pltpu.roll(A, d, 0), 1.0)
                b_sh = jnp.where(keep, pltpu.roll(B, d, 0), 0.0)
                B = A * b_sh + B
                A = A * a_sh
            hh = A * c + B
            h_ref[pl.ds(o, SUBLANES), :] = hh
            return jnp.broadcast_to(hh[SUBLANES - 1:SUBLANES, :], (SUBLANES, D))

        carry[...] = lax.fori_loop(0, hb, step, carry[...])
        y_ref[...] = (_gelu(g_ref[...]) * h_ref[...]).astype(BF16)

    row_spec = lambda col: pl.BlockSpec((tc, D), lambda i, col=col: (i, col))
    halo = pl.BlockSpec((SUBLANES, D), lambda i: (jnp.maximum(i * hb - 1, 0), 0))
    full = lambda shape: pl.BlockSpec(shape, lambda i: tuple(0 for _ in shape))
    return pl.pallas_call(
        body, name="lru_fwd", grid=(S // tc,),
        in_specs=[row_spec(0), halo, row_spec(1), full((CONV_W, D)), full((1, D)),
                  full((N_GROUPS, LANES, 2 * LANES)), full((1, D)), full((1, D)), full((1, D))],
        out_specs=[pl.BlockSpec((tc, D), lambda i: (i, 0)), pl.BlockSpec((tc, D), lambda i: (i, 0))],
        out_shape=[jax.ShapeDtypeStruct((S, D), F32), jax.ShapeDtypeStruct((S, D), BF16)],
        scratch_shapes=[pltpu.VMEM((tc + SUBLANES, D), F32), pltpu.VMEM((tc, D), F32),
                        pltpu.VMEM((tc, D), F32), pltpu.VMEM((SUBLANES, D), F32)],
        compiler_params=_cparams("arbitrary"),
    )(xg, xg, xg, cw, cb, bd, ba, bx, lam)


def _lru_bwd(xg, h, dyain, cw, cb, bd, ba, bx, lam):
    S = xg.shape[0]
    D = D_MODEL
    tc = min(LRU_CHUNK, S)
    hb = tc // SUBLANES
    nc = S // tc

    def body(xl_ref, xhalo_ref, g_ref, h_ref, hhalo_ref, dy_ref, cw_ref, cb_ref, bd_ref, ba_ref, bx_ref,
             lam_ref, dxg_ref, dcw_ref, dcb_ref, dba_ref, dbx_ref, dlam_ref, dbd_ref,
             xpad, hpad, a_s, b_s, dh_s, g_s, xa_s, r_s, ig_s, m_s, dxa_pad, carry_e, dxa_head):
        i = pl.program_id(0)
        c = nc - 1 - i

        @pl.when(i == 0)
        def _():
            carry_e[...] = jnp.zeros_like(carry_e)
            dxa_head[...] = jnp.zeros_like(dxa_head)
            for ref in (dcw_ref, dcb_ref, dba_ref, dbx_ref, dlam_ref, dbd_ref):
                ref[...] = jnp.zeros_like(ref)

        xpad[0:SUBLANES, :] = jnp.where(c > 0, xhalo_ref[...], 0.0)
        xpad[SUBLANES:, :] = xl_ref[...]
        hpad[0:SUBLANES, :] = jnp.where(c > 0, hhalo_ref[...], 0.0)
        hpad[SUBLANES:, :] = h_ref[...]

        for j in range(N_GROUPS):
            sl = slice(LANES * j, LANES * (j + 1))
            xa = _conv_rows(xpad, cw_ref, cb_ref, sl, tc)
            sp = _softplus(-lam_ref[:, sl])
            r, ig, a, mult = _lru_gates(xa, bd_ref[j], ba_ref[:, sl], bx_ref[:, sl], sp)
            gl, dgl = _gelu_and_grad(g_ref[:, sl])
            dy = dy_ref[:, sl]
            dh = dy * gl
            dxg_ref[:, D + LANES * j:D + LANES * (j + 1)] = (dy * h_ref[:, sl] * dgl).astype(BF16)
            a_s[:, sl] = a
            b_s[:, sl] = a * dh
            dh_s[:, sl] = dh
            xa_s[:, sl] = xa
            r_s[:, sl] = r
            ig_s[:, sl] = ig
            m_s[:, sl] = mult

        row = lax.broadcasted_iota(jnp.int32, (SUBLANES, D), 0)

        def step(tt, ce):
            o = pl.multiple_of((hb - 1 - tt) * SUBLANES, SUBLANES)
            A = a_s[pl.ds(o, SUBLANES), :]
            B = b_s[pl.ds(o, SUBLANES), :]
            for d in (1, 2, 4):
                keep = row < SUBLANES - d
                a_sh = jnp.where(keep, pltpu.roll(A, SUBLANES - d, 0), 1.0)
                b_sh = jnp.where(keep, pltpu.roll(B, SUBLANES - d, 0), 0.0)
                B = A * b_sh + B
                A = A * a_sh
            e = A * ce + B
            e_next = jnp.where(row < SUBLANES - 1, pltpu.roll(e, SUBLANES - 1, 0), ce)
            g_s[pl.ds(o, SUBLANES), :] = dh_s[pl.ds(o, SUBLANES), :] + e_next
            return jnp.broadcast_to(e[0:1, :], (SUBLANES, D))

        carry_e[...] = lax.fori_loop(0, hb, step, carry_e[...])

        for j in range(N_GROUPS):
            sl = slice(LANES * j, LANES * (j + 1))
            gg = g_s[:, sl]
            xa, r, ig, mult, a = xa_s[:, sl], r_s[:, sl], ig_s[:, sl], m_s[:, sl], a_s[:, sl]
            hprev = hpad[pl.ds(SUBLANES - 1, tc), sl]
            sp = _softplus(-lam_ref[:, sl])
            da = gg * hprev
            dmult = gg * (ig * xa)
            dig = gg * (mult * xa)
            dxa = gg * (mult * ig)
            dla = da * a - dmult * ((a * a) / mult)
            dr = dla * ((-LRU_C) * sp)
            dlam_ref[:, sl] += jnp.sum(dla * r, axis=0, keepdims=True)
            dza = dr * r * (1.0 - r)
            dzx = dig * ig * (1.0 - ig)
            dba_ref[:, sl] += jnp.sum(dza, axis=0, keepdims=True)
            dbx_ref[:, sl] += jnp.sum(dzx, axis=0, keepdims=True)
            dz = jnp.concatenate([dza, dzx], axis=1).astype(BF16)
            dbd_ref[j] += lax.dot_general(xa.astype(BF16), dz, TN_DIMS, preferred_element_type=F32)
            dxa = dxa + lax.dot_general(dz, bd_ref[j], NT_DIMS, preferred_element_type=F32)
            dxa_pad[0:tc, sl] = dxa

        dxa_pad[tc:, :] = dxa_head[...]
        dxa_head[...] = dxa_pad[0:SUBLANES, :]

        for j in range(N_GROUPS):
            sl = slice(LANES * j, LANES * (j + 1))
            dxa = dxa_pad[0:tc, sl]
            dxl = jnp.zeros((tc, LANES), F32)
            for k in range(CONV_W):
                dxl = dxl + dxa_pad[pl.ds(CONV_W - 1 - k, tc), sl] * cw_ref[k:k + 1, sl]
                dcw_ref[k:k + 1, sl] += jnp.sum(
                    dxa * xpad[pl.ds(SUBLANES - (CONV_W - 1) + k, tc), sl], axis=0, keepdims=True)
            dxg_ref[:, sl] = dxl.astype(BF16)
            dcb_ref[:, sl] += jnp.sum(dxa, axis=0, keepdims=True)

        @pl.when(i == nc - 1)
        def _():
            dlam_ref[...] = dlam_ref[...] * (LRU_C * _sigmoid(-lam_ref[...]))

    rev = lambda col: pl.BlockSpec((tc, D), lambda i, col=col: (nc - 1 - i, col))
    halo = pl.BlockSpec((SUBLANES, D), lambda i: (jnp.maximum((nc - 1 - i) * hb - 1, 0), 0))
    full = lambda shape: pl.BlockSpec(shape, lambda i: tuple(0 for _ in shape))
    big = lambda: pltpu.VMEM((tc, D), F32)
    return pl.pallas_call(
        body, name="lru_bwd", grid=(nc,),
        in_specs=[rev(0), halo, rev(1), rev(0), halo, rev(0), full((CONV_W, D)), full((1, D)),
                  full((N_GROUPS, LANES, 2 * LANES)), full((1, D)), full((1, D)), full((1, D))],
        out_specs=[pl.BlockSpec((tc, 2 * D), lambda i: (nc - 1 - i, 0)), full((CONV_W, D)), full((1, D)),
                   full((1, D)), full((1, D)), full((1, D)), full((N_GROUPS, LANES, 2 * LANES))],
        out_shape=[jax.ShapeDtypeStruct((S, 2 * D), BF16), jax.ShapeDtypeStruct((CONV_W, D), F32),
                   jax.ShapeDtypeStruct((1, D), F32), jax.ShapeDtypeStruct((1, D), F32),
                   jax.ShapeDtypeStruct((1, D), F32), jax.ShapeDtypeStruct((1, D), F32),
                   jax.ShapeDtypeStruct((N_GROUPS, LANES, 2 * LANES), F32)],
        scratch_shapes=[pltpu.VMEM((tc + SUBLANES, D), F32), pltpu.VMEM((tc + SUBLANES, D), F32),
                        big(), big(), big(), big(), big(), big(), big(), big(),
                        pltpu.VMEM((tc + SUBLANES, D), F32), pltpu.VMEM((SUBLANES, D), F32),
                        pltpu.VMEM((SUBLANES, D), F32)],
        compiler_params=_cparams("arbitrary"),
    )(xg, xg, xg, h, h, dyain, cw, cb, bd, ba, bx, lam)


def _forget_cumsum(fl, fb):
    S = fl.shape[0]
    tr = min(ROW_TILE, S)
    hb = tr // SUBLANES

    def body(fl_ref, fb_ref, o_ref, rep_ref, lf_s, carry):
        @pl.when(pl.program_id(0) == 0)
        def _():
            carry[...] = jnp.zeros_like(carry)

        lf_s[...] = -_softplus(-(fl_ref[...] + fb_ref[...]))
        row = lax.broadcasted_iota(jnp.int32, (SUBLANES, LANES), 0)

        def step(t, c):
            o = pl.multiple_of(t * SUBLANES, SUBLANES)
            B = lf_s[pl.ds(o, SUBLANES), :]
            for d in (1, 2, 4):
                B = B + jnp.where(row >= d, pltpu.roll(B, d, 0), 0.0)
            B = B + c
            o_ref[pl.ds(o, SUBLANES), :] = B * LOG2E
            return jnp.broadcast_to(B[SUBLANES - 1:SUBLANES, :], (SUBLANES, LANES))

        carry[...] = lax.fori_loop(0, hb, step, carry[...])
        for h in range(N_HEADS):
            rep_ref[h] = jnp.broadcast_to(o_ref[:, h:h + 1], (tr, LANES))

    return pl.pallas_call(
        body, name="forget_cumsum", grid=(S // tr,),
        in_specs=[pl.BlockSpec((tr, LANES), lambda i: (i, 0)), pl.BlockSpec((1, LANES), lambda i: (0, 0))],
        out_specs=[pl.BlockSpec((tr, LANES), lambda i: (i, 0)),
                   pl.BlockSpec((N_HEADS, tr, LANES), lambda i: (0, i, 0))],
        out_shape=[jax.ShapeDtypeStruct((S, LANES), F32), jax.ShapeDtypeStruct((N_HEADS, S, LANES), F32)],
        scratch_shapes=[pltpu.VMEM((tr, LANES), F32), pltpu.VMEM((SUBLANES, LANES), F32)],
        compiler_params=_cparams("arbitrary"),
    )(fl, fb)


def _forget_bwd(dF, fl, fb):
    S = fl.shape[0]
    tr = min(ROW_TILE, S)
    hb = tr // SUBLANES
    nc = S // tr

    def body(df_ref, fl_ref, fb_ref, o_ref, dfb_ref, carry):
        @pl.when(pl.program_id(0) == 0)
        def _():
            carry[...] = jnp.zeros_like(carry)
            dfb_ref[...] = jnp.zeros_like(dfb_ref)

        row = lax.broadcasted_iota(jnp.int32, (SUBLANES, LANES), 0)

        def step(tt, carried):
            c, acc = carried
            o = pl.multiple_of((hb - 1 - tt) * SUBLANES, SUBLANES)
            B = df_ref[pl.ds(o, SUBLANES), :]
            for d in (1, 2, 4):
                B = B + jnp.where(row < SUBLANES - d, pltpu.roll(B, SUBLANES - d, 0), 0.0)
            B = B + c
            z = fl_ref[pl.ds(o, SUBLANES), :] + fb_ref[...]
            dz = B * _sigmoid(-z)
            o_ref[pl.ds(o, SUBLANES), :] = dz.astype(BF16)
            return jnp.broadcast_to(B[0:1, :], (SUBLANES, LANES)), acc + dz

        c, acc = lax.fori_loop(0, hb, step, (carry[...], jnp.zeros((SUBLANES, LANES), F32)))
        carry[...] = c
        dfb_ref[...] += jnp.sum(acc, axis=0, keepdims=True)

    rev = pl.BlockSpec((tr, LANES), lambda i: (nc - 1 - i, 0))
    vec = pl.BlockSpec((1, LANES), lambda i: (0, 0))
    return pl.pallas_call(
        body, name="forget_bwd", grid=(nc,),
        in_specs=[rev, rev, vec],
        out_specs=[rev, vec],
        out_shape=[jax.ShapeDtypeStruct((S, LANES), BF16), jax.ShapeDtypeStruct((1, LANES), F32)],
        scratch_shapes=[pltpu.VMEM((SUBLANES, LANES), F32)],
        compiler_params=_cparams("arbitrary"),
    )(dF, fl, fb)


def _triangle(n, qw, key_major):
    pairs = [(q, k) for q in range(n) for k in range(qw * (q + 1))]
    if key_major:
        pairs.sort(key=lambda qk: (qk[1], qk[0]))
    return (jnp.asarray([q for q, _ in pairs], jnp.int32), jnp.asarray([k for _, k in pairs], jnp.int32))


def _strip_plan(bk, bq, strip, rel):
    plan = []
    for j in range(bq // strip):
        if rel is None:
            plan.append((j, bk, None))
            continue
        reach = strip * (j + 1) - rel * bk
        if reach > 0:
            plan.append((j, min(reach, bk), strip * j - rel * bk if reach <= bk else None))
    return plan


def _strip_scores(k_ref, qt_ref, fk_ref, strip, j, nkeys, mask_off):
    cols = slice(strip * j, strip * (j + 1))
    s = jnp.dot(k_ref[0:nkeys, :], qt_ref[:, cols], preferred_element_type=F32) * (ATTN_SCALE * LOG2E)
    fk = fk_ref[0:nkeys, :]
    s = s - jnp.concatenate([fk] * (strip // LANES), axis=1)
    keep = None
    if mask_off is not None:
        keys = lax.broadcasted_iota(jnp.int32, (nkeys, strip), 0)
        queries = lax.broadcasted_iota(jnp.int32, (nkeys, strip), 1) + mask_off
        keep = keys <= queries
    return s, keep


def _attn_fwd(kv, qkv_t, f_row, f_rep):
    S = kv.shape[0]
    bk = min(ATTN_BLOCK, S)
    strip = min(ATTN_STRIP, bk)
    qw = min(ATTN_Q_TILES, S // bk)
    bq = qw * bk
    tri_q, tri_k = _triangle(S // bq, qw, key_major=False)
    ones_rows = 2 * SUBLANES

    def body(tq_ref, tk_ref, k_ref, qt_ref, vt_ref, fq_ref, fk_ref, ot_ref, lse_ref, m_s, acc_s, vta_s):
        t = pl.program_id(1)
        qi, ki = tq_ref[t], tk_ref[t]
        rel = ki - qw * qi

        @pl.when(ki == 0)
        def _():
            m_s[...] = jnp.full_like(m_s, NEG_BIG)
            acc_s[...] = jnp.zeros_like(acc_s)

        vta_s[0:HEAD_DIM, :] = vt_ref[...]
        vta_s[HEAD_DIM:, :] = jnp.ones((ones_rows, bk), BF16)

        def update(plan):
            scores = lambda entry: _strip_scores(k_ref, qt_ref, fk_ref, strip, *entry)

            def weighted_values(j, nkeys, alpha, pb):
                cols = slice(strip * j, strip * (j + 1))
                acc_s[:, cols] = alpha * acc_s[:, cols] + jnp.dot(
                    vta_s[:, 0:nkeys], pb, preferred_element_type=F32)

            ahead, behind = scores(plan[0]), None
            for i, (j, nkeys, mask_off) in enumerate(plan):
                cols = slice(strip * j, strip * (j + 1))
                (s, keep), ahead = ahead, (scores(plan[i + 1]) if i + 1 < len(plan) else None)
                if behind is not None:
                    weighted_values(*behind)
                if keep is not None:
                    s = jnp.where(keep, s, NEG_BIG)
                fq = fq_ref[:, cols]
                m_old = m_s[:, cols]
                m_new = jnp.maximum(m_old, jnp.max(s, axis=0, keepdims=True) + fq)
                p = jnp.exp2(s - (m_new - fq))
                behind = (j, nkeys, jnp.exp2(m_old - m_new), p.astype(BF16))
                m_s[:, cols] = m_new
            weighted_values(*behind)

        @pl.when(rel < 0)
        def _():
            update(_strip_plan(bk, bq, strip, None))

        for d in range(qw):
            @pl.when(rel == d)
            def _(d=d):
                update(_strip_plan(bk, bq, strip, d))
                if d == qw - 1:
                    denom = acc_s[HEAD_DIM:HEAD_DIM + 1, :]
                    ot_ref[...] = (acc_s[0:HEAD_DIM, :] / denom).astype(BF16)
                    lse_ref[...] = m_s[...] + jnp.log2(denom)

    return pl.pallas_call(
        body, name="attn_fwd",
        grid_spec=pltpu.PrefetchScalarGridSpec(
            num_scalar_prefetch=2, grid=(N_HEADS, tri_q.shape[0]),
            in_specs=[pl.BlockSpec((bk, HEAD_DIM), lambda h, t, tq, tk: (tk[t], h)),
                      pl.BlockSpec((HEAD_DIM, bq), lambda h, t, tq, tk: (h, tq[t])),
                      pl.BlockSpec((HEAD_DIM, bk), lambda h, t, tq, tk: (2 * N_HEADS + h, tk[t])),
                      pl.BlockSpec((None, 1, bq), lambda h, t, tq, tk: (h, 0, tq[t])),
                      pl.BlockSpec((None, bk, LANES), lambda h, t, tq, tk: (h, tk[t], 0))],
            out_specs=[pl.BlockSpec((HEAD_DIM, bq), lambda h, t, tq, tk: (h, tq[t])),
                       pl.BlockSpec((None, 1, bq), lambda h, t, tq, tk: (h, 0, tq[t]))],
            scratch_shapes=[pltpu.VMEM((1, bq), F32), pltpu.VMEM((HEAD_DIM + ones_rows, bq), F32),
                            pltpu.VMEM((HEAD_DIM + ones_rows, bk), BF16)]),
        out_shape=[jax.ShapeDtypeStruct((N_HEADS * HEAD_DIM, S), BF16), jax.ShapeDtypeStruct((N_HEADS, 1, S), F32)],
        compiler_params=_cparams("parallel", "arbitrary"),
    )(tri_q, tri_k, kv, qkv_t, qkv_t, f_row, f_rep)


def _attn_bwd(kv, qkv_t, do_t, o_t, lse, f_row, f_rep):
    S = kv.shape[0]
    bk = min(ATTN_BLOCK, S)
    strip = min(ATTN_STRIP, bk)
    qw = min(ATTN_Q_TILES, S // bk)
    bq = qw * bk
    nq = S // bq
    tri_q, tri_k = _triangle(nq, qw, key_major=True)
    n_tiles = tri_q.shape[0]

    def body(tq_ref, tk_ref, k_ref, v_ref, qt_ref, kt_ref, dot_ref, ot_ref, lse_ref, fq_ref, fk_ref,
             dqt_ref, dkt_ref, dvt_ref, dfk_ref, dfq_ref, dq_s, dk_s, dv_s, dfk_s, dfq_s, row_s):
        t = pl.program_id(1)
        qi, ki = tq_ref[t], tk_ref[t]
        rel = ki - qw * qi

        @pl.when(t == 0)
        def _():
            dq_s[...] = jnp.zeros_like(dq_s)
            dfq_s[...] = jnp.zeros_like(dfq_s)

        @pl.when(rel >= 0)
        def _():
            dk_s[...] = jnp.zeros_like(dk_s)
            dv_s[...] = jnp.zeros_like(dv_s)
            dfk_s[...] = jnp.zeros_like(dfk_s)

        def update(plan):
            row_s[...] = fq_ref[...] - lse_ref[...]

            def matmuls_in(j, nkeys, mask_off):
                s, keep = _strip_scores(k_ref, qt_ref, fk_ref, strip, j, nkeys, mask_off)
                dp = jnp.dot(v_ref[0:nkeys, :], dot_ref[:, strip * j:strip * (j + 1)], preferred_element_type=F32)
                return s, keep, dp

            def matmuls_out(j, nkeys, pb, dsb):
                cols = slice(strip * j, strip * (j + 1))
                dv_s[:, 0:nkeys] += lax.dot_general(dot_ref[:, cols], pb, NT_DIMS, preferred_element_type=F32)
                dk_s[:, 0:nkeys] += lax.dot_general(qt_ref[:, cols], dsb, NT_DIMS, preferred_element_type=F32)
                dq_s[qi, :, cols] += jnp.dot(kt_ref[:, 0:nkeys], dsb, preferred_element_type=F32)

            ahead, behind = matmuls_in(*plan[0]), None
            for i, (j, nkeys, mask_off) in enumerate(plan):
                cols = slice(strip * j, strip * (j + 1))
                (s, keep, dp), ahead = ahead, (matmuls_in(*plan[i + 1]) if i + 1 < len(plan) else None)
                if behind is not None:
                    matmuls_out(*behind)
                p = jnp.exp2(s + row_s[:, cols])
                if keep is not None:
                    p = jnp.where(keep, p, 0.0)
                dot = dot_ref[:, cols]
                delta = jnp.sum(dot.astype(F32) * ot_ref[:, cols].astype(F32), axis=0, keepdims=True)
                ds = p * (dp - delta)
                behind = (j, nkeys, p.astype(BF16), ds.astype(BF16))
                lane_part = ds[:, 0:LANES]
                for g in range(1, strip // LANES):
                    lane_part = lane_part + ds[:, LANES * g:LANES * (g + 1)]
                dfk_s[0:nkeys, :] += lane_part
                sub_part = ds[0:SUBLANES, :]
                for g in range(1, nkeys // SUBLANES):
                    sub_part = sub_part + ds[SUBLANES * g:SUBLANES * (g + 1), :]
                dfq_s[qi, :, cols] += sub_part
            matmuls_out(*behind)

        @pl.when(rel < 0)
        def _():
            update(_strip_plan(bk, bq, strip, None))

        for d in range(qw):
            @pl.when(rel == d)
            def _(d=d):
                update(_strip_plan(bk, bq, strip, d))

        @pl.when(qi == nq - 1)
        def _():
            dkt_ref[...] = (dk_s[...] * ATTN_SCALE).astype(BF16)
            dvt_ref[...] = dv_s[...].astype(BF16)
            dfk_ref[...] = -jnp.sum(dfk_s[...].T, axis=0, keepdims=True)

        @pl.when(t == n_tiles - 1)
        def _():
            for j in range(nq):
                dqt_ref[:, bq * j:bq * (j + 1)] = (dq_s[j] * ATTN_SCALE).astype(BF16)
                dfq_ref[:, bq * j:bq * (j + 1)] = jnp.sum(dfq_s[j], axis=0, keepdims=True)

    q_feat = pl.BlockSpec((HEAD_DIM, bq), lambda h, t, tq, tk: (h, tq[t]))
    q_row = pl.BlockSpec((None, 1, bq), lambda h, t, tq, tk: (h, 0, tq[t]))
    k_feat = pl.BlockSpec((HEAD_DIM, bk), lambda h, t, tq, tk: (h, tk[t]))
    return pl.pallas_call(
        body, name="attn_bwd",
        grid_spec=pltpu.PrefetchScalarGridSpec(
            num_scalar_prefetch=2, grid=(N_HEADS, n_tiles),
            in_specs=[pl.BlockSpec((bk, HEAD_DIM), lambda h, t, tq, tk: (tk[t], h)),
                      pl.BlockSpec((bk, HEAD_DIM), lambda h, t, tq, tk: (tk[t], N_HEADS + h)),
                      q_feat,
                      pl.BlockSpec((HEAD_DIM, bk), lambda h, t, tq, tk: (N_HEADS + h, tk[t])),
                      q_feat, q_feat, q_row, q_row,
                      pl.BlockSpec((None, bk, LANES), lambda h, t, tq, tk: (h, tk[t], 0))],
            out_specs=[pl.BlockSpec((HEAD_DIM, S), lambda h, t, tq, tk: (h, 0)), k_feat, k_feat,
                       pl.BlockSpec((None, 1, bk), lambda h, t, tq, tk: (h, 0, tk[t])),
                       pl.BlockSpec((None, 1, S), lambda h, t, tq, tk: (h, 0, 0))],
            scratch_shapes=[pltpu.VMEM((nq, HEAD_DIM, bq), F32), pltpu.VMEM((HEAD_DIM, bk), F32),
                            pltpu.VMEM((HEAD_DIM, bk), F32), pltpu.VMEM((bk, LANES), F32),
                            pltpu.VMEM((nq, SUBLANES, bq), F32), pltpu.VMEM((1, bq), F32)]),
        out_shape=[jax.ShapeDtypeStruct((N_HEADS * HEAD_DIM, S), BF16)] * 3
        + [jax.ShapeDtypeStruct((N_HEADS, 1, S), F32), jax.ShapeDtypeStruct((N_HEADS, 1, S), F32)],
        compiler_params=_cparams("parallel", "arbitrary"),
    )(tri_q, tri_k, kv, kv, qkv_t, qkv_t, do_t, o_t, lse, f_row, f_rep)


def _gate_mix(gates, ya, yb):
    S, D = ya.shape
    tr = min(ROW_TILE, S)

    def body(ga_ref, gb_ref, ya_ref, yb_ref, o_ref):
        o_ref[...] = (_sigmoid(ga_ref[...]) * ya_ref[...] + _sigmoid(gb_ref[...]) * yb_ref[...]).astype(BF16)

    col = lambda j: pl.BlockSpec((tr, D), lambda i, j=j: (i, j))
    return pl.pallas_call(
        body, name="gate_mix", grid=(S // tr,),
        in_specs=[col(0), col(1), col(0), col(0)],
        out_specs=col(0),
        out_shape=jax.ShapeDtypeStruct((S, D), BF16),
        compiler_params=_cparams("parallel"),
    )(gates, gates, ya, yb)


def _gate_bwd(dmix, gates, ya, yb):
    S, D = ya.shape
    tr = min(ROW_TILE, S)

    def body(dm_ref, ga_ref, gb_ref, ya_ref, yb_ref, dya_ref, dyb_ref, dg_ref):
        dm = dm_ref[...]
        sa, sb = _sigmoid(ga_ref[...]), _sigmoid(gb_ref[...])
        dya_ref[...] = (dm * sa).astype(BF16)
        dyb_ref[...] = (dm * sb).astype(BF16)
        dg_ref[:, 0:D] = ((dm * ya_ref[...]) * (sa * (1.0 - sa))).astype(BF16)
        dg_ref[:, D:] = ((dm * yb_ref[...]) * (sb * (1.0 - sb))).astype(BF16)

    col = lambda j: pl.BlockSpec((tr, D), lambda i, j=j: (i, j))
    return pl.pallas_call(
        body, name="gate_bwd", grid=(S // tr,),
        in_specs=[col(0), col(0), col(1), col(0), col(0)],
        out_specs=[col(0), col(0), pl.BlockSpec((tr, 2 * D), lambda i: (i, 0))],
        out_shape=[jax.ShapeDtypeStruct((S, D), BF16), jax.ShapeDtypeStruct((S, D), BF16),
                   jax.ShapeDtypeStruct((S, 2 * D), BF16)],
        compiler_params=_cparams("parallel"),
    )(dmix, gates, gates, ya, yb)


def _mesh_place():
    x, y, c = lax.axis_index("x"), lax.axis_index("y"), lax.axis_index("c")
    chips = [(1 - x, y), (x, 1 - y), (1 - x, 1 - y)]
    return x, y, c, chips


def _all_gather(shards):
    n = len(shards)

    def body(*refs):
        ins, outs = refs[:n], refs[n:2 * n]
        send_sems, recv_sems, local_sems = refs[2 * n:]
        x, y, c, chips = _mesh_place()
        me, sib = (x, y, c), (x, y, 1 - c)

        def copy(a, k, block, to, src=None):
            px, py, pc = block
            dst = outs[a].at[4 * px + 2 * py + pc]
            return pltpu.make_async_remote_copy(
                src_ref=dst if src is None else src, dst_ref=dst,
                send_sem=send_sems.at[a, k], recv_sem=recv_sems.at[a, k],
                device_id=to, device_id_type=MESH_ID)

        mine = [pltpu.make_async_copy(ins[a], outs[a].at[4 * x + 2 * y + c], local_sems.at[a]) for a in range(n)]
        for cp in mine:
            cp.start()
        first = []
        for a in range(n):
            first.append(copy(a, 0, me, sib, src=ins[a]))
            for j, chip in enumerate(chips):
                first.append(copy(a, 1 + j, me, (*chip, c), src=ins[a]))
        for cp in first:
            cp.start()
        passed = []
        for j, chip in enumerate(chips):
            for a in range(n):
                copy(a, 1 + j, (*chip, c), me).wait_recv()
                fwd = copy(a, 4 + j, (*chip, c), sib)
                fwd.start()
                passed.append(fwd)
        for a in range(n):
            copy(a, 0, sib, me).wait_recv()
            for j, chip in enumerate(chips):
                copy(a, 4 + j, (*chip, 1 - c), me).wait_recv()
        for cp in first + passed:
            cp.wait_send()
        for cp in mine:
            cp.wait()

    return pl.pallas_call(
        body, name="all_gather_weights",
        in_specs=[ANY] * n, out_specs=[ANY] * n,
        out_shape=[jax.ShapeDtypeStruct((N_DEV,) + s.shape, s.dtype) for s in shards],
        scratch_shapes=[pltpu.SemaphoreType.DMA((n, 7)), pltpu.SemaphoreType.DMA((n, 7)),
                        pltpu.SemaphoreType.DMA((n,))],
    )(*shards)


def _chip_partial_sum(blocks, got, core):
    R, C = got.shape[1:]
    tr = min(256, R)
    assert R % tr == 0

    def body(core_ref, a_ref, b_ref, s_ref, sb_ref):
        s = a_ref[...] + b_ref[...]
        s_ref[...] = s
        sb_ref[...] = s.astype(BF16)

    blk = pl.BlockSpec((None, tr, C), lambda k, i, core_ref: (k, i, 0))
    return pl.pallas_call(
        body, name="chip_partial_sum",
        grid_spec=pltpu.PrefetchScalarGridSpec(
            num_scalar_prefetch=1, grid=(4, R // tr),
            in_specs=[pl.BlockSpec((None, tr, C), lambda k, i, core_ref: (2 * k + core_ref[0], i, 0)), blk],
            out_specs=[blk, blk]),
        out_shape=[jax.ShapeDtypeStruct(got.shape, F32), jax.ShapeDtypeStruct(got.shape, BF16)],
        compiler_params=_cparams("parallel", "parallel"),
    )(core, blocks, got)


HBM_SPEC = pl.BlockSpec(memory_space=pltpu.HBM)
SEM_SPEC = pl.BlockSpec(memory_space=pltpu.SEMAPHORE)
FLIPS = [(dx, dy, dc) for dx in (0, 1) for dy in (0, 1) for dc in (0, 1) if (dx, dy, dc) != (0, 0, 0)]


def _flip(v, d):
    return 1 - v if d else v


def _gather_copies(srcs, lands, send_sems, recv_sems):
    x, y, c, _ = _mesh_place()
    sends, recvs = [], []
    for a in range(len(srcs)):
        for k, (dx, dy, dc) in enumerate(FLIPS):
            px, py, pc = _flip(x, dx), _flip(y, dy), _flip(c, dc)
            sem = len(FLIPS) * a + k
            common = dict(send_sem=send_sems.at[sem], recv_sem=recv_sems.at[sem],
                          device_id=(px, py, pc), device_id_type=MESH_ID)
            sends.append(pltpu.make_async_remote_copy(
                src_ref=srcs[a], dst_ref=lands[a].at[4 * x + 2 * y + c], **common))
            recvs.append(pltpu.make_async_remote_copy(
                src_ref=srcs[a], dst_ref=lands[a].at[4 * px + 2 * py + pc], **common))
    return sends, recvs


def _cores_copies(srcs, lands, send_sems, recv_sems):
    x, y, c, _ = _mesh_place()
    copies = []
    for a in range(len(srcs)):
        for k in range(4):
            copies.append(pltpu.make_async_remote_copy(
                src_ref=srcs[a].at[2 * k + (1 - c)], dst_ref=lands[a].at[k],
                send_sem=send_sems.at[4 * a + k], recv_sem=recv_sems.at[4 * a + k],
                device_id=(x, y, 1 - c), device_id_type=MESH_ID))
    return copies, copies


def _scatter_copies(srcs, lands, send_sems, recv_sems):
    x, y, c, chips = _mesh_place()
    sends = []
    for a in range(len(srcs)):
        for j, (px, py) in enumerate(chips):
            sends.append(pltpu.make_async_remote_copy(
                src_ref=srcs[a].at[2 * px + py], dst_ref=lands[a].at[j],
                send_sem=send_sems.at[3 * a + j], recv_sem=recv_sems.at[3 * a + j],
                device_id=(px, py, c), device_id_type=MESH_ID))
    return sends, sends


def _exchange_start(srcs, land_shapes, copies, n_copies, name):
    n = len(srcs)

    def body(*refs):
        src_refs, land_refs = refs[:n], refs[n:2 * n]
        send_sems, recv_sems = refs[2 * n], refs[2 * n + 1]
        token = refs[-1]
        sends, _ = copies(src_refs, land_refs, send_sems, recv_sems)
        for cp in sends:
            cp.start()
        token[...] = jnp.zeros_like(token)

    lands = [pltpu.with_memory_space_constraint(lax.empty(s.shape, s.dtype), pltpu.HBM) for s in land_shapes]
    srcs = [pltpu.with_memory_space_constraint(s, pltpu.HBM) for s in srcs]
    res = pl.pallas_call(
        body, name=name,
        out_shape=(pltpu.SemaphoreType.DMA((n * n_copies,)), pltpu.SemaphoreType.DMA((n * n_copies,)),
                   *[pltpu.HBM(s.shape, s.dtype) for s in srcs], *[pltpu.HBM(s.shape, s.dtype) for s in land_shapes],
                   jax.ShapeDtypeStruct((SUBLANES, LANES), F32)),
        in_specs=[HBM_SPEC] * (2 * n),
        out_specs=(SEM_SPEC, SEM_SPEC, *[HBM_SPEC] * (2 * n), pl.BlockSpec(memory_space=pltpu.VMEM)),
        input_output_aliases={i: 2 + i for i in range(2 * n)},
        compiler_params=pltpu.CompilerParams(has_side_effects=pltpu.SideEffectType.DATAFLOW_SIDE_EFFECTING),
    )(*srcs, *lands)
    return res[0], res[1], list(res[2:2 + n]), list(res[2 + n:2 + 2 * n]), res[-1]


def _exchange_wait(started, copies, after, name):
    send_sems, recv_sems, srcs, lands, _ = started
    n = len(srcs)

    def body(*refs):
        src_refs, land_refs = refs[:n], refs[n:2 * n]
        send_ref, recv_ref = refs[2 * n], refs[2 * n + 1]
        sends, recvs = copies(src_refs, land_refs, send_ref, recv_ref)
        for cp in sends:
            cp.wait_send()
        for cp in recvs:
            cp.wait_recv()

    res = pl.pallas_call(
        body, name=name,
        out_shape=tuple(pltpu.HBM(s.shape, s.dtype) for s in srcs + lands),
        in_specs=[HBM_SPEC] * (2 * n) + [SEM_SPEC, SEM_SPEC, ANY],
        out_specs=tuple([HBM_SPEC] * (2 * n)),
        input_output_aliases={i: i for i in range(2 * n)},
        compiler_params=pltpu.CompilerParams(has_side_effects=pltpu.SideEffectType.DATAFLOW_SIDE_EFFECTING),
    )(*srcs, *lands, send_sems, recv_sems, after)
    return list(res[:n]), list(res[n:])


def _all_reduce_small(vec):
    R = vec.shape[0]

    def body(v_ref, o_ref, sib_buf, chip_buf, send_sems, recv_sems):
        x, y, c, chips = _mesh_place()
        swap = pltpu.make_async_remote_copy(
            src_ref=v_ref, dst_ref=sib_buf, send_sem=send_sems.at[0], recv_sem=recv_sems.at[0],
            device_id=(x, y, 1 - c), device_id_type=MESH_ID)
        swap.start()
        swap.wait()
        my_chip = 2 * x + y
        chip_buf[my_chip] = v_ref[...] + sib_buf[...]
        sends = []
        for j, (px, py) in enumerate(chips):
            cp = pltpu.make_async_remote_copy(
                src_ref=chip_buf.at[my_chip], dst_ref=chip_buf.at[my_chip],
                send_sem=send_sems.at[1 + j], recv_sem=recv_sems.at[1 + j],
                device_id=(px, py, c), device_id_type=MESH_ID)
            cp.start()
            sends.append(cp)
        for j, (px, py) in enumerate(chips):
            pltpu.make_async_remote_copy(
                src_ref=chip_buf.at[2 * px + py], dst_ref=chip_buf.at[2 * px + py],
                send_sem=send_sems.at[1 + j], recv_sem=recv_sems.at[1 + j],
                device_id=(px, py, c), device_id_type=MESH_ID).wait_recv()
        for cp in sends:
            cp.wait_send()
        o_ref[...] = ((chip_buf[0] + chip_buf[1]) + chip_buf[2]) + chip_buf[3]

    vm = pl.BlockSpec(memory_space=pltpu.VMEM)
    return pl.pallas_call(
        body, name="all_reduce_small",
        in_specs=[vm], out_specs=vm,
        out_shape=jax.ShapeDtypeStruct(vec.shape, F32),
        scratch_shapes=[pltpu.VMEM((R, LANES), F32), pltpu.VMEM((4, R, LANES), F32),
                        pltpu.SemaphoreType.DMA((4,)), pltpu.SemaphoreType.DMA((4,))],
    )(vec)


def _adamw_math(w, g, m, v):
    m = ADAM_B1 * m + (1.0 - ADAM_B1) * g
    v = ADAM_B2 * v + (1.0 - ADAM_B2) * (g * g)
    m_hat = m / (1.0 - ADAM_B1 ** ADAM_STEP)
    v_hat = v / (1.0 - ADAM_B2 ** ADAM_STEP)
    delta = -ADAM_LR * (m_hat / (jnp.sqrt(v_hat) + ADAM_EPS) + ADAM_WD * w)
    return delta, m, v


def _adamw(w, m, v, g_own, g_got, chip, name):
    R, C = w.shape
    tr = R if R * C <= 256 * D_MODEL else 256
    assert R % tr == 0
    n_got = g_got.shape[0]

    def body(*refs):
        w_ref, m_ref, v_ref, go_ref = refs[1:5]
        got = refs[5:5 + n_got]
        g_ref, d_ref, nm_ref, nv_ref = refs[5 + n_got:]
        g = go_ref[...]
        for r in got:
            g = g + r[...].astype(F32)
        delta, m_new, v_new = _adamw_math(w_ref[...], g, m_ref[...], v_ref[...])
        g_ref[...] = g
        d_ref[...] = delta
        nm_ref[...] = m_new
        nv_ref[...] = v_new

    blk = pl.BlockSpec((tr, C), lambda i, chip_ref: (i, 0))
    own_spec = pl.BlockSpec((None, tr, C), lambda i, chip_ref: (chip_ref[0], i, 0))
    got_specs = [pl.BlockSpec((None, tr, C), lambda i, chip_ref, j=j: (j, i, 0)) for j in range(n_got)]
    return pl.pallas_call(
        body, name=name,
        grid_spec=pltpu.PrefetchScalarGridSpec(
            num_scalar_prefetch=1, grid=(R // tr,),
            in_specs=[blk] * 3 + [own_spec] + got_specs, out_specs=[blk] * 4),
        out_shape=[jax.ShapeDtypeStruct((R, C), F32)] * 4,
        compiler_params=_cparams("parallel"),
    )(chip, w, m, v, g_own, *([g_got] * n_got))


def _block_diag_pairs(wa, wx):
    def pairs(w):
        w = w.reshape(N_GROUPS, 2, LRU_BW, LRU_BW)
        z = jnp.zeros((N_GROUPS, LRU_BW, LRU_BW), w.dtype)
        top = jnp.concatenate([w[:, 0], z], axis=2)
        bot = jnp.concatenate([z, w[:, 1]], axis=2)
        return jnp.concatenate([top, bot], axis=1)
    return jnp.concatenate([pairs(wa), pairs(wx)], axis=2).astype(BF16)


def _block_diag_unpair(dbd):
    def unpair(g):
        blocks = jnp.stack([g[:, :LRU_BW, :LRU_BW], g[:, LRU_BW:, LRU_BW:]], axis=1)
        return blocks.reshape(LRU_BLOCKS, LRU_BW, LRU_BW)
    return unpair(dbd[:, :, :LANES]), unpair(dbd[:, :, LANES:])


def _local_step(x, target, W, small, late_weights=None, hooks=None):
    def hook(name, *args):
        return hooks[name](*args) if hooks is not None else (None, 0.0)

    S, D = x.shape
    g1, g2, g3 = small["norm_mix_g"], small["norm_mlp_g"], small["norm_final_g"]
    cw, cb = small["conv_w"], small["conv_b"].reshape(1, D)
    ba, bx, lam = (small[k].reshape(1, D) for k in ("lru_ba", "lru_bx", "lru_lambda"))
    fb = jnp.pad(small["forget_b"], (0, LANES - N_HEADS)).reshape(1, LANES)
    bd = _block_diag_pairs(small["lru_wa"], small["lru_wx"])
    big = dict(tm=1024, tn=1024)

    u = _norm_fwd(x, g1, "norm_mix")
    (xg,) = _mm([(u, W["in_xg"])], tks=[D], outs=[F32], name="proj_xg", **big)
    (qkv_t,) = _mm([(W["in_qkv_t"], u)], tb=True, tks=[D], outs=[BF16], name="proj_qkv_t", **big)
    (kv,) = _mm([(u, W["in_kv"])], tks=[D], outs=[BF16], name="proj_kv", **big)
    (gates,) = _mm([(u, W["in_gates"])], tks=[D], outs=[F32], name="proj_gates", **big)
    (fl,) = _mm([(u, W["in_f"])], tks=[D], outs=[F32], name="proj_forget", **big)
    h, yain = _lru_fwd(xg, cw, cb, bd, ba, bx, lam)
    fcum, f_rep = _forget_cumsum(fl, fb)
    f_row = fcum[:, :N_HEADS].T.reshape(N_HEADS, 1, S)
    ob_t, lse = _attn_fwd(kv, qkv_t, f_row, f_rep)
    if late_weights is not None:
        W = {**W, **late_weights(lse)}
    (ya,) = _mm([(yain, W["branch_a"])], tks=[D], outs=[F32], name="branch_a", **big)
    (yb,) = _mm([(ob_t, W["branch_b"])], ta=True, tks=[D], outs=[F32], name="branch_b", **big)
    mix = _gate_mix(gates, ya, yb)
    (x1,) = _mm([(mix, W["out"])], tks=[D], outs=[F32], name="out_proj", extra=(x,),
                epi=lambda acc, res: (res + acc,), **big)
    m = _norm_fwd(x1, g2, "norm_mlp")
    relu, hh = _mm([(m, W["up"])], tks=[D], outs=[BF16, BF16], name="mlp_up",
                   epi=lambda acc: (jnp.maximum(acc, 0.0), jnp.square(jnp.maximum(acc, 0.0))), **big)
    deep = dict(tm=512, tn=1024, tks=[D_FF])
    wgrad = dict(tm=1024, tn=512, tks=[min(4096, S)])
    (x2,) = _mm([(hh, W["down"])], outs=[F32], name="mlp_down", extra=(x1,),
                epi=lambda acc, res: (res + acc,), **deep)
    loss_acc, dg3, dx2, dx2b = _final_norm_loss(x2, target, g3)

    (dhpre,) = _mm([(dx2b, W["down"])], tb=True, tks=[D], outs=[BF16], name="d_mlp_act", extra=(relu,),
                   epi=lambda acc, r: (acc * (2.0 * r.astype(F32)),), **big)
    (dw_down,) = _mm([(hh, dx2b)], ta=True, outs=[F32], name="dw_down", **wgrad)
    (dm,) = _mm([(dhpre, W["up"])], tb=True, outs=[F32], name="d_mlp_in", **deep)
    assert wgrad["tn"] == D_FF // N_DEV
    (dw_up,) = _mm([(m, dhpre)], ta=True, outs=[F32], name="dw_up", col_blocked=True, **wgrad)
    dx1, dx1b, dg2 = _norm_bwd(dm, x1, g2, dx2, "norm_mlp_bwd")
    (dmix,) = _mm([(dx1b, W["out"])], tb=True, tks=[D], outs=[F32], name="d_mix", **big)
    (dw_out,) = _mm([(mix, dx1b)], ta=True, outs=[F32], name="dw_out", **wgrad)
    dya, dyb, dgates = _gate_bwd(dmix, gates, ya, yb)
    (dob_t,) = _mm([(W["branch_b"], dyb)], tb=True, tks=[D], outs=[BF16], name="d_attn_out_t", **big)
    (dw_b,) = _mm([(ob_t, dyb)], outs=[F32], name="dw_branch_b", **wgrad)
    (dyain,) = _mm([(dya, W["branch_a"])], tb=True, tks=[D], outs=[F32], name="d_lru_out", **big)
    (dw_a,) = _mm([(yain, dya)], ta=True, outs=[F32], name="dw_branch_a", **wgrad)
    early = dict(w_branch_a=dw_a, w_branch_b=dw_b, w_out=dw_out, w_up=dw_up, w_down=dw_down)
    early_state, zero = hook("early_start", early)
    dq_t, dk_t, dv_t, dfk, dfq = _attn_bwd(kv, qkv_t, dob_t, ob_t, lse + zero, f_row, f_rep)
    early_state, zero = hook("early_mid", early_state, dfq)
    dF = jnp.pad((dfk.reshape(N_HEADS, S) + dfq.reshape(N_HEADS, S)).T, ((0, 0), (0, LANES - N_HEADS)))
    dfl, dfb = _forget_bwd(dF, fl, fb)
    dxg, dcw, dcb, dba, dbx, dlam, dbd = _lru_bwd(xg, h, dyain, cw, cb, bd, ba, bx, lam + zero)
    dw_in_parts = [
        _mm([(u, dxg)], ta=True, outs=[F32], name="dw_in_xg", **wgrad)[0],
        _mm([(dq_t, u)], outs=[F32], name="dw_in_q_t", **wgrad)[0].T,
        _mm([(dk_t, u)], outs=[F32], name="dw_in_k_t", **wgrad)[0].T,
        _mm([(dv_t, u)], outs=[F32], name="dw_in_v_t", **wgrad)[0].T,
        _mm([(u, dgates)], ta=True, outs=[F32], name="dw_in_gates", **wgrad)[0],
        _mm([(u, dfl)], ta=True, outs=[F32], name="dw_in_forget", **wgrad)[0][:, :N_HEADS],
    ]
    dw_in = jnp.concatenate(dw_in_parts, axis=1)
    in_state, zero = hook("in_start", dw_in)
    wq_t, wk_t, wv_t = (W["in_qkv_t"][D * i:D * (i + 1)] for i in range(3))
    (du_tok,) = _mm([(dxg, W["in_xg"]), (dgates, W["in_gates"]), (dfl, W["in_f"] + jnp.asarray(zero, BF16))],
                    tb=True, tks=[2 * D, 2 * D, LANES], outs=[F32], name="d_norm_mix_out_tok", tm=1024, tn=512)
    in_state, zero = hook("in_mid", in_state, du_tok)
    (du,) = _mm([(dq_t, wq_t + jnp.asarray(zero, BF16)), (dk_t, wk_t), (dv_t, wv_t)], ta=True, tks=[D, D, D],
                outs=[F32], name="d_norm_mix_out", extra=(du_tok,), epi=lambda acc, prev: (prev + acc,),
                tm=1024, tn=512)
    grad_x, _, dg1 = _norm_bwd(du, x, g1, dx1, "norm_mix_bwd")

    dwa, dwx = _block_diag_unpair(dbd)
    big_grads = dict(early, w_in=dw_in)
    small_grads = dict(norm_mix_g=dg1.reshape(D), conv_w=dcw, conv_b=dcb.reshape(D), lru_wa=dwa, lru_ba=dba.reshape(D),
                       lru_wx=dwx, lru_bx=dbx.reshape(D), lru_lambda=dlam.reshape(D), forget_b=dfb[0, :N_HEADS],
                       norm_mlp_g=dg2.reshape(D), norm_final_g=dg3.reshape(D))
    return loss_acc[0, 0], grad_x, big_grads, small_grads, (early_state, in_state)


SMALL_NAMES = ("norm_mix_g", "conv_b", "lru_wa", "lru_ba", "lru_wx", "lru_bx", "lru_lambda", "forget_b",
               "norm_mlp_g", "norm_final_g")
TILE_ELEMS = SUBLANES * LANES


def _pack_small(parts):
    rows = []
    for p in parts:
        flat = p.reshape(-1)
        flat = jnp.pad(flat, (0, (-flat.shape[0]) % TILE_ELEMS))
        rows.append(flat.reshape(-1, LANES))
    return jnp.concatenate(rows, axis=0)


def _packed_rows(shape):
    return -(-math.prod(shape) // TILE_ELEMS) * SUBLANES


def _adamw_small(g_packed, g_conv_w, weights, moms, vels):
    def rows_view(a):
        flat = a.reshape(-1)
        flat = jnp.pad(flat, (0, (-flat.shape[0]) % LANES))
        return flat.reshape(-1, LANES)

    names = SMALL_NAMES + ("conv_w",)
    views = [[rows_view(src[k]) for k in names] for src in (weights, moms, vels)]
    n = len(names)
    starts, r = [], 0
    for k in SMALL_NAMES:
        starts.append(r)
        r += _packed_rows(weights[k].shape)

    def body(*refs):
        gp_ref, gc_ref = refs[0], refs[1]
        w_refs, m_refs, v_refs = refs[2:2 + n], refs[2 + n:2 + 2 * n], refs[2 + 2 * n:2 + 3 * n]
        outs = refs[2 + 3 * n:]
        for i in range(n):
            rows = w_refs[i].shape[0]
            g = gc_ref[...] if i == n - 1 else gp_ref[starts[i]:starts[i] + rows, :]
            delta, m_new, v_new = _adamw_math(w_refs[i][...], g, m_refs[i][...], v_refs[i][...])
            for o_ref, val in zip(outs[4 * i:4 * i + 4], (g, delta, m_new, v_new)):
                o_ref[...] = val

    vm = pl.BlockSpec(memory_space=pltpu.VMEM)
    out_shape = [jax.ShapeDtypeStruct(v.shape, F32) for v in views[0] for _ in range(4)]
    res = pl.pallas_call(
        body, name="adamw_small",
        in_specs=[vm] * (2 + 3 * n), out_specs=[vm] * (4 * n), out_shape=out_shape,
    )(g_packed, g_conv_w, *views[0], *views[1], *views[2])
    dicts = ({}, {}, {}, {})
    for i, k in enumerate(names):
        size = math.prod(weights[k].shape)
        for d, arr in zip(dicts, res[4 * i:4 * i + 4]):
            d[k] = arr.reshape(-1)[:size].reshape(weights[k].shape)
    return dicts


BIG_NAMES = ("w_in", "w_branch_a", "w_branch_b", "w_out", "w_up", "w_down")
WEIGHT_ORDER = ("norm_mix_g", "w_in", "conv_w", "conv_b", "lru_wa", "lru_ba", "lru_wx", "lru_bx", "lru_lambda",
                "forget_b", "w_branch_a", "w_branch_b", "w_out", "norm_mlp_g", "w_up", "w_down", "norm_final_g")


def _to_dest_blocks(name, g):
    if g.ndim == 3:
        return g
    if name in ("w_in", "w_up"):
        return g.reshape(g.shape[0], N_DEV, g.shape[1] // N_DEV).transpose(1, 0, 2)
    return g.reshape(N_DEV, g.shape[0] // N_DEV, g.shape[1])


def kernel(x, norm_mix_g, w_in, conv_w, conv_b, lru_wa, lru_ba, lru_wx, lru_bx, lru_lambda, forget_b, w_branch_a, w_branch_b, w_out, norm_mlp_g, w_up, w_down, norm_final_g, loss_target, m_norm_mix_g, m_w_in, m_conv_w, m_conv_b, m_lru_wa, m_lru_ba, m_lru_wx, m_lru_bx, m_lru_lambda, m_forget_b, m_w_branch_a, m_w_branch_b, m_w_out, m_norm_mlp_g, m_w_up, m_w_down, m_norm_final_g, v_norm_mix_g, v_w_in, v_conv_w, v_conv_b, v_lru_wa, v_lru_ba, v_lru_wx, v_lru_bx, v_lru_lambda, v_forget_b, v_w_branch_a, v_w_branch_b, v_w_out, v_norm_mlp_g, v_w_up, v_w_down, v_norm_final_g):
    weights = dict(norm_mix_g=norm_mix_g, w_in=w_in, conv_w=conv_w, conv_b=conv_b, lru_wa=lru_wa, lru_ba=lru_ba,
                   lru_wx=lru_wx, lru_bx=lru_bx, lru_lambda=lru_lambda, forget_b=forget_b, w_branch_a=w_branch_a,
                   w_branch_b=w_branch_b, w_out=w_out, norm_mlp_g=norm_mlp_g, w_up=w_up, w_down=w_down,
                   norm_final_g=norm_final_g)
    moms = dict(norm_mix_g=m_norm_mix_g, w_in=m_w_in, conv_w=m_conv_w, conv_b=m_conv_b, lru_wa=m_lru_wa,
                lru_ba=m_lru_ba, lru_wx=m_lru_wx, lru_bx=m_lru_bx, lru_lambda=m_lru_lambda, forget_b=m_forget_b,
                w_branch_a=m_w_branch_a, w_branch_b=m_w_branch_b, w_out=m_w_out, norm_mlp_g=m_norm_mlp_g,
                w_up=m_w_up, w_down=m_w_down, norm_final_g=m_norm_final_g)
    vels = dict(norm_mix_g=v_norm_mix_g, w_in=v_w_in, conv_w=v_conv_w, conv_b=v_conv_b, lru_wa=v_lru_wa,
                lru_ba=v_lru_ba, lru_wx=v_lru_wx, lru_bx=v_lru_bx, lru_lambda=v_lru_lambda, forget_b=v_forget_b,
                w_branch_a=v_w_branch_a, w_branch_b=v_w_branch_b, w_out=v_w_out, norm_mlp_g=v_norm_mlp_g,
                w_up=v_w_up, w_down=v_w_down, norm_final_g=v_norm_final_g)
    S, D = x.shape[1], x.shape[2]
    me = 4 * lax.axis_index("x") + 2 * lax.axis_index("y") + lax.axis_index("c")

    core = lax.axis_index("c").astype(jnp.int32).reshape(1)
    chip = (2 * lax.axis_index("x") + lax.axis_index("y")).astype(jnp.int32).reshape(1)
    late_names = BIG_NAMES[1:]

    win_g, cw_g = _all_gather([w_in.astype(BF16), conv_w])
    late_shards = [weights[k].astype(BF16) for k in late_names]
    gather = _exchange_start(late_shards, [jax.ShapeDtypeStruct((N_DEV,) + s.shape, BF16) for s in late_shards],
                             _gather_copies, len(FLIPS), "gather_late_start")
    w_in_full = win_g.transpose(1, 0, 2).reshape(D, -1)
    cuts = (0, 2 * D, 5 * D, 7 * D)
    W = dict(in_xg=w_in_full[:, cuts[0]:cuts[1]], in_qkv_t=w_in_full[:, cuts[1]:cuts[2]].T,
             in_kv=w_in_full[:, cuts[1] + D:cuts[2]], in_gates=w_in_full[:, cuts[2]:cuts[3]],
             in_f=jnp.pad(w_in_full[:, cuts[3]:], ((0, 0), (0, LANES - N_HEADS))))
    small = {k: weights[k] for k in SMALL_NAMES}
    small["conv_w"] = cw_g.transpose(1, 0, 2).reshape(CONV_W, D)
    small["norm_mix_g"] = norm_mix_g + gather[4][0, 0]

    def late_weights(after):
        shards, lands = _exchange_wait(gather, _gather_copies, after, "gather_late_wait")
        wa_g, wb_g, wo_g, wup_g, wdn_g = (
            lax.dynamic_update_slice_in_dim(land, shard[None], me, axis=0) for land, shard in zip(lands, shards))
        return dict(branch_a=wa_g.reshape(D, D), branch_b=wb_g.reshape(D, D), out=wo_g.reshape(D, D),
                    up=wup_g.transpose(1, 0, 2).reshape(D, D_FF), down=wdn_g.reshape(D_FF, D))

    def cores_start(names, grads_by_name, tag):
        blocks = [_to_dest_blocks(k, grads_by_name[k]) for k in names]
        started = _exchange_start(blocks, [jax.ShapeDtypeStruct((4,) + b.shape[1:], F32) for b in blocks],
                                  _cores_copies, 4, "cores_" + tag + "_start")
        return started, started[4][0, 0]

    def chips_start(started, after, tag):
        blocks, got = _exchange_wait(started, _cores_copies, after, "cores_" + tag + "_wait")
        sums = [_chip_partial_sum(b, g, core) for b, g in zip(blocks, got)]
        wire = [s[1] for s in sums]
        scatter = _exchange_start(wire, [jax.ShapeDtypeStruct((3,) + s.shape[1:], BF16) for s in wire],
                                  _scatter_copies, 3, "scatter_" + tag + "_start")
        return (sums, scatter), scatter[4][0, 0]

    hooks = dict(early_start=lambda g: cores_start(late_names, g, "early"),
                 early_mid=lambda st, after: chips_start(st, after, "early"),
                 in_start=lambda g: cores_start(BIG_NAMES[:1], dict(w_in=g), "w_in"),
                 in_mid=lambda st, after: chips_start(st, after, "w_in"))
    loss_part, grad_x, _, small_grads, ((early_sums, early_scatter), (in_sums, in_scatter)) = _local_step(
        x.reshape(S, D), loss_target.reshape(S, D), W, small, late_weights, hooks)
    loss = lax.psum(loss_part, MESH_AXES)
    _, early_others = _exchange_wait(early_scatter, _scatter_copies, grad_x, "scatter_early_wait")
    _, in_others = _exchange_wait(in_scatter, _scatter_copies, grad_x, "scatter_w_in_wait")
    sums = list(in_sums) + list(early_sums)
    others = list(in_others) + list(early_others)

    reduced = _all_reduce_small(_pack_small([small_grads[k] for k in SMALL_NAMES] + [small_grads["conv_w"]]))
    cw_full = reduced[reduced.shape[0] - _packed_rows((CONV_W, D)):].reshape(CONV_W, D)
    cw_cols = lax.dynamic_slice_in_dim(cw_full, me * (D // N_DEV), D // N_DEV, axis=1)

    grads, deltas, new_m, new_v = _adamw_small(reduced, cw_cols, weights, moms, vels)
    for k, s, g_got in zip(BIG_NAMES, sums, others):
        grads[k], deltas[k], new_m[k], new_v[k] = _adamw(weights[k], moms[k], vels[k], s[0], g_got, chip, "adamw_" + k)

    return (loss, grad_x.reshape(1, S, D), *[grads[k] for k in WEIGHT_ORDER], *[deltas[k] for k in WEIGHT_ORDER],
            *[new_m[k] for k in WEIGHT_ORDER], *[new_v[k] for k in WEIGHT_ORDER])
```

```python
import functools
import math

import jax
import jax.numpy as jnp
from jax import lax
from jax.experimental import pallas as pl
from jax.experimental.pallas import tpu as pltpu

F32 = jnp.float32
BF16 = jnp.bfloat16

D_MODEL = 1024
N_HEADS = 8
HEAD_DIM = 128
D_FF = 4096
LRU_BLOCKS = 16
LRU_BW = 64
LRU_C = 8.0
CONV_W = 4
RMS_EPS = 1e-6
N_DEV = 8
LANES = 128
SUBLANES = 8
N_GROUPS = D_MODEL // LANES
VMEM_LIMIT_BYTES = 52 * 1024 * 1024
ATTN_SCALE = 1.0 / math.sqrt(HEAD_DIM)
LOG2E = math.log2(math.e)
NEG_BIG = -1e30
ADAM_LR = 0.001
ADAM_B1 = 0.9
ADAM_B2 = 0.999
ADAM_EPS = 1e-08
ADAM_WD = 0.01
ADAM_STEP = 10
ATTN_BLOCK = 1024
ATTN_Q_TILES = 4
ATTN_FWD_BLOCK = 2048
ATTN_FWD_Q_TILES = 2
ATTN_STRIP = 256
LRU_CHUNK = 256
ROW_TILE = 512
MESH_AXES = ("x", "y", "c")
MESH_ID = pl.DeviceIdType.MESH
ANY = pl.BlockSpec(memory_space=pl.ANY)

NT_DIMS = (((1,), (1,)), ((), ()))
TN_DIMS = (((0,), (0,)), ((), ()))
NN_DIMS = (((1,), (0,)), ((), ()))


def _cparams(*sem):
    return pltpu.CompilerParams(dimension_semantics=sem if sem else None, vmem_limit_bytes=VMEM_LIMIT_BYTES)


def _sigmoid(x):
    return 0.5 * (jnp.tanh(0.5 * x) + 1.0)


def _log1p_pos(e):
    u = 1.0 + e
    return jnp.where(u == 1.0, e, jnp.log(u) * (e / (u - 1.0)))


def _softplus(z):
    return jnp.maximum(z, 0.0) + _log1p_pos(jnp.exp(-jnp.abs(z)))


def _expm1_neg(x):
    series = x * (1.0 + x * 0.5 * (1.0 + x * (1.0 / 3.0) * (1.0 + x * 0.25)))
    return jnp.where(x > -0.03, series, jnp.exp(x) - 1.0)


GELU_C = math.sqrt(2.0 / math.pi)
GELU_K = 0.044715


def _gelu(x):
    return 0.5 * x * (1.0 + jnp.tanh(GELU_C * (x + GELU_K * (x * x * x))))


def _gelu_and_grad(x):
    t = jnp.tanh(GELU_C * (x + GELU_K * (x * x * x)))
    g = 0.5 * x * (1.0 + t)
    dg = 0.5 * (1.0 + t) + 0.5 * x * (1.0 - t * t) * (GELU_C * (1.0 + 3.0 * GELU_K * (x * x)))
    return g, dg


def _mm(pairs, *, ta=False, tb=False, tm, tn, tks, outs, name, epi=None, extra=(), col_blocked=False):
    n_pairs, n_extra, n_out = len(pairs), len(extra), len(outs)
    tas = list(ta) if isinstance(ta, (list, tuple)) else [ta] * n_pairs
    tbs = list(tb) if isinstance(tb, (list, tuple)) else [tb] * n_pairs
    a0, b0 = pairs[0]
    M = a0.shape[1] if tas[0] else a0.shape[0]
    N = b0.shape[0] if tbs[0] else b0.shape[1]
    tm, tn = min(tm, M), min(tn, N)
    nks, offs = [], []
    for (a, b), tk, pta in zip(pairs, tks, tas):
        K = a.shape[0] if pta else a.shape[1]
        assert K % tk == 0 and M % tm == 0 and N % tn == 0
        offs.append(sum(nks))
        nks.append(K // tk)
    nk_total = sum(nks)
    dims = [(((0 if pta else 1,), (1 if ptb else 0,)), ((), ())) for pta, ptb in zip(tas, tbs)]

    def kmap(off, nk):
        return lambda k: jnp.clip(k - off, 0, nk - 1)

    in_specs, operands = [], []
    for (a, b), tk, off, nk, pta, ptb in zip(pairs, tks, offs, nks, tas, tbs):
        km = kmap(off, nk)
        if pta:
            in_specs.append(pl.BlockSpec((tk, tm), lambda i, j, k, km=km: (km(k), i)))
        else:
            in_specs.append(pl.BlockSpec((tm, tk), lambda i, j, k, km=km: (i, km(k))))
        if ptb:
            in_specs.append(pl.BlockSpec((tn, tk), lambda i, j, k, km=km: (j, km(k))))
        else:
            in_specs.append(pl.BlockSpec((tk, tn), lambda i, j, k, km=km: (km(k), j)))
        operands += [a, b]
    for e in extra:
        in_specs.append(pl.BlockSpec((tm, tn), lambda i, j, k: (i, j)))
        operands.append(e)

    def body(*refs):
        ab = refs[:2 * n_pairs]
        ex = refs[2 * n_pairs:2 * n_pairs + n_extra]
        o = refs[2 * n_pairs + n_extra:2 * n_pairs + n_extra + n_out]
        k = pl.program_id(2)

        def finish(acc):
            res = epi(acc, *[e[...] for e in ex]) if epi is not None else (acc,)
            for r, oref in zip(res, o):
                oref[...] = r.astype(oref.dtype)

        if nk_total == 1:
            finish(lax.dot_general(ab[0][...], ab[1][...], dims[0], preferred_element_type=F32))
            return
        acc = refs[-1]
        for p in range(n_pairs):
            a_ref, b_ref = ab[2 * p], ab[2 * p + 1]

            @pl.when((k >= offs[p]) & (k < offs[p] + nks[p]))
            def _(a_ref=a_ref, b_ref=b_ref, pdims=dims[p]):
                prod = lax.dot_general(a_ref[...], b_ref[...], pdims, preferred_element_type=F32)

                @pl.when(k == 0)
                def _():
                    acc[...] = prod

                @pl.when(k > 0)
                def _():
                    acc[...] += prod

        @pl.when(k == nk_total - 1)
        def _():
            finish(acc[...])

    return pl.pallas_call(
        body,
        name=name,
        grid=(M // tm, N // tn, nk_total),
        in_specs=in_specs,
        out_specs=[pl.BlockSpec((None, tm, tn), lambda i, j, k: (j, i, 0)) if col_blocked
                   else pl.BlockSpec((tm, tn), lambda i, j, k: (i, j)) for _ in outs],
        out_shape=[jax.ShapeDtypeStruct((N // tn, M, tn) if col_blocked else (M, N), dt) for dt in outs],
        scratch_shapes=[] if nk_total == 1 else [pltpu.VMEM((tm, tn), F32)],
        compiler_params=_cparams("parallel", "parallel", "arbitrary"),
    )(*operands)


def _norm_fwd(x, g, name):
    S, D = x.shape
    tr = min(ROW_TILE, S)

    def body(x_ref, g_ref, o_ref):
        xv = x_ref[...]
        r = lax.rsqrt(jnp.mean(xv * xv, axis=-1, keepdims=True) + RMS_EPS)
        o_ref[...] = ((xv * r) * g_ref[...]).astype(o_ref.dtype)

    return pl.pallas_call(
        body, name=name, grid=(S // tr,),
        in_specs=[pl.BlockSpec((tr, D), lambda i: (i, 0)), pl.BlockSpec((1, D), lambda i: (0, 0))],
        out_specs=pl.BlockSpec((tr, D), lambda i: (i, 0)),
        out_shape=jax.ShapeDtypeStruct((S, D), BF16),
        compiler_params=_cparams("parallel"),
    )(x, g.reshape(1, D))


def _rms_bwd_rows(dy, xv, g):
    r = lax.rsqrt(jnp.mean(xv * xv, axis=-1, keepdims=True) + RMS_EPS)
    xn = xv * r
    dxn = dy * g
    dx = r * (dxn - xn * jnp.mean(dxn * xn, axis=-1, keepdims=True))
    dg = jnp.sum(dy * xn, axis=0, keepdims=True)
    return dx, dg


def _norm_bwd(dy, x, g, dres, name):
    S, D = x.shape
    tr = min(ROW_TILE, S)

    def body(dy_ref, x_ref, g_ref, dres_ref, dx_ref, dxb_ref, dg_ref):
        dx, dg = _rms_bwd_rows(dy_ref[...], x_ref[...], g_ref[...])
        dx = dres_ref[...] + dx
        dx_ref[...] = dx
        dxb_ref[...] = dx.astype(BF16)

        @pl.when(pl.program_id(0) == 0)
        def _():
            dg_ref[...] = jnp.zeros_like(dg_ref)

        dg_ref[...] += dg

    row = pl.BlockSpec((tr, D), lambda i: (i, 0))
    vec = pl.BlockSpec((1, D), lambda i: (0, 0))
    return pl.pallas_call(
        body, name=name, grid=(S // tr,),
        in_specs=[row, row, vec, row],
        out_specs=[row, row, vec],
        out_shape=[jax.ShapeDtypeStruct((S, D), F32), jax.ShapeDtypeStruct((S, D), BF16),
                   jax.ShapeDtypeStruct((1, D), F32)],
        compiler_params=_cparams("arbitrary"),
    )(dy, x, g.reshape(1, D), dres)


def _final_norm_loss(x2, target, g):
    S, D = x2.shape
    tr = min(ROW_TILE, S)

    def body(x_ref, t_ref, g_ref, loss_ref, dg_ref, dx_ref, dxb_ref):
        xv = x_ref[...]
        gv = g_ref[...]
        r = lax.rsqrt(jnp.mean(xv * xv, axis=-1, keepdims=True) + RMS_EPS)
        y = (xv * r) * gv
        err = y - t_ref[...]
        part = 0.5 * jnp.sum(jnp.mean(err * err, axis=-1, keepdims=True), axis=0, keepdims=True)
        dy = err * (1.0 / D)
        dx, dg = _rms_bwd_rows(dy, xv, gv)
        dx_ref[...] = dx
        dxb_ref[...] = dx.astype(BF16)

        @pl.when(pl.program_id(0) == 0)
        def _():
            dg_ref[...] = jnp.zeros_like(dg_ref)
            loss_ref[...] = jnp.zeros_like(loss_ref)

        dg_ref[...] += dg
        loss_ref[...] += jnp.broadcast_to(part, loss_ref.shape)

    row = pl.BlockSpec((tr, D), lambda i: (i, 0))
    vec = pl.BlockSpec((1, D), lambda i: (0, 0))
    return pl.pallas_call(
        body, name="final_norm_loss", grid=(S // tr,),
        in_specs=[row, row, vec],
        out_specs=[pl.BlockSpec((SUBLANES, LANES), lambda i: (0, 0)), vec, row, row],
        out_shape=[jax.ShapeDtypeStruct((SUBLANES, LANES), F32), jax.ShapeDtypeStruct((1, D), F32),
                   jax.ShapeDtypeStruct((S, D), F32), jax.ShapeDtypeStruct((S, D), BF16)],
        compiler_params=_cparams("arbitrary"),
    )(x2, target, g.reshape(1, D))


def _lru_gates(xa, bd_j, ba_j, bx_j, sp_j):
    z = jnp.dot(xa.astype(BF16), bd_j, preferred_element_type=F32)
    r = _sigmoid(z[:, :LANES] + ba_j)
    ig = _sigmoid(z[:, LANES:] + bx_j)
    log_a = (-LRU_C) * r * sp_j
    a = jnp.exp(log_a)
    mult = jnp.sqrt(-_expm1_neg(2.0 * log_a))
    return r, ig, a, mult


def _conv_rows(xpad, cw_ref, cb_ref, sl, tc):
    out = jnp.broadcast_to(cb_ref[:, sl], (tc, LANES))
    for k in range(CONV_W):
        out = out + xpad[pl.ds(SUBLANES - (CONV_W - 1) + k, tc), sl] * cw_ref[k:k + 1, sl]
    return out


def _lru_fwd(xg, cw, cb, bd, ba, bx, lam):
    S = xg.shape[0]
    D = D_MODEL
    tc = min(LRU_CHUNK, S)
    hb = tc // SUBLANES

    def body(xl_ref, halo_ref, g_ref, cw_ref, cb_ref, bd_ref, ba_ref, bx_ref, lam_ref,
             h_ref, y_ref, xpad, a_s, b_s, carry):
        i = pl.program_id(0)

        @pl.when(i == 0)
        def _():
            carry[...] = jnp.zeros_like(carry)

        xpad[0:SUBLANES, :] = jnp.where(i > 0, halo_ref[...], 0.0)
        xpad[SUBLANES:, :] = xl_ref[...]
        for j in range(N_GROUPS):
            sl = slice(LANES * j, LANES * (j + 1))
            xa = _conv_rows(xpad, cw_ref, cb_ref, sl, tc)
            sp = _softplus(-lam_ref[:, sl])
            _, ig, a, mult = _lru_gates(xa, bd_ref[j], ba_ref[:, sl], bx_ref[:, sl], sp)
            a_s[:, sl] = a
            b_s[:, sl] = mult * (ig * xa)

        row = lax.broadcasted_iota(jnp.int32, (SUBLANES, D), 0)

        def step(t, c):
            o = pl.multiple_of(t * SUBLANES, SUBLANES)
            A = a_s[pl.ds(o, SUBLANES), :]
            B = b_s[pl.ds(o, SUBLANES), :]
            for d in (1, 2, 4):
                keep = row >= d
                a_sh = jnp.where(keep, pltpu.roll(A, d, 0), 1.0)
                b_sh = jnp.where(keep, pltpu.roll(B, d, 0), 0.0)
                B = A * b_sh + B
                A = A * a_sh
            hh = A * c + B
            h_ref[pl.ds(o, SUBLANES), :] = hh
            return jnp.broadcast_to(hh[SUBLANES - 1:SUBLANES, :], (SUBLANES, D))

        carry[...] = lax.fori_loop(0, hb, step, carry[...])
        y_ref[...] = (_gelu(g_ref[...]) * h_ref[...]).astype(BF16)

    row_spec = lambda col: pl.BlockSpec((tc, D), lambda i, col=col: (i, col))
    halo = pl.BlockSpec((SUBLANES, D), lambda i: (jnp.maximum(i * hb - 1, 0), 0))
    full = lambda shape: pl.BlockSpec(shape, lambda i: tuple(0 for _ in shape))
    return pl.pallas_call(
        body, name="lru_fwd", grid=(S // tc,),
        in_specs=[row_spec(0), halo, row_spec(1), full((CONV_W, D)), full((1, D)),
                  full((N_GROUPS, LANES, 2 * LANES)), full((1, D)), full((1, D)), full((1, D))],
        out_specs=[pl.BlockSpec((tc, D), lambda i: (i, 0)), pl.BlockSpec((tc, D), lambda i: (i, 0))],
        out_shape=[jax.ShapeDtypeStruct((S, D), F32), jax.ShapeDtypeStruct((S, D), BF16)],
        scratch_shapes=[pltpu.VMEM((tc + SUBLANES, D), F32), pltpu.VMEM((tc, D), F32),
                        pltpu.VMEM((tc, D), F32), pltpu.VMEM((SUBLANES, D), F32)],
        compiler_params=_cparams("arbitrary"),
    )(xg, xg, xg, cw, cb, bd, ba, bx, lam)


def _lru_bwd(xg, h, dyain, cw, cb, bd, ba, bx, lam):
    S = xg.shape[0]
    D = D_MODEL
    tc = min(LRU_CHUNK, S)
    hb = tc // SUBLANES
    nc = S // tc

    def body(xl_ref, xhalo_ref, g_ref, h_ref, hhalo_ref, dy_ref, cw_ref, cb_ref, bd_ref, ba_ref, bx_ref,
             lam_ref, dxg_ref, dcw_ref, dcb_ref, dba_ref, dbx_ref, dlam_ref, dbd_ref,
             xpad, hpad, a_s, b_s, dh_s, g_s, xa_s, r_s, ig_s, m_s, dxa_pad, carry_e, dxa_head):
        i = pl.program_id(0)
        c = nc - 1 - i

        @pl.when(i == 0)
        def _():
            carry_e[...] = jnp.zeros_like(carry_e)
            dxa_head[...] = jnp.zeros_like(dxa_head)
            for ref in (dcw_ref, dcb_ref, dba_ref, dbx_ref, dlam_ref, dbd_ref):
                ref[...] = jnp.zeros_like(ref)

        xpad[0:SUBLANES, :] = jnp.where(c > 0, xhalo_ref[...], 0.0)
        xpad[SUBLANES:, :] = xl_ref[...]
        hpad[0:SUBLANES, :] = jnp.where(c > 0, hhalo_ref[...], 0.0)
        hpad[SUBLANES:, :] = h_ref[...]

        for j in range(N_GROUPS):
            sl = slice(LANES * j, LANES * (j + 1))
            xa = _conv_rows(xpad, cw_ref, cb_ref, sl, tc)
            sp = _softplus(-lam_ref[:, sl])
            r, ig, a, mult = _lru_gates(xa, bd_ref[j], ba_ref[:, sl], bx_ref[:, sl], sp)
            gl, dgl = _gelu_and_grad(g_ref[:, sl])
            dy = dy_ref[:, sl]
            dh = dy * gl
            dxg_ref[:, D + LANES * j:D + LANES * (j + 1)] = (dy * h_ref[:, sl] * dgl).astype(BF16)
            a_s[:, sl] = a
            b_s[:, sl] = a * dh
            dh_s[:, sl] = dh
            xa_s[:, sl] = xa
            r_s[:, sl] = r
            ig_s[:, sl] = ig
            m_s[:, sl] = mult

        row = lax.broadcasted_iota(jnp.int32, (SUBLANES, D), 0)

        def step(tt, ce):
            o = pl.multiple_of((hb - 1 - tt) * SUBLANES, SUBLANES)
            A = a_s[pl.ds(o, SUBLANES), :]
            B = b_s[pl.ds(o, SUBLANES), :]
            for d in (1, 2, 4):
                keep = row < SUBLANES - d
                a_sh = jnp.where(keep, pltpu.roll(A, SUBLANES - d, 0), 1.0)
                b_sh = jnp.where(keep, pltpu.roll(B, SUBLANES - d, 0), 0.0)
                B = A * b_sh + B
                A = A * a_sh
            e = A * ce + B
            e_next = jnp.where(row < SUBLANES - 1, pltpu.roll(e, SUBLANES - 1, 0), ce)
            g_s[pl.ds(o, SUBLANES), :] = dh_s[pl.ds(o, SUBLANES), :] + e_next
            return jnp.broadcast_to(e[0:1, :], (SUBLANES, D))

        carry_e[...] = lax.fori_loop(0, hb, step, carry_e[...])

        for j in range(N_GROUPS):
            sl = slice(LANES * j, LANES * (j + 1))
            gg = g_s[:, sl]
            xa, r, ig, mult, a = xa_s[:, sl], r_s[:, sl], ig_s[:, sl], m_s[:, sl], a_s[:, sl]
            hprev = hpad[pl.ds(SUBLANES - 1, tc), sl]
            sp = _softplus(-lam_ref[:, sl])
            da = gg * hprev
            dmult = gg * (ig * xa)
            dig = gg * (mult * xa)
            dxa = gg * (mult * ig)
            dla = da * a - dmult * ((a * a) / mult)
            dr = dla * ((-LRU_C) * sp)
            dlam_ref[:, sl] += jnp.sum(dla * r, axis=0, keepdims=True)
            dza = dr * r * (1.0 - r)
            dzx = dig * ig * (1.0 - ig)
            dba_ref[:, sl] += jnp.sum(dza, axis=0, keepdims=True)
            dbx_ref[:, sl] += jnp.sum(dzx, axis=0, keepdims=True)
            dz = jnp.concatenate([dza, dzx], axis=1).astype(BF16)
            dbd_ref[j] += lax.dot_general(xa.astype(BF16), dz, TN_DIMS, preferred_element_type=F32)
            dxa = dxa + lax.dot_general(dz, bd_ref[j], NT_DIMS, preferred_element_type=F32)
            dxa_pad[0:tc, sl] = dxa

        dxa_pad[tc:, :] = dxa_head[...]
        dxa_head[...] = dxa_pad[0:SUBLANES, :]

        for j in range(N_GROUPS):
            sl = slice(LANES * j, LANES * (j + 1))
            dxa = dxa_pad[0:tc, sl]
            dxl = jnp.zeros((tc, LANES), F32)
            for k in range(CONV_W):
                dxl = dxl + dxa_pad[pl.ds(CONV_W - 1 - k, tc), sl] * cw_ref[k:k + 1, sl]
                dcw_ref[k:k + 1, sl] += jnp.sum(
                    dxa * xpad[pl.ds(SUBLANES - (CONV_W - 1) + k, tc), sl], axis=0, keepdims=True)
            dxg_ref[:, sl] = dxl.astype(BF16)
            dcb_ref[:, sl] += jnp.sum(dxa, axis=0, keepdims=True)

        @pl.when(i == nc - 1)
        def _():
            dlam_ref[...] = dlam_ref[...] * (LRU_C * _sigmoid(-lam_ref[...]))

    rev = lambda col: pl.BlockSpec((tc, D), lambda i, col=col: (nc - 1 - i, col))
    halo = pl.BlockSpec((SUBLANES, D), lambda i: (jnp.maximum((nc - 1 - i) * hb - 1, 0), 0))
    full = lambda shape: pl.BlockSpec(shape, lambda i: tuple(0 for _ in shape))
    big = lambda: pltpu.VMEM((tc, D), F32)
    return pl.pallas_call(
        body, name="lru_bwd", grid=(nc,),
        in_specs=[rev(0), halo, rev(1), rev(0), halo, rev(0), full((CONV_W, D)), full((1, D)),
                  full((N_GROUPS, LANES, 2 * LANES)), full((1, D)), full((1, D)), full((1, D))],
        out_specs=[pl.BlockSpec((tc, 2 * D), lambda i: (nc - 1 - i, 0)), full((CONV_W, D)), full((1, D)),
                   full((1, D)), full((1, D)), full((1, D)), full((N_GROUPS, LANES, 2 * LANES))],
        out_shape=[jax.ShapeDtypeStruct((S, 2 * D), BF16), jax.ShapeDtypeStruct((CONV_W, D), F32),
                   jax.ShapeDtypeStruct((1, D), F32), jax.ShapeDtypeStruct((1, D), F32),
                   jax.ShapeDtypeStruct((1, D), F32), jax.ShapeDtypeStruct((1, D), F32),
                   jax.ShapeDtypeStruct((N_GROUPS, LANES, 2 * LANES), F32)],
        scratch_shapes=[pltpu.VMEM((tc + SUBLANES, D), F32), pltpu.VMEM((tc + SUBLANES, D), F32),
                        big(), big(), big(), big(), big(), big(), big(), big(),
                        pltpu.VMEM((tc + SUBLANES, D), F32), pltpu.VMEM((SUBLANES, D), F32),
                        pltpu.VMEM((SUBLANES, D), F32)],
        compiler_params=_cparams("arbitrary"),
    )(xg, xg, xg, h, h, dyain, cw, cb, bd, ba, bx, lam)


def _forget_cumsum(fl, fb):
    S = fl.shape[0]
    tr = min(ROW_TILE, S)
    hb = tr // SUBLANES

    def body(fl_ref, fb_ref, o_ref, rep_ref, lf_s, carry):
        @pl.when(pl.program_id(0) == 0)
        def _():
            carry[...] = jnp.zeros_like(carry)

        lf_s[...] = -_softplus(-(fl_ref[...] + fb_ref[...]))
        row = lax.broadcasted_iota(jnp.int32, (SUBLANES, LANES), 0)

        def step(t, c):
            o = pl.multiple_of(t * SUBLANES, SUBLANES)
            B = lf_s[pl.ds(o, SUBLANES), :]
            for d in (1, 2, 4):
                B = B + jnp.where(row >= d, pltpu.roll(B, d, 0), 0.0)
            B = B + c
            o_ref[pl.ds(o, SUBLANES), :] = B * LOG2E
            return jnp.broadcast_to(B[SUBLANES - 1:SUBLANES, :], (SUBLANES, LANES))

        carry[...] = lax.fori_loop(0, hb, step, carry[...])
        for h in range(N_HEADS):
            rep_ref[h] = jnp.broadcast_to(o_ref[:, h:h + 1], (tr, LANES))

    return pl.pallas_call(
        body, name="forget_cumsum", grid=(S // tr,),
        in_specs=[pl.BlockSpec((tr, LANES), lambda i: (i, 0)), pl.BlockSpec((1, LANES), lambda i: (0, 0))],
        out_specs=[pl.BlockSpec((tr, LANES), lambda i: (i, 0)),
                   pl.BlockSpec((N_HEADS, tr, LANES), lambda i: (0, i, 0))],
        out_shape=[jax.ShapeDtypeStruct((S, LANES), F32), jax.ShapeDtypeStruct((N_HEADS, S, LANES), F32)],
        scratch_shapes=[pltpu.VMEM((tr, LANES), F32), pltpu.VMEM((SUBLANES, LANES), F32)],
        compiler_params=_cparams("arbitrary"),
    )(fl, fb)


def _forget_bwd(dF, fl, fb):
    S = fl.shape[0]
    tr = min(ROW_TILE, S)
    hb = tr // SUBLANES
    nc = S // tr

    def body(df_ref, fl_ref, fb_ref, o_ref, dfb_ref, carry):
        @pl.when(pl.program_id(0) == 0)
        def _():
            carry[...] = jnp.zeros_like(carry)
            dfb_ref[...] = jnp.zeros_like(dfb_ref)

        row = lax.broadcasted_iota(jnp.int32, (SUBLANES, LANES), 0)

        def step(tt, carried):
            c, acc = carried
            o = pl.multiple_of((hb - 1 - tt) * SUBLANES, SUBLANES)
            B = df_ref[pl.ds(o, SUBLANES), :]
            for d in (1, 2, 4):
                B = B + jnp.where(row < SUBLANES - d, pltpu.roll(B, SUBLANES - d, 0), 0.0)
            B = B + c
            z = fl_ref[pl.ds(o, SUBLANES), :] + fb_ref[...]
            dz = B * _sigmoid(-z)
            o_ref[pl.ds(o, SUBLANES), :] = dz.astype(BF16)
            return jnp.broadcast_to(B[0:1, :], (SUBLANES, LANES)), acc + dz

        c, acc = lax.fori_loop(0, hb, step, (carry[...], jnp.zeros((SUBLANES, LANES), F32)))
        carry[...] = c
        dfb_ref[...] += jnp.sum(acc, axis=0, keepdims=True)

    rev = pl.BlockSpec((tr, LANES), lambda i: (nc - 1 - i, 0))
    vec = pl.BlockSpec((1, LANES), lambda i: (0, 0))
    return pl.pallas_call(
        body, name="forget_bwd", grid=(nc,),
        in_specs=[rev, rev, vec],
        out_specs=[rev, vec],
        out_shape=[jax.ShapeDtypeStruct((S, LANES), BF16), jax.ShapeDtypeStruct((1, LANES), F32)],
        scratch_shapes=[pltpu.VMEM((SUBLANES, LANES), F32)],
        compiler_params=_cparams("arbitrary"),
    )(dF, fl, fb)


def _triangle(n, qw, key_major):
    pairs = [(q, k) for q in range(n) for k in range(qw * (q + 1))]
    if key_major:
        pairs.sort(key=lambda qk: (qk[1], qk[0]))
    return (jnp.asarray([q for q, _ in pairs], jnp.int32), jnp.asarray([k for _, k in pairs], jnp.int32))


def _strip_plan(bk, bq, strip, rel):
    plan = []
    for j in range(bq // strip):
        if rel is None:
            plan.append((j, bk, None))
            continue
        reach = strip * (j + 1) - rel * bk
        if reach > 0:
            plan.append((j, min(reach, bk), strip * j - rel * bk if reach <= bk else None))
    return plan


def _strip_scores(k_ref, qt_ref, fk_ref, strip, j, nkeys, mask_off):
    cols = slice(strip * j, strip * (j + 1))
    s = jnp.dot(k_ref[0:nkeys, :], qt_ref[:, cols], preferred_element_type=F32) * (ATTN_SCALE * LOG2E)
    fk = fk_ref[0:nkeys, :]
    s = s - jnp.concatenate([fk] * (strip // LANES), axis=1)
    keep = None
    if mask_off is not None:
        keys = lax.broadcasted_iota(jnp.int32, (nkeys, strip), 0)
        queries = lax.broadcasted_iota(jnp.int32, (nkeys, strip), 1) + mask_off
        keep = keys <= queries
    return s, keep


def _attn_fwd(kv, qkv_t, f_row, f_rep):
    S = kv.shape[0]
    bk = min(ATTN_FWD_BLOCK, S)
    strip = min(ATTN_STRIP, bk)
    qw = min(ATTN_FWD_Q_TILES, S // bk)
    bq = qw * bk
    tri_q, tri_k = _triangle(S // bq, qw, key_major=False)
    ones_rows = 2 * SUBLANES

    def body(tq_ref, tk_ref, k_ref, qt_ref, vt_ref, fq_ref, fk_ref, ot_ref, lse_ref, m_s, acc_s, vta_s):
        t = pl.program_id(1)
        qi, ki = tq_ref[t], tk_ref[t]
        rel = ki - qw * qi

        @pl.when(ki == 0)
        def _():
            m_s[...] = jnp.full_like(m_s, NEG_BIG)
            acc_s[...] = jnp.zeros_like(acc_s)

        vta_s[0:HEAD_DIM, :] = vt_ref[...]
        vta_s[HEAD_DIM:, :] = jnp.ones((ones_rows, bk), BF16)

        def update(plan):
            scores = lambda entry: _strip_scores(k_ref, qt_ref, fk_ref, strip, *entry)

            def weighted_values(j, nkeys, alpha, pb):
                cols = slice(strip * j, strip * (j + 1))
                acc_s[:, cols] = alpha * acc_s[:, cols] + jnp.dot(
                    vta_s[:, 0:nkeys], pb, preferred_element_type=F32)

            ahead, behind = scores(plan[0]), None
            for i, (j, nkeys, mask_off) in enumerate(plan):
                cols = slice(strip * j, strip * (j + 1))
                (s, keep), ahead = ahead, (scores(plan[i + 1]) if i + 1 < len(plan) else None)
                if behind is not None:
                    weighted_values(*behind)
                if keep is not None:
                    s = jnp.where(keep, s, NEG_BIG)
                fq = fq_ref[:, cols]
                m_old = m_s[:, cols]
                m_new = jnp.maximum(m_old, jnp.max(s, axis=0, keepdims=True) + fq)
                p = jnp.exp2(s - (m_new - fq))
                behind = (j, nkeys, jnp.exp2(m_old - m_new), p.astype(BF16))
                m_s[:, cols] = m_new
            weighted_values(*behind)

        @pl.when(rel < 0)
        def _():
            update(_strip_plan(bk, bq, strip, None))

        for d in range(qw):
            @pl.when(rel == d)
            def _(d=d):
                update(_strip_plan(bk, bq, strip, d))
                if d == qw - 1:
                    denom = acc_s[HEAD_DIM:HEAD_DIM + 1, :]
                    ot_ref[...] = (acc_s[0:HEAD_DIM, :] / denom).astype(BF16)
                    lse_ref[...] = m_s[...] + jnp.log2(denom)

    return pl.pallas_call(
        body, name="attn_fwd",
        grid_spec=pltpu.PrefetchScalarGridSpec(
            num_scalar_prefetch=2, grid=(N_HEADS, tri_q.shape[0]),
            in_specs=[pl.BlockSpec((bk, HEAD_DIM), lambda h, t, tq, tk: (tk[t], h)),
                      pl.BlockSpec((HEAD_DIM, bq), lambda h, t, tq, tk: (h, tq[t])),
                      pl.BlockSpec((HEAD_DIM, bk), lambda h, t, tq, tk: (2 * N_HEADS + h, tk[t])),
                      pl.BlockSpec((None, 1, bq), lambda h, t, tq, tk: (h, 0, tq[t])),
                      pl.BlockSpec((None, bk, LANES), lambda h, t, tq, tk: (h, tk[t], 0))],
            out_specs=[pl.BlockSpec((HEAD_DIM, bq), lambda h, t, tq, tk: (h, tq[t])),
                       pl.BlockSpec((None, 1, bq), lambda h, t, tq, tk: (h, 0, tq[t]))],
            scratch_shapes=[pltpu.VMEM((1, bq), F32), pltpu.VMEM((HEAD_DIM + ones_rows, bq), F32),
                            pltpu.VMEM((HEAD_DIM + ones_rows, bk), BF16)]),
        out_shape=[jax.ShapeDtypeStruct((N_HEADS * HEAD_DIM, S), BF16), jax.ShapeDtypeStruct((N_HEADS, 1, S), F32)],
        compiler_params=_cparams("parallel", "arbitrary"),
    )(tri_q, tri_k, kv, qkv_t, qkv_t, f_row, f_rep)


def _attn_bwd(kv, qkv_t, do_t, o_t, lse, f_row, f_rep):
    S = kv.shape[0]
    bk = min(ATTN_BLOCK, S)
    strip = min(ATTN_STRIP, bk)
    qw = min(ATTN_Q_TILES, S // bk)
    bq = qw * bk
    nq = S // bq
    tri_q, tri_k = _triangle(nq, qw, key_major=True)
    n_tiles = tri_q.shape[0]

    def body(tq_ref, tk_ref, k_ref, v_ref, qt_ref, kt_ref, dot_ref, ot_ref, lse_ref, fq_ref, fk_ref,
             dqt_ref, dkt_ref, dvt_ref, dfk_ref, dfq_ref, dq_s, dk_s, dv_s, dfk_s, dfq_s, row_s):
        t = pl.program_id(1)
        qi, ki = tq_ref[t], tk_ref[t]
        rel = ki - qw * qi

        @pl.when(t == 0)
        def _():
            dq_s[...] = jnp.zeros_like(dq_s)
            dfq_s[...] = jnp.zeros_like(dfq_s)

        @pl.when(rel >= 0)
        def _():
            dk_s[...] = jnp.zeros_like(dk_s)
            dv_s[...] = jnp.zeros_like(dv_s)
            dfk_s[...] = jnp.zeros_like(dfk_s)

        def update(plan):
            row_s[...] = fq_ref[...] - lse_ref[...]

            def matmuls_in(j, nkeys, mask_off):
                s, keep = _strip_scores(k_ref, qt_ref, fk_ref, strip, j, nkeys, mask_off)
                dp = jnp.dot(v_ref[0:nkeys, :], dot_ref[:, strip * j:strip * (j + 1)], preferred_element_type=F32)
                return s, keep, dp

            def matmuls_out(j, nkeys, pb, dsb):
                cols = slice(strip * j, strip * (j + 1))
                dv_s[:, 0:nkeys] += lax.dot_general(dot_ref[:, cols], pb, NT_DIMS, preferred_element_type=F32)
                dk_s[:, 0:nkeys] += lax.dot_general(qt_ref[:, cols], dsb, NT_DIMS, preferred_element_type=F32)
                dq_s[qi, :, cols] += jnp.dot(kt_ref[:, 0:nkeys], dsb, preferred_element_type=F32)

            ahead, behind = matmuls_in(*plan[0]), None
            for i, (j, nkeys, mask_off) in enumerate(plan):
                cols = slice(strip * j, strip * (j + 1))
                (s, keep, dp), ahead = ahead, (matmuls_in(*plan[i + 1]) if i + 1 < len(plan) else None)
                if behind is not None:
                    matmuls_out(*behind)
                p = jnp.exp2(s + row_s[:, cols])
                if keep is not None:
                    p = jnp.where(keep, p, 0.0)
                dot = dot_ref[:, cols]
                delta = jnp.sum(dot.astype(F32) * ot_ref[:, cols].astype(F32), axis=0, keepdims=True)
                ds = p * (dp - delta)
                behind = (j, nkeys, p.astype(BF16), ds.astype(BF16))
                lane_part = ds[:, 0:LANES]
                for g in range(1, strip // LANES):
                    lane_part = lane_part + ds[:, LANES * g:LANES * (g + 1)]
                dfk_s[0:nkeys, :] += lane_part
                sub_part = ds[0:SUBLANES, :]
                for g in range(1, nkeys // SUBLANES):
                    sub_part = sub_part + ds[SUBLANES * g:SUBLANES * (g + 1), :]
                dfq_s[qi, :, cols] += sub_part
            matmuls_out(*behind)

        @pl.when(rel < 0)
        def _():
            update(_strip_plan(bk, bq, strip, None))

        for d in range(qw):
            @pl.when(rel == d)
            def _(d=d):
                update(_strip_plan(bk, bq, strip, d))

        @pl.when(qi == nq - 1)
        def _():
            dkt_ref[...] = (dk_s[...] * ATTN_SCALE).astype(BF16)
            dvt_ref[...] = dv_s[...].astype(BF16)
            dfk_ref[...] = -jnp.sum(dfk_s[...].T, axis=0, keepdims=True)

        @pl.when(t == n_tiles - 1)
        def _():
            for j in range(nq):
                dqt_ref[:, bq * j:bq * (j + 1)] = (dq_s[j] * ATTN_SCALE).astype(BF16)
                dfq_ref[:, bq * j:bq * (j + 1)] = jnp.sum(dfq_s[j], axis=0, keepdims=True)

    q_feat = pl.BlockSpec((HEAD_DIM, bq), lambda h, t, tq, tk: (h, tq[t]))
    q_row = pl.BlockSpec((None, 1, bq), lambda h, t, tq, tk: (h, 0, tq[t]))
    k_feat = pl.BlockSpec((HEAD_DIM, bk), lambda h, t, tq, tk: (h, tk[t]))
    return pl.pallas_call(
        body, name="attn_bwd",
        grid_spec=pltpu.PrefetchScalarGridSpec(
            num_scalar_prefetch=2, grid=(N_HEADS, n_tiles),
            in_specs=[pl.BlockSpec((bk, HEAD_DIM), lambda h, t, tq, tk: (tk[t], h)),
                      pl.BlockSpec((bk, HEAD_DIM), lambda h, t, tq, tk: (tk[t], N_HEADS + h)),
                      q_feat,
                      pl.BlockSpec((HEAD_DIM, bk), lambda h, t, tq, tk: (N_HEADS + h, tk[t])),
                      q_feat, q_feat, q_row, q_row,
                      pl.BlockSpec((None, bk, LANES), lambda h, t, tq, tk: (h, tk[t], 0))],
            out_specs=[pl.BlockSpec((HEAD_DIM, S), lambda h, t, tq, tk: (h, 0)), k_feat, k_feat,
                       pl.BlockSpec((None, 1, bk), lambda h, t, tq, tk: (h, 0, tk[t])),
                       pl.BlockSpec((None, 1, S), lambda h, t, tq, tk: (h, 0, 0))],
            scratch_shapes=[pltpu.VMEM((nq, HEAD_DIM, bq), F32), pltpu.VMEM((HEAD_DIM, bk), F32),
                            pltpu.VMEM((HEAD_DIM, bk), F32), pltpu.VMEM((bk, LANES), F32),
                            pltpu.VMEM((nq, SUBLANES, bq), F32), pltpu.VMEM((1, bq), F32)]),
        out_shape=[jax.ShapeDtypeStruct((N_HEADS * HEAD_DIM, S), BF16)] * 3
        + [jax.ShapeDtypeStruct((N_HEADS, 1, S), F32), jax.ShapeDtypeStruct((N_HEADS, 1, S), F32)],
        compiler_params=_cparams("parallel", "arbitrary"),
    )(tri_q, tri_k, kv, kv, qkv_t, qkv_t, do_t, o_t, lse, f_row, f_rep)


def _gate_mix(gates, ya, yb):
    S, D = ya.shape
    tr = min(ROW_TILE, S)

    def body(ga_ref, gb_ref, ya_ref, yb_ref, o_ref):
        o_ref[...] = (_sigmoid(ga_ref[...]) * ya_ref[...] + _sigmoid(gb_ref[...]) * yb_ref[...]).astype(BF16)

    col = lambda j: pl.BlockSpec((tr, D), lambda i, j=j: (i, j))
    return pl.pallas_call(
        body, name="gate_mix", grid=(S // tr,),
        in_specs=[col(0), col(1), col(0), col(0)],
        out_specs=col(0),
        out_shape=jax.ShapeDtypeStruct((S, D), BF16),
        compiler_params=_cparams("parallel"),
    )(gates, gates, ya, yb)


def _gate_bwd(dmix, gates, ya, yb):
    S, D = ya.shape
    tr = min(ROW_TILE, S)

    def body(dm_ref, ga_ref, gb_ref, ya_ref, yb_ref, dya_ref, dyb_ref, dg_ref):
        dm = dm_ref[...]
        sa, sb = _sigmoid(ga_ref[...]), _sigmoid(gb_ref[...])
        dya_ref[...] = (dm * sa).astype(BF16)
        dyb_ref[...] = (dm * sb).astype(BF16)
        dg_ref[:, 0:D] = ((dm * ya_ref[...]) * (sa * (1.0 - sa))).astype(BF16)
        dg_ref[:, D:] = ((dm * yb_ref[...]) * (sb * (1.0 - sb))).astype(BF16)

    col = lambda j: pl.BlockSpec((tr, D), lambda i, j=j: (i, j))
    return pl.pallas_call(
        body, name="gate_bwd", grid=(S // tr,),
        in_specs=[col(0), col(0), col(1), col(0), col(0)],
        out_specs=[col(0), col(0), pl.BlockSpec((tr, 2 * D), lambda i: (i, 0))],
        out_shape=[jax.ShapeDtypeStruct((S, D), BF16), jax.ShapeDtypeStruct((S, D), BF16),
                   jax.ShapeDtypeStruct((S, 2 * D), BF16)],
        compiler_params=_cparams("parallel"),
    )(dmix, gates, gates, ya, yb)


def _mesh_place():
    x, y, c = lax.axis_index("x"), lax.axis_index("y"), lax.axis_index("c")
    chips = [(1 - x, y), (x, 1 - y), (1 - x, 1 - y)]
    return x, y, c, chips


def _all_gather(shards):
    n = len(shards)

    def body(*refs):
        ins, outs = refs[:n], refs[n:2 * n]
        send_sems, recv_sems, local_sems = refs[2 * n:]
        x, y, c, chips = _mesh_place()
        me, sib = (x, y, c), (x, y, 1 - c)

        def copy(a, k, block, to, src=None):
            px, py, pc = block
            dst = outs[a].at[4 * px + 2 * py + pc]
            return pltpu.make_async_remote_copy(
                src_ref=dst if src is None else src, dst_ref=dst,
                send_sem=send_sems.at[a, k], recv_sem=recv_sems.at[a, k],
                device_id=to, device_id_type=MESH_ID)

        mine = [pltpu.make_async_copy(ins[a], outs[a].at[4 * x + 2 * y + c], local_sems.at[a]) for a in range(n)]
        for cp in mine:
            cp.start()
        first = []
        for a in range(n):
            first.append(copy(a, 0, me, sib, src=ins[a]))
            for j, chip in enumerate(chips):
                first.append(copy(a, 1 + j, me, (*chip, c), src=ins[a]))
        for cp in first:
            cp.start()
        passed = []
        for j, chip in enumerate(chips):
            for a in range(n):
                copy(a, 1 + j, (*chip, c), me).wait_recv()
                fwd = copy(a, 4 + j, (*chip, c), sib)
                fwd.start()
                passed.append(fwd)
        for a in range(n):
            copy(a, 0, sib, me).wait_recv()
            for j, chip in enumerate(chips):
                copy(a, 4 + j, (*chip, 1 - c), me).wait_recv()
        for cp in first + passed:
            cp.wait_send()
        for cp in mine:
            cp.wait()

    return pl.pallas_call(
        body, name="all_gather_weights",
        in_specs=[ANY] * n, out_specs=[ANY] * n,
        out_shape=[jax.ShapeDtypeStruct((N_DEV,) + s.shape, s.dtype) for s in shards],
        scratch_shapes=[pltpu.SemaphoreType.DMA((n, 7)), pltpu.SemaphoreType.DMA((n, 7)),
                        pltpu.SemaphoreType.DMA((n,))],
    )(*shards)


def _chip_partial_sum(blocks, got, core):
    R, C = got.shape[1:]
    tr = min(256, R)
    assert R % tr == 0

    def body(core_ref, a_ref, b_ref, s_ref, sb_ref):
        s = a_ref[...] + b_ref[...]
        s_ref[...] = s
        sb_ref[...] = s.astype(BF16)

    blk = pl.BlockSpec((None, tr, C), lambda k, i, core_ref: (k, i, 0))
    return pl.pallas_call(
        body, name="chip_partial_sum",
        grid_spec=pltpu.PrefetchScalarGridSpec(
            num_scalar_prefetch=1, grid=(4, R // tr),
            in_specs=[pl.BlockSpec((None, tr, C), lambda k, i, core_ref: (2 * k + core_ref[0], i, 0)), blk],
            out_specs=[blk, blk]),
        out_shape=[jax.ShapeDtypeStruct(got.shape, F32), jax.ShapeDtypeStruct(got.shape, BF16)],
        compiler_params=_cparams("parallel", "parallel"),
    )(core, blocks, got)


HBM_SPEC = pl.BlockSpec(memory_space=pltpu.HBM)
SEM_SPEC = pl.BlockSpec(memory_space=pltpu.SEMAPHORE)
FLIPS = [(dx, dy, dc) for dx in (0, 1) for dy in (0, 1) for dc in (0, 1) if (dx, dy, dc) != (0, 0, 0)]


def _flip(v, d):
    return 1 - v if d else v


def _gather_copies(srcs, lands, send_sems, recv_sems):
    x, y, c, _ = _mesh_place()
    sends, recvs = [], []
    for a in range(len(srcs)):
        for k, (dx, dy, dc) in enumerate(FLIPS):
            px, py, pc = _flip(x, dx), _flip(y, dy), _flip(c, dc)
            sem = len(FLIPS) * a + k
            common = dict(send_sem=send_sems.at[sem], recv_sem=recv_sems.at[sem],
                          device_id=(px, py, pc), device_id_type=MESH_ID)
            sends.append(pltpu.make_async_remote_copy(
                src_ref=srcs[a], dst_ref=lands[a].at[4 * x + 2 * y + c], **common))
            recvs.append(pltpu.make_async_remote_copy(
                src_ref=srcs[a], dst_ref=lands[a].at[4 * px + 2 * py + pc], **common))
    return sends, recvs


def _cores_copies(srcs, lands, send_sems, recv_sems):
    x, y, c, _ = _mesh_place()
    copies = []
    for a in range(len(srcs)):
        for k in range(4):
            copies.append(pltpu.make_async_remote_copy(
                src_ref=srcs[a].at[2 * k + (1 - c)], dst_ref=lands[a].at[k],
                send_sem=send_sems.at[4 * a + k], recv_sem=recv_sems.at[4 * a + k],
                device_id=(x, y, 1 - c), device_id_type=MESH_ID))
    return copies, copies


def _scatter_copies(srcs, lands, send_sems, recv_sems):
    x, y, c, chips = _mesh_place()
    sends = []
    for a in range(len(srcs)):
        for j, (px, py) in enumerate(chips):
            sends.append(pltpu.make_async_remote_copy(
                src_ref=srcs[a].at[2 * px + py], dst_ref=lands[a].at[j],
                send_sem=send_sems.at[3 * a + j], recv_sem=recv_sems.at[3 * a + j],
                device_id=(px, py, c), device_id_type=MESH_ID))
    return sends, sends


def _exchange_start(srcs, land_shapes, copies, n_copies, name):
    n = len(srcs)

    def body(*refs):
        src_refs, land_refs = refs[:n], refs[n:2 * n]
        send_sems, recv_sems = refs[2 * n], refs[2 * n + 1]
        token = refs[-1]
        sends, _ = copies(src_refs, land_refs, send_sems, recv_sems)
        for cp in sends:
            cp.start()
        token[...] = jnp.zeros_like(token)

    lands = [pltpu.with_memory_space_constraint(lax.empty(s.shape, s.dtype), pltpu.HBM) for s in land_shapes]
    srcs = [pltpu.with_memory_space_constraint(s, pltpu.HBM) for s in srcs]
    res = pl.pallas_call(
        body, name=name,
        out_shape=(pltpu.SemaphoreType.DMA((n * n_copies,)), pltpu.SemaphoreType.DMA((n * n_copies,)),
                   *[pltpu.HBM(s.shape, s.dtype) for s in srcs], *[pltpu.HBM(s.shape, s.dtype) for s in land_shapes],
                   jax.ShapeDtypeStruct((SUBLANES, LANES), F32)),
        in_specs=[HBM_SPEC] * (2 * n),
        out_specs=(SEM_SPEC, SEM_SPEC, *[HBM_SPEC] * (2 * n), pl.BlockSpec(memory_space=pltpu.VMEM)),
        input_output_aliases={i: 2 + i for i in range(2 * n)},
        compiler_params=pltpu.CompilerParams(has_side_effects=pltpu.SideEffectType.DATAFLOW_SIDE_EFFECTING),
    )(*srcs, *lands)
    return res[0], res[1], list(res[2:2 + n]), list(res[2 + n:2 + 2 * n]), res[-1]


def _exchange_wait(started, copies, after, name):
    send_sems, recv_sems, srcs, lands, _ = started
    n = len(srcs)

    def body(*refs):
        src_refs, land_refs = refs[:n], refs[n:2 * n]
        send_ref, recv_ref = refs[2 * n], refs[2 * n + 1]
        sends, recvs = copies(src_refs, land_refs, send_ref, recv_ref)
        for cp in sends:
            cp.wait_send()
        for cp in recvs:
            cp.wait_recv()

    res = pl.pallas_call(
        body, name=name,
        out_shape=tuple(pltpu.HBM(s.shape, s.dtype) for s in srcs + lands),
        in_specs=[HBM_SPEC] * (2 * n) + [SEM_SPEC, SEM_SPEC, ANY],
        out_specs=tuple([HBM_SPEC] * (2 * n)),
        input_output_aliases={i: i for i in range(2 * n)},
        compiler_params=pltpu.CompilerParams(has_side_effects=pltpu.SideEffectType.DATAFLOW_SIDE_EFFECTING),
    )(*srcs, *lands, send_sems, recv_sems, after)
    return list(res[:n]), list(res[n:])


def _all_reduce_small(vec):
    R = vec.shape[0]

    def body(v_ref, o_ref, sib_buf, chip_buf, send_sems, recv_sems):
        x, y, c, chips = _mesh_place()
        swap = pltpu.make_async_remote_copy(
            src_ref=v_ref, dst_ref=sib_buf, send_sem=send_sems.at[0], recv_sem=recv_sems.at[0],
            device_id=(x, y, 1 - c), device_id_type=MESH_ID)
        swap.start()
        swap.wait()
        my_chip = 2 * x + y
        chip_buf[my_chip] = v_ref[...] + sib_buf[...]
        sends = []
        for j, (px, py) in enumerate(chips):
            cp = pltpu.make_async_remote_copy(
                src_ref=chip_buf.at[my_chip], dst_ref=chip_buf.at[my_chip],
                send_sem=send_sems.at[1 + j], recv_sem=recv_sems.at[1 + j],
                device_id=(px, py, c), device_id_type=MESH_ID)
            cp.start()
            sends.append(cp)
        for j, (px, py) in enumerate(chips):
            pltpu.make_async_remote_copy(
                src_ref=chip_buf.at[2 * px + py], dst_ref=chip_buf.at[2 * px + py],
                send_sem=send_sems.at[1 + j], recv_sem=recv_sems.at[1 + j],
                device_id=(px, py, c), device_id_type=MESH_ID).wait_recv()
        for cp in sends:
            cp.wait_send()
        o_ref[...] = ((chip_buf[0] + chip_buf[1]) + chip_buf[2]) + chip_buf[3]

    vm = pl.BlockSpec(memory_space=pltpu.VMEM)
    return pl.pallas_call(
        body, name="all_reduce_small",
        in_specs=[vm], out_specs=vm,
        out_shape=jax.ShapeDtypeStruct(vec.shape, F32),
        scratch_shapes=[pltpu.VMEM((R, LANES), F32), pltpu.VMEM((4, R, LANES), F32),
                        pltpu.SemaphoreType.DMA((4,)), pltpu.SemaphoreType.DMA((4,))],
    )(vec)


def _adamw_math(w, g, m, v):
    m = ADAM_B1 * m + (1.0 - ADAM_B1) * g
    v = ADAM_B2 * v + (1.0 - ADAM_B2) * (g * g)
    m_hat = m / (1.0 - ADAM_B1 ** ADAM_STEP)
    v_hat = v / (1.0 - ADAM_B2 ** ADAM_STEP)
    delta = -ADAM_LR * (m_hat / (jnp.sqrt(v_hat) + ADAM_EPS) + ADAM_WD * w)
    return delta, m, v


def _adamw(w, m, v, g_own, g_got, chip, name):
    R, C = w.shape
    tr = R if R * C <= 256 * D_MODEL else 256
    assert R % tr == 0
    n_got = g_got.shape[0]

    def body(*refs):
        w_ref, m_ref, v_ref, go_ref = refs[1:5]
        got = refs[5:5 + n_got]
        g_ref, d_ref, nm_ref, nv_ref = refs[5 + n_got:]
        g = go_ref[...]
        for r in got:
            g = g + r[...].astype(F32)
        delta, m_new, v_new = _adamw_math(w_ref[...], g, m_ref[...], v_ref[...])
        g_ref[...] = g
        d_ref[...] = delta
        nm_ref[...] = m_new
        nv_ref[...] = v_new

    blk = pl.BlockSpec((tr, C), lambda i, chip_ref: (i, 0))
    own_spec = pl.BlockSpec((None, tr, C), lambda i, chip_ref: (chip_ref[0], i, 0))
    got_specs = [pl.BlockSpec((None, tr, C), lambda i, chip_ref, j=j: (j, i, 0)) for j in range(n_got)]
    return pl.pallas_call(
        body, name=name,
        grid_spec=pltpu.PrefetchScalarGridSpec(
            num_scalar_prefetch=1, grid=(R // tr,),
            in_specs=[blk] * 3 + [own_spec] + got_specs, out_specs=[blk] * 4),
        out_shape=[jax.ShapeDtypeStruct((R, C), F32)] * 4,
        compiler_params=_cparams("parallel"),
    )(chip, w, m, v, g_own, *([g_got] * n_got))


def _block_diag_pairs(wa, wx):
    def pairs(w):
        w = w.reshape(N_GROUPS, 2, LRU_BW, LRU_BW)
        z = jnp.zeros((N_GROUPS, LRU_BW, LRU_BW), w.dtype)
        top = jnp.concatenate([w[:, 0], z], axis=2)
        bot = jnp.concatenate([z, w[:, 1]], axis=2)
        return jnp.concatenate([top, bot], axis=1)
    return jnp.concatenate([pairs(wa), pairs(wx)], axis=2).astype(BF16)


def _block_diag_unpair(dbd):
    def unpair(g):
        blocks = jnp.stack([g[:, :LRU_BW, :LRU_BW], g[:, LRU_BW:, LRU_BW:]], axis=1)
        return blocks.reshape(LRU_BLOCKS, LRU_BW, LRU_BW)
    return unpair(dbd[:, :, :LANES]), unpair(dbd[:, :, LANES:])


def _local_step(x, target, W, small, late_weights=None, hooks=None):
    def hook(name, *args):
        return hooks[name](*args) if hooks is not None else (None, 0.0)

    S, D = x.shape
    g1, g2, g3 = small["norm_mix_g"], small["norm_mlp_g"], small["norm_final_g"]
    cw, cb = small["conv_w"], small["conv_b"].reshape(1, D)
    ba, bx, lam = (small[k].reshape(1, D) for k in ("lru_ba", "lru_bx", "lru_lambda"))
    fb = jnp.pad(small["forget_b"], (0, LANES - N_HEADS)).reshape(1, LANES)
    bd = _block_diag_pairs(small["lru_wa"], small["lru_wx"])
    big = dict(tm=1024, tn=1024)

    u = _norm_fwd(x, g1, "norm_mix")
    (xg,) = _mm([(u, W["in_xg"])], tks=[D], outs=[F32], name="proj_xg", **big)
    (qkv_t,) = _mm([(W["in_qkv_t"], u)], tb=True, tks=[D], outs=[BF16], name="proj_qkv_t", **big)
    (kv,) = _mm([(u, W["in_kv"])], tks=[D], outs=[BF16], name="proj_kv", **big)
    (gates,) = _mm([(u, W["in_gates"])], tks=[D], outs=[F32], name="proj_gates", **big)
    (fl,) = _mm([(u, W["in_f"])], tks=[D], outs=[F32], name="proj_forget", **big)
    h, yain = _lru_fwd(xg, cw, cb, bd, ba, bx, lam)
    fcum, f_rep = _forget_cumsum(fl, fb)
    f_row = fcum[:, :N_HEADS].T.reshape(N_HEADS, 1, S)
    ob_t, lse = _attn_fwd(kv, qkv_t, f_row, f_rep)
    if late_weights is not None:
        W = {**W, **late_weights(lse)}
    (ya,) = _mm([(yain, W["branch_a"])], tks=[D], outs=[F32], name="branch_a", **big)
    (yb,) = _mm([(ob_t, W["branch_b"])], ta=True, tks=[D], outs=[F32], name="branch_b", **big)
    mix = _gate_mix(gates, ya, yb)
    (x1,) = _mm([(mix, W["out"])], tks=[D], outs=[F32], name="out_proj", extra=(x,),
                epi=lambda acc, res: (res + acc,), **big)
    m = _norm_fwd(x1, g2, "norm_mlp")
    relu, hh = _mm([(m, W["up"])], tks=[D], outs=[BF16, BF16], name="mlp_up",
                   epi=lambda acc: (jnp.maximum(acc, 0.0), jnp.square(jnp.maximum(acc, 0.0))), **big)
    deep = dict(tm=512, tn=1024, tks=[D_FF])
    wgrad = dict(tm=1024, tn=512, tks=[min(4096, S)])
    (x2,) = _mm([(hh, W["down"])], outs=[F32], name="mlp_down", extra=(x1,),
                epi=lambda acc, res: (res + acc,), **deep)
    loss_acc, dg3, dx2, dx2b = _final_norm_loss(x2, target, g3)

    (dhpre,) = _mm([(dx2b, W["down"])], tb=True, tks=[D], outs=[BF16], name="d_mlp_act", extra=(relu,),
                   epi=lambda acc, r: (acc * (2.0 * r.astype(F32)),), **big)
    (dw_down,) = _mm([(hh, dx2b)], ta=True, outs=[F32], name="dw_down", **wgrad)
    (dm,) = _mm([(dhpre, W["up"])], tb=True, outs=[F32], name="d_mlp_in", **deep)
    assert wgrad["tn"] == D_FF // N_DEV
    (dw_up,) = _mm([(m, dhpre)], ta=True, outs=[F32], name="dw_up", col_blocked=True, **wgrad)
    dx1, dx1b, dg2 = _norm_bwd(dm, x1, g2, dx2, "norm_mlp_bwd")
    (dmix,) = _mm([(dx1b, W["out"])], tb=True, tks=[D], outs=[F32], name="d_mix", **big)
    (dw_out,) = _mm([(mix, dx1b)], ta=True, outs=[F32], name="dw_out", **wgrad)
    dya, dyb, dgates = _gate_bwd(dmix, gates, ya, yb)
    (dob_t,) = _mm([(W["branch_b"], dyb)], tb=True, tks=[D], outs=[BF16], name="d_attn_out_t", **big)
    (dw_b,) = _mm([(ob_t, dyb)], outs=[F32], name="dw_branch_b", **wgrad)
    (dyain,) = _mm([(dya, W["branch_a"])], tb=True, tks=[D], outs=[F32], name="d_lru_out", **big)
    (dw_a,) = _mm([(yain, dya)], ta=True, outs=[F32], name="dw_branch_a", **wgrad)
    early = dict(w_branch_a=dw_a, w_branch_b=dw_b, w_out=dw_out, w_up=dw_up, w_down=dw_down)
    early_state, zero = hook("early_start", early)
    dq_t, dk_t, dv_t, dfk, dfq = _attn_bwd(kv, qkv_t, dob_t, ob_t, lse + zero, f_row, f_rep)
    early_state, zero = hook("early_mid", early_state, dfq)
    dF = jnp.pad((dfk.reshape(N_HEADS, S) + dfq.reshape(N_HEADS, S)).T, ((0, 0), (0, LANES - N_HEADS)))
    dfl, dfb = _forget_bwd(dF, fl, fb)
    dxg, dcw, dcb, dba, dbx, dlam, dbd = _lru_bwd(xg, h, dyain, cw, cb, bd, ba, bx, lam + zero)
    dw_in_parts = [
        _mm([(u, dxg)], ta=True, outs=[F32], name="dw_in_xg", **wgrad)[0],
        _mm([(dq_t, u)], outs=[F32], name="dw_in_q_t", **wgrad)[0].T,
        _mm([(dk_t, u)], outs=[F32], name="dw_in_k_t", **wgrad)[0].T,
        _mm([(dv_t, u)], outs=[F32], name="dw_in_v_t", **wgrad)[0].T,
        _mm([(u, dgates)], ta=True, outs=[F32], name="dw_in_gates", **wgrad)[0],
        _mm([(u, dfl)], ta=True, outs=[F32], name="dw_in_forget", **wgrad)[0][:, :N_HEADS],
    ]
    dw_in = jnp.concatenate(dw_in_parts, axis=1)
    in_state, zero = hook("in_start", dw_in)
    wq_t, wk_t, wv_t = (W["in_qkv_t"][D * i:D * (i + 1)] for i in range(3))
    (du_tok,) = _mm([(dxg, W["in_xg"]), (dgates, W["in_gates"]), (dfl, W["in_f"] + jnp.asarray(zero, BF16))],
                    tb=True, tks=[2 * D, 2 * D, LANES], outs=[F32], name="d_norm_mix_out_tok", tm=1024, tn=512)
    in_state, zero = hook("in_mid", in_state, du_tok)
    (du,) = _mm([(dq_t, wq_t + jnp.asarray(zero, BF16)), (dk_t, wk_t), (dv_t, wv_t)], ta=True, tks=[D, D, D],
                outs=[F32], name="d_norm_mix_out", extra=(du_tok,), epi=lambda acc, prev: (prev + acc,),
                tm=1024, tn=512)
    grad_x, _, dg1 = _norm_bwd(du, x, g1, dx1, "norm_mix_bwd")

    dwa, dwx = _block_diag_unpair(dbd)
    big_grads = dict(early, w_in=dw_in)
    small_grads = dict(norm_mix_g=dg1.reshape(D), conv_w=dcw, conv_b=dcb.reshape(D), lru_wa=dwa, lru_ba=dba.reshape(D),
                       lru_wx=dwx, lru_bx=dbx.reshape(D), lru_lambda=dlam.reshape(D), forget_b=dfb[0, :N_HEADS],
                       norm_mlp_g=dg2.reshape(D), norm_final_g=dg3.reshape(D))
    return loss_acc[0, 0], grad_x, big_grads, small_grads, (early_state, in_state)


SMALL_NAMES = ("norm_mix_g", "conv_b", "lru_wa", "lru_ba", "lru_wx", "lru_bx", "lru_lambda", "forget_b",
               "norm_mlp_g", "norm_final_g")
TILE_ELEMS = SUBLANES * LANES


def _pack_small(parts):
    rows = []
    for p in parts:
        flat = p.reshape(-1)
        flat = jnp.pad(flat, (0, (-flat.shape[0]) % TILE_ELEMS))
        rows.append(flat.reshape(-1, LANES))
    return jnp.concatenate(rows, axis=0)


def _packed_rows(shape):
    return -(-math.prod(shape) // TILE_ELEMS) * SUBLANES


def _adamw_small(g_packed, g_conv_w, weights, moms, vels):
    def rows_view(a):
        flat = a.reshape(-1)
        flat = jnp.pad(flat, (0, (-flat.shape[0]) % LANES))
        return flat.reshape(-1, LANES)

    names = SMALL_NAMES + ("conv_w",)
    views = [[rows_view(src[k]) for k in names] for src in (weights, moms, vels)]
    n = len(names)
    starts, r = [], 0
    for k in SMALL_NAMES:
        starts.append(r)
        r += _packed_rows(weights[k].shape)

    def body(*refs):
        gp_ref, gc_ref = refs[0], refs[1]
        w_refs, m_refs, v_refs = refs[2:2 + n], refs[2 + n:2 + 2 * n], refs[2 + 2 * n:2 + 3 * n]
        outs = refs[2 + 3 * n:]
        for i in range(n):
            rows = w_refs[i].shape[0]
            g = gc_ref[...] if i == n - 1 else gp_ref[starts[i]:starts[i] + rows, :]
            delta, m_new, v_new = _adamw_math(w_refs[i][...], g, m_refs[i][...], v_refs[i][...])
            for o_ref, val in zip(outs[4 * i:4 * i + 4], (g, delta, m_new, v_new)):
                o_ref[...] = val

    vm = pl.BlockSpec(memory_space=pltpu.VMEM)
    out_shape = [jax.ShapeDtypeStruct(v.shape, F32) for v in views[0] for _ in range(4)]
    res = pl.pallas_call(
        body, name="adamw_small",
        in_specs=[vm] * (2 + 3 * n), out_specs=[vm] * (4 * n), out_shape=out_shape,
    )(g_packed, g_conv_w, *views[0], *views[1], *views[2])
    dicts = ({}, {}, {}, {})
    for i, k in enumerate(names):
        size = math.prod(weights[k].shape)
        for d, arr in zip(dicts, res[4 * i:4 * i + 4]):
            d[k] = arr.reshape(-1)[:size].reshape(weights[k].shape)
    return dicts


BIG_NAMES = ("w_in", "w_branch_a", "w_branch_b", "w_out", "w_up", "w_down")
WEIGHT_ORDER = ("norm_mix_g", "w_in", "conv_w", "conv_b", "lru_wa", "lru_ba", "lru_wx", "lru_bx", "lru_lambda",
                "forget_b", "w_branch_a", "w_branch_b", "w_out", "norm_mlp_g", "w_up", "w_down", "norm_final_g")


def _to_dest_blocks(name, g):
    if g.ndim == 3:
        return g
    if name in ("w_in", "w_up"):
        return g.reshape(g.shape[0], N_DEV, g.shape[1] // N_DEV).transpose(1, 0, 2)
    return g.reshape(N_DEV, g.shape[0] // N_DEV, g.shape[1])


def kernel(x, norm_mix_g, w_in, conv_w, conv_b, lru_wa, lru_ba, lru_wx, lru_bx, lru_lambda, forget_b, w_branch_a, w_branch_b, w_out, norm_mlp_g, w_up, w_down, norm_final_g, loss_target, m_norm_mix_g, m_w_in, m_conv_w, m_conv_b, m_lru_wa, m_lru_ba, m_lru_wx, m_lru_bx, m_lru_lambda, m_forget_b, m_w_branch_a, m_w_branch_b, m_w_out, m_norm_mlp_g, m_w_up, m_w_down, m_norm_final_g, v_norm_mix_g, v_w_in, v_conv_w, v_conv_b, v_lru_wa, v_lru_ba, v_lru_wx, v_lru_bx, v_lru_lambda, v_forget_b, v_w_branch_a, v_w_branch_b, v_w_out, v_norm_mlp_g, v_w_up, v_w_down, v_norm_final_g):
    weights = dict(norm_mix_g=norm_mix_g, w_in=w_in, conv_w=conv_w, conv_b=conv_b, lru_wa=lru_wa, lru_ba=lru_ba,
                   lru_wx=lru_wx, lru_bx=lru_bx, lru_lambda=lru_lambda, forget_b=forget_b, w_branch_a=w_branch_a,
                   w_branch_b=w_branch_b, w_out=w_out, norm_mlp_g=norm_mlp_g, w_up=w_up, w_down=w_down,
                   norm_final_g=norm_final_g)
    moms = dict(norm_mix_g=m_norm_mix_g, w_in=m_w_in, conv_w=m_conv_w, conv_b=m_conv_b, lru_wa=m_lru_wa,
                lru_ba=m_lru_ba, lru_wx=m_lru_wx, lru_bx=m_lru_bx, lru_lambda=m_lru_lambda, forget_b=m_forget_b,
                w_branch_a=m_w_branch_a, w_branch_b=m_w_branch_b, w_out=m_w_out, norm_mlp_g=m_norm_mlp_g,
                w_up=m_w_up, w_down=m_w_down, norm_final_g=m_norm_final_g)
    vels = dict(norm_mix_g=v_norm_mix_g, w_in=v_w_in, conv_w=v_conv_w, conv_b=v_conv_b, lru_wa=v_lru_wa,
                lru_ba=v_lru_ba, lru_wx=v_lru_wx, lru_bx=v_lru_bx, lru_lambda=v_lru_lambda, forget_b=v_forget_b,
                w_branch_a=v_w_branch_a, w_branch_b=v_w_branch_b, w_out=v_w_out, norm_mlp_g=v_norm_mlp_g,
                w_up=v_w_up, w_down=v_w_down, norm_final_g=v_norm_final_g)
    S, D = x.shape[1], x.shape[2]
    me = 4 * lax.axis_index("x") + 2 * lax.axis_index("y") + lax.axis_index("c")

    core = lax.axis_index("c").astype(jnp.int32).reshape(1)
    chip = (2 * lax.axis_index("x") + lax.axis_index("y")).astype(jnp.int32).reshape(1)
    late_names = BIG_NAMES[1:]

    win_g, cw_g = _all_gather([w_in.astype(BF16), conv_w])
    late_shards = [weights[k].astype(BF16) for k in late_names]
    gather = _exchange_start(late_shards, [jax.ShapeDtypeStruct((N_DEV,) + s.shape, BF16) for s in late_shards],
                             _gather_copies, len(FLIPS), "gather_late_start")
    w_in_full = win_g.transpose(1, 0, 2).reshape(D, -1)
    cuts = (0, 2 * D, 5 * D, 7 * D)
    W = dict(in_xg=w_in_full[:, cuts[0]:cuts[1]], in_qkv_t=w_in_full[:, cuts[1]:cuts[2]].T,
             in_kv=w_in_full[:, cuts[1] + D:cuts[2]], in_gates=w_in_full[:, cuts[2]:cuts[3]],
             in_f=jnp.pad(w_in_full[:, cuts[3]:], ((0, 0), (0, LANES - N_HEADS))))
    small = {k: weights[k] for k in SMALL_NAMES}
    small["conv_w"] = cw_g.transpose(1, 0, 2).reshape(CONV_W, D)
    small["norm_mix_g"] = norm_mix_g + gather[4][0, 0]

    def late_weights(after):
        shards, lands = _exchange_wait(gather, _gather_copies, after, "gather_late_wait")
        wa_g, wb_g, wo_g, wup_g, wdn_g = (
            lax.dynamic_update_slice_in_dim(land, shard[None], me, axis=0) for land, shard in zip(lands, shards))
        return dict(branch_a=wa_g.reshape(D, D), branch_b=wb_g.reshape(D, D), out=wo_g.reshape(D, D),
                    up=wup_g.transpose(1, 0, 2).reshape(D, D_FF), down=wdn_g.reshape(D_FF, D))

    def cores_start(names, grads_by_name, tag):
        blocks = [_to_dest_blocks(k, grads_by_name[k]) for k in names]
        started = _exchange_start(blocks, [jax.ShapeDtypeStruct((4,) + b.shape[1:], F32) for b in blocks],
                                  _cores_copies, 4, "cores_" + tag + "_start")
        return started, started[4][0, 0]

    def chips_start(started, after, tag):
        blocks, got = _exchange_wait(started, _cores_copies, after, "cores_" + tag + "_wait")
        sums = [_chip_partial_sum(b, g, core) for b, g in zip(blocks, got)]
        wire = [s[1] for s in sums]
        scatter = _exchange_start(wire, [jax.ShapeDtypeStruct((3,) + s.shape[1:], BF16) for s in wire],
                                  _scatter_copies, 3, "scatter_" + tag + "_start")
        return (sums, scatter), scatter[4][0, 0]

    hooks = dict(early_start=lambda g: cores_start(late_names, g, "early"),
                 early_mid=lambda st, after: chips_start(st, after, "early"),
                 in_start=lambda g: cores_start(BIG_NAMES[:1], dict(w_in=g), "w_in"),
                 in_mid=lambda st, after: chips_start(st, after, "w_in"))
    loss_part, grad_x, _, small_grads, ((early_sums, early_scatter), (in_sums, in_scatter)) = _local_step(
        x.reshape(S, D), loss_target.reshape(S, D), W, small, late_weights, hooks)
    loss = lax.psum(loss_part, MESH_AXES)
    _, early_others = _exchange_wait(early_scatter, _scatter_copies, grad_x, "scatter_early_wait")
    _, in_others = _exchange_wait(in_scatter, _scatter_copies, grad_x, "scatter_w_in_wait")
    sums = list(in_sums) + list(early_sums)
    others = list(in_others) + list(early_others)

    reduced = _all_reduce_small(_pack_small([small_grads[k] for k in SMALL_NAMES] + [small_grads["conv_w"]]))
    cw_full = reduced[reduced.shape[0] - _packed_rows((CONV_W, D)):].reshape(CONV_W, D)
    cw_cols = lax.dynamic_slice_in_dim(cw_full, me * (D // N_DEV), D // N_DEV, axis=1)

    grads, deltas, new_m, new_v = _adamw_small(reduced, cw_cols, weights, moms, vels)
    for k, s, g_got in zip(BIG_NAMES, sums, others):
        grads[k], deltas[k], new_m[k], new_v[k] = _adamw(weights[k], moms[k], vels[k], s[0], g_got, chip, "adamw_" + k)

    return (loss, grad_x.reshape(1, S, D), *[grads[k] for k in WEIGHT_ORDER], *[deltas[k] for k in WEIGHT_ORDER],
            *[new_m[k] for k in WEIGHT_ORDER], *[new_v[k] for k in WEIGHT_ORDER])
```

```python
import functools
import math

import jax
import jax.numpy as jnp
from jax import lax
from jax.experimental import pallas as pl
from jax.experimental.pallas import tpu as pltpu

F32 = jnp.float32
BF16 = jnp.bfloat16

D_MODEL = 1024
N_HEADS = 8
HEAD_DIM = 128
D_FF = 4096
LRU_BLOCKS = 16
LRU_BW = 64
LRU_C = 8.0
CONV_W = 4
RMS_EPS = 1e-6
N_DEV = 8
LANES = 128
SUBLANES = 8
N_GROUPS = D_MODEL // LANES
VMEM_LIMIT_BYTES = 52 * 1024 * 1024
ATTN_SCALE = 1.0 / math.sqrt(HEAD_DIM)
LOG2E = math.log2(math.e)
NEG_BIG = -1e30
ADAM_LR = 0.001
ADAM_B1 = 0.9
ADAM_B2 = 0.999
ADAM_EPS = 1e-08
ADAM_WD = 0.01
ADAM_STEP = 10
ATTN_BLOCK = 1024
ATTN_Q_TILES = 4
ATTN_FWD_BLOCK = 2048
ATTN_FWD_Q_TILES = 2
ATTN_STRIP = 256
LRU_CHUNK = 256
ROW_TILE = 512
MESH_AXES = ("x", "y", "c")
MESH_ID = pl.DeviceIdType.MESH
ANY = pl.BlockSpec(memory_space=pl.ANY)

NT_DIMS = (((1,), (1,)), ((), ()))
TN_DIMS = (((0,), (0,)), ((), ()))
NN_DIMS = (((1,), (0,)), ((), ()))


def _cparams(*sem):
    return pltpu.CompilerParams(dimension_semantics=sem if sem else None, vmem_limit_bytes=VMEM_LIMIT_BYTES)


def _sigmoid(x):
    return 0.5 * (jnp.tanh(0.5 * x) + 1.0)


def _log1p_pos(e):
    u = 1.0 + e
    return jnp.where(u == 1.0, e, jnp.log(u) * (e / (u - 1.0)))


def _softplus(z):
    return jnp.maximum(z, 0.0) + _log1p_pos(jnp.exp(-jnp.abs(z)))


def _expm1_neg(x):
    series = x * (1.0 + x * 0.5 * (1.0 + x * (1.0 / 3.0) * (1.0 + x * 0.25)))
    return jnp.where(x > -0.03, series, jnp.exp(x) - 1.0)


GELU_C = math.sqrt(2.0 / math.pi)
GELU_K = 0.044715


def _gelu(x):
    return 0.5 * x * (1.0 + jnp.tanh(GELU_C * (x + GELU_K * (x * x * x))))


def _gelu_and_grad(x):
    t = jnp.tanh(GELU_C * (x + GELU_K * (x * x * x)))
    g = 0.5 * x * (1.0 + t)
    dg = 0.5 * (1.0 + t) + 0.5 * x * (1.0 - t * t) * (GELU_C * (1.0 + 3.0 * GELU_K * (x * x)))
    return g, dg


def _mm(pairs, *, ta=False, tb=False, tm, tn, tks, outs, name, epi=None, extra=(), rows=(), col_blocked=False):
    n_pairs, n_extra, n_out = len(pairs), len(extra), len(outs)
    tas = list(ta) if isinstance(ta, (list, tuple)) else [ta] * n_pairs
    tbs = list(tb) if isinstance(tb, (list, tuple)) else [tb] * n_pairs
    a0, b0 = pairs[0]
    M = a0.shape[1] if tas[0] else a0.shape[0]
    N = b0.shape[0] if tbs[0] else b0.shape[1]
    tm, tn = min(tm, M), min(tn, N)
    nks, offs = [], []
    for (a, b), tk, pta in zip(pairs, tks, tas):
        K = a.shape[0] if pta else a.shape[1]
        assert K % tk == 0 and M % tm == 0 and N % tn == 0
        offs.append(sum(nks))
        nks.append(K // tk)
    nk_total = sum(nks)
    dims = [(((0 if pta else 1,), (1 if ptb else 0,)), ((), ())) for pta, ptb in zip(tas, tbs)]

    def kmap(off, nk):
        return lambda k: jnp.clip(k - off, 0, nk - 1)

    in_specs, operands = [], []
    for (a, b), tk, off, nk, pta, ptb in zip(pairs, tks, offs, nks, tas, tbs):
        km = kmap(off, nk)
        if pta:
            in_specs.append(pl.BlockSpec((tk, tm), lambda i, j, k, km=km: (km(k), i)))
        else:
            in_specs.append(pl.BlockSpec((tm, tk), lambda i, j, k, km=km: (i, km(k))))
        if ptb:
            in_specs.append(pl.BlockSpec((tn, tk), lambda i, j, k, km=km: (j, km(k))))
        else:
            in_specs.append(pl.BlockSpec((tk, tn), lambda i, j, k, km=km: (km(k), j)))
        operands += [a, b]
    for e in extra:
        in_specs.append(pl.BlockSpec((tm, tn), lambda i, j, k: (i, j)))
        operands.append(e)
    for r in rows:
        in_specs.append(pl.BlockSpec((1, tn), lambda i, j, k: (0, j)))
        operands.append(r)
    n_extra += len(rows)

    def body(*refs):
        ab = refs[:2 * n_pairs]
        ex = refs[2 * n_pairs:2 * n_pairs + n_extra]
        o = refs[2 * n_pairs + n_extra:2 * n_pairs + n_extra + n_out]
        k = pl.program_id(2)

        def finish(acc):
            res = epi(acc, *[e[...] for e in ex]) if epi is not None else (acc,)
            for r, oref in zip(res, o):
                oref[...] = r.astype(oref.dtype)

        if nk_total == 1:
            finish(lax.dot_general(ab[0][...], ab[1][...], dims[0], preferred_element_type=F32))
            return
        acc = refs[-1]
        for p in range(n_pairs):
            a_ref, b_ref = ab[2 * p], ab[2 * p + 1]

            @pl.when((k >= offs[p]) & (k < offs[p] + nks[p]))
            def _(a_ref=a_ref, b_ref=b_ref, pdims=dims[p]):
                prod = lax.dot_general(a_ref[...], b_ref[...], pdims, preferred_element_type=F32)

                @pl.when(k == 0)
                def _():
                    acc[...] = prod

                @pl.when(k > 0)
                def _():
                    acc[...] += prod

        @pl.when(k == nk_total - 1)
        def _():
            finish(acc[...])

    return pl.pallas_call(
        body,
        name=name,
        grid=(M // tm, N // tn, nk_total),
        in_specs=in_specs,
        out_specs=[pl.BlockSpec((None, tm, tn), lambda i, j, k: (j, i, 0)) if col_blocked
                   else pl.BlockSpec((tm, tn), lambda i, j, k: (i, j)) for _ in outs],
        out_shape=[jax.ShapeDtypeStruct((N // tn, M, tn) if col_blocked else (M, N), dt) for dt in outs],
        scratch_shapes=[] if nk_total == 1 else [pltpu.VMEM((tm, tn), F32)],
        compiler_params=_cparams("parallel", "parallel", "arbitrary"),
    )(*operands)


def _norm_fwd(x, g, name):
    S, D = x.shape
    tr = min(ROW_TILE, S)

    def body(x_ref, g_ref, o_ref):
        xv = x_ref[...]
        r = lax.rsqrt(jnp.mean(xv * xv, axis=-1, keepdims=True) + RMS_EPS)
        o_ref[...] = ((xv * r) * g_ref[...]).astype(o_ref.dtype)

    return pl.pallas_call(
        body, name=name, grid=(S // tr,),
        in_specs=[pl.BlockSpec((tr, D), lambda i: (i, 0)), pl.BlockSpec((1, D), lambda i: (0, 0))],
        out_specs=pl.BlockSpec((tr, D), lambda i: (i, 0)),
        out_shape=jax.ShapeDtypeStruct((S, D), BF16),
        compiler_params=_cparams("parallel"),
    )(x, g.reshape(1, D))


def _rms_bwd_rows(dy, xv, g):
    r = lax.rsqrt(jnp.mean(xv * xv, axis=-1, keepdims=True) + RMS_EPS)
    xn = xv * r
    dxn = dy * g
    dx = r * (dxn - xn * jnp.mean(dxn * xn, axis=-1, keepdims=True))
    dg = jnp.sum(dy * xn, axis=0, keepdims=True)
    return dx, dg


def _norm_bwd(dy, x, g, dres, name):
    S, D = x.shape
    tr = min(ROW_TILE, S)

    def body(dy_ref, x_ref, g_ref, dres_ref, dx_ref, dxb_ref, dg_ref):
        dx, dg = _rms_bwd_rows(dy_ref[...], x_ref[...], g_ref[...])
        dx = dres_ref[...] + dx
        dx_ref[...] = dx
        dxb_ref[...] = dx.astype(BF16)

        @pl.when(pl.program_id(0) == 0)
        def _():
            dg_ref[...] = jnp.zeros_like(dg_ref)

        dg_ref[...] += dg

    row = pl.BlockSpec((tr, D), lambda i: (i, 0))
    vec = pl.BlockSpec((1, D), lambda i: (0, 0))
    return pl.pallas_call(
        body, name=name, grid=(S // tr,),
        in_specs=[row, row, vec, row],
        out_specs=[row, row, vec],
        out_shape=[jax.ShapeDtypeStruct((S, D), F32), jax.ShapeDtypeStruct((S, D), BF16),
                   jax.ShapeDtypeStruct((1, D), F32)],
        compiler_params=_cparams("arbitrary"),
    )(dy, x, g.reshape(1, D), dres)


def _final_norm_loss(x2, target, g):
    S, D = x2.shape
    tr = min(ROW_TILE, S)

    def body(x_ref, t_ref, g_ref, loss_ref, dg_ref, dx_ref, dxb_ref):
        xv = x_ref[...]
        gv = g_ref[...]
        r = lax.rsqrt(jnp.mean(xv * xv, axis=-1, keepdims=True) + RMS_EPS)
        y = (xv * r) * gv
        err = y - t_ref[...]
        part = 0.5 * jnp.sum(jnp.mean(err * err, axis=-1, keepdims=True), axis=0, keepdims=True)
        dy = err * (1.0 / D)
        dx, dg = _rms_bwd_rows(dy, xv, gv)
        dx_ref[...] = dx
        dxb_ref[...] = dx.astype(BF16)

        @pl.when(pl.program_id(0) == 0)
        def _():
            dg_ref[...] = jnp.zeros_like(dg_ref)
            loss_ref[...] = jnp.zeros_like(loss_ref)

        dg_ref[...] += dg
        loss_ref[...] += jnp.broadcast_to(part, loss_ref.shape)

    row = pl.BlockSpec((tr, D), lambda i: (i, 0))
    vec = pl.BlockSpec((1, D), lambda i: (0, 0))
    return pl.pallas_call(
        body, name="final_norm_loss", grid=(S // tr,),
        in_specs=[row, row, vec],
        out_specs=[pl.BlockSpec((SUBLANES, LANES), lambda i: (0, 0)), vec, row, row],
        out_shape=[jax.ShapeDtypeStruct((SUBLANES, LANES), F32), jax.ShapeDtypeStruct((1, D), F32),
                   jax.ShapeDtypeStruct((S, D), F32), jax.ShapeDtypeStruct((S, D), BF16)],
        compiler_params=_cparams("arbitrary"),
    )(x2, target, g.reshape(1, D))


def _lru_gates(xa, bd_j, ba_j, bx_j, sp_j):
    z = jnp.dot(xa.astype(BF16), bd_j, preferred_element_type=F32)
    r = _sigmoid(z[:, :LANES] + ba_j)
    ig = _sigmoid(z[:, LANES:] + bx_j)
    log_a = (-LRU_C) * r * sp_j
    a = jnp.exp(log_a)
    mult = jnp.sqrt(-_expm1_neg(2.0 * log_a))
    return r, ig, a, mult


def _conv_rows(xpad, cw_ref, cb_ref, sl, tc):
    out = jnp.broadcast_to(cb_ref[:, sl], (tc, LANES))
    for k in range(CONV_W):
        out = out + xpad[pl.ds(SUBLANES - (CONV_W - 1) + k, tc), sl] * cw_ref[k:k + 1, sl]
    return out


def _lru_fwd(xg, cw, cb, bd, ba, bx, lam):
    S = xg.shape[0]
    D = D_MODEL
    tc = min(LRU_CHUNK, S)
    hb = tc // SUBLANES

    def body(xl_ref, halo_ref, g_ref, cw_ref, cb_ref, bd_ref, ba_ref, bx_ref, lam_ref,
             h_ref, y_ref, xpad, a_s, b_s, carry):
        i = pl.program_id(0)

        @pl.when(i == 0)
        def _():
            carry[...] = jnp.zeros_like(carry)

        xpad[0:SUBLANES, :] = jnp.where(i > 0, halo_ref[...], 0.0)
        xpad[SUBLANES:, :] = xl_ref[...]
        for j in range(N_GROUPS):
            sl = slice(LANES * j, LANES * (j + 1))
            xa = _conv_rows(xpad, cw_ref, cb_ref, sl, tc)
            sp = _softplus(-lam_ref[:, sl])
            _, ig, a, mult = _lru_gates(xa, bd_ref[j], ba_ref[:, sl], bx_ref[:, sl], sp)
            a_s[:, sl] = a
            b_s[:, sl] = mult * (ig * xa)

        row = lax.broadcasted_iota(jnp.int32, (SUBLANES, D), 0)

        def step(t, c):
            o = pl.multiple_of(t * SUBLANES, SUBLANES)
            A = a_s[pl.ds(o, SUBLANES), :]
            B = b_s[pl.ds(o, SUBLANES), :]
            for d in (1, 2, 4):
                keep = row >= d
                a_sh = jnp.where(keep, pltpu.roll(A, d, 0), 1.0)
                b_sh = jnp.where(keep, pltpu.roll(B, d, 0), 0.0)
                B = A * b_sh + B
                A = A * a_sh
            hh = A * c + B
            h_ref[pl.ds(o, SUBLANES), :] = hh
            return jnp.broadcast_to(hh[SUBLANES - 1:SUBLANES, :], (SUBLANES, D))

        carry[...] = lax.fori_loop(0, hb, step, carry[...])
        y_ref[...] = (_gelu(g_ref[...]) * h_ref[...]).astype(BF16)

    row_spec = lambda col: pl.BlockSpec((tc, D), lambda i, col=col: (i, col))
    halo = pl.BlockSpec((SUBLANES, D), lambda i: (jnp.maximum(i * hb - 1, 0), 0))
    full = lambda shape: pl.BlockSpec(shape, lambda i: tuple(0 for _ in shape))
    return pl.pallas_call(
        body, name="lru_fwd", grid=(S // tc,),
        in_specs=[row_spec(0), halo, row_spec(1), full((CONV_W, D)), full((1, D)),
                  full((N_GROUPS, LANES, 2 * LANES)), full((1, D)), full((1, D)), full((1, D))],
        out_specs=[pl.BlockSpec((tc, D), lambda i: (i, 0)), pl.BlockSpec((tc, D), lambda i: (i, 0))],
        out_shape=[jax.ShapeDtypeStruct((S, D), F32), jax.ShapeDtypeStruct((S, D), BF16)],
        scratch_shapes=[pltpu.VMEM((tc + SUBLANES, D), F32), pltpu.VMEM((tc, D), F32),
                        pltpu.VMEM((tc, D), F32), pltpu.VMEM((SUBLANES, D), F32)],
        compiler_params=_cparams("arbitrary"),
    )(xg, xg, xg, cw, cb, bd, ba, bx, lam)


def _lru_bwd(xg, h, dyain, cw, cb, bd, ba, bx, lam):
    S = xg.shape[0]
    D = D_MODEL
    tc = min(LRU_CHUNK, S)
    hb = tc // SUBLANES
    nc = S // tc

    def body(xl_ref, xhalo_ref, g_ref, h_ref, hhalo_ref, dy_ref, cw_ref, cb_ref, bd_ref, ba_ref, bx_ref,
             lam_ref, dxg_ref, dcw_ref, dcb_ref, dba_ref, dbx_ref, dlam_ref, dbd_ref,
             xpad, hpad, a_s, b_s, dh_s, g_s, xa_s, r_s, ig_s, m_s, dxa_pad, carry_e, dxa_head):
        i = pl.program_id(0)
        c = nc - 1 - i

        @pl.when(i == 0)
        def _():
            carry_e[...] = jnp.zeros_like(carry_e)
            dxa_head[...] = jnp.zeros_like(dxa_head)
            for ref in (dcw_ref, dcb_ref, dba_ref, dbx_ref, dlam_ref, dbd_ref):
                ref[...] = jnp.zeros_like(ref)

        xpad[0:SUBLANES, :] = jnp.where(c > 0, xhalo_ref[...], 0.0)
        xpad[SUBLANES:, :] = xl_ref[...]
        hpad[0:SUBLANES, :] = jnp.where(c > 0, hhalo_ref[...], 0.0)
        hpad[SUBLANES:, :] = h_ref[...]

        for j in range(N_GROUPS):
            sl = slice(LANES * j, LANES * (j + 1))
            xa = _conv_rows(xpad, cw_ref, cb_ref, sl, tc)
            sp = _softplus(-lam_ref[:, sl])
            r, ig, a, mult = _lru_gates(xa, bd_ref[j], ba_ref[:, sl], bx_ref[:, sl], sp)
            gl, dgl = _gelu_and_grad(g_ref[:, sl])
            dy = dy_ref[:, sl]
            dh = dy * gl
            dxg_ref[:, D + LANES * j:D + LANES * (j + 1)] = (dy * h_ref[:, sl] * dgl).astype(BF16)
            a_s[:, sl] = a
            b_s[:, sl] = a * dh
            dh_s[:, sl] = dh
            xa_s[:, sl] = xa
            r_s[:, sl] = r
            ig_s[:, sl] = ig
            m_s[:, sl] = mult

        row = lax.broadcasted_iota(jnp.int32, (SUBLANES, D), 0)

        def step(tt, ce):
            o = pl.multiple_of((hb - 1 - tt) * SUBLANES, SUBLANES)
            A = a_s[pl.ds(o, SUBLANES), :]
            B = b_s[pl.ds(o, SUBLANES), :]
            for d in (1, 2, 4):
                keep = row < SUBLANES - d
                a_sh = jnp.where(keep, pltpu.roll(A, SUBLANES - d, 0), 1.0)
                b_sh = jnp.where(keep, pltpu.roll(B, SUBLANES - d, 0), 0.0)
                B = A * b_sh + B
                A = A * a_sh
            e = A * ce + B
            e_next = jnp.where(row < SUBLANES - 1, pltpu.roll(e, SUBLANES - 1, 0), ce)
            g_s[pl.ds(o, SUBLANES), :] = dh_s[pl.ds(o, SUBLANES), :] + e_next
            return jnp.broadcast_to(e[0:1, :], (SUBLANES, D))

        carry_e[...] = lax.fori_loop(0, hb, step, carry_e[...])

        for j in range(N_GROUPS):
            sl = slice(LANES * j, LANES * (j + 1))
            gg = g_s[:, sl]
            xa, r, ig, mult, a = xa_s[:, sl], r_s[:, sl], ig_s[:, sl], m_s[:, sl], a_s[:, sl]
            hprev = hpad[pl.ds(SUBLANES - 1, tc), sl]
            sp = _softplus(-lam_ref[:, sl])
            da = gg * hprev
            dmult = gg * (ig * xa)
            dig = gg * (mult * xa)
            dxa = gg * (mult * ig)
            dla = da * a - dmult * ((a * a) / mult)
            dr = dla * ((-LRU_C) * sp)
            dlam_ref[:, sl] += jnp.sum(dla * r, axis=0, keepdims=True)
            dza = dr * r * (1.0 - r)
            dzx = dig * ig * (1.0 - ig)
            dba_ref[:, sl] += jnp.sum(dza, axis=0, keepdims=True)
            dbx_ref[:, sl] += jnp.sum(dzx, axis=0, keepdims=True)
            dz = jnp.concatenate([dza, dzx], axis=1).astype(BF16)
            dbd_ref[j] += lax.dot_general(xa.astype(BF16), dz, TN_DIMS, preferred_element_type=F32)
            dxa = dxa + lax.dot_general(dz, bd_ref[j], NT_DIMS, preferred_element_type=F32)
            dxa_pad[0:tc, sl] = dxa

        dxa_pad[tc:, :] = dxa_head[...]
        dxa_head[...] = dxa_pad[0:SUBLANES, :]

        for j in range(N_GROUPS):
            sl = slice(LANES * j, LANES * (j + 1))
            dxa = dxa_pad[0:tc, sl]
            dxl = jnp.zeros((tc, LANES), F32)
            for k in range(CONV_W):
                dxl = dxl + dxa_pad[pl.ds(CONV_W - 1 - k, tc), sl] * cw_ref[k:k + 1, sl]
                dcw_ref[k:k + 1, sl] += jnp.sum(
                    dxa * xpad[pl.ds(SUBLANES - (CONV_W - 1) + k, tc), sl], axis=0, keepdims=True)
            dxg_ref[:, sl] = dxl.astype(BF16)
            dcb_ref[:, sl] += jnp.sum(dxa, axis=0, keepdims=True)

        @pl.when(i == nc - 1)
        def _():
            dlam_ref[...] = dlam_ref[...] * (LRU_C * _sigmoid(-lam_ref[...]))

    rev = lambda col: pl.BlockSpec((tc, D), lambda i, col=col: (nc - 1 - i, col))
    halo = pl.BlockSpec((SUBLANES, D), lambda i: (jnp.maximum((nc - 1 - i) * hb - 1, 0), 0))
    full = lambda shape: pl.BlockSpec(shape, lambda i: tuple(0 for _ in shape))
    big = lambda: pltpu.VMEM((tc, D), F32)
    return pl.pallas_call(
        body, name="lru_bwd", grid=(nc,),
        in_specs=[rev(0), halo, rev(1), rev(0), halo, rev(0), full((CONV_W, D)), full((1, D)),
                  full((N_GROUPS, LANES, 2 * LANES)), full((1, D)), full((1, D)), full((1, D))],
        out_specs=[pl.BlockSpec((tc, 2 * D), lambda i: (nc - 1 - i, 0)), full((CONV_W, D)), full((1, D)),
                   full((1, D)), full((1, D)), full((1, D)), full((N_GROUPS, LANES, 2 * LANES))],
        out_shape=[jax.ShapeDtypeStruct((S, 2 * D), BF16), jax.ShapeDtypeStruct((CONV_W, D), F32),
                   jax.ShapeDtypeStruct((1, D), F32), jax.ShapeDtypeStruct((1, D), F32),
                   jax.ShapeDtypeStruct((1, D), F32), jax.ShapeDtypeStruct((1, D), F32),
                   jax.ShapeDtypeStruct((N_GROUPS, LANES, 2 * LANES), F32)],
        scratch_shapes=[pltpu.VMEM((tc + SUBLANES, D), F32), pltpu.VMEM((tc + SUBLANES, D), F32),
                        big(), big(), big(), big(), big(), big(), big(), big(),
                        pltpu.VMEM((tc + SUBLANES, D), F32), pltpu.VMEM((SUBLANES, D), F32),
                        pltpu.VMEM((SUBLANES, D), F32)],
        compiler_params=_cparams("arbitrary"),
    )(xg, xg, xg, h, h, dyain, cw, cb, bd, ba, bx, lam)


def _forget_cumsum(fl, fb):
    S = fl.shape[0]
    tr = min(ROW_TILE, S)
    hb = tr // SUBLANES

    def body(fl_ref, fb_ref, o_ref, rep_ref, lf_s, carry):
        @pl.when(pl.program_id(0) == 0)
        def _():
            carry[...] = jnp.zeros_like(carry)

        lf_s[...] = -_softplus(-(fl_ref[...] + fb_ref[...]))
        row = lax.broadcasted_iota(jnp.int32, (SUBLANES, LANES), 0)

        def step(t, c):
            o = pl.multiple_of(t * SUBLANES, SUBLANES)
            B = lf_s[pl.ds(o, SUBLANES), :]
            for d in (1, 2, 4):
                B = B + jnp.where(row >= d, pltpu.roll(B, d, 0), 0.0)
            B = B + c
            o_ref[pl.ds(o, SUBLANES), :] = B * LOG2E
            return jnp.broadcast_to(B[SUBLANES - 1:SUBLANES, :], (SUBLANES, LANES))

        carry[...] = lax.fori_loop(0, hb, step, carry[...])
        for h in range(N_HEADS):
            rep_ref[h] = jnp.broadcast_to(o_ref[:, h:h + 1], (tr, LANES))

    return pl.pallas_call(
        body, name="forget_cumsum", grid=(S // tr,),
        in_specs=[pl.BlockSpec((tr, LANES), lambda i: (i, 0)), pl.BlockSpec((1, LANES), lambda i: (0, 0))],
        out_specs=[pl.BlockSpec((tr, LANES), lambda i: (i, 0)),
                   pl.BlockSpec((N_HEADS, tr, LANES), lambda i: (0, i, 0))],
        out_shape=[jax.ShapeDtypeStruct((S, LANES), F32), jax.ShapeDtypeStruct((N_HEADS, S, LANES), F32)],
        scratch_shapes=[pltpu.VMEM((tr, LANES), F32), pltpu.VMEM((SUBLANES, LANES), F32)],
        compiler_params=_cparams("arbitrary"),
    )(fl, fb)


def _forget_bwd(dF, fl, fb):
    S = fl.shape[0]
    tr = min(ROW_TILE, S)
    hb = tr // SUBLANES
    nc = S // tr

    def body(df_ref, fl_ref, fb_ref, o_ref, dfb_ref, carry):
        @pl.when(pl.program_id(0) == 0)
        def _():
            carry[...] = jnp.zeros_like(carry)
            dfb_ref[...] = jnp.zeros_like(dfb_ref)

        row = lax.broadcasted_iota(jnp.int32, (SUBLANES, LANES), 0)

        def step(tt, carried):
            c, acc = carried
            o = pl.multiple_of((hb - 1 - tt) * SUBLANES, SUBLANES)
            B = df_ref[pl.ds(o, SUBLANES), :]
            for d in (1, 2, 4):
                B = B + jnp.where(row < SUBLANES - d, pltpu.roll(B, SUBLANES - d, 0), 0.0)
            B = B + c
            z = fl_ref[pl.ds(o, SUBLANES), :] + fb_ref[...]
            dz = B * _sigmoid(-z)
            o_ref[pl.ds(o, SUBLANES), :] = dz.astype(BF16)
            return jnp.broadcast_to(B[0:1, :], (SUBLANES, LANES)), acc + dz

        c, acc = lax.fori_loop(0, hb, step, (carry[...], jnp.zeros((SUBLANES, LANES), F32)))
        carry[...] = c
        dfb_ref[...] += jnp.sum(acc, axis=0, keepdims=True)

    rev = pl.BlockSpec((tr, LANES), lambda i: (nc - 1 - i, 0))
    vec = pl.BlockSpec((1, LANES), lambda i: (0, 0))
    return pl.pallas_call(
        body, name="forget_bwd", grid=(nc,),
        in_specs=[rev, rev, vec],
        out_specs=[rev, vec],
        out_shape=[jax.ShapeDtypeStruct((S, LANES), BF16), jax.ShapeDtypeStruct((1, LANES), F32)],
        scratch_shapes=[pltpu.VMEM((SUBLANES, LANES), F32)],
        compiler_params=_cparams("arbitrary"),
    )(dF, fl, fb)


def _triangle(n, qw, key_major):
    pairs = [(q, k) for q in range(n) for k in range(qw * (q + 1))]
    if key_major:
        pairs.sort(key=lambda qk: (qk[1], qk[0]))
    return (jnp.asarray([q for q, _ in pairs], jnp.int32), jnp.asarray([k for _, k in pairs], jnp.int32))


def _strip_plan(bk, bq, strip, rel):
    plan = []
    for j in range(bq // strip):
        if rel is None:
            plan.append((j, bk, None))
            continue
        reach = strip * (j + 1) - rel * bk
        if reach > 0:
            plan.append((j, min(reach, bk), strip * j - rel * bk if reach <= bk else None))
    return plan


def _strip_scores(k_ref, qt_ref, fk_ref, strip, j, nkeys, mask_off):
    cols = slice(strip * j, strip * (j + 1))
    s = jnp.dot(k_ref[0:nkeys, :], qt_ref[:, cols], preferred_element_type=F32) * (ATTN_SCALE * LOG2E)
    fk = fk_ref[0:nkeys, :]
    s = s - jnp.concatenate([fk] * (strip // LANES), axis=1)
    keep = None
    if mask_off is not None:
        keys = lax.broadcasted_iota(jnp.int32, (nkeys, strip), 0)
        queries = lax.broadcasted_iota(jnp.int32, (nkeys, strip), 1) + mask_off
        keep = keys <= queries
    return s, keep


def _attn_fwd(kv, qkv_t, f_row, f_rep):
    S = kv.shape[0]
    bk = min(ATTN_FWD_BLOCK, S)
    strip = min(ATTN_STRIP, bk)
    qw = min(ATTN_FWD_Q_TILES, S // bk)
    bq = qw * bk
    tri_q, tri_k = _triangle(S // bq, qw, key_major=False)
    ones_rows = 2 * SUBLANES

    def body(tq_ref, tk_ref, k_ref, qt_ref, vt_ref, fq_ref, fk_ref, ot_ref, lse_ref, m_s, acc_s, vta_s):
        t = pl.program_id(1)
        qi, ki = tq_ref[t], tk_ref[t]
        rel = ki - qw * qi

        @pl.when(ki == 0)
        def _():
            m_s[...] = jnp.full_like(m_s, NEG_BIG)
            acc_s[...] = jnp.zeros_like(acc_s)

        vta_s[0:HEAD_DIM, :] = vt_ref[...]
        vta_s[HEAD_DIM:, :] = jnp.ones((ones_rows, bk), BF16)

        def update(plan):
            scores = lambda entry: _strip_scores(k_ref, qt_ref, fk_ref, strip, *entry)

            def weighted_values(j, nkeys, alpha, pb):
                cols = slice(strip * j, strip * (j + 1))
                acc_s[:, cols] = alpha * acc_s[:, cols] + jnp.dot(
                    vta_s[:, 0:nkeys], pb, preferred_element_type=F32)

            ahead, behind = scores(plan[0]), None
            for i, (j, nkeys, mask_off) in enumerate(plan):
                cols = slice(strip * j, strip * (j + 1))
                (s, keep), ahead = ahead, (scores(plan[i + 1]) if i + 1 < len(plan) else None)
                if behind is not None:
                    weighted_values(*behind)
                if keep is not None:
                    s = jnp.where(keep, s, NEG_BIG)
                fq = fq_ref[:, cols]
                m_old = m_s[:, cols]
                m_new = jnp.maximum(m_old, jnp.max(s, axis=0, keepdims=True) + fq)
                p = jnp.exp2(s - (m_new - fq))
                behind = (j, nkeys, jnp.exp2(m_old - m_new), p.astype(BF16))
                m_s[:, cols] = m_new
            weighted_values(*behind)

        @pl.when(rel < 0)
        def _():
            update(_strip_plan(bk, bq, strip, None))

        for d in range(qw):
            @pl.when(rel == d)
            def _(d=d):
                update(_strip_plan(bk, bq, strip, d))
                if d == qw - 1:
                    denom = acc_s[HEAD_DIM:HEAD_DIM + 1, :]
                    ot_ref[...] = (acc_s[0:HEAD_DIM, :] / denom).astype(BF16)
                    lse_ref[...] = m_s[...] + jnp.log2(denom)

    return pl.pallas_call(
        body, name="attn_fwd",
        grid_spec=pltpu.PrefetchScalarGridSpec(
            num_scalar_prefetch=2, grid=(N_HEADS, tri_q.shape[0]),
            in_specs=[pl.BlockSpec((bk, HEAD_DIM), lambda h, t, tq, tk: (tk[t], h)),
                      pl.BlockSpec((HEAD_DIM, bq), lambda h, t, tq, tk: (h, tq[t])),
                      pl.BlockSpec((HEAD_DIM, bk), lambda h, t, tq, tk: (2 * N_HEADS + h, tk[t])),
                      pl.BlockSpec((None, 1, bq), lambda h, t, tq, tk: (h, 0, tq[t])),
                      pl.BlockSpec((None, bk, LANES), lambda h, t, tq, tk: (h, tk[t], 0))],
            out_specs=[pl.BlockSpec((HEAD_DIM, bq), lambda h, t, tq, tk: (h, tq[t])),
                       pl.BlockSpec((None, 1, bq), lambda h, t, tq, tk: (h, 0, tq[t]))],
            scratch_shapes=[pltpu.VMEM((1, bq), F32), pltpu.VMEM((HEAD_DIM + ones_rows, bq), F32),
                            pltpu.VMEM((HEAD_DIM + ones_rows, bk), BF16)]),
        out_shape=[jax.ShapeDtypeStruct((N_HEADS * HEAD_DIM, S), BF16), jax.ShapeDtypeStruct((N_HEADS, 1, S), F32)],
        compiler_params=_cparams("parallel", "arbitrary"),
    )(tri_q, tri_k, kv, qkv_t, qkv_t, f_row, f_rep)


def _attn_bwd(kv, qkv_t, do_t, o_t, lse, f_row, f_rep):
    S = kv.shape[0]
    bk = min(ATTN_BLOCK, S)
    strip = min(ATTN_STRIP, bk)
    qw = min(ATTN_Q_TILES, S // bk)
    bq = qw * bk
    nq = S // bq
    tri_q, tri_k = _triangle(nq, qw, key_major=True)
    n_tiles = tri_q.shape[0]

    def body(tq_ref, tk_ref, k_ref, v_ref, qt_ref, kt_ref, dot_ref, ot_ref, lse_ref, fq_ref, fk_ref,
             dqt_ref, dkt_ref, dvt_ref, dfk_ref, dfq_ref, dq_s, dk_s, dv_s, dfk_s, dfq_s, row_s):
        t = pl.program_id(1)
        qi, ki = tq_ref[t], tk_ref[t]
        rel = ki - qw * qi

        @pl.when(t == 0)
        def _():
            dq_s[...] = jnp.zeros_like(dq_s)
            dfq_s[...] = jnp.zeros_like(dfq_s)

        @pl.when(rel >= 0)
        def _():
            dk_s[...] = jnp.zeros_like(dk_s)
            dv_s[...] = jnp.zeros_like(dv_s)
            dfk_s[...] = jnp.zeros_like(dfk_s)

        def update(plan):
            row_s[...] = fq_ref[...] - lse_ref[...]

            def matmuls_in(j, nkeys, mask_off):
                s, keep = _strip_scores(k_ref, qt_ref, fk_ref, strip, j, nkeys, mask_off)
                dp = jnp.dot(v_ref[0:nkeys, :], dot_ref[:, strip * j:strip * (j + 1)], preferred_element_type=F32)
                return s, keep, dp

            def matmuls_out(j, nkeys, pb, dsb):
                cols = slice(strip * j, strip * (j + 1))
                dv_s[:, 0:nkeys] += lax.dot_general(dot_ref[:, cols], pb, NT_DIMS, preferred_element_type=F32)
                dk_s[:, 0:nkeys] += lax.dot_general(qt_ref[:, cols], dsb, NT_DIMS, preferred_element_type=F32)
                dq_s[qi, :, cols] += jnp.dot(kt_ref[:, 0:nkeys], dsb, preferred_element_type=F32)

            ahead, behind = matmuls_in(*plan[0]), None
            for i, (j, nkeys, mask_off) in enumerate(plan):
                cols = slice(strip * j, strip * (j + 1))
                (s, keep, dp), ahead = ahead, (matmuls_in(*plan[i + 1]) if i + 1 < len(plan) else None)
                if behind is not None:
                    matmuls_out(*behind)
                p = jnp.exp2(s + row_s[:, cols])
                if keep is not None:
                    p = jnp.where(keep, p, 0.0)
                dot = dot_ref[:, cols]
                delta = jnp.sum(dot.astype(F32) * ot_ref[:, cols].astype(F32), axis=0, keepdims=True)
                ds = p * (dp - delta)
                behind = (j, nkeys, p.astype(BF16), ds.astype(BF16))
                lane_part = ds[:, 0:LANES]
                for g in range(1, strip // LANES):
                    lane_part = lane_part + ds[:, LANES * g:LANES * (g + 1)]
                dfk_s[0:nkeys, :] += lane_part
                sub_part = ds[0:SUBLANES, :]
                for g in range(1, nkeys // SUBLANES):
                    sub_part = sub_part + ds[SUBLANES * g:SUBLANES * (g + 1), :]
                dfq_s[qi, :, cols] += sub_part
            matmuls_out(*behind)

        @pl.when(rel < 0)
        def _():
            update(_strip_plan(bk, bq, strip, None))

        for d in range(qw):
            @pl.when(rel == d)
            def _(d=d):
                update(_strip_plan(bk, bq, strip, d))

        @pl.when(qi == nq - 1)
        def _():
            dkt_ref[...] = (dk_s[...] * ATTN_SCALE).astype(BF16)
            dvt_ref[...] = dv_s[...].astype(BF16)
            dfk_ref[...] = -jnp.sum(dfk_s[...].T, axis=0, keepdims=True)

        @pl.when(t == n_tiles - 1)
        def _():
            for j in range(nq):
                dqt_ref[:, bq * j:bq * (j + 1)] = (dq_s[j] * ATTN_SCALE).astype(BF16)
                dfq_ref[:, bq * j:bq * (j + 1)] = jnp.sum(dfq_s[j], axis=0, keepdims=True)

    q_feat = pl.BlockSpec((HEAD_DIM, bq), lambda h, t, tq, tk: (h, tq[t]))
    q_row = pl.BlockSpec((None, 1, bq), lambda h, t, tq, tk: (h, 0, tq[t]))
    k_feat = pl.BlockSpec((HEAD_DIM, bk), lambda h, t, tq, tk: (h, tk[t]))
    return pl.pallas_call(
        body, name="attn_bwd",
        grid_spec=pltpu.PrefetchScalarGridSpec(
            num_scalar_prefetch=2, grid=(N_HEADS, n_tiles),
            in_specs=[pl.BlockSpec((bk, HEAD_DIM), lambda h, t, tq, tk: (tk[t], h)),
                      pl.BlockSpec((bk, HEAD_DIM), lambda h, t, tq, tk: (tk[t], N_HEADS + h)),
                      q_feat,
                      pl.BlockSpec((HEAD_DIM, bk), lambda h, t, tq, tk: (N_HEADS + h, tk[t])),
                      q_feat, q_feat, q_row, q_row,
                      pl.BlockSpec((None, bk, LANES), lambda h, t, tq, tk: (h, tk[t], 0))],
            out_specs=[pl.BlockSpec((HEAD_DIM, S), lambda h, t, tq, tk: (h, 0)), k_feat, k_feat,
                       pl.BlockSpec((None, 1, bk), lambda h, t, tq, tk: (h, 0, tk[t])),
                       pl.BlockSpec((None, 1, S), lambda h, t, tq, tk: (h, 0, 0))],
            scratch_shapes=[pltpu.VMEM((nq, HEAD_DIM, bq), F32), pltpu.VMEM((HEAD_DIM, bk), F32),
                            pltpu.VMEM((HEAD_DIM, bk), F32), pltpu.VMEM((bk, LANES), F32),
                            pltpu.VMEM((nq, SUBLANES, bq), F32), pltpu.VMEM((1, bq), F32)]),
        out_shape=[jax.ShapeDtypeStruct((N_HEADS * HEAD_DIM, S), BF16)] * 3
        + [jax.ShapeDtypeStruct((N_HEADS, 1, S), F32), jax.ShapeDtypeStruct((N_HEADS, 1, S), F32)],
        compiler_params=_cparams("parallel", "arbitrary"),
    )(tri_q, tri_k, kv, kv, qkv_t, qkv_t, do_t, o_t, lse, f_row, f_rep)


def _gate_mix(gates, ya, yb):
    S, D = ya.shape
    tr = min(ROW_TILE, S)

    def body(ga_ref, gb_ref, ya_ref, yb_ref, o_ref):
        o_ref[...] = (_sigmoid(ga_ref[...]) * ya_ref[...] + _sigmoid(gb_ref[...]) * yb_ref[...]).astype(BF16)

    col = lambda j: pl.BlockSpec((tr, D), lambda i, j=j: (i, j))
    return pl.pallas_call(
        body, name="gate_mix", grid=(S // tr,),
        in_specs=[col(0), col(1), col(0), col(0)],
        out_specs=col(0),
        out_shape=jax.ShapeDtypeStruct((S, D), BF16),
        compiler_params=_cparams("parallel"),
    )(gates, gates, ya, yb)


def _gate_bwd(dmix, gates, ya, yb):
    S, D = ya.shape
    tr = min(ROW_TILE, S)

    def body(dm_ref, ga_ref, gb_ref, ya_ref, yb_ref, dya_ref, dyb_ref, dg_ref):
        dm = dm_ref[...]
        sa, sb = _sigmoid(ga_ref[...]), _sigmoid(gb_ref[...])
        dya_ref[...] = (dm * sa).astype(BF16)
        dyb_ref[...] = (dm * sb).astype(BF16)
        dg_ref[:, 0:D] = ((dm * ya_ref[...]) * (sa * (1.0 - sa))).astype(BF16)
        dg_ref[:, D:] = ((dm * yb_ref[...]) * (sb * (1.0 - sb))).astype(BF16)

    col = lambda j: pl.BlockSpec((tr, D), lambda i, j=j: (i, j))
    return pl.pallas_call(
        body, name="gate_bwd", grid=(S // tr,),
        in_specs=[col(0), col(0), col(1), col(0), col(0)],
        out_specs=[col(0), col(0), pl.BlockSpec((tr, 2 * D), lambda i: (i, 0))],
        out_shape=[jax.ShapeDtypeStruct((S, D), BF16), jax.ShapeDtypeStruct((S, D), BF16),
                   jax.ShapeDtypeStruct((S, 2 * D), BF16)],
        compiler_params=_cparams("parallel"),
    )(dmix, gates, gates, ya, yb)


def _mesh_place():
    x, y, c = lax.axis_index("x"), lax.axis_index("y"), lax.axis_index("c")
    chips = [(1 - x, y), (x, 1 - y), (1 - x, 1 - y)]
    return x, y, c, chips


def _all_gather(shards):
    n = len(shards)

    def body(*refs):
        ins, outs = refs[:n], refs[n:2 * n]
        send_sems, recv_sems, local_sems = refs[2 * n:]
        x, y, c, chips = _mesh_place()
        me, sib = (x, y, c), (x, y, 1 - c)

        def copy(a, k, block, to, src=None):
            px, py, pc = block
            dst = outs[a].at[4 * px + 2 * py + pc]
            return pltpu.make_async_remote_copy(
                src_ref=dst if src is None else src, dst_ref=dst,
                send_sem=send_sems.at[a, k], recv_sem=recv_sems.at[a, k],
                device_id=to, device_id_type=MESH_ID)

        mine = [pltpu.make_async_copy(ins[a], outs[a].at[4 * x + 2 * y + c], local_sems.at[a]) for a in range(n)]
        for cp in mine:
            cp.start()
        first = []
        for a in range(n):
            first.append(copy(a, 0, me, sib, src=ins[a]))
            for j, chip in enumerate(chips):
                first.append(copy(a, 1 + j, me, (*chip, c), src=ins[a]))
        for cp in first:
            cp.start()
        passed = []
        for j, chip in enumerate(chips):
            for a in range(n):
                copy(a, 1 + j, (*chip, c), me).wait_recv()
                fwd = copy(a, 4 + j, (*chip, c), sib)
                fwd.start()
                passed.append(fwd)
        for a in range(n):
            copy(a, 0, sib, me).wait_recv()
            for j, chip in enumerate(chips):
                copy(a, 4 + j, (*chip, 1 - c), me).wait_recv()
        for cp in first + passed:
            cp.wait_send()
        for cp in mine:
            cp.wait()

    return pl.pallas_call(
        body, name="all_gather_weights",
        in_specs=[ANY] * n, out_specs=[ANY] * n,
        out_shape=[jax.ShapeDtypeStruct((N_DEV,) + s.shape, s.dtype) for s in shards],
        scratch_shapes=[pltpu.SemaphoreType.DMA((n, 7)), pltpu.SemaphoreType.DMA((n, 7)),
                        pltpu.SemaphoreType.DMA((n,))],
    )(*shards)


def _chip_partial_sum(blocks, got, core):
    R, C = got.shape[1:]
    tr = min(256, R)
    assert R % tr == 0

    def body(core_ref, a_ref, b_ref, s_ref, sb_ref):
        s = a_ref[...] + b_ref[...]
        s_ref[...] = s
        sb_ref[...] = s.astype(BF16)

    blk = pl.BlockSpec((None, tr, C), lambda k, i, core_ref: (k, i, 0))
    return pl.pallas_call(
        body, name="chip_partial_sum",
        grid_spec=pltpu.PrefetchScalarGridSpec(
            num_scalar_prefetch=1, grid=(4, R // tr),
            in_specs=[pl.BlockSpec((None, tr, C), lambda k, i, core_ref: (2 * k + core_ref[0], i, 0)), blk],
            out_specs=[blk, blk]),
        out_shape=[jax.ShapeDtypeStruct(got.shape, F32), jax.ShapeDtypeStruct(got.shape, BF16)],
        compiler_params=_cparams("parallel", "parallel"),
    )(core, blocks, got)


HBM_SPEC = pl.BlockSpec(memory_space=pltpu.HBM)
SEM_SPEC = pl.BlockSpec(memory_space=pltpu.SEMAPHORE)
FLIPS = [(dx, dy, dc) for dx in (0, 1) for dy in (0, 1) for dc in (0, 1) if (dx, dy, dc) != (0, 0, 0)]


def _flip(v, d):
    return 1 - v if d else v


def _gather_copies(srcs, lands, send_sems, recv_sems):
    x, y, c, _ = _mesh_place()
    sends, recvs = [], []
    for a in range(len(srcs)):
        for k, (dx, dy, dc) in enumerate(FLIPS):
            px, py, pc = _flip(x, dx), _flip(y, dy), _flip(c, dc)
            sem = len(FLIPS) * a + k
            common = dict(send_sem=send_sems.at[sem], recv_sem=recv_sems.at[sem],
                          device_id=(px, py, pc), device_id_type=MESH_ID)
            sends.append(pltpu.make_async_remote_copy(
                src_ref=srcs[a], dst_ref=lands[a].at[4 * x + 2 * y + c], **common))
            recvs.append(pltpu.make_async_remote_copy(
                src_ref=srcs[a], dst_ref=lands[a].at[4 * px + 2 * py + pc], **common))
    return sends, recvs


def _cores_copies(srcs, lands, send_sems, recv_sems):
    x, y, c, _ = _mesh_place()
    copies = []
    for a in range(len(srcs)):
        for k in range(4):
            copies.append(pltpu.make_async_remote_copy(
                src_ref=srcs[a].at[2 * k + (1 - c)], dst_ref=lands[a].at[k],
                send_sem=send_sems.at[4 * a + k], recv_sem=recv_sems.at[4 * a + k],
                device_id=(x, y, 1 - c), device_id_type=MESH_ID))
    return copies, copies


def _scatter_copies(srcs, lands, send_sems, recv_sems):
    x, y, c, chips = _mesh_place()
    sends = []
    for a in range(len(srcs)):
        for j, (px, py) in enumerate(chips):
            sends.append(pltpu.make_async_remote_copy(
                src_ref=srcs[a].at[2 * px + py], dst_ref=lands[a].at[j],
                send_sem=send_sems.at[3 * a + j], recv_sem=recv_sems.at[3 * a + j],
                device_id=(px, py, c), device_id_type=MESH_ID))
    return sends, sends


def _exchange_start(srcs, land_shapes, copies, n_copies, name):
    n = len(srcs)

    def body(*refs):
        src_refs, land_refs = refs[:n], refs[n:2 * n]
        send_sems, recv_sems = refs[2 * n], refs[2 * n + 1]
        token = refs[-1]
        sends, _ = copies(src_refs, land_refs, send_sems, recv_sems)
        for cp in sends:
            cp.start()
        token[...] = jnp.zeros_like(token)

    lands = [pltpu.with_memory_space_constraint(lax.empty(s.shape, s.dtype), pltpu.HBM) for s in land_shapes]
    srcs = [pltpu.with_memory_space_constraint(s, pltpu.HBM) for s in srcs]
    res = pl.pallas_call(
        body, name=name,
        out_shape=(pltpu.SemaphoreType.DMA((n * n_copies,)), pltpu.SemaphoreType.DMA((n * n_copies,)),
                   *[pltpu.HBM(s.shape, s.dtype) for s in srcs], *[pltpu.HBM(s.shape, s.dtype) for s in land_shapes],
                   jax.ShapeDtypeStruct((SUBLANES, LANES), F32)),
        in_specs=[HBM_SPEC] * (2 * n),
        out_specs=(SEM_SPEC, SEM_SPEC, *[HBM_SPEC] * (2 * n), pl.BlockSpec(memory_space=pltpu.VMEM)),
        input_output_aliases={i: 2 + i for i in range(2 * n)},
        compiler_params=pltpu.CompilerParams(has_side_effects=pltpu.SideEffectType.DATAFLOW_SIDE_EFFECTING),
    )(*srcs, *lands)
    return res[0], res[1], list(res[2:2 + n]), list(res[2 + n:2 + 2 * n]), res[-1]


def _exchange_wait(started, copies, after, name):
    send_sems, recv_sems, srcs, lands, _ = started
    n = len(srcs)

    def body(*refs):
        src_refs, land_refs = refs[:n], refs[n:2 * n]
        send_ref, recv_ref = refs[2 * n], refs[2 * n + 1]
        sends, recvs = copies(src_refs, land_refs, send_ref, recv_ref)
        for cp in sends:
            cp.wait_send()
        for cp in recvs:
            cp.wait_recv()

    res = pl.pallas_call(
        body, name=name,
        out_shape=tuple(pltpu.HBM(s.shape, s.dtype) for s in srcs + lands),
        in_specs=[HBM_SPEC] * (2 * n) + [SEM_SPEC, SEM_SPEC, ANY],
        out_specs=tuple([HBM_SPEC] * (2 * n)),
        input_output_aliases={i: i for i in range(2 * n)},
        compiler_params=pltpu.CompilerParams(has_side_effects=pltpu.SideEffectType.DATAFLOW_SIDE_EFFECTING),
    )(*srcs, *lands, send_sems, recv_sems, after)
    return list(res[:n]), list(res[n:])


def _all_reduce_small(vec):
    R = vec.shape[0]

    def body(v_ref, o_ref, sib_buf, chip_buf, send_sems, recv_sems):
        x, y, c, chips = _mesh_place()
        swap = pltpu.make_async_remote_copy(
            src_ref=v_ref, dst_ref=sib_buf, send_sem=send_sems.at[0], recv_sem=recv_sems.at[0],
            device_id=(x, y, 1 - c), device_id_type=MESH_ID)
        swap.start()
        swap.wait()
        my_chip = 2 * x + y
        chip_buf[my_chip] = v_ref[...] + sib_buf[...]
        sends = []
        for j, (px, py) in enumerate(chips):
            cp = pltpu.make_async_remote_copy(
                src_ref=chip_buf.at[my_chip], dst_ref=chip_buf.at[my_chip],
                send_sem=send_sems.at[1 + j], recv_sem=recv_sems.at[1 + j],
                device_id=(px, py, c), device_id_type=MESH_ID)
            cp.start()
            sends.append(cp)
        for j, (px, py) in enumerate(chips):
            pltpu.make_async_remote_copy(
                src_ref=chip_buf.at[2 * px + py], dst_ref=chip_buf.at[2 * px + py],
                send_sem=send_sems.at[1 + j], recv_sem=recv_sems.at[1 + j],
                device_id=(px, py, c), device_id_type=MESH_ID).wait_recv()
        for cp in sends:
            cp.wait_send()
        o_ref[...] = ((chip_buf[0] + chip_buf[1]) + chip_buf[2]) + chip_buf[3]

    vm = pl.BlockSpec(memory_space=pltpu.VMEM)
    return pl.pallas_call(
        body, name="all_reduce_small",
        in_specs=[vm], out_specs=vm,
        out_shape=jax.ShapeDtypeStruct(vec.shape, F32),
        scratch_shapes=[pltpu.VMEM((R, LANES), F32), pltpu.VMEM((4, R, LANES), F32),
                        pltpu.SemaphoreType.DMA((4,)), pltpu.SemaphoreType.DMA((4,))],
    )(vec)


def _adamw_math(w, g, m, v):
    m = ADAM_B1 * m + (1.0 - ADAM_B1) * g
    v = ADAM_B2 * v + (1.0 - ADAM_B2) * (g * g)
    m_hat = m / (1.0 - ADAM_B1 ** ADAM_STEP)
    v_hat = v / (1.0 - ADAM_B2 ** ADAM_STEP)
    delta = -ADAM_LR * (m_hat / (jnp.sqrt(v_hat) + ADAM_EPS) + ADAM_WD * w)
    return delta, m, v


def _adamw(w, m, v, g_own, g_got, chip, name):
    R, C = w.shape
    tr = R if R * C <= 256 * D_MODEL else 256
    assert R % tr == 0
    n_got = g_got.shape[0]

    def body(*refs):
        w_ref, m_ref, v_ref, go_ref = refs[1:5]
        got = refs[5:5 + n_got]
        g_ref, d_ref, nm_ref, nv_ref = refs[5 + n_got:]
        g = go_ref[...]
        for r in got:
            g = g + r[...].astype(F32)
        delta, m_new, v_new = _adamw_math(w_ref[...], g, m_ref[...], v_ref[...])
        g_ref[...] = g
        d_ref[...] = delta
        nm_ref[...] = m_new
        nv_ref[...] = v_new

    blk = pl.BlockSpec((tr, C), lambda i, chip_ref: (i, 0))
    own_spec = pl.BlockSpec((None, tr, C), lambda i, chip_ref: (chip_ref[0], i, 0))
    got_specs = [pl.BlockSpec((None, tr, C), lambda i, chip_ref, j=j: (j, i, 0)) for j in range(n_got)]
    return pl.pallas_call(
        body, name=name,
        grid_spec=pltpu.PrefetchScalarGridSpec(
            num_scalar_prefetch=1, grid=(R // tr,),
            in_specs=[blk] * 3 + [own_spec] + got_specs, out_specs=[blk] * 4),
        out_shape=[jax.ShapeDtypeStruct((R, C), F32)] * 4,
        compiler_params=_cparams("parallel"),
    )(chip, w, m, v, g_own, *([g_got] * n_got))


def _block_diag_pairs(wa, wx):
    def pairs(w):
        w = w.reshape(N_GROUPS, 2, LRU_BW, LRU_BW)
        z = jnp.zeros((N_GROUPS, LRU_BW, LRU_BW), w.dtype)
        top = jnp.concatenate([w[:, 0], z], axis=2)
        bot = jnp.concatenate([z, w[:, 1]], axis=2)
        return jnp.concatenate([top, bot], axis=1)
    return jnp.concatenate([pairs(wa), pairs(wx)], axis=2).astype(BF16)


def _block_diag_unpair(dbd):
    def unpair(g):
        blocks = jnp.stack([g[:, :LRU_BW, :LRU_BW], g[:, LRU_BW:, LRU_BW:]], axis=1)
        return blocks.reshape(LRU_BLOCKS, LRU_BW, LRU_BW)
    return unpair(dbd[:, :, :LANES]), unpair(dbd[:, :, LANES:])


def _local_step(x, target, W, small, late_weights=None, hooks=None):
    def hook(name, *args):
        return hooks[name](*args) if hooks is not None else (None, 0.0)

    S, D = x.shape
    g1, g2, g3 = small["norm_mix_g"], small["norm_mlp_g"], small["norm_final_g"]
    cw, cb = small["conv_w"], small["conv_b"].reshape(1, D)
    ba, bx, lam = (small[k].reshape(1, D) for k in ("lru_ba", "lru_bx", "lru_lambda"))
    fb = jnp.pad(small["forget_b"], (0, LANES - N_HEADS)).reshape(1, LANES)
    bd = _block_diag_pairs(small["lru_wa"], small["lru_wx"])
    big = dict(tm=1024, tn=1024)

    u = _norm_fwd(x, g1, "norm_mix")
    (xg,) = _mm([(u, W["in_xg"])], tks=[D], outs=[F32], name="proj_xg", **big)
    (qkv_t,) = _mm([(W["in_qkv_t"], u)], tb=True, tks=[D], outs=[BF16], name="proj_qkv_t", **big)
    (kv,) = _mm([(u, W["in_kv"])], tks=[D], outs=[BF16], name="proj_kv", **big)
    (gates,) = _mm([(u, W["in_gates"])], tks=[D], outs=[F32], name="proj_gates", **big)
    (fl,) = _mm([(u, W["in_f"])], tks=[D], outs=[F32], name="proj_forget", **big)
    h, yain = _lru_fwd(xg, cw, cb, bd, ba, bx, lam)
    fcum, f_rep = _forget_cumsum(fl, fb)
    f_row = fcum[:, :N_HEADS].T.reshape(N_HEADS, 1, S)
    ob_t, lse = _attn_fwd(kv, qkv_t, f_row, f_rep)
    if late_weights is not None:
        W = {**W, **late_weights(lse)}
    (ya,) = _mm([(yain, W["branch_a"])], tks=[D], outs=[F32], name="branch_a", **big)
    (yb,) = _mm([(ob_t, W["branch_b"])], ta=True, tks=[D], outs=[F32], name="branch_b", **big)
    mix = _gate_mix(gates, ya, yb)
    def out_and_norm(acc, res, gain):
        x1_tile = res + acc
        r = lax.rsqrt(jnp.mean(x1_tile * x1_tile, axis=-1, keepdims=True) + RMS_EPS)
        return x1_tile, (x1_tile * r) * gain

    assert big["tn"] == D
    x1, m = _mm([(mix, W["out"])], tks=[D], outs=[F32, BF16], name="out_proj", extra=(x,), rows=(g2.reshape(1, D),),
                epi=out_and_norm, **big)
    relu, hh = _mm([(m, W["up"])], tks=[D], outs=[BF16, BF16], name="mlp_up",
                   epi=lambda acc: (jnp.maximum(acc, 0.0), jnp.square(jnp.maximum(acc, 0.0))), **big)
    deep = dict(tm=512, tn=1024, tks=[D_FF])
    wgrad = dict(tm=1024, tn=512, tks=[min(4096, S)])
    (x2,) = _mm([(hh, W["down"])], outs=[F32], name="mlp_down", extra=(x1,),
                epi=lambda acc, res: (res + acc,), **deep)
    loss_acc, dg3, dx2, dx2b = _final_norm_loss(x2, target, g3)

    (dhpre,) = _mm([(dx2b, W["down"])], tb=True, tks=[D], outs=[BF16], name="d_mlp_act", extra=(relu,),
                   epi=lambda acc, r: (acc * (2.0 * r.astype(F32)),), **big)
    (dw_down,) = _mm([(hh, dx2b)], ta=True, outs=[F32], name="dw_down", **wgrad)
    (dm,) = _mm([(dhpre, W["up"])], tb=True, outs=[F32], name="d_mlp_in", **deep)
    assert wgrad["tn"] == D_FF // N_DEV
    (dw_up,) = _mm([(m, dhpre)], ta=True, outs=[F32], name="dw_up", col_blocked=True, **wgrad)
    dx1, dx1b, dg2 = _norm_bwd(dm, x1, g2, dx2, "norm_mlp_bwd")
    (dmix,) = _mm([(dx1b, W["out"])], tb=True, tks=[D], outs=[F32], name="d_mix", **big)
    (dw_out,) = _mm([(mix, dx1b)], ta=True, outs=[F32], name="dw_out", **wgrad)
    dya, dyb, dgates = _gate_bwd(dmix, gates, ya, yb)
    (dob_t,) = _mm([(W["branch_b"], dyb)], tb=True, tks=[D], outs=[BF16], name="d_attn_out_t", **big)
    (dw_b,) = _mm([(ob_t, dyb)], outs=[F32], name="dw_branch_b", **wgrad)
    (dyain,) = _mm([(dya, W["branch_a"])], tb=True, tks=[D], outs=[F32], name="d_lru_out", **big)
    (dw_a,) = _mm([(yain, dya)], ta=True, outs=[F32], name="dw_branch_a", **wgrad)
    early = dict(w_branch_a=dw_a, w_branch_b=dw_b, w_out=dw_out, w_up=dw_up, w_down=dw_down)
    early_state, zero = hook("early_start", early)
    dq_t, dk_t, dv_t, dfk, dfq = _attn_bwd(kv, qkv_t, dob_t, ob_t, lse + zero, f_row, f_rep)
    early_state, zero = hook("early_mid", early_state, dfq)
    dF = jnp.pad((dfk.reshape(N_HEADS, S) + dfq.reshape(N_HEADS, S)).T, ((0, 0), (0, LANES - N_HEADS)))
    dfl, dfb = _forget_bwd(dF, fl, fb)
    dxg, dcw, dcb, dba, dbx, dlam, dbd = _lru_bwd(xg, h, dyain, cw, cb, bd, ba, bx, lam + zero)
    dw_in_parts = [
        _mm([(u, dxg)], ta=True, outs=[F32], name="dw_in_xg", **wgrad)[0],
        _mm([(dq_t, u)], outs=[F32], name="dw_in_q_t", **wgrad)[0].T,
        _mm([(dk_t, u)], outs=[F32], name="dw_in_k_t", **wgrad)[0].T,
        _mm([(dv_t, u)], outs=[F32], name="dw_in_v_t", **wgrad)[0].T,
        _mm([(u, dgates)], ta=True, outs=[F32], name="dw_in_gates", **wgrad)[0],
        _mm([(u, dfl)], ta=True, outs=[F32], name="dw_in_forget", **wgrad)[0][:, :N_HEADS],
    ]
    dw_in = jnp.concatenate(dw_in_parts, axis=1)
    in_state, zero = hook("in_start", dw_in)
    wq_t, wk_t, wv_t = (W["in_qkv_t"][D * i:D * (i + 1)] for i in range(3))
    (du_tok,) = _mm([(dxg, W["in_xg"]), (dgates, W["in_gates"]), (dfl, W["in_f"] + jnp.asarray(zero, BF16))],
                    tb=True, tks=[2 * D, 2 * D, LANES], outs=[F32], name="d_norm_mix_out_tok", tm=1024, tn=512)
    in_state, zero = hook("in_mid", in_state, du_tok)
    (du,) = _mm([(dq_t, wq_t + jnp.asarray(zero, BF16)), (dk_t, wk_t), (dv_t, wv_t)], ta=True, tks=[D, D, D],
                outs=[F32], name="d_norm_mix_out", extra=(du_tok,), epi=lambda acc, prev: (prev + acc,),
                tm=1024, tn=512)
    grad_x, _, dg1 = _norm_bwd(du, x, g1, dx1, "norm_mix_bwd")

    dwa, dwx = _block_diag_unpair(dbd)
    big_grads = dict(early, w_in=dw_in)
    small_grads = dict(norm_mix_g=dg1.reshape(D), conv_w=dcw, conv_b=dcb.reshape(D), lru_wa=dwa, lru_ba=dba.reshape(D),
                       lru_wx=dwx, lru_bx=dbx.reshape(D), lru_lambda=dlam.reshape(D), forget_b=dfb[0, :N_HEADS],
                       norm_mlp_g=dg2.reshape(D), norm_final_g=dg3.reshape(D))
    return loss_acc[0, 0], grad_x, big_grads, small_grads, (early_state, in_state)


SMALL_NAMES = ("norm_mix_g", "conv_b", "lru_wa", "lru_ba", "lru_wx", "lru_bx", "lru_lambda", "forget_b",
               "norm_mlp_g", "norm_final_g")
TILE_ELEMS = SUBLANES * LANES


def _pack_small(parts):
    rows = []
    for p in parts:
        flat = p.reshape(-1)
        flat = jnp.pad(flat, (0, (-flat.shape[0]) % TILE_ELEMS))
        rows.append(flat.reshape(-1, LANES))
    return jnp.concatenate(rows, axis=0)


def _packed_rows(shape):
    return -(-math.prod(shape) // TILE_ELEMS) * SUBLANES


def _adamw_small(g_packed, g_conv_w, weights, moms, vels):
    def rows_view(a):
        flat = a.reshape(-1)
        flat = jnp.pad(flat, (0, (-flat.shape[0]) % LANES))
        return flat.reshape(-1, LANES)

    names = SMALL_NAMES + ("conv_w",)
    views = [[rows_view(src[k]) for k in names] for src in (weights, moms, vels)]
    n = len(names)
    starts, r = [], 0
    for k in SMALL_NAMES:
        starts.append(r)
        r += _packed_rows(weights[k].shape)

    def body(*refs):
        gp_ref, gc_ref = refs[0], refs[1]
        w_refs, m_refs, v_refs = refs[2:2 + n], refs[2 + n:2 + 2 * n], refs[2 + 2 * n:2 + 3 * n]
        outs = refs[2 + 3 * n:]
        for i in range(n):
            rows = w_refs[i].shape[0]
            g = gc_ref[...] if i == n - 1 else gp_ref[starts[i]:starts[i] + rows, :]
            delta, m_new, v_new = _adamw_math(w_refs[i][...], g, m_refs[i][...], v_refs[i][...])
            for o_ref, val in zip(outs[4 * i:4 * i + 4], (g, delta, m_new, v_new)):
                o_ref[...] = val

    vm = pl.BlockSpec(memory_space=pltpu.VMEM)
    out_shape = [jax.ShapeDtypeStruct(v.shape, F32) for v in views[0] for _ in range(4)]
    res = pl.pallas_call(
        body, name="adamw_small",
        in_specs=[vm] * (2 + 3 * n), out_specs=[vm] * (4 * n), out_shape=out_shape,
    )(g_packed, g_conv_w, *views[0], *views[1], *views[2])
    dicts = ({}, {}, {}, {})
    for i, k in enumerate(names):
        size = math.prod(weights[k].shape)
        for d, arr in zip(dicts, res[4 * i:4 * i + 4]):
            d[k] = arr.reshape(-1)[:size].reshape(weights[k].shape)
    return dicts


BIG_NAMES = ("w_in", "w_branch_a", "w_branch_b", "w_out", "w_up", "w_down")
WEIGHT_ORDER = ("norm_mix_g", "w_in", "conv_w", "conv_b", "lru_wa", "lru_ba", "lru_wx", "lru_bx", "lru_lambda",
                "forget_b", "w_branch_a", "w_branch_b", "w_out", "norm_mlp_g", "w_up", "w_down", "norm_final_g")


def _to_dest_blocks(name, g):
    if g.ndim == 3:
        return g
    if name in ("w_in", "w_up"):
        return g.reshape(g.shape[0], N_DEV, g.shape[1] // N_DEV).transpose(1, 0, 2)
    return g.reshape(N_DEV, g.shape[0] // N_DEV, g.shape[1])


def kernel(x, norm_mix_g, w_in, conv_w, conv_b, lru_wa, lru_ba, lru_wx, lru_bx, lru_lambda, forget_b, w_branch_a, w_branch_b, w_out, norm_mlp_g, w_up, w_down, norm_final_g, loss_target, m_norm_mix_g, m_w_in, m_conv_w, m_conv_b, m_lru_wa, m_lru_ba, m_lru_wx, m_lru_bx, m_lru_lambda, m_forget_b, m_w_branch_a, m_w_branch_b, m_w_out, m_norm_mlp_g, m_w_up, m_w_down, m_norm_final_g, v_norm_mix_g, v_w_in, v_conv_w, v_conv_b, v_lru_wa, v_lru_ba, v_lru_wx, v_lru_bx, v_lru_lambda, v_forget_b, v_w_branch_a, v_w_branch_b, v_w_out, v_norm_mlp_g, v_w_up, v_w_down, v_norm_final_g):
    weights = dict(norm_mix_g=norm_mix_g, w_in=w_in, conv_w=conv_w, conv_b=conv_b, lru_wa=lru_wa, lru_ba=lru_ba,
                   lru_wx=lru_wx, lru_bx=lru_bx, lru_lambda=lru_lambda, forget_b=forget_b, w_branch_a=w_branch_a,
                   w_branch_b=w_branch_b, w_out=w_out, norm_mlp_g=norm_mlp_g, w_up=w_up, w_down=w_down,
                   norm_final_g=norm_final_g)
    moms = dict(norm_mix_g=m_norm_mix_g, w_in=m_w_in, conv_w=m_conv_w, conv_b=m_conv_b, lru_wa=m_lru_wa,
                lru_ba=m_lru_ba, lru_wx=m_lru_wx, lru_bx=m_lru_bx, lru_lambda=m_lru_lambda, forget_b=m_forget_b,
                w_branch_a=m_w_branch_a, w_branch_b=m_w_branch_b, w_out=m_w_out, norm_mlp_g=m_norm_mlp_g,
                w_up=m_w_up, w_down=m_w_down, norm_final_g=m_norm_final_g)
    vels = dict(norm_mix_g=v_norm_mix_g, w_in=v_w_in, conv_w=v_conv_w, conv_b=v_conv_b, lru_wa=v_lru_wa,
                lru_ba=v_lru_ba, lru_wx=v_lru_wx, lru_bx=v_lru_bx, lru_lambda=v_lru_lambda, forget_b=v_forget_b,
                w_branch_a=v_w_branch_a, w_branch_b=v_w_branch_b, w_out=v_w_out, norm_mlp_g=v_norm_mlp_g,
                w_up=v_w_up, w_down=v_w_down, norm_final_g=v_norm_final_g)
    S, D = x.shape[1], x.shape[2]
    me = 4 * lax.axis_index("x") + 2 * lax.axis_index("y") + lax.axis_index("c")

    core = lax.axis_index("c").astype(jnp.int32).reshape(1)
    chip = (2 * lax.axis_index("x") + lax.axis_index("y")).astype(jnp.int32).reshape(1)
    late_names = BIG_NAMES[1:]

    win_g, cw_g = _all_gather([w_in.astype(BF16), conv_w])
    late_shards = [weights[k].astype(BF16) for k in late_names]
    gather = _exchange_start(late_shards, [jax.ShapeDtypeStruct((N_DEV,) + s.shape, BF16) for s in late_shards],
                             _gather_copies, len(FLIPS), "gather_late_start")
    w_in_full = win_g.transpose(1, 0, 2).reshape(D, -1)
    cuts = (0, 2 * D, 5 * D, 7 * D)
    W = dict(in_xg=w_in_full[:, cuts[0]:cuts[1]], in_qkv_t=w_in_full[:, cuts[1]:cuts[2]].T,
             in_kv=w_in_full[:, cuts[1] + D:cuts[2]], in_gates=w_in_full[:, cuts[2]:cuts[3]],
             in_f=jnp.pad(w_in_full[:, cuts[3]:], ((0, 0), (0, LANES - N_HEADS))))
    small = {k: weights[k] for k in SMALL_NAMES}
    small["conv_w"] = cw_g.transpose(1, 0, 2).reshape(CONV_W, D)
    small["norm_mix_g"] = norm_mix_g + gather[4][0, 0]

    def late_weights(after):
        shards, lands = _exchange_wait(gather, _gather_copies, after, "gather_late_wait")
        wa_g, wb_g, wo_g, wup_g, wdn_g = (
            lax.dynamic_update_slice_in_dim(land, shard[None], me, axis=0) for land, shard in zip(lands, shards))
        return dict(branch_a=wa_g.reshape(D, D), branch_b=wb_g.reshape(D, D), out=wo_g.reshape(D, D),
                    up=wup_g.transpose(1, 0, 2).reshape(D, D_FF), down=wdn_g.reshape(D_FF, D))

    def cores_start(names, grads_by_name, tag):
        blocks = [_to_dest_blocks(k, grads_by_name[k]) for k in names]
        started = _exchange_start(blocks, [jax.ShapeDtypeStruct((4,) + b.shape[1:], F32) for b in blocks],
                                  _cores_copies, 4, "cores_" + tag + "_start")
        return started, started[4][0, 0]

    def chips_start(started, after, tag):
        blocks, got = _exchange_wait(started, _cores_copies, after, "cores_" + tag + "_wait")
        sums = [_chip_partial_sum(b, g, core) for b, g in zip(blocks, got)]
        wire = [s[1] for s in sums]
        scatter = _exchange_start(wire, [jax.ShapeDtypeStruct((3,) + s.shape[1:], BF16) for s in wire],
                                  _scatter_copies, 3, "scatter_" + tag + "_start")
        return (sums, scatter), scatter[4][0, 0]

    hooks = dict(early_start=lambda g: cores_start(late_names, g, "early"),
                 early_mid=lambda st, after: chips_start(st, after, "early"),
                 in_start=lambda g: cores_start(BIG_NAMES[:1], dict(w_in=g), "w_in"),
                 in_mid=lambda st, after: chips_start(st, after, "w_in"))
    loss_part, grad_x, _, small_grads, ((early_sums, early_scatter), (in_sums, in_scatter)) = _local_step(
        x.reshape(S, D), loss_target.reshape(S, D), W, small, late_weights, hooks)
    loss = lax.psum(loss_part, MESH_AXES)
    _, early_others = _exchange_wait(early_scatter, _scatter_copies, grad_x, "scatter_early_wait")
    _, in_others = _exchange_wait(in_scatter, _scatter_copies, grad_x, "scatter_w_in_wait")
    sums = list(in_sums) + list(early_sums)
    others = list(in_others) + list(early_others)

    reduced = _all_reduce_small(_pack_small([small_grads[k] for k in SMALL_NAMES] + [small_grads["conv_w"]]))
    cw_full = reduced[reduced.shape[0] - _packed_rows((CONV_W, D)):].reshape(CONV_W, D)
    cw_cols = lax.dynamic_slice_in_dim(cw_full, me * (D // N_DEV), D // N_DEV, axis=1)

    grads, deltas, new_m, new_v = _adamw_small(reduced, cw_cols, weights, moms, vels)
    for k, s, g_got in zip(BIG_NAMES, sums, others):
        grads[k], deltas[k], new_m[k], new_v[k] = _adamw(weights[k], moms[k], vels[k], s[0], g_got, chip, "adamw_" + k)

    return (loss, grad_x.reshape(1, S, D), *[grads[k] for k in WEIGHT_ORDER], *[deltas[k] for k in WEIGHT_ORDER],
            *[new_m[k] for k in WEIGHT_ORDER], *[new_v[k] for k in WEIGHT_ORDER])
```
